```python
import math
import jax, jax.numpy as jnp
from jax import lax
import numpy as np

D_MODEL = 1024
BATCH = 4
SEQ = 4096
DEPTH = 1

N_MOD = 6
GLA_HEADS = 4
GLA_DK = D_MODEL // 2
GLA_DV = D_MODEL
GLA_HEAD_K = GLA_DK // GLA_HEADS
GLA_HEAD_V = GLA_DV // GLA_HEADS
GLA_LOWRANK = 16
GLA_TAU = 16.0
GLA_CHUNK = 64
DIL_PATTERNS = ((128, 1), (512, 4), (2048, 16))
DIL_GROUPS = len(DIL_PATTERNS)
DIL_HEADS_PER_GROUP = 8
DIL_HEAD_DIM = 64
DIL_HEADS = DIL_GROUPS * DIL_HEADS_PER_GROUP
DIL_WIDTH = DIL_HEADS * DIL_HEAD_DIM
DIL_OUT = DIL_HEADS_PER_GROUP * DIL_HEAD_DIM
DIL_BLOCK = 128
REL_BUCKETS = 32
REL_MAX_DIST = 2048
IN_SPLITS = (GLA_DK, GLA_DK, GLA_DV, GLA_DV, GLA_LOWRANK, DIL_WIDTH, DIL_WIDTH, DIL_WIDTH, D_MODEL, D_MODEL)
IN_COLS = sum(IN_SPLITS)
MOE_GROUPS = 4
MOE_EXPERTS = 8
MOE_TOPK = 2
MOE_FF = D_MODEL // 2
LN_EPS = 1e-5

kernel_name = "hybrid_gla_dilated_attn_hmoe_deepnorm"


def layer_norm(x, g, b):
    xf = x.astype(jnp.float32)
    mu = jnp.mean(xf, -1, keepdims=True)
    var = jnp.mean(jnp.square(xf - mu), -1, keepdims=True)
    return ((xf - mu) * lax.rsqrt(var + LN_EPS) * g + b).astype(x.dtype)


def t5_bucket(dist):
    exact = REL_BUCKETS // 2
    d = jnp.maximum(dist, 1).astype(jnp.float32)
    large = exact + (jnp.log(d / exact) / math.log(REL_MAX_DIST / exact) * (REL_BUCKETS - exact)).astype(jnp.int32)
    large = jnp.minimum(large, REL_BUCKETS - 1)
    return jnp.where(dist < exact, dist, large)


def gla(q, k, v, r, lr, w_gate, b_gate, norm_g):
    B, S, _ = q.shape
    n = S // GLA_CHUNK
    log_a = jax.nn.log_sigmoid((lr @ w_gate + b_gate).astype(jnp.float32)) / GLA_TAU

    def chunks(t, hd):
        return t.reshape(B, n, GLA_CHUNK, GLA_HEADS, hd).transpose(1, 0, 3, 2, 4)

    qc = chunks(q * GLA_HEAD_K ** -0.5, GLA_HEAD_K)
    kc = chunks(k, GLA_HEAD_K)
    vc = chunks(v, GLA_HEAD_V)
    gc = chunks(log_a, GLA_HEAD_K)
    causal = jnp.tril(jnp.ones((GLA_CHUNK, GLA_CHUNK), bool))

    def step(state, inp):
        qi, ki, vi, gi = inp
        b = jnp.cumsum(gi, axis=-2)
        o_inter = jnp.einsum('bhck,bhkv->bhcv', qi * jnp.exp(b), state)
        diff = jnp.where(causal[:, :, None], b[:, :, :, None, :] - b[:, :, None, :, :], -jnp.inf)
        att = jnp.einsum('bhtk,bhtsk,bhsk->bhts', qi, jnp.exp(diff), ki)
        o_intra = jnp.einsum('bhts,bhsv->bhtv', att, vi)
        b_last = b[:, :, -1:, :]
        state = state * jnp.exp(b_last)[:, :, 0, :, None] + jnp.einsum('bhsk,bhsv->bhkv', ki * jnp.exp(b_last - b), vi)
        return state, o_inter + o_intra

    state0 = jnp.zeros((B, GLA_HEADS, GLA_HEAD_K, GLA_HEAD_V), jnp.float32)
    _, o = lax.scan(step, state0, (qc, kc, vc, gc))
    o = o.transpose(1, 0, 3, 2, 4).reshape(B, S, GLA_HEADS, GLA_HEAD_V)
    o = o * lax.rsqrt(jnp.mean(o * o, -1, keepdims=True) + LN_EPS) * norm_g.reshape(GLA_HEADS, GLA_HEAD_V)
    o = o * jax.nn.silu(r.astype(jnp.float32)).reshape(B, S, GLA_HEADS, GLA_HEAD_V)
    return o.reshape(B, S, GLA_DV).astype(q.dtype)


def dilated_group(q, k, v, bias_table, window, dilation):
    B, S, H, Dh = q.shape
    L = S // dilation
    nb = -(-L // DIL_BLOCK)
    Lp = nb * DIL_BLOCK

    def to_blocks(t):
        t = t.reshape(B, L, dilation, H, Dh).transpose(0, 2, 3, 1, 4)
        t = jnp.pad(t, ((0, 0), (0, 0), (0, 0), (0, Lp - L), (0, 0)))
        return t.reshape(B, dilation, H, nb, DIL_BLOCK, Dh)

    def with_prev(t):
        prev = jnp.pad(t, ((0, 0), (0, 0), (0, 0), (1, 0), (0, 0), (0, 0)))[:, :, :, :-1]
        return jnp.concatenate([prev, t], axis=-2)

    qb = to_blocks(q)
    kk = with_prev(to_blocks(k))
    vv = with_prev(to_blocks(v))
    s = jnp.einsum('brhnqe,brhnke->brhnqk', qb, kk).astype(jnp.float32) * Dh ** -0.5
    qi = jnp.arange(DIL_BLOCK)[:, None]
    kj = jnp.arange(2 * DIL_BLOCK)[None, :]
    m = qi + DIL_BLOCK - kj
    n_steps = window // dilation
    band = (m >= 0) & (m <= n_steps)
    key_pos = jnp.arange(nb)[:, None, None] * DIL_BLOCK - DIL_BLOCK + kj[None]
    valid = band[None] & (key_pos >= 0)
    bucket = t5_bucket(jnp.clip(m, 0, n_steps) * dilation)
    bias = jnp.transpose(bias_table[bucket], (2, 0, 1)).astype(jnp.float32)
    s = jnp.where(valid, s + bias[:, None], -jnp.inf)
    mx = jnp.max(s, axis=-1, keepdims=True)
    p = jnp.exp(s - mx)
    den = jnp.sum(p, axis=-1, keepdims=True)
    o = jnp.einsum('brhnqk,brhnke->brhnqe', p / den, vv)
    lse = (mx + jnp.log(den))[..., 0]
    o = o.reshape(B, dilation, H, Lp, Dh)[:, :, :, :L].transpose(0, 3, 1, 2, 4).reshape(B, S, H, Dh)
    lse = lse.reshape(B, dilation, H, Lp)[..., :L].transpose(0, 3, 1, 2).reshape(B, S, H)
    return o, lse


def dilated_attention(q, k, v, rel_bias):
    B, S = q.shape[:2]
    outs, lses = [], []
    for g, (window, dilation) in enumerate(DIL_PATTERNS):
        table = rel_bias[:, g * DIL_HEADS_PER_GROUP:(g + 1) * DIL_HEADS_PER_GROUP]
        o, lse = dilated_group(q[:, :, g], k[:, :, g], v[:, :, g], table, window, dilation)
        outs.append(o)
        lses.append(lse)
    wts = jax.nn.softmax(jnp.stack(lses), axis=0)
    out = jnp.einsum('gbsh,gbshe->bshe', wts, jnp.stack(outs))
    return out.reshape(B, S, DIL_OUT).astype(q.dtype)


def mixer_sublayer(u, w_in, w_gla_gate, b_gla_gate, gla_norm, w_proj_gla, w_proj_attn, w_out, rel_bias):
    B, S, _ = u.shape
    z = u @ w_in
    cuts = np.cumsum(IN_SPLITS)[:-1].tolist()
    qa, ka, va, ra, lra, qd, kd, vd, ga, gd = jnp.split(z, cuts, axis=-1)
    y_gla = gla(qa, ka, va, ra, lra, w_gla_gate, b_gla_gate, gla_norm)
    hs = (B, S, DIL_GROUPS, DIL_HEADS_PER_GROUP, DIL_HEAD_DIM)
    y_att = dilated_attention(qd.reshape(hs), kd.reshape(hs), vd.reshape(hs), rel_bias)
    merged = jax.nn.sigmoid(ga) * (y_gla @ w_proj_gla) + jax.nn.sigmoid(gd) * (y_att @ w_proj_attn)
    return merged @ w_out


def hier_moe(u, w_rg, b_rg, w_re, b_re, w_gate, w_up, w_down):
    B, S, D = u.shape
    xf = u.reshape(-1, D)
    N = xf.shape[0]
    g_logits = (xf @ w_rg + b_rg).astype(jnp.float32)
    g_prob = jax.nn.softmax(g_logits, -1)
    g_val, g_idx = lax.top_k(g_prob, 1)
    e_logits = (xf @ w_re + b_re).astype(jnp.float32).reshape(N, MOE_GROUPS, MOE_EXPERTS)
    e_in = jnp.take_along_axis(e_logits, g_idx[:, :, None], axis=1)[:, 0]
    top_v, top_i = lax.top_k(e_in, MOE_TOPK)
    top_w = jax.nn.softmax(top_v, -1)
    e_w = jnp.sum(jax.nn.one_hot(top_i, MOE_EXPERTS, dtype=jnp.float32) * top_w[..., None], axis=1)
    combine = jax.nn.one_hot(g_idx[:, 0], MOE_GROUPS, dtype=jnp.float32)[:, :, None] * (g_val * e_w)[:, None, :]
    out = jnp.zeros((N, D), jnp.float32)
    for g in range(MOE_GROUPS):
        h = jax.nn.silu(jnp.einsum('nd,edf->nef', xf, w_gate[g])) * jnp.einsum('nd,edf->nef', xf, w_up[g])
        out = out + jnp.einsum('nef,ne,efd->nd', h, combine[:, g], w_down[g])
    return out.reshape(B, S, D).astype(u.dtype)


def setup_inputs(seed: int = 0) -> dict:
    key = jax.random.key(seed)
    ks = jax.random.split(key, 24)
    beta = (8.0 * DEPTH) ** -0.25
    f32 = jnp.float32

    def nrm(k, shape, scale):
        return jax.random.normal(k, shape, f32) * scale

    starts = np.concatenate([[0], np.cumsum(IN_SPLITS)])
    col_scale = np.ones(IN_COLS, np.float32)
    col_scale[starts[2]:starts[3]] = beta
    col_scale[starts[7]:starts[8]] = beta
    Ld = DEPTH
    return {
        "x": nrm(ks[0], (BATCH, SEQ, D_MODEL), 1.0),
        "c": nrm(ks[1], (BATCH, D_MODEL), 1.0),
        "rel_bias": nrm(ks[2], (REL_BUCKETS, DIL_HEADS), 0.1),
        "w_ada": nrm(ks[3], (Ld, D_MODEL, N_MOD * D_MODEL), D_MODEL ** -0.5),
        "b_ada": nrm(ks[4], (Ld, N_MOD * D_MODEL), 0.02),
        "w_in": nrm(ks[5], (Ld, D_MODEL, IN_COLS), D_MODEL ** -0.5) * jnp.asarray(col_scale),
        "w_gla_gate": nrm(ks[6], (Ld, GLA_LOWRANK, GLA_DK), GLA_LOWRANK ** -0.5),
        "b_gla_gate": nrm(ks[7], (Ld, GLA_DK), 0.1),
        "gla_norm": 1.0 + nrm(ks[8], (Ld, GLA_DV), 0.02),
        "w_proj_gla": nrm(ks[9], (Ld, GLA_DV, D_MODEL), GLA_DV ** -0.5 * beta),
        "w_proj_attn": nrm(ks[10], (Ld, DIL_OUT, D_MODEL), DIL_OUT ** -0.5 * beta),
        "w_out": nrm(ks[11], (Ld, D_MODEL, D_MODEL), D_MODEL ** -0.5 * beta),
        "ln1_g": 1.0 + nrm(ks[12], (Ld, D_MODEL), 0.02),
        "ln1_b": nrm(ks[13], (Ld, D_MODEL), 0.02),
        "w_router_group": nrm(ks[14], (Ld, D_MODEL, MOE_GROUPS), D_MODEL ** -0.5),
        "b_router_group": nrm(ks[15], (Ld, MOE_GROUPS), 0.01),
        "w_router_expert": nrm(ks[16], (Ld, D_MODEL, MOE_GROUPS * MOE_EXPERTS), D_MODEL ** -0.5),
        "b_router_expert": nrm(ks[17], (Ld, MOE_GROUPS * MOE_EXPERTS), 0.01),
        "w_exp_gate": nrm(ks[18], (Ld, MOE_GROUPS, MOE_EXPERTS, D_MODEL, MOE_FF), D_MODEL ** -0.5),
        "w_exp_up": nrm(ks[19], (Ld, MOE_GROUPS, MOE_EXPERTS, D_MODEL, MOE_FF), D_MODEL ** -0.5 * beta),
        "w_exp_down": nrm(ks[20], (Ld, MOE_GROUPS, MOE_EXPERTS, MOE_FF, D_MODEL), MOE_FF ** -0.5 * beta),
        "ln2_g": 1.0 + nrm(ks[21], (Ld, D_MODEL), 0.02),
        "ln2_b": nrm(ks[22], (Ld, D_MODEL), 0.02),
    }


def reference(x, c, rel_bias, w_ada, b_ada, w_in, w_gla_gate, b_gla_gate, gla_norm, w_proj_gla, w_proj_attn,
              w_out, ln1_g, ln1_b, w_router_group, b_router_group, w_router_expert, b_router_expert,
              w_exp_gate, w_exp_up, w_exp_down, ln2_g, ln2_b):
    alpha = (2.0 * DEPTH) ** 0.25
    for l in range(DEPTH):
        mods = jax.nn.silu(c) @ w_ada[l] + b_ada[l]
        sh1, sc1, g1, sh2, sc2, g2 = [m[:, None, :] for m in jnp.split(mods, N_MOD, axis=-1)]
        u = x * (1.0 + sc1) + sh1
        y = mixer_sublayer(u, w_in[l], w_gla_gate[l], b_gla_gate[l], gla_norm[l], w_proj_gla[l],
                           w_proj_attn[l], w_out[l], rel_bias)
        x = layer_norm(alpha * x + g1 * y, ln1_g[l], ln1_b[l])
        u = x * (1.0 + sc2) + sh2
        y = hier_moe(u, w_router_group[l], b_router_group[l], w_router_expert[l], b_router_expert[l],
                     w_exp_gate[l], w_exp_up[l], w_exp_down[l])
        x = layer_norm(alpha * x + g2 * y, ln2_g[l], ln2_b[l])
    return x
```

```python
import functools
import math

import numpy as np
import jax
import jax.numpy as jnp
from jax import lax
from jax.experimental import pallas as pl
from jax.experimental.pallas import tpu as pltpu

F32 = jnp.float32
BF16 = jnp.bfloat16

N_MOD = 6
GLA_HEADS = 4
GLA_LOWRANK = 16
GLA_TAU = 16.0
GLA_CHUNK = 64
DIL_PATTERNS = ((128, 1), (512, 4), (2048, 16))
DIL_GROUPS = len(DIL_PATTERNS)
DIL_HEADS_PER_GROUP = 8
DIL_HEAD_DIM = 64
DIL_GROUP_WIDTH = DIL_HEADS_PER_GROUP * DIL_HEAD_DIM
DIL_BLOCK = 128
REL_BUCKETS = 32
REL_MAX_DIST = 2048
MOE_GROUPS = 4
MOE_EXPERTS = 8
MOE_TOTAL = MOE_GROUPS * MOE_EXPERTS
LN_EPS = 1e-5
DEPTH = 1

LANES = 128
VMEM_LIMIT = 56 * 1024 * 1024
NEG = -1e30
ROW_TILE = 512
EXPERT_TILE = 256

HIGHEST = lax.Precision.HIGHEST


def _cparams(sem):
    return pltpu.CompilerParams(dimension_semantics=sem, vmem_limit_bytes=VMEM_LIMIT)


def _sigmoid(x):
    return 1.0 / (1.0 + jnp.exp(-x))


def _silu(x):
    return x * _sigmoid(x)


def _layer_norm(x, g, b):
    mu = jnp.mean(x, axis=-1, keepdims=True)
    xc = x - mu
    var = jnp.mean(xc * xc, axis=-1, keepdims=True)
    return xc * lax.rsqrt(var + LN_EPS) * g + b


def _mods_kernel(c_ref, w_ref, b_ref, o_ref):
    a = _silu(c_ref[...])
    o_ref[...] = jnp.dot(a, w_ref[...], precision=HIGHEST, preferred_element_type=F32) + b_ref[...]


def _ada_mods(c, w, b):
    bsz, d = c.shape
    n = w.shape[1]
    tn = 1536
    assert n % tn == 0
    return pl.pallas_call(
        _mods_kernel,
        out_shape=jax.ShapeDtypeStruct((bsz, n), F32),
        grid=(n // tn,),
        in_specs=[pl.BlockSpec((bsz, d), lambda j: (0, 0)),
                  pl.BlockSpec((d, tn), lambda j: (0, j)),
                  pl.BlockSpec((1, tn), lambda j: (0, j))],
        out_specs=pl.BlockSpec((bsz, tn), lambda j: (0, j)),
        compiler_params=_cparams(("arbitrary",)),
        name="ada_mods",
    )(c, w, b.reshape(1, n))


def _proj_pieces(d_model):
    dk = d_model // 2
    return (
        ("q_gla", dk, "scale_q_gla"), ("k_gla", dk, None), ("v_gla", d_model, None), ("r_gla", d_model, "silu"),
        ("q_att", DIL_GROUPS * DIL_GROUP_WIDTH, "scale_q_att"), ("k_att", DIL_GROUPS * DIL_GROUP_WIDTH, None),
        ("v_att", DIL_GROUPS * DIL_GROUP_WIDTH, None),
        ("g_gla", d_model, "sigmoid"), ("g_att", d_model, "sigmoid"),
        ("lr", LANES, "lowrank"),
    )


def _proj_kernel(pieces, head_k, x_ref, sc_ref, sh_ref, w_ref, *out_refs):
    u = (x_ref[0] * (1.0 + sc_ref[0]) + sh_ref[0]).astype(BF16)
    off = 0
    for (name, width, post), o_ref in zip(pieces, out_refs):
        chunk = min(width, 512)
        for c0 in range(0, width, chunk):
            acc = jnp.dot(u, w_ref[:, off + c0:off + c0 + chunk], preferred_element_type=F32)
            if post == "silu":
                acc = _silu(acc)
            elif post == "sigmoid":
                acc = _sigmoid(acc)
            elif post == "scale_q_gla":
                acc = acc * (head_k ** -0.5)
            elif post == "scale_q_att":
                acc = acc * (DIL_HEAD_DIM ** -0.5)
            if post == "lowrank":
                o_ref[0] = acc[:, :GLA_LOWRANK]
            else:
                o_ref[0, :, c0:c0 + chunk] = acc.astype(o_ref.dtype)
        off += width


def _in_projection(x, sc1, sh1, w_perm):
    bsz, s, d = x.shape
    pieces = _proj_pieces(d)
    assert sum(p[1] for p in pieces) == w_perm.shape[1]
    tm = min(ROW_TILE, s)
    assert s % tm == 0
    head_k = (d // 2) // GLA_HEADS
    out_shape, out_specs = [], []
    for name, width, post in pieces:
        if post == "lowrank":
            out_shape.append(jax.ShapeDtypeStruct((bsz, s, GLA_LOWRANK), F32))
            out_specs.append(pl.BlockSpec((1, tm, GLA_LOWRANK), lambda b, i: (b, i, 0)))
        else:
            out_shape.append(jax.ShapeDtypeStruct((bsz, s, width), BF16))
            out_specs.append(pl.BlockSpec((1, tm, width), lambda b, i: (b, i, 0)))
    outs = pl.pallas_call(
        functools.partial(_proj_kernel, pieces, head_k),
        out_shape=out_shape,
        grid=(bsz, s // tm),
        in_specs=[pl.BlockSpec((1, tm, d), lambda b, i: (b, i, 0)),
                  pl.BlockSpec((1, 1, d), lambda b, i: (b, 0, 0)),
                  pl.BlockSpec((1, 1, d), lambda b, i: (b, 0, 0)),
                  pl.BlockSpec(w_perm.shape, lambda b, i: (0, 0), pipeline_mode=pl.Buffered(1))],
        out_specs=out_specs,
        compiler_params=_cparams(("parallel", "arbitrary")),
        name="in_projection",
    )(x, sc1, sh1, w_perm)
    return dict(zip([p[0] for p in pieces], outs))


def _gla_kernel(n_chunks, head_k, head_v, q_ref, k_ref, v_ref, r_ref, lr_ref, wg_ref, bg_ref, ng_ref, o_ref,
                state_ref):
    @pl.when(pl.program_id(1) == 0)
    def _():
        state_ref[...] = jnp.zeros_like(state_ref)

    c = GLA_CHUNK
    row = lax.broadcasted_iota(jnp.int32, (c, c), 0)
    col = lax.broadcasted_iota(jnp.int32, (c, c), 1)
    causal = row >= col
    tril = causal.astype(F32)
    ones_cv = jnp.ones((c, head_v), F32)
    mid = c // 2 - 1
    for ci in range(n_chunks):
        rows = slice(ci * c, (ci + 1) * c)
        gate_in = jnp.dot(lr_ref[0, rows, :], wg_ref[...], precision=HIGHEST, preferred_element_type=F32) + bg_ref[...]
        g = (jnp.minimum(gate_in, 0.0) - jnp.log(1.0 + jnp.exp(-jnp.abs(gate_in)))) * (1.0 / GLA_TAU)
        bc = jnp.dot(tril, g, precision=HIGHEST, preferred_element_type=F32)
        b_mid = bc[mid:mid + 1, :]
        b_last = bc[c - 1:c, :]
        qf = q_ref[0, rows, :].astype(F32)
        kf = k_ref[0, rows, :].astype(F32)
        q_in = (qf * jnp.exp(bc - b_mid)).astype(BF16)
        k_in = (kf * jnp.exp(b_mid - bc)).astype(BF16)
        q_st = (qf * jnp.exp(bc)).astype(BF16)
        k_st = (kf * jnp.exp(b_last - bc)).astype(BF16)
        for h in range(GLA_HEADS):
            ks = slice(h * head_k, (h + 1) * head_k)
            vs = slice(h * head_v, (h + 1) * head_v)
            vh = v_ref[0, rows, vs]
            att = lax.dot_general(q_in[:, ks], k_in[:, ks], (((1,), (1,)), ((), ())), preferred_element_type=F32)
            att = jnp.where(causal, att, 0.0).astype(BF16)
            st = state_ref[h]
            o = jnp.dot(att, vh, preferred_element_type=F32)
            o = o + jnp.dot(q_st[:, ks], st.astype(BF16), preferred_element_type=F32)
            kv = lax.dot_general(k_st[:, ks], vh, (((0,), (0,)), ((), ())), preferred_element_type=F32)
            dec = jnp.exp(lax.dot_general(g[:, ks], ones_cv, (((0,), (0,)), ((), ())), precision=HIGHEST,
                                          preferred_element_type=F32))
            state_ref[h] = st * dec + kv
            ms = jnp.mean(o * o, axis=-1, keepdims=True)
            o = o * lax.rsqrt(ms + LN_EPS) * ng_ref[:, vs] * r_ref[0, rows, vs].astype(F32)
            o_ref[0, rows, vs] = o.astype(o_ref.dtype)


def _gla(q, k, v, r_silu, lr, w_gate, b_gate, norm_g, chunks_per_step=2):
    bsz, s, dk = q.shape
    dv = v.shape[-1]
    head_k, head_v = dk // GLA_HEADS, dv // GLA_HEADS
    ct = GLA_CHUNK * chunks_per_step
    assert s % ct == 0
    row_spec = lambda w: pl.BlockSpec((1, ct, w), lambda b, i: (b, i, 0))
    full = lambda a: pl.BlockSpec(a.shape, lambda b, i: (0,) * a.ndim)
    bg = b_gate.reshape(1, dk)
    ng = norm_g.reshape(1, dv)
    return pl.pallas_call(
        functools.partial(_gla_kernel, chunks_per_step, head_k, head_v),
        out_shape=jax.ShapeDtypeStruct((bsz, s, dv), BF16),
        grid=(bsz, s // ct),
        in_specs=[row_spec(dk), row_spec(dk), row_spec(dv), row_spec(dv), row_spec(GLA_LOWRANK),
                  full(w_gate), full(bg), full(ng)],
        out_specs=row_spec(dv),
        scratch_shapes=[pltpu.VMEM((GLA_HEADS, head_k, head_v), F32)],
        compiler_params=_cparams(("parallel", "arbitrary")),
        name="gla",
    )(q, k, v, r_silu, lr, w_gate, bg, ng)


def _t5_bucket_np(dist):
    exact = REL_BUCKETS // 2
    d = np.maximum(dist, 1).astype(np.float32)
    large = exact + (np.log(d / np.float32(exact)) / np.float32(math.log(REL_MAX_DIST / exact))
                     * np.float32(REL_BUCKETS - exact)).astype(np.int32)
    large = np.minimum(large, REL_BUCKETS - 1)
    return np.where(dist < exact, dist, large).astype(np.int32)


def _band_tables(window, dilation):
    qi = np.arange(DIL_BLOCK)[:, None]
    kj = np.arange(2 * DIL_BLOCK)[None, :]
    m = qi + DIL_BLOCK - kj
    n_steps = window // dilation
    band = (m >= 0) & (m <= n_steps)
    bucket = _t5_bucket_np(np.clip(m, 0, n_steps) * dilation)
    return np.where(band, bucket, -1).astype(np.int32)


def _attn_kernel(table_ref, bucket_ref, q_ref, kp_ref, kc_ref, vp_ref, vc_ref, o_ref, lse_ref, bias_ref):
    i = pl.program_id(1)
    blk = DIL_BLOCK

    @pl.when((pl.program_id(0) == 0) & (i == 0))
    def _():
        bucket = bucket_ref[...]
        for h in range(DIL_HEADS_PER_GROUP):
            acc = jnp.full(bucket.shape, NEG, F32)
            for bkt in range(REL_BUCKETS):
                acc = jnp.where(bucket == bkt, table_ref[bkt, h], acc)
            bias_ref[h] = acc

    prev_pen = jnp.where(i > 0, 0.0, NEG).astype(F32)
    for h in range(DIL_HEADS_PER_GROUP):
        hs = slice(h * DIL_HEAD_DIM, (h + 1) * DIL_HEAD_DIM)
        qh = q_ref[0, :, hs]
        nt = (((1,), (1,)), ((), ()))
        s_p = lax.dot_general(qh, kp_ref[0, :, hs], nt, preferred_element_type=F32) + (bias_ref[h, :, :blk] + prev_pen)
        s_c = lax.dot_general(qh, kc_ref[0, :, hs], nt, preferred_element_type=F32) + bias_ref[h, :, blk:]
        mx = jnp.max(jnp.maximum(s_p, s_c), axis=-1, keepdims=True)
        p_p = jnp.exp(s_p - mx)
        p_c = jnp.exp(s_c - mx)
        den = jnp.sum(p_p + p_c, axis=-1, keepdims=True)
        o = jnp.dot(p_p.astype(BF16), vp_ref[0, :, hs], preferred_element_type=F32)
        o = o + jnp.dot(p_c.astype(BF16), vc_ref[0, :, hs], preferred_element_type=F32)
        o_ref[0, :, hs] = (o / den).astype(o_ref.dtype)
        lse_ref[0, :, h:h + 1] = mx + jnp.log(den)


def _dilated_group_attention(q, k, v, table, window, dilation):
    bb, l, w = q.shape
    assert l % DIL_BLOCK == 0
    nb = l // DIL_BLOCK
    bucket = jnp.asarray(_band_tables(window, dilation))
    cur = pl.BlockSpec((1, DIL_BLOCK, w), lambda b, i: (b, i, 0))
    prev = pl.BlockSpec((1, DIL_BLOCK, w), lambda b, i: (b, jnp.maximum(i - 1, 0), 0))
    return pl.pallas_call(
        _attn_kernel,
        out_shape=[jax.ShapeDtypeStruct((bb, l, w), BF16),
                   jax.ShapeDtypeStruct((bb, l, DIL_HEADS_PER_GROUP), F32)],
        grid=(bb, nb),
        in_specs=[pl.BlockSpec(memory_space=pltpu.SMEM),
                  pl.BlockSpec(bucket.shape, lambda b, i: (0, 0)),
                  cur, prev, cur, prev, cur],
        out_specs=[cur, pl.BlockSpec((1, DIL_BLOCK, DIL_HEADS_PER_GROUP), lambda b, i: (b, i, 0))],
        scratch_shapes=[pltpu.VMEM((DIL_HEADS_PER_GROUP, DIL_BLOCK, 2 * DIL_BLOCK), F32)],
        compiler_params=_cparams(("arbitrary", "arbitrary")),
        name=f"dilated_attn_d{dilation}",
    )(table, bucket, q, k, k, v, v)


def _to_strided(t, dilation):
    bsz, s, w = t.shape
    if dilation == 1:
        return t
    return t.reshape(bsz, s // dilation, dilation, w).transpose(0, 2, 1, 3).reshape(bsz * dilation, s // dilation, w)


def _from_strided(t, dilation, bsz):
    if dilation == 1:
        return t
    bb, l, w = t.shape
    return t.reshape(bsz, dilation, l, w).transpose(0, 2, 1, 3).reshape(bsz, l * dilation, w)


def _merge_kernel(alpha, ygla_ref, o0_ref, o1_ref, o2_ref, l0_ref, l1_ref, l2_ref, gg_ref, ga_ref, x_ref,
                  g1_ref, sc2_ref, sh2_ref, ln_g_ref, ln_b_ref, wpg_ref, wpa_ref, wout_ref, wr_ref, br_ref,
                  x1_ref, u2_ref, eid_ref, ew_ref):
    hpg = DIL_HEADS_PER_GROUP
    l0, l1, l2 = l0_ref[0], l1_ref[0], l2_ref[0]
    lm = jnp.maximum(jnp.maximum(l0, l1), l2)
    e0, e1, e2 = jnp.exp(l0 - lm), jnp.exp(l1 - lm), jnp.exp(l2 - lm)
    inv = 1.0 / (e0 + e1 + e2)
    hrow = lax.broadcasted_iota(jnp.int32, (hpg, DIL_GROUP_WIDTH), 0)
    hcol = lax.broadcasted_iota(jnp.int32, (hpg, DIL_GROUP_WIDTH), 1) // DIL_HEAD_DIM
    expand = (hrow == hcol).astype(F32)
    y_att = jnp.zeros(o0_ref.shape[1:], F32)
    for e, o_ref in ((e0, o0_ref), (e1, o1_ref), (e2, o2_ref)):
        wfull = jnp.dot(e * inv, expand, precision=HIGHEST, preferred_element_type=F32)
        y_att = y_att + wfull * o_ref[0].astype(F32)
    p_gla = jnp.dot(ygla_ref[0], wpg_ref[...], preferred_element_type=F32)
    p_att = jnp.dot(y_att.astype(BF16), wpa_ref[...], preferred_element_type=F32)
    merged = gg_ref[0].astype(F32) * p_gla + ga_ref[0].astype(F32) * p_att
    y = jnp.dot(merged.astype(BF16), wout_ref[...], preferred_element_type=F32)
    x1 = _layer_norm(alpha * x_ref[0] + g1_ref[0] * y, ln_g_ref[...], ln_b_ref[...])
    x1_ref[0] = x1
    u2 = x1 * (1.0 + sc2_ref[0]) + sh2_ref[0]
    u2_ref[0] = u2.astype(BF16)

    u_hi = u2.astype(BF16)
    u_lo = (u2 - u_hi.astype(F32)).astype(BF16)
    wr = wr_ref[...]
    w_hi = wr.astype(BF16)
    w_lo = (wr - w_hi.astype(F32)).astype(BF16)
    logits = (jnp.dot(u_hi, w_hi, preferred_element_type=F32) + jnp.dot(u_lo, w_hi, preferred_element_type=F32)
              + jnp.dot(u_hi, w_lo, preferred_element_type=F32)) + br_ref[...]
    lane = lax.broadcasted_iota(jnp.int32, logits.shape, 1)
    big = jnp.int32(LANES)
    lg = jnp.where(lane < MOE_GROUPS, logits, NEG)
    gmax = jnp.max(lg, axis=-1, keepdims=True)
    gidx = jnp.min(jnp.where(lg == gmax, lane, big), axis=-1, keepdims=True)
    gval = 1.0 / jnp.sum(jnp.exp(lg - gmax), axis=-1, keepdims=True)
    in_group = (lane >= MOE_GROUPS + gidx * MOE_EXPERTS) & (lane < MOE_GROUPS + (gidx + 1) * MOE_EXPERTS)
    le = jnp.where(in_group, logits, NEG)
    m1 = jnp.max(le, axis=-1, keepdims=True)
    i1 = jnp.min(jnp.where(le == m1, lane, big), axis=-1, keepdims=True)
    le2 = jnp.where(lane == i1, NEG, le)
    m2 = jnp.max(le2, axis=-1, keepdims=True)
    i2 = jnp.min(jnp.where(le2 == m2, lane, big), axis=-1, keepdims=True)
    t = jnp.exp(m2 - m1)
    w1 = 1.0 / (1.0 + t)
    w2 = t * w1
    eid_ref[0] = jnp.where(lane == 0, i1 - MOE_GROUPS, jnp.where(lane == 1, i2 - MOE_GROUPS, 0))
    ew_ref[0] = jnp.where(lane == 0, gval * w1, jnp.where(lane == 1, gval * w2, 0.0))


def _merge(alpha, y_gla, o_groups, lse_groups, g_gla, g_att, x, g1, sc2, sh2, ln_g, ln_b, wpg, wpa, wout, wr, br):
    bsz, s, d = x.shape
    tm = min(ROW_TILE, s)
    assert s % tm == 0
    row = lambda w: pl.BlockSpec((1, tm, w), lambda b, i: (b, i, 0))
    per_b = pl.BlockSpec((1, 1, d), lambda b, i: (b, 0, 0))
    full = lambda a: pl.BlockSpec(a.shape, lambda b, i: (0,) * a.ndim)
    ln_g2, ln_b2 = ln_g.reshape(1, d), ln_b.reshape(1, d)
    return pl.pallas_call(
        functools.partial(_merge_kernel, alpha),
        out_shape=[jax.ShapeDtypeStruct((bsz, s, d), F32), jax.ShapeDtypeStruct((bsz, s, d), BF16),
                   jax.ShapeDtypeStruct((bsz, s, LANES), jnp.int32), jax.ShapeDtypeStruct((bsz, s, LANES), F32)],
        grid=(bsz, s // tm),
        in_specs=[row(y_gla.shape[-1])] + [row(DIL_GROUP_WIDTH)] * 3 + [row(DIL_HEADS_PER_GROUP)] * 3
                 + [row(d), row(d), row(d), per_b, per_b, per_b, full(ln_g2), full(ln_b2),
                    full(wpg), full(wpa), full(wout), full(wr), full(br)],
        out_specs=[row(d), row(d), row(LANES), row(LANES)],
        compiler_params=_cparams(("parallel", "arbitrary")),
        name="merge_ln1_router",
    )(y_gla, *o_groups, *lse_groups, g_gla, g_att, x, g1, sc2, sh2, ln_g2, ln_b2, wpg, wpa, wout, wr, br)


def _expert_kernel(te_ref, nt_ref, x_ref, wg_ref, wu_ref, wd_ref, o_ref, wg_s, wu_s, wd_s):
    t = pl.program_id(0)
    prev_e = te_ref[jnp.maximum(t - 1, 0)]

    @pl.when((t < nt_ref[0]) & ((t == 0) | (te_ref[t] != prev_e)))
    def _():
        wg_s[...] = wg_ref[0].astype(BF16)
        wu_s[...] = wu_ref[0].astype(BF16)
        wd_s[...] = wd_ref[0].astype(BF16)

    @pl.when(t < nt_ref[0])
    def _():
        xt = x_ref[...]
        hg = jnp.dot(xt, wg_s[...], preferred_element_type=F32)
        hu = jnp.dot(xt, wu_s[...], preferred_element_type=F32)
        h = (_silu(hg) * hu).astype(BF16)
        o_ref[...] = jnp.dot(h, wd_s[...], preferred_element_type=F32).astype(o_ref.dtype)

    @pl.when(t >= nt_ref[0])
    def _():
        o_ref[...] = jnp.zeros_like(o_ref)


def _expert_ffn(tile_expert, n_tiles_used, xg, w_gate, w_up, w_down):
    p, d = xg.shape
    ne, _, ff = w_gate.shape
    tm = EXPERT_TILE
    n_tiles = p // tm
    grid_spec = pltpu.PrefetchScalarGridSpec(
        num_scalar_prefetch=2,
        grid=(n_tiles,),
        in_specs=[pl.BlockSpec((tm, d), lambda t, te, nt: (t, 0)),
                  pl.BlockSpec((1, d, ff), lambda t, te, nt: (te[t], 0, 0)),
                  pl.BlockSpec((1, d, ff), lambda t, te, nt: (te[t], 0, 0)),
                  pl.BlockSpec((1, ff, d), lambda t, te, nt: (te[t], 0, 0))],
        out_specs=pl.BlockSpec((tm, d), lambda t, te, nt: (t, 0)),
        scratch_shapes=[pltpu.VMEM((d, ff), BF16), pltpu.VMEM((d, ff), BF16), pltpu.VMEM((ff, d), BF16)],
    )
    return pl.pallas_call(
        _expert_kernel,
        out_shape=jax.ShapeDtypeStruct((p, d), BF16),
        grid_spec=grid_spec,
        compiler_params=_cparams(("arbitrary",)),
        name="expert_ffn",
    )(tile_expert, n_tiles_used, xg, w_gate, w_up, w_down)


def _final_kernel(alpha, x1_ref, ya_ref, yb_ref, ew_ref, g2_ref, ln_g_ref, ln_b_ref, o_ref):
    ew = ew_ref[0]
    y = ew[:, 0:1] * ya_ref[0].astype(F32) + ew[:, 1:2] * yb_ref[0].astype(F32)
    o_ref[0] = _layer_norm(alpha * x1_ref[0] + g2_ref[0] * y, ln_g_ref[...], ln_b_ref[...])


def _final(alpha, x1, ya, yb, ew, g2, ln_g, ln_b):
    bsz, s, d = x1.shape
    tm = min(ROW_TILE, s)
    row = lambda w: pl.BlockSpec((1, tm, w), lambda b, i: (b, i, 0))
    full = lambda a: pl.BlockSpec(a.shape, lambda b, i: (0,) * a.ndim)
    ln_g2, ln_b2 = ln_g.reshape(1, d), ln_b.reshape(1, d)
    return pl.pallas_call(
        functools.partial(_final_kernel, alpha),
        out_shape=jax.ShapeDtypeStruct((bsz, s, d), F32),
        grid=(bsz, s // tm),
        in_specs=[row(d), row(d), row(d), row(LANES), pl.BlockSpec((1, 1, d), lambda b, i: (b, 0, 0)),
                  full(ln_g2), full(ln_b2)],
        out_specs=row(d),
        compiler_params=_cparams(("parallel", "arbitrary")),
        name="combine_ln2",
    )(x1, ya, yb, ew, g2, ln_g2, ln_b2)


def _dispatch_plan(eid, n_tokens):
    tm = EXPERT_TILE
    flat_e = eid.reshape(-1)
    n_assign = flat_e.shape[0]
    order = jnp.argsort(flat_e, stable=True)
    sorted_e = flat_e[order]
    counts = jnp.bincount(flat_e, length=MOE_TOTAL)
    tiles_per = (counts + tm - 1) // tm
    tile_end = jnp.cumsum(tiles_per)
    pad_start = (tile_end - tiles_per) * tm
    seg_start = jnp.cumsum(counts) - counts
    rank = jnp.arange(n_assign, dtype=jnp.int32) - seg_start[sorted_e].astype(jnp.int32)
    dest = pad_start[sorted_e].astype(jnp.int32) + rank
    n_rows = n_assign + MOE_TOTAL * tm
    src_token = jnp.zeros((n_rows,), jnp.int32).at[dest].set((order // 2).astype(jnp.int32))
    pos = jnp.zeros((n_assign,), jnp.int32).at[order].set(dest)
    n_tiles = n_rows // tm
    tile_expert = jnp.minimum(jnp.searchsorted(tile_end, jnp.arange(n_tiles), side="right"),
                              MOE_TOTAL - 1).astype(jnp.int32)
    return src_token, pos.reshape(n_tokens, 2), tile_expert, tile_end[-1:].astype(jnp.int32)


def _layer(x, c, rel_bias, w_ada, b_ada, w_in, w_gla_gate, b_gla_gate, gla_norm, w_proj_gla, w_proj_attn, w_out,
           ln1_g, ln1_b, w_rg, b_rg, w_re, b_re, w_eg, w_eu, w_ed, ln2_g, ln2_b):
    bsz, s, d = x.shape
    alpha = (2.0 * DEPTH) ** 0.25
    mods = _ada_mods(c, w_ada, b_ada)
    sh1, sc1, g1, sh2, sc2, g2 = [m.reshape(bsz, 1, d) for m in jnp.split(mods, N_MOD, axis=-1)]

    lr0 = d // 2 * 2 + 2 * d
    w_perm = jnp.concatenate([w_in[:, :lr0], w_in[:, lr0 + GLA_LOWRANK:], w_in[:, lr0:lr0 + GLA_LOWRANK],
                              jnp.zeros((d, LANES - GLA_LOWRANK), w_in.dtype)], axis=1).astype(BF16)
    z = _in_projection(x, sc1, sh1, w_perm)

    y_gla = _gla(z["q_gla"], z["k_gla"], z["v_gla"], z["r_gla"], z["lr"], w_gla_gate, b_gla_gate, gla_norm)

    o_groups, lse_groups = [], []
    for g, (window, dilation) in enumerate(DIL_PATTERNS):
        cs = slice(g * DIL_GROUP_WIDTH, (g + 1) * DIL_GROUP_WIDTH)
        qg, kg, vg = (_to_strided(z[n][:, :, cs], dilation) for n in ("q_att", "k_att", "v_att"))
        table = rel_bias[:, g * DIL_HEADS_PER_GROUP:(g + 1) * DIL_HEADS_PER_GROUP]
        o, lse = _dilated_group_attention(qg, kg, vg, table, window, dilation)
        o_groups.append(_from_strided(o, dilation, bsz))
        lse_groups.append(_from_strided(lse, dilation, bsz))

    wr = jnp.concatenate([w_rg, w_re, jnp.zeros((d, LANES - MOE_GROUPS - MOE_TOTAL), F32)], axis=1)
    br = jnp.concatenate([b_rg, b_re, jnp.zeros((LANES - MOE_GROUPS - MOE_TOTAL,), F32)]).reshape(1, LANES)
    x1, u2, eid, ew = _merge(alpha, y_gla, o_groups, lse_groups, z["g_gla"], z["g_att"], x, g1, sc2, sh2,
                             ln1_g, ln1_b, w_proj_gla.astype(BF16), w_proj_attn.astype(BF16), w_out.astype(BF16),
                             wr, br)

    n = bsz * s
    src_token, pos, tile_expert, n_used = _dispatch_plan(eid.reshape(n, LANES)[:, :2], n)
    xg = u2.reshape(n, d)[src_token]
    ff = w_eg.shape[-1]
    yo = _expert_ffn(tile_expert, n_used, xg, w_eg.reshape(MOE_TOTAL, d, ff), w_eu.reshape(MOE_TOTAL, d, ff),
                     w_ed.reshape(MOE_TOTAL, ff, d))
    ya = yo[pos[:, 0]].reshape(bsz, s, d)
    yb = yo[pos[:, 1]].reshape(bsz, s, d)
    return _final(alpha, x1, ya, yb, ew, g2, ln2_g, ln2_b)


def kernel(x, c, rel_bias, w_ada, b_ada, w_in, w_gla_gate, b_gla_gate, gla_norm, w_proj_gla, w_proj_attn, w_out,
           ln1_g, ln1_b, w_router_group, b_router_group, w_router_expert, b_router_expert, w_exp_gate, w_exp_up,
           w_exp_down, ln2_g, ln2_b):
    assert w_ada.shape[0] == DEPTH
    return _layer(x, c, rel_bias, w_ada[0], b_ada[0], w_in[0], w_gla_gate[0], b_gla_gate[0], gla_norm[0],
                  w_proj_gla[0], w_proj_attn[0], w_out[0], ln1_g[0], ln1_b[0], w_router_group[0],
                  b_router_group[0], w_router_expert[0], b_router_expert[0], w_exp_gate[0], w_exp_up[0],
                  w_exp_down[0], ln2_g[0], ln2_b[0])
```

```python
import functools
import math

import numpy as np
import jax
import jax.numpy as jnp
from jax import lax
from jax.experimental import pallas as pl
from jax.experimental.pallas import tpu as pltpu

F32 = jnp.float32
BF16 = jnp.bfloat16

N_MOD = 6
GLA_HEADS = 4
GLA_LOWRANK = 16
GLA_TAU = 16.0
GLA_CHUNK = 64
DIL_PATTERNS = ((128, 1), (512, 4), (2048, 16))
DIL_GROUPS = len(DIL_PATTERNS)
DIL_HEADS_PER_GROUP = 8
DIL_HEAD_DIM = 64
DIL_GROUP_WIDTH = DIL_HEADS_PER_GROUP * DIL_HEAD_DIM
DIL_BLOCK = 128
REL_BUCKETS = 32
REL_MAX_DIST = 2048
MOE_GROUPS = 4
MOE_EXPERTS = 8
MOE_TOTAL = MOE_GROUPS * MOE_EXPERTS
LN_EPS = 1e-5
DEPTH = 1

LANES = 128
VMEM_LIMIT = 56 * 1024 * 1024
NEG = -1e30
ROW_TILE = 512
EXPERT_TILE = 256
GLA_STEP_CHUNKS = 4
ATT_STEP_BLOCKS = 2

HIGHEST = lax.Precision.HIGHEST
NT_DIMS = (((1,), (1,)), ((), ()))
TN_DIMS = (((0,), (0,)), ((), ()))


def _cparams(sem):
    return pltpu.CompilerParams(dimension_semantics=sem, vmem_limit_bytes=VMEM_LIMIT)


def _sigmoid(x):
    return 1.0 / (1.0 + jnp.exp(-x))


def _silu(x):
    return x * _sigmoid(x)


def _layer_norm(x, g, b):
    mu = jnp.mean(x, axis=-1, keepdims=True)
    xc = x - mu
    var = jnp.mean(xc * xc, axis=-1, keepdims=True)
    return xc * lax.rsqrt(var + LN_EPS) * g + b


def _split3(x):
    hi = x.astype(BF16)
    r1 = x - hi.astype(F32)
    mid = r1.astype(BF16)
    lo = (r1 - mid.astype(F32)).astype(BF16)
    return hi, mid, lo


def _mods_kernel(c_ref, w_ref, b_ref, o_ref):
    a = _silu(c_ref[...])
    o_ref[...] = jnp.dot(a, w_ref[...], precision=HIGHEST, preferred_element_type=F32) + b_ref[...]


def _ada_mods(c, w, b):
    bsz, d = c.shape
    n = w.shape[1]
    tn = 1536
    assert n % tn == 0
    return pl.pallas_call(
        _mods_kernel,
        out_shape=jax.ShapeDtypeStruct((bsz, n), F32),
        grid=(n // tn,),
        in_specs=[pl.BlockSpec((bsz, d), lambda j: (0, 0)),
                  pl.BlockSpec((d, tn), lambda j: (0, j)),
                  pl.BlockSpec((1, tn), lambda j: (0, j))],
        out_specs=pl.BlockSpec((bsz, tn), lambda j: (0, j)),
        compiler_params=_cparams(("arbitrary",)),
        name="ada_mods",
    )(c, w, b.reshape(1, n))


def _proj_pieces(d_model):
    dk = d_model // 2
    pieces = [("q_gla", dk, "scale_q_gla"), ("k_gla", dk, None), ("v_gla", d_model, None), ("r_gla", d_model, "silu")]
    for name, post in (("q_att", "scale_q_att"), ("k_att", None), ("v_att", None)):
        for g, (_, dilation) in enumerate(DIL_PATTERNS):
            pieces.append((f"{name}{g}", DIL_GROUP_WIDTH, (post, dilation)))
    pieces += [("g_gla", d_model, "sigmoid"), ("g_att", d_model, "sigmoid"), ("lr", LANES, "lowrank")]
    return tuple(pieces)


def _proj_kernel(pieces, head_k, x_ref, sc_ref, sh_ref, w_ref, *refs):
    out_refs, stage_ref = refs[:-1], refs[-1]
    tm = x_ref.shape[1]
    u = (x_ref[0] * (1.0 + sc_ref[0]) + sh_ref[0]).astype(BF16)
    off = 0
    for (name, width, post), o_ref in zip(pieces, out_refs):
        chunk = min(width, 512)
        for c0 in range(0, width, chunk):
            acc = jnp.dot(u, w_ref[:, off + c0:off + c0 + chunk], preferred_element_type=F32)
            if post == "silu":
                acc = _silu(acc)
            elif post == "sigmoid":
                acc = _sigmoid(acc)
            elif post == "scale_q_gla":
                acc = acc * (head_k ** -0.5)
            if post == "lowrank":
                o_ref[0] = acc[:, :GLA_LOWRANK]
            elif isinstance(post, tuple):
                scale, dilation = post
                if scale is not None:
                    acc = acc * (DIL_HEAD_DIM ** -0.5)
                if dilation == 1:
                    o_ref[0, 0] = acc.astype(o_ref.dtype)
                else:
                    for t in range(width // LANES):
                        stage_ref[t] = acc[:, t * LANES:(t + 1) * LANES]
                    for r in range(dilation):
                        for t in range(width // LANES):
                            o_ref[0, r, :, t * LANES:(t + 1) * LANES] = stage_ref[
                                t, pl.ds(r, tm // dilation, stride=dilation), :].astype(o_ref.dtype)
            else:
                o_ref[0, :, c0:c0 + chunk] = acc.astype(o_ref.dtype)
        off += width


def _in_projection(x, sc1, sh1, w_perm):
    bsz, s, d = x.shape
    pieces = _proj_pieces(d)
    assert sum(p[1] for p in pieces) == w_perm.shape[1]
    tm = min(ROW_TILE, s)
    assert s % tm == 0
    head_k = (d // 2) // GLA_HEADS
    out_shape, out_specs = [], []
    for name, width, post in pieces:
        if post == "lowrank":
            out_shape.append(jax.ShapeDtypeStruct((bsz, s, GLA_LOWRANK), F32))
            out_specs.append(pl.BlockSpec((1, tm, GLA_LOWRANK), lambda b, i: (b, i, 0)))
        elif isinstance(post, tuple):
            dil = post[1]
            assert tm % (dil * 16) == 0
            out_shape.append(jax.ShapeDtypeStruct((bsz, dil, s // dil, width), BF16))
            out_specs.append(pl.BlockSpec((1, dil, tm // dil, width), lambda b, i: (b, 0, i, 0)))
        else:
            out_shape.append(jax.ShapeDtypeStruct((bsz, s, width), BF16))
            out_specs.append(pl.BlockSpec((1, tm, width), lambda b, i: (b, i, 0)))
    outs = pl.pallas_call(
        functools.partial(_proj_kernel, pieces, head_k),
        out_shape=out_shape,
        grid=(bsz, s // tm),
        in_specs=[pl.BlockSpec((1, tm, d), lambda b, i: (b, i, 0)),
                  pl.BlockSpec((1, 1, d), lambda b, i: (b, 0, 0)),
                  pl.BlockSpec((1, 1, d), lambda b, i: (b, 0, 0)),
                  pl.BlockSpec(w_perm.shape, lambda b, i: (0, 0), pipeline_mode=pl.Buffered(1))],
        out_specs=out_specs,
        scratch_shapes=[pltpu.VMEM((DIL_GROUP_WIDTH // LANES, tm, LANES), F32)],
        compiler_params=_cparams(("parallel", "arbitrary")),
        name="in_projection",
    )(x, sc1, sh1, w_perm)
    return dict(zip([p[0] for p in pieces], outs))


def _gla_kernel(n_chunks, head_k, head_v, q_ref, k_ref, v_ref, r_ref, lr_ref, wg_ref, bg_ref, ng_ref, o_ref,
                state_ref):
    @pl.when(pl.program_id(1) == 0)
    def _():
        state_ref[...] = jnp.zeros_like(state_ref)

    c = GLA_CHUNK
    row = lax.broadcasted_iota(jnp.int32, (c, c), 0)
    col = lax.broadcasted_iota(jnp.int32, (c, c), 1)
    causal = row >= col
    tril = causal.astype(BF16)
    mid = c // 2 - 1
    gate_in = jnp.dot(lr_ref[0], wg_ref[...], precision=HIGHEST, preferred_element_type=F32) + bg_ref[...]
    g_all = (jnp.minimum(gate_in, 0.0) - jnp.log(1.0 + jnp.exp(-jnp.abs(gate_in)))) * (1.0 / GLA_TAU)
    g_hi, g_mid, g_lo = _split3(g_all)
    for ci in range(n_chunks):
        rows = slice(ci * c, (ci + 1) * c)
        bc = (jnp.dot(tril, g_hi[rows], preferred_element_type=F32)
              + jnp.dot(tril, g_mid[rows], preferred_element_type=F32)
              + jnp.dot(tril, g_lo[rows], preferred_element_type=F32))
        b_mid = bc[mid:mid + 1, :]
        b_last = bc[c - 1:c, :]
        qf = q_ref[0, rows, :].astype(F32)
        kf = k_ref[0, rows, :].astype(F32)
        q_in = (qf * jnp.exp(bc - b_mid)).astype(BF16)
        k_in = (kf * jnp.exp(b_mid - bc)).astype(BF16)
        q_st = (qf * jnp.exp(bc)).astype(BF16)
        k_st = (kf * jnp.exp(b_last - bc)).astype(BF16)
        dec = jnp.exp(b_last)
        for h in range(GLA_HEADS):
            ks = slice(h * head_k, (h + 1) * head_k)
            vs = slice(h * head_v, (h + 1) * head_v)
            vh = v_ref[0, rows, vs]
            att = lax.dot_general(q_in[:, ks], k_in[:, ks], NT_DIMS, preferred_element_type=F32)
            att = jnp.where(causal, att, 0.0).astype(BF16)
            st = state_ref[h]
            o = jnp.dot(att, vh, preferred_element_type=F32)
            o = o + lax.dot_general(q_st[:, ks], st.astype(BF16), NT_DIMS, preferred_element_type=F32)
            kv_t = lax.dot_general(vh, k_st[:, ks], TN_DIMS, preferred_element_type=F32)
            state_ref[h] = st * dec[:, ks] + kv_t
            ms = jnp.mean(o * o, axis=-1, keepdims=True)
            o = o * lax.rsqrt(ms + LN_EPS) * ng_ref[:, vs] * r_ref[0, rows, vs].astype(F32)
            o_ref[0, rows, vs] = o.astype(o_ref.dtype)


def _gla(q, k, v, r_silu, lr, w_gate, b_gate, norm_g):
    bsz, s, dk = q.shape
    dv = v.shape[-1]
    head_k, head_v = dk // GLA_HEADS, dv // GLA_HEADS
    n_chunks = min(GLA_STEP_CHUNKS, s // GLA_CHUNK)
    ct = GLA_CHUNK * n_chunks
    assert s % ct == 0
    row_spec = lambda w: pl.BlockSpec((1, ct, w), lambda b, i: (b, i, 0))
    full = lambda a: pl.BlockSpec(a.shape, lambda b, i: (0,) * a.ndim)
    bg = b_gate.reshape(1, dk)
    ng = norm_g.reshape(1, dv)
    return pl.pallas_call(
        functools.partial(_gla_kernel, n_chunks, head_k, head_v),
        out_shape=jax.ShapeDtypeStruct((bsz, s, dv), BF16),
        grid=(bsz, s // ct),
        in_specs=[row_spec(dk), row_spec(dk), row_spec(dv), row_spec(dv), row_spec(GLA_LOWRANK),
                  full(w_gate), full(bg), full(ng)],
        out_specs=row_spec(dv),
        scratch_shapes=[pltpu.VMEM((GLA_HEADS, head_v, head_k), F32)],
        compiler_params=_cparams(("parallel", "arbitrary")),
        name="gla",
    )(q, k, v, r_silu, lr, w_gate, bg, ng)


def _t5_bucket_np(dist):
    exact = REL_BUCKETS // 2
    d = np.maximum(dist, 1).astype(np.float32)
    large = exact + (np.log(d / np.float32(exact)) / np.float32(math.log(REL_MAX_DIST / exact))
                     * np.float32(REL_BUCKETS - exact)).astype(np.int32)
    large = np.minimum(large, REL_BUCKETS - 1)
    return np.where(dist < exact, dist, large).astype(np.int32)


def _band_tables(window, dilation):
    qi = np.arange(DIL_BLOCK)[:, None]
    kj = np.arange(2 * DIL_BLOCK)[None, :]
    m = qi + DIL_BLOCK - kj
    n_steps = window // dilation
    band = (m >= 0) & (m <= n_steps)
    bucket = _t5_bucket_np(np.clip(m, 0, n_steps) * dilation)
    return np.where(band, bucket, -1).astype(np.int32)


def _attn_kernel(nq, table_ref, bucket_ref, q_ref, kp_ref, kc_ref, vp_ref, vc_ref, o_ref, lse_ref,
                 bias_ref, s_ref, p_ref):
    i = pl.program_id(1)
    blk = DIL_BLOCK
    hpg = DIL_HEADS_PER_GROUP
    n_pairs = hpg // 2

    @pl.when((pl.program_id(0) == 0) & (i == 0))
    def _():
        bucket = bucket_ref[...]
        for h in range(hpg):
            acc = jnp.full(bucket.shape, NEG, F32)
            for bkt in range(REL_BUCKETS):
                acc = jnp.where(bucket == bkt, table_ref[bkt, h], acc)
            bias_ref[h * blk:(h + 1) * blk, :] = acc

    lane = lax.broadcasted_iota(jnp.int32, (blk, LANES), 1)
    low = lane < DIL_HEAD_DIM
    ones_rhs = jnp.ones((2 * blk, LANES), BF16)

    def windows(ref_p, ref_c, qb, cols):
        if qb == 0:
            return jnp.concatenate([ref_p[0, :, cols], ref_c[0, 0:blk, cols]], axis=0)
        return ref_c[0, (qb - 1) * blk:(qb + 1) * blk, cols]

    for qb in range(nq):
        rows = slice(qb * blk, (qb + 1) * blk)
        for hp in range(n_pairs):
            cols = slice(hp * LANES, (hp + 1) * LANES)
            qp = q_ref[0, rows, cols]
            zero = jnp.zeros_like(qp)
            qq = jnp.concatenate([jnp.where(low, qp, zero), jnp.where(low, zero, qp)], axis=0)
            keys = windows(kp_ref, kc_ref, qb, cols)
            base = (qb * hpg + 2 * hp) * blk
            s_ref[base:base + 2 * blk, :] = lax.dot_general(qq, keys, NT_DIMS, preferred_element_type=F32)

    @pl.when(i == 0)
    def _():
        s_ref[0:hpg * blk, 0:blk] = jnp.full((hpg * blk, blk), NEG, F32)

    mxs = []
    for qb in range(nq):
        rs = slice(qb * hpg * blk, (qb + 1) * hpg * blk)
        s = s_ref[rs, :] + bias_ref[...]
        mx = jnp.max(s, axis=-1, keepdims=True)
        p_ref[rs, :] = jnp.exp(s - mx).astype(BF16)
        mxs.append(mx)

    for qb in range(nq):
        rows = slice(qb * blk, (qb + 1) * blk)
        for hp in range(n_pairs):
            cols = slice(hp * LANES, (hp + 1) * LANES)
            vals = windows(vp_ref, vc_ref, qb, cols)
            rhs = jnp.concatenate([vals, ones_rhs], axis=1)
            base = (qb * hpg + 2 * hp) * blk
            res = jnp.dot(p_ref[base:base + 2 * blk, :], rhs, preferred_element_type=F32)
            num = jnp.where(low, res[0:blk, 0:LANES], res[blk:2 * blk, 0:LANES])
            den = jnp.where(low, res[0:blk, LANES:], res[blk:2 * blk, LANES:])
            off = 2 * hp * blk
            mx = jnp.where(low, mxs[qb][off:off + blk], mxs[qb][off + blk:off + 2 * blk])
            o_ref[0, rows, cols] = (num / den).astype(o_ref.dtype)
            lse_ref[0, rows, cols] = mx + jnp.log(den)


def _dilated_group_attention(q, k, v, table, window, dilation):
    bb, l, w = q.shape
    nq = ATT_STEP_BLOCKS
    assert l % (nq * DIL_BLOCK) == 0
    steps = l // (nq * DIL_BLOCK)
    bucket = jnp.asarray(_band_tables(window, dilation))
    cur = pl.BlockSpec((1, nq * DIL_BLOCK, w), lambda b, i: (b, i, 0))
    prev = pl.BlockSpec((1, DIL_BLOCK, w), lambda b, i: (b, jnp.maximum(nq * i - 1, 0), 0))
    rows_all = nq * DIL_HEADS_PER_GROUP * DIL_BLOCK
    return pl.pallas_call(
        functools.partial(_attn_kernel, nq),
        out_shape=[jax.ShapeDtypeStruct((bb, l, w), BF16), jax.ShapeDtypeStruct((bb, l, w), F32)],
        grid=(bb, steps),
        in_specs=[pl.BlockSpec(memory_space=pltpu.SMEM),
                  pl.BlockSpec(bucket.shape, lambda b, i: (0, 0)),
                  cur, prev, cur, prev, cur],
        out_specs=[cur, cur],
        scratch_shapes=[pltpu.VMEM((DIL_HEADS_PER_GROUP * DIL_BLOCK, 2 * DIL_BLOCK), F32),
                        pltpu.VMEM((rows_all, 2 * DIL_BLOCK), F32),
                        pltpu.VMEM((rows_all, 2 * DIL_BLOCK), BF16)],
        compiler_params=_cparams(("arbitrary", "arbitrary")),
        name=f"dilated_attn_d{dilation}",
    )(table, bucket, q, k, k, v, v)


def _merge_kernel(alpha, dilations, ygla_ref, o0_ref, o1_ref, o2_ref, l0_ref, l1_ref, l2_ref, gg_ref, ga_ref, x_ref,
                  g1_ref, sc2_ref, sh2_ref, ln_g_ref, ln_b_ref, wpg_ref, wpa_ref, wout_ref, wr_ref, br_ref, ltri_ref,
                  x1_ref, u2_ref, route_ref, ew_ref, cnt_ref, stage_ref, carry_ref):
    tm = x_ref.shape[1]

    @pl.when((pl.program_id(0) == 0) & (pl.program_id(1) == 0))
    def _():
        carry_ref[...] = jnp.zeros_like(carry_ref)

    def natural(ref, dilation, slot):
        if dilation == 1:
            return ref[0, 0].astype(F32)
        n_lt = DIL_GROUP_WIDTH // LANES
        for r in range(dilation):
            for t in range(n_lt):
                stage_ref[slot, t, pl.ds(r, tm // dilation, stride=dilation), :] = ref[
                    0, r, :, t * LANES:(t + 1) * LANES].astype(F32)
        return jnp.concatenate([stage_ref[slot, t] for t in range(n_lt)], axis=1)

    lses = [natural(ref, dil, 0 + 2 * gi) for gi, (ref, dil) in enumerate(zip((l0_ref, l1_ref, l2_ref), dilations))]
    outs = [natural(ref, dil, 1 + 2 * gi) for gi, (ref, dil) in enumerate(zip((o0_ref, o1_ref, o2_ref), dilations))]
    lm = jnp.maximum(jnp.maximum(lses[0], lses[1]), lses[2])
    es = [jnp.exp(l - lm) for l in lses]
    y_att = (es[0] * outs[0] + es[1] * outs[1] + es[2] * outs[2]) / (es[0] + es[1] + es[2])

    p_gla = jnp.dot(ygla_ref[0], wpg_ref[...], preferred_element_type=F32)
    p_att = jnp.dot(y_att.astype(BF16), wpa_ref[...], preferred_element_type=F32)
    merged = gg_ref[0].astype(F32) * p_gla + ga_ref[0].astype(F32) * p_att
    y = jnp.dot(merged.astype(BF16), wout_ref[...], preferred_element_type=F32)
    x1 = _layer_norm(alpha * x_ref[0] + g1_ref[0] * y, ln_g_ref[...], ln_b_ref[...])
    x1_ref[0] = x1
    u2 = x1 * (1.0 + sc2_ref[0]) + sh2_ref[0]
    u2_ref[0] = u2.astype(BF16)

    u_hi = u2.astype(BF16)
    u_lo = (u2 - u_hi.astype(F32)).astype(BF16)
    wr = wr_ref[...]
    w_hi = wr.astype(BF16)
    w_lo = (wr - w_hi.astype(F32)).astype(BF16)
    logits = (jnp.dot(u_hi, w_hi, preferred_element_type=F32) + jnp.dot(u_lo, w_hi, preferred_element_type=F32)
              + jnp.dot(u_hi, w_lo, preferred_element_type=F32)) + br_ref[...]
    lane = lax.broadcasted_iota(jnp.int32, logits.shape, 1)
    big = jnp.int32(LANES)
    lg = jnp.where(lane < MOE_GROUPS, logits, NEG)
    gmax = jnp.max(lg, axis=-1, keepdims=True)
    gidx = jnp.min(jnp.where(lg == gmax, lane, big), axis=-1, keepdims=True)
    gval = 1.0 / jnp.sum(jnp.exp(lg - gmax), axis=-1, keepdims=True)
    in_group = (lane >= MOE_GROUPS + gidx * MOE_EXPERTS) & (lane < MOE_GROUPS + (gidx + 1) * MOE_EXPERTS)
    le = jnp.where(in_group, logits, NEG)
    m1 = jnp.max(le, axis=-1, keepdims=True)
    i1 = jnp.min(jnp.where(le == m1, lane, big), axis=-1, keepdims=True)
    le2 = jnp.where(lane == i1, NEG, le)
    m2 = jnp.max(le2, axis=-1, keepdims=True)
    i2 = jnp.min(jnp.where(le2 == m2, lane, big), axis=-1, keepdims=True)
    t = jnp.exp(m2 - m1)
    w1 = 1.0 / (1.0 + t)
    w2 = t * w1

    hit1, hit2 = lane == i1, lane == i2
    onehot = jnp.where(hit1 | hit2, 1.0, 0.0)
    earlier = jnp.dot(ltri_ref[...], onehot.astype(BF16), preferred_element_type=F32) + carry_ref[...]
    rank1 = jnp.sum(jnp.where(hit1, earlier, 0.0), axis=-1, keepdims=True).astype(jnp.int32)
    rank2 = jnp.sum(jnp.where(hit2, earlier, 0.0), axis=-1, keepdims=True).astype(jnp.int32)
    carry = carry_ref[...] + jnp.sum(onehot, axis=0, keepdims=True)
    carry_ref[...] = carry
    cnt_ref[...] = carry.astype(jnp.int32)
    route_ref[0] = jnp.where(lane == 0, i1 - MOE_GROUPS, jnp.where(lane == 1, i2 - MOE_GROUPS,
                             jnp.where(lane == 2, rank1, jnp.where(lane == 3, rank2, 0))))
    ew_ref[0] = jnp.where(lane == 0, gval * w1, jnp.where(lane == 1, gval * w2, 0.0))


def _merge(alpha, y_gla, o_groups, lse_groups, g_gla, g_att, x, g1, sc2, sh2, ln_g, ln_b, wpg, wpa, wout, wr, br):
    bsz, s, d = x.shape
    tm = min(ROW_TILE, s)
    assert s % tm == 0
    dilations = tuple(dil for _, dil in DIL_PATTERNS)
    row = lambda w: pl.BlockSpec((1, tm, w), lambda b, i: (b, i, 0))
    sub = lambda dil: pl.BlockSpec((1, dil, tm // dil, DIL_GROUP_WIDTH), lambda b, i: (b, 0, i, 0))
    per_b = pl.BlockSpec((1, 1, d), lambda b, i: (b, 0, 0))
    full = lambda a: pl.BlockSpec(a.shape, lambda b, i: (0,) * a.ndim)
    ln_g2, ln_b2 = ln_g.reshape(1, d), ln_b.reshape(1, d)
    ltri = jnp.asarray(np.tril(np.ones((tm, tm), np.float32), -1), BF16)
    return pl.pallas_call(
        functools.partial(_merge_kernel, alpha, dilations),
        out_shape=[jax.ShapeDtypeStruct((bsz, s, d), F32), jax.ShapeDtypeStruct((bsz, s, d), BF16),
                   jax.ShapeDtypeStruct((bsz, s, LANES), jnp.int32), jax.ShapeDtypeStruct((bsz, s, LANES), F32),
                   jax.ShapeDtypeStruct((1, LANES), jnp.int32)],
        grid=(bsz, s // tm),
        in_specs=[row(y_gla.shape[-1])] + [sub(dil) for dil in dilations] * 2
                 + [row(d), row(d), row(d), per_b, per_b, per_b, full(ln_g2), full(ln_b2),
                    full(wpg), full(wpa), full(wout), full(wr), full(br), full(ltri)],
        out_specs=[row(d), row(d), row(LANES), row(LANES), pl.BlockSpec((1, LANES), lambda b, i: (0, 0))],
        scratch_shapes=[pltpu.VMEM((2 * DIL_GROUPS, DIL_GROUP_WIDTH // LANES, tm, LANES), F32),
                        pltpu.VMEM((1, LANES), F32)],
        compiler_params=_cparams(("arbitrary", "arbitrary")),
        name="merge_ln1_router",
    )(y_gla, *o_groups, *lse_groups, g_gla, g_att, x, g1, sc2, sh2, ln_g2, ln_b2, wpg, wpa, wout, wr, br, ltri)


def _expert_kernel(te_ref, nt_ref, x_ref, wg_ref, wu_ref, wd_ref, o_ref, wg_s, wu_s, wd_s):
    t = pl.program_id(0)
    prev_e = te_ref[jnp.maximum(t - 1, 0)]

    @pl.when((t < nt_ref[0]) & ((t == 0) | (te_ref[t] != prev_e)))
    def _():
        wg_s[...] = wg_ref[0].astype(BF16)
        wu_s[...] = wu_ref[0].astype(BF16)
        wd_s[...] = wd_ref[0].astype(BF16)

    @pl.when(t < nt_ref[0])
    def _():
        xt = x_ref[...]
        hg = jnp.dot(xt, wg_s[...], preferred_element_type=F32)
        hu = jnp.dot(xt, wu_s[...], preferred_element_type=F32)
        h = (_silu(hg) * hu).astype(BF16)
        o_ref[...] = jnp.dot(h, wd_s[...], preferred_element_type=F32).astype(o_ref.dtype)

    @pl.when(t >= nt_ref[0])
    def _():
        o_ref[...] = jnp.zeros_like(o_ref)


def _expert_ffn(tile_expert, n_tiles_used, xg, w_gate, w_up, w_down):
    p, d = xg.shape
    ne, _, ff = w_gate.shape
    tm = EXPERT_TILE
    n_tiles = p // tm
    grid_spec = pltpu.PrefetchScalarGridSpec(
        num_scalar_prefetch=2,
        grid=(n_tiles,),
        in_specs=[pl.BlockSpec((tm, d), lambda t, te, nt: (t, 0)),
                  pl.BlockSpec((1, d, ff), lambda t, te, nt: (te[t], 0, 0)),
                  pl.BlockSpec((1, d, ff), lambda t, te, nt: (te[t], 0, 0)),
                  pl.BlockSpec((1, ff, d), lambda t, te, nt: (te[t], 0, 0))],
        out_specs=pl.BlockSpec((tm, d), lambda t, te, nt: (t, 0)),
        scratch_shapes=[pltpu.VMEM((d, ff), BF16), pltpu.VMEM((d, ff), BF16), pltpu.VMEM((ff, d), BF16)],
    )
    return pl.pallas_call(
        _expert_kernel,
        out_shape=jax.ShapeDtypeStruct((p, d), BF16),
        grid_spec=grid_spec,
        compiler_params=_cparams(("arbitrary",)),
        name="expert_ffn",
    )(tile_expert, n_tiles_used, xg, w_gate, w_up, w_down)


def _final_kernel(alpha, x1_ref, ya_ref, yb_ref, ew_ref, g2_ref, ln_g_ref, ln_b_ref, o_ref):
    ew = ew_ref[0]
    y = ew[:, 0:1] * ya_ref[0].astype(F32) + ew[:, 1:2] * yb_ref[0].astype(F32)
    o_ref[0] = _layer_norm(alpha * x1_ref[0] + g2_ref[0] * y, ln_g_ref[...], ln_b_ref[...])


def _final(alpha, x1, ya, yb, ew, g2, ln_g, ln_b):
    bsz, s, d = x1.shape
    tm = min(ROW_TILE, s)
    row = lambda w: pl.BlockSpec((1, tm, w), lambda b, i: (b, i, 0))
    full = lambda a: pl.BlockSpec(a.shape, lambda b, i: (0,) * a.ndim)
    ln_g2, ln_b2 = ln_g.reshape(1, d), ln_b.reshape(1, d)
    return pl.pallas_call(
        functools.partial(_final_kernel, alpha),
        out_shape=jax.ShapeDtypeStruct((bsz, s, d), F32),
        grid=(bsz, s // tm),
        in_specs=[row(d), row(d), row(d), row(LANES), pl.BlockSpec((1, 1, d), lambda b, i: (b, 0, 0)),
                  full(ln_g2), full(ln_b2)],
        out_specs=row(d),
        compiler_params=_cparams(("parallel", "arbitrary")),
        name="combine_ln2",
    )(x1, ya, yb, ew, g2, ln_g2, ln_b2)


def _dispatch_plan(eid, rank, counts):
    tm = EXPERT_TILE
    n_assign = eid.size
    tiles_per = (counts + tm - 1) // tm
    tile_end = jnp.cumsum(tiles_per)
    pad_start = ((tile_end - tiles_per) * tm).astype(jnp.int32)
    dest = pad_start[eid] + rank
    n_tiles = (n_assign + MOE_TOTAL * tm) // tm
    tile_expert = jnp.minimum(jnp.searchsorted(tile_end, jnp.arange(n_tiles), side="right"),
                              MOE_TOTAL - 1).astype(jnp.int32)
    return dest, tile_expert, tile_end[-1:].astype(jnp.int32), n_tiles * tm


def _layer(x, c, rel_bias, w_ada, b_ada, w_in, w_gla_gate, b_gla_gate, gla_norm, w_proj_gla, w_proj_attn, w_out,
           ln1_g, ln1_b, w_rg, b_rg, w_re, b_re, w_eg, w_eu, w_ed, ln2_g, ln2_b):
    bsz, s, d = x.shape
    alpha = (2.0 * DEPTH) ** 0.25
    mods = _ada_mods(c, w_ada, b_ada)
    sh1, sc1, g1, sh2, sc2, g2 = [m.reshape(bsz, 1, d) for m in jnp.split(mods, N_MOD, axis=-1)]

    lr0 = d // 2 * 2 + 2 * d
    w_perm = jnp.concatenate([w_in[:, :lr0], w_in[:, lr0 + GLA_LOWRANK:], w_in[:, lr0:lr0 + GLA_LOWRANK],
                              jnp.zeros((d, LANES - GLA_LOWRANK), w_in.dtype)], axis=1).astype(BF16)
    z = _in_projection(x, sc1, sh1, w_perm)

    y_gla = _gla(z["q_gla"], z["k_gla"], z["v_gla"], z["r_gla"], z["lr"], w_gla_gate, b_gla_gate, gla_norm)

    o_groups, lse_groups = [], []
    for g, (window, dilation) in enumerate(DIL_PATTERNS):
        l = s // dilation
        qg, kg, vg = (z[f"{n}{g}"].reshape(bsz * dilation, l, DIL_GROUP_WIDTH) for n in ("q_att", "k_att", "v_att"))
        table = rel_bias[:, g * DIL_HEADS_PER_GROUP:(g + 1) * DIL_HEADS_PER_GROUP]
        o, lse = _dilated_group_attention(qg, kg, vg, table, window, dilation)
        o_groups.append(o.reshape(bsz, dilation, l, DIL_GROUP_WIDTH))
        lse_groups.append(lse.reshape(bsz, dilation, l, DIL_GROUP_WIDTH))

    wr = jnp.concatenate([w_rg, w_re, jnp.zeros((d, LANES - MOE_GROUPS - MOE_TOTAL), F32)], axis=1)
    br = jnp.concatenate([b_rg, b_re, jnp.zeros((LANES - MOE_GROUPS - MOE_TOTAL,), F32)]).reshape(1, LANES)
    x1, u2, route, ew, cnt = _merge(alpha, y_gla, o_groups, lse_groups, z["g_gla"], z["g_att"], x, g1, sc2, sh2,
                                    ln1_g, ln1_b, w_proj_gla.astype(BF16), w_proj_attn.astype(BF16),
                                    w_out.astype(BF16), wr, br)

    n = bsz * s
    route = route.reshape(n, LANES)
    counts = cnt[0, MOE_GROUPS:MOE_GROUPS + MOE_TOTAL]
    dest, tile_expert, n_used, n_rows = _dispatch_plan(route[:, 0:2], route[:, 2:4], counts)
    u2f = u2.reshape(n, d)
    xg = jnp.zeros((n_rows, d), BF16)
    xg = xg.at[dest[:, 0]].set(u2f, unique_indices=True, mode="drop")
    xg = xg.at[dest[:, 1]].set(u2f, unique_indices=True, mode="drop")
    ff = w_eg.shape[-1]
    yo = _expert_ffn(tile_expert, n_used, xg, w_eg.reshape(MOE_TOTAL, d, ff), w_eu.reshape(MOE_TOTAL, d, ff),
                     w_ed.reshape(MOE_TOTAL, ff, d))
    ya = yo[dest[:, 0]].reshape(bsz, s, d)
    yb = yo[dest[:, 1]].reshape(bsz, s, d)
    return _final(alpha, x1, ya, yb, ew, g2, ln2_g, ln2_b)


def kernel(x, c, rel_bias, w_ada, b_ada, w_in, w_gla_gate, b_gla_gate, gla_norm, w_proj_gla, w_proj_attn, w_out,
           ln1_g, ln1_b, w_router_group, b_router_group, w_router_expert, b_router_expert, w_exp_gate, w_exp_up,
           w_exp_down, ln2_g, ln2_b):
    assert w_ada.shape[0] == DEPTH
    return _layer(x, c, rel_bias, w_ada[0], b_ada[0], w_in[0], w_gla_gate[0], b_gla_gate[0], gla_norm[0],
                  w_proj_gla[0], w_proj_attn[0], w_out[0], ln1_g[0], ln1_b[0], w_router_group[0],
                  b_router_group[0], w_router_expert[0], b_router_expert[0], w_exp_gate[0], w_exp_up[0],
                  w_exp_down[0], ln2_g[0], ln2_b[0])
```

```python
import functools
import math

import numpy as np
import jax
import jax.numpy as jnp
from jax import lax
from jax.experimental import pallas as pl
from jax.experimental.pallas import tpu as pltpu
from jax.experimental.pallas import tpu_sc as plsc

F32 = jnp.float32
BF16 = jnp.bfloat16

N_MOD = 6
GLA_HEADS = 4
GLA_LOWRANK = 16
GLA_TAU = 16.0
GLA_CHUNK = 64
DIL_PATTERNS = ((128, 1), (512, 4), (2048, 16))
DIL_GROUPS = len(DIL_PATTERNS)
DIL_HEADS_PER_GROUP = 8
DIL_HEAD_DIM = 64
DIL_GROUP_WIDTH = DIL_HEADS_PER_GROUP * DIL_HEAD_DIM
DIL_BLOCK = 128
REL_BUCKETS = 32
REL_MAX_DIST = 2048
MOE_GROUPS = 4
MOE_EXPERTS = 8
MOE_TOTAL = MOE_GROUPS * MOE_EXPERTS
LN_EPS = 1e-5
DEPTH = 1

LANES = 128
VMEM_LIMIT = 56 * 1024 * 1024
NEG = -1e30
ROW_TILE = 512
EXPERT_TILE = 256
GLA_STEP_CHUNKS = 4
ATT_STEP_BLOCKS = 2

HIGHEST = lax.Precision.HIGHEST
NT_DIMS = (((1,), (1,)), ((), ()))
TN_DIMS = (((0,), (0,)), ((), ()))


def _cparams(sem):
    return pltpu.CompilerParams(dimension_semantics=sem, vmem_limit_bytes=VMEM_LIMIT)


def _sigmoid(x):
    return 1.0 / (1.0 + jnp.exp(-x))


def _silu(x):
    return x * _sigmoid(x)


def _layer_norm(x, g, b):
    mu = jnp.mean(x, axis=-1, keepdims=True)
    xc = x - mu
    var = jnp.mean(xc * xc, axis=-1, keepdims=True)
    return xc * lax.rsqrt(var + LN_EPS) * g + b


def _pack_bf16_pairs(x):
    w = x.shape[1] // 2
    lo = lax.bitcast_convert_type(x[:, :w].astype(BF16).astype(F32), jnp.uint32) >> 16
    hi = lax.bitcast_convert_type(x[:, w:].astype(BF16).astype(F32), jnp.uint32) & jnp.uint32(0xFFFF0000)
    return lax.bitcast_convert_type(lo | hi, jnp.int32)


def _unpack_bf16_pairs(p):
    u = lax.bitcast_convert_type(p, jnp.uint32)
    lo = lax.bitcast_convert_type(u << 16, F32)
    hi = lax.bitcast_convert_type(u & jnp.uint32(0xFFFF0000), F32)
    return jnp.concatenate([lo, hi], axis=1)


def _split3(x):
    hi = x.astype(BF16)
    r1 = x - hi.astype(F32)
    mid = r1.astype(BF16)
    lo = (r1 - mid.astype(F32)).astype(BF16)
    return hi, mid, lo


def _mods_kernel(c_ref, w_ref, b_ref, o_ref):
    a = _silu(c_ref[...])
    o_ref[...] = jnp.dot(a, w_ref[...], precision=HIGHEST, preferred_element_type=F32) + b_ref[...]


def _ada_mods(c, w, b):
    bsz, d = c.shape
    n = w.shape[1]
    tn = 1536
    assert n % tn == 0
    return pl.pallas_call(
        _mods_kernel,
        out_shape=jax.ShapeDtypeStruct((bsz, n), F32),
        grid=(n // tn,),
        in_specs=[pl.BlockSpec((bsz, d), lambda j: (0, 0)),
                  pl.BlockSpec((d, tn), lambda j: (0, j)),
                  pl.BlockSpec((1, tn), lambda j: (0, j))],
        out_specs=pl.BlockSpec((bsz, tn), lambda j: (0, j)),
        compiler_params=_cparams(("arbitrary",)),
        name="ada_mods",
    )(c, w, b.reshape(1, n))


def _proj_pieces(d_model):
    dk = d_model // 2
    pieces = [("q_gla", dk, "scale_q_gla"), ("k_gla", dk, None), ("v_gla", d_model, None), ("r_gla", d_model, "silu")]
    for name, post in (("q_att", "scale_q_att"), ("k_att", None), ("v_att", None)):
        for g, (_, dilation) in enumerate(DIL_PATTERNS):
            pieces.append((f"{name}{g}", DIL_GROUP_WIDTH, (post, dilation)))
    pieces += [("g_gla", d_model, "sigmoid"), ("g_att", d_model, "sigmoid"), ("lr", LANES, "lowrank")]
    return tuple(pieces)


def _proj_kernel(pieces, head_k, x_ref, sc_ref, sh_ref, w_ref, *refs):
    out_refs, stage_ref = refs[:-1], refs[-1]
    tm = x_ref.shape[1]
    u = (x_ref[0] * (1.0 + sc_ref[0]) + sh_ref[0]).astype(BF16)
    off = 0
    for (name, width, post), o_ref in zip(pieces, out_refs):
        chunk = min(width, 512)
        for c0 in range(0, width, chunk):
            acc = jnp.dot(u, w_ref[:, off + c0:off + c0 + chunk], preferred_element_type=F32)
            if post == "silu":
                acc = _silu(acc)
            elif post == "sigmoid":
                acc = _sigmoid(acc)
            elif post == "scale_q_gla":
                acc = acc * (head_k ** -0.5)
            if post == "lowrank":
                o_ref[0] = acc[:, :GLA_LOWRANK]
            elif isinstance(post, tuple):
                scale, dilation = post
                if scale is not None:
                    acc = acc * (DIL_HEAD_DIM ** -0.5)
                if dilation == 1:
                    o_ref[0, 0] = acc.astype(o_ref.dtype)
                else:
                    for t in range(width // LANES):
                        stage_ref[t] = acc[:, t * LANES:(t + 1) * LANES]
                    for r in range(dilation):
                        for t in range(width // LANES):
                            o_ref[0, r, :, t * LANES:(t + 1) * LANES] = stage_ref[
                                t, pl.ds(r, tm // dilation, stride=dilation), :].astype(o_ref.dtype)
            else:
                o_ref[0, :, c0:c0 + chunk] = acc.astype(o_ref.dtype)
        off += width


def _in_projection(x, sc1, sh1, w_perm):
    bsz, s, d = x.shape
    pieces = _proj_pieces(d)
    assert sum(p[1] for p in pieces) == w_perm.shape[1]
    tm = min(ROW_TILE, s)
    assert s % tm == 0
    head_k = (d // 2) // GLA_HEADS
    out_shape, out_specs = [], []
    for name, width, post in pieces:
        if post == "lowrank":
            out_shape.append(jax.ShapeDtypeStruct((bsz, s, GLA_LOWRANK), F32))
            out_specs.append(pl.BlockSpec((1, tm, GLA_LOWRANK), lambda b, i: (b, i, 0)))
        elif isinstance(post, tuple):
            dil = post[1]
            assert tm % (dil * 16) == 0
            out_shape.append(jax.ShapeDtypeStruct((bsz, dil, s // dil, width), BF16))
            out_specs.append(pl.BlockSpec((1, dil, tm // dil, width), lambda b, i: (b, 0, i, 0)))
        else:
            out_shape.append(jax.ShapeDtypeStruct((bsz, s, width), BF16))
            out_specs.append(pl.BlockSpec((1, tm, width), lambda b, i: (b, i, 0)))
    outs = pl.pallas_call(
        functools.partial(_proj_kernel, pieces, head_k),
        out_shape=out_shape,
        grid=(bsz, s // tm),
        in_specs=[pl.BlockSpec((1, tm, d), lambda b, i: (b, i, 0)),
                  pl.BlockSpec((1, 1, d), lambda b, i: (b, 0, 0)),
                  pl.BlockSpec((1, 1, d), lambda b, i: (b, 0, 0)),
                  pl.BlockSpec(w_perm.shape, lambda b, i: (0, 0), pipeline_mode=pl.Buffered(1))],
        out_specs=out_specs,
        scratch_shapes=[pltpu.VMEM((DIL_GROUP_WIDTH // LANES, tm, LANES), F32)],
        compiler_params=_cparams(("parallel", "arbitrary")),
        name="in_projection",
    )(x, sc1, sh1, w_perm)
    return dict(zip([p[0] for p in pieces], outs))


def _gla_kernel(n_chunks, head_k, head_v, q_ref, k_ref, v_ref, r_ref, lr_ref, wg_ref, bg_ref, ng_ref, o_ref,
                state_ref):
    @pl.when(pl.program_id(1) == 0)
    def _():
        state_ref[...] = jnp.zeros_like(state_ref)

    c = GLA_CHUNK
    row = lax.broadcasted_iota(jnp.int32, (c, c), 0)
    col = lax.broadcasted_iota(jnp.int32, (c, c), 1)
    causal = row >= col
    tril = causal.astype(BF16)
    mid = c // 2 - 1
    gate_in = jnp.dot(lr_ref[0], wg_ref[...], precision=HIGHEST, preferred_element_type=F32) + bg_ref[...]
    g_all = (jnp.minimum(gate_in, 0.0) - jnp.log(1.0 + jnp.exp(-jnp.abs(gate_in)))) * (1.0 / GLA_TAU)
    g_hi, g_mid, g_lo = _split3(g_all)
    for ci in range(n_chunks):
        rows = slice(ci * c, (ci + 1) * c)
        bc = (jnp.dot(tril, g_hi[rows], preferred_element_type=F32)
              + jnp.dot(tril, g_mid[rows], preferred_element_type=F32)
              + jnp.dot(tril, g_lo[rows], preferred_element_type=F32))
        b_mid = bc[mid:mid + 1, :]
        b_last = bc[c - 1:c, :]
        qf = q_ref[0, rows, :].astype(F32)
        kf = k_ref[0, rows, :].astype(F32)
        q_in = (qf * jnp.exp(bc - b_mid)).astype(BF16)
        k_in = (kf * jnp.exp(b_mid - bc)).astype(BF16)
        q_st = (qf * jnp.exp(bc)).astype(BF16)
        k_st = (kf * jnp.exp(b_last - bc)).astype(BF16)
        dec = jnp.exp(b_last)
        for h in range(GLA_HEADS):
            ks = slice(h * head_k, (h + 1) * head_k)
            vs = slice(h * head_v, (h + 1) * head_v)
            vh = v_ref[0, rows, vs]
            att = lax.dot_general(q_in[:, ks], k_in[:, ks], NT_DIMS, preferred_element_type=F32)
            att = jnp.where(causal, att, 0.0).astype(BF16)
            st = state_ref[h]
            o = jnp.dot(att, vh, preferred_element_type=F32)
            o = o + lax.dot_general(q_st[:, ks], st.astype(BF16), NT_DIMS, preferred_element_type=F32)
            kv_t = lax.dot_general(vh, k_st[:, ks], TN_DIMS, preferred_element_type=F32)
            state_ref[h] = st * dec[:, ks] + kv_t
            ms = jnp.mean(o * o, axis=-1, keepdims=True)
            o = o * lax.rsqrt(ms + LN_EPS) * ng_ref[:, vs] * r_ref[0, rows, vs].astype(F32)
            o_ref[0, rows, vs] = o.astype(o_ref.dtype)


def _gla(q, k, v, r_silu, lr, w_gate, b_gate, norm_g):
    bsz, s, dk = q.shape
    dv = v.shape[-1]
    head_k, head_v = dk // GLA_HEADS, dv // GLA_HEADS
    n_chunks = min(GLA_STEP_CHUNKS, s // GLA_CHUNK)
    ct = GLA_CHUNK * n_chunks
    assert s % ct == 0
    row_spec = lambda w: pl.BlockSpec((1, ct, w), lambda b, i: (b, i, 0))
    full = lambda a: pl.BlockSpec(a.shape, lambda b, i: (0,) * a.ndim)
    bg = b_gate.reshape(1, dk)
    ng = norm_g.reshape(1, dv)
    return pl.pallas_call(
        functools.partial(_gla_kernel, n_chunks, head_k, head_v),
        out_shape=jax.ShapeDtypeStruct((bsz, s, dv), BF16),
        grid=(bsz, s // ct),
        in_specs=[row_spec(dk), row_spec(dk), row_spec(dv), row_spec(dv), row_spec(GLA_LOWRANK),
                  full(w_gate), full(bg), full(ng)],
        out_specs=row_spec(dv),
        scratch_shapes=[pltpu.VMEM((GLA_HEADS, head_v, head_k), F32)],
        compiler_params=_cparams(("parallel", "arbitrary")),
        name="gla",
    )(q, k, v, r_silu, lr, w_gate, bg, ng)


def _t5_bucket_np(dist):
    exact = REL_BUCKETS // 2
    d = np.maximum(dist, 1).astype(np.float32)
    large = exact + (np.log(d / np.float32(exact)) / np.float32(math.log(REL_MAX_DIST / exact))
                     * np.float32(REL_BUCKETS - exact)).astype(np.int32)
    large = np.minimum(large, REL_BUCKETS - 1)
    return np.where(dist < exact, dist, large).astype(np.int32)


def _band_tables(window, dilation):
    qi = np.arange(DIL_BLOCK)[:, None]
    kj = np.arange(2 * DIL_BLOCK)[None, :]
    m = qi + DIL_BLOCK - kj
    n_steps = window // dilation
    band = (m >= 0) & (m <= n_steps)
    bucket = _t5_bucket_np(np.clip(m, 0, n_steps) * dilation)
    return np.where(band, bucket, -1).astype(np.int32)


def _attn_kernel(nq, table_ref, bucket_ref, q_ref, kp_ref, kc_ref, vp_ref, vc_ref, o_ref, lse_ref,
                 bias_ref, s_ref, p_ref):
    i = pl.program_id(1)
    blk = DIL_BLOCK
    hpg = DIL_HEADS_PER_GROUP
    n_pairs = hpg // 2

    @pl.when((pl.program_id(0) == 0) & (i == 0))
    def _():
        bucket = bucket_ref[...]
        for h in range(hpg):
            acc = jnp.full(bucket.shape, NEG, F32)
            for bkt in range(REL_BUCKETS):
                acc = jnp.where(bucket == bkt, table_ref[bkt, h], acc)
            bias_ref[h * blk:(h + 1) * blk, :] = acc

    lane = lax.broadcasted_iota(jnp.int32, (blk, LANES), 1)
    low = lane < DIL_HEAD_DIM
    ones_rhs = jnp.ones((2 * blk, LANES), BF16)

    def windows(ref_p, ref_c, qb, cols):
        if qb == 0:
            return jnp.concatenate([ref_p[0, :, cols], ref_c[0, 0:blk, cols]], axis=0)
        return ref_c[0, (qb - 1) * blk:(qb + 1) * blk, cols]

    for qb in range(nq):
        rows = slice(qb * blk, (qb + 1) * blk)
        for hp in range(n_pairs):
            cols = slice(hp * LANES, (hp + 1) * LANES)
            qp = q_ref[0, rows, cols]
            zero = jnp.zeros_like(qp)
            qq = jnp.concatenate([jnp.where(low, qp, zero), jnp.where(low, zero, qp)], axis=0)
            keys = windows(kp_ref, kc_ref, qb, cols)
            base = (qb * hpg + 2 * hp) * blk
            s_ref[base:base + 2 * blk, :] = lax.dot_general(qq, keys, NT_DIMS, preferred_element_type=F32)

    @pl.when(i == 0)
    def _():
        s_ref[0:hpg * blk, 0:blk] = jnp.full((hpg * blk, blk), NEG, F32)

    mxs = []
    for qb in range(nq):
        rs = slice(qb * hpg * blk, (qb + 1) * hpg * blk)
        s = s_ref[rs, :] + bias_ref[...]
        mx = jnp.max(s, axis=-1, keepdims=True)
        p_ref[rs, :] = jnp.exp(s - mx).astype(BF16)
        mxs.append(mx)

    for qb in range(nq):
        rows = slice(qb * blk, (qb + 1) * blk)
        for hp in range(n_pairs):
            cols = slice(hp * LANES, (hp + 1) * LANES)
            vals = windows(vp_ref, vc_ref, qb, cols)
            rhs = jnp.concatenate([vals, ones_rhs], axis=1)
            base = (qb * hpg + 2 * hp) * blk
            res = jnp.dot(p_ref[base:base + 2 * blk, :], rhs, preferred_element_type=F32)
            num = jnp.where(low, res[0:blk, 0:LANES], res[blk:2 * blk, 0:LANES])
            den = jnp.where(low, res[0:blk, LANES:], res[blk:2 * blk, LANES:])
            off = 2 * hp * blk
            mx = jnp.where(low, mxs[qb][off:off + blk], mxs[qb][off + blk:off + 2 * blk])
            o_ref[0, rows, cols] = (num / den).astype(o_ref.dtype)
            lse_ref[0, rows, cols] = mx + jnp.log(den)


def _dilated_group_attention(q, k, v, table, window, dilation):
    bb, l, w = q.shape
    nq = ATT_STEP_BLOCKS
    assert l % (nq * DIL_BLOCK) == 0
    steps = l // (nq * DIL_BLOCK)
    bucket = jnp.asarray(_band_tables(window, dilation))
    cur = pl.BlockSpec((1, nq * DIL_BLOCK, w), lambda b, i: (b, i, 0))
    prev = pl.BlockSpec((1, DIL_BLOCK, w), lambda b, i: (b, jnp.maximum(nq * i - 1, 0), 0))
    rows_all = nq * DIL_HEADS_PER_GROUP * DIL_BLOCK
    return pl.pallas_call(
        functools.partial(_attn_kernel, nq),
        out_shape=[jax.ShapeDtypeStruct((bb, l, w), BF16), jax.ShapeDtypeStruct((bb, l, w), F32)],
        grid=(bb, steps),
        in_specs=[pl.BlockSpec(memory_space=pltpu.SMEM),
                  pl.BlockSpec(bucket.shape, lambda b, i: (0, 0)),
                  cur, prev, cur, prev, cur],
        out_specs=[cur, cur],
        scratch_shapes=[pltpu.VMEM((DIL_HEADS_PER_GROUP * DIL_BLOCK, 2 * DIL_BLOCK), F32),
                        pltpu.VMEM((rows_all, 2 * DIL_BLOCK), F32),
                        pltpu.VMEM((rows_all, 2 * DIL_BLOCK), BF16)],
        compiler_params=_cparams(("arbitrary", "arbitrary")),
        name=f"dilated_attn_d{dilation}",
    )(table, bucket, q, k, k, v, v)


def _merge_kernel(alpha, dilations, ygla_ref, o0_ref, o1_ref, o2_ref, l0_ref, l1_ref, l2_ref, gg_ref, ga_ref, x_ref,
                  g1_ref, sc2_ref, sh2_ref, ln_g_ref, ln_b_ref, wpg_ref, wpa_ref, wout_ref, wr_ref, br_ref, ltri_ref,
                  x1_ref, u2_ref, route_ref, ew_ref, cnt_ref, stage_ref, carry_ref):
    tm = x_ref.shape[1]

    @pl.when((pl.program_id(0) == 0) & (pl.program_id(1) == 0))
    def _():
        carry_ref[...] = jnp.zeros_like(carry_ref)

    def natural(ref, dilation, slot):
        if dilation == 1:
            return ref[0, 0].astype(F32)
        n_lt = DIL_GROUP_WIDTH // LANES
        for r in range(dilation):
            for t in range(n_lt):
                stage_ref[slot, t, pl.ds(r, tm // dilation, stride=dilation), :] = ref[
                    0, r, :, t * LANES:(t + 1) * LANES].astype(F32)
        return jnp.concatenate([stage_ref[slot, t] for t in range(n_lt)], axis=1)

    lses = [natural(ref, dil, 0 + 2 * gi) for gi, (ref, dil) in enumerate(zip((l0_ref, l1_ref, l2_ref), dilations))]
    outs = [natural(ref, dil, 1 + 2 * gi) for gi, (ref, dil) in enumerate(zip((o0_ref, o1_ref, o2_ref), dilations))]
    lm = jnp.maximum(jnp.maximum(lses[0], lses[1]), lses[2])
    es = [jnp.exp(l - lm) for l in lses]
    y_att = (es[0] * outs[0] + es[1] * outs[1] + es[2] * outs[2]) / (es[0] + es[1] + es[2])

    p_gla = jnp.dot(ygla_ref[0], wpg_ref[...], preferred_element_type=F32)
    p_att = jnp.dot(y_att.astype(BF16), wpa_ref[...], preferred_element_type=F32)
    merged = gg_ref[0].astype(F32) * p_gla + ga_ref[0].astype(F32) * p_att
    y = jnp.dot(merged.astype(BF16), wout_ref[...], preferred_element_type=F32)
    x1 = _layer_norm(alpha * x_ref[0] + g1_ref[0] * y, ln_g_ref[...], ln_b_ref[...])
    x1_ref[0] = x1
    u2 = x1 * (1.0 + sc2_ref[0]) + sh2_ref[0]
    u2_ref[0] = _pack_bf16_pairs(u2)

    u_hi = u2.astype(BF16)
    u_lo = (u2 - u_hi.astype(F32)).astype(BF16)
    wr = wr_ref[...]
    w_hi = wr.astype(BF16)
    w_lo = (wr - w_hi.astype(F32)).astype(BF16)
    logits = (jnp.dot(u_hi, w_hi, preferred_element_type=F32) + jnp.dot(u_lo, w_hi, preferred_element_type=F32)
              + jnp.dot(u_hi, w_lo, preferred_element_type=F32)) + br_ref[...]
    lane = lax.broadcasted_iota(jnp.int32, logits.shape, 1)
    big = jnp.int32(LANES)
    lg = jnp.where(lane < MOE_GROUPS, logits, NEG)
    gmax = jnp.max(lg, axis=-1, keepdims=True)
    gidx = jnp.min(jnp.where(lg == gmax, lane, big), axis=-1, keepdims=True)
    gval = 1.0 / jnp.sum(jnp.exp(lg - gmax), axis=-1, keepdims=True)
    in_group = (lane >= MOE_GROUPS + gidx * MOE_EXPERTS) & (lane < MOE_GROUPS + (gidx + 1) * MOE_EXPERTS)
    le = jnp.where(in_group, logits, NEG)
    m1 = jnp.max(le, axis=-1, keepdims=True)
    i1 = jnp.min(jnp.where(le == m1, lane, big), axis=-1, keepdims=True)
    le2 = jnp.where(lane == i1, NEG, le)
    m2 = jnp.max(le2, axis=-1, keepdims=True)
    i2 = jnp.min(jnp.where(le2 == m2, lane, big), axis=-1, keepdims=True)
    t = jnp.exp(m2 - m1)
    w1 = 1.0 / (1.0 + t)
    w2 = t * w1

    hit1, hit2 = lane == i1, lane == i2
    onehot = jnp.where(hit1 | hit2, 1.0, 0.0)
    earlier = jnp.dot(ltri_ref[...], onehot.astype(BF16), preferred_element_type=F32) + carry_ref[...]
    rank1 = jnp.sum(jnp.where(hit1, earlier, 0.0), axis=-1, keepdims=True).astype(jnp.int32)
    rank2 = jnp.sum(jnp.where(hit2, earlier, 0.0), axis=-1, keepdims=True).astype(jnp.int32)
    carry = carry_ref[...] + jnp.sum(onehot, axis=0, keepdims=True)
    carry_ref[...] = carry
    cnt_ref[...] = carry.astype(jnp.int32)
    route_ref[0] = jnp.where(lane == 0, i1 - MOE_GROUPS, jnp.where(lane == 1, i2 - MOE_GROUPS,
                             jnp.where(lane == 2, rank1, jnp.where(lane == 3, rank2, 0))))
    ew_ref[0] = jnp.where(lane == 0, gval * w1, jnp.where(lane == 1, gval * w2, 0.0))


def _merge(alpha, y_gla, o_groups, lse_groups, g_gla, g_att, x, g1, sc2, sh2, ln_g, ln_b, wpg, wpa, wout, wr, br):
    bsz, s, d = x.shape
    tm = min(ROW_TILE, s)
    assert s % tm == 0
    dilations = tuple(dil for _, dil in DIL_PATTERNS)
    row = lambda w: pl.BlockSpec((1, tm, w), lambda b, i: (b, i, 0))
    sub = lambda dil: pl.BlockSpec((1, dil, tm // dil, DIL_GROUP_WIDTH), lambda b, i: (b, 0, i, 0))
    per_b = pl.BlockSpec((1, 1, d), lambda b, i: (b, 0, 0))
    full = lambda a: pl.BlockSpec(a.shape, lambda b, i: (0,) * a.ndim)
    ln_g2, ln_b2 = ln_g.reshape(1, d), ln_b.reshape(1, d)
    ltri = jnp.asarray(np.tril(np.ones((tm, tm), np.float32), -1), BF16)
    return pl.pallas_call(
        functools.partial(_merge_kernel, alpha, dilations),
        out_shape=[jax.ShapeDtypeStruct((bsz, s, d), F32), jax.ShapeDtypeStruct((bsz, s, d // 2), jnp.int32),
                   jax.ShapeDtypeStruct((bsz, s, LANES), jnp.int32), jax.ShapeDtypeStruct((bsz, s, LANES), F32),
                   jax.ShapeDtypeStruct((1, LANES), jnp.int32)],
        grid=(bsz, s // tm),
        in_specs=[row(y_gla.shape[-1])] + [sub(dil) for dil in dilations] * 2
                 + [row(d), row(d), row(d), per_b, per_b, per_b, full(ln_g2), full(ln_b2),
                    full(wpg), full(wpa), full(wout), full(wr), full(br), full(ltri)],
        out_specs=[row(d), row(d // 2), row(LANES), row(LANES), pl.BlockSpec((1, LANES), lambda b, i: (0, 0))],
        scratch_shapes=[pltpu.VMEM((2 * DIL_GROUPS, DIL_GROUP_WIDTH // LANES, tm, LANES), F32),
                        pltpu.VMEM((1, LANES), F32)],
        compiler_params=_cparams(("arbitrary", "arbitrary")),
        name="merge_ln1_router",
    )(y_gla, *o_groups, *lse_groups, g_gla, g_att, x, g1, sc2, sh2, ln_g2, ln_b2, wpg, wpa, wout, wr, br, ltri)


def _expert_kernel(te_ref, nt_ref, x_ref, wg_ref, wu_ref, wd_ref, o_ref, wg_s, wu_s, wd_s):
    t = pl.program_id(0)
    prev_e = te_ref[jnp.maximum(t - 1, 0)]

    @pl.when((t < nt_ref[0]) & ((t == 0) | (te_ref[t] != prev_e)))
    def _():
        wg_s[...] = wg_ref[0].astype(BF16)
        wu_s[...] = wu_ref[0].astype(BF16)
        wd_s[...] = wd_ref[0].astype(BF16)

    @pl.when(t < nt_ref[0])
    def _():
        xt = _unpack_bf16_pairs(x_ref[...]).astype(BF16)
        hg = jnp.dot(xt, wg_s[...], preferred_element_type=F32)
        hu = jnp.dot(xt, wu_s[...], preferred_element_type=F32)
        h = (_silu(hg) * hu).astype(BF16)
        o_ref[...] = _pack_bf16_pairs(jnp.dot(h, wd_s[...], preferred_element_type=F32))

    @pl.when(t >= nt_ref[0])
    def _():
        o_ref[...] = jnp.zeros_like(o_ref)


def _expert_ffn(tile_expert, n_tiles_used, xg, w_gate, w_up, w_down):
    p = xg.shape[0]
    ne, d, ff = w_gate.shape
    tm = EXPERT_TILE
    n_tiles = p // tm
    grid_spec = pltpu.PrefetchScalarGridSpec(
        num_scalar_prefetch=2,
        grid=(n_tiles,),
        in_specs=[pl.BlockSpec((tm, d // 2), lambda t, te, nt: (t, 0)),
                  pl.BlockSpec((1, d, ff), lambda t, te, nt: (te[t], 0, 0)),
                  pl.BlockSpec((1, d, ff), lambda t, te, nt: (te[t], 0, 0)),
                  pl.BlockSpec((1, ff, d), lambda t, te, nt: (te[t], 0, 0))],
        out_specs=pl.BlockSpec((tm, d // 2), lambda t, te, nt: (t, 0)),
        scratch_shapes=[pltpu.VMEM((d, ff), BF16), pltpu.VMEM((d, ff), BF16), pltpu.VMEM((ff, d), BF16)],
    )
    return pl.pallas_call(
        _expert_kernel,
        out_shape=jax.ShapeDtypeStruct((p, d // 2), jnp.int32),
        grid_spec=grid_spec,
        compiler_params=_cparams(("arbitrary",)),
        name="expert_ffn",
    )(tile_expert, n_tiles_used, xg, w_gate, w_up, w_down)


def _final_kernel(alpha, x1_ref, ya_ref, yb_ref, ew_ref, g2_ref, ln_g_ref, ln_b_ref, o_ref):
    ew = ew_ref[0]
    y = ew[:, 0:1] * _unpack_bf16_pairs(ya_ref[0]) + ew[:, 1:2] * _unpack_bf16_pairs(yb_ref[0])
    o_ref[0] = _layer_norm(alpha * x1_ref[0] + g2_ref[0] * y, ln_g_ref[...], ln_b_ref[...])


def _final(alpha, x1, ya, yb, ew, g2, ln_g, ln_b):
    bsz, s, d = x1.shape
    tm = min(ROW_TILE, s)
    row = lambda w: pl.BlockSpec((1, tm, w), lambda b, i: (b, i, 0))
    full = lambda a: pl.BlockSpec(a.shape, lambda b, i: (0,) * a.ndim)
    ln_g2, ln_b2 = ln_g.reshape(1, d), ln_b.reshape(1, d)
    return pl.pallas_call(
        functools.partial(_final_kernel, alpha),
        out_shape=jax.ShapeDtypeStruct((bsz, s, d), F32),
        grid=(bsz, s // tm),
        in_specs=[row(d), row(d // 2), row(d // 2), row(LANES), pl.BlockSpec((1, 1, d), lambda b, i: (b, 0, 0)),
                  full(ln_g2), full(ln_b2)],
        out_specs=row(d),
        compiler_params=_cparams(("parallel", "arbitrary")),
        name="combine_ln2",
    )(x1, ya, yb, ew, g2, ln_g2, ln_b2)


SC_CORES = 2
SC_SUBCORES = 16
SC_CHUNK = 64


def _sc_mesh():
    return plsc.VectorSubcoreMesh(core_axis_name="c", subcore_axis_name="s")


def _sc_scatter_rows(rows, dest, n_rows):
    n, w = rows.shape
    n_workers = SC_CORES * SC_SUBCORES
    assert n % (n_workers * SC_CHUNK) == 0
    n_chunks = n // (n_workers * SC_CHUNK)
    d0 = dest[:, 0].reshape(n // SC_CHUNK, 1, SC_CHUNK)
    d1 = dest[:, 1].reshape(n // SC_CHUNK, 1, SC_CHUNK)

    @functools.partial(
        pl.kernel, mesh=_sc_mesh(), out_type=jax.ShapeDtypeStruct((n_rows, w), rows.dtype),
        scratch_types=[pltpu.VMEM((1, SC_CHUNK), jnp.int32), pltpu.VMEM((1, SC_CHUNK), jnp.int32),
                       pltpu.VMEM((SC_CHUNK, w), rows.dtype)])
    def scatter_kernel(rows_hbm, d0_hbm, d1_hbm, out_hbm, i0_v, i1_v, rows_v):
        wid = lax.axis_index("s") * SC_CORES + lax.axis_index("c")

        @pl.loop(0, n_chunks)
        def _(j):
            c = wid * n_chunks + j
            pltpu.sync_copy(rows_hbm.at[pl.ds(c * SC_CHUNK, SC_CHUNK)], rows_v)
            pltpu.sync_copy(d0_hbm.at[c], i0_v)
            pltpu.sync_copy(d1_hbm.at[c], i1_v)
            pltpu.sync_copy(rows_v, out_hbm.at[i0_v.at[0]])
            pltpu.sync_copy(rows_v, out_hbm.at[i1_v.at[0]])

    return scatter_kernel(rows, d0, d1)


def _sc_gather_rows(table, dest):
    n = dest.shape[0]
    w = table.shape[1]
    n_workers = SC_CORES * SC_SUBCORES
    assert n % (n_workers * SC_CHUNK) == 0
    n_chunks = n // (n_workers * SC_CHUNK)
    d0 = dest[:, 0].reshape(n // SC_CHUNK, 1, SC_CHUNK)
    d1 = dest[:, 1].reshape(n // SC_CHUNK, 1, SC_CHUNK)
    out = jax.ShapeDtypeStruct((n, w), table.dtype)

    @functools.partial(
        pl.kernel, mesh=_sc_mesh(), out_type=(out, out),
        scratch_types=[pltpu.VMEM((1, SC_CHUNK), jnp.int32), pltpu.VMEM((SC_CHUNK, w), table.dtype)])
    def gather_kernel(table_hbm, d0_hbm, d1_hbm, a_hbm, b_hbm, i_v, rows_v):
        wid = lax.axis_index("s") * SC_CORES + lax.axis_index("c")

        @pl.loop(0, n_chunks)
        def _(j):
            c = wid * n_chunks + j
            for d_hbm, o_hbm in ((d0_hbm, a_hbm), (d1_hbm, b_hbm)):
                pltpu.sync_copy(d_hbm.at[c], i_v)
                pltpu.sync_copy(table_hbm.at[i_v.at[0]], rows_v)
                pltpu.sync_copy(rows_v, o_hbm.at[pl.ds(c * SC_CHUNK, SC_CHUNK)])

    return gather_kernel(table, d0, d1)


def _dispatch_plan(eid, rank, counts):
    tm = EXPERT_TILE
    n_assign = eid.size
    tiles_per = (counts + tm - 1) // tm
    tile_end = jnp.cumsum(tiles_per)
    pad_start = ((tile_end - tiles_per) * tm).astype(jnp.int32)
    dest = pad_start[eid] + rank
    n_tiles = (n_assign + MOE_TOTAL * tm) // tm
    tile_expert = jnp.minimum(jnp.sum(tile_end[None, :] <= jnp.arange(n_tiles)[:, None], axis=1),
                              MOE_TOTAL - 1).astype(jnp.int32)
    return dest, tile_expert, tile_end[-1:].astype(jnp.int32), n_tiles * tm


def _layer(x, c, rel_bias, w_ada, b_ada, w_in, w_gla_gate, b_gla_gate, gla_norm, w_proj_gla, w_proj_attn, w_out,
           ln1_g, ln1_b, w_rg, b_rg, w_re, b_re, w_eg, w_eu, w_ed, ln2_g, ln2_b):
    bsz, s, d = x.shape
    alpha = (2.0 * DEPTH) ** 0.25
    mods = _ada_mods(c, w_ada, b_ada)
    sh1, sc1, g1, sh2, sc2, g2 = [m.reshape(bsz, 1, d) for m in jnp.split(mods, N_MOD, axis=-1)]

    lr0 = d // 2 * 2 + 2 * d
    w_perm = jnp.concatenate([w_in[:, :lr0], w_in[:, lr0 + GLA_LOWRANK:], w_in[:, lr0:lr0 + GLA_LOWRANK],
                              jnp.zeros((d, LANES - GLA_LOWRANK), w_in.dtype)], axis=1).astype(BF16)
    z = _in_projection(x, sc1, sh1, w_perm)

    y_gla = _gla(z["q_gla"], z["k_gla"], z["v_gla"], z["r_gla"], z["lr"], w_gla_gate, b_gla_gate, gla_norm)

    o_groups, lse_groups = [], []
    for g, (window, dilation) in enumerate(DIL_PATTERNS):
        l = s // dilation
        qg, kg, vg = (z[f"{n}{g}"].reshape(bsz * dilation, l, DIL_GROUP_WIDTH) for n in ("q_att", "k_att", "v_att"))
        table = rel_bias[:, g * DIL_HEADS_PER_GROUP:(g + 1) * DIL_HEADS_PER_GROUP]
        o, lse = _dilated_group_attention(qg, kg, vg, table, window, dilation)
        o_groups.append(o.reshape(bsz, dilation, l, DIL_GROUP_WIDTH))
        lse_groups.append(lse.reshape(bsz, dilation, l, DIL_GROUP_WIDTH))

    wr = jnp.concatenate([w_rg, w_re, jnp.zeros((d, LANES - MOE_GROUPS - MOE_TOTAL), F32)], axis=1)
    br = jnp.concatenate([b_rg, b_re, jnp.zeros((LANES - MOE_GROUPS - MOE_TOTAL,), F32)]).reshape(1, LANES)
    x1, u2, route, ew, cnt = _merge(alpha, y_gla, o_groups, lse_groups, z["g_gla"], z["g_att"], x, g1, sc2, sh2,
                                    ln1_g, ln1_b, w_proj_gla.astype(BF16), w_proj_attn.astype(BF16),
                                    w_out.astype(BF16), wr, br)

    n = bsz * s
    route = route.reshape(n, LANES)
    counts = cnt[0, MOE_GROUPS:MOE_GROUPS + MOE_TOTAL]
    dest, tile_expert, n_used, n_rows = _dispatch_plan(route[:, 0:2], route[:, 2:4], counts)
    xg = _sc_scatter_rows(u2.reshape(n, d // 2), dest, n_rows)
    ff = w_eg.shape[-1]
    yo = _expert_ffn(tile_expert, n_used, xg, w_eg.reshape(MOE_TOTAL, d, ff), w_eu.reshape(MOE_TOTAL, d, ff),
                     w_ed.reshape(MOE_TOTAL, ff, d))
    ya, yb = (y.reshape(bsz, s, d // 2) for y in _sc_gather_rows(yo, dest))
    return _final(alpha, x1, ya, yb, ew, g2, ln2_g, ln2_b)


def kernel(x, c, rel_bias, w_ada, b_ada, w_in, w_gla_gate, b_gla_gate, gla_norm, w_proj_gla, w_proj_attn, w_out,
           ln1_g, ln1_b, w_router_group, b_router_group, w_router_expert, b_router_expert, w_exp_gate, w_exp_up,
           w_exp_down, ln2_g, ln2_b):
    assert w_ada.shape[0] == DEPTH
    return _layer(x, c, rel_bias, w_ada[0], b_ada[0], w_in[0], w_gla_gate[0], b_gla_gate[0], gla_norm[0],
                  w_proj_gla[0], w_proj_attn[0], w_out[0], ln1_g[0], ln1_b[0], w_router_group[0],
                  b_router_group[0], w_router_expert[0], b_router_expert[0], w_exp_gate[0], w_exp_up[0],
                  w_exp_down[0], ln2_g[0], ln2_b[0])
```

```python
import functools
import math

import numpy as np
import jax
import jax.numpy as jnp
from jax import lax
from jax.experimental import pallas as pl
from jax.experimental.pallas import tpu as pltpu
from jax.experimental.pallas import tpu_sc as plsc

F32 = jnp.float32
BF16 = jnp.bfloat16

N_MOD = 6
GLA_HEADS = 4
GLA_LOWRANK = 16
GLA_TAU = 16.0
GLA_CHUNK = 64
DIL_PATTERNS = ((128, 1), (512, 4), (2048, 16))
DIL_GROUPS = len(DIL_PATTERNS)
DIL_HEADS_PER_GROUP = 8
DIL_HEAD_DIM = 64
DIL_GROUP_WIDTH = DIL_HEADS_PER_GROUP * DIL_HEAD_DIM
DIL_BLOCK = 128
REL_BUCKETS = 32
REL_MAX_DIST = 2048
MOE_GROUPS = 4
MOE_EXPERTS = 8
MOE_TOTAL = MOE_GROUPS * MOE_EXPERTS
LN_EPS = 1e-5
DEPTH = 1

LANES = 128
VMEM_LIMIT = 56 * 1024 * 1024
NEG = -1e30
ROW_TILE = 512
EXPERT_TILE = 256
GLA_STEP_CHUNKS = 4
ATT_STEP_BLOCKS = 2
ROUTE_ROWS = 8

HIGHEST = lax.Precision.HIGHEST
NT_DIMS = (((1,), (1,)), ((), ()))
TN_DIMS = (((0,), (0,)), ((), ()))


def _cparams(sem):
    return pltpu.CompilerParams(dimension_semantics=sem, vmem_limit_bytes=VMEM_LIMIT)


def _sigmoid(x):
    return 1.0 / (1.0 + jnp.exp(-x))


def _silu(x):
    return x * _sigmoid(x)


def _layer_norm(x, g, b):
    mu = jnp.mean(x, axis=-1, keepdims=True)
    xc = x - mu
    var = jnp.mean(xc * xc, axis=-1, keepdims=True)
    return xc * lax.rsqrt(var + LN_EPS) * g + b


def _pack_bf16_pairs(x):
    w = x.shape[1] // 2
    lo = lax.bitcast_convert_type(x[:, :w].astype(BF16).astype(F32), jnp.uint32) >> 16
    hi = lax.bitcast_convert_type(x[:, w:].astype(BF16).astype(F32), jnp.uint32) & jnp.uint32(0xFFFF0000)
    return lax.bitcast_convert_type(lo | hi, jnp.int32)


def _unpack_bf16_pairs(p):
    u = lax.bitcast_convert_type(p, jnp.uint32)
    lo = lax.bitcast_convert_type(u << 16, F32)
    hi = lax.bitcast_convert_type(u & jnp.uint32(0xFFFF0000), F32)
    return jnp.concatenate([lo, hi], axis=1)


def _split3(x):
    hi = x.astype(BF16)
    r1 = x - hi.astype(F32)
    mid = r1.astype(BF16)
    lo = (r1 - mid.astype(F32)).astype(BF16)
    return hi, mid, lo


def _mods_kernel(c_ref, w_ref, b_ref, o_ref):
    a = _silu(c_ref[...])
    o_ref[...] = jnp.dot(a, w_ref[...], precision=HIGHEST, preferred_element_type=F32) + b_ref[...]


def _ada_mods(c, w, b):
    bsz, d = c.shape
    n = w.shape[1]
    tn = 1536
    assert n % tn == 0
    return pl.pallas_call(
        _mods_kernel,
        out_shape=jax.ShapeDtypeStruct((bsz, n), F32),
        grid=(n // tn,),
        in_specs=[pl.BlockSpec((bsz, d), lambda j: (0, 0)),
                  pl.BlockSpec((d, tn), lambda j: (0, j)),
                  pl.BlockSpec((1, tn), lambda j: (0, j))],
        out_specs=pl.BlockSpec((bsz, tn), lambda j: (0, j)),
        compiler_params=_cparams(("arbitrary",)),
        name="ada_mods",
    )(c, w, b.reshape(1, n))


def _proj_pieces(d_model):
    dk = d_model // 2
    pieces = [("q_gla", dk, "scale_q_gla"), ("k_gla", dk, None), ("v_gla", d_model, None), ("r_gla", d_model, "silu")]
    for name, post in (("q_att", "scale_q_att"), ("k_att", None), ("v_att", None)):
        for g, (_, dilation) in enumerate(DIL_PATTERNS):
            pieces.append((f"{name}{g}", DIL_GROUP_WIDTH, (post, dilation)))
    pieces += [("g_gla", d_model, "sigmoid"), ("g_att", d_model, "sigmoid"), ("lr", LANES, "lowrank")]
    return tuple(pieces)


def _wprep_kernel(lr0, w_ref, o_ref):
    main = w_ref.shape[1] - GLA_LOWRANK
    step = 512
    for c0 in range(0, lr0, step):
        o_ref[:, c0:c0 + step] = w_ref[:, c0:c0 + step].astype(BF16)
    for c0 in range(lr0, main, step):
        o_ref[:, c0:c0 + step] = w_ref[:, c0 + GLA_LOWRANK:c0 + GLA_LOWRANK + step].astype(BF16)
    tail = w_ref[:, lr0:lr0 + LANES]
    lane = lax.broadcasted_iota(jnp.int32, tail.shape, 1)
    o_ref[:, main:main + LANES] = jnp.where(lane < GLA_LOWRANK, tail, 0.0).astype(BF16)


def _prep_in_weight(w_in, lr0):
    d, n_in = w_in.shape
    n_out = n_in - GLA_LOWRANK + LANES
    assert lr0 % 512 == 0 and (n_in - GLA_LOWRANK - lr0) % 512 == 0
    tr = 128
    return pl.pallas_call(
        functools.partial(_wprep_kernel, lr0),
        out_shape=jax.ShapeDtypeStruct((d, n_out), BF16),
        grid=(d // tr,),
        in_specs=[pl.BlockSpec((tr, n_in), lambda i: (i, 0))],
        out_specs=pl.BlockSpec((tr, n_out), lambda i: (i, 0)),
        compiler_params=_cparams(("parallel",)),
        name="prep_in_weight",
    )(w_in)


def _proj_kernel(pieces, head_k, x_ref, sc_ref, sh_ref, w_ref, *refs):
    out_refs, stage_ref = refs[:-1], refs[-1]
    tm = x_ref.shape[1]
    u = (x_ref[0] * (1.0 + sc_ref[0]) + sh_ref[0]).astype(BF16)
    off = 0
    for (name, width, post), o_ref in zip(pieces, out_refs):
        chunk = min(width, 512)
        for c0 in range(0, width, chunk):
            acc = jnp.dot(u, w_ref[:, off + c0:off + c0 + chunk], preferred_element_type=F32)
            if post == "silu":
                acc = _silu(acc)
            elif post == "sigmoid":
                acc = _sigmoid(acc)
            elif post == "scale_q_gla":
                acc = acc * (head_k ** -0.5)
            if post == "lowrank":
                o_ref[0] = acc[:, :GLA_LOWRANK]
            elif isinstance(post, tuple):
                scale, dilation = post
                if scale is not None:
                    acc = acc * (DIL_HEAD_DIM ** -0.5)
                if dilation == 1:
                    o_ref[0, 0] = acc.astype(o_ref.dtype)
                else:
                    for t in range(width // LANES):
                        stage_ref[t] = acc[:, t * LANES:(t + 1) * LANES]
                    for r in range(dilation):
                        for t in range(width // LANES):
                            o_ref[0, r, :, t * LANES:(t + 1) * LANES] = stage_ref[
                                t, pl.ds(r, tm // dilation, stride=dilation), :].astype(o_ref.dtype)
            else:
                o_ref[0, :, c0:c0 + chunk] = acc.astype(o_ref.dtype)
        off += width


def _in_projection(x, sc1, sh1, w_perm):
    bsz, s, d = x.shape
    pieces = _proj_pieces(d)
    assert sum(p[1] for p in pieces) == w_perm.shape[1]
    tm = min(ROW_TILE, s)
    assert s % tm == 0
    head_k = (d // 2) // GLA_HEADS
    out_shape, out_specs = [], []
    for name, width, post in pieces:
        if post == "lowrank":
            out_shape.append(jax.ShapeDtypeStruct((bsz, s, GLA_LOWRANK), F32))
            out_specs.append(pl.BlockSpec((1, tm, GLA_LOWRANK), lambda b, i: (b, i, 0)))
        elif isinstance(post, tuple):
            dil = post[1]
            assert tm % (dil * 16) == 0
            out_shape.append(jax.ShapeDtypeStruct((bsz, dil, s // dil, width), BF16))
            out_specs.append(pl.BlockSpec((1, dil, tm // dil, width), lambda b, i: (b, 0, i, 0)))
        else:
            out_shape.append(jax.ShapeDtypeStruct((bsz, s, width), BF16))
            out_specs.append(pl.BlockSpec((1, tm, width), lambda b, i: (b, i, 0)))
    outs = pl.pallas_call(
        functools.partial(_proj_kernel, pieces, head_k),
        out_shape=out_shape,
        grid=(bsz, s // tm),
        in_specs=[pl.BlockSpec((1, tm, d), lambda b, i: (b, i, 0)),
                  pl.BlockSpec((1, 1, d), lambda b, i: (b, 0, 0)),
                  pl.BlockSpec((1, 1, d), lambda b, i: (b, 0, 0)),
                  pl.BlockSpec(w_perm.shape, lambda b, i: (0, 0), pipeline_mode=pl.Buffered(1))],
        out_specs=out_specs,
        scratch_shapes=[pltpu.VMEM((DIL_GROUP_WIDTH // LANES, tm, LANES), F32)],
        compiler_params=_cparams(("parallel", "arbitrary")),
        name="in_projection",
    )(x, sc1, sh1, w_perm)
    return dict(zip([p[0] for p in pieces], outs))


def _gla_kernel(n_chunks, head_k, head_v, q_ref, k_ref, v_ref, r_ref, lr_ref, wg_ref, bg_ref, ng_ref, o_ref,
                state_ref):
    @pl.when(pl.program_id(1) == 0)
    def _():
        state_ref[...] = jnp.zeros_like(state_ref)

    c = GLA_CHUNK
    row = lax.broadcasted_iota(jnp.int32, (c, c), 0)
    col = lax.broadcasted_iota(jnp.int32, (c, c), 1)
    causal = row >= col
    tril = causal.astype(BF16)
    mid = c // 2 - 1
    gate_in = jnp.dot(lr_ref[0], wg_ref[...], precision=HIGHEST, preferred_element_type=F32) + bg_ref[...]
    g_all = (jnp.minimum(gate_in, 0.0) - jnp.log(1.0 + jnp.exp(-jnp.abs(gate_in)))) * (1.0 / GLA_TAU)
    g_hi, g_mid, g_lo = _split3(g_all)
    for ci in range(n_chunks):
        rows = slice(ci * c, (ci + 1) * c)
        bc = (jnp.dot(tril, g_hi[rows], preferred_element_type=F32)
              + jnp.dot(tril, g_mid[rows], preferred_element_type=F32)
              + jnp.dot(tril, g_lo[rows], preferred_element_type=F32))
        b_mid = bc[mid:mid + 1, :]
        b_last = bc[c - 1:c, :]
        qf = q_ref[0, rows, :].astype(F32)
        kf = k_ref[0, rows, :].astype(F32)
        q_in = (qf * jnp.exp(bc - b_mid)).astype(BF16)
        k_in = (kf * jnp.exp(b_mid - bc)).astype(BF16)
        q_st = (qf * jnp.exp(bc)).astype(BF16)
        k_st = (kf * jnp.exp(b_last - bc)).astype(BF16)
        dec = jnp.exp(b_last)
        for h in range(GLA_HEADS):
            ks = slice(h * head_k, (h + 1) * head_k)
            vs = slice(h * head_v, (h + 1) * head_v)
            vh = v_ref[0, rows, vs]
            att = lax.dot_general(q_in[:, ks], k_in[:, ks], NT_DIMS, preferred_element_type=F32)
            att = jnp.where(causal, att, 0.0).astype(BF16)
            st = state_ref[h]
            o = jnp.dot(att, vh, preferred_element_type=F32)
            o = o + lax.dot_general(q_st[:, ks], st.astype(BF16), NT_DIMS, preferred_element_type=F32)
            kv_t = lax.dot_general(vh, k_st[:, ks], TN_DIMS, preferred_element_type=F32)
            state_ref[h] = st * dec[:, ks] + kv_t
            ms = jnp.mean(o * o, axis=-1, keepdims=True)
            o = o * lax.rsqrt(ms + LN_EPS) * ng_ref[:, vs] * r_ref[0, rows, vs].astype(F32)
            o_ref[0, rows, vs] = o.astype(o_ref.dtype)


def _gla(q, k, v, r_silu, lr, w_gate, b_gate, norm_g):
    bsz, s, dk = q.shape
    dv = v.shape[-1]
    head_k, head_v = dk // GLA_HEADS, dv // GLA_HEADS
    n_chunks = min(GLA_STEP_CHUNKS, s // GLA_CHUNK)
    ct = GLA_CHUNK * n_chunks
    assert s % ct == 0
    row_spec = lambda w: pl.BlockSpec((1, ct, w), lambda b, i: (b, i, 0))
    full = lambda a: pl.BlockSpec(a.shape, lambda b, i: (0,) * a.ndim)
    bg = b_gate.reshape(1, dk)
    ng = norm_g.reshape(1, dv)
    return pl.pallas_call(
        functools.partial(_gla_kernel, n_chunks, head_k, head_v),
        out_shape=jax.ShapeDtypeStruct((bsz, s, dv), BF16),
        grid=(bsz, s // ct),
        in_specs=[row_spec(dk), row_spec(dk), row_spec(dv), row_spec(dv), row_spec(GLA_LOWRANK),
                  full(w_gate), full(bg), full(ng)],
        out_specs=row_spec(dv),
        scratch_shapes=[pltpu.VMEM((GLA_HEADS, head_v, head_k), F32)],
        compiler_params=_cparams(("parallel", "arbitrary")),
        name="gla",
    )(q, k, v, r_silu, lr, w_gate, bg, ng)


def _t5_bucket_np(dist):
    exact = REL_BUCKETS // 2
    d = np.maximum(dist, 1).astype(np.float32)
    large = exact + (np.log(d / np.float32(exact)) / np.float32(math.log(REL_MAX_DIST / exact))
                     * np.float32(REL_BUCKETS - exact)).astype(np.int32)
    large = np.minimum(large, REL_BUCKETS - 1)
    return np.where(dist < exact, dist, large).astype(np.int32)


def _band_tables(window, dilation):
    qi = np.arange(DIL_BLOCK)[:, None]
    kj = np.arange(2 * DIL_BLOCK)[None, :]
    m = qi + DIL_BLOCK - kj
    n_steps = window // dilation
    band = (m >= 0) & (m <= n_steps)
    bucket = _t5_bucket_np(np.clip(m, 0, n_steps) * dilation)
    return np.where(band, bucket, -1).astype(np.int32)


def _attn_kernel(nq, table_ref, bucket_ref, q_ref, kp_ref, kc_ref, vp_ref, vc_ref, o_ref, lse_ref,
                 bias_ref, s_ref, p_ref):
    i = pl.program_id(1)
    blk = DIL_BLOCK
    hpg = DIL_HEADS_PER_GROUP
    n_pairs = hpg // 2

    @pl.when((pl.program_id(0) == 0) & (i == 0))
    def _():
        bucket = bucket_ref[...]
        for h in range(hpg):
            acc = jnp.full(bucket.shape, NEG, F32)
            for bkt in range(REL_BUCKETS):
                acc = jnp.where(bucket == bkt, table_ref[bkt, h], acc)
            bias_ref[h * blk:(h + 1) * blk, :] = acc

    lane = lax.broadcasted_iota(jnp.int32, (blk, LANES), 1)
    low = lane < DIL_HEAD_DIM
    ones_rhs = jnp.ones((2 * blk, LANES), BF16)

    def windows(ref_p, ref_c, qb, cols):
        if qb == 0:
            return jnp.concatenate([ref_p[0, :, cols], ref_c[0, 0:blk, cols]], axis=0)
        return ref_c[0, (qb - 1) * blk:(qb + 1) * blk, cols]

    for qb in range(nq):
        rows = slice(qb * blk, (qb + 1) * blk)
        for hp in range(n_pairs):
            cols = slice(hp * LANES, (hp + 1) * LANES)
            qp = q_ref[0, rows, cols]
            zero = jnp.zeros_like(qp)
            qq = jnp.concatenate([jnp.where(low, qp, zero), jnp.where(low, zero, qp)], axis=0)
            keys = windows(kp_ref, kc_ref, qb, cols)
            base = (qb * hpg + 2 * hp) * blk
            s_ref[base:base + 2 * blk, :] = lax.dot_general(qq, keys, NT_DIMS, preferred_element_type=F32)

    @pl.when(i == 0)
    def _():
        s_ref[0:hpg * blk, 0:blk] = jnp.full((hpg * blk, blk), NEG, F32)

    mxs = []
    for qb in range(nq):
        rs = slice(qb * hpg * blk, (qb + 1) * hpg * blk)
        s = s_ref[rs, :] + bias_ref[...]
        mx = jnp.max(s, axis=-1, keepdims=True)
        p_ref[rs, :] = jnp.exp(s - mx).astype(BF16)
        mxs.append(mx)

    for qb in range(nq):
        rows = slice(qb * blk, (qb + 1) * blk)
        for hp in range(n_pairs):
            cols = slice(hp * LANES, (hp + 1) * LANES)
            vals = windows(vp_ref, vc_ref, qb, cols)
            rhs = jnp.concatenate([vals, ones_rhs], axis=1)
            base = (qb * hpg + 2 * hp) * blk
            res = jnp.dot(p_ref[base:base + 2 * blk, :], rhs, preferred_element_type=F32)
            num = jnp.where(low, res[0:blk, 0:LANES], res[blk:2 * blk, 0:LANES])
            den = jnp.where(low, res[0:blk, LANES:], res[blk:2 * blk, LANES:])
            off = 2 * hp * blk
            mx = jnp.where(low, mxs[qb][off:off + blk], mxs[qb][off + blk:off + 2 * blk])
            o_ref[0, rows, cols] = (num / den).astype(o_ref.dtype)
            lse_ref[0, rows, cols] = mx + jnp.log(den)


def _dilated_group_attention(q, k, v, table, window, dilation):
    bb, l, w = q.shape
    nq = ATT_STEP_BLOCKS
    assert l % (nq * DIL_BLOCK) == 0
    steps = l // (nq * DIL_BLOCK)
    bucket = jnp.asarray(_band_tables(window, dilation))
    cur = pl.BlockSpec((1, nq * DIL_BLOCK, w), lambda b, i: (b, i, 0))
    prev = pl.BlockSpec((1, DIL_BLOCK, w), lambda b, i: (b, jnp.maximum(nq * i - 1, 0), 0))
    rows_all = nq * DIL_HEADS_PER_GROUP * DIL_BLOCK
    return pl.pallas_call(
        functools.partial(_attn_kernel, nq),
        out_shape=[jax.ShapeDtypeStruct((bb, l, w), BF16), jax.ShapeDtypeStruct((bb, l, w), F32)],
        grid=(bb, steps),
        in_specs=[pl.BlockSpec(memory_space=pltpu.SMEM),
                  pl.BlockSpec(bucket.shape, lambda b, i: (0, 0)),
                  cur, prev, cur, prev, cur],
        out_specs=[cur, cur],
        scratch_shapes=[pltpu.VMEM((DIL_HEADS_PER_GROUP * DIL_BLOCK, 2 * DIL_BLOCK), F32),
                        pltpu.VMEM((rows_all, 2 * DIL_BLOCK), F32),
                        pltpu.VMEM((rows_all, 2 * DIL_BLOCK), BF16)],
        compiler_params=_cparams(("arbitrary", "arbitrary")),
        name=f"dilated_attn_d{dilation}",
    )(table, bucket, q, k, k, v, v)


def _merge_kernel(alpha, dilations, ygla_ref, o0_ref, o1_ref, o2_ref, l0_ref, l1_ref, l2_ref, gg_ref, ga_ref, x_ref,
                  g1_ref, sc2_ref, sh2_ref, ln_g_ref, ln_b_ref, wpg_ref, wpa_ref, wout_ref, wr_ref, br_ref, ltri_ref,
                  x1_ref, u2_ref, route_ref, ew_ref, cnt_ref, stage_ref, carry_ref):
    tm = x_ref.shape[1]

    @pl.when((pl.program_id(0) == 0) & (pl.program_id(1) == 0))
    def _():
        carry_ref[...] = jnp.zeros_like(carry_ref)

    def natural(ref, dilation, slot):
        if dilation == 1:
            return ref[0, 0].astype(F32)
        n_lt = DIL_GROUP_WIDTH // LANES
        for r in range(dilation):
            for t in range(n_lt):
                stage_ref[slot, t, pl.ds(r, tm // dilation, stride=dilation), :] = ref[
                    0, r, :, t * LANES:(t + 1) * LANES].astype(F32)
        return jnp.concatenate([stage_ref[slot, t] for t in range(n_lt)], axis=1)

    lses = [natural(ref, dil, 0 + 2 * gi) for gi, (ref, dil) in enumerate(zip((l0_ref, l1_ref, l2_ref), dilations))]
    outs = [natural(ref, dil, 1 + 2 * gi) for gi, (ref, dil) in enumerate(zip((o0_ref, o1_ref, o2_ref), dilations))]
    lm = jnp.maximum(jnp.maximum(lses[0], lses[1]), lses[2])
    es = [jnp.exp(l - lm) for l in lses]
    y_att = (es[0] * outs[0] + es[1] * outs[1] + es[2] * outs[2]) / (es[0] + es[1] + es[2])

    p_gla = jnp.dot(ygla_ref[0], wpg_ref[...], preferred_element_type=F32)
    p_att = jnp.dot(y_att.astype(BF16), wpa_ref[...], preferred_element_type=F32)
    merged = gg_ref[0].astype(F32) * p_gla + ga_ref[0].astype(F32) * p_att
    y = jnp.dot(merged.astype(BF16), wout_ref[...], preferred_element_type=F32)
    x1 = _layer_norm(alpha * x_ref[0] + g1_ref[0] * y, ln_g_ref[...], ln_b_ref[...])
    x1_ref[0] = x1
    u2 = x1 * (1.0 + sc2_ref[0]) + sh2_ref[0]
    u2_ref[0] = _pack_bf16_pairs(u2)

    u_hi = u2.astype(BF16)
    u_lo = (u2 - u_hi.astype(F32)).astype(BF16)
    wr = wr_ref[...]
    w_hi = wr.astype(BF16)
    w_lo = (wr - w_hi.astype(F32)).astype(BF16)
    logits = (jnp.dot(u_hi, w_hi, preferred_element_type=F32) + jnp.dot(u_lo, w_hi, preferred_element_type=F32)
              + jnp.dot(u_hi, w_lo, preferred_element_type=F32)) + br_ref[...]
    lane = lax.broadcasted_iota(jnp.int32, logits.shape, 1)
    big = jnp.int32(LANES)
    lg = jnp.where(lane < MOE_GROUPS, logits, NEG)
    gmax = jnp.max(lg, axis=-1, keepdims=True)
    gidx = jnp.min(jnp.where(lg == gmax, lane, big), axis=-1, keepdims=True)
    gval = 1.0 / jnp.sum(jnp.exp(lg - gmax), axis=-1, keepdims=True)
    in_group = (lane >= MOE_GROUPS + gidx * MOE_EXPERTS) & (lane < MOE_GROUPS + (gidx + 1) * MOE_EXPERTS)
    le = jnp.where(in_group, logits, NEG)
    m1 = jnp.max(le, axis=-1, keepdims=True)
    i1 = jnp.min(jnp.where(le == m1, lane, big), axis=-1, keepdims=True)
    le2 = jnp.where(lane == i1, NEG, le)
    m2 = jnp.max(le2, axis=-1, keepdims=True)
    i2 = jnp.min(jnp.where(le2 == m2, lane, big), axis=-1, keepdims=True)
    t = jnp.exp(m2 - m1)
    w1 = 1.0 / (1.0 + t)
    w2 = t * w1

    hit1, hit2 = lane == i1, lane == i2
    onehot = jnp.where(hit1 | hit2, 1.0, 0.0)
    earlier = jnp.dot(ltri_ref[...], onehot.astype(BF16), preferred_element_type=F32) + carry_ref[...]
    rank1 = jnp.sum(jnp.where(hit1, earlier, 0.0), axis=-1, keepdims=True).astype(jnp.int32)
    rank2 = jnp.sum(jnp.where(hit2, earlier, 0.0), axis=-1, keepdims=True).astype(jnp.int32)
    carry = carry_ref[...] + jnp.sum(onehot, axis=0, keepdims=True)
    carry_ref[...] = carry
    cnt_ref[...] = carry.astype(jnp.int32)
    route = jnp.where(lane == 0, i1 - MOE_GROUPS, jnp.where(lane == 1, i2 - MOE_GROUPS,
                      jnp.where(lane == 2, rank1, jnp.where(lane == 3, rank2, 0))))
    route_ref[0] = jnp.transpose(route)[0:ROUTE_ROWS, :]
    ew_ref[0] = jnp.where(lane == 0, gval * w1, jnp.where(lane == 1, gval * w2, 0.0))


def _merge(alpha, y_gla, o_groups, lse_groups, g_gla, g_att, x, g1, sc2, sh2, ln_g, ln_b, wpg, wpa, wout, wr, br):
    bsz, s, d = x.shape
    tm = min(ROW_TILE, s)
    assert s % tm == 0
    dilations = tuple(dil for _, dil in DIL_PATTERNS)
    row = lambda w: pl.BlockSpec((1, tm, w), lambda b, i: (b, i, 0))
    sub = lambda dil: pl.BlockSpec((1, dil, tm // dil, DIL_GROUP_WIDTH), lambda b, i: (b, 0, i, 0))
    per_b = pl.BlockSpec((1, 1, d), lambda b, i: (b, 0, 0))
    full = lambda a: pl.BlockSpec(a.shape, lambda b, i: (0,) * a.ndim)
    ln_g2, ln_b2 = ln_g.reshape(1, d), ln_b.reshape(1, d)
    ltri = jnp.asarray(np.tril(np.ones((tm, tm), np.float32), -1), BF16)
    return pl.pallas_call(
        functools.partial(_merge_kernel, alpha, dilations),
        out_shape=[jax.ShapeDtypeStruct((bsz, s, d), F32), jax.ShapeDtypeStruct((bsz, s, d // 2), jnp.int32),
                   jax.ShapeDtypeStruct((bsz, ROUTE_ROWS, s), jnp.int32), jax.ShapeDtypeStruct((bsz, s, LANES), F32),
                   jax.ShapeDtypeStruct((1, LANES), jnp.int32)],
        grid=(bsz, s // tm),
        in_specs=[row(y_gla.shape[-1])] + [sub(dil) for dil in dilations] * 2
                 + [row(d), row(d), row(d), per_b, per_b, per_b, full(ln_g2), full(ln_b2),
                    full(wpg), full(wpa), full(wout), full(wr), full(br), full(ltri)],
        out_specs=[row(d), row(d // 2), pl.BlockSpec((1, ROUTE_ROWS, tm), lambda b, i: (b, 0, i)), row(LANES),
                   pl.BlockSpec((1, LANES), lambda b, i: (0, 0))],
        scratch_shapes=[pltpu.VMEM((2 * DIL_GROUPS, DIL_GROUP_WIDTH // LANES, tm, LANES), F32),
                        pltpu.VMEM((1, LANES), F32)],
        compiler_params=_cparams(("arbitrary", "arbitrary")),
        name="merge_ln1_router",
    )(y_gla, *o_groups, *lse_groups, g_gla, g_att, x, g1, sc2, sh2, ln_g2, ln_b2, wpg, wpa, wout, wr, br, ltri)


def _expert_kernel(run_ref, rexp_ref, used_ref, x_ref, wg_hbm, wu_hbm, wd_hbm, o_ref,
                   wg_f, wu_f, wd_f, wg_s, wu_s, wd_s, sem):
    t = pl.program_id(0)
    n_tiles_used, n_runs = used_ref[0], used_ref[1]
    run = run_ref[t]
    active = t < n_tiles_used
    first_of_run = (t == 0) | (run_ref[jnp.maximum(t - 1, 0)] != run)

    def weight_copies(r):
        e, slot = rexp_ref[r], r % 2
        return [pltpu.make_async_copy(hbm.at[e], buf.at[slot], sem.at[slot, j])
                for j, (hbm, buf) in enumerate(((wg_hbm, wg_f), (wu_hbm, wu_f), (wd_hbm, wd_f)))]

    @pl.when(active & (t == 0))
    def _():
        for cp in weight_copies(0):
            cp.start()

    @pl.when(active & first_of_run)
    def _():
        @pl.when(run + 1 < n_runs)
        def _():
            for cp in weight_copies(run + 1):
                cp.start()

        for cp in weight_copies(run):
            cp.wait()
        slot = run % 2
        wg_s[...] = wg_f[slot].astype(BF16)
        wu_s[...] = wu_f[slot].astype(BF16)
        wd_s[...] = wd_f[slot].astype(BF16)

    @pl.when(active)
    def _():
        xt = _unpack_bf16_pairs(x_ref[...]).astype(BF16)
        hg = jnp.dot(xt, wg_s[...], preferred_element_type=F32)
        hu = jnp.dot(xt, wu_s[...], preferred_element_type=F32)
        h = (_silu(hg) * hu).astype(BF16)
        o_ref[...] = _pack_bf16_pairs(jnp.dot(h, wd_s[...], preferred_element_type=F32))

    @pl.when(jnp.logical_not(active))
    def _():
        o_ref[...] = jnp.zeros_like(o_ref)


def _expert_ffn(tile_run, run_expert, used, xg, w_gate, w_up, w_down):
    p = xg.shape[0]
    ne, d, ff = w_gate.shape
    tm = EXPERT_TILE
    n_tiles = p // tm
    hbm = pl.BlockSpec(memory_space=pl.ANY)
    grid_spec = pltpu.PrefetchScalarGridSpec(
        num_scalar_prefetch=3,
        grid=(n_tiles,),
        in_specs=[pl.BlockSpec((tm, d // 2), lambda t, *_: (t, 0)), hbm, hbm, hbm],
        out_specs=pl.BlockSpec((tm, d // 2), lambda t, *_: (t, 0)),
        scratch_shapes=[pltpu.VMEM((2, d, ff), F32), pltpu.VMEM((2, d, ff), F32), pltpu.VMEM((2, ff, d), F32),
                        pltpu.VMEM((d, ff), BF16), pltpu.VMEM((d, ff), BF16), pltpu.VMEM((ff, d), BF16),
                        pltpu.SemaphoreType.DMA((2, 3))],
    )
    return pl.pallas_call(
        _expert_kernel,
        out_shape=jax.ShapeDtypeStruct((p, d // 2), jnp.int32),
        grid_spec=grid_spec,
        compiler_params=_cparams(("arbitrary",)),
        name="expert_ffn",
    )(tile_run, run_expert, used, xg, w_gate, w_up, w_down)


def _final_kernel(alpha, x1_ref, ya_ref, yb_ref, ew_ref, g2_ref, ln_g_ref, ln_b_ref, o_ref):
    ew = ew_ref[0]
    y = ew[:, 0:1] * _unpack_bf16_pairs(ya_ref[0]) + ew[:, 1:2] * _unpack_bf16_pairs(yb_ref[0])
    o_ref[0] = _layer_norm(alpha * x1_ref[0] + g2_ref[0] * y, ln_g_ref[...], ln_b_ref[...])


def _final(alpha, x1, ya, yb, ew, g2, ln_g, ln_b):
    bsz, s, d = x1.shape
    tm = min(ROW_TILE, s)
    row = lambda w: pl.BlockSpec((1, tm, w), lambda b, i: (b, i, 0))
    full = lambda a: pl.BlockSpec(a.shape, lambda b, i: (0,) * a.ndim)
    ln_g2, ln_b2 = ln_g.reshape(1, d), ln_b.reshape(1, d)
    return pl.pallas_call(
        functools.partial(_final_kernel, alpha),
        out_shape=jax.ShapeDtypeStruct((bsz, s, d), F32),
        grid=(bsz, s // tm),
        in_specs=[row(d), row(d // 2), row(d // 2), row(LANES), pl.BlockSpec((1, 1, d), lambda b, i: (b, 0, 0)),
                  full(ln_g2), full(ln_b2)],
        out_specs=row(d),
        compiler_params=_cparams(("parallel", "arbitrary")),
        name="combine_ln2",
    )(x1, ya, yb, ew, g2, ln_g2, ln_b2)


SC_CORES = 2
SC_SUBCORES = 16
SC_CHUNK = 64


def _sc_mesh():
    return plsc.VectorSubcoreMesh(core_axis_name="c", subcore_axis_name="s")


def _sc_scatter_rows(rows, dest0, dest1, n_rows):
    n, w = rows.shape
    n_workers = SC_CORES * SC_SUBCORES
    assert n % (n_workers * SC_CHUNK) == 0
    n_chunks = n // (n_workers * SC_CHUNK)
    d0 = dest0.reshape(n // SC_CHUNK, 1, SC_CHUNK)
    d1 = dest1.reshape(n // SC_CHUNK, 1, SC_CHUNK)

    @functools.partial(
        pl.kernel, mesh=_sc_mesh(), out_type=jax.ShapeDtypeStruct((n_rows, w), rows.dtype),
        scratch_types=[pltpu.VMEM((1, SC_CHUNK), jnp.int32), pltpu.VMEM((1, SC_CHUNK), jnp.int32),
                       pltpu.VMEM((SC_CHUNK, w), rows.dtype)])
    def scatter_kernel(rows_hbm, d0_hbm, d1_hbm, out_hbm, i0_v, i1_v, rows_v):
        wid = lax.axis_index("s") * SC_CORES + lax.axis_index("c")

        @pl.loop(0, n_chunks)
        def _(j):
            c = wid * n_chunks + j
            pltpu.sync_copy(rows_hbm.at[pl.ds(c * SC_CHUNK, SC_CHUNK)], rows_v)
            pltpu.sync_copy(d0_hbm.at[c], i0_v)
            pltpu.sync_copy(d1_hbm.at[c], i1_v)
            pltpu.sync_copy(rows_v, out_hbm.at[i0_v.at[0]])
            pltpu.sync_copy(rows_v, out_hbm.at[i1_v.at[0]])

    return scatter_kernel(rows, d0, d1)


def _sc_gather_rows(table, dest0, dest1):
    n = dest0.shape[0]
    w = table.shape[1]
    n_workers = SC_CORES * SC_SUBCORES
    assert n % (n_workers * SC_CHUNK) == 0
    n_chunks = n // (n_workers * SC_CHUNK)
    d0 = dest0.reshape(n // SC_CHUNK, 1, SC_CHUNK)
    d1 = dest1.reshape(n // SC_CHUNK, 1, SC_CHUNK)
    out = jax.ShapeDtypeStruct((n, w), table.dtype)

    @functools.partial(
        pl.kernel, mesh=_sc_mesh(), out_type=(out, out),
        scratch_types=[pltpu.VMEM((1, SC_CHUNK), jnp.int32), pltpu.VMEM((SC_CHUNK, w), table.dtype)])
    def gather_kernel(table_hbm, d0_hbm, d1_hbm, a_hbm, b_hbm, i_v, rows_v):
        wid = lax.axis_index("s") * SC_CORES + lax.axis_index("c")

        @pl.loop(0, n_chunks)
        def _(j):
            c = wid * n_chunks + j
            for d_hbm, o_hbm in ((d0_hbm, a_hbm), (d1_hbm, b_hbm)):
                pltpu.sync_copy(d_hbm.at[c], i_v)
                pltpu.sync_copy(table_hbm.at[i_v.at[0]], rows_v)
                pltpu.sync_copy(rows_v, o_hbm.at[pl.ds(c * SC_CHUNK, SC_CHUNK)])

    return gather_kernel(table, d0, d1)


def _dispatch_plan(route, counts):
    tm = EXPERT_TILE
    e0, e1, r0, r1 = (route[:, j, :].reshape(-1) for j in range(4))
    experts = jnp.arange(MOE_TOTAL, dtype=jnp.int32)
    tiles_per = (counts + tm - 1) // tm
    tile_end = jnp.cumsum(tiles_per)
    pad_start = ((tile_end - tiles_per) * tm).astype(jnp.int32)

    def lookup(e):
        return jnp.sum(jnp.where(e[None, :] == experts[:, None], pad_start[:, None], 0), axis=0)

    dest0, dest1 = lookup(e0) + r0, lookup(e1) + r1
    n_tiles = (2 * e0.size + MOE_TOTAL * tm) // tm
    tile_expert = jnp.minimum(jnp.sum(tile_end[None, :] <= jnp.arange(n_tiles)[:, None], axis=1), MOE_TOTAL - 1)
    nonempty = counts > 0
    run_of_expert = jnp.cumsum(nonempty.astype(jnp.int32)) - 1
    run_expert = jnp.sum(jnp.where(nonempty[None, :] & (run_of_expert[None, :] == experts[:, None]),
                                   experts[None, :], 0), axis=1).astype(jnp.int32)
    tile_run = jnp.sum(jnp.where(tile_expert[:, None] == experts[None, :], run_of_expert[None, :], 0),
                       axis=1).astype(jnp.int32)
    used = jnp.stack([tile_end[-1], jnp.sum(nonempty)]).astype(jnp.int32)
    return dest0, dest1, tile_run, run_expert, used, n_tiles * tm


def _layer(x, c, rel_bias, w_ada, b_ada, w_in, w_gla_gate, b_gla_gate, gla_norm, w_proj_gla, w_proj_attn, w_out,
           ln1_g, ln1_b, w_rg, b_rg, w_re, b_re, w_eg, w_eu, w_ed, ln2_g, ln2_b):
    bsz, s, d = x.shape
    alpha = (2.0 * DEPTH) ** 0.25
    mods = _ada_mods(c, w_ada, b_ada)
    sh1, sc1, g1, sh2, sc2, g2 = [m.reshape(bsz, 1, d) for m in jnp.split(mods, N_MOD, axis=-1)]

    lr0 = d // 2 * 2 + 2 * d
    z = _in_projection(x, sc1, sh1, _prep_in_weight(w_in, lr0))

    y_gla = _gla(z["q_gla"], z["k_gla"], z["v_gla"], z["r_gla"], z["lr"], w_gla_gate, b_gla_gate, gla_norm)

    o_groups, lse_groups = [], []
    for g, (window, dilation) in enumerate(DIL_PATTERNS):
        l = s // dilation
        qg, kg, vg = (z[f"{n}{g}"].reshape(bsz * dilation, l, DIL_GROUP_WIDTH) for n in ("q_att", "k_att", "v_att"))
        table = rel_bias[:, g * DIL_HEADS_PER_GROUP:(g + 1) * DIL_HEADS_PER_GROUP]
        o, lse = _dilated_group_attention(qg, kg, vg, table, window, dilation)
        o_groups.append(o.reshape(bsz, dilation, l, DIL_GROUP_WIDTH))
        lse_groups.append(lse.reshape(bsz, dilation, l, DIL_GROUP_WIDTH))

    wr = jnp.concatenate([w_rg, w_re, jnp.zeros((d, LANES - MOE_GROUPS - MOE_TOTAL), F32)], axis=1)
    br = jnp.concatenate([b_rg, b_re, jnp.zeros((LANES - MOE_GROUPS - MOE_TOTAL,), F32)]).reshape(1, LANES)
    x1, u2, route, ew, cnt = _merge(alpha, y_gla, o_groups, lse_groups, z["g_gla"], z["g_att"], x, g1, sc2, sh2,
                                    ln1_g, ln1_b, w_proj_gla.astype(BF16), w_proj_attn.astype(BF16),
                                    w_out.astype(BF16), wr, br)

    n = bsz * s
    counts = cnt[0, MOE_GROUPS:MOE_GROUPS + MOE_TOTAL]
    dest0, dest1, tile_run, run_expert, used, n_rows = _dispatch_plan(route, counts)
    xg = _sc_scatter_rows(u2.reshape(n, d // 2), dest0, dest1, n_rows)
    ff = w_eg.shape[-1]
    yo = _expert_ffn(tile_run, run_expert, used, xg, w_eg.reshape(MOE_TOTAL, d, ff),
                     w_eu.reshape(MOE_TOTAL, d, ff), w_ed.reshape(MOE_TOTAL, ff, d))
    ya, yb = (y.reshape(bsz, s, d // 2) for y in _sc_gather_rows(yo, dest0, dest1))
    return _final(alpha, x1, ya, yb, ew, g2, ln2_g, ln2_b)


def kernel(x, c, rel_bias, w_ada, b_ada, w_in, w_gla_gate, b_gla_gate, gla_norm, w_proj_gla, w_proj_attn, w_out,
           ln1_g, ln1_b, w_router_group, b_router_group, w_router_expert, b_router_expert, w_exp_gate, w_exp_up,
           w_exp_down, ln2_g, ln2_b):
    assert w_ada.shape[0] == DEPTH
    return _layer(x, c, rel_bias, w_ada[0], b_ada[0], w_in[0], w_gla_gate[0], b_gla_gate[0], gla_norm[0],
                  w_proj_gla[0], w_proj_attn[0], w_out[0], ln1_g[0], ln1_b[0], w_router_group[0],
                  b_router_group[0], w_router_expert[0], b_router_expert[0], w_exp_gate[0], w_exp_up[0],
                  w_exp_down[0], ln2_g[0], ln2_b[0])
```

```python
import functools
import math

import numpy as np
import jax
import jax.numpy as jnp
from jax import lax
from jax.experimental import pallas as pl
from jax.experimental.pallas import tpu as pltpu
from jax.experimental.pallas import tpu_sc as plsc

F32 = jnp.float32
BF16 = jnp.bfloat16

N_MOD = 6
GLA_HEADS = 4
GLA_LOWRANK = 16
GLA_TAU = 16.0
GLA_CHUNK = 64
DIL_PATTERNS = ((128, 1), (512, 4), (2048, 16))
DIL_GROUPS = len(DIL_PATTERNS)
DIL_HEADS_PER_GROUP = 8
DIL_HEAD_DIM = 64
DIL_GROUP_WIDTH = DIL_HEADS_PER_GROUP * DIL_HEAD_DIM
DIL_BLOCK = 128
REL_BUCKETS = 32
REL_MAX_DIST = 2048
MOE_GROUPS = 4
MOE_EXPERTS = 8
MOE_TOTAL = MOE_GROUPS * MOE_EXPERTS
LN_EPS = 1e-5
DEPTH = 1

LANES = 128
VMEM_LIMIT = 56 * 1024 * 1024
LOG2E = 1.4426950408889634
LN2 = 0.6931471805599453
NEG = -1e30
ROW_TILE = 512
EXPERT_TILE = 512
EXPERT_BLOCK = 256
GLA_STEP_CHUNKS = 4
ATT_STEP_BLOCKS = 2
MERGE_SUB_ROWS = 512
ROUTE_ROWS = 8

HIGHEST = lax.Precision.HIGHEST
NT_DIMS = (((1,), (1,)), ((), ()))
TN_DIMS = (((0,), (0,)), ((), ()))


def _cparams(sem):
    return pltpu.CompilerParams(dimension_semantics=sem, vmem_limit_bytes=VMEM_LIMIT)


def _sigmoid(x):
    return 0.5 * jnp.tanh(0.5 * x) + 0.5


def _silu(x):
    return x * _sigmoid(x)


def _layer_norm(x, g, b):
    mu = jnp.mean(x, axis=-1, keepdims=True)
    xc = x - mu
    var = jnp.mean(xc * xc, axis=-1, keepdims=True)
    return xc * lax.rsqrt(var + LN_EPS) * g + b


def _pack_bf16_pairs(x):
    w = x.shape[1] // 2
    lo = lax.bitcast_convert_type(x[:, :w].astype(BF16).astype(F32), jnp.uint32) >> 16
    hi = lax.bitcast_convert_type(x[:, w:].astype(BF16).astype(F32), jnp.uint32) & jnp.uint32(0xFFFF0000)
    return lax.bitcast_convert_type(lo | hi, jnp.int32)


def _unpack_bf16_pairs(p):
    u = lax.bitcast_convert_type(p, jnp.uint32)
    lo = lax.bitcast_convert_type(u << 16, F32)
    hi = lax.bitcast_convert_type(u & jnp.uint32(0xFFFF0000), F32)
    return jnp.concatenate([lo, hi], axis=1)


def _split3(x):
    hi = x.astype(BF16)
    r1 = x - hi.astype(F32)
    mid = r1.astype(BF16)
    lo = (r1 - mid.astype(F32)).astype(BF16)
    return hi, mid, lo


def _mods_kernel(c_ref, w_ref, b_ref, o_ref):
    a = _silu(c_ref[...])
    o_ref[...] = jnp.dot(a, w_ref[...], precision=HIGHEST, preferred_element_type=F32) + b_ref[...]


def _ada_mods(c, w, b):
    bsz, d = c.shape
    n = w.shape[1]
    tn = 1536
    assert n % tn == 0
    return pl.pallas_call(
        _mods_kernel,
        out_shape=jax.ShapeDtypeStruct((bsz, n), F32),
        grid=(n // tn,),
        in_specs=[pl.BlockSpec((bsz, d), lambda j: (0, 0)),
                  pl.BlockSpec((d, tn), lambda j: (0, j)),
                  pl.BlockSpec((1, tn), lambda j: (0, j))],
        out_specs=pl.BlockSpec((bsz, tn), lambda j: (0, j)),
        compiler_params=_cparams(("arbitrary",)),
        name="ada_mods",
    )(c, w, b.reshape(1, n))


def _proj_pieces(d_model):
    dk = d_model // 2
    pieces = [("q_gla", dk, "scale_q_gla"), ("k_gla", dk, None), ("v_gla", d_model, None), ("r_gla", d_model, "silu")]
    for name, post in (("q_att", "scale_q_att"), ("k_att", None), ("v_att", None)):
        for g, (_, dilation) in enumerate(DIL_PATTERNS):
            pieces.append((f"{name}{g}", DIL_GROUP_WIDTH, (post, dilation)))
    pieces += [("g_gla", d_model, "sigmoid"), ("g_att", d_model, "sigmoid"), ("lr", LANES, "lowrank")]
    return tuple(pieces)


def _wprep_kernel(lr0, w_ref, o_ref):
    main = w_ref.shape[1] - GLA_LOWRANK
    step = 512
    for c0 in range(0, lr0, step):
        o_ref[:, c0:c0 + step] = w_ref[:, c0:c0 + step].astype(BF16)
    for c0 in range(lr0, main, step):
        o_ref[:, c0:c0 + step] = w_ref[:, c0 + GLA_LOWRANK:c0 + GLA_LOWRANK + step].astype(BF16)
    tail = w_ref[:, lr0:lr0 + LANES]
    lane = lax.broadcasted_iota(jnp.int32, tail.shape, 1)
    o_ref[:, main:main + LANES] = jnp.where(lane < GLA_LOWRANK, tail, 0.0).astype(BF16)


def _prep_in_weight(w_in, lr0):
    _, d, n_in = w_in.shape
    n_out = n_in - GLA_LOWRANK + LANES
    assert lr0 % 512 == 0 and (n_in - GLA_LOWRANK - lr0) % 512 == 0
    tr = 128
    return pl.pallas_call(
        functools.partial(_wprep_kernel, lr0),
        out_shape=jax.ShapeDtypeStruct((d, n_out), BF16),
        grid=(d // tr,),
        in_specs=[pl.BlockSpec((None, tr, n_in), lambda i: (0, i, 0))],
        out_specs=pl.BlockSpec((tr, n_out), lambda i: (i, 0)),
        compiler_params=_cparams(("parallel",)),
        name="prep_in_weight",
    )(w_in)


def _proj_kernel(pieces, head_k, x_ref, sc_ref, sh_ref, w_ref, *refs):
    out_refs, stage_ref = refs[:-1], refs[-1]
    tm = x_ref.shape[1]
    u = (x_ref[0] * (1.0 + sc_ref[0]) + sh_ref[0]).astype(BF16)
    off = 0
    for (name, width, post), o_ref in zip(pieces, out_refs):
        chunk = min(width, 512)
        for c0 in range(0, width, chunk):
            acc = jnp.dot(u, w_ref[:, off + c0:off + c0 + chunk], preferred_element_type=F32)
            if post == "silu":
                acc = _silu(acc)
            elif post == "sigmoid":
                acc = _sigmoid(acc)
            elif post == "scale_q_gla":
                acc = acc * (head_k ** -0.5)
            if post == "lowrank":
                o_ref[0] = acc[:, :GLA_LOWRANK]
            elif isinstance(post, tuple):
                scale, dilation = post
                if scale is not None:
                    acc = acc * (DIL_HEAD_DIM ** -0.5 * LOG2E)
                if dilation == 1:
                    o_ref[0, 0] = acc.astype(o_ref.dtype)
                else:
                    for t in range(width // LANES):
                        stage_ref[t] = acc[:, t * LANES:(t + 1) * LANES]
                    for r in range(dilation):
                        for t in range(width // LANES):
                            o_ref[0, r, :, t * LANES:(t + 1) * LANES] = stage_ref[
                                t, pl.ds(r, tm // dilation, stride=dilation), :].astype(o_ref.dtype)
            else:
                o_ref[0, :, c0:c0 + chunk] = acc.astype(o_ref.dtype)
        off += width


def _in_projection(x, sc1, sh1, w_perm):
    bsz, s, d = x.shape
    pieces = _proj_pieces(d)
    assert sum(p[1] for p in pieces) == w_perm.shape[1]
    tm = min(ROW_TILE, s)
    assert s % tm == 0
    head_k = (d // 2) // GLA_HEADS
    out_shape, out_specs = [], []
    for name, width, post in pieces:
        if post == "lowrank":
            out_shape.append(jax.ShapeDtypeStruct((bsz, s, GLA_LOWRANK), F32))
            out_specs.append(pl.BlockSpec((1, tm, GLA_LOWRANK), lambda b, i: (b, i, 0)))
        elif isinstance(post, tuple):
            dil = post[1]
            assert tm % (dil * 16) == 0
            out_shape.append(jax.ShapeDtypeStruct((bsz, dil, s // dil, width), BF16))
            out_specs.append(pl.BlockSpec((1, dil, tm // dil, width), lambda b, i: (b, 0, i, 0)))
        else:
            out_shape.append(jax.ShapeDtypeStruct((bsz, s, width), BF16))
            out_specs.append(pl.BlockSpec((1, tm, width), lambda b, i: (b, i, 0)))
    outs = pl.pallas_call(
        functools.partial(_proj_kernel, pieces, head_k),
        out_shape=out_shape,
        grid=(bsz, s // tm),
        in_specs=[pl.BlockSpec((1, tm, d), lambda b, i: (b, i, 0)),
                  pl.BlockSpec((1, 1, d), lambda b, i: (b, 0, 0)),
                  pl.BlockSpec((1, 1, d), lambda b, i: (b, 0, 0)),
                  pl.BlockSpec(w_perm.shape, lambda b, i: (0, 0), pipeline_mode=pl.Buffered(1))],
        out_specs=out_specs,
        scratch_shapes=[pltpu.VMEM((DIL_GROUP_WIDTH // LANES, tm, LANES), F32)],
        compiler_params=_cparams(("parallel", "arbitrary")),
        name="in_projection",
    )(x, sc1, sh1, w_perm)
    return dict(zip([p[0] for p in pieces], outs))


def _gla_kernel(n_chunks, head_k, head_v, q_ref, k_ref, v_ref, r_ref, lr_ref, wg_ref, bg_ref, ng_ref, o_ref,
                state_ref):
    @pl.when(pl.program_id(1) == 0)
    def _():
        state_ref[...] = jnp.zeros_like(state_ref)

    c = GLA_CHUNK
    row = lax.broadcasted_iota(jnp.int32, (c, c), 0)
    col = lax.broadcasted_iota(jnp.int32, (c, c), 1)
    causal = row >= col
    tril = causal.astype(BF16)
    mid = c // 2 - 1
    gate_in = jnp.dot(lr_ref[0], wg_ref[...], precision=HIGHEST, preferred_element_type=F32) + bg_ref[...]
    g_all = (jnp.minimum(gate_in, 0.0) - jnp.log(1.0 + jnp.exp(-jnp.abs(gate_in)))) * (1.0 / GLA_TAU)
    g_hi, g_mid, g_lo = _split3(g_all)
    for ci in range(n_chunks):
        rows = slice(ci * c, (ci + 1) * c)
        bc = (jnp.dot(tril, g_hi[rows], preferred_element_type=F32)
              + jnp.dot(tril, g_mid[rows], preferred_element_type=F32)
              + jnp.dot(tril, g_lo[rows], preferred_element_type=F32))
        b_mid = bc[mid:mid + 1, :]
        b_last = bc[c - 1:c, :]
        qf = q_ref[0, rows, :].astype(F32)
        kf = k_ref[0, rows, :].astype(F32)
        q_in = (qf * jnp.exp(bc - b_mid)).astype(BF16)
        k_in = (kf * jnp.exp(b_mid - bc)).astype(BF16)
        q_st = (qf * jnp.exp(bc)).astype(BF16)
        k_st = (kf * jnp.exp(b_last - bc)).astype(BF16)
        dec = jnp.exp(b_last)
        for h in range(GLA_HEADS):
            ks = slice(h * head_k, (h + 1) * head_k)
            vs = slice(h * head_v, (h + 1) * head_v)
            vh = v_ref[0, rows, vs]
            att = lax.dot_general(q_in[:, ks], k_in[:, ks], NT_DIMS, preferred_element_type=F32)
            att = jnp.where(causal, att, 0.0).astype(BF16)
            st = state_ref[h]
            o = jnp.dot(att, vh, preferred_element_type=F32)
            o = o + lax.dot_general(q_st[:, ks], st.astype(BF16), NT_DIMS, preferred_element_type=F32)
            kv_t = lax.dot_general(vh, k_st[:, ks], TN_DIMS, preferred_element_type=F32)
            state_ref[h] = st * dec[:, ks] + kv_t
            ms = jnp.mean(o * o, axis=-1, keepdims=True)
            o = o * lax.rsqrt(ms + LN_EPS) * ng_ref[:, vs] * r_ref[0, rows, vs].astype(F32)
            o_ref[0, rows, vs] = o.astype(o_ref.dtype)


def _gla(q, k, v, r_silu, lr, w_gate, b_gate, norm_g):
    bsz, s, dk = q.shape
    dv = v.shape[-1]
    head_k, head_v = dk // GLA_HEADS, dv // GLA_HEADS
    n_chunks = min(GLA_STEP_CHUNKS, s // GLA_CHUNK)
    ct = GLA_CHUNK * n_chunks
    assert s % ct == 0
    row_spec = lambda w: pl.BlockSpec((1, ct, w), lambda b, i: (b, i, 0))
    full = lambda a: pl.BlockSpec(a.shape, lambda b, i: (0,) * a.ndim)
    bg = b_gate.reshape(1, dk)
    ng = norm_g.reshape(1, dv)
    return pl.pallas_call(
        functools.partial(_gla_kernel, n_chunks, head_k, head_v),
        out_shape=jax.ShapeDtypeStruct((bsz, s, dv), BF16),
        grid=(bsz, s // ct),
        in_specs=[row_spec(dk), row_spec(dk), row_spec(dv), row_spec(dv), row_spec(GLA_LOWRANK),
                  full(w_gate), full(bg), full(ng)],
        out_specs=row_spec(dv),
        scratch_shapes=[pltpu.VMEM((GLA_HEADS, head_v, head_k), F32)],
        compiler_params=_cparams(("parallel", "arbitrary")),
        name="gla",
    )(q, k, v, r_silu, lr, w_gate, bg, ng)


def _t5_bucket_np(dist):
    exact = REL_BUCKETS // 2
    d = np.maximum(dist, 1).astype(np.float32)
    large = exact + (np.log(d / np.float32(exact)) / np.float32(math.log(REL_MAX_DIST / exact))
                     * np.float32(REL_BUCKETS - exact)).astype(np.int32)
    large = np.minimum(large, REL_BUCKETS - 1)
    return np.where(dist < exact, dist, large).astype(np.int32)


def _band_tables(window, dilation):
    qi = np.arange(DIL_BLOCK)[:, None]
    kj = np.arange(2 * DIL_BLOCK)[None, :]
    m = qi + DIL_BLOCK - kj
    n_steps = window // dilation
    band = (m >= 0) & (m <= n_steps)
    bucket = _t5_bucket_np(np.clip(m, 0, n_steps) * dilation)
    return np.where(band, bucket, -1).astype(np.int32)


def _attn_kernel(nq, table_ref, bucket_ref, q_ref, kp_ref, kc_ref, vp_ref, vc_ref, o_ref, lse_ref,
                 bias_ref, s_ref, p_ref):
    i = pl.program_id(1)
    blk = DIL_BLOCK
    hpg = DIL_HEADS_PER_GROUP
    n_pairs = hpg // 2

    @pl.when((pl.program_id(0) == 0) & (i == 0))
    def _():
        bucket = bucket_ref[...]
        for h in range(hpg):
            acc = jnp.full(bucket.shape, NEG, F32)
            for bkt in range(REL_BUCKETS):
                acc = jnp.where(bucket == bkt, table_ref[bkt, h] * LOG2E, acc)
            bias_ref[h * blk:(h + 1) * blk, :] = acc

    lane = lax.broadcasted_iota(jnp.int32, (blk, LANES), 1)
    low = lane < DIL_HEAD_DIM
    ones_rhs = jnp.ones((2 * blk, LANES), BF16)

    def windows(ref_p, ref_c, qb, cols):
        if qb == 0:
            return jnp.concatenate([ref_p[0, :, cols], ref_c[0, 0:blk, cols]], axis=0)
        return ref_c[0, (qb - 1) * blk:(qb + 1) * blk, cols]

    for qb in range(nq):
        rows = slice(qb * blk, (qb + 1) * blk)
        for hp in range(n_pairs):
            cols = slice(hp * LANES, (hp + 1) * LANES)
            qp = q_ref[0, rows, cols]
            zero = jnp.zeros_like(qp)
            qq = jnp.concatenate([jnp.where(low, qp, zero), jnp.where(low, zero, qp)], axis=0)
            keys = windows(kp_ref, kc_ref, qb, cols)
            base = (qb * hpg + 2 * hp) * blk
            s_ref[base:base + 2 * blk, :] = lax.dot_general(qq, keys, NT_DIMS, preferred_element_type=F32)

    @pl.when(i == 0)
    def _():
        s_ref[0:hpg * blk, 0:blk] = jnp.full((hpg * blk, blk), NEG, F32)

    mxs = []
    for qb in range(nq):
        rs = slice(qb * hpg * blk, (qb + 1) * hpg * blk)
        s = s_ref[rs, :] + bias_ref[...]
        mx = jnp.max(s, axis=-1, keepdims=True)
        p_ref[rs, :] = jnp.exp2(s - mx).astype(BF16)
        mxs.append(mx)

    for qb in range(nq):
        rows = slice(qb * blk, (qb + 1) * blk)
        for hp in range(n_pairs):
            cols = slice(hp * LANES, (hp + 1) * LANES)
            vals = windows(vp_ref, vc_ref, qb, cols)
            rhs = jnp.concatenate([vals, ones_rhs], axis=1)
            base = (qb * hpg + 2 * hp) * blk
            res = jnp.dot(p_ref[base:base + 2 * blk, :], rhs, preferred_element_type=F32)
            num = jnp.where(low, res[0:blk, 0:LANES], res[blk:2 * blk, 0:LANES])
            den = jnp.where(low, res[0:blk, LANES:], res[blk:2 * blk, LANES:])
            off = 2 * hp * blk
            mx = jnp.where(low, mxs[qb][off:off + blk], mxs[qb][off + blk:off + 2 * blk])
            o_ref[0, rows, cols] = (num / den).astype(o_ref.dtype)
            lse_ref[0, rows, cols] = (mx + jnp.log2(den)) * LN2


def _dilated_group_attention(q, k, v, table, window, dilation):
    bb, l, w = q.shape
    nq = ATT_STEP_BLOCKS
    assert l % (nq * DIL_BLOCK) == 0
    steps = l // (nq * DIL_BLOCK)
    bucket = jnp.asarray(_band_tables(window, dilation))
    cur = pl.BlockSpec((1, nq * DIL_BLOCK, w), lambda b, i: (b, i, 0))
    prev = pl.BlockSpec((1, DIL_BLOCK, w), lambda b, i: (b, jnp.maximum(nq * i - 1, 0), 0))
    rows_all = nq * DIL_HEADS_PER_GROUP * DIL_BLOCK
    return pl.pallas_call(
        functools.partial(_attn_kernel, nq),
        out_shape=[jax.ShapeDtypeStruct((bb, l, w), BF16), jax.ShapeDtypeStruct((bb, l, w), F32)],
        grid=(bb, steps),
        in_specs=[pl.BlockSpec(memory_space=pltpu.SMEM),
                  pl.BlockSpec(bucket.shape, lambda b, i: (0, 0)),
                  cur, prev, cur, prev, cur],
        out_specs=[cur, cur],
        scratch_shapes=[pltpu.VMEM((DIL_HEADS_PER_GROUP * DIL_BLOCK, 2 * DIL_BLOCK), F32),
                        pltpu.VMEM((rows_all, 2 * DIL_BLOCK), F32),
                        pltpu.VMEM((rows_all, 2 * DIL_BLOCK), BF16)],
        compiler_params=_cparams(("arbitrary", "arbitrary")),
        name=f"dilated_attn_d{dilation}",
    )(table, bucket, q, k, k, v, v)


def _merge_kernel(alpha, dilations, ygla_ref, o0_ref, o1_ref, o2_ref, l0_ref, l1_ref, l2_ref, gg_ref, ga_ref, x_ref,
                  g1_ref, sc2_ref, sh2_ref, ln_g_ref, ln_b_ref, wpg_ref, wpa_ref, wout_ref, wr_ref, br_ref, ltri_ref,
                  x1_ref, u2_ref, route_ref, ew_ref, cnt_ref, stage_ref, carry_ref):
    tm = x_ref.shape[1]

    @pl.when((pl.program_id(0) == 0) & (pl.program_id(1) == 0))
    def _():
        carry_ref[...] = jnp.zeros_like(carry_ref)

    n_lt = DIL_GROUP_WIDTH // LANES
    group_refs = tuple(zip((l0_ref, l1_ref, l2_ref), (o0_ref, o1_ref, o2_ref), dilations))
    for gi, (l_ref, o_ref, dil) in enumerate(group_refs):
        if dil > 1:
            for slot, ref in ((2 * gi, l_ref), (2 * gi + 1, o_ref)):
                for r in range(dil):
                    for t in range(n_lt):
                        stage_ref[slot, t, pl.ds(r, tm // dil, stride=dil), :] = ref[
                            0, r, :, t * LANES:(t + 1) * LANES].astype(F32)

    wr = wr_ref[...]
    w_hi = wr.astype(BF16)
    w_lo = (wr - w_hi.astype(F32)).astype(BF16)
    sub = ltri_ref.shape[0]
    for rows in (slice(r0, r0 + sub) for r0 in range(0, tm, sub)):
        def natural(ref, dil, slot):
            if dil == 1:
                return ref[0, 0, rows, :].astype(F32)
            return jnp.concatenate([stage_ref[slot, t, rows, :] for t in range(n_lt)], axis=1)

        lses = [natural(l_ref, dil, 2 * gi) for gi, (l_ref, _, dil) in enumerate(group_refs)]
        outs = [natural(o_ref, dil, 2 * gi + 1) for gi, (_, o_ref, dil) in enumerate(group_refs)]
        lm = jnp.maximum(jnp.maximum(lses[0], lses[1]), lses[2])
        es = [jnp.exp(l - lm) for l in lses]
        y_att = (es[0] * outs[0] + es[1] * outs[1] + es[2] * outs[2]) / (es[0] + es[1] + es[2])

        p_gla = jnp.dot(ygla_ref[0, rows, :], wpg_ref[...], preferred_element_type=F32)
        p_att = jnp.dot(y_att.astype(BF16), wpa_ref[...], preferred_element_type=F32)
        merged = gg_ref[0, rows, :].astype(F32) * p_gla + ga_ref[0, rows, :].astype(F32) * p_att
        y = jnp.dot(merged.astype(BF16), wout_ref[...], preferred_element_type=F32)
        x1 = _layer_norm(alpha * x_ref[0, rows, :] + g1_ref[0] * y, ln_g_ref[...], ln_b_ref[...])
        x1_ref[0, rows, :] = x1
        u2 = x1 * (1.0 + sc2_ref[0]) + sh2_ref[0]
        u2_ref[0, rows, :] = _pack_bf16_pairs(u2)

        u_hi = u2.astype(BF16)
        u_lo = (u2 - u_hi.astype(F32)).astype(BF16)
        logits = (jnp.dot(u_hi, w_hi, preferred_element_type=F32) + jnp.dot(u_lo, w_hi, preferred_element_type=F32)
                  + jnp.dot(u_hi, w_lo, preferred_element_type=F32)) + br_ref[...]
        lane = lax.broadcasted_iota(jnp.int32, logits.shape, 1)
        big = jnp.int32(LANES)
        lg = jnp.where(lane < MOE_GROUPS, logits, NEG)
        gmax = jnp.max(lg, axis=-1, keepdims=True)
        gidx = jnp.min(jnp.where(lg == gmax, lane, big), axis=-1, keepdims=True)
        gval = 1.0 / jnp.sum(jnp.exp(lg - gmax), axis=-1, keepdims=True)
        in_group = (lane >= MOE_GROUPS + gidx * MOE_EXPERTS) & (lane < MOE_GROUPS + (gidx + 1) * MOE_EXPERTS)
        le = jnp.where(in_group, logits, NEG)
        m1 = jnp.max(le, axis=-1, keepdims=True)
        i1 = jnp.min(jnp.where(le == m1, lane, big), axis=-1, keepdims=True)
        le2 = jnp.where(lane == i1, NEG, le)
        m2 = jnp.max(le2, axis=-1, keepdims=True)
        i2 = jnp.min(jnp.where(le2 == m2, lane, big), axis=-1, keepdims=True)
        t = jnp.exp(m2 - m1)
        w1 = 1.0 / (1.0 + t)
        w2 = t * w1

        hit1, hit2 = lane == i1, lane == i2
        onehot = jnp.where(hit1 | hit2, 1.0, 0.0)
        earlier = jnp.dot(ltri_ref[...], onehot.astype(BF16), preferred_element_type=F32) + carry_ref[...]
        rank1 = jnp.sum(jnp.where(hit1, earlier, 0.0), axis=-1, keepdims=True).astype(jnp.int32)
        rank2 = jnp.sum(jnp.where(hit2, earlier, 0.0), axis=-1, keepdims=True).astype(jnp.int32)
        carry_ref[...] = carry_ref[...] + jnp.sum(onehot, axis=0, keepdims=True)
        route = jnp.where(lane == 0, i1 - MOE_GROUPS, jnp.where(lane == 1, i2 - MOE_GROUPS,
                          jnp.where(lane == 2, rank1, jnp.where(lane == 3, rank2, 0))))
        route_ref[0, :, rows] = jnp.transpose(route)[0:ROUTE_ROWS, :]
        ew_ref[0, rows, :] = jnp.where(lane == 0, gval * w1, jnp.where(lane == 1, gval * w2, 0.0))
    cnt_ref[...] = carry_ref[...].astype(jnp.int32)


def _merge(alpha, y_gla, o_groups, lse_groups, g_gla, g_att, x, g1, sc2, sh2, ln_g, ln_b, wpg, wpa, wout, wr, br):
    bsz, s, d = x.shape
    tm = min(ROW_TILE, s)
    assert s % tm == 0
    dilations = tuple(dil for _, dil in DIL_PATTERNS)
    row = lambda w: pl.BlockSpec((1, tm, w), lambda b, i: (b, i, 0))
    sub = lambda dil: pl.BlockSpec((1, dil, tm // dil, DIL_GROUP_WIDTH), lambda b, i: (b, 0, i, 0))
    per_b = pl.BlockSpec((1, 1, d), lambda b, i: (b, 0, 0))
    full = lambda a: pl.BlockSpec(a.shape, lambda b, i: (0,) * a.ndim)
    ln_g2, ln_b2 = ln_g.reshape(1, d), ln_b.reshape(1, d)
    sub_rows = min(MERGE_SUB_ROWS, tm)
    assert tm % sub_rows == 0
    ltri = jnp.asarray(np.tril(np.ones((sub_rows, sub_rows), np.float32), -1), BF16)
    return pl.pallas_call(
        functools.partial(_merge_kernel, alpha, dilations),
        out_shape=[jax.ShapeDtypeStruct((bsz, s, d), F32), jax.ShapeDtypeStruct((bsz, s, d // 2), jnp.int32),
                   jax.ShapeDtypeStruct((bsz, ROUTE_ROWS, s), jnp.int32), jax.ShapeDtypeStruct((bsz, s, LANES), F32),
                   jax.ShapeDtypeStruct((1, LANES), jnp.int32)],
        grid=(bsz, s // tm),
        in_specs=[row(y_gla.shape[-1])] + [sub(dil) for dil in dilations] * 2
                 + [row(d), row(d), row(d), per_b, per_b, per_b, full(ln_g2), full(ln_b2),
                    full(wpg), full(wpa), full(wout), full(wr), full(br), full(ltri)],
        out_specs=[row(d), row(d // 2), pl.BlockSpec((1, ROUTE_ROWS, tm), lambda b, i: (b, 0, i)), row(LANES),
                   pl.BlockSpec((1, LANES), lambda b, i: (0, 0))],
        scratch_shapes=[pltpu.VMEM((2 * DIL_GROUPS, DIL_GROUP_WIDTH // LANES, tm, LANES), F32),
                        pltpu.VMEM((1, LANES), F32)],
        compiler_params=_cparams(("arbitrary", "arbitrary")),
        name="merge_ln1_router",
    )(y_gla, *o_groups, *lse_groups, g_gla, g_att, x, g1, sc2, sh2, ln_g2, ln_b2, wpg, wpa, wout, wr, br, ltri)


def _expert_kernel(run_ref, valid_ref, rexp_ref, used_ref, x_ref, wg_hbm, wu_hbm, wd_hbm, o_ref,
                   wg_f, wu_f, wd_f, wg_s, wu_s, wd_s, sem):
    t = pl.program_id(0)
    n_tiles_used, n_runs = used_ref[0], used_ref[1]
    run = run_ref[t]
    active = t < n_tiles_used
    first_of_run = (t == 0) | (run_ref[jnp.maximum(t - 1, 0)] != run)

    def weight_copies(r):
        e, slot = rexp_ref[r], r % 2
        return [pltpu.make_async_copy(hbm.at[e], buf.at[slot], sem.at[slot, j])
                for j, (hbm, buf) in enumerate(((wg_hbm, wg_f), (wu_hbm, wu_f), (wd_hbm, wd_f)))]

    @pl.when(active & (t == 0))
    def _():
        for cp in weight_copies(0):
            cp.start()

    @pl.when(active & first_of_run)
    def _():
        @pl.when(run + 1 < n_runs)
        def _():
            for cp in weight_copies(run + 1):
                cp.start()

        for cp in weight_copies(run):
            cp.wait()
        slot = run % 2
        wg_s[...] = wg_f[slot].astype(BF16)
        wu_s[...] = wu_f[slot].astype(BF16)
        wd_s[...] = wd_f[slot].astype(BF16)

    n_valid = jnp.where(active, valid_ref[t], 0)
    for r0 in range(0, x_ref.shape[0], EXPERT_BLOCK):
        rows = slice(r0, r0 + EXPERT_BLOCK)

        @pl.when(n_valid > r0)
        def _():
            xt = _unpack_bf16_pairs(x_ref[rows, :]).astype(BF16)
            hg = jnp.dot(xt, wg_s[...], preferred_element_type=F32)
            hu = jnp.dot(xt, wu_s[...], preferred_element_type=F32)
            h = (_silu(hg) * hu).astype(BF16)
            o_ref[rows, :] = _pack_bf16_pairs(jnp.dot(h, wd_s[...], preferred_element_type=F32))

        @pl.when(n_valid <= r0)
        def _():
            o_ref[rows, :] = jnp.zeros((EXPERT_BLOCK, o_ref.shape[1]), o_ref.dtype)


def _expert_ffn(tile_run, tile_valid, run_expert, used, xg, w_gate, w_up, w_down):
    p = xg.shape[0]
    ne, d, ff = w_gate.shape
    tm = EXPERT_TILE
    n_tiles = p // tm
    hbm = pl.BlockSpec(memory_space=pl.ANY)
    grid_spec = pltpu.PrefetchScalarGridSpec(
        num_scalar_prefetch=4,
        grid=(n_tiles,),
        in_specs=[pl.BlockSpec((tm, d // 2), lambda t, *_: (t, 0)), hbm, hbm, hbm],
        out_specs=pl.BlockSpec((tm, d // 2), lambda t, *_: (t, 0)),
        scratch_shapes=[pltpu.VMEM((2, d, ff), F32), pltpu.VMEM((2, d, ff), F32), pltpu.VMEM((2, ff, d), F32),
                        pltpu.VMEM((d, ff), BF16), pltpu.VMEM((d, ff), BF16), pltpu.VMEM((ff, d), BF16),
                        pltpu.SemaphoreType.DMA((2, 3))],
    )
    return pl.pallas_call(
        _expert_kernel,
        out_shape=jax.ShapeDtypeStruct((p, d // 2), jnp.int32),
        grid_spec=grid_spec,
        compiler_params=_cparams(("arbitrary",)),
        name="expert_ffn",
    )(tile_run, tile_valid, run_expert, used, xg, w_gate, w_up, w_down)


def _final_kernel(alpha, x1_ref, ya_ref, yb_ref, ew_ref, g2_ref, ln_g_ref, ln_b_ref, o_ref):
    ew = ew_ref[0]
    y = ew[:, 0:1] * _unpack_bf16_pairs(ya_ref[0]) + ew[:, 1:2] * _unpack_bf16_pairs(yb_ref[0])
    o_ref[0] = _layer_norm(alpha * x1_ref[0] + g2_ref[0] * y, ln_g_ref[...], ln_b_ref[...])


def _final(alpha, x1, ya, yb, ew, g2, ln_g, ln_b):
    bsz, s, d = x1.shape
    tm = min(ROW_TILE, s)
    row = lambda w: pl.BlockSpec((1, tm, w), lambda b, i: (b, i, 0))
    full = lambda a: pl.BlockSpec(a.shape, lambda b, i: (0,) * a.ndim)
    ln_g2, ln_b2 = ln_g.reshape(1, d), ln_b.reshape(1, d)
    return pl.pallas_call(
        functools.partial(_final_kernel, alpha),
        out_shape=jax.ShapeDtypeStruct((bsz, s, d), F32),
        grid=(bsz, s // tm),
        in_specs=[row(d), row(d // 2), row(d // 2), row(LANES), pl.BlockSpec((1, 1, d), lambda b, i: (b, 0, 0)),
                  full(ln_g2), full(ln_b2)],
        out_specs=row(d),
        compiler_params=_cparams(("parallel", "arbitrary")),
        name="combine_ln2",
    )(x1, ya, yb, ew, g2, ln_g2, ln_b2)


SC_CORES = 2
SC_SUBCORES = 16
SC_CHUNK = 64


def _sc_mesh():
    return plsc.VectorSubcoreMesh(core_axis_name="c", subcore_axis_name="s")


def _sc_scatter_rows(rows, dest0, dest1, n_rows):
    n, w = rows.shape
    n_workers = SC_CORES * SC_SUBCORES
    assert n % (n_workers * SC_CHUNK) == 0
    n_chunks = n // (n_workers * SC_CHUNK)
    d0 = dest0.reshape(n // SC_CHUNK, 1, SC_CHUNK)
    d1 = dest1.reshape(n // SC_CHUNK, 1, SC_CHUNK)

    @functools.partial(
        pl.kernel, mesh=_sc_mesh(), out_type=jax.ShapeDtypeStruct((n_rows, w), rows.dtype),
        scratch_types=[pltpu.VMEM((1, SC_CHUNK), jnp.int32), pltpu.VMEM((1, SC_CHUNK), jnp.int32),
                       pltpu.VMEM((SC_CHUNK, w), rows.dtype)])
    def scatter_kernel(rows_hbm, d0_hbm, d1_hbm, out_hbm, i0_v, i1_v, rows_v):
        wid = lax.axis_index("s") * SC_CORES + lax.axis_index("c")

        @pl.loop(0, n_chunks)
        def _(j):
            c = wid * n_chunks + j
            pltpu.sync_copy(rows_hbm.at[pl.ds(c * SC_CHUNK, SC_CHUNK)], rows_v)
            pltpu.sync_copy(d0_hbm.at[c], i0_v)
            pltpu.sync_copy(d1_hbm.at[c], i1_v)
            pltpu.sync_copy(rows_v, out_hbm.at[i0_v.at[0]])
            pltpu.sync_copy(rows_v, out_hbm.at[i1_v.at[0]])

    return scatter_kernel(rows, d0, d1)


def _sc_gather_rows(table, dest0, dest1):
    n = dest0.shape[0]
    w = table.shape[1]
    n_workers = SC_CORES * SC_SUBCORES
    assert n % (n_workers * SC_CHUNK) == 0
    n_chunks = n // (n_workers * SC_CHUNK)
    d0 = dest0.reshape(n // SC_CHUNK, 1, SC_CHUNK)
    d1 = dest1.reshape(n // SC_CHUNK, 1, SC_CHUNK)
    out = jax.ShapeDtypeStruct((n, w), table.dtype)

    @functools.partial(
        pl.kernel, mesh=_sc_mesh(), out_type=(out, out),
        scratch_types=[pltpu.VMEM((1, SC_CHUNK), jnp.int32), pltpu.VMEM((SC_CHUNK, w), table.dtype)])
    def gather_kernel(table_hbm, d0_hbm, d1_hbm, a_hbm, b_hbm, i_v, rows_v):
        wid = lax.axis_index("s") * SC_CORES + lax.axis_index("c")

        @pl.loop(0, n_chunks)
        def _(j):
            c = wid * n_chunks + j
            for d_hbm, o_hbm in ((d0_hbm, a_hbm), (d1_hbm, b_hbm)):
                pltpu.sync_copy(d_hbm.at[c], i_v)
                pltpu.sync_copy(table_hbm.at[i_v.at[0]], rows_v)
                pltpu.sync_copy(rows_v, o_hbm.at[pl.ds(c * SC_CHUNK, SC_CHUNK)])

    return gather_kernel(table, d0, d1)


def _dispatch_plan(route, counts):
    tm = EXPERT_TILE
    e0, e1, r0, r1 = (route[:, j, :].reshape(-1) for j in range(4))
    experts = jnp.arange(MOE_TOTAL, dtype=jnp.int32)
    tiles_per = (counts + tm - 1) // tm
    tile_end = jnp.cumsum(tiles_per)
    pad_start = ((tile_end - tiles_per) * tm).astype(jnp.int32)

    def lookup(e):
        return jnp.sum(jnp.where(e[None, :] == experts[:, None], pad_start[:, None], 0), axis=0)

    dest0, dest1 = lookup(e0) + r0, lookup(e1) + r1
    n_tiles = (2 * e0.size + MOE_TOTAL * tm) // tm
    tile_expert = jnp.minimum(jnp.sum(tile_end[None, :] <= jnp.arange(n_tiles)[:, None], axis=1), MOE_TOTAL - 1)
    nonempty = counts > 0
    run_of_expert = jnp.cumsum(nonempty.astype(jnp.int32)) - 1
    run_expert = jnp.sum(jnp.where(nonempty[None, :] & (run_of_expert[None, :] == experts[:, None]),
                                   experts[None, :], 0), axis=1).astype(jnp.int32)
    of_tile = tile_expert[:, None] == experts[None, :]
    tile_run = jnp.sum(jnp.where(of_tile, run_of_expert[None, :], 0), axis=1).astype(jnp.int32)
    rows_left = (counts + pad_start)[None, :] - jnp.arange(n_tiles)[:, None] * tm
    tile_valid = jnp.clip(jnp.sum(jnp.where(of_tile, rows_left, 0), axis=1), 0, tm).astype(jnp.int32)
    used = jnp.stack([tile_end[-1], jnp.sum(nonempty)]).astype(jnp.int32)
    return dest0, dest1, tile_run, tile_valid, run_expert, used, n_tiles * tm


def _layer(x, c, rel_bias, w_ada, b_ada, w_in, w_gla_gate, b_gla_gate, gla_norm, w_proj_gla, w_proj_attn, w_out,
           ln1_g, ln1_b, w_rg, b_rg, w_re, b_re, w_eg, w_eu, w_ed, ln2_g, ln2_b):
    bsz, s, d = x.shape
    alpha = (2.0 * DEPTH) ** 0.25
    mods = _ada_mods(c, w_ada, b_ada)
    sh1, sc1, g1, sh2, sc2, g2 = [m.reshape(bsz, 1, d) for m in jnp.split(mods, N_MOD, axis=-1)]

    lr0 = d // 2 * 2 + 2 * d
    z = _in_projection(x, sc1, sh1, _prep_in_weight(w_in, lr0))

    y_gla = _gla(z["q_gla"], z["k_gla"], z["v_gla"], z["r_gla"], z["lr"], w_gla_gate, b_gla_gate, gla_norm)

    o_groups, lse_groups = [], []
    for g, (window, dilation) in enumerate(DIL_PATTERNS):
        l = s // dilation
        qg, kg, vg = (z[f"{n}{g}"].reshape(bsz * dilation, l, DIL_GROUP_WIDTH) for n in ("q_att", "k_att", "v_att"))
        table = rel_bias[:, g * DIL_HEADS_PER_GROUP:(g + 1) * DIL_HEADS_PER_GROUP]
        o, lse = _dilated_group_attention(qg, kg, vg, table, window, dilation)
        o_groups.append(o.reshape(bsz, dilation, l, DIL_GROUP_WIDTH))
        lse_groups.append(lse.reshape(bsz, dilation, l, DIL_GROUP_WIDTH))

    wr = jnp.concatenate([w_rg, w_re, jnp.zeros((d, LANES - MOE_GROUPS - MOE_TOTAL), F32)], axis=1)
    br = jnp.concatenate([b_rg, b_re, jnp.zeros((LANES - MOE_GROUPS - MOE_TOTAL,), F32)]).reshape(1, LANES)
    x1, u2, route, ew, cnt = _merge(alpha, y_gla, o_groups, lse_groups, z["g_gla"], z["g_att"], x, g1, sc2, sh2,
                                    ln1_g, ln1_b, w_proj_gla.astype(BF16), w_proj_attn.astype(BF16),
                                    w_out.astype(BF16), wr, br)

    n = bsz * s
    counts = cnt[0, MOE_GROUPS:MOE_GROUPS + MOE_TOTAL]
    dest0, dest1, tile_run, tile_valid, run_expert, used, n_rows = _dispatch_plan(route, counts)
    xg = _sc_scatter_rows(u2.reshape(n, d // 2), dest0, dest1, n_rows)
    ff = w_eg.shape[-1]
    yo = _expert_ffn(tile_run, tile_valid, run_expert, used, xg, w_eg.reshape(MOE_TOTAL, d, ff),
                     w_eu.reshape(MOE_TOTAL, d, ff), w_ed.reshape(MOE_TOTAL, ff, d))
    ya, yb = (y.reshape(bsz, s, d // 2) for y in _sc_gather_rows(yo, dest0, dest1))
    return _final(alpha, x1, ya, yb, ew, g2, ln2_g, ln2_b)


def kernel(x, c, rel_bias, w_ada, b_ada, w_in, w_gla_gate, b_gla_gate, gla_norm, w_proj_gla, w_proj_attn, w_out,
           ln1_g, ln1_b, w_router_group, b_router_group, w_router_expert, b_router_expert, w_exp_gate, w_exp_up,
           w_exp_down, ln2_g, ln2_b):
    assert w_ada.shape[0] == DEPTH
    return _layer(x, c, rel_bias, w_ada[0], b_ada[0], w_in[0:1], w_gla_gate[0], b_gla_gate[0], gla_norm[0],
                  w_proj_gla[0], w_proj_attn[0], w_out[0], ln1_g[0], ln1_b[0], w_router_group[0],
                  b_router_group[0], w_router_expert[0], b_router_expert[0], w_exp_gate[0], w_exp_up[0],
                  w_exp_down[0], ln2_g[0], ln2_b[0])
```

```python
import functools
import math

import numpy as np
import jax
import jax.numpy as jnp
from jax import lax
from jax.experimental import pallas as pl
from jax.experimental.pallas import tpu as pltpu
from jax.experimental.pallas import tpu_sc as plsc

F32 = jnp.float32
BF16 = jnp.bfloat16

N_MOD = 6
GLA_HEADS = 4
GLA_LOWRANK = 16
GLA_TAU = 16.0
GLA_CHUNK = 64
DIL_PATTERNS = ((128, 1), (512, 4), (2048, 16))
DIL_GROUPS = len(DIL_PATTERNS)
DIL_HEADS_PER_GROUP = 8
DIL_HEAD_DIM = 64
DIL_GROUP_WIDTH = DIL_HEADS_PER_GROUP * DIL_HEAD_DIM
DIL_BLOCK = 128
REL_BUCKETS = 32
REL_MAX_DIST = 2048
MOE_GROUPS = 4
MOE_EXPERTS = 8
MOE_TOTAL = MOE_GROUPS * MOE_EXPERTS
LN_EPS = 1e-5
DEPTH = 1

LANES = 128
VMEM_LIMIT = 56 * 1024 * 1024
LOG2E = 1.4426950408889634
LN2 = 0.6931471805599453
NEG = -1e30
ROW_TILE = 512
EXPERT_TILE = 512
EXPERT_BLOCK = 256
GLA_STEP_CHUNKS = 8
ATT_STEP_BLOCKS = 4
MERGE_SUB_ROWS = 512
ROUTE_ROWS = 8

HIGHEST = lax.Precision.HIGHEST
NT_DIMS = (((1,), (1,)), ((), ()))
TN_DIMS = (((0,), (0,)), ((), ()))


def _cparams(sem):
    return pltpu.CompilerParams(dimension_semantics=sem, vmem_limit_bytes=VMEM_LIMIT)


def _sigmoid(x):
    return 0.5 * jnp.tanh(0.5 * x) + 0.5


def _silu(x):
    return x * _sigmoid(x)


def _layer_norm(x, g, b):
    mu = jnp.mean(x, axis=-1, keepdims=True)
    xc = x - mu
    var = jnp.mean(xc * xc, axis=-1, keepdims=True)
    return xc * lax.rsqrt(var + LN_EPS) * g + b


def _pack_bf16_pairs(x):
    w = x.shape[1] // 2
    lo = lax.bitcast_convert_type(x[:, :w].astype(BF16).astype(F32), jnp.uint32) >> 16
    hi = lax.bitcast_convert_type(x[:, w:].astype(BF16).astype(F32), jnp.uint32) & jnp.uint32(0xFFFF0000)
    return lax.bitcast_convert_type(lo | hi, jnp.int32)


def _unpack_bf16_pairs(p):
    u = lax.bitcast_convert_type(p, jnp.uint32)
    lo = lax.bitcast_convert_type(u << 16, F32)
    hi = lax.bitcast_convert_type(u & jnp.uint32(0xFFFF0000), F32)
    return jnp.concatenate([lo, hi], axis=1)


def _split3(x):
    hi = x.astype(BF16)
    r1 = x - hi.astype(F32)
    mid = r1.astype(BF16)
    lo = (r1 - mid.astype(F32)).astype(BF16)
    return hi, mid, lo


def _mods_kernel(c_ref, w_ref, b_ref, o_ref):
    a = _silu(c_ref[...])
    o_ref[...] = jnp.dot(a, w_ref[...], precision=HIGHEST, preferred_element_type=F32) + b_ref[...]


def _ada_mods(c, w, b):
    bsz, d = c.shape
    n = w.shape[1]
    tn = 1536
    assert n % tn == 0
    return pl.pallas_call(
        _mods_kernel,
        out_shape=jax.ShapeDtypeStruct((bsz, n), F32),
        grid=(n // tn,),
        in_specs=[pl.BlockSpec((bsz, d), lambda j: (0, 0)),
                  pl.BlockSpec((d, tn), lambda j: (0, j)),
                  pl.BlockSpec((1, tn), lambda j: (0, j))],
        out_specs=pl.BlockSpec((bsz, tn), lambda j: (0, j)),
        compiler_params=_cparams(("arbitrary",)),
        name="ada_mods",
    )(c, w, b.reshape(1, n))


def _proj_pieces(d_model):
    dk = d_model // 2
    pieces = [("q_gla", dk, "scale_q_gla"), ("k_gla", dk, None), ("v_gla", d_model, None), ("r_gla", d_model, "silu")]
    for name, post in (("q_att", "scale_q_att"), ("k_att", None), ("v_att", None)):
        for g, (_, dilation) in enumerate(DIL_PATTERNS):
            pieces.append((f"{name}{g}", DIL_GROUP_WIDTH, (post, dilation)))
    pieces += [("g_gla", d_model, "sigmoid"), ("g_att", d_model, "sigmoid"), ("lr", LANES, "lowrank")]
    return tuple(pieces)


def _wprep_kernel(lr0, w_ref, o_ref):
    main = w_ref.shape[1] - GLA_LOWRANK
    step = 512
    for c0 in range(0, lr0, step):
        o_ref[:, c0:c0 + step] = w_ref[:, c0:c0 + step].astype(BF16)
    for c0 in range(lr0, main, step):
        o_ref[:, c0:c0 + step] = w_ref[:, c0 + GLA_LOWRANK:c0 + GLA_LOWRANK + step].astype(BF16)
    tail = w_ref[:, lr0:lr0 + LANES]
    lane = lax.broadcasted_iota(jnp.int32, tail.shape, 1)
    o_ref[:, main:main + LANES] = jnp.where(lane < GLA_LOWRANK, tail, 0.0).astype(BF16)


def _prep_in_weight(w_in, lr0):
    _, d, n_in = w_in.shape
    n_out = n_in - GLA_LOWRANK + LANES
    assert lr0 % 512 == 0 and (n_in - GLA_LOWRANK - lr0) % 512 == 0
    tr = 128
    return pl.pallas_call(
        functools.partial(_wprep_kernel, lr0),
        out_shape=jax.ShapeDtypeStruct((d, n_out), BF16),
        grid=(d // tr,),
        in_specs=[pl.BlockSpec((None, tr, n_in), lambda i: (0, i, 0))],
        out_specs=pl.BlockSpec((tr, n_out), lambda i: (i, 0)),
        compiler_params=_cparams(("parallel",)),
        name="prep_in_weight",
    )(w_in)


def _proj_kernel(pieces, head_k, x_ref, sc_ref, sh_ref, w_ref, *refs):
    out_refs, stage_ref = refs[:-1], refs[-1]
    tm = x_ref.shape[1]
    u = (x_ref[0] * (1.0 + sc_ref[0]) + sh_ref[0]).astype(BF16)
    off = 0
    for (name, width, post), o_ref in zip(pieces, out_refs):
        chunk = min(width, 512)
        for c0 in range(0, width, chunk):
            acc = jnp.dot(u, w_ref[:, off + c0:off + c0 + chunk], preferred_element_type=F32)
            if post == "silu":
                acc = _silu(acc)
            elif post == "sigmoid":
                acc = _sigmoid(acc)
            elif post == "scale_q_gla":
                acc = acc * (head_k ** -0.5)
            if post == "lowrank":
                o_ref[0] = acc[:, :GLA_LOWRANK]
            elif isinstance(post, tuple):
                scale, dilation = post
                if scale is not None:
                    acc = acc * (DIL_HEAD_DIM ** -0.5 * LOG2E)
                if dilation == 1:
                    o_ref[0, 0] = acc.astype(o_ref.dtype)
                else:
                    for t in range(width // LANES):
                        stage_ref[t] = acc[:, t * LANES:(t + 1) * LANES]
                    for r in range(dilation):
                        for t in range(width // LANES):
                            o_ref[0, r, :, t * LANES:(t + 1) * LANES] = stage_ref[
                                t, pl.ds(r, tm // dilation, stride=dilation), :].astype(o_ref.dtype)
            else:
                o_ref[0, :, c0:c0 + chunk] = acc.astype(o_ref.dtype)
        off += width


def _in_projection(x, sc1, sh1, w_perm):
    bsz, s, d = x.shape
    pieces = _proj_pieces(d)
    assert sum(p[1] for p in pieces) == w_perm.shape[1]
    tm = min(ROW_TILE, s)
    assert s % tm == 0
    head_k = (d // 2) // GLA_HEADS
    out_shape, out_specs = [], []
    for name, width, post in pieces:
        if post == "lowrank":
            out_shape.append(jax.ShapeDtypeStruct((bsz, s, GLA_LOWRANK), F32))
            out_specs.append(pl.BlockSpec((1, tm, GLA_LOWRANK), lambda b, i: (b, i, 0)))
        elif isinstance(post, tuple):
            dil = post[1]
            assert tm % (dil * 16) == 0
            out_shape.append(jax.ShapeDtypeStruct((bsz, dil, s // dil, width), BF16))
            out_specs.append(pl.BlockSpec((1, dil, tm // dil, width), lambda b, i: (b, 0, i, 0)))
        else:
            out_shape.append(jax.ShapeDtypeStruct((bsz, s, width), BF16))
            out_specs.append(pl.BlockSpec((1, tm, width), lambda b, i: (b, i, 0)))
    outs = pl.pallas_call(
        functools.partial(_proj_kernel, pieces, head_k),
        out_shape=out_shape,
        grid=(bsz, s // tm),
        in_specs=[pl.BlockSpec((1, tm, d), lambda b, i: (b, i, 0)),
                  pl.BlockSpec((1, 1, d), lambda b, i: (b, 0, 0)),
                  pl.BlockSpec((1, 1, d), lambda b, i: (b, 0, 0)),
                  pl.BlockSpec(w_perm.shape, lambda b, i: (0, 0), pipeline_mode=pl.Buffered(1))],
        out_specs=out_specs,
        scratch_shapes=[pltpu.VMEM((DIL_GROUP_WIDTH // LANES, tm, LANES), F32)],
        compiler_params=_cparams(("parallel", "arbitrary")),
        name="in_projection",
    )(x, sc1, sh1, w_perm)
    return dict(zip([p[0] for p in pieces], outs))


def _gla_kernel(n_chunks, head_k, head_v, q_ref, k_ref, v_ref, r_ref, lr_ref, wg_ref, bg_ref, ng_ref, o_ref,
                state_ref):
    @pl.when(pl.program_id(1) == 0)
    def _():
        state_ref[...] = jnp.zeros_like(state_ref)

    c = GLA_CHUNK
    row = lax.broadcasted_iota(jnp.int32, (c, c), 0)
    col = lax.broadcasted_iota(jnp.int32, (c, c), 1)
    causal = row >= col
    tril = causal.astype(BF16)
    mid = c // 2 - 1
    gate_in = jnp.dot(lr_ref[0], wg_ref[...], precision=HIGHEST, preferred_element_type=F32) + bg_ref[...]
    g_all = (jnp.minimum(gate_in, 0.0) - jnp.log(1.0 + jnp.exp(-jnp.abs(gate_in)))) * (1.0 / GLA_TAU)
    g_hi, g_mid, g_lo = _split3(g_all)
    for ci in range(n_chunks):
        rows = slice(ci * c, (ci + 1) * c)
        bc = (jnp.dot(tril, g_hi[rows], preferred_element_type=F32)
              + jnp.dot(tril, g_mid[rows], preferred_element_type=F32)
              + jnp.dot(tril, g_lo[rows], preferred_element_type=F32))
        b_mid = bc[mid:mid + 1, :]
        b_last = bc[c - 1:c, :]
        qf = q_ref[0, rows, :].astype(F32)
        kf = k_ref[0, rows, :].astype(F32)
        q_in = (qf * jnp.exp(bc - b_mid)).astype(BF16)
        k_in = (kf * jnp.exp(b_mid - bc)).astype(BF16)
        q_st = (qf * jnp.exp(bc)).astype(BF16)
        k_st = (kf * jnp.exp(b_last - bc)).astype(BF16)
        dec = jnp.exp(b_last)
        for h in range(GLA_HEADS):
            ks = slice(h * head_k, (h + 1) * head_k)
            vs = slice(h * head_v, (h + 1) * head_v)
            vh = v_ref[0, rows, vs]
            att = lax.dot_general(q_in[:, ks], k_in[:, ks], NT_DIMS, preferred_element_type=F32)
            att = jnp.where(causal, att, 0.0).astype(BF16)
            st = state_ref[h]
            o = jnp.dot(att, vh, preferred_element_type=F32)
            o = o + lax.dot_general(q_st[:, ks], st.astype(BF16), NT_DIMS, preferred_element_type=F32)
            kv_t = lax.dot_general(vh, k_st[:, ks], TN_DIMS, preferred_element_type=F32)
            state_ref[h] = st * dec[:, ks] + kv_t
            ms = jnp.mean(o * o, axis=-1, keepdims=True)
            o = o * lax.rsqrt(ms + LN_EPS) * ng_ref[:, vs] * r_ref[0, rows, vs].astype(F32)
            o_ref[0, rows, vs] = o.astype(o_ref.dtype)


def _gla(q, k, v, r_silu, lr, w_gate, b_gate, norm_g):
    bsz, s, dk = q.shape
    dv = v.shape[-1]
    head_k, head_v = dk // GLA_HEADS, dv // GLA_HEADS
    n_chunks = min(GLA_STEP_CHUNKS, s // GLA_CHUNK)
    ct = GLA_CHUNK * n_chunks
    assert s % ct == 0
    row_spec = lambda w: pl.BlockSpec((1, ct, w), lambda b, i: (b, i, 0))
    full = lambda a: pl.BlockSpec(a.shape, lambda b, i: (0,) * a.ndim)
    bg = b_gate.reshape(1, dk)
    ng = norm_g.reshape(1, dv)
    return pl.pallas_call(
        functools.partial(_gla_kernel, n_chunks, head_k, head_v),
        out_shape=jax.ShapeDtypeStruct((bsz, s, dv), BF16),
        grid=(bsz, s // ct),
        in_specs=[row_spec(dk), row_spec(dk), row_spec(dv), row_spec(dv), row_spec(GLA_LOWRANK),
                  full(w_gate), full(bg), full(ng)],
        out_specs=row_spec(dv),
        scratch_shapes=[pltpu.VMEM((GLA_HEADS, head_v, head_k), F32)],
        compiler_params=_cparams(("parallel", "arbitrary")),
        name="gla",
    )(q, k, v, r_silu, lr, w_gate, bg, ng)


def _t5_bucket_np(dist):
    exact = REL_BUCKETS // 2
    d = np.maximum(dist, 1).astype(np.float32)
    large = exact + (np.log(d / np.float32(exact)) / np.float32(math.log(REL_MAX_DIST / exact))
                     * np.float32(REL_BUCKETS - exact)).astype(np.int32)
    large = np.minimum(large, REL_BUCKETS - 1)
    return np.where(dist < exact, dist, large).astype(np.int32)


def _band_tables(window, dilation):
    qi = np.arange(DIL_BLOCK)[:, None]
    kj = np.arange(2 * DIL_BLOCK)[None, :]
    m = qi + DIL_BLOCK - kj
    n_steps = window // dilation
    band = (m >= 0) & (m <= n_steps)
    bucket = _t5_bucket_np(np.clip(m, 0, n_steps) * dilation)
    return np.where(band, bucket, -1).astype(np.int32)


def _attn_kernel(nq, table_ref, bucket_ref, q_ref, kp_ref, kc_ref, vp_ref, vc_ref, o_ref, lse_ref,
                 bias_ref, p_ref):
    i = pl.program_id(1)
    blk = DIL_BLOCK
    hpg = DIL_HEADS_PER_GROUP
    n_pairs = hpg // 2

    @pl.when((pl.program_id(0) == 0) & (i == 0))
    def _():
        bucket = bucket_ref[...]
        for h in range(hpg):
            acc = jnp.full(bucket.shape, NEG, F32)
            for bkt in range(REL_BUCKETS):
                acc = jnp.where(bucket == bkt, table_ref[bkt, h] * LOG2E, acc)
            bias_ref[h * blk:(h + 1) * blk, :] = acc

    lane = lax.broadcasted_iota(jnp.int32, (blk, LANES), 1)
    low = lane < DIL_HEAD_DIM
    ones_rhs = jnp.ones((2 * blk, LANES), BF16)

    def windows(ref_p, ref_c, qb, cols):
        if qb == 0:
            return jnp.concatenate([ref_p[0, :, cols], ref_c[0, 0:blk, cols]], axis=0)
        return ref_c[0, (qb - 1) * blk:(qb + 1) * blk, cols]

    key_lane = lax.broadcasted_iota(jnp.int32, (1, 2 * blk), 1)
    no_prev = jnp.where((key_lane < blk) & (i == 0), NEG, 0.0)

    mxs = {}
    for qb in range(nq):
        rows = slice(qb * blk, (qb + 1) * blk)
        for hp in range(n_pairs):
            cols = slice(hp * LANES, (hp + 1) * LANES)
            qp = q_ref[0, rows, cols]
            zero = jnp.zeros_like(qp)
            qq = jnp.concatenate([jnp.where(low, qp, zero), jnp.where(low, zero, qp)], axis=0)
            keys = windows(kp_ref, kc_ref, qb, cols)
            s = lax.dot_general(qq, keys, NT_DIMS, preferred_element_type=F32) + bias_ref[2 * hp * blk:(2 * hp + 2) * blk, :]
            if qb == 0:
                s = s + no_prev
            mx = jnp.max(s, axis=-1, keepdims=True)
            base = (qb * n_pairs + hp) * 2 * blk
            p_ref[base:base + 2 * blk, :] = jnp.exp2(s - mx).astype(BF16)
            mxs[qb, hp] = mx

    for qb in range(nq):
        rows = slice(qb * blk, (qb + 1) * blk)
        for hp in range(n_pairs):
            cols = slice(hp * LANES, (hp + 1) * LANES)
            vals = windows(vp_ref, vc_ref, qb, cols)
            rhs = jnp.concatenate([vals, ones_rhs], axis=1)
            base = (qb * n_pairs + hp) * 2 * blk
            res = jnp.dot(p_ref[base:base + 2 * blk, :], rhs, preferred_element_type=F32)
            num = jnp.where(low, res[0:blk, 0:LANES], res[blk:2 * blk, 0:LANES])
            den = jnp.where(low, res[0:blk, LANES:], res[blk:2 * blk, LANES:])
            mx = jnp.where(low, mxs[qb, hp][0:blk], mxs[qb, hp][blk:2 * blk])
            o_ref[0, rows, cols] = (num / den).astype(o_ref.dtype)
            lse_ref[0, rows, cols] = (mx + jnp.log2(den)) * LN2


def _dilated_group_attention(q, k, v, table, window, dilation):
    bb, l, w = q.shape
    nq = ATT_STEP_BLOCKS if l % (ATT_STEP_BLOCKS * DIL_BLOCK) == 0 else 2
    assert l % (nq * DIL_BLOCK) == 0
    steps = l // (nq * DIL_BLOCK)
    bucket = jnp.asarray(_band_tables(window, dilation))
    cur = pl.BlockSpec((1, nq * DIL_BLOCK, w), lambda b, i: (b, i, 0))
    prev = pl.BlockSpec((1, DIL_BLOCK, w), lambda b, i: (b, jnp.maximum(nq * i - 1, 0), 0))
    rows_all = nq * DIL_HEADS_PER_GROUP * DIL_BLOCK
    return pl.pallas_call(
        functools.partial(_attn_kernel, nq),
        out_shape=[jax.ShapeDtypeStruct((bb, l, w), BF16), jax.ShapeDtypeStruct((bb, l, w), F32)],
        grid=(bb, steps),
        in_specs=[pl.BlockSpec(memory_space=pltpu.SMEM),
                  pl.BlockSpec(bucket.shape, lambda b, i: (0, 0)),
                  cur, prev, cur, prev, cur],
        out_specs=[cur, cur],
        scratch_shapes=[pltpu.VMEM((DIL_HEADS_PER_GROUP * DIL_BLOCK, 2 * DIL_BLOCK), F32),
                        pltpu.VMEM((rows_all, 2 * DIL_BLOCK), BF16)],
        compiler_params=_cparams(("arbitrary", "arbitrary")),
        name=f"dilated_attn_d{dilation}",
    )(table, bucket, q, k, k, v, v)


def _merge_kernel(alpha, dilations, ygla_ref, o0_ref, o1_ref, o2_ref, l0_ref, l1_ref, l2_ref, gg_ref, ga_ref, x_ref,
                  g1_ref, sc2_ref, sh2_ref, ln_g_ref, ln_b_ref, wpg_ref, wpa_ref, wout_ref, wr_ref, br_ref, ltri_ref,
                  x1_ref, u2_ref, route_ref, ew_ref, cnt_ref, stage_ref, carry_ref):
    tm = x_ref.shape[1]

    @pl.when((pl.program_id(0) == 0) & (pl.program_id(1) == 0))
    def _():
        carry_ref[...] = jnp.zeros_like(carry_ref)

    n_lt = DIL_GROUP_WIDTH // LANES
    group_refs = tuple(zip((l0_ref, l1_ref, l2_ref), (o0_ref, o1_ref, o2_ref), dilations))
    for gi, (l_ref, o_ref, dil) in enumerate(group_refs):
        if dil > 1:
            for slot, ref in ((2 * gi, l_ref), (2 * gi + 1, o_ref)):
                for r in range(dil):
                    for t in range(n_lt):
                        stage_ref[slot, t, pl.ds(r, tm // dil, stride=dil), :] = ref[
                            0, r, :, t * LANES:(t + 1) * LANES].astype(F32)

    wr = wr_ref[...]
    w_hi = wr.astype(BF16)
    w_lo = (wr - w_hi.astype(F32)).astype(BF16)
    sub = ltri_ref.shape[0]
    for rows in (slice(r0, r0 + sub) for r0 in range(0, tm, sub)):
        def natural(ref, dil, slot):
            if dil == 1:
                return ref[0, 0, rows, :].astype(F32)
            return jnp.concatenate([stage_ref[slot, t, rows, :] for t in range(n_lt)], axis=1)

        lses = [natural(l_ref, dil, 2 * gi) for gi, (l_ref, _, dil) in enumerate(group_refs)]
        outs = [natural(o_ref, dil, 2 * gi + 1) for gi, (_, o_ref, dil) in enumerate(group_refs)]
        lm = jnp.maximum(jnp.maximum(lses[0], lses[1]), lses[2])
        es = [jnp.exp(l - lm) for l in lses]
        y_att = (es[0] * outs[0] + es[1] * outs[1] + es[2] * outs[2]) / (es[0] + es[1] + es[2])

        p_gla = jnp.dot(ygla_ref[0, rows, :], wpg_ref[...], preferred_element_type=F32)
        p_att = jnp.dot(y_att.astype(BF16), wpa_ref[...], preferred_element_type=F32)
        merged = gg_ref[0, rows, :].astype(F32) * p_gla + ga_ref[0, rows, :].astype(F32) * p_att
        y = jnp.dot(merged.astype(BF16), wout_ref[...], preferred_element_type=F32)
        x1 = _layer_norm(alpha * x_ref[0, rows, :] + g1_ref[0] * y, ln_g_ref[...], ln_b_ref[...])
        x1_ref[0, rows, :] = x1
        u2 = x1 * (1.0 + sc2_ref[0]) + sh2_ref[0]
        u2_ref[0, rows, :] = _pack_bf16_pairs(u2)

        u_hi = u2.astype(BF16)
        u_lo = (u2 - u_hi.astype(F32)).astype(BF16)
        logits = (jnp.dot(u_hi, w_hi, preferred_element_type=F32) + jnp.dot(u_lo, w_hi, preferred_element_type=F32)
                  + jnp.dot(u_hi, w_lo, preferred_element_type=F32)) + br_ref[...]
        lane = lax.broadcasted_iota(jnp.int32, logits.shape, 1)
        big = jnp.int32(LANES)
        lg = jnp.where(lane < MOE_GROUPS, logits, NEG)
        gmax = jnp.max(lg, axis=-1, keepdims=True)
        gidx = jnp.min(jnp.where(lg == gmax, lane, big), axis=-1, keepdims=True)
        gval = 1.0 / jnp.sum(jnp.exp(lg - gmax), axis=-1, keepdims=True)
        in_group = (lane >= MOE_GROUPS + gidx * MOE_EXPERTS) & (lane < MOE_GROUPS + (gidx + 1) * MOE_EXPERTS)
        le = jnp.where(in_group, logits, NEG)
        m1 = jnp.max(le, axis=-1, keepdims=True)
        i1 = jnp.min(jnp.where(le == m1, lane, big), axis=-1, keepdims=True)
        le2 = jnp.where(lane == i1, NEG, le)
        m2 = jnp.max(le2, axis=-1, keepdims=True)
        i2 = jnp.min(jnp.where(le2 == m2, lane, big), axis=-1, keepdims=True)
        t = jnp.exp(m2 - m1)
        w1 = 1.0 / (1.0 + t)
        w2 = t * w1

        hit1, hit2 = lane == i1, lane == i2
        onehot = jnp.where(hit1 | hit2, 1.0, 0.0)
        earlier = jnp.dot(ltri_ref[...], onehot.astype(BF16), preferred_element_type=F32) + carry_ref[...]
        rank1 = jnp.sum(jnp.where(hit1, earlier, 0.0), axis=-1, keepdims=True).astype(jnp.int32)
        rank2 = jnp.sum(jnp.where(hit2, earlier, 0.0), axis=-1, keepdims=True).astype(jnp.int32)
        carry_ref[...] = carry_ref[...] + jnp.sum(onehot, axis=0, keepdims=True)
        route = jnp.where(lane == 0, i1 - MOE_GROUPS, jnp.where(lane == 1, i2 - MOE_GROUPS,
                          jnp.where(lane == 2, rank1, jnp.where(lane == 3, rank2, 0))))
        route_ref[0, :, rows] = jnp.transpose(route)[0:ROUTE_ROWS, :]
        ew_ref[0, rows, :] = jnp.where(lane == 0, gval * w1, jnp.where(lane == 1, gval * w2, 0.0))
    cnt_ref[...] = carry_ref[...].astype(jnp.int32)


def _merge(alpha, y_gla, o_groups, lse_groups, g_gla, g_att, x, g1, sc2, sh2, ln_g, ln_b, wpg, wpa, wout, wr, br):
    bsz, s, d = x.shape
    tm = min(ROW_TILE, s)
    assert s % tm == 0
    dilations = tuple(dil for _, dil in DIL_PATTERNS)
    row = lambda w: pl.BlockSpec((1, tm, w), lambda b, i: (b, i, 0))
    sub = lambda dil: pl.BlockSpec((1, dil, tm // dil, DIL_GROUP_WIDTH), lambda b, i: (b, 0, i, 0))
    per_b = pl.BlockSpec((1, 1, d), lambda b, i: (b, 0, 0))
    full = lambda a: pl.BlockSpec(a.shape, lambda b, i: (0,) * a.ndim)
    ln_g2, ln_b2 = ln_g.reshape(1, d), ln_b.reshape(1, d)
    sub_rows = min(MERGE_SUB_ROWS, tm)
    assert tm % sub_rows == 0
    ltri = jnp.asarray(np.tril(np.ones((sub_rows, sub_rows), np.float32), -1), BF16)
    return pl.pallas_call(
        functools.partial(_merge_kernel, alpha, dilations),
        out_shape=[jax.ShapeDtypeStruct((bsz, s, d), F32), jax.ShapeDtypeStruct((bsz, s, d // 2), jnp.int32),
                   jax.ShapeDtypeStruct((bsz, ROUTE_ROWS, s), jnp.int32), jax.ShapeDtypeStruct((bsz, s, LANES), F32),
                   jax.ShapeDtypeStruct((1, LANES), jnp.int32)],
        grid=(bsz, s // tm),
        in_specs=[row(y_gla.shape[-1])] + [sub(dil) for dil in dilations] * 2
                 + [row(d), row(d), row(d), per_b, per_b, per_b, full(ln_g2), full(ln_b2),
                    full(wpg), full(wpa), full(wout), full(wr), full(br), full(ltri)],
        out_specs=[row(d), row(d // 2), pl.BlockSpec((1, ROUTE_ROWS, tm), lambda b, i: (b, 0, i)), row(LANES),
                   pl.BlockSpec((1, LANES), lambda b, i: (0, 0))],
        scratch_shapes=[pltpu.VMEM((2 * DIL_GROUPS, DIL_GROUP_WIDTH // LANES, tm, LANES), F32),
                        pltpu.VMEM((1, LANES), F32)],
        compiler_params=_cparams(("arbitrary", "arbitrary")),
        name="merge_ln1_router",
    )(y_gla, *o_groups, *lse_groups, g_gla, g_att, x, g1, sc2, sh2, ln_g2, ln_b2, wpg, wpa, wout, wr, br, ltri)


def _expert_kernel(run_ref, valid_ref, rexp_ref, used_ref, x_ref, wg_hbm, wu_hbm, wd_hbm, o_ref,
                   wg_f, wu_f, wd_f, wg_s, wu_s, wd_s, sem):
    t = pl.program_id(0)
    n_tiles_used, n_runs = used_ref[0], used_ref[1]
    run = run_ref[t]
    active = t < n_tiles_used
    first_of_run = (t == 0) | (run_ref[jnp.maximum(t - 1, 0)] != run)

    def weight_copies(r):
        e, slot = rexp_ref[r], r % 2
        return [pltpu.make_async_copy(hbm.at[e], buf.at[slot], sem.at[slot, j])
                for j, (hbm, buf) in enumerate(((wg_hbm, wg_f), (wu_hbm, wu_f), (wd_hbm, wd_f)))]

    @pl.when(active & (t == 0))
    def _():
        for cp in weight_copies(0):
            cp.start()

    @pl.when(active & first_of_run)
    def _():
        @pl.when(run + 1 < n_runs)
        def _():
            for cp in weight_copies(run + 1):
                cp.start()

        for cp in weight_copies(run):
            cp.wait()
        slot = run % 2
        wg_s[...] = wg_f[slot].astype(BF16)
        wu_s[...] = wu_f[slot].astype(BF16)
        wd_s[...] = wd_f[slot].astype(BF16)

    n_valid = jnp.where(active, valid_ref[t], 0)
    for r0 in range(0, x_ref.shape[0], EXPERT_BLOCK):
        rows = slice(r0, r0 + EXPERT_BLOCK)

        @pl.when(n_valid > r0)
        def _():
            xt = _unpack_bf16_pairs(x_ref[rows, :]).astype(BF16)
            hg = jnp.dot(xt, wg_s[...], preferred_element_type=F32)
            hu = jnp.dot(xt, wu_s[...], preferred_element_type=F32)
            h = (_silu(hg) * hu).astype(BF16)
            o_ref[rows, :] = _pack_bf16_pairs(jnp.dot(h, wd_s[...], preferred_element_type=F32))

        @pl.when(n_valid <= r0)
        def _():
            o_ref[rows, :] = jnp.zeros((EXPERT_BLOCK, o_ref.shape[1]), o_ref.dtype)


def _expert_ffn(tile_run, tile_valid, run_expert, used, xg, w_gate, w_up, w_down):
    p = xg.shape[0]
    ne, d, ff = w_gate.shape
    tm = EXPERT_TILE
    n_tiles = p // tm
    hbm = pl.BlockSpec(memory_space=pl.ANY)
    grid_spec = pltpu.PrefetchScalarGridSpec(
        num_scalar_prefetch=4,
        grid=(n_tiles,),
        in_specs=[pl.BlockSpec((tm, d // 2), lambda t, *_: (t, 0)), hbm, hbm, hbm],
        out_specs=pl.BlockSpec((tm, d // 2), lambda t, *_: (t, 0)),
        scratch_shapes=[pltpu.VMEM((2, d, ff), F32), pltpu.VMEM((2, d, ff), F32), pltpu.VMEM((2, ff, d), F32),
                        pltpu.VMEM((d, ff), BF16), pltpu.VMEM((d, ff), BF16), pltpu.VMEM((ff, d), BF16),
                        pltpu.SemaphoreType.DMA((2, 3))],
    )
    return pl.pallas_call(
        _expert_kernel,
        out_shape=jax.ShapeDtypeStruct((p, d // 2), jnp.int32),
        grid_spec=grid_spec,
        compiler_params=_cparams(("arbitrary",)),
        name="expert_ffn",
    )(tile_run, tile_valid, run_expert, used, xg, w_gate, w_up, w_down)


def _final_kernel(alpha, x1_ref, ya_ref, yb_ref, ew_ref, g2_ref, ln_g_ref, ln_b_ref, o_ref):
    ew = ew_ref[0]
    y = ew[:, 0:1] * _unpack_bf16_pairs(ya_ref[0]) + ew[:, 1:2] * _unpack_bf16_pairs(yb_ref[0])
    o_ref[0] = _layer_norm(alpha * x1_ref[0] + g2_ref[0] * y, ln_g_ref[...], ln_b_ref[...])


def _final(alpha, x1, ya, yb, ew, g2, ln_g, ln_b):
    bsz, s, d = x1.shape
    tm = min(ROW_TILE, s)
    row = lambda w: pl.BlockSpec((1, tm, w), lambda b, i: (b, i, 0))
    full = lambda a: pl.BlockSpec(a.shape, lambda b, i: (0,) * a.ndim)
    ln_g2, ln_b2 = ln_g.reshape(1, d), ln_b.reshape(1, d)
    return pl.pallas_call(
        functools.partial(_final_kernel, alpha),
        out_shape=jax.ShapeDtypeStruct((bsz, s, d), F32),
        grid=(bsz, s // tm),
        in_specs=[row(d), row(d // 2), row(d // 2), row(LANES), pl.BlockSpec((1, 1, d), lambda b, i: (b, 0, 0)),
                  full(ln_g2), full(ln_b2)],
        out_specs=row(d),
        compiler_params=_cparams(("parallel", "arbitrary")),
        name="combine_ln2",
    )(x1, ya, yb, ew, g2, ln_g2, ln_b2)


SC_CORES = 2
SC_SUBCORES = 16
SC_CHUNK = 64


def _sc_mesh():
    return plsc.VectorSubcoreMesh(core_axis_name="c", subcore_axis_name="s")


def _sc_scatter_rows(rows, dest0, dest1, n_rows):
    n, w = rows.shape
    n_workers = SC_CORES * SC_SUBCORES
    assert n % (n_workers * SC_CHUNK) == 0
    n_chunks = n // (n_workers * SC_CHUNK)
    d0 = dest0.reshape(n // SC_CHUNK, 1, SC_CHUNK)
    d1 = dest1.reshape(n // SC_CHUNK, 1, SC_CHUNK)

    @functools.partial(
        pl.kernel, mesh=_sc_mesh(), out_type=jax.ShapeDtypeStruct((n_rows, w), rows.dtype),
        scratch_types=[pltpu.VMEM((n_chunks, 1, SC_CHUNK), jnp.int32), pltpu.VMEM((n_chunks, 1, SC_CHUNK), jnp.int32),
                       pltpu.VMEM((2, SC_CHUNK, w), rows.dtype),
                       pltpu.SemaphoreType.DMA((2,)), pltpu.SemaphoreType.DMA((2, 2))])
    def scatter_kernel(rows_hbm, d0_hbm, d1_hbm, out_hbm, i0_v, i1_v, rows_v, read_sem, scat_sem):
        wid = lax.axis_index("s") * SC_CORES + lax.axis_index("c")
        first = wid * n_chunks
        pltpu.sync_copy(d0_hbm.at[pl.ds(first, n_chunks)], i0_v)
        pltpu.sync_copy(d1_hbm.at[pl.ds(first, n_chunks)], i1_v)

        def read(j):
            return pltpu.make_async_copy(rows_hbm.at[pl.ds((first + j) * SC_CHUNK, SC_CHUNK)], rows_v.at[j % 2],
                                         read_sem.at[j % 2])

        def scatters(j):
            return [pltpu.make_async_copy(rows_v.at[j % 2], out_hbm.at[idx.at[j].at[0]], scat_sem.at[j % 2, k])
                    for k, idx in enumerate((i0_v, i1_v))]

        read(0).start()
        for j in range(n_chunks):
            read(j).wait()
            if j + 1 < n_chunks:
                if j >= 1:
                    for cp in scatters(j - 1):
                        cp.wait()
                read(j + 1).start()
            for cp in scatters(j):
                cp.start()
        for j in range(max(n_chunks - 2, 0), n_chunks):
            for cp in scatters(j):
                cp.wait()

    return scatter_kernel(rows, d0, d1)


def _sc_gather_rows(table, dest0, dest1):
    n = dest0.shape[0]
    w = table.shape[1]
    n_workers = SC_CORES * SC_SUBCORES
    assert n % (n_workers * SC_CHUNK) == 0
    n_chunks = n // (n_workers * SC_CHUNK)
    d0 = dest0.reshape(n // SC_CHUNK, 1, SC_CHUNK)
    d1 = dest1.reshape(n // SC_CHUNK, 1, SC_CHUNK)
    out = jax.ShapeDtypeStruct((n, w), table.dtype)

    @functools.partial(
        pl.kernel, mesh=_sc_mesh(), out_type=(out, out),
        scratch_types=[pltpu.VMEM((n_chunks, 1, SC_CHUNK), jnp.int32), pltpu.VMEM((n_chunks, 1, SC_CHUNK), jnp.int32),
                       pltpu.VMEM((2, SC_CHUNK, w), table.dtype),
                       pltpu.SemaphoreType.DMA((2,)), pltpu.SemaphoreType.DMA((2,))])
    def gather_kernel(table_hbm, d0_hbm, d1_hbm, a_hbm, b_hbm, i0_v, i1_v, rows_v, gather_sem, write_sem):
        wid = lax.axis_index("s") * SC_CORES + lax.axis_index("c")
        first = wid * n_chunks
        pltpu.sync_copy(d0_hbm.at[pl.ds(first, n_chunks)], i0_v)
        pltpu.sync_copy(d1_hbm.at[pl.ds(first, n_chunks)], i1_v)
        n_items = 2 * n_chunks

        def gather(m):
            idx = (i0_v, i1_v)[m % 2]
            return pltpu.make_async_copy(table_hbm.at[idx.at[m // 2].at[0]], rows_v.at[m % 2], gather_sem.at[m % 2])

        def write(m):
            o_hbm = (a_hbm, b_hbm)[m % 2]
            return pltpu.make_async_copy(rows_v.at[m % 2], o_hbm.at[pl.ds((first + m // 2) * SC_CHUNK, SC_CHUNK)],
                                         write_sem.at[m % 2])

        gather(0).start()
        for m in range(n_items):
            gather(m).wait()
            if m + 1 < n_items:
                if m >= 1:
                    write(m - 1).wait()
                gather(m + 1).start()
            write(m).start()
        for m in range(max(n_items - 2, 0), n_items):
            write(m).wait()

    return gather_kernel(table, d0, d1)


def _dispatch_plan(route, counts):
    tm = EXPERT_TILE
    e0, e1, r0, r1 = (route[:, j, :].reshape(-1) for j in range(4))
    experts = jnp.arange(MOE_TOTAL, dtype=jnp.int32)
    tiles_per = (counts + tm - 1) // tm
    tile_end = jnp.cumsum(tiles_per)
    pad_start = ((tile_end - tiles_per) * tm).astype(jnp.int32)

    def lookup(e):
        return jnp.sum(jnp.where(e[None, :] == experts[:, None], pad_start[:, None], 0), axis=0)

    dest0, dest1 = lookup(e0) + r0, lookup(e1) + r1
    n_tiles = (2 * e0.size + MOE_TOTAL * tm) // tm
    tile_expert = jnp.minimum(jnp.sum(tile_end[None, :] <= jnp.arange(n_tiles)[:, None], axis=1), MOE_TOTAL - 1)
    nonempty = counts > 0
    run_of_expert = jnp.cumsum(nonempty.astype(jnp.int32)) - 1
    run_expert = jnp.sum(jnp.where(nonempty[None, :] & (run_of_expert[None, :] == experts[:, None]),
                                   experts[None, :], 0), axis=1).astype(jnp.int32)
    of_tile = tile_expert[:, None] == experts[None, :]
    tile_run = jnp.sum(jnp.where(of_tile, run_of_expert[None, :], 0), axis=1).astype(jnp.int32)
    rows_left = (counts + pad_start)[None, :] - jnp.arange(n_tiles)[:, None] * tm
    tile_valid = jnp.clip(jnp.sum(jnp.where(of_tile, rows_left, 0), axis=1), 0, tm).astype(jnp.int32)
    used = jnp.stack([tile_end[-1], jnp.sum(nonempty)]).astype(jnp.int32)
    return dest0, dest1, tile_run, tile_valid, run_expert, used, n_tiles * tm


def _layer(x, c, rel_bias, w_ada, b_ada, w_in, w_gla_gate, b_gla_gate, gla_norm, w_proj_gla, w_proj_attn, w_out,
           ln1_g, ln1_b, w_rg, b_rg, w_re, b_re, w_eg, w_eu, w_ed, ln2_g, ln2_b):
    bsz, s, d = x.shape
    alpha = (2.0 * DEPTH) ** 0.25
    mods = _ada_mods(c, w_ada, b_ada)
    sh1, sc1, g1, sh2, sc2, g2 = [m.reshape(bsz, 1, d) for m in jnp.split(mods, N_MOD, axis=-1)]

    lr0 = d // 2 * 2 + 2 * d
    z = _in_projection(x, sc1, sh1, _prep_in_weight(w_in, lr0))

    y_gla = _gla(z["q_gla"], z["k_gla"], z["v_gla"], z["r_gla"], z["lr"], w_gla_gate, b_gla_gate, gla_norm)

    o_groups, lse_groups = [], []
    for g, (window, dilation) in enumerate(DIL_PATTERNS):
        l = s // dilation
        qg, kg, vg = (z[f"{n}{g}"].reshape(bsz * dilation, l, DIL_GROUP_WIDTH) for n in ("q_att", "k_att", "v_att"))
        table = rel_bias[:, g * DIL_HEADS_PER_GROUP:(g + 1) * DIL_HEADS_PER_GROUP]
        o, lse = _dilated_group_attention(qg, kg, vg, table, window, dilation)
        o_groups.append(o.reshape(bsz, dilation, l, DIL_GROUP_WIDTH))
        lse_groups.append(lse.reshape(bsz, dilation, l, DIL_GROUP_WIDTH))

    wr = jnp.concatenate([w_rg, w_re, jnp.zeros((d, LANES - MOE_GROUPS - MOE_TOTAL), F32)], axis=1)
    br = jnp.concatenate([b_rg, b_re, jnp.zeros((LANES - MOE_GROUPS - MOE_TOTAL,), F32)]).reshape(1, LANES)
    x1, u2, route, ew, cnt = _merge(alpha, y_gla, o_groups, lse_groups, z["g_gla"], z["g_att"], x, g1, sc2, sh2,
                                    ln1_g, ln1_b, w_proj_gla.astype(BF16), w_proj_attn.astype(BF16),
                                    w_out.astype(BF16), wr, br)

    n = bsz * s
    counts = cnt[0, MOE_GROUPS:MOE_GROUPS + MOE_TOTAL]
    dest0, dest1, tile_run, tile_valid, run_expert, used, n_rows = _dispatch_plan(route, counts)
    xg = _sc_scatter_rows(u2.reshape(n, d // 2), dest0, dest1, n_rows)
    ff = w_eg.shape[-1]
    yo = _expert_ffn(tile_run, tile_valid, run_expert, used, xg, w_eg.reshape(MOE_TOTAL, d, ff),
                     w_eu.reshape(MOE_TOTAL, d, ff), w_ed.reshape(MOE_TOTAL, ff, d))
    ya, yb = (y.reshape(bsz, s, d // 2) for y in _sc_gather_rows(yo, dest0, dest1))
    return _final(alpha, x1, ya, yb, ew, g2, ln2_g, ln2_b)


def kernel(x, c, rel_bias, w_ada, b_ada, w_in, w_gla_gate, b_gla_gate, gla_norm, w_proj_gla, w_proj_attn, w_out,
           ln1_g, ln1_b, w_router_group, b_router_group, w_router_expert, b_router_expert, w_exp_gate, w_exp_up,
           w_exp_down, ln2_g, ln2_b):
    assert w_ada.shape[0] == DEPTH
    return _layer(x, c, rel_bias, w_ada[0], b_ada[0], w_in[0:1], w_gla_gate[0], b_gla_gate[0], gla_norm[0],
                  w_proj_gla[0], w_proj_attn[0], w_out[0], ln1_g[0], ln1_b[0], w_router_group[0],
                  b_router_group[0], w_router_expert[0], b_router_expert[0], w_exp_gate[0], w_exp_up[0],
                  w_exp_down[0], ln2_g[0], ln2_b[0])
```

```python
import functools
import math

import numpy as np
import jax
import jax.numpy as jnp
from jax import lax
from jax.experimental import pallas as pl
from jax.experimental.pallas import tpu as pltpu
from jax.experimental.pallas import tpu_sc as plsc

F32 = jnp.float32
BF16 = jnp.bfloat16

N_MOD = 6
GLA_HEADS = 4
GLA_LOWRANK = 16
GLA_TAU = 16.0
GLA_CHUNK = 64
DIL_PATTERNS = ((128, 1), (512, 4), (2048, 16))
DIL_GROUPS = len(DIL_PATTERNS)
DIL_HEADS_PER_GROUP = 8
DIL_HEAD_DIM = 64
DIL_GROUP_WIDTH = DIL_HEADS_PER_GROUP * DIL_HEAD_DIM
DIL_BLOCK = 128
REL_BUCKETS = 32
REL_MAX_DIST = 2048
MOE_GROUPS = 4
MOE_EXPERTS = 8
MOE_TOTAL = MOE_GROUPS * MOE_EXPERTS
LN_EPS = 1e-5
DEPTH = 1

LANES = 128
VMEM_LIMIT = 56 * 1024 * 1024
LOG2E = 1.4426950408889634
LN2 = 0.6931471805599453
NEG = -1e30
ROW_TILE = 512
EXPERT_TILE = 512
EXPERT_BLOCK = 256
GLA_STEP_CHUNKS = 8
ATT_STEP_BLOCKS = 4
MERGE_SUB_ROWS = 512
ROUTE_ROWS = 8

HIGHEST = lax.Precision.HIGHEST
NT_DIMS = (((1,), (1,)), ((), ()))
TN_DIMS = (((0,), (0,)), ((), ()))


def _cparams(sem):
    return pltpu.CompilerParams(dimension_semantics=sem, vmem_limit_bytes=VMEM_LIMIT)


def _sigmoid(x):
    return 0.5 * jnp.tanh(0.5 * x) + 0.5


def _silu(x):
    return x * _sigmoid(x)


def _layer_norm(x, g, b):
    mu = jnp.mean(x, axis=-1, keepdims=True)
    xc = x - mu
    var = jnp.mean(xc * xc, axis=-1, keepdims=True)
    return xc * lax.rsqrt(var + LN_EPS) * g + b


def _pack_bf16_pairs(x):
    w = x.shape[1] // 2
    lo = lax.bitcast_convert_type(x[:, :w].astype(BF16).astype(F32), jnp.uint32) >> 16
    hi = lax.bitcast_convert_type(x[:, w:].astype(BF16).astype(F32), jnp.uint32) & jnp.uint32(0xFFFF0000)
    return lax.bitcast_convert_type(lo | hi, jnp.int32)


def _unpack_bf16_pairs(p):
    u = lax.bitcast_convert_type(p, jnp.uint32)
    lo = lax.bitcast_convert_type(u << 16, F32)
    hi = lax.bitcast_convert_type(u & jnp.uint32(0xFFFF0000), F32)
    return jnp.concatenate([lo, hi], axis=1)


def _split3(x):
    hi = x.astype(BF16)
    r1 = x - hi.astype(F32)
    mid = r1.astype(BF16)
    lo = (r1 - mid.astype(F32)).astype(BF16)
    return hi, mid, lo


def _mods_kernel(c_ref, w_ref, b_ref, o_ref):
    a = _silu(c_ref[...])
    o_ref[...] = jnp.dot(a, w_ref[...], precision=HIGHEST, preferred_element_type=F32) + b_ref[...]


def _ada_mods(c, w, b):
    bsz, d = c.shape
    n = w.shape[1]
    tn = 1536
    assert n % tn == 0
    return pl.pallas_call(
        _mods_kernel,
        out_shape=jax.ShapeDtypeStruct((bsz, n), F32),
        grid=(n // tn,),
        in_specs=[pl.BlockSpec((bsz, d), lambda j: (0, 0)),
                  pl.BlockSpec((d, tn), lambda j: (0, j)),
                  pl.BlockSpec((1, tn), lambda j: (0, j))],
        out_specs=pl.BlockSpec((bsz, tn), lambda j: (0, j)),
        compiler_params=_cparams(("arbitrary",)),
        name="ada_mods",
    )(c, w, b.reshape(1, n))


def _proj_pieces(d_model):
    dk = d_model // 2
    pieces = [("q_gla", dk, "scale_q_gla"), ("k_gla", dk, None), ("v_gla", d_model, None), ("r_gla", d_model, "silu")]
    for name, post in (("q_att", "scale_q_att"), ("k_att", None), ("v_att", None)):
        for g, (_, dilation) in enumerate(DIL_PATTERNS):
            pieces.append((f"{name}{g}", DIL_GROUP_WIDTH, (post, dilation)))
    pieces += [("g_gla", d_model, "sigmoid"), ("g_att", d_model, "sigmoid"), ("lr", LANES, "lowrank")]
    return tuple(pieces)


WT_BLOCK = 256


def _wprep_kernel(n_main_blocks, w_ref, o_ref):
    blk = w_ref[0]
    row = lax.broadcasted_iota(jnp.int32, blk.shape, 0)
    keep = (pl.program_id(0) < n_main_blocks) | (row < GLA_LOWRANK)
    o_ref[...] = jnp.where(keep, blk, 0.0).astype(BF16)


def _prep_in_weight(w_in, lr0):
    w_t = jnp.swapaxes(w_in, 1, 2)
    _, n_in, d = w_t.shape
    n_main = n_in - GLA_LOWRANK
    assert lr0 % WT_BLOCK == 0 and n_main % WT_BLOCK == 0
    n_main_blocks = n_main // WT_BLOCK

    def src_row(j):
        start = j * WT_BLOCK
        octet = jnp.where(j < n_main_blocks, (start + jnp.where(start >= lr0, GLA_LOWRANK, 0)) // 8, lr0 // 8)
        return octet * 8

    return pl.pallas_call(
        functools.partial(_wprep_kernel, n_main_blocks),
        out_shape=jax.ShapeDtypeStruct((n_main + WT_BLOCK, d), BF16),
        grid=(n_main_blocks + 1,),
        in_specs=[pl.BlockSpec((pl.Element(1), pl.Element(WT_BLOCK), pl.Element(d)), lambda j: (0, src_row(j), 0))],
        out_specs=pl.BlockSpec((WT_BLOCK, d), lambda j: (j, 0)),
        compiler_params=_cparams(("parallel",)),
        name="prep_in_weight",
    )(w_t)


def _proj_kernel(pieces, head_k, x_ref, sc_ref, sh_ref, w_ref, *refs):
    out_refs, stage_ref = refs[:-1], refs[-1]
    tm = x_ref.shape[1]
    u = (x_ref[0] * (1.0 + sc_ref[0]) + sh_ref[0]).astype(BF16)
    off = 0
    for (name, width, post), o_ref in zip(pieces, out_refs):
        chunk = min(width, 512)
        for c0 in range(0, width, chunk):
            acc = lax.dot_general(u, w_ref[off + c0:off + c0 + chunk, :], NT_DIMS, preferred_element_type=F32)
            if post == "silu":
                acc = _silu(acc)
            elif post == "sigmoid":
                acc = _sigmoid(acc)
            elif post == "scale_q_gla":
                acc = acc * (head_k ** -0.5)
            if post == "lowrank":
                o_ref[0] = acc[:, :GLA_LOWRANK]
            elif isinstance(post, tuple):
                scale, dilation = post
                if scale is not None:
                    acc = acc * (DIL_HEAD_DIM ** -0.5 * LOG2E)
                if dilation == 1:
                    o_ref[0, 0] = acc.astype(o_ref.dtype)
                else:
                    for t in range(width // LANES):
                        stage_ref[t] = acc[:, t * LANES:(t + 1) * LANES]
                    for r in range(dilation):
                        for t in range(width // LANES):
                            o_ref[0, r, :, t * LANES:(t + 1) * LANES] = stage_ref[
                                t, pl.ds(r, tm // dilation, stride=dilation), :].astype(o_ref.dtype)
            else:
                o_ref[0, :, c0:c0 + chunk] = acc.astype(o_ref.dtype)
        off += width


def _in_projection(x, sc1, sh1, w_perm):
    bsz, s, d = x.shape
    pieces = _proj_pieces(d)
    assert sum(p[1] for p in pieces) <= w_perm.shape[0]
    tm = min(ROW_TILE, s)
    assert s % tm == 0
    head_k = (d // 2) // GLA_HEADS
    out_shape, out_specs = [], []
    for name, width, post in pieces:
        if post == "lowrank":
            out_shape.append(jax.ShapeDtypeStruct((bsz, s, GLA_LOWRANK), F32))
            out_specs.append(pl.BlockSpec((1, tm, GLA_LOWRANK), lambda b, i: (b, i, 0)))
        elif isinstance(post, tuple):
            dil = post[1]
            assert tm % (dil * 16) == 0
            out_shape.append(jax.ShapeDtypeStruct((bsz, dil, s // dil, width), BF16))
            out_specs.append(pl.BlockSpec((1, dil, tm // dil, width), lambda b, i: (b, 0, i, 0)))
        else:
            out_shape.append(jax.ShapeDtypeStruct((bsz, s, width), BF16))
            out_specs.append(pl.BlockSpec((1, tm, width), lambda b, i: (b, i, 0)))
    outs = pl.pallas_call(
        functools.partial(_proj_kernel, pieces, head_k),
        out_shape=out_shape,
        grid=(bsz, s // tm),
        in_specs=[pl.BlockSpec((1, tm, d), lambda b, i: (b, i, 0)),
                  pl.BlockSpec((1, 1, d), lambda b, i: (b, 0, 0)),
                  pl.BlockSpec((1, 1, d), lambda b, i: (b, 0, 0)),
                  pl.BlockSpec(w_perm.shape, lambda b, i: (0, 0), pipeline_mode=pl.Buffered(1))],
        out_specs=out_specs,
        scratch_shapes=[pltpu.VMEM((DIL_GROUP_WIDTH // LANES, tm, LANES), F32)],
        compiler_params=_cparams(("parallel", "arbitrary")),
        name="in_projection",
    )(x, sc1, sh1, w_perm)
    return dict(zip([p[0] for p in pieces], outs))


def _gla_kernel(n_chunks, head_k, head_v, q_ref, k_ref, v_ref, r_ref, lr_ref, wg_ref, bg_ref, ng_ref, o_ref,
                state_ref):
    @pl.when(pl.program_id(1) == 0)
    def _():
        state_ref[...] = jnp.zeros_like(state_ref)

    c = GLA_CHUNK
    row = lax.broadcasted_iota(jnp.int32, (c, c), 0)
    col = lax.broadcasted_iota(jnp.int32, (c, c), 1)
    causal = row >= col
    tril = causal.astype(BF16)
    mid = c // 2 - 1
    gate_in = jnp.dot(lr_ref[0], wg_ref[...], precision=HIGHEST, preferred_element_type=F32) + bg_ref[...]
    g_all = (jnp.minimum(gate_in, 0.0) - jnp.log(1.0 + jnp.exp(-jnp.abs(gate_in)))) * (1.0 / GLA_TAU)
    g_hi, g_mid, g_lo = _split3(g_all)
    for ci in range(n_chunks):
        rows = slice(ci * c, (ci + 1) * c)
        bc = (jnp.dot(tril, g_hi[rows], preferred_element_type=F32)
              + jnp.dot(tril, g_mid[rows], preferred_element_type=F32)
              + jnp.dot(tril, g_lo[rows], preferred_element_type=F32))
        b_mid = bc[mid:mid + 1, :]
        b_last = bc[c - 1:c, :]
        qf = q_ref[0, rows, :].astype(F32)
        kf = k_ref[0, rows, :].astype(F32)
        q_in = (qf * jnp.exp(bc - b_mid)).astype(BF16)
        k_in = (kf * jnp.exp(b_mid - bc)).astype(BF16)
        q_st = (qf * jnp.exp(bc)).astype(BF16)
        k_st = (kf * jnp.exp(b_last - bc)).astype(BF16)
        dec = jnp.exp(b_last)
        for h in range(GLA_HEADS):
            ks = slice(h * head_k, (h + 1) * head_k)
            vs = slice(h * head_v, (h + 1) * head_v)
            vh = v_ref[0, rows, vs]
            att = lax.dot_general(q_in[:, ks], k_in[:, ks], NT_DIMS, preferred_element_type=F32)
            att = jnp.where(causal, att, 0.0).astype(BF16)
            st = state_ref[h]
            o = jnp.dot(att, vh, preferred_element_type=F32)
            o = o + lax.dot_general(q_st[:, ks], st.astype(BF16), NT_DIMS, preferred_element_type=F32)
            kv_t = lax.dot_general(vh, k_st[:, ks], TN_DIMS, preferred_element_type=F32)
            state_ref[h] = st * dec[:, ks] + kv_t
            ms = jnp.mean(o * o, axis=-1, keepdims=True)
            o = o * lax.rsqrt(ms + LN_EPS) * ng_ref[:, vs] * r_ref[0, rows, vs].astype(F32)
            o_ref[0, rows, vs] = o.astype(o_ref.dtype)


def _gla(q, k, v, r_silu, lr, w_gate, b_gate, norm_g):
    bsz, s, dk = q.shape
    dv = v.shape[-1]
    head_k, head_v = dk // GLA_HEADS, dv // GLA_HEADS
    n_chunks = min(GLA_STEP_CHUNKS, s // GLA_CHUNK)
    ct = GLA_CHUNK * n_chunks
    assert s % ct == 0
    row_spec = lambda w: pl.BlockSpec((1, ct, w), lambda b, i: (b, i, 0))
    full = lambda a: pl.BlockSpec(a.shape, lambda b, i: (0,) * a.ndim)
    bg = b_gate.reshape(1, dk)
    ng = norm_g.reshape(1, dv)
    return pl.pallas_call(
        functools.partial(_gla_kernel, n_chunks, head_k, head_v),
        out_shape=jax.ShapeDtypeStruct((bsz, s, dv), BF16),
        grid=(bsz, s // ct),
        in_specs=[row_spec(dk), row_spec(dk), row_spec(dv), row_spec(dv), row_spec(GLA_LOWRANK),
                  full(w_gate), full(bg), full(ng)],
        out_specs=row_spec(dv),
        scratch_shapes=[pltpu.VMEM((GLA_HEADS, head_v, head_k), F32)],
        compiler_params=_cparams(("parallel", "arbitrary")),
        name="gla",
    )(q, k, v, r_silu, lr, w_gate, bg, ng)


def _t5_bucket_np(dist):
    exact = REL_BUCKETS // 2
    d = np.maximum(dist, 1).astype(np.float32)
    large = exact + (np.log(d / np.float32(exact)) / np.float32(math.log(REL_MAX_DIST / exact))
                     * np.float32(REL_BUCKETS - exact)).astype(np.int32)
    large = np.minimum(large, REL_BUCKETS - 1)
    return np.where(dist < exact, dist, large).astype(np.int32)


def _band_tables(window, dilation):
    qi = np.arange(DIL_BLOCK)[:, None]
    kj = np.arange(2 * DIL_BLOCK)[None, :]
    m = qi + DIL_BLOCK - kj
    n_steps = window // dilation
    band = (m >= 0) & (m <= n_steps)
    bucket = _t5_bucket_np(np.clip(m, 0, n_steps) * dilation)
    return np.where(band, bucket, -1).astype(np.int32)


def _attn_kernel(nq, table_ref, bucket_ref, q_ref, kp_ref, kc_ref, vp_ref, vc_ref, o_ref, lse_ref,
                 bias_ref, p_ref):
    i = pl.program_id(1)
    blk = DIL_BLOCK
    hpg = DIL_HEADS_PER_GROUP
    n_pairs = hpg // 2

    @pl.when((pl.program_id(0) == 0) & (i == 0))
    def _():
        bucket = bucket_ref[...]
        for h in range(hpg):
            acc = jnp.full(bucket.shape, NEG, F32)
            for bkt in range(REL_BUCKETS):
                acc = jnp.where(bucket == bkt, table_ref[bkt, h] * LOG2E, acc)
            bias_ref[h * blk:(h + 1) * blk, :] = acc

    lane = lax.broadcasted_iota(jnp.int32, (blk, LANES), 1)
    low = lane < DIL_HEAD_DIM
    ones_rhs = jnp.ones((2 * blk, LANES), BF16)

    def windows(ref_p, ref_c, sq, qb, cols):
        if qb == 0:
            return jnp.concatenate([ref_p[sq, :, cols], ref_c[sq, 0:blk, cols]], axis=0)
        return ref_c[sq, (qb - 1) * blk:(qb + 1) * blk, cols]

    key_lane = lax.broadcasted_iota(jnp.int32, (1, 2 * blk), 1)
    no_prev = jnp.where((key_lane < blk) & (i == 0), NEG, 0.0)
    items = [(sq, qb, hp) for sq in range(q_ref.shape[0]) for qb in range(nq) for hp in range(n_pairs)]

    mxs = []
    for n, (sq, qb, hp) in enumerate(items):
        rows = slice(qb * blk, (qb + 1) * blk)
        cols = slice(hp * LANES, (hp + 1) * LANES)
        qp = q_ref[sq, rows, cols]
        zero = jnp.zeros_like(qp)
        qq = jnp.concatenate([jnp.where(low, qp, zero), jnp.where(low, zero, qp)], axis=0)
        keys = windows(kp_ref, kc_ref, sq, qb, cols)
        s = lax.dot_general(qq, keys, NT_DIMS, preferred_element_type=F32) + bias_ref[2 * hp * blk:(2 * hp + 2) * blk, :]
        if qb == 0:
            s = s + no_prev
        mx = jnp.max(s, axis=-1, keepdims=True)
        p_ref[n * 2 * blk:(n + 1) * 2 * blk, :] = jnp.exp2(s - mx).astype(BF16)
        mxs.append(mx)

    for n, (sq, qb, hp) in enumerate(items):
        rows = slice(qb * blk, (qb + 1) * blk)
        cols = slice(hp * LANES, (hp + 1) * LANES)
        vals = windows(vp_ref, vc_ref, sq, qb, cols)
        rhs = jnp.concatenate([vals, ones_rhs], axis=1)
        res = jnp.dot(p_ref[n * 2 * blk:(n + 1) * 2 * blk, :], rhs, preferred_element_type=F32)
        num = jnp.where(low, res[0:blk, 0:LANES], res[blk:2 * blk, 0:LANES])
        den = jnp.where(low, res[0:blk, LANES:], res[blk:2 * blk, LANES:])
        mx = jnp.where(low, mxs[n][0:blk], mxs[n][blk:2 * blk])
        o_ref[sq, rows, cols] = (num / den).astype(o_ref.dtype)
        lse_ref[sq, rows, cols] = (mx + jnp.log2(den)) * LN2


def _dilated_group_attention(q, k, v, table, window, dilation):
    bb, l, w = q.shape
    nq = min(ATT_STEP_BLOCKS, l // DIL_BLOCK)
    nsq = ATT_STEP_BLOCKS // nq
    assert l % (nq * DIL_BLOCK) == 0 and bb % nsq == 0
    steps = l // (nq * DIL_BLOCK)
    bucket = jnp.asarray(_band_tables(window, dilation))
    cur = pl.BlockSpec((nsq, nq * DIL_BLOCK, w), lambda b, i: (b, i, 0))
    prev = pl.BlockSpec((nsq, DIL_BLOCK, w), lambda b, i: (b, jnp.maximum(nq * i - 1, 0), 0))
    rows_all = nsq * nq * DIL_HEADS_PER_GROUP * DIL_BLOCK
    return pl.pallas_call(
        functools.partial(_attn_kernel, nq),
        out_shape=[jax.ShapeDtypeStruct((bb, l, w), BF16), jax.ShapeDtypeStruct((bb, l, w), F32)],
        grid=(bb // nsq, steps),
        in_specs=[pl.BlockSpec(memory_space=pltpu.SMEM),
                  pl.BlockSpec(bucket.shape, lambda b, i: (0, 0)),
                  cur, prev, cur, prev, cur],
        out_specs=[cur, cur],
        scratch_shapes=[pltpu.VMEM((DIL_HEADS_PER_GROUP * DIL_BLOCK, 2 * DIL_BLOCK), F32),
                        pltpu.VMEM((rows_all, 2 * DIL_BLOCK), BF16)],
        compiler_params=_cparams(("arbitrary", "arbitrary")),
        name=f"dilated_attn_d{dilation}",
    )(table, bucket, q, k, k, v, v)


def _merge_kernel(alpha, dilations, ygla_ref, o0_ref, o1_ref, o2_ref, l0_ref, l1_ref, l2_ref, gg_ref, ga_ref, x_ref,
                  g1_ref, sc2_ref, sh2_ref, ln_g_ref, ln_b_ref, wpg_ref, wpa_ref, wout_ref, wr_ref, br_ref, ltri_ref,
                  x1_ref, u2_ref, route_ref, ew_ref, cnt_ref, stage_ref, carry_ref):
    tm = x_ref.shape[1]

    @pl.when((pl.program_id(0) == 0) & (pl.program_id(1) == 0))
    def _():
        carry_ref[...] = jnp.zeros_like(carry_ref)

    n_lt = DIL_GROUP_WIDTH // LANES
    group_refs = tuple(zip((l0_ref, l1_ref, l2_ref), (o0_ref, o1_ref, o2_ref), dilations))
    for gi, (l_ref, o_ref, dil) in enumerate(group_refs):
        if dil > 1:
            for slot, ref in ((2 * gi, l_ref), (2 * gi + 1, o_ref)):
                for r in range(dil):
                    for t in range(n_lt):
                        stage_ref[slot, t, pl.ds(r, tm // dil, stride=dil), :] = ref[
                            0, r, :, t * LANES:(t + 1) * LANES].astype(F32)

    wr = wr_ref[...]
    w_hi = wr.astype(BF16)
    w_lo = (wr - w_hi.astype(F32)).astype(BF16)
    sub = ltri_ref.shape[0]
    for rows in (slice(r0, r0 + sub) for r0 in range(0, tm, sub)):
        def natural(ref, dil, slot):
            if dil == 1:
                return ref[0, 0, rows, :].astype(F32)
            return jnp.concatenate([stage_ref[slot, t, rows, :] for t in range(n_lt)], axis=1)

        lses = [natural(l_ref, dil, 2 * gi) for gi, (l_ref, _, dil) in enumerate(group_refs)]
        outs = [natural(o_ref, dil, 2 * gi + 1) for gi, (_, o_ref, dil) in enumerate(group_refs)]
        lm = jnp.maximum(jnp.maximum(lses[0], lses[1]), lses[2])
        es = [jnp.exp(l - lm) for l in lses]
        y_att = (es[0] * outs[0] + es[1] * outs[1] + es[2] * outs[2]) / (es[0] + es[1] + es[2])

        p_gla = jnp.dot(ygla_ref[0, rows, :], wpg_ref[...], preferred_element_type=F32)
        p_att = jnp.dot(y_att.astype(BF16), wpa_ref[...], preferred_element_type=F32)
        merged = gg_ref[0, rows, :].astype(F32) * p_gla + ga_ref[0, rows, :].astype(F32) * p_att
        y = jnp.dot(merged.astype(BF16), wout_ref[...], preferred_element_type=F32)
        x1 = _layer_norm(alpha * x_ref[0, rows, :] + g1_ref[0] * y, ln_g_ref[...], ln_b_ref[...])
        x1_ref[0, rows, :] = x1
        u2 = x1 * (1.0 + sc2_ref[0]) + sh2_ref[0]
        u2_ref[0, rows, :] = _pack_bf16_pairs(u2)

        u_hi = u2.astype(BF16)
        u_lo = (u2 - u_hi.astype(F32)).astype(BF16)
        logits = (jnp.dot(u_hi, w_hi, preferred_element_type=F32) + jnp.dot(u_lo, w_hi, preferred_element_type=F32)
                  + jnp.dot(u_hi, w_lo, preferred_element_type=F32)) + br_ref[...]
        lane = lax.broadcasted_iota(jnp.int32, logits.shape, 1)
        big = jnp.int32(LANES)
        lg = jnp.where(lane < MOE_GROUPS, logits, NEG)
        gmax = jnp.max(lg, axis=-1, keepdims=True)
        gidx = jnp.min(jnp.where(lg == gmax, lane, big), axis=-1, keepdims=True)
        gval = 1.0 / jnp.sum(jnp.exp(lg - gmax), axis=-1, keepdims=True)
        in_group = (lane >= MOE_GROUPS + gidx * MOE_EXPERTS) & (lane < MOE_GROUPS + (gidx + 1) * MOE_EXPERTS)
        le = jnp.where(in_group, logits, NEG)
        m1 = jnp.max(le, axis=-1, keepdims=True)
        i1 = jnp.min(jnp.where(le == m1, lane, big), axis=-1, keepdims=True)
        le2 = jnp.where(lane == i1, NEG, le)
        m2 = jnp.max(le2, axis=-1, keepdims=True)
        i2 = jnp.min(jnp.where(le2 == m2, lane, big), axis=-1, keepdims=True)
        t = jnp.exp(m2 - m1)
        w1 = 1.0 / (1.0 + t)
        w2 = t * w1

        hit1, hit2 = lane == i1, lane == i2
        onehot = jnp.where(hit1 | hit2, 1.0, 0.0)
        earlier = jnp.dot(ltri_ref[...], onehot.astype(BF16), preferred_element_type=F32) + carry_ref[...]
        rank1 = jnp.sum(jnp.where(hit1, earlier, 0.0), axis=-1, keepdims=True).astype(jnp.int32)
        rank2 = jnp.sum(jnp.where(hit2, earlier, 0.0), axis=-1, keepdims=True).astype(jnp.int32)
        carry_ref[...] = carry_ref[...] + jnp.sum(onehot, axis=0, keepdims=True)
        route = jnp.where(lane == 0, i1 - MOE_GROUPS, jnp.where(lane == 1, i2 - MOE_GROUPS,
                          jnp.where(lane == 2, rank1, jnp.where(lane == 3, rank2, 0))))
        route_ref[0, :, rows] = jnp.transpose(route)[0:ROUTE_ROWS, :]
        ew_ref[0, rows, :] = jnp.where(lane == 0, gval * w1, jnp.where(lane == 1, gval * w2, 0.0))
    cnt_ref[...] = carry_ref[...].astype(jnp.int32)


def _merge(alpha, y_gla, o_groups, lse_groups, g_gla, g_att, x, g1, sc2, sh2, ln_g, ln_b, wpg, wpa, wout, wr, br):
    bsz, s, d = x.shape
    tm = min(ROW_TILE, s)
    assert s % tm == 0
    dilations = tuple(dil for _, dil in DIL_PATTERNS)
    row = lambda w: pl.BlockSpec((1, tm, w), lambda b, i: (b, i, 0))
    sub = lambda dil: pl.BlockSpec((1, dil, tm // dil, DIL_GROUP_WIDTH), lambda b, i: (b, 0, i, 0))
    per_b = pl.BlockSpec((1, 1, d), lambda b, i: (b, 0, 0))
    full = lambda a: pl.BlockSpec(a.shape, lambda b, i: (0,) * a.ndim)
    ln_g2, ln_b2 = ln_g.reshape(1, d), ln_b.reshape(1, d)
    sub_rows = min(MERGE_SUB_ROWS, tm)
    assert tm % sub_rows == 0
    ltri = jnp.asarray(np.tril(np.ones((sub_rows, sub_rows), np.float32), -1), BF16)
    return pl.pallas_call(
        functools.partial(_merge_kernel, alpha, dilations),
        out_shape=[jax.ShapeDtypeStruct((bsz, s, d), F32), jax.ShapeDtypeStruct((bsz, s, d // 2), jnp.int32),
                   jax.ShapeDtypeStruct((bsz, ROUTE_ROWS, s), jnp.int32), jax.ShapeDtypeStruct((bsz, s, LANES), F32),
                   jax.ShapeDtypeStruct((1, LANES), jnp.int32)],
        grid=(bsz, s // tm),
        in_specs=[row(y_gla.shape[-1])] + [sub(dil) for dil in dilations] * 2
                 + [row(d), row(d), row(d), per_b, per_b, per_b, full(ln_g2), full(ln_b2),
                    full(wpg), full(wpa), full(wout), full(wr), full(br), full(ltri)],
        out_specs=[row(d), row(d // 2), pl.BlockSpec((1, ROUTE_ROWS, tm), lambda b, i: (b, 0, i)), row(LANES),
                   pl.BlockSpec((1, LANES), lambda b, i: (0, 0))],
        scratch_shapes=[pltpu.VMEM((2 * DIL_GROUPS, DIL_GROUP_WIDTH // LANES, tm, LANES), F32),
                        pltpu.VMEM((1, LANES), F32)],
        compiler_params=_cparams(("arbitrary", "arbitrary")),
        name="merge_ln1_router",
    )(y_gla, *o_groups, *lse_groups, g_gla, g_att, x, g1, sc2, sh2, ln_g2, ln_b2, wpg, wpa, wout, wr, br, ltri)


def _expert_kernel(run_ref, valid_ref, rexp_ref, used_ref, x_ref, wg_hbm, wu_hbm, wd_hbm, o_ref,
                   wg_f, wu_f, wd_f, wg_s, wu_s, wd_s, sem):
    t = pl.program_id(0)
    n_tiles_used, n_runs = used_ref[0], used_ref[1]
    run = run_ref[t]
    active = t < n_tiles_used
    first_of_run = (t == 0) | (run_ref[jnp.maximum(t - 1, 0)] != run)

    def weight_copies(r):
        e, slot = rexp_ref[r], r % 2
        return [pltpu.make_async_copy(hbm.at[e], buf.at[slot], sem.at[slot, j])
                for j, (hbm, buf) in enumerate(((wg_hbm, wg_f), (wu_hbm, wu_f), (wd_hbm, wd_f)))]

    @pl.when(active & (t == 0))
    def _():
        for cp in weight_copies(0):
            cp.start()

    @pl.when(active & first_of_run)
    def _():
        @pl.when(run + 1 < n_runs)
        def _():
            for cp in weight_copies(run + 1):
                cp.start()

        for cp in weight_copies(run):
            cp.wait()
        slot = run % 2
        wg_s[...] = wg_f[slot].astype(BF16)
        wu_s[...] = wu_f[slot].astype(BF16)
        wd_s[...] = wd_f[slot].astype(BF16)

    n_valid = jnp.where(active, valid_ref[t], 0)
    for r0 in range(0, x_ref.shape[0], EXPERT_BLOCK):
        rows = slice(r0, r0 + EXPERT_BLOCK)

        @pl.when(n_valid > r0)
        def _():
            xt = _unpack_bf16_pairs(x_ref[rows, :]).astype(BF16)
            hg = jnp.dot(xt, wg_s[...], preferred_element_type=F32)
            hu = jnp.dot(xt, wu_s[...], preferred_element_type=F32)
            h = (_silu(hg) * hu).astype(BF16)
            o_ref[rows, :] = _pack_bf16_pairs(jnp.dot(h, wd_s[...], preferred_element_type=F32))

        @pl.when(n_valid <= r0)
        def _():
            o_ref[rows, :] = jnp.zeros((EXPERT_BLOCK, o_ref.shape[1]), o_ref.dtype)


def _expert_ffn(tile_run, tile_valid, run_expert, used, xg, w_gate, w_up, w_down):
    p = xg.shape[0]
    ne, d, ff = w_gate.shape
    tm = EXPERT_TILE
    n_tiles = p // tm
    hbm = pl.BlockSpec(memory_space=pl.ANY)
    grid_spec = pltpu.PrefetchScalarGridSpec(
        num_scalar_prefetch=4,
        grid=(n_tiles,),
        in_specs=[pl.BlockSpec((tm, d // 2), lambda t, *_: (t, 0)), hbm, hbm, hbm],
        out_specs=pl.BlockSpec((tm, d // 2), lambda t, *_: (t, 0)),
        scratch_shapes=[pltpu.VMEM((2, d, ff), F32), pltpu.VMEM((2, d, ff), F32), pltpu.VMEM((2, ff, d), F32),
                        pltpu.VMEM((d, ff), BF16), pltpu.VMEM((d, ff), BF16), pltpu.VMEM((ff, d), BF16),
                        pltpu.SemaphoreType.DMA((2, 3))],
    )
    return pl.pallas_call(
        _expert_kernel,
        out_shape=jax.ShapeDtypeStruct((p, d // 2), jnp.int32),
        grid_spec=grid_spec,
        compiler_params=_cparams(("arbitrary",)),
        name="expert_ffn",
    )(tile_run, tile_valid, run_expert, used, xg, w_gate, w_up, w_down)


def _final_kernel(alpha, x1_ref, ya_ref, yb_ref, ew_ref, g2_ref, ln_g_ref, ln_b_ref, o_ref):
    ew = ew_ref[0]
    y = ew[:, 0:1] * _unpack_bf16_pairs(ya_ref[0]) + ew[:, 1:2] * _unpack_bf16_pairs(yb_ref[0])
    o_ref[0] = _layer_norm(alpha * x1_ref[0] + g2_ref[0] * y, ln_g_ref[...], ln_b_ref[...])


def _final(alpha, x1, ya, yb, ew, g2, ln_g, ln_b):
    bsz, s, d = x1.shape
    tm = min(ROW_TILE, s)
    row = lambda w: pl.BlockSpec((1, tm, w), lambda b, i: (b, i, 0))
    full = lambda a: pl.BlockSpec(a.shape, lambda b, i: (0,) * a.ndim)
    ln_g2, ln_b2 = ln_g.reshape(1, d), ln_b.reshape(1, d)
    return pl.pallas_call(
        functools.partial(_final_kernel, alpha),
        out_shape=jax.ShapeDtypeStruct((bsz, s, d), F32),
        grid=(bsz, s // tm),
        in_specs=[row(d), row(d // 2), row(d // 2), row(LANES), pl.BlockSpec((1, 1, d), lambda b, i: (b, 0, 0)),
                  full(ln_g2), full(ln_b2)],
        out_specs=row(d),
        compiler_params=_cparams(("parallel", "arbitrary")),
        name="combine_ln2",
    )(x1, ya, yb, ew, g2, ln_g2, ln_b2)


SC_CORES = 2
SC_SUBCORES = 16
SC_CHUNK = 64


def _sc_mesh():
    return plsc.VectorSubcoreMesh(core_axis_name="c", subcore_axis_name="s")


def _sc_scatter_rows(rows, dest0, dest1, n_rows):
    n, w = rows.shape
    n_workers = SC_CORES * SC_SUBCORES
    assert n % (n_workers * SC_CHUNK) == 0
    n_chunks = n // (n_workers * SC_CHUNK)
    d0 = dest0.reshape(n // SC_CHUNK, 1, SC_CHUNK)
    d1 = dest1.reshape(n // SC_CHUNK, 1, SC_CHUNK)

    @functools.partial(
        pl.kernel, mesh=_sc_mesh(), out_type=jax.ShapeDtypeStruct((n_rows, w), rows.dtype),
        scratch_types=[pltpu.VMEM((n_chunks, 1, SC_CHUNK), jnp.int32), pltpu.VMEM((n_chunks, 1, SC_CHUNK), jnp.int32),
                       pltpu.VMEM((2, SC_CHUNK, w), rows.dtype),
                       pltpu.SemaphoreType.DMA((2,)), pltpu.SemaphoreType.DMA((2, 2))])
    def scatter_kernel(rows_hbm, d0_hbm, d1_hbm, out_hbm, i0_v, i1_v, rows_v, read_sem, scat_sem):
        wid = lax.axis_index("s") * SC_CORES + lax.axis_index("c")
        first = wid * n_chunks
        pltpu.sync_copy(d0_hbm.at[pl.ds(first, n_chunks)], i0_v)
        pltpu.sync_copy(d1_hbm.at[pl.ds(first, n_chunks)], i1_v)

        def read(j):
            return pltpu.make_async_copy(rows_hbm.at[pl.ds((first + j) * SC_CHUNK, SC_CHUNK)], rows_v.at[j % 2],
                                         read_sem.at[j % 2])

        def scatters(j):
            return [pltpu.make_async_copy(rows_v.at[j % 2], out_hbm.at[idx.at[j].at[0]], scat_sem.at[j % 2, k])
                    for k, idx in enumerate((i0_v, i1_v))]

        read(0).start()
        for j in range(n_chunks):
            read(j).wait()
            if j + 1 < n_chunks:
                if j >= 1:
                    for cp in scatters(j - 1):
                        cp.wait()
                read(j + 1).start()
            for cp in scatters(j):
                cp.start()
        for j in range(max(n_chunks - 2, 0), n_chunks):
            for cp in scatters(j):
                cp.wait()

    return scatter_kernel(rows, d0, d1)


def _sc_gather_rows(table, dest0, dest1):
    n = dest0.shape[0]
    w = table.shape[1]
    n_workers = SC_CORES * SC_SUBCORES
    assert n % (n_workers * SC_CHUNK) == 0
    n_chunks = n // (n_workers * SC_CHUNK)
    d0 = dest0.reshape(n // SC_CHUNK, 1, SC_CHUNK)
    d1 = dest1.reshape(n // SC_CHUNK, 1, SC_CHUNK)
    out = jax.ShapeDtypeStruct((n, w), table.dtype)

    @functools.partial(
        pl.kernel, mesh=_sc_mesh(), out_type=(out, out),
        scratch_types=[pltpu.VMEM((n_chunks, 1, SC_CHUNK), jnp.int32), pltpu.VMEM((n_chunks, 1, SC_CHUNK), jnp.int32),
                       pltpu.VMEM((2, SC_CHUNK, w), table.dtype),
                       pltpu.SemaphoreType.DMA((2,)), pltpu.SemaphoreType.DMA((2,))])
    def gather_kernel(table_hbm, d0_hbm, d1_hbm, a_hbm, b_hbm, i0_v, i1_v, rows_v, gather_sem, write_sem):
        wid = lax.axis_index("s") * SC_CORES + lax.axis_index("c")
        first = wid * n_chunks
        pltpu.sync_copy(d0_hbm.at[pl.ds(first, n_chunks)], i0_v)
        pltpu.sync_copy(d1_hbm.at[pl.ds(first, n_chunks)], i1_v)
        n_items = 2 * n_chunks

        def gather(m):
            idx = (i0_v, i1_v)[m % 2]
            return pltpu.make_async_copy(table_hbm.at[idx.at[m // 2].at[0]], rows_v.at[m % 2], gather_sem.at[m % 2])

        def write(m):
            o_hbm = (a_hbm, b_hbm)[m % 2]
            return pltpu.make_async_copy(rows_v.at[m % 2], o_hbm.at[pl.ds((first + m // 2) * SC_CHUNK, SC_CHUNK)],
                                         write_sem.at[m % 2])

        gather(0).start()
        for m in range(n_items):
            gather(m).wait()
            if m + 1 < n_items:
                if m >= 1:
                    write(m - 1).wait()
                gather(m + 1).start()
            write(m).start()
        for m in range(max(n_items - 2, 0), n_items):
            write(m).wait()

    return gather_kernel(table, d0, d1)


def _dispatch_plan(route, counts):
    tm = EXPERT_TILE
    e0, e1, r0, r1 = (route[:, j, :].reshape(-1) for j in range(4))
    experts = jnp.arange(MOE_TOTAL, dtype=jnp.int32)
    tiles_per = (counts + tm - 1) // tm
    tile_end = jnp.cumsum(tiles_per)
    pad_start = ((tile_end - tiles_per) * tm).astype(jnp.int32)

    def lookup(e):
        return jnp.sum(jnp.where(e[None, :] == experts[:, None], pad_start[:, None], 0), axis=0)

    dest0, dest1 = lookup(e0) + r0, lookup(e1) + r1
    n_tiles = (2 * e0.size + MOE_TOTAL * tm) // tm
    tile_expert = jnp.minimum(jnp.sum(tile_end[None, :] <= jnp.arange(n_tiles)[:, None], axis=1), MOE_TOTAL - 1)
    nonempty = counts > 0
    run_of_expert = jnp.cumsum(nonempty.astype(jnp.int32)) - 1
    run_expert = jnp.sum(jnp.where(nonempty[None, :] & (run_of_expert[None, :] == experts[:, None]),
                                   experts[None, :], 0), axis=1).astype(jnp.int32)
    of_tile = tile_expert[:, None] == experts[None, :]
    tile_run = jnp.sum(jnp.where(of_tile, run_of_expert[None, :], 0), axis=1).astype(jnp.int32)
    rows_left = (counts + pad_start)[None, :] - jnp.arange(n_tiles)[:, None] * tm
    tile_valid = jnp.clip(jnp.sum(jnp.where(of_tile, rows_left, 0), axis=1), 0, tm).astype(jnp.int32)
    used = jnp.stack([tile_end[-1], jnp.sum(nonempty)]).astype(jnp.int32)
    return dest0, dest1, tile_run, tile_valid, run_expert, used, n_tiles * tm


def _layer(x, c, rel_bias, w_ada, b_ada, w_in, w_gla_gate, b_gla_gate, gla_norm, w_proj_gla, w_proj_attn, w_out,
           ln1_g, ln1_b, w_rg, b_rg, w_re, b_re, w_eg, w_eu, w_ed, ln2_g, ln2_b):
    bsz, s, d = x.shape
    alpha = (2.0 * DEPTH) ** 0.25
    mods = _ada_mods(c, w_ada, b_ada)
    sh1, sc1, g1, sh2, sc2, g2 = [m.reshape(bsz, 1, d) for m in jnp.split(mods, N_MOD, axis=-1)]

    lr0 = d // 2 * 2 + 2 * d
    z = _in_projection(x, sc1, sh1, _prep_in_weight(w_in, lr0))

    y_gla = _gla(z["q_gla"], z["k_gla"], z["v_gla"], z["r_gla"], z["lr"], w_gla_gate, b_gla_gate, gla_norm)

    o_groups, lse_groups = [], []
    for g, (window, dilation) in enumerate(DIL_PATTERNS):
        l = s // dilation
        qg, kg, vg = (z[f"{n}{g}"].reshape(bsz * dilation, l, DIL_GROUP_WIDTH) for n in ("q_att", "k_att", "v_att"))
        table = rel_bias[:, g * DIL_HEADS_PER_GROUP:(g + 1) * DIL_HEADS_PER_GROUP]
        o, lse = _dilated_group_attention(qg, kg, vg, table, window, dilation)
        o_groups.append(o.reshape(bsz, dilation, l, DIL_GROUP_WIDTH))
        lse_groups.append(lse.reshape(bsz, dilation, l, DIL_GROUP_WIDTH))

    wr = jnp.concatenate([w_rg, w_re, jnp.zeros((d, LANES - MOE_GROUPS - MOE_TOTAL), F32)], axis=1)
    br = jnp.concatenate([b_rg, b_re, jnp.zeros((LANES - MOE_GROUPS - MOE_TOTAL,), F32)]).reshape(1, LANES)
    x1, u2, route, ew, cnt = _merge(alpha, y_gla, o_groups, lse_groups, z["g_gla"], z["g_att"], x, g1, sc2, sh2,
                                    ln1_g, ln1_b, w_proj_gla.astype(BF16), w_proj_attn.astype(BF16),
                                    w_out.astype(BF16), wr, br)

    n = bsz * s
    counts = cnt[0, MOE_GROUPS:MOE_GROUPS + MOE_TOTAL]
    dest0, dest1, tile_run, tile_valid, run_expert, used, n_rows = _dispatch_plan(route, counts)
    xg = _sc_scatter_rows(u2.reshape(n, d // 2), dest0, dest1, n_rows)
    ff = w_eg.shape[-1]
    yo = _expert_ffn(tile_run, tile_valid, run_expert, used, xg, w_eg.reshape(MOE_TOTAL, d, ff),
                     w_eu.reshape(MOE_TOTAL, d, ff), w_ed.reshape(MOE_TOTAL, ff, d))
    ya, yb = (y.reshape(bsz, s, d // 2) for y in _sc_gather_rows(yo, dest0, dest1))
    return _final(alpha, x1, ya, yb, ew, g2, ln2_g, ln2_b)


def kernel(x, c, rel_bias, w_ada, b_ada, w_in, w_gla_gate, b_gla_gate, gla_norm, w_proj_gla, w_proj_attn, w_out,
           ln1_g, ln1_b, w_router_group, b_router_group, w_router_expert, b_router_expert, w_exp_gate, w_exp_up,
           w_exp_down, ln2_g, ln2_b):
    assert w_ada.shape[0] == DEPTH
    return _layer(x, c, rel_bias, w_ada[0], b_ada[0], w_in[0:1], w_gla_gate[0], b_gla_gate[0], gla_norm[0],
                  w_proj_gla[0], w_proj_attn[0], w_out[0], ln1_g[0], ln1_b[0], w_router_group[0],
                  b_router_group[0], w_router_expert[0], b_router_expert[0], w_exp_gate[0], w_exp_up[0],
                  w_exp_down[0], ln2_g[0], ln2_b[0])
```

```python
import functools
import math

import numpy as np
import jax
import jax.numpy as jnp
from jax import lax
from jax.experimental import pallas as pl
from jax.experimental.pallas import tpu as pltpu
from jax.experimental.pallas import tpu_sc as plsc

F32 = jnp.float32
BF16 = jnp.bfloat16

N_MOD = 6
GLA_HEADS = 4
GLA_LOWRANK = 16
GLA_TAU = 16.0
GLA_CHUNK = 64
DIL_PATTERNS = ((128, 1), (512, 4), (2048, 16))
DIL_GROUPS = len(DIL_PATTERNS)
DIL_HEADS_PER_GROUP = 8
DIL_HEAD_DIM = 64
DIL_GROUP_WIDTH = DIL_HEADS_PER_GROUP * DIL_HEAD_DIM
DIL_BLOCK = 128
REL_BUCKETS = 32
REL_MAX_DIST = 2048
MOE_GROUPS = 4
MOE_EXPERTS = 8
MOE_TOTAL = MOE_GROUPS * MOE_EXPERTS
LN_EPS = 1e-5
DEPTH = 1

LANES = 128
VMEM_LIMIT = 56 * 1024 * 1024
LOG2E = 1.4426950408889634
LN2 = 0.6931471805599453
NEG = -1e30
ROW_TILE = 512
EXPERT_TILE = 1024
EXPERT_BLOCK = 256
GLA_STEP_CHUNKS = 8
ATT_STEP_BLOCKS = 4
MERGE_SUB_ROWS = 512
ROUTE_ROWS = 8

HIGHEST = lax.Precision.HIGHEST
NT_DIMS = (((1,), (1,)), ((), ()))
TN_DIMS = (((0,), (0,)), ((), ()))


def _cparams(sem):
    return pltpu.CompilerParams(dimension_semantics=sem, vmem_limit_bytes=VMEM_LIMIT)


def _sigmoid(x):
    return 0.5 * jnp.tanh(0.5 * x) + 0.5


def _silu(x):
    return x * _sigmoid(x)


def _layer_norm(x, g, b):
    mu = jnp.mean(x, axis=-1, keepdims=True)
    xc = x - mu
    var = jnp.mean(xc * xc, axis=-1, keepdims=True)
    return xc * lax.rsqrt(var + LN_EPS) * g + b


def _pack_bf16_pairs(x):
    w = x.shape[1] // 2
    lo = lax.bitcast_convert_type(x[:, :w].astype(BF16).astype(F32), jnp.uint32) >> 16
    hi = lax.bitcast_convert_type(x[:, w:].astype(BF16).astype(F32), jnp.uint32) & jnp.uint32(0xFFFF0000)
    return lax.bitcast_convert_type(lo | hi, jnp.int32)


def _unpack_bf16_pairs(p):
    u = lax.bitcast_convert_type(p, jnp.uint32)
    lo = lax.bitcast_convert_type(u << 16, F32)
    hi = lax.bitcast_convert_type(u & jnp.uint32(0xFFFF0000), F32)
    return jnp.concatenate([lo, hi], axis=1)


def _split3(x):
    hi = x.astype(BF16)
    r1 = x - hi.astype(F32)
    mid = r1.astype(BF16)
    lo = (r1 - mid.astype(F32)).astype(BF16)
    return hi, mid, lo


def _mods_kernel(c_ref, w_ref, b_ref, o_ref):
    a = _silu(c_ref[...])
    o_ref[...] = jnp.dot(a, w_ref[...], precision=HIGHEST, preferred_element_type=F32) + b_ref[...]


def _ada_mods(c, w, b):
    bsz, d = c.shape
    n = w.shape[1]
    tn = 1536
    assert n % tn == 0
    return pl.pallas_call(
        _mods_kernel,
        out_shape=jax.ShapeDtypeStruct((bsz, n), F32),
        grid=(n // tn,),
        in_specs=[pl.BlockSpec((bsz, d), lambda j: (0, 0)),
                  pl.BlockSpec((d, tn), lambda j: (0, j)),
                  pl.BlockSpec((1, tn), lambda j: (0, j))],
        out_specs=pl.BlockSpec((bsz, tn), lambda j: (0, j)),
        compiler_params=_cparams(("arbitrary",)),
        name="ada_mods",
    )(c, w, b.reshape(1, n))


def _proj_pieces(d_model):
    dk = d_model // 2
    pieces = [("q_gla", dk, "scale_q_gla"), ("k_gla", dk, None), ("v_gla", d_model, None), ("r_gla", d_model, "silu")]
    for name, post in (("q_att", "scale_q_att"), ("k_att", None), ("v_att", None)):
        for g, (_, dilation) in enumerate(DIL_PATTERNS):
            pieces.append((f"{name}{g}", DIL_GROUP_WIDTH, (post, dilation)))
    pieces += [("g_gla", d_model, "sigmoid"), ("g_att", d_model, "sigmoid"), ("lr", LANES, "lowrank")]
    return tuple(pieces)


WT_BLOCK = 512


def _wprep_kernel(n_main_blocks, w_ref, o_ref):
    blk = w_ref[0]
    row = lax.broadcasted_iota(jnp.int32, blk.shape, 0)
    keep = (pl.program_id(0) < n_main_blocks) | (row < GLA_LOWRANK)
    o_ref[...] = jnp.where(keep, blk, 0.0).astype(BF16)


def _prep_in_weight(w_in, lr0):
    w_t = jnp.swapaxes(w_in, 1, 2)
    _, n_in, d = w_t.shape
    n_main = n_in - GLA_LOWRANK
    assert lr0 % WT_BLOCK == 0 and n_main % WT_BLOCK == 0
    n_main_blocks = n_main // WT_BLOCK

    def src_row(j):
        start = j * WT_BLOCK
        octet = jnp.where(j < n_main_blocks, (start + jnp.where(start >= lr0, GLA_LOWRANK, 0)) // 8, lr0 // 8)
        return octet * 8

    return pl.pallas_call(
        functools.partial(_wprep_kernel, n_main_blocks),
        out_shape=jax.ShapeDtypeStruct((n_main + WT_BLOCK, d), BF16),
        grid=(n_main_blocks + 1,),
        in_specs=[pl.BlockSpec((pl.Element(1), pl.Element(WT_BLOCK), pl.Element(d)), lambda j: (0, src_row(j), 0))],
        out_specs=pl.BlockSpec((WT_BLOCK, d), lambda j: (j, 0)),
        compiler_params=_cparams(("parallel",)),
        name="prep_in_weight",
    )(w_t)


def _proj_kernel(pieces, head_k, x_ref, sc_ref, sh_ref, w_ref, *refs):
    out_refs, stage_ref = refs[:-1], refs[-1]
    tm = x_ref.shape[1]
    u = (x_ref[0] * (1.0 + sc_ref[0]) + sh_ref[0]).astype(BF16)
    off = 0
    for (name, width, post), o_ref in zip(pieces, out_refs):
        chunk = min(width, 512)
        for c0 in range(0, width, chunk):
            acc = lax.dot_general(u, w_ref[off + c0:off + c0 + chunk, :], NT_DIMS, preferred_element_type=F32)
            if post == "silu":
                acc = _silu(acc)
            elif post == "sigmoid":
                acc = _sigmoid(acc)
            elif post == "scale_q_gla":
                acc = acc * (head_k ** -0.5)
            if post == "lowrank":
                o_ref[0] = acc[:, :GLA_LOWRANK]
            elif isinstance(post, tuple):
                scale, dilation = post
                if scale is not None:
                    acc = acc * (DIL_HEAD_DIM ** -0.5 * LOG2E)
                if dilation == 1:
                    o_ref[0, 0] = acc.astype(o_ref.dtype)
                else:
                    for t in range(width // LANES):
                        stage_ref[t] = acc[:, t * LANES:(t + 1) * LANES]
                    for r in range(dilation):
                        for t in range(width // LANES):
                            o_ref[0, r, :, t * LANES:(t + 1) * LANES] = stage_ref[
                                t, pl.ds(r, tm // dilation, stride=dilation), :].astype(o_ref.dtype)
            else:
                o_ref[0, :, c0:c0 + chunk] = acc.astype(o_ref.dtype)
        off += width


def _in_projection(x, sc1, sh1, w_perm):
    bsz, s, d = x.shape
    pieces = _proj_pieces(d)
    assert sum(p[1] for p in pieces) <= w_perm.shape[0]
    tm = min(ROW_TILE, s)
    assert s % tm == 0
    head_k = (d // 2) // GLA_HEADS
    out_shape, out_specs = [], []
    for name, width, post in pieces:
        if post == "lowrank":
            out_shape.append(jax.ShapeDtypeStruct((bsz, s, GLA_LOWRANK), F32))
            out_specs.append(pl.BlockSpec((1, tm, GLA_LOWRANK), lambda b, i: (b, i, 0)))
        elif isinstance(post, tuple):
            dil = post[1]
            assert tm % (dil * 16) == 0
            out_shape.append(jax.ShapeDtypeStruct((bsz, dil, s // dil, width), BF16))
            out_specs.append(pl.BlockSpec((1, dil, tm // dil, width), lambda b, i: (b, 0, i, 0)))
        else:
            out_shape.append(jax.ShapeDtypeStruct((bsz, s, width), BF16))
            out_specs.append(pl.BlockSpec((1, tm, width), lambda b, i: (b, i, 0)))
    outs = pl.pallas_call(
        functools.partial(_proj_kernel, pieces, head_k),
        out_shape=out_shape,
        grid=(bsz, s // tm),
        in_specs=[pl.BlockSpec((1, tm, d), lambda b, i: (b, i, 0)),
                  pl.BlockSpec((1, 1, d), lambda b, i: (b, 0, 0)),
                  pl.BlockSpec((1, 1, d), lambda b, i: (b, 0, 0)),
                  pl.BlockSpec(w_perm.shape, lambda b, i: (0, 0), pipeline_mode=pl.Buffered(1))],
        out_specs=out_specs,
        scratch_shapes=[pltpu.VMEM((DIL_GROUP_WIDTH // LANES, tm, LANES), F32)],
        compiler_params=_cparams(("parallel", "arbitrary")),
        name="in_projection",
    )(x, sc1, sh1, w_perm)
    return dict(zip([p[0] for p in pieces], outs))


def _gla_kernel(n_chunks, head_k, head_v, q_ref, k_ref, v_ref, r_ref, lr_ref, wg_ref, bg_ref, ng_ref, o_ref,
                state_ref):
    @pl.when(pl.program_id(1) == 0)
    def _():
        state_ref[...] = jnp.zeros_like(state_ref)

    c = GLA_CHUNK
    row = lax.broadcasted_iota(jnp.int32, (c, c), 0)
    col = lax.broadcasted_iota(jnp.int32, (c, c), 1)
    causal = row >= col
    tril = causal.astype(BF16)
    mid = c // 2 - 1
    lr, wg = lr_ref[0], wg_ref[...]
    lr_hi, wg_hi = lr.astype(BF16), wg.astype(BF16)
    lr_lo, wg_lo = (lr - lr_hi.astype(F32)).astype(BF16), (wg - wg_hi.astype(F32)).astype(BF16)
    gate_in = (jnp.dot(lr_hi, wg_hi, preferred_element_type=F32) + jnp.dot(lr_lo, wg_hi, preferred_element_type=F32)
               + jnp.dot(lr_hi, wg_lo, preferred_element_type=F32)) + bg_ref[...]
    g_all = (jnp.minimum(gate_in, 0.0) - jnp.log(1.0 + jnp.exp(-jnp.abs(gate_in)))) * (1.0 / GLA_TAU)
    g_hi, g_mid, g_lo = _split3(g_all)
    for ci in range(n_chunks):
        rows = slice(ci * c, (ci + 1) * c)
        bc = (jnp.dot(tril, g_hi[rows], preferred_element_type=F32)
              + jnp.dot(tril, g_mid[rows], preferred_element_type=F32)
              + jnp.dot(tril, g_lo[rows], preferred_element_type=F32))
        b_mid = bc[mid:mid + 1, :]
        b_last = bc[c - 1:c, :]
        qf = q_ref[0, rows, :].astype(F32)
        kf = k_ref[0, rows, :].astype(F32)
        q_in = (qf * jnp.exp(bc - b_mid)).astype(BF16)
        k_in = (kf * jnp.exp(b_mid - bc)).astype(BF16)
        q_st = (qf * jnp.exp(bc)).astype(BF16)
        k_st = (kf * jnp.exp(b_last - bc)).astype(BF16)
        dec = jnp.exp(b_last)
        for h in range(GLA_HEADS):
            ks = slice(h * head_k, (h + 1) * head_k)
            vs = slice(h * head_v, (h + 1) * head_v)
            vh = v_ref[0, rows, vs]
            att = lax.dot_general(q_in[:, ks], k_in[:, ks], NT_DIMS, preferred_element_type=F32)
            att = jnp.where(causal, att, 0.0).astype(BF16)
            st = state_ref[h]
            o = jnp.dot(att, vh, preferred_element_type=F32)
            o = o + lax.dot_general(q_st[:, ks], st.astype(BF16), NT_DIMS, preferred_element_type=F32)
            kv_t = lax.dot_general(vh, k_st[:, ks], TN_DIMS, preferred_element_type=F32)
            state_ref[h] = st * dec[:, ks] + kv_t
            ms = jnp.mean(o * o, axis=-1, keepdims=True)
            o = o * lax.rsqrt(ms + LN_EPS) * ng_ref[:, vs] * r_ref[0, rows, vs].astype(F32)
            o_ref[0, rows, vs] = o.astype(o_ref.dtype)


def _gla(q, k, v, r_silu, lr, w_gate, b_gate, norm_g):
    bsz, s, dk = q.shape
    dv = v.shape[-1]
    head_k, head_v = dk // GLA_HEADS, dv // GLA_HEADS
    n_chunks = min(GLA_STEP_CHUNKS, s // GLA_CHUNK)
    ct = GLA_CHUNK * n_chunks
    assert s % ct == 0
    row_spec = lambda w: pl.BlockSpec((1, ct, w), lambda b, i: (b, i, 0))
    full = lambda a: pl.BlockSpec(a.shape, lambda b, i: (0,) * a.ndim)
    bg = b_gate.reshape(1, dk)
    ng = norm_g.reshape(1, dv)
    return pl.pallas_call(
        functools.partial(_gla_kernel, n_chunks, head_k, head_v),
        out_shape=jax.ShapeDtypeStruct((bsz, s, dv), BF16),
        grid=(bsz, s // ct),
        in_specs=[row_spec(dk), row_spec(dk), row_spec(dv), row_spec(dv), row_spec(GLA_LOWRANK),
                  full(w_gate), full(bg), full(ng)],
        out_specs=row_spec(dv),
        scratch_shapes=[pltpu.VMEM((GLA_HEADS, head_v, head_k), F32)],
        compiler_params=_cparams(("parallel", "arbitrary")),
        name="gla",
    )(q, k, v, r_silu, lr, w_gate, bg, ng)


def _t5_bucket_np(dist):
    exact = REL_BUCKETS // 2
    d = np.maximum(dist, 1).astype(np.float32)
    large = exact + (np.log(d / np.float32(exact)) / np.float32(math.log(REL_MAX_DIST / exact))
                     * np.float32(REL_BUCKETS - exact)).astype(np.int32)
    large = np.minimum(large, REL_BUCKETS - 1)
    return np.where(dist < exact, dist, large).astype(np.int32)


def _band_tables(window, dilation):
    qi = np.arange(DIL_BLOCK)[:, None]
    kj = np.arange(2 * DIL_BLOCK)[None, :]
    m = qi + DIL_BLOCK - kj
    n_steps = window // dilation
    band = (m >= 0) & (m <= n_steps)
    bucket = _t5_bucket_np(np.clip(m, 0, n_steps) * dilation)
    return np.where(band, bucket, -1).astype(np.int32)


def _attn_kernel(nq, table_ref, bucket_ref, q_ref, kp_ref, kc_ref, vp_ref, vc_ref, o_ref, lse_ref,
                 bias_ref, p_ref):
    i = pl.program_id(1)
    blk = DIL_BLOCK
    hpg = DIL_HEADS_PER_GROUP
    n_pairs = hpg // 2

    @pl.when((pl.program_id(0) == 0) & (i == 0))
    def _():
        bucket = bucket_ref[...]
        for h in range(hpg):
            acc = jnp.full(bucket.shape, NEG, F32)
            for bkt in range(REL_BUCKETS):
                acc = jnp.where(bucket == bkt, table_ref[bkt, h] * LOG2E, acc)
            bias_ref[h * blk:(h + 1) * blk, :] = acc

    lane = lax.broadcasted_iota(jnp.int32, (blk, LANES), 1)
    low = lane < DIL_HEAD_DIM
    ones_rhs = jnp.ones((2 * blk, LANES), BF16)

    def windows(ref_p, ref_c, sq, qb, cols):
        if qb == 0:
            return jnp.concatenate([ref_p[sq, :, cols], ref_c[sq, 0:blk, cols]], axis=0)
        return ref_c[sq, (qb - 1) * blk:(qb + 1) * blk, cols]

    key_lane = lax.broadcasted_iota(jnp.int32, (1, 2 * blk), 1)
    no_prev = jnp.where((key_lane < blk) & (i == 0), NEG, 0.0)
    items = [(sq, qb, hp) for sq in range(q_ref.shape[0]) for qb in range(nq) for hp in range(n_pairs)]

    mxs = []
    for n, (sq, qb, hp) in enumerate(items):
        rows = slice(qb * blk, (qb + 1) * blk)
        cols = slice(hp * LANES, (hp + 1) * LANES)
        qp = q_ref[sq, rows, cols]
        zero = jnp.zeros_like(qp)
        qq = jnp.concatenate([jnp.where(low, qp, zero), jnp.where(low, zero, qp)], axis=0)
        keys = windows(kp_ref, kc_ref, sq, qb, cols)
        s = lax.dot_general(qq, keys, NT_DIMS, preferred_element_type=F32) + bias_ref[2 * hp * blk:(2 * hp + 2) * blk, :]
        if qb == 0:
            s = s + no_prev
        mx = jnp.max(s, axis=-1, keepdims=True)
        p_ref[n * 2 * blk:(n + 1) * 2 * blk, :] = jnp.exp2(s - mx).astype(BF16)
        mxs.append(mx)

    for n, (sq, qb, hp) in enumerate(items):
        rows = slice(qb * blk, (qb + 1) * blk)
        cols = slice(hp * LANES, (hp + 1) * LANES)
        vals = windows(vp_ref, vc_ref, sq, qb, cols)
        rhs = jnp.concatenate([vals, ones_rhs], axis=1)
        res = jnp.dot(p_ref[n * 2 * blk:(n + 1) * 2 * blk, :], rhs, preferred_element_type=F32)
        num = jnp.where(low, res[0:blk, 0:LANES], res[blk:2 * blk, 0:LANES])
        den = jnp.where(low, res[0:blk, LANES:], res[blk:2 * blk, LANES:])
        mx = jnp.where(low, mxs[n][0:blk], mxs[n][blk:2 * blk])
        o_ref[sq, rows, cols] = (num / den).astype(o_ref.dtype)
        lse_ref[sq, rows, cols] = (mx + jnp.log2(den)) * LN2


def _dilated_group_attention(q, k, v, table, window, dilation):
    bb, l, w = q.shape
    nq = min(ATT_STEP_BLOCKS, l // DIL_BLOCK)
    nsq = ATT_STEP_BLOCKS // nq
    assert l % (nq * DIL_BLOCK) == 0 and bb % nsq == 0
    steps = l // (nq * DIL_BLOCK)
    bucket = jnp.asarray(_band_tables(window, dilation))
    cur = pl.BlockSpec((nsq, nq * DIL_BLOCK, w), lambda b, i: (b, i, 0))
    prev = pl.BlockSpec((nsq, DIL_BLOCK, w), lambda b, i: (b, jnp.maximum(nq * i - 1, 0), 0))
    rows_all = nsq * nq * DIL_HEADS_PER_GROUP * DIL_BLOCK
    return pl.pallas_call(
        functools.partial(_attn_kernel, nq),
        out_shape=[jax.ShapeDtypeStruct((bb, l, w), BF16), jax.ShapeDtypeStruct((bb, l, w), F32)],
        grid=(bb // nsq, steps),
        in_specs=[pl.BlockSpec(memory_space=pltpu.SMEM),
                  pl.BlockSpec(bucket.shape, lambda b, i: (0, 0)),
                  cur, prev, cur, prev, cur],
        out_specs=[cur, cur],
        scratch_shapes=[pltpu.VMEM((DIL_HEADS_PER_GROUP * DIL_BLOCK, 2 * DIL_BLOCK), F32),
                        pltpu.VMEM((rows_all, 2 * DIL_BLOCK), BF16)],
        compiler_params=_cparams(("arbitrary", "arbitrary")),
        name=f"dilated_attn_d{dilation}",
    )(table, bucket, q, k, k, v, v)


def _merge_kernel(alpha, dilations, ygla_ref, o0_ref, o1_ref, o2_ref, l0_ref, l1_ref, l2_ref, gg_ref, ga_ref, x_ref,
                  g1_ref, sc2_ref, sh2_ref, ln_g_ref, ln_b_ref, wpg_ref, wpa_ref, wout_ref, wr_ref, br_ref, ltri_ref,
                  x1_ref, u2_ref, route_ref, ew_ref, cnt_ref, stage_ref, carry_ref):
    tm = x_ref.shape[1]

    @pl.when((pl.program_id(0) == 0) & (pl.program_id(1) == 0))
    def _():
        carry_ref[...] = jnp.zeros_like(carry_ref)

    n_lt = DIL_GROUP_WIDTH // LANES
    group_refs = tuple(zip((l0_ref, l1_ref, l2_ref), (o0_ref, o1_ref, o2_ref), dilations))
    for gi, (l_ref, o_ref, dil) in enumerate(group_refs):
        if dil > 1:
            for slot, ref in ((2 * gi, l_ref), (2 * gi + 1, o_ref)):
                for r in range(dil):
                    for t in range(n_lt):
                        stage_ref[slot, t, pl.ds(r, tm // dil, stride=dil), :] = ref[
                            0, r, :, t * LANES:(t + 1) * LANES].astype(F32)

    wr = wr_ref[...]
    w_hi = wr.astype(BF16)
    w_lo = (wr - w_hi.astype(F32)).astype(BF16)
    sub = ltri_ref.shape[0]
    for rows in (slice(r0, r0 + sub) for r0 in range(0, tm, sub)):
        def natural(ref, dil, slot):
            if dil == 1:
                return ref[0, 0, rows, :].astype(F32)
            return jnp.concatenate([stage_ref[slot, t, rows, :] for t in range(n_lt)], axis=1)

        lses = [natural(l_ref, dil, 2 * gi) for gi, (l_ref, _, dil) in enumerate(group_refs)]
        outs = [natural(o_ref, dil, 2 * gi + 1) for gi, (_, o_ref, dil) in enumerate(group_refs)]
        lm = jnp.maximum(jnp.maximum(lses[0], lses[1]), lses[2])
        es = [jnp.exp(l - lm) for l in lses]
        y_att = (es[0] * outs[0] + es[1] * outs[1] + es[2] * outs[2]) / (es[0] + es[1] + es[2])

        p_gla = jnp.dot(ygla_ref[0, rows, :], wpg_ref[...], preferred_element_type=F32)
        p_att = jnp.dot(y_att.astype(BF16), wpa_ref[...], preferred_element_type=F32)
        merged = gg_ref[0, rows, :].astype(F32) * p_gla + ga_ref[0, rows, :].astype(F32) * p_att
        y = jnp.dot(merged.astype(BF16), wout_ref[...], preferred_element_type=F32)
        x1 = _layer_norm(alpha * x_ref[0, rows, :] + g1_ref[0] * y, ln_g_ref[...], ln_b_ref[...])
        x1_ref[0, rows, :] = x1
        u2 = x1 * (1.0 + sc2_ref[0]) + sh2_ref[0]
        u2_ref[0, rows, :] = _pack_bf16_pairs(u2)

        u_hi = u2.astype(BF16)
        u_lo = (u2 - u_hi.astype(F32)).astype(BF16)
        logits = (jnp.dot(u_hi, w_hi, preferred_element_type=F32) + jnp.dot(u_lo, w_hi, preferred_element_type=F32)
                  + jnp.dot(u_hi, w_lo, preferred_element_type=F32)) + br_ref[...]
        lane = lax.broadcasted_iota(jnp.int32, logits.shape, 1)
        big = jnp.int32(LANES)
        lg = jnp.where(lane < MOE_GROUPS, logits, NEG)
        gmax = jnp.max(lg, axis=-1, keepdims=True)
        gidx = jnp.min(jnp.where(lg == gmax, lane, big), axis=-1, keepdims=True)
        gval = 1.0 / jnp.sum(jnp.exp(lg - gmax), axis=-1, keepdims=True)
        in_group = (lane >= MOE_GROUPS + gidx * MOE_EXPERTS) & (lane < MOE_GROUPS + (gidx + 1) * MOE_EXPERTS)
        le = jnp.where(in_group, logits, NEG)
        m1 = jnp.max(le, axis=-1, keepdims=True)
        i1 = jnp.min(jnp.where(le == m1, lane, big), axis=-1, keepdims=True)
        le2 = jnp.where(lane == i1, NEG, le)
        m2 = jnp.max(le2, axis=-1, keepdims=True)
        i2 = jnp.min(jnp.where(le2 == m2, lane, big), axis=-1, keepdims=True)
        t = jnp.exp(m2 - m1)
        w1 = 1.0 / (1.0 + t)
        w2 = t * w1

        hit1, hit2 = lane == i1, lane == i2
        onehot = jnp.where(hit1 | hit2, 1.0, 0.0)
        earlier = jnp.dot(ltri_ref[...], onehot.astype(BF16), preferred_element_type=F32) + carry_ref[...]
        rank1 = jnp.sum(jnp.where(hit1, earlier, 0.0), axis=-1, keepdims=True).astype(jnp.int32)
        rank2 = jnp.sum(jnp.where(hit2, earlier, 0.0), axis=-1, keepdims=True).astype(jnp.int32)
        carry_ref[...] = carry_ref[...] + jnp.sum(onehot, axis=0, keepdims=True)
        route = jnp.where(lane == 0, i1 - MOE_GROUPS, jnp.where(lane == 1, i2 - MOE_GROUPS,
                          jnp.where(lane == 2, rank1, jnp.where(lane == 3, rank2, 0))))
        route_ref[0, :, rows] = jnp.transpose(route)[0:ROUTE_ROWS, :]
        ew_ref[0, rows, :] = jnp.where(lane == 0, gval * w1, jnp.where(lane == 1, gval * w2, 0.0))
    cnt_ref[...] = carry_ref[...].astype(jnp.int32)


def _merge(alpha, y_gla, o_groups, lse_groups, g_gla, g_att, x, g1, sc2, sh2, ln_g, ln_b, wpg, wpa, wout, wr, br):
    bsz, s, d = x.shape
    tm = min(ROW_TILE, s)
    assert s % tm == 0
    dilations = tuple(dil for _, dil in DIL_PATTERNS)
    row = lambda w: pl.BlockSpec((1, tm, w), lambda b, i: (b, i, 0))
    sub = lambda dil: pl.BlockSpec((1, dil, tm // dil, DIL_GROUP_WIDTH), lambda b, i: (b, 0, i, 0))
    per_b = pl.BlockSpec((1, 1, d), lambda b, i: (b, 0, 0))
    full = lambda a: pl.BlockSpec(a.shape, lambda b, i: (0,) * a.ndim)
    ln_g2, ln_b2 = ln_g.reshape(1, d), ln_b.reshape(1, d)
    sub_rows = min(MERGE_SUB_ROWS, tm)
    assert tm % sub_rows == 0
    ltri = jnp.asarray(np.tril(np.ones((sub_rows, sub_rows), np.float32), -1), BF16)
    return pl.pallas_call(
        functools.partial(_merge_kernel, alpha, dilations),
        out_shape=[jax.ShapeDtypeStruct((bsz, s, d), F32), jax.ShapeDtypeStruct((bsz, s, d // 2), jnp.int32),
                   jax.ShapeDtypeStruct((bsz, ROUTE_ROWS, s), jnp.int32), jax.ShapeDtypeStruct((bsz, s, LANES), F32),
                   jax.ShapeDtypeStruct((1, LANES), jnp.int32)],
        grid=(bsz, s // tm),
        in_specs=[row(y_gla.shape[-1])] + [sub(dil) for dil in dilations] * 2
                 + [row(d), row(d), row(d), per_b, per_b, per_b, full(ln_g2), full(ln_b2),
                    full(wpg), full(wpa), full(wout), full(wr), full(br), full(ltri)],
        out_specs=[row(d), row(d // 2), pl.BlockSpec((1, ROUTE_ROWS, tm), lambda b, i: (b, 0, i)), row(LANES),
                   pl.BlockSpec((1, LANES), lambda b, i: (0, 0))],
        scratch_shapes=[pltpu.VMEM((2 * DIL_GROUPS, DIL_GROUP_WIDTH // LANES, tm, LANES), F32),
                        pltpu.VMEM((1, LANES), F32)],
        compiler_params=_cparams(("arbitrary", "arbitrary")),
        name="merge_ln1_router",
    )(y_gla, *o_groups, *lse_groups, g_gla, g_att, x, g1, sc2, sh2, ln_g2, ln_b2, wpg, wpa, wout, wr, br, ltri)


def _expert_kernel(run_ref, valid_ref, rexp_ref, used_ref, x_ref, wg_hbm, wu_hbm, wd_hbm, o_ref,
                   wg_f, wu_f, wd_f, wg_s, wu_s, wd_s, sem):
    t = pl.program_id(0)
    n_tiles_used, n_runs = used_ref[0], used_ref[1]
    run = run_ref[t]
    active = t < n_tiles_used
    first_of_run = (t == 0) | (run_ref[jnp.maximum(t - 1, 0)] != run)

    def weight_copies(r):
        e, slot = rexp_ref[r], r % 2
        return [pltpu.make_async_copy(hbm.at[e], buf.at[slot], sem.at[slot, j])
                for j, (hbm, buf) in enumerate(((wg_hbm, wg_f), (wu_hbm, wu_f), (wd_hbm, wd_f)))]

    @pl.when(active & (t == 0))
    def _():
        for cp in weight_copies(0):
            cp.start()

    @pl.when(active & first_of_run)
    def _():
        @pl.when(run + 1 < n_runs)
        def _():
            for cp in weight_copies(run + 1):
                cp.start()

        for cp in weight_copies(run):
            cp.wait()
        slot = run % 2
        wg_s[...] = wg_f[slot].astype(BF16)
        wu_s[...] = wu_f[slot].astype(BF16)
        wd_s[...] = wd_f[slot].astype(BF16)

    n_valid = jnp.where(active, valid_ref[t], 0)
    for r0 in range(0, x_ref.shape[0], EXPERT_BLOCK):
        rows = slice(r0, r0 + EXPERT_BLOCK)

        @pl.when(n_valid > r0)
        def _():
            xt = _unpack_bf16_pairs(x_ref[rows, :]).astype(BF16)
            hg = jnp.dot(xt, wg_s[...], preferred_element_type=F32)
            hu = jnp.dot(xt, wu_s[...], preferred_element_type=F32)
            h = (_silu(hg) * hu).astype(BF16)
            o_ref[rows, :] = _pack_bf16_pairs(jnp.dot(h, wd_s[...], preferred_element_type=F32))

        @pl.when(n_valid <= r0)
        def _():
            o_ref[rows, :] = jnp.zeros((EXPERT_BLOCK, o_ref.shape[1]), o_ref.dtype)


def _expert_ffn(tile_run, tile_valid, run_expert, used, xg, w_gate, w_up, w_down):
    p = xg.shape[0]
    ne, d, ff = w_gate.shape
    tm = EXPERT_TILE
    n_tiles = p // tm
    hbm = pl.BlockSpec(memory_space=pl.ANY)
    grid_spec = pltpu.PrefetchScalarGridSpec(
        num_scalar_prefetch=4,
        grid=(n_tiles,),
        in_specs=[pl.BlockSpec((tm, d // 2), lambda t, *_: (t, 0)), hbm, hbm, hbm],
        out_specs=pl.BlockSpec((tm, d // 2), lambda t, *_: (t, 0)),
        scratch_shapes=[pltpu.VMEM((2, d, ff), F32), pltpu.VMEM((2, d, ff), F32), pltpu.VMEM((2, ff, d), F32),
                        pltpu.VMEM((d, ff), BF16), pltpu.VMEM((d, ff), BF16), pltpu.VMEM((ff, d), BF16),
                        pltpu.SemaphoreType.DMA((2, 3))],
    )
    return pl.pallas_call(
        _expert_kernel,
        out_shape=jax.ShapeDtypeStruct((p, d // 2), jnp.int32),
        grid_spec=grid_spec,
        compiler_params=_cparams(("arbitrary",)),
        name="expert_ffn",
    )(tile_run, tile_valid, run_expert, used, xg, w_gate, w_up, w_down)


def _final_kernel(alpha, x1_ref, ya_ref, yb_ref, ew_ref, g2_ref, ln_g_ref, ln_b_ref, o_ref):
    ew = ew_ref[0]
    y = ew[:, 0:1] * _unpack_bf16_pairs(ya_ref[0]) + ew[:, 1:2] * _unpack_bf16_pairs(yb_ref[0])
    o_ref[0] = _layer_norm(alpha * x1_ref[0] + g2_ref[0] * y, ln_g_ref[...], ln_b_ref[...])


def _final(alpha, x1, ya, yb, ew, g2, ln_g, ln_b):
    bsz, s, d = x1.shape
    tm = min(ROW_TILE, s)
    row = lambda w: pl.BlockSpec((1, tm, w), lambda b, i: (b, i, 0))
    full = lambda a: pl.BlockSpec(a.shape, lambda b, i: (0,) * a.ndim)
    ln_g2, ln_b2 = ln_g.reshape(1, d), ln_b.reshape(1, d)
    return pl.pallas_call(
        functools.partial(_final_kernel, alpha),
        out_shape=jax.ShapeDtypeStruct((bsz, s, d), F32),
        grid=(bsz, s // tm),
        in_specs=[row(d), row(d // 2), row(d // 2), row(LANES), pl.BlockSpec((1, 1, d), lambda b, i: (b, 0, 0)),
                  full(ln_g2), full(ln_b2)],
        out_specs=row(d),
        compiler_params=_cparams(("parallel", "arbitrary")),
        name="combine_ln2",
    )(x1, ya, yb, ew, g2, ln_g2, ln_b2)


SC_CORES = 2
SC_SUBCORES = 16
SC_CHUNK = 64


def _sc_mesh():
    return plsc.VectorSubcoreMesh(core_axis_name="c", subcore_axis_name="s")


def _sc_scatter_rows(rows, dest0, dest1, n_rows):
    n, w = rows.shape
    n_workers = SC_CORES * SC_SUBCORES
    assert n % (n_workers * SC_CHUNK) == 0
    n_chunks = n // (n_workers * SC_CHUNK)
    d0 = dest0.reshape(n // SC_CHUNK, 1, SC_CHUNK)
    d1 = dest1.reshape(n // SC_CHUNK, 1, SC_CHUNK)

    @functools.partial(
        pl.kernel, mesh=_sc_mesh(), out_type=jax.ShapeDtypeStruct((n_rows, w), rows.dtype),
        scratch_types=[pltpu.VMEM((n_chunks, 1, SC_CHUNK), jnp.int32), pltpu.VMEM((n_chunks, 1, SC_CHUNK), jnp.int32),
                       pltpu.VMEM((2, SC_CHUNK, w), rows.dtype),
                       pltpu.SemaphoreType.DMA((2,)), pltpu.SemaphoreType.DMA((2, 2))])
    def scatter_kernel(rows_hbm, d0_hbm, d1_hbm, out_hbm, i0_v, i1_v, rows_v, read_sem, scat_sem):
        wid = lax.axis_index("s") * SC_CORES + lax.axis_index("c")
        first = wid * n_chunks
        pltpu.sync_copy(d0_hbm.at[pl.ds(first, n_chunks)], i0_v)
        pltpu.sync_copy(d1_hbm.at[pl.ds(first, n_chunks)], i1_v)

        def read(j):
            return pltpu.make_async_copy(rows_hbm.at[pl.ds((first + j) * SC_CHUNK, SC_CHUNK)], rows_v.at[j % 2],
                                         read_sem.at[j % 2])

        def scatters(j):
            return [pltpu.make_async_copy(rows_v.at[j % 2], out_hbm.at[idx.at[j].at[0]], scat_sem.at[j % 2, k])
                    for k, idx in enumerate((i0_v, i1_v))]

        read(0).start()
        for j in range(n_chunks):
            read(j).wait()
            if j + 1 < n_chunks:
                if j >= 1:
                    for cp in scatters(j - 1):
                        cp.wait()
                read(j + 1).start()
            for cp in scatters(j):
                cp.start()
        for j in range(max(n_chunks - 2, 0), n_chunks):
            for cp in scatters(j):
                cp.wait()

    return scatter_kernel(rows, d0, d1)


def _sc_gather_rows(table, dest0, dest1):
    n = dest0.shape[0]
    w = table.shape[1]
    n_workers = SC_CORES * SC_SUBCORES
    assert n % (n_workers * SC_CHUNK) == 0
    n_chunks = n // (n_workers * SC_CHUNK)
    d0 = dest0.reshape(n // SC_CHUNK, 1, SC_CHUNK)
    d1 = dest1.reshape(n // SC_CHUNK, 1, SC_CHUNK)
    out = jax.ShapeDtypeStruct((n, w), table.dtype)

    @functools.partial(
        pl.kernel, mesh=_sc_mesh(), out_type=(out, out),
        scratch_types=[pltpu.VMEM((n_chunks, 1, SC_CHUNK), jnp.int32), pltpu.VMEM((n_chunks, 1, SC_CHUNK), jnp.int32),
                       pltpu.VMEM((2, SC_CHUNK, w), table.dtype),
                       pltpu.SemaphoreType.DMA((2,)), pltpu.SemaphoreType.DMA((2,))])
    def gather_kernel(table_hbm, d0_hbm, d1_hbm, a_hbm, b_hbm, i0_v, i1_v, rows_v, gather_sem, write_sem):
        wid = lax.axis_index("s") * SC_CORES + lax.axis_index("c")
        first = wid * n_chunks
        pltpu.sync_copy(d0_hbm.at[pl.ds(first, n_chunks)], i0_v)
        pltpu.sync_copy(d1_hbm.at[pl.ds(first, n_chunks)], i1_v)
        n_items = 2 * n_chunks

        def gather(m):
            idx = (i0_v, i1_v)[m % 2]
            return pltpu.make_async_copy(table_hbm.at[idx.at[m // 2].at[0]], rows_v.at[m % 2], gather_sem.at[m % 2])

        def write(m):
            o_hbm = (a_hbm, b_hbm)[m % 2]
            return pltpu.make_async_copy(rows_v.at[m % 2], o_hbm.at[pl.ds((first + m // 2) * SC_CHUNK, SC_CHUNK)],
                                         write_sem.at[m % 2])

        gather(0).start()
        for m in range(n_items):
            gather(m).wait()
            if m + 1 < n_items:
                if m >= 1:
                    write(m - 1).wait()
                gather(m + 1).start()
            write(m).start()
        for m in range(max(n_items - 2, 0), n_items):
            write(m).wait()

    return gather_kernel(table, d0, d1)


def _dispatch_plan(route, counts):
    tm = EXPERT_TILE
    e0, e1, r0, r1 = (route[:, j, :].reshape(-1) for j in range(4))
    experts = jnp.arange(MOE_TOTAL, dtype=jnp.int32)
    tiles_per = (counts + tm - 1) // tm
    tile_end = jnp.cumsum(tiles_per)
    pad_start = ((tile_end - tiles_per) * tm).astype(jnp.int32)

    def lookup(e):
        return jnp.sum(jnp.where(e[None, :] == experts[:, None], pad_start[:, None], 0), axis=0)

    dest0, dest1 = lookup(e0) + r0, lookup(e1) + r1
    n_tiles = (2 * e0.size + MOE_TOTAL * tm) // tm
    tile_expert = jnp.minimum(jnp.sum(tile_end[None, :] <= jnp.arange(n_tiles)[:, None], axis=1), MOE_TOTAL - 1)
    nonempty = counts > 0
    run_of_expert = jnp.cumsum(nonempty.astype(jnp.int32)) - 1
    run_expert = jnp.sum(jnp.where(nonempty[None, :] & (run_of_expert[None, :] == experts[:, None]),
                                   experts[None, :], 0), axis=1).astype(jnp.int32)
    of_tile = tile_expert[:, None] == experts[None, :]
    tile_run = jnp.sum(jnp.where(of_tile, run_of_expert[None, :], 0), axis=1).astype(jnp.int32)
    rows_left = (counts + pad_start)[None, :] - jnp.arange(n_tiles)[:, None] * tm
    tile_valid = jnp.clip(jnp.sum(jnp.where(of_tile, rows_left, 0), axis=1), 0, tm).astype(jnp.int32)
    used = jnp.stack([tile_end[-1], jnp.sum(nonempty)]).astype(jnp.int32)
    return dest0, dest1, tile_run, tile_valid, run_expert, used, n_tiles * tm


def _layer(x, c, rel_bias, w_ada, b_ada, w_in, w_gla_gate, b_gla_gate, gla_norm, w_proj_gla, w_proj_attn, w_out,
           ln1_g, ln1_b, w_rg, b_rg, w_re, b_re, w_eg, w_eu, w_ed, ln2_g, ln2_b):
    bsz, s, d = x.shape
    alpha = (2.0 * DEPTH) ** 0.25
    mods = _ada_mods(c, w_ada, b_ada)
    sh1, sc1, g1, sh2, sc2, g2 = [m.reshape(bsz, 1, d) for m in jnp.split(mods, N_MOD, axis=-1)]

    lr0 = d // 2 * 2 + 2 * d
    z = _in_projection(x, sc1, sh1, _prep_in_weight(w_in, lr0))

    y_gla = _gla(z["q_gla"], z["k_gla"], z["v_gla"], z["r_gla"], z["lr"], w_gla_gate, b_gla_gate, gla_norm)

    o_groups, lse_groups = [], []
    for g, (window, dilation) in enumerate(DIL_PATTERNS):
        l = s // dilation
        qg, kg, vg = (z[f"{n}{g}"].reshape(bsz * dilation, l, DIL_GROUP_WIDTH) for n in ("q_att", "k_att", "v_att"))
        table = rel_bias[:, g * DIL_HEADS_PER_GROUP:(g + 1) * DIL_HEADS_PER_GROUP]
        o, lse = _dilated_group_attention(qg, kg, vg, table, window, dilation)
        o_groups.append(o.reshape(bsz, dilation, l, DIL_GROUP_WIDTH))
        lse_groups.append(lse.reshape(bsz, dilation, l, DIL_GROUP_WIDTH))

    wr = jnp.concatenate([w_rg, w_re, jnp.zeros((d, LANES - MOE_GROUPS - MOE_TOTAL), F32)], axis=1)
    br = jnp.concatenate([b_rg, b_re, jnp.zeros((LANES - MOE_GROUPS - MOE_TOTAL,), F32)]).reshape(1, LANES)
    x1, u2, route, ew, cnt = _merge(alpha, y_gla, o_groups, lse_groups, z["g_gla"], z["g_att"], x, g1, sc2, sh2,
                                    ln1_g, ln1_b, w_proj_gla.astype(BF16), w_proj_attn.astype(BF16),
                                    w_out.astype(BF16), wr, br)

    n = bsz * s
    counts = cnt[0, MOE_GROUPS:MOE_GROUPS + MOE_TOTAL]
    dest0, dest1, tile_run, tile_valid, run_expert, used, n_rows = _dispatch_plan(route, counts)
    xg = _sc_scatter_rows(u2.reshape(n, d // 2), dest0, dest1, n_rows)
    ff = w_eg.shape[-1]
    yo = _expert_ffn(tile_run, tile_valid, run_expert, used, xg, w_eg.reshape(MOE_TOTAL, d, ff),
                     w_eu.reshape(MOE_TOTAL, d, ff), w_ed.reshape(MOE_TOTAL, ff, d))
    ya, yb = (y.reshape(bsz, s, d // 2) for y in _sc_gather_rows(yo, dest0, dest1))
    return _final(alpha, x1, ya, yb, ew, g2, ln2_g, ln2_b)


def kernel(x, c, rel_bias, w_ada, b_ada, w_in, w_gla_gate, b_gla_gate, gla_norm, w_proj_gla, w_proj_attn, w_out,
           ln1_g, ln1_b, w_router_group, b_router_group, w_router_expert, b_router_expert, w_exp_gate, w_exp_up,
           w_exp_down, ln2_g, ln2_b):
    assert w_ada.shape[0] == DEPTH
    return _layer(x, c, rel_bias, w_ada[0], b_ada[0], w_in[0:1], w_gla_gate[0], b_gla_gate[0], gla_norm[0],
                  w_proj_gla[0], w_proj_attn[0], w_out[0], ln1_g[0], ln1_b[0], w_router_group[0],
                  b_router_group[0], w_router_expert[0], b_router_expert[0], w_exp_gate[0], w_exp_up[0],
                  w_exp_down[0], ln2_g[0], ln2_b[0])
```

```python
import functools
import math

import numpy as np
import jax
import jax.numpy as jnp
from jax import lax
from jax.experimental import pallas as pl
from jax.experimental.pallas import tpu as pltpu
from jax.experimental.pallas import tpu_sc as plsc

F32 = jnp.float32
BF16 = jnp.bfloat16

N_MOD = 6
GLA_HEADS = 4
GLA_LOWRANK = 16
GLA_TAU = 16.0
GLA_CHUNK = 64
DIL_PATTERNS = ((128, 1), (512, 4), (2048, 16))
DIL_GROUPS = len(DIL_PATTERNS)
DIL_HEADS_PER_GROUP = 8
DIL_HEAD_DIM = 64
DIL_GROUP_WIDTH = DIL_HEADS_PER_GROUP * DIL_HEAD_DIM
DIL_BLOCK = 128
REL_BUCKETS = 32
REL_MAX_DIST = 2048
MOE_GROUPS = 4
MOE_EXPERTS = 8
MOE_TOTAL = MOE_GROUPS * MOE_EXPERTS
LN_EPS = 1e-5
DEPTH = 1

LANES = 128
VMEM_LIMIT = 56 * 1024 * 1024
LOG2E = 1.4426950408889634
LN2 = 0.6931471805599453
NEG = -1e30
ROW_TILE = 512
EXPERT_TILE = 512
EXPERT_BLOCK = 256
GLA_STEP_CHUNKS = 8
ATT_STEP_BLOCKS = 4
MERGE_SUB_ROWS = 512
ROUTE_ROWS = 8

HIGHEST = lax.Precision.HIGHEST
NT_DIMS = (((1,), (1,)), ((), ()))
TN_DIMS = (((0,), (0,)), ((), ()))


def _cparams(sem):
    return pltpu.CompilerParams(dimension_semantics=sem, vmem_limit_bytes=VMEM_LIMIT)


def _sigmoid(x):
    return 0.5 * jnp.tanh(0.5 * x) + 0.5


def _silu(x):
    return x * _sigmoid(x)


def _layer_norm(x, g, b):
    mu = jnp.mean(x, axis=-1, keepdims=True)
    xc = x - mu
    var = jnp.mean(xc * xc, axis=-1, keepdims=True)
    return xc * lax.rsqrt(var + LN_EPS) * g + b


def _pack_bf16_pairs(x):
    w = x.shape[1] // 2
    lo = lax.bitcast_convert_type(x[:, :w].astype(BF16).astype(F32), jnp.uint32) >> 16
    hi = lax.bitcast_convert_type(x[:, w:].astype(BF16).astype(F32), jnp.uint32) & jnp.uint32(0xFFFF0000)
    return lax.bitcast_convert_type(lo | hi, jnp.int32)


def _unpack_bf16_pairs(p):
    u = lax.bitcast_convert_type(p, jnp.uint32)
    lo = lax.bitcast_convert_type(u << 16, F32)
    hi = lax.bitcast_convert_type(u & jnp.uint32(0xFFFF0000), F32)
    return jnp.concatenate([lo, hi], axis=1)


def _split3(x):
    hi = x.astype(BF16)
    r1 = x - hi.astype(F32)
    mid = r1.astype(BF16)
    lo = (r1 - mid.astype(F32)).astype(BF16)
    return hi, mid, lo


def _mods_kernel(c_ref, w_ref, b_ref, o_ref):
    a = _silu(c_ref[...])
    o_ref[...] = jnp.dot(a, w_ref[...], precision=HIGHEST, preferred_element_type=F32) + b_ref[...]


def _ada_mods(c, w, b):
    bsz, d = c.shape
    n = w.shape[1]
    tn = 1536
    assert n % tn == 0
    return pl.pallas_call(
        _mods_kernel,
        out_shape=jax.ShapeDtypeStruct((bsz, n), F32),
        grid=(n // tn,),
        in_specs=[pl.BlockSpec((bsz, d), lambda j: (0, 0)),
                  pl.BlockSpec((d, tn), lambda j: (0, j)),
                  pl.BlockSpec((1, tn), lambda j: (0, j))],
        out_specs=pl.BlockSpec((bsz, tn), lambda j: (0, j)),
        compiler_params=_cparams(("arbitrary",)),
        name="ada_mods",
    )(c, w, b.reshape(1, n))


def _proj_pieces(d_model):
    dk = d_model // 2
    pieces = [("q_gla", dk, "scale_q_gla"), ("k_gla", dk, None), ("v_gla", d_model, None), ("r_gla", d_model, "silu")]
    for name, post in (("q_att", "scale_q_att"), ("k_att", None), ("v_att", None)):
        for g, (_, dilation) in enumerate(DIL_PATTERNS):
            pieces.append((f"{name}{g}", DIL_GROUP_WIDTH, (post, dilation)))
    pieces += [("g_gla", d_model, "sigmoid"), ("g_att", d_model, "sigmoid"), ("lr", LANES, "lowrank")]
    return tuple(pieces)


WT_BLOCK = 512


def _wprep_kernel(n_main_blocks, w_ref, o_ref):
    blk = w_ref[0]
    row = lax.broadcasted_iota(jnp.int32, blk.shape, 0)
    keep = (pl.program_id(0) < n_main_blocks) | (row < GLA_LOWRANK)
    o_ref[...] = jnp.where(keep, blk, 0.0).astype(BF16)


def _prep_in_weight(w_in, lr0):
    w_t = jnp.swapaxes(w_in, 1, 2)
    _, n_in, d = w_t.shape
    n_main = n_in - GLA_LOWRANK
    assert lr0 % WT_BLOCK == 0 and n_main % WT_BLOCK == 0
    n_main_blocks = n_main // WT_BLOCK

    def src_row(j):
        start = j * WT_BLOCK
        octet = jnp.where(j < n_main_blocks, (start + jnp.where(start >= lr0, GLA_LOWRANK, 0)) // 8, lr0 // 8)
        return octet * 8

    return pl.pallas_call(
        functools.partial(_wprep_kernel, n_main_blocks),
        out_shape=jax.ShapeDtypeStruct((n_main + WT_BLOCK, d), BF16),
        grid=(n_main_blocks + 1,),
        in_specs=[pl.BlockSpec((pl.Element(1), pl.Element(WT_BLOCK), pl.Element(d)), lambda j: (0, src_row(j), 0))],
        out_specs=pl.BlockSpec((WT_BLOCK, d), lambda j: (j, 0)),
        compiler_params=_cparams(("parallel",)),
        name="prep_in_weight",
    )(w_t)


GLA_HELD = ("lr", "q_gla", "k_gla")


def _gla_operands(hold, wg_ref, bg_ref, qin_ref, kin_ref, qst_ref, kst_ref, dec_ref):
    c = GLA_CHUNK
    tm = hold["q_gla"].shape[0]
    tril = (lax.broadcasted_iota(jnp.int32, (c, c), 0) >= lax.broadcasted_iota(jnp.int32, (c, c), 1)).astype(BF16)
    mid = c // 2 - 1
    lr, wg = hold["lr"][:, 0:GLA_LOWRANK], wg_ref[...]
    lr_hi, wg_hi = lr.astype(BF16), wg.astype(BF16)
    lr_lo, wg_lo = (lr - lr_hi.astype(F32)).astype(BF16), (wg - wg_hi.astype(F32)).astype(BF16)
    gate_in = (jnp.dot(lr_hi, wg_hi, preferred_element_type=F32) + jnp.dot(lr_lo, wg_hi, preferred_element_type=F32)
               + jnp.dot(lr_hi, wg_lo, preferred_element_type=F32)) + bg_ref[...]
    g_all = (jnp.minimum(gate_in, 0.0) - jnp.log(1.0 + jnp.exp(-jnp.abs(gate_in)))) * (1.0 / GLA_TAU)
    g_hi, g_mid, g_lo = _split3(g_all)
    for ci in range(tm // c):
        rows = slice(ci * c, (ci + 1) * c)
        bc = (jnp.dot(tril, g_hi[rows], preferred_element_type=F32)
              + jnp.dot(tril, g_mid[rows], preferred_element_type=F32)
              + jnp.dot(tril, g_lo[rows], preferred_element_type=F32))
        b_mid = bc[mid:mid + 1, :]
        b_last = bc[c - 1:c, :]
        qf = hold["q_gla"][rows, :]
        kf = hold["k_gla"][rows, :]
        qin_ref[0, rows, :] = (qf * jnp.exp(bc - b_mid)).astype(BF16)
        kin_ref[0, rows, :] = (kf * jnp.exp(b_mid - bc)).astype(BF16)
        qst_ref[0, rows, :] = (qf * jnp.exp(bc)).astype(BF16)
        kst_ref[0, rows, :] = (kf * jnp.exp(b_last - bc)).astype(BF16)
        dec_ref[0, ci:ci + 1, :] = jnp.exp(b_last)


def _proj_kernel(pieces, head_k, x_ref, sc_ref, sh_ref, w_ref, wg_ref, bg_ref, *refs):
    n_out = len(pieces) - len(GLA_HELD)
    out_refs = dict(zip([p[0] for p in pieces if p[0] not in GLA_HELD], refs[:n_out]))
    gla_out_refs = refs[n_out:n_out + 5]
    stage_ref = refs[n_out + 5]
    hold = dict(zip(GLA_HELD, refs[n_out + 6:]))
    tm = x_ref.shape[1]
    u = (x_ref[0] * (1.0 + sc_ref[0]) + sh_ref[0]).astype(BF16)
    offsets, off = {}, 0
    for name, width, _ in pieces:
        offsets[name] = off
        off += width
    by_name = {p[0]: p for p in pieces}
    held = [(by_name[n], 0) for n in GLA_HELD]
    rest = [(p, c0) for p in pieces if p[0] not in GLA_HELD for c0 in range(0, p[1], min(p[1], 512))]
    for n, (piece, c0) in enumerate(held + rest):
        if n == len(held):
            _gla_operands(hold, wg_ref, bg_ref, *gla_out_refs)
        name, width, post = piece
        o_ref = hold[name] if name in GLA_HELD else out_refs[name]
        off = offsets[name]
        chunk = min(width, 512)
        if True:
            acc = lax.dot_general(u, w_ref[off + c0:off + c0 + chunk, :], NT_DIMS, preferred_element_type=F32)
            if post == "silu":
                acc = _silu(acc)
            elif post == "sigmoid":
                acc = _sigmoid(acc)
            elif post == "scale_q_gla":
                acc = acc * (head_k ** -0.5)
            if name in GLA_HELD:
                o_ref[...] = acc
            elif isinstance(post, tuple):
                scale, dilation = post
                if scale is not None:
                    acc = acc * (DIL_HEAD_DIM ** -0.5 * LOG2E)
                if dilation == 1:
                    o_ref[0, 0] = acc.astype(o_ref.dtype)
                else:
                    for t in range(width // LANES):
                        stage_ref[t] = acc[:, t * LANES:(t + 1) * LANES]
                    for r in range(dilation):
                        for t in range(width // LANES):
                            o_ref[0, r, :, t * LANES:(t + 1) * LANES] = stage_ref[
                                t, pl.ds(r, tm // dilation, stride=dilation), :].astype(o_ref.dtype)
            else:
                o_ref[0, :, c0:c0 + chunk] = acc.astype(o_ref.dtype)


def _in_projection(x, sc1, sh1, w_perm, w_gate, b_gate):
    bsz, s, d = x.shape
    pieces = _proj_pieces(d)
    assert sum(p[1] for p in pieces) <= w_perm.shape[0]
    tm = min(ROW_TILE, s)
    assert s % tm == 0 and tm % (8 * GLA_CHUNK) == 0
    dk = d // 2
    head_k = dk // GLA_HEADS
    out_shape, out_specs = [], []
    for name, width, post in pieces:
        if name in GLA_HELD:
            continue
        if isinstance(post, tuple):
            dil = post[1]
            assert tm % (dil * 16) == 0
            out_shape.append(jax.ShapeDtypeStruct((bsz, dil, s // dil, width), BF16))
            out_specs.append(pl.BlockSpec((1, dil, tm // dil, width), lambda b, i: (b, 0, i, 0)))
        else:
            out_shape.append(jax.ShapeDtypeStruct((bsz, s, width), BF16))
            out_specs.append(pl.BlockSpec((1, tm, width), lambda b, i: (b, i, 0)))
    row = lambda w: pl.BlockSpec((1, tm, w), lambda b, i: (b, i, 0))
    gla_names = ("q_in", "k_in", "q_st", "k_st", "dec")
    out_shape += [jax.ShapeDtypeStruct((bsz, s, dk), BF16)] * 4 + [jax.ShapeDtypeStruct((bsz, s // GLA_CHUNK, dk), F32)]
    out_specs += [row(dk)] * 4 + [pl.BlockSpec((1, tm // GLA_CHUNK, dk), lambda b, i: (b, i, 0))]
    bg = b_gate.reshape(1, dk)
    full = lambda a: pl.BlockSpec(a.shape, lambda b, i: (0,) * a.ndim)
    outs = pl.pallas_call(
        functools.partial(_proj_kernel, pieces, head_k),
        out_shape=out_shape,
        grid=(bsz, s // tm),
        in_specs=[row(d),
                  pl.BlockSpec((1, 1, d), lambda b, i: (b, 0, 0)),
                  pl.BlockSpec((1, 1, d), lambda b, i: (b, 0, 0)),
                  pl.BlockSpec(w_perm.shape, lambda b, i: (0, 0), pipeline_mode=pl.Buffered(1)),
                  full(w_gate), full(bg)],
        out_specs=out_specs,
        scratch_shapes=[pltpu.VMEM((DIL_GROUP_WIDTH // LANES, tm, LANES), F32),
                        pltpu.VMEM((tm, LANES), F32), pltpu.VMEM((tm, dk), F32), pltpu.VMEM((tm, dk), F32)],
        compiler_params=_cparams(("parallel", "arbitrary")),
        name="in_projection",
    )(x, sc1, sh1, w_perm, w_gate, bg)
    return dict(zip([p[0] for p in pieces if p[0] not in GLA_HELD] + list(gla_names), outs))


def _gla_kernel(n_chunks, head_k, head_v, qin_ref, kin_ref, qst_ref, kst_ref, dec_ref, v_ref, r_ref, ng_ref, o_ref,
                state_ref):
    @pl.when(pl.program_id(1) == 0)
    def _():
        state_ref[...] = jnp.zeros_like(state_ref)

    c = GLA_CHUNK
    causal = lax.broadcasted_iota(jnp.int32, (c, c), 0) >= lax.broadcasted_iota(jnp.int32, (c, c), 1)
    for ci in range(n_chunks):
        rows = slice(ci * c, (ci + 1) * c)
        for h in range(GLA_HEADS):
            ks = slice(h * head_k, (h + 1) * head_k)
            vs = slice(h * head_v, (h + 1) * head_v)
            vh = v_ref[0, rows, vs]
            att = lax.dot_general(qin_ref[0, rows, ks], kin_ref[0, rows, ks], NT_DIMS, preferred_element_type=F32)
            att = jnp.where(causal, att, 0.0).astype(BF16)
            st = state_ref[h]
            o = jnp.dot(att, vh, preferred_element_type=F32)
            o = o + lax.dot_general(qst_ref[0, rows, ks], st.astype(BF16), NT_DIMS, preferred_element_type=F32)
            kv_t = lax.dot_general(vh, kst_ref[0, rows, ks], TN_DIMS, preferred_element_type=F32)
            state_ref[h] = st * dec_ref[0, ci:ci + 1, ks] + kv_t
            ms = jnp.mean(o * o, axis=-1, keepdims=True)
            o = o * lax.rsqrt(ms + LN_EPS) * ng_ref[:, vs] * r_ref[0, rows, vs].astype(F32)
            o_ref[0, rows, vs] = o.astype(o_ref.dtype)


def _gla(q_in, k_in, q_st, k_st, dec, v, r_silu, norm_g):
    bsz, s, dk = q_in.shape
    dv = v.shape[-1]
    head_k, head_v = dk // GLA_HEADS, dv // GLA_HEADS
    n_chunks = min(GLA_STEP_CHUNKS, s // GLA_CHUNK)
    ct = GLA_CHUNK * n_chunks
    assert s % ct == 0
    row_spec = lambda w: pl.BlockSpec((1, ct, w), lambda b, i: (b, i, 0))
    full = lambda a: pl.BlockSpec(a.shape, lambda b, i: (0,) * a.ndim)
    ng = norm_g.reshape(1, dv)
    return pl.pallas_call(
        functools.partial(_gla_kernel, n_chunks, head_k, head_v),
        out_shape=jax.ShapeDtypeStruct((bsz, s, dv), BF16),
        grid=(bsz, s // ct),
        in_specs=[row_spec(dk)] * 4 + [pl.BlockSpec((1, n_chunks, dk), lambda b, i: (b, i, 0)),
                                       row_spec(dv), row_spec(dv), full(ng)],
        out_specs=row_spec(dv),
        scratch_shapes=[pltpu.VMEM((GLA_HEADS, head_v, head_k), F32)],
        compiler_params=_cparams(("parallel", "arbitrary")),
        name="gla",
    )(q_in, k_in, q_st, k_st, dec, v, r_silu, ng)


def _t5_bucket_np(dist):
    exact = REL_BUCKETS // 2
    d = np.maximum(dist, 1).astype(np.float32)
    large = exact + (np.log(d / np.float32(exact)) / np.float32(math.log(REL_MAX_DIST / exact))
                     * np.float32(REL_BUCKETS - exact)).astype(np.int32)
    large = np.minimum(large, REL_BUCKETS - 1)
    return np.where(dist < exact, dist, large).astype(np.int32)


def _band_tables(window, dilation):
    qi = np.arange(DIL_BLOCK)[:, None]
    kj = np.arange(2 * DIL_BLOCK)[None, :]
    m = qi + DIL_BLOCK - kj
    n_steps = window // dilation
    band = (m >= 0) & (m <= n_steps)
    bucket = _t5_bucket_np(np.clip(m, 0, n_steps) * dilation)
    return np.where(band, bucket, -1).astype(np.int32)


def _attn_kernel(nq, table_ref, bucket_ref, q_ref, kp_ref, kc_ref, vp_ref, vc_ref, o_ref, lse_ref,
                 bias_ref, p_ref):
    i = pl.program_id(1)
    blk = DIL_BLOCK
    hpg = DIL_HEADS_PER_GROUP
    n_pairs = hpg // 2

    @pl.when((pl.program_id(0) == 0) & (i == 0))
    def _():
        bucket = bucket_ref[...]
        for h in range(hpg):
            acc = jnp.full(bucket.shape, NEG, F32)
            for bkt in range(REL_BUCKETS):
                acc = jnp.where(bucket == bkt, table_ref[bkt, h] * LOG2E, acc)
            bias_ref[h * blk:(h + 1) * blk, :] = acc

    lane = lax.broadcasted_iota(jnp.int32, (blk, LANES), 1)
    low = lane < DIL_HEAD_DIM
    ones_rhs = jnp.ones((2 * blk, LANES), BF16)

    def windows(ref_p, ref_c, sq, qb, cols):
        if qb == 0:
            return jnp.concatenate([ref_p[sq, :, cols], ref_c[sq, 0:blk, cols]], axis=0)
        return ref_c[sq, (qb - 1) * blk:(qb + 1) * blk, cols]

    key_lane = lax.broadcasted_iota(jnp.int32, (1, 2 * blk), 1)
    no_prev = jnp.where((key_lane < blk) & (i == 0), NEG, 0.0)
    items = [(sq, qb, hp) for sq in range(q_ref.shape[0]) for qb in range(nq) for hp in range(n_pairs)]

    mxs = []
    for n, (sq, qb, hp) in enumerate(items):
        rows = slice(qb * blk, (qb + 1) * blk)
        cols = slice(hp * LANES, (hp + 1) * LANES)
        qp = q_ref[sq, rows, cols]
        zero = jnp.zeros_like(qp)
        qq = jnp.concatenate([jnp.where(low, qp, zero), jnp.where(low, zero, qp)], axis=0)
        keys = windows(kp_ref, kc_ref, sq, qb, cols)
        s = lax.dot_general(qq, keys, NT_DIMS, preferred_element_type=F32) + bias_ref[2 * hp * blk:(2 * hp + 2) * blk, :]
        if qb == 0:
            s = s + no_prev
        mx = jnp.max(s, axis=-1, keepdims=True)
        p_ref[n * 2 * blk:(n + 1) * 2 * blk, :] = jnp.exp2(s - mx).astype(BF16)
        mxs.append(mx)

    for n, (sq, qb, hp) in enumerate(items):
        rows = slice(qb * blk, (qb + 1) * blk)
        cols = slice(hp * LANES, (hp + 1) * LANES)
        vals = windows(vp_ref, vc_ref, sq, qb, cols)
        rhs = jnp.concatenate([vals, ones_rhs], axis=1)
        res = jnp.dot(p_ref[n * 2 * blk:(n + 1) * 2 * blk, :], rhs, preferred_element_type=F32)
        num = jnp.where(low, res[0:blk, 0:LANES], res[blk:2 * blk, 0:LANES])
        den = jnp.where(low, res[0:blk, LANES:], res[blk:2 * blk, LANES:])
        mx = jnp.where(low, mxs[n][0:blk], mxs[n][blk:2 * blk])
        o_ref[sq, rows, cols] = (num / den).astype(o_ref.dtype)
        lse_ref[sq, rows, cols] = (mx + jnp.log2(den)) * LN2


def _dilated_group_attention(q, k, v, table, window, dilation):
    bb, l, w = q.shape
    nq = min(ATT_STEP_BLOCKS, l // DIL_BLOCK)
    nsq = ATT_STEP_BLOCKS // nq
    assert l % (nq * DIL_BLOCK) == 0 and bb % nsq == 0
    steps = l // (nq * DIL_BLOCK)
    bucket = jnp.asarray(_band_tables(window, dilation))
    cur = pl.BlockSpec((nsq, nq * DIL_BLOCK, w), lambda b, i: (b, i, 0))
    prev = pl.BlockSpec((nsq, DIL_BLOCK, w), lambda b, i: (b, jnp.maximum(nq * i - 1, 0), 0))
    rows_all = nsq * nq * DIL_HEADS_PER_GROUP * DIL_BLOCK
    return pl.pallas_call(
        functools.partial(_attn_kernel, nq),
        out_shape=[jax.ShapeDtypeStruct((bb, l, w), BF16), jax.ShapeDtypeStruct((bb, l, w), F32)],
        grid=(bb // nsq, steps),
        in_specs=[pl.BlockSpec(memory_space=pltpu.SMEM),
                  pl.BlockSpec(bucket.shape, lambda b, i: (0, 0)),
                  cur, prev, cur, prev, cur],
        out_specs=[cur, cur],
        scratch_shapes=[pltpu.VMEM((DIL_HEADS_PER_GROUP * DIL_BLOCK, 2 * DIL_BLOCK), F32),
                        pltpu.VMEM((rows_all, 2 * DIL_BLOCK), BF16)],
        compiler_params=_cparams(("arbitrary", "arbitrary")),
        name=f"dilated_attn_d{dilation}",
    )(table, bucket, q, k, k, v, v)


def _merge_kernel(alpha, dilations, ygla_ref, o0_ref, o1_ref, o2_ref, l0_ref, l1_ref, l2_ref, gg_ref, ga_ref, x_ref,
                  g1_ref, sc2_ref, sh2_ref, ln_g_ref, ln_b_ref, wpg_ref, wpa_ref, wout_ref, wr_ref, br_ref, ltri_ref,
                  x1_ref, u2_ref, route_ref, ew_ref, cnt_ref, stage_ref, carry_ref):
    tm = x_ref.shape[1]

    @pl.when((pl.program_id(0) == 0) & (pl.program_id(1) == 0))
    def _():
        carry_ref[...] = jnp.zeros_like(carry_ref)

    n_lt = DIL_GROUP_WIDTH // LANES
    group_refs = tuple(zip((l0_ref, l1_ref, l2_ref), (o0_ref, o1_ref, o2_ref), dilations))
    for gi, (l_ref, o_ref, dil) in enumerate(group_refs):
        if dil > 1:
            for slot, ref in ((2 * gi, l_ref), (2 * gi + 1, o_ref)):
                for r in range(dil):
                    for t in range(n_lt):
                        stage_ref[slot, t, pl.ds(r, tm // dil, stride=dil), :] = ref[
                            0, r, :, t * LANES:(t + 1) * LANES].astype(F32)

    wr = wr_ref[...]
    w_hi = wr.astype(BF16)
    w_lo = (wr - w_hi.astype(F32)).astype(BF16)
    sub = ltri_ref.shape[0]
    for rows in (slice(r0, r0 + sub) for r0 in range(0, tm, sub)):
        def natural(ref, dil, slot):
            if dil == 1:
                return ref[0, 0, rows, :].astype(F32)
            return jnp.concatenate([stage_ref[slot, t, rows, :] for t in range(n_lt)], axis=1)

        lses = [natural(l_ref, dil, 2 * gi) for gi, (l_ref, _, dil) in enumerate(group_refs)]
        outs = [natural(o_ref, dil, 2 * gi + 1) for gi, (_, o_ref, dil) in enumerate(group_refs)]
        lm = jnp.maximum(jnp.maximum(lses[0], lses[1]), lses[2])
        es = [jnp.exp(l - lm) for l in lses]
        y_att = (es[0] * outs[0] + es[1] * outs[1] + es[2] * outs[2]) / (es[0] + es[1] + es[2])

        p_gla = jnp.dot(ygla_ref[0, rows, :], wpg_ref[...], preferred_element_type=F32)
        p_att = jnp.dot(y_att.astype(BF16), wpa_ref[...], preferred_element_type=F32)
        merged = gg_ref[0, rows, :].astype(F32) * p_gla + ga_ref[0, rows, :].astype(F32) * p_att
        y = jnp.dot(merged.astype(BF16), wout_ref[...], preferred_element_type=F32)
        x1 = _layer_norm(alpha * x_ref[0, rows, :] + g1_ref[0] * y, ln_g_ref[...], ln_b_ref[...])
        x1_ref[0, rows, :] = x1
        u2 = x1 * (1.0 + sc2_ref[0]) + sh2_ref[0]
        u2_ref[0, rows, :] = _pack_bf16_pairs(u2)

        u_hi = u2.astype(BF16)
        u_lo = (u2 - u_hi.astype(F32)).astype(BF16)
        logits = (jnp.dot(u_hi, w_hi, preferred_element_type=F32) + jnp.dot(u_lo, w_hi, preferred_element_type=F32)
                  + jnp.dot(u_hi, w_lo, preferred_element_type=F32)) + br_ref[...]
        lane = lax.broadcasted_iota(jnp.int32, logits.shape, 1)
        big = jnp.int32(LANES)
        lg = jnp.where(lane < MOE_GROUPS, logits, NEG)
        gmax = jnp.max(lg, axis=-1, keepdims=True)
        gidx = jnp.min(jnp.where(lg == gmax, lane, big), axis=-1, keepdims=True)
        gval = 1.0 / jnp.sum(jnp.exp(lg - gmax), axis=-1, keepdims=True)
        in_group = (lane >= MOE_GROUPS + gidx * MOE_EXPERTS) & (lane < MOE_GROUPS + (gidx + 1) * MOE_EXPERTS)
        le = jnp.where(in_group, logits, NEG)
        m1 = jnp.max(le, axis=-1, keepdims=True)
        i1 = jnp.min(jnp.where(le == m1, lane, big), axis=-1, keepdims=True)
        le2 = jnp.where(lane == i1, NEG, le)
        m2 = jnp.max(le2, axis=-1, keepdims=True)
        i2 = jnp.min(jnp.where(le2 == m2, lane, big), axis=-1, keepdims=True)
        t = jnp.exp(m2 - m1)
        w1 = 1.0 / (1.0 + t)
        w2 = t * w1

        hit1, hit2 = lane == i1, lane == i2
        onehot = jnp.where(hit1 | hit2, 1.0, 0.0)
        earlier = jnp.dot(ltri_ref[...], onehot.astype(BF16), preferred_element_type=F32) + carry_ref[...]
        rank1 = jnp.sum(jnp.where(hit1, earlier, 0.0), axis=-1, keepdims=True).astype(jnp.int32)
        rank2 = jnp.sum(jnp.where(hit2, earlier, 0.0), axis=-1, keepdims=True).astype(jnp.int32)
        carry_ref[...] = carry_ref[...] + jnp.sum(onehot, axis=0, keepdims=True)
        route = jnp.where(lane == 0, i1 - MOE_GROUPS, jnp.where(lane == 1, i2 - MOE_GROUPS,
                          jnp.where(lane == 2, rank1, jnp.where(lane == 3, rank2, 0))))
        route_ref[0, :, rows] = jnp.transpose(route)[0:ROUTE_ROWS, :]
        ew_ref[0, rows, :] = jnp.where(lane == 0, gval * w1, jnp.where(lane == 1, gval * w2, 0.0))
    cnt_ref[...] = carry_ref[...].astype(jnp.int32)


def _merge(alpha, y_gla, o_groups, lse_groups, g_gla, g_att, x, g1, sc2, sh2, ln_g, ln_b, wpg, wpa, wout, wr, br):
    bsz, s, d = x.shape
    tm = min(ROW_TILE, s)
    assert s % tm == 0
    dilations = tuple(dil for _, dil in DIL_PATTERNS)
    row = lambda w: pl.BlockSpec((1, tm, w), lambda b, i: (b, i, 0))
    sub = lambda dil: pl.BlockSpec((1, dil, tm // dil, DIL_GROUP_WIDTH), lambda b, i: (b, 0, i, 0))
    per_b = pl.BlockSpec((1, 1, d), lambda b, i: (b, 0, 0))
    full = lambda a: pl.BlockSpec(a.shape, lambda b, i: (0,) * a.ndim)
    ln_g2, ln_b2 = ln_g.reshape(1, d), ln_b.reshape(1, d)
    sub_rows = min(MERGE_SUB_ROWS, tm)
    assert tm % sub_rows == 0
    ltri = jnp.asarray(np.tril(np.ones((sub_rows, sub_rows), np.float32), -1), BF16)
    return pl.pallas_call(
        functools.partial(_merge_kernel, alpha, dilations),
        out_shape=[jax.ShapeDtypeStruct((bsz, s, d), F32), jax.ShapeDtypeStruct((bsz, s, d // 2), jnp.int32),
                   jax.ShapeDtypeStruct((bsz, ROUTE_ROWS, s), jnp.int32), jax.ShapeDtypeStruct((bsz, s, LANES), F32),
                   jax.ShapeDtypeStruct((1, LANES), jnp.int32)],
        grid=(bsz, s // tm),
        in_specs=[row(y_gla.shape[-1])] + [sub(dil) for dil in dilations] * 2
                 + [row(d), row(d), row(d), per_b, per_b, per_b, full(ln_g2), full(ln_b2),
                    full(wpg), full(wpa), full(wout), full(wr), full(br), full(ltri)],
        out_specs=[row(d), row(d // 2), pl.BlockSpec((1, ROUTE_ROWS, tm), lambda b, i: (b, 0, i)), row(LANES),
                   pl.BlockSpec((1, LANES), lambda b, i: (0, 0))],
        scratch_shapes=[pltpu.VMEM((2 * DIL_GROUPS, DIL_GROUP_WIDTH // LANES, tm, LANES), F32),
                        pltpu.VMEM((1, LANES), F32)],
        compiler_params=_cparams(("arbitrary", "arbitrary")),
        name="merge_ln1_router",
    )(y_gla, *o_groups, *lse_groups, g_gla, g_att, x, g1, sc2, sh2, ln_g2, ln_b2, wpg, wpa, wout, wr, br, ltri)


def _expert_kernel(run_ref, valid_ref, rexp_ref, used_ref, x_ref, wg_hbm, wu_hbm, wd_hbm, o_ref,
                   wg_f, wu_f, wd_f, wg_s, wu_s, wd_s, sem):
    t = pl.program_id(0)
    n_tiles_used, n_runs = used_ref[0], used_ref[1]
    run = run_ref[t]
    active = t < n_tiles_used
    first_of_run = (t == 0) | (run_ref[jnp.maximum(t - 1, 0)] != run)

    def weight_copies(r):
        e, slot = rexp_ref[r], r % 2
        return [pltpu.make_async_copy(hbm.at[e], buf.at[slot], sem.at[slot, j])
                for j, (hbm, buf) in enumerate(((wg_hbm, wg_f), (wu_hbm, wu_f), (wd_hbm, wd_f)))]

    @pl.when(active & (t == 0))
    def _():
        for cp in weight_copies(0):
            cp.start()

    @pl.when(active & first_of_run)
    def _():
        @pl.when(run + 1 < n_runs)
        def _():
            for cp in weight_copies(run + 1):
                cp.start()

        for cp in weight_copies(run):
            cp.wait()
        slot = run % 2
        wg_s[...] = wg_f[slot].astype(BF16)
        wu_s[...] = wu_f[slot].astype(BF16)
        wd_s[...] = wd_f[slot].astype(BF16)

    n_valid = jnp.where(active, valid_ref[t], 0)
    for r0 in range(0, x_ref.shape[0], EXPERT_BLOCK):
        rows = slice(r0, r0 + EXPERT_BLOCK)

        @pl.when(n_valid > r0)
        def _():
            xt = _unpack_bf16_pairs(x_ref[rows, :]).astype(BF16)
            hg = jnp.dot(xt, wg_s[...], preferred_element_type=F32)
            hu = jnp.dot(xt, wu_s[...], preferred_element_type=F32)
            h = (_silu(hg) * hu).astype(BF16)
            o_ref[rows, :] = _pack_bf16_pairs(jnp.dot(h, wd_s[...], preferred_element_type=F32))

        @pl.when(n_valid <= r0)
        def _():
            o_ref[rows, :] = jnp.zeros((EXPERT_BLOCK, o_ref.shape[1]), o_ref.dtype)


def _expert_ffn(tile_run, tile_valid, run_expert, used, xg, w_gate, w_up, w_down):
    p = xg.shape[0]
    ne, d, ff = w_gate.shape
    tm = EXPERT_TILE
    n_tiles = p // tm
    hbm = pl.BlockSpec(memory_space=pl.ANY)
    grid_spec = pltpu.PrefetchScalarGridSpec(
        num_scalar_prefetch=4,
        grid=(n_tiles,),
        in_specs=[pl.BlockSpec((tm, d // 2), lambda t, *_: (t, 0)), hbm, hbm, hbm],
        out_specs=pl.BlockSpec((tm, d // 2), lambda t, *_: (t, 0)),
        scratch_shapes=[pltpu.VMEM((2, d, ff), F32), pltpu.VMEM((2, d, ff), F32), pltpu.VMEM((2, ff, d), F32),
                        pltpu.VMEM((d, ff), BF16), pltpu.VMEM((d, ff), BF16), pltpu.VMEM((ff, d), BF16),
                        pltpu.SemaphoreType.DMA((2, 3))],
    )
    return pl.pallas_call(
        _expert_kernel,
        out_shape=jax.ShapeDtypeStruct((p, d // 2), jnp.int32),
        grid_spec=grid_spec,
        compiler_params=_cparams(("arbitrary",)),
        name="expert_ffn",
    )(tile_run, tile_valid, run_expert, used, xg, w_gate, w_up, w_down)


def _final_kernel(alpha, x1_ref, ya_ref, yb_ref, ew_ref, g2_ref, ln_g_ref, ln_b_ref, o_ref):
    ew = ew_ref[0]
    y = ew[:, 0:1] * _unpack_bf16_pairs(ya_ref[0]) + ew[:, 1:2] * _unpack_bf16_pairs(yb_ref[0])
    o_ref[0] = _layer_norm(alpha * x1_ref[0] + g2_ref[0] * y, ln_g_ref[...], ln_b_ref[...])


def _final(alpha, x1, ya, yb, ew, g2, ln_g, ln_b):
    bsz, s, d = x1.shape
    tm = min(ROW_TILE, s)
    row = lambda w: pl.BlockSpec((1, tm, w), lambda b, i: (b, i, 0))
    full = lambda a: pl.BlockSpec(a.shape, lambda b, i: (0,) * a.ndim)
    ln_g2, ln_b2 = ln_g.reshape(1, d), ln_b.reshape(1, d)
    return pl.pallas_call(
        functools.partial(_final_kernel, alpha),
        out_shape=jax.ShapeDtypeStruct((bsz, s, d), F32),
        grid=(bsz, s // tm),
        in_specs=[row(d), row(d // 2), row(d // 2), row(LANES), pl.BlockSpec((1, 1, d), lambda b, i: (b, 0, 0)),
                  full(ln_g2), full(ln_b2)],
        out_specs=row(d),
        compiler_params=_cparams(("parallel", "arbitrary")),
        name="combine_ln2",
    )(x1, ya, yb, ew, g2, ln_g2, ln_b2)


SC_CORES = 2
SC_SUBCORES = 16
SC_CHUNK = 64


def _sc_mesh():
    return plsc.VectorSubcoreMesh(core_axis_name="c", subcore_axis_name="s")


def _sc_scatter_rows(rows, dest0, dest1, n_rows):
    n, w = rows.shape
    n_workers = SC_CORES * SC_SUBCORES
    assert n % (n_workers * SC_CHUNK) == 0
    n_chunks = n // (n_workers * SC_CHUNK)
    d0 = dest0.reshape(n // SC_CHUNK, 1, SC_CHUNK)
    d1 = dest1.reshape(n // SC_CHUNK, 1, SC_CHUNK)

    @functools.partial(
        pl.kernel, mesh=_sc_mesh(), out_type=jax.ShapeDtypeStruct((n_rows, w), rows.dtype),
        scratch_types=[pltpu.VMEM((n_chunks, 1, SC_CHUNK), jnp.int32), pltpu.VMEM((n_chunks, 1, SC_CHUNK), jnp.int32),
                       pltpu.VMEM((2, SC_CHUNK, w), rows.dtype),
                       pltpu.SemaphoreType.DMA((2,)), pltpu.SemaphoreType.DMA((2, 2))])
    def scatter_kernel(rows_hbm, d0_hbm, d1_hbm, out_hbm, i0_v, i1_v, rows_v, read_sem, scat_sem):
        wid = lax.axis_index("s") * SC_CORES + lax.axis_index("c")
        first = wid * n_chunks
        pltpu.sync_copy(d0_hbm.at[pl.ds(first, n_chunks)], i0_v)
        pltpu.sync_copy(d1_hbm.at[pl.ds(first, n_chunks)], i1_v)

        def read(j):
            return pltpu.make_async_copy(rows_hbm.at[pl.ds((first + j) * SC_CHUNK, SC_CHUNK)], rows_v.at[j % 2],
                                         read_sem.at[j % 2])

        def scatters(j):
            return [pltpu.make_async_copy(rows_v.at[j % 2], out_hbm.at[idx.at[j].at[0]], scat_sem.at[j % 2, k])
                    for k, idx in enumerate((i0_v, i1_v))]

        read(0).start()
        for j in range(n_chunks):
            read(j).wait()
            if j + 1 < n_chunks:
                if j >= 1:
                    for cp in scatters(j - 1):
                        cp.wait()
                read(j + 1).start()
            for cp in scatters(j):
                cp.start()
        for j in range(max(n_chunks - 2, 0), n_chunks):
            for cp in scatters(j):
                cp.wait()

    return scatter_kernel(rows, d0, d1)


def _sc_gather_rows(table, dest0, dest1):
    n = dest0.shape[0]
    w = table.shape[1]
    n_workers = SC_CORES * SC_SUBCORES
    assert n % (n_workers * SC_CHUNK) == 0
    n_chunks = n // (n_workers * SC_CHUNK)
    d0 = dest0.reshape(n // SC_CHUNK, 1, SC_CHUNK)
    d1 = dest1.reshape(n // SC_CHUNK, 1, SC_CHUNK)
    out = jax.ShapeDtypeStruct((n, w), table.dtype)

    @functools.partial(
        pl.kernel, mesh=_sc_mesh(), out_type=(out, out),
        scratch_types=[pltpu.VMEM((n_chunks, 1, SC_CHUNK), jnp.int32), pltpu.VMEM((n_chunks, 1, SC_CHUNK), jnp.int32),
                       pltpu.VMEM((2, SC_CHUNK, w), table.dtype),
                       pltpu.SemaphoreType.DMA((2,)), pltpu.SemaphoreType.DMA((2,))])
    def gather_kernel(table_hbm, d0_hbm, d1_hbm, a_hbm, b_hbm, i0_v, i1_v, rows_v, gather_sem, write_sem):
        wid = lax.axis_index("s") * SC_CORES + lax.axis_index("c")
        first = wid * n_chunks
        pltpu.sync_copy(d0_hbm.at[pl.ds(first, n_chunks)], i0_v)
        pltpu.sync_copy(d1_hbm.at[pl.ds(first, n_chunks)], i1_v)
        n_items = 2 * n_chunks

        def gather(m):
            idx = (i0_v, i1_v)[m % 2]
            return pltpu.make_async_copy(table_hbm.at[idx.at[m // 2].at[0]], rows_v.at[m % 2], gather_sem.at[m % 2])

        def write(m):
            o_hbm = (a_hbm, b_hbm)[m % 2]
            return pltpu.make_async_copy(rows_v.at[m % 2], o_hbm.at[pl.ds((first + m // 2) * SC_CHUNK, SC_CHUNK)],
                                         write_sem.at[m % 2])

        gather(0).start()
        for m in range(n_items):
            gather(m).wait()
            if m + 1 < n_items:
                if m >= 1:
                    write(m - 1).wait()
                gather(m + 1).start()
            write(m).start()
        for m in range(max(n_items - 2, 0), n_items):
            write(m).wait()

    return gather_kernel(table, d0, d1)


def _dispatch_plan(route, counts):
    tm = EXPERT_TILE
    e0, e1, r0, r1 = (route[:, j, :].reshape(-1) for j in range(4))
    experts = jnp.arange(MOE_TOTAL, dtype=jnp.int32)
    tiles_per = (counts + tm - 1) // tm
    tile_end = jnp.cumsum(tiles_per)
    pad_start = ((tile_end - tiles_per) * tm).astype(jnp.int32)

    def lookup(e):
        return jnp.sum(jnp.where(e[None, :] == experts[:, None], pad_start[:, None], 0), axis=0)

    dest0, dest1 = lookup(e0) + r0, lookup(e1) + r1
    n_tiles = (2 * e0.size + MOE_TOTAL * tm) // tm
    tile_expert = jnp.minimum(jnp.sum(tile_end[None, :] <= jnp.arange(n_tiles)[:, None], axis=1), MOE_TOTAL - 1)
    nonempty = counts > 0
    run_of_expert = jnp.cumsum(nonempty.astype(jnp.int32)) - 1
    run_expert = jnp.sum(jnp.where(nonempty[None, :] & (run_of_expert[None, :] == experts[:, None]),
                                   experts[None, :], 0), axis=1).astype(jnp.int32)
    of_tile = tile_expert[:, None] == experts[None, :]
    tile_run = jnp.sum(jnp.where(of_tile, run_of_expert[None, :], 0), axis=1).astype(jnp.int32)
    rows_left = (counts + pad_start)[None, :] - jnp.arange(n_tiles)[:, None] * tm
    tile_valid = jnp.clip(jnp.sum(jnp.where(of_tile, rows_left, 0), axis=1), 0, tm).astype(jnp.int32)
    used = jnp.stack([tile_end[-1], jnp.sum(nonempty)]).astype(jnp.int32)
    return dest0, dest1, tile_run, tile_valid, run_expert, used, n_tiles * tm


def _layer(x, c, rel_bias, w_ada, b_ada, w_in, w_gla_gate, b_gla_gate, gla_norm, w_proj_gla, w_proj_attn, w_out,
           ln1_g, ln1_b, w_rg, b_rg, w_re, b_re, w_eg, w_eu, w_ed, ln2_g, ln2_b):
    bsz, s, d = x.shape
    alpha = (2.0 * DEPTH) ** 0.25
    mods = _ada_mods(c, w_ada, b_ada)
    sh1, sc1, g1, sh2, sc2, g2 = [m.reshape(bsz, 1, d) for m in jnp.split(mods, N_MOD, axis=-1)]

    lr0 = d // 2 * 2 + 2 * d
    z = _in_projection(x, sc1, sh1, _prep_in_weight(w_in, lr0), w_gla_gate, b_gla_gate)

    y_gla = _gla(z["q_in"], z["k_in"], z["q_st"], z["k_st"], z["dec"], z["v_gla"], z["r_gla"], gla_norm)

    o_groups, lse_groups = [], []
    for g, (window, dilation) in enumerate(DIL_PATTERNS):
        l = s // dilation
        qg, kg, vg = (z[f"{n}{g}"].reshape(bsz * dilation, l, DIL_GROUP_WIDTH) for n in ("q_att", "k_att", "v_att"))
        table = rel_bias[:, g * DIL_HEADS_PER_GROUP:(g + 1) * DIL_HEADS_PER_GROUP]
        o, lse = _dilated_group_attention(qg, kg, vg, table, window, dilation)
        o_groups.append(o.reshape(bsz, dilation, l, DIL_GROUP_WIDTH))
        lse_groups.append(lse.reshape(bsz, dilation, l, DIL_GROUP_WIDTH))

    wr = jnp.concatenate([w_rg, w_re, jnp.zeros((d, LANES - MOE_GROUPS - MOE_TOTAL), F32)], axis=1)
    br = jnp.concatenate([b_rg, b_re, jnp.zeros((LANES - MOE_GROUPS - MOE_TOTAL,), F32)]).reshape(1, LANES)
    x1, u2, route, ew, cnt = _merge(alpha, y_gla, o_groups, lse_groups, z["g_gla"], z["g_att"], x, g1, sc2, sh2,
                                    ln1_g, ln1_b, w_proj_gla.astype(BF16), w_proj_attn.astype(BF16),
                                    w_out.astype(BF16), wr, br)

    n = bsz * s
    counts = cnt[0, MOE_GROUPS:MOE_GROUPS + MOE_TOTAL]
    dest0, dest1, tile_run, tile_valid, run_expert, used, n_rows = _dispatch_plan(route, counts)
    xg = _sc_scatter_rows(u2.reshape(n, d // 2), dest0, dest1, n_rows)
    ff = w_eg.shape[-1]
    yo = _expert_ffn(tile_run, tile_valid, run_expert, used, xg, w_eg.reshape(MOE_TOTAL, d, ff),
                     w_eu.reshape(MOE_TOTAL, d, ff), w_ed.reshape(MOE_TOTAL, ff, d))
    ya, yb = (y.reshape(bsz, s, d // 2) for y in _sc_gather_rows(yo, dest0, dest1))
    return _final(alpha, x1, ya, yb, ew, g2, ln2_g, ln2_b)


def kernel(x, c, rel_bias, w_ada, b_ada, w_in, w_gla_gate, b_gla_gate, gla_norm, w_proj_gla, w_proj_attn, w_out,
           ln1_g, ln1_b, w_router_group, b_router_group, w_router_expert, b_router_expert, w_exp_gate, w_exp_up,
           w_exp_down, ln2_g, ln2_b):
    assert w_ada.shape[0] == DEPTH
    return _layer(x, c, rel_bias, w_ada[0], b_ada[0], w_in[0:1], w_gla_gate[0], b_gla_gate[0], gla_norm[0],
                  w_proj_gla[0], w_proj_attn[0], w_out[0], ln1_g[0], ln1_b[0], w_router_group[0],
                  b_router_group[0], w_router_expert[0], b_router_expert[0], w_exp_gate[0], w_exp_up[0],
                  w_exp_down[0], ln2_g[0], ln2_b[0])
```

```python
import functools
import math

import numpy as np
import jax
import jax.numpy as jnp
from jax import lax
from jax.experimental import pallas as pl
from jax.experimental.pallas import tpu as pltpu
from jax.experimental.pallas import tpu_sc as plsc

F32 = jnp.float32
BF16 = jnp.bfloat16

N_MOD = 6
GLA_HEADS = 4
GLA_LOWRANK = 16
GLA_TAU = 16.0
GLA_CHUNK = 64
DIL_PATTERNS = ((128, 1), (512, 4), (2048, 16))
DIL_GROUPS = len(DIL_PATTERNS)
DIL_HEADS_PER_GROUP = 8
DIL_HEAD_DIM = 64
DIL_GROUP_WIDTH = DIL_HEADS_PER_GROUP * DIL_HEAD_DIM
DIL_BLOCK = 128
REL_BUCKETS = 32
REL_MAX_DIST = 2048
MOE_GROUPS = 4
MOE_EXPERTS = 8
MOE_TOTAL = MOE_GROUPS * MOE_EXPERTS
LN_EPS = 1e-5
DEPTH = 1

LANES = 128
VMEM_LIMIT = 56 * 1024 * 1024
LOG2E = 1.4426950408889634
LN2 = 0.6931471805599453
NEG = -1e30
ROW_TILE = 512
EXPERT_TILE = 512
EXPERT_BLOCK = 256
GLA_STEP_CHUNKS = 8
ATT_STEP_BLOCKS = 4
MERGE_SUB_ROWS = 512
ROUTE_ROWS = 8

HIGHEST = lax.Precision.HIGHEST
NT_DIMS = (((1,), (1,)), ((), ()))
TN_DIMS = (((0,), (0,)), ((), ()))


def _cparams(sem):
    return pltpu.CompilerParams(dimension_semantics=sem, vmem_limit_bytes=VMEM_LIMIT)


def _sigmoid(x):
    return 0.5 * jnp.tanh(0.5 * x) + 0.5


def _silu(x):
    return x * _sigmoid(x)


def _layer_norm(x, g, b):
    mu = jnp.mean(x, axis=-1, keepdims=True)
    xc = x - mu
    var = jnp.mean(xc * xc, axis=-1, keepdims=True)
    return xc * lax.rsqrt(var + LN_EPS) * g + b


def _pack_bf16_pairs(x):
    w = x.shape[1] // 2
    lo = lax.bitcast_convert_type(x[:, :w].astype(BF16).astype(F32), jnp.uint32) >> 16
    hi = lax.bitcast_convert_type(x[:, w:].astype(BF16).astype(F32), jnp.uint32) & jnp.uint32(0xFFFF0000)
    return lax.bitcast_convert_type(lo | hi, jnp.int32)


def _unpack_bf16_pairs(p):
    u = lax.bitcast_convert_type(p, jnp.uint32)
    lo = lax.bitcast_convert_type(u << 16, F32)
    hi = lax.bitcast_convert_type(u & jnp.uint32(0xFFFF0000), F32)
    return jnp.concatenate([lo, hi], axis=1)


def _split3(x):
    hi = x.astype(BF16)
    r1 = x - hi.astype(F32)
    mid = r1.astype(BF16)
    lo = (r1 - mid.astype(F32)).astype(BF16)
    return hi, mid, lo


def _mods_kernel(c_ref, w_ref, b_ref, o_ref):
    a = _silu(c_ref[...])
    o_ref[...] = jnp.dot(a, w_ref[...], precision=HIGHEST, preferred_element_type=F32) + b_ref[...]


def _ada_mods(c, w, b):
    bsz, d = c.shape
    n = w.shape[1]
    tn = 1536
    assert n % tn == 0
    return pl.pallas_call(
        _mods_kernel,
        out_shape=jax.ShapeDtypeStruct((bsz, n), F32),
        grid=(n // tn,),
        in_specs=[pl.BlockSpec((bsz, d), lambda j: (0, 0)),
                  pl.BlockSpec((d, tn), lambda j: (0, j)),
                  pl.BlockSpec((1, tn), lambda j: (0, j))],
        out_specs=pl.BlockSpec((bsz, tn), lambda j: (0, j)),
        compiler_params=_cparams(("arbitrary",)),
        name="ada_mods",
    )(c, w, b.reshape(1, n))


def _proj_pieces(d_model):
    dk = d_model // 2
    pieces = [("q_gla", dk, "scale_q_gla"), ("k_gla", dk, None), ("v_gla", d_model, None), ("r_gla", d_model, "silu")]
    for name, post in (("q_att", "scale_q_att"), ("k_att", None), ("v_att", None)):
        for g, (_, dilation) in enumerate(DIL_PATTERNS):
            pieces.append((f"{name}{g}", DIL_GROUP_WIDTH, (post, dilation)))
    pieces += [("g_gla", d_model, "sigmoid"), ("g_att", d_model, "sigmoid"), ("lr", LANES, "lowrank")]
    return tuple(pieces)


WT_BLOCK = 512


def _wprep_kernel(n_main_blocks, w_ref, o_ref):
    blk = w_ref[0]
    row = lax.broadcasted_iota(jnp.int32, blk.shape, 0)
    keep = (pl.program_id(0) < n_main_blocks) | (row < GLA_LOWRANK)
    o_ref[...] = jnp.where(keep, blk, 0.0).astype(BF16)


def _prep_in_weight(w_in, lr0):
    w_t = jnp.swapaxes(w_in, 1, 2)
    _, n_in, d = w_t.shape
    n_main = n_in - GLA_LOWRANK
    assert lr0 % WT_BLOCK == 0 and n_main % WT_BLOCK == 0
    n_main_blocks = n_main // WT_BLOCK

    def src_row(j):
        start = j * WT_BLOCK
        octet = jnp.where(j < n_main_blocks, (start + jnp.where(start >= lr0, GLA_LOWRANK, 0)) // 8, lr0 // 8)
        return octet * 8

    return pl.pallas_call(
        functools.partial(_wprep_kernel, n_main_blocks),
        out_shape=jax.ShapeDtypeStruct((n_main + WT_BLOCK, d), BF16),
        grid=(n_main_blocks + 1,),
        in_specs=[pl.BlockSpec((pl.Element(1), pl.Element(WT_BLOCK), pl.Element(d)), lambda j: (0, src_row(j), 0))],
        out_specs=pl.BlockSpec((WT_BLOCK, d), lambda j: (j, 0)),
        compiler_params=_cparams(("parallel",)),
        name="prep_in_weight",
    )(w_t)


GLA_HELD = ("lr", "q_gla", "k_gla")


def _gla_operands(hold, wg_ref, bg_ref, qin_ref, kin_ref, qst_ref, kst_ref, dec_ref):
    c = GLA_CHUNK
    tm = hold["q_gla"].shape[0]
    tril = (lax.broadcasted_iota(jnp.int32, (c, c), 0) >= lax.broadcasted_iota(jnp.int32, (c, c), 1)).astype(BF16)
    mid = c // 2 - 1
    lr, wg = hold["lr"][:, 0:GLA_LOWRANK], wg_ref[...]
    lr_hi, wg_hi = lr.astype(BF16), wg.astype(BF16)
    lr_lo, wg_lo = (lr - lr_hi.astype(F32)).astype(BF16), (wg - wg_hi.astype(F32)).astype(BF16)
    gate_in = (jnp.dot(lr_hi, wg_hi, preferred_element_type=F32) + jnp.dot(lr_lo, wg_hi, preferred_element_type=F32)
               + jnp.dot(lr_hi, wg_lo, preferred_element_type=F32)) + bg_ref[...]
    g_all = (jnp.minimum(gate_in, 0.0) - jnp.log(1.0 + jnp.exp(-jnp.abs(gate_in)))) * (1.0 / GLA_TAU)
    g_hi, g_mid, g_lo = _split3(g_all)
    for ci in range(tm // c):
        rows = slice(ci * c, (ci + 1) * c)
        bc = (jnp.dot(tril, g_hi[rows], preferred_element_type=F32)
              + jnp.dot(tril, g_mid[rows], preferred_element_type=F32)
              + jnp.dot(tril, g_lo[rows], preferred_element_type=F32))
        b_mid = bc[mid:mid + 1, :]
        b_last = bc[c - 1:c, :]
        qf = hold["q_gla"][rows, :]
        kf = hold["k_gla"][rows, :]
        qin_ref[0, rows, :] = (qf * jnp.exp(bc - b_mid)).astype(BF16)
        kin_ref[0, rows, :] = (kf * jnp.exp(b_mid - bc)).astype(BF16)
        qst_ref[0, rows, :] = (qf * jnp.exp(bc)).astype(BF16)
        kst_ref[0, rows, :] = (kf * jnp.exp(b_last - bc)).astype(BF16)
        dec_ref[0, ci:ci + 1, :] = jnp.exp(b_last)


def _proj_kernel(pieces, head_k, x_ref, sc_ref, sh_ref, w_ref, wg_ref, bg_ref, *refs):
    n_out = len(pieces) - len(GLA_HELD)
    out_refs = dict(zip([p[0] for p in pieces if p[0] not in GLA_HELD], refs[:n_out]))
    gla_out_refs = refs[n_out:n_out + 5]
    stage_ref = refs[n_out + 5]
    hold = dict(zip(GLA_HELD, refs[n_out + 6:]))
    tm = x_ref.shape[1]
    u = (x_ref[0] * (1.0 + sc_ref[0]) + sh_ref[0]).astype(BF16)
    offsets, off = {}, 0
    for name, width, _ in pieces:
        offsets[name] = off
        off += width
    by_name = {p[0]: p for p in pieces}
    held = [(by_name[n], 0) for n in GLA_HELD]
    rest = [(p, c0) for p in pieces if p[0] not in GLA_HELD for c0 in range(0, p[1], min(p[1], 512))]
    for n, (piece, c0) in enumerate(held + rest):
        if n == len(held):
            _gla_operands(hold, wg_ref, bg_ref, *gla_out_refs)
        name, width, post = piece
        o_ref = hold[name] if name in GLA_HELD else out_refs[name]
        off = offsets[name]
        chunk = min(width, 512)
        if True:
            acc = lax.dot_general(u, w_ref[off + c0:off + c0 + chunk, :], NT_DIMS, preferred_element_type=F32)
            if post == "silu":
                acc = _silu(acc)
            elif post == "sigmoid":
                acc = _sigmoid(acc)
            elif post == "scale_q_gla":
                acc = acc * (head_k ** -0.5)
            if name in GLA_HELD:
                o_ref[...] = acc
            elif isinstance(post, tuple):
                scale, dilation = post
                if scale is not None:
                    acc = acc * (DIL_HEAD_DIM ** -0.5 * LOG2E)
                if dilation == 1:
                    o_ref[0, 0] = acc.astype(o_ref.dtype)
                else:
                    for t in range(width // LANES):
                        stage_ref[t] = acc[:, t * LANES:(t + 1) * LANES]
                    for r in range(dilation):
                        for t in range(width // LANES):
                            o_ref[0, r, :, t * LANES:(t + 1) * LANES] = stage_ref[
                                t, pl.ds(r, tm // dilation, stride=dilation), :].astype(o_ref.dtype)
            else:
                o_ref[0, :, c0:c0 + chunk] = acc.astype(o_ref.dtype)


def _in_projection(x, sc1, sh1, w_perm, w_gate, b_gate):
    bsz, s, d = x.shape
    pieces = _proj_pieces(d)
    assert sum(p[1] for p in pieces) <= w_perm.shape[0]
    tm = min(ROW_TILE, s)
    assert s % tm == 0 and tm % (8 * GLA_CHUNK) == 0
    dk = d // 2
    head_k = dk // GLA_HEADS
    out_shape, out_specs = [], []
    for name, width, post in pieces:
        if name in GLA_HELD:
            continue
        if isinstance(post, tuple):
            dil = post[1]
            assert tm % (dil * 16) == 0
            out_shape.append(jax.ShapeDtypeStruct((bsz, dil, s // dil, width), BF16))
            out_specs.append(pl.BlockSpec((1, dil, tm // dil, width), lambda b, i: (b, 0, i, 0)))
        else:
            out_shape.append(jax.ShapeDtypeStruct((bsz, s, width), BF16))
            out_specs.append(pl.BlockSpec((1, tm, width), lambda b, i: (b, i, 0)))
    row = lambda w: pl.BlockSpec((1, tm, w), lambda b, i: (b, i, 0))
    gla_names = ("q_in", "k_in", "q_st", "k_st", "dec")
    out_shape += [jax.ShapeDtypeStruct((bsz, s, dk), BF16)] * 4 + [jax.ShapeDtypeStruct((bsz, s // GLA_CHUNK, dk), F32)]
    out_specs += [row(dk)] * 4 + [pl.BlockSpec((1, tm // GLA_CHUNK, dk), lambda b, i: (b, i, 0))]
    bg = b_gate.reshape(1, dk)
    full = lambda a: pl.BlockSpec(a.shape, lambda b, i: (0,) * a.ndim)
    outs = pl.pallas_call(
        functools.partial(_proj_kernel, pieces, head_k),
        out_shape=out_shape,
        grid=(bsz, s // tm),
        in_specs=[row(d),
                  pl.BlockSpec((1, 1, d), lambda b, i: (b, 0, 0)),
                  pl.BlockSpec((1, 1, d), lambda b, i: (b, 0, 0)),
                  pl.BlockSpec(w_perm.shape, lambda b, i: (0, 0), pipeline_mode=pl.Buffered(1)),
                  full(w_gate), full(bg)],
        out_specs=out_specs,
        scratch_shapes=[pltpu.VMEM((DIL_GROUP_WIDTH // LANES, tm, LANES), F32),
                        pltpu.VMEM((tm, LANES), F32), pltpu.VMEM((tm, dk), F32), pltpu.VMEM((tm, dk), F32)],
        compiler_params=_cparams(("parallel", "arbitrary")),
        name="in_projection",
    )(x, sc1, sh1, w_perm, w_gate, bg)
    return dict(zip([p[0] for p in pieces if p[0] not in GLA_HELD] + list(gla_names), outs))


def _gla_kernel(n_chunks, head_k, head_v, qin_ref, kin_ref, qst_ref, kst_ref, dec_ref, v_ref, r_ref, ng_ref, o_ref,
                state_ref):
    @pl.when(pl.program_id(1) == 0)
    def _():
        state_ref[...] = jnp.zeros_like(state_ref)

    c = GLA_CHUNK
    causal = lax.broadcasted_iota(jnp.int32, (c, c), 0) >= lax.broadcasted_iota(jnp.int32, (c, c), 1)
    for ci in range(n_chunks):
        rows = slice(ci * c, (ci + 1) * c)
        for h in range(GLA_HEADS):
            ks = slice(h * head_k, (h + 1) * head_k)
            vs = slice(h * head_v, (h + 1) * head_v)
            vh = v_ref[0, rows, vs]
            att = lax.dot_general(qin_ref[0, rows, ks], kin_ref[0, rows, ks], NT_DIMS, preferred_element_type=F32)
            att = jnp.where(causal, att, 0.0).astype(BF16)
            st = state_ref[h]
            o = jnp.dot(att, vh, preferred_element_type=F32)
            o = o + lax.dot_general(qst_ref[0, rows, ks], st.astype(BF16), NT_DIMS, preferred_element_type=F32)
            kv_t = lax.dot_general(vh, kst_ref[0, rows, ks], TN_DIMS, preferred_element_type=F32)
            state_ref[h] = st * dec_ref[0, ci:ci + 1, ks] + kv_t
            ms = jnp.mean(o * o, axis=-1, keepdims=True)
            o = o * lax.rsqrt(ms + LN_EPS) * ng_ref[:, vs] * r_ref[0, rows, vs].astype(F32)
            o_ref[0, rows, vs] = o.astype(o_ref.dtype)


def _gla(q_in, k_in, q_st, k_st, dec, v, r_silu, norm_g):
    bsz, s, dk = q_in.shape
    dv = v.shape[-1]
    head_k, head_v = dk // GLA_HEADS, dv // GLA_HEADS
    n_chunks = min(GLA_STEP_CHUNKS, s // GLA_CHUNK)
    ct = GLA_CHUNK * n_chunks
    assert s % ct == 0
    row_spec = lambda w: pl.BlockSpec((1, ct, w), lambda b, i: (b, i, 0))
    full = lambda a: pl.BlockSpec(a.shape, lambda b, i: (0,) * a.ndim)
    ng = norm_g.reshape(1, dv)
    return pl.pallas_call(
        functools.partial(_gla_kernel, n_chunks, head_k, head_v),
        out_shape=jax.ShapeDtypeStruct((bsz, s, dv), BF16),
        grid=(bsz, s // ct),
        in_specs=[row_spec(dk)] * 4 + [pl.BlockSpec((1, n_chunks, dk), lambda b, i: (b, i, 0)),
                                       row_spec(dv), row_spec(dv), full(ng)],
        out_specs=row_spec(dv),
        scratch_shapes=[pltpu.VMEM((GLA_HEADS, head_v, head_k), F32)],
        compiler_params=_cparams(("parallel", "arbitrary")),
        name="gla",
    )(q_in, k_in, q_st, k_st, dec, v, r_silu, ng)


def _t5_bucket_np(dist):
    exact = REL_BUCKETS // 2
    d = np.maximum(dist, 1).astype(np.float32)
    large = exact + (np.log(d / np.float32(exact)) / np.float32(math.log(REL_MAX_DIST / exact))
                     * np.float32(REL_BUCKETS - exact)).astype(np.int32)
    large = np.minimum(large, REL_BUCKETS - 1)
    return np.where(dist < exact, dist, large).astype(np.int32)


def _band_tables(window, dilation):
    qi = np.arange(DIL_BLOCK)[:, None]
    kj = np.arange(2 * DIL_BLOCK)[None, :]
    m = qi + DIL_BLOCK - kj
    n_steps = window // dilation
    band = (m >= 0) & (m <= n_steps)
    bucket = _t5_bucket_np(np.clip(m, 0, n_steps) * dilation)
    return np.where(band, bucket, -1).astype(np.int32)


def _attn_kernel(nq, table_ref, bucket_ref, q_ref, kp_ref, kc_ref, vp_ref, vc_ref, o_ref, lse_ref,
                 bias_ref, p_ref):
    i = pl.program_id(1)
    blk = DIL_BLOCK
    hpg = DIL_HEADS_PER_GROUP
    n_pairs = hpg // 2

    @pl.when((pl.program_id(0) == 0) & (i == 0))
    def _():
        bucket = bucket_ref[...]
        for h in range(hpg):
            acc = jnp.full(bucket.shape, NEG, F32)
            for bkt in range(REL_BUCKETS):
                acc = jnp.where(bucket == bkt, table_ref[bkt, h] * LOG2E, acc)
            bias_ref[h * blk:(h + 1) * blk, :] = acc

    lane = lax.broadcasted_iota(jnp.int32, (blk, LANES), 1)
    low = lane < DIL_HEAD_DIM
    ones_rhs = jnp.ones((2 * blk, LANES), BF16)

    def windows(ref_p, ref_c, sq, qb, cols):
        if qb == 0:
            return jnp.concatenate([ref_p[sq, :, cols], ref_c[sq, 0:blk, cols]], axis=0)
        return ref_c[sq, (qb - 1) * blk:(qb + 1) * blk, cols]

    key_lane = lax.broadcasted_iota(jnp.int32, (1, 2 * blk), 1)
    no_prev = jnp.where((key_lane < blk) & (i == 0), NEG, 0.0)
    items = [(sq, qb, hp) for sq in range(q_ref.shape[0]) for qb in range(nq) for hp in range(n_pairs)]

    mxs = []
    for n, (sq, qb, hp) in enumerate(items):
        rows = slice(qb * blk, (qb + 1) * blk)
        cols = slice(hp * LANES, (hp + 1) * LANES)
        qp = q_ref[sq, rows, cols]
        zero = jnp.zeros_like(qp)
        qq = jnp.concatenate([jnp.where(low, qp, zero), jnp.where(low, zero, qp)], axis=0)
        keys = windows(kp_ref, kc_ref, sq, qb, cols)
        s = lax.dot_general(qq, keys, NT_DIMS, preferred_element_type=F32) + bias_ref[2 * hp * blk:(2 * hp + 2) * blk, :]
        if qb == 0:
            s = s + no_prev
        mx = jnp.max(s, axis=-1, keepdims=True)
        p_ref[n * 2 * blk:(n + 1) * 2 * blk, :] = jnp.exp2(s - mx).astype(BF16)
        mxs.append(mx)

    for n, (sq, qb, hp) in enumerate(items):
        rows = slice(qb * blk, (qb + 1) * blk)
        cols = slice(hp * LANES, (hp + 1) * LANES)
        vals = windows(vp_ref, vc_ref, sq, qb, cols)
        rhs = jnp.concatenate([vals, ones_rhs], axis=1)
        res = jnp.dot(p_ref[n * 2 * blk:(n + 1) * 2 * blk, :], rhs, preferred_element_type=F32)
        num = jnp.where(low, res[0:blk, 0:LANES], res[blk:2 * blk, 0:LANES])
        den = jnp.where(low, res[0:blk, LANES:], res[blk:2 * blk, LANES:])
        mx = jnp.where(low, mxs[n][0:blk], mxs[n][blk:2 * blk])
        o_ref[sq, rows, cols] = (num / den).astype(o_ref.dtype)
        lse_ref[sq, rows, cols] = (mx + jnp.log2(den)) * LN2


def _dilated_group_attention(q, k, v, table, window, dilation):
    bb, l, w = q.shape
    nq = min(ATT_STEP_BLOCKS, l // DIL_BLOCK)
    nsq = ATT_STEP_BLOCKS // nq
    assert l % (nq * DIL_BLOCK) == 0 and bb % nsq == 0
    steps = l // (nq * DIL_BLOCK)
    bucket = jnp.asarray(_band_tables(window, dilation))
    cur = pl.BlockSpec((nsq, nq * DIL_BLOCK, w), lambda b, i: (b, i, 0))
    prev = pl.BlockSpec((nsq, DIL_BLOCK, w), lambda b, i: (b, jnp.maximum(nq * i - 1, 0), 0))
    rows_all = nsq * nq * DIL_HEADS_PER_GROUP * DIL_BLOCK
    return pl.pallas_call(
        functools.partial(_attn_kernel, nq),
        out_shape=[jax.ShapeDtypeStruct((bb, l, w), BF16), jax.ShapeDtypeStruct((bb, l, w), F32)],
        grid=(bb // nsq, steps),
        in_specs=[pl.BlockSpec(memory_space=pltpu.SMEM),
                  pl.BlockSpec(bucket.shape, lambda b, i: (0, 0)),
                  cur, prev, cur, prev, cur],
        out_specs=[cur, cur],
        scratch_shapes=[pltpu.VMEM((DIL_HEADS_PER_GROUP * DIL_BLOCK, 2 * DIL_BLOCK), F32),
                        pltpu.VMEM((rows_all, 2 * DIL_BLOCK), BF16)],
        compiler_params=_cparams(("arbitrary", "arbitrary")),
        name=f"dilated_attn_d{dilation}",
    )(table, bucket, q, k, k, v, v)


def _merge_kernel(alpha, dilations, ygla_ref, o0_ref, o1_ref, o2_ref, l0_ref, l1_ref, l2_ref, gg_ref, ga_ref, x_ref,
                  g1_ref, sc2_ref, sh2_ref, ln_g_ref, ln_b_ref, wpg_ref, wpa_ref, wout_ref, wr_ref, br_ref, ltri_ref,
                  x1_ref, u2_ref, route_ref, ew_ref, cnt_ref, stage_ref, carry_ref):
    tm = x_ref.shape[1]

    @pl.when((pl.program_id(0) == 0) & (pl.program_id(1) == 0))
    def _():
        carry_ref[...] = jnp.zeros_like(carry_ref)

    n_lt = DIL_GROUP_WIDTH // LANES
    group_refs = tuple(zip((l0_ref, l1_ref, l2_ref), (o0_ref, o1_ref, o2_ref), dilations))
    for gi, (l_ref, o_ref, dil) in enumerate(group_refs):
        if dil > 1:
            for slot, ref in ((2 * gi, l_ref), (2 * gi + 1, o_ref)):
                for r in range(dil):
                    for t in range(n_lt):
                        stage_ref[slot, t, pl.ds(r, tm // dil, stride=dil), :] = ref[
                            0, r, :, t * LANES:(t + 1) * LANES].astype(F32)

    w_hi = wr_ref[...].astype(BF16)
    sub = ltri_ref.shape[0]
    for rows in (slice(r0, r0 + sub) for r0 in range(0, tm, sub)):
        def natural(ref, dil, slot):
            if dil == 1:
                return ref[0, 0, rows, :].astype(F32)
            return jnp.concatenate([stage_ref[slot, t, rows, :] for t in range(n_lt)], axis=1)

        lses = [natural(l_ref, dil, 2 * gi) for gi, (l_ref, _, dil) in enumerate(group_refs)]
        outs = [natural(o_ref, dil, 2 * gi + 1) for gi, (_, o_ref, dil) in enumerate(group_refs)]
        lm = jnp.maximum(jnp.maximum(lses[0], lses[1]), lses[2])
        es = [jnp.exp(l - lm) for l in lses]
        y_att = (es[0] * outs[0] + es[1] * outs[1] + es[2] * outs[2]) / (es[0] + es[1] + es[2])

        p_gla = jnp.dot(ygla_ref[0, rows, :], wpg_ref[...], preferred_element_type=F32)
        p_att = jnp.dot(y_att.astype(BF16), wpa_ref[...], preferred_element_type=F32)
        merged = gg_ref[0, rows, :].astype(F32) * p_gla + ga_ref[0, rows, :].astype(F32) * p_att
        y = jnp.dot(merged.astype(BF16), wout_ref[...], preferred_element_type=F32)
        x1 = _layer_norm(alpha * x_ref[0, rows, :] + g1_ref[0] * y, ln_g_ref[...], ln_b_ref[...])
        x1_ref[0, rows, :] = x1
        u2 = x1 * (1.0 + sc2_ref[0]) + sh2_ref[0]
        u2_ref[0, rows, :] = _pack_bf16_pairs(u2)

        logits = jnp.dot(u2.astype(BF16), w_hi, preferred_element_type=F32) + br_ref[...]
        lane = lax.broadcasted_iota(jnp.int32, logits.shape, 1)
        big = jnp.int32(LANES)
        lg = jnp.where(lane < MOE_GROUPS, logits, NEG)
        gmax = jnp.max(lg, axis=-1, keepdims=True)
        gidx = jnp.min(jnp.where(lg == gmax, lane, big), axis=-1, keepdims=True)
        gval = 1.0 / jnp.sum(jnp.exp(lg - gmax), axis=-1, keepdims=True)
        in_group = (lane >= MOE_GROUPS + gidx * MOE_EXPERTS) & (lane < MOE_GROUPS + (gidx + 1) * MOE_EXPERTS)
        le = jnp.where(in_group, logits, NEG)
        m1 = jnp.max(le, axis=-1, keepdims=True)
        i1 = jnp.min(jnp.where(le == m1, lane, big), axis=-1, keepdims=True)
        le2 = jnp.where(lane == i1, NEG, le)
        m2 = jnp.max(le2, axis=-1, keepdims=True)
        i2 = jnp.min(jnp.where(le2 == m2, lane, big), axis=-1, keepdims=True)
        t = jnp.exp(m2 - m1)
        w1 = 1.0 / (1.0 + t)
        w2 = t * w1

        hit1, hit2 = lane == i1, lane == i2
        onehot = jnp.where(hit1 | hit2, 1.0, 0.0)
        earlier = jnp.dot(ltri_ref[...], onehot.astype(BF16), preferred_element_type=F32) + carry_ref[...]
        rank1 = jnp.sum(jnp.where(hit1, earlier, 0.0), axis=-1, keepdims=True).astype(jnp.int32)
        rank2 = jnp.sum(jnp.where(hit2, earlier, 0.0), axis=-1, keepdims=True).astype(jnp.int32)
        carry_ref[...] = carry_ref[...] + jnp.sum(onehot, axis=0, keepdims=True)
        route = jnp.where(lane == 0, i1 - MOE_GROUPS, jnp.where(lane == 1, i2 - MOE_GROUPS,
                          jnp.where(lane == 2, rank1, jnp.where(lane == 3, rank2, 0))))
        route_ref[0, :, rows] = jnp.transpose(route)[0:ROUTE_ROWS, :]
        ew_ref[0, rows, :] = jnp.where(lane == 0, gval * w1, jnp.where(lane == 1, gval * w2, 0.0))
    cnt_ref[...] = carry_ref[...].astype(jnp.int32)


def _merge(alpha, y_gla, o_groups, lse_groups, g_gla, g_att, x, g1, sc2, sh2, ln_g, ln_b, wpg, wpa, wout, wr, br):
    bsz, s, d = x.shape
    tm = min(ROW_TILE, s)
    assert s % tm == 0
    dilations = tuple(dil for _, dil in DIL_PATTERNS)
    row = lambda w: pl.BlockSpec((1, tm, w), lambda b, i: (b, i, 0))
    sub = lambda dil: pl.BlockSpec((1, dil, tm // dil, DIL_GROUP_WIDTH), lambda b, i: (b, 0, i, 0))
    per_b = pl.BlockSpec((1, 1, d), lambda b, i: (b, 0, 0))
    full = lambda a: pl.BlockSpec(a.shape, lambda b, i: (0,) * a.ndim)
    ln_g2, ln_b2 = ln_g.reshape(1, d), ln_b.reshape(1, d)
    sub_rows = min(MERGE_SUB_ROWS, tm)
    assert tm % sub_rows == 0
    ltri = jnp.asarray(np.tril(np.ones((sub_rows, sub_rows), np.float32), -1), BF16)
    return pl.pallas_call(
        functools.partial(_merge_kernel, alpha, dilations),
        out_shape=[jax.ShapeDtypeStruct((bsz, s, d), F32), jax.ShapeDtypeStruct((bsz, s, d // 2), jnp.int32),
                   jax.ShapeDtypeStruct((bsz, ROUTE_ROWS, s), jnp.int32), jax.ShapeDtypeStruct((bsz, s, LANES), F32),
                   jax.ShapeDtypeStruct((1, LANES), jnp.int32)],
        grid=(bsz, s // tm),
        in_specs=[row(y_gla.shape[-1])] + [sub(dil) for dil in dilations] * 2
                 + [row(d), row(d), row(d), per_b, per_b, per_b, full(ln_g2), full(ln_b2),
                    full(wpg), full(wpa), full(wout), full(wr), full(br), full(ltri)],
        out_specs=[row(d), row(d // 2), pl.BlockSpec((1, ROUTE_ROWS, tm), lambda b, i: (b, 0, i)), row(LANES),
                   pl.BlockSpec((1, LANES), lambda b, i: (0, 0))],
        scratch_shapes=[pltpu.VMEM((2 * DIL_GROUPS, DIL_GROUP_WIDTH // LANES, tm, LANES), F32),
                        pltpu.VMEM((1, LANES), F32)],
        compiler_params=_cparams(("arbitrary", "arbitrary")),
        name="merge_ln1_router",
    )(y_gla, *o_groups, *lse_groups, g_gla, g_att, x, g1, sc2, sh2, ln_g2, ln_b2, wpg, wpa, wout, wr, br, ltri)


def _expert_kernel(run_ref, valid_ref, rexp_ref, used_ref, x_ref, wg_hbm, wu_hbm, wd_hbm, o_ref,
                   wg_f, wu_f, wd_f, wg_s, wu_s, wd_s, sem):
    t = pl.program_id(0)
    n_tiles_used, n_runs = used_ref[0], used_ref[1]
    run = run_ref[t]
    active = t < n_tiles_used
    first_of_run = (t == 0) | (run_ref[jnp.maximum(t - 1, 0)] != run)

    def weight_copies(r):
        e, slot = rexp_ref[r], r % 2
        return [pltpu.make_async_copy(hbm.at[e], buf.at[slot], sem.at[slot, j])
                for j, (hbm, buf) in enumerate(((wg_hbm, wg_f), (wu_hbm, wu_f), (wd_hbm, wd_f)))]

    @pl.when(active & (t == 0))
    def _():
        for cp in weight_copies(0):
            cp.start()

    @pl.when(active & first_of_run)
    def _():
        @pl.when(run + 1 < n_runs)
        def _():
            for cp in weight_copies(run + 1):
                cp.start()

        for cp in weight_copies(run):
            cp.wait()
        slot = run % 2
        wg_s[...] = wg_f[slot].astype(BF16)
        wu_s[...] = wu_f[slot].astype(BF16)
        wd_s[...] = wd_f[slot].astype(BF16)

    n_valid = jnp.where(active, valid_ref[t], 0)
    def ffn(rows):
        xt = _unpack_bf16_pairs(x_ref[rows, :]).astype(BF16)
        hg = jnp.dot(xt, wg_s[...], preferred_element_type=F32)
        hu = jnp.dot(xt, wu_s[...], preferred_element_type=F32)
        h = (_silu(hg) * hu).astype(BF16)
        o_ref[rows, :] = _pack_bf16_pairs(jnp.dot(h, wd_s[...], preferred_element_type=F32))

    def zero(rows):
        o_ref[rows, :] = jnp.zeros((rows.stop - rows.start, o_ref.shape[1]), o_ref.dtype)

    tm = x_ref.shape[0]
    first, second = slice(0, EXPERT_BLOCK), slice(EXPERT_BLOCK, tm)

    @pl.when(n_valid > EXPERT_BLOCK)
    def _():
        ffn(slice(0, tm))

    @pl.when((n_valid > 0) & (n_valid <= EXPERT_BLOCK))
    def _():
        ffn(first)
        zero(second)

    @pl.when(n_valid <= 0)
    def _():
        zero(slice(0, tm))


def _expert_ffn(tile_run, tile_valid, run_expert, used, xg, w_gate, w_up, w_down):
    p = xg.shape[0]
    ne, d, ff = w_gate.shape
    tm = EXPERT_TILE
    n_tiles = p // tm
    hbm = pl.BlockSpec(memory_space=pl.ANY)
    grid_spec = pltpu.PrefetchScalarGridSpec(
        num_scalar_prefetch=4,
        grid=(n_tiles,),
        in_specs=[pl.BlockSpec((tm, d // 2), lambda t, *_: (t, 0)), hbm, hbm, hbm],
        out_specs=pl.BlockSpec((tm, d // 2), lambda t, *_: (t, 0)),
        scratch_shapes=[pltpu.VMEM((2, d, ff), F32), pltpu.VMEM((2, d, ff), F32), pltpu.VMEM((2, ff, d), F32),
                        pltpu.VMEM((d, ff), BF16), pltpu.VMEM((d, ff), BF16), pltpu.VMEM((ff, d), BF16),
                        pltpu.SemaphoreType.DMA((2, 3))],
    )
    return pl.pallas_call(
        _expert_kernel,
        out_shape=jax.ShapeDtypeStruct((p, d // 2), jnp.int32),
        grid_spec=grid_spec,
        compiler_params=_cparams(("arbitrary",)),
        name="expert_ffn",
    )(tile_run, tile_valid, run_expert, used, xg, w_gate, w_up, w_down)


def _final_kernel(alpha, x1_ref, ya_ref, yb_ref, ew_ref, g2_ref, ln_g_ref, ln_b_ref, o_ref):
    ew = ew_ref[0]
    y = ew[:, 0:1] * _unpack_bf16_pairs(ya_ref[0]) + ew[:, 1:2] * _unpack_bf16_pairs(yb_ref[0])
    o_ref[0] = _layer_norm(alpha * x1_ref[0] + g2_ref[0] * y, ln_g_ref[...], ln_b_ref[...])


def _final(alpha, x1, ya, yb, ew, g2, ln_g, ln_b):
    bsz, s, d = x1.shape
    tm = min(ROW_TILE, s)
    row = lambda w: pl.BlockSpec((1, tm, w), lambda b, i: (b, i, 0))
    full = lambda a: pl.BlockSpec(a.shape, lambda b, i: (0,) * a.ndim)
    ln_g2, ln_b2 = ln_g.reshape(1, d), ln_b.reshape(1, d)
    return pl.pallas_call(
        functools.partial(_final_kernel, alpha),
        out_shape=jax.ShapeDtypeStruct((bsz, s, d), F32),
        grid=(bsz, s // tm),
        in_specs=[row(d), row(d // 2), row(d // 2), row(LANES), pl.BlockSpec((1, 1, d), lambda b, i: (b, 0, 0)),
                  full(ln_g2), full(ln_b2)],
        out_specs=row(d),
        compiler_params=_cparams(("parallel", "arbitrary")),
        name="combine_ln2",
    )(x1, ya, yb, ew, g2, ln_g2, ln_b2)


SC_CORES = 2
SC_SUBCORES = 16
SC_CHUNK = 64


def _sc_mesh():
    return plsc.VectorSubcoreMesh(core_axis_name="c", subcore_axis_name="s")


def _sc_scatter_rows(rows, dest0, dest1, n_rows):
    n, w = rows.shape
    n_workers = SC_CORES * SC_SUBCORES
    assert n % (n_workers * SC_CHUNK) == 0
    n_chunks = n // (n_workers * SC_CHUNK)
    d0 = dest0.reshape(n // SC_CHUNK, 1, SC_CHUNK)
    d1 = dest1.reshape(n // SC_CHUNK, 1, SC_CHUNK)

    @functools.partial(
        pl.kernel, mesh=_sc_mesh(), out_type=jax.ShapeDtypeStruct((n_rows, w), rows.dtype),
        scratch_types=[pltpu.VMEM((n_chunks, 1, SC_CHUNK), jnp.int32), pltpu.VMEM((n_chunks, 1, SC_CHUNK), jnp.int32),
                       pltpu.VMEM((2, SC_CHUNK, w), rows.dtype),
                       pltpu.SemaphoreType.DMA((2,)), pltpu.SemaphoreType.DMA((2, 2))])
    def scatter_kernel(rows_hbm, d0_hbm, d1_hbm, out_hbm, i0_v, i1_v, rows_v, read_sem, scat_sem):
        wid = lax.axis_index("s") * SC_CORES + lax.axis_index("c")
        first = wid * n_chunks
        pltpu.sync_copy(d0_hbm.at[pl.ds(first, n_chunks)], i0_v)
        pltpu.sync_copy(d1_hbm.at[pl.ds(first, n_chunks)], i1_v)

        def read(j):
            return pltpu.make_async_copy(rows_hbm.at[pl.ds((first + j) * SC_CHUNK, SC_CHUNK)], rows_v.at[j % 2],
                                         read_sem.at[j % 2])

        def scatters(j):
            return [pltpu.make_async_copy(rows_v.at[j % 2], out_hbm.at[idx.at[j].at[0]], scat_sem.at[j % 2, k])
                    for k, idx in enumerate((i0_v, i1_v))]

        read(0).start()
        for j in range(n_chunks):
            read(j).wait()
            if j + 1 < n_chunks:
                if j >= 1:
                    for cp in scatters(j - 1):
                        cp.wait()
                read(j + 1).start()
            for cp in scatters(j):
                cp.start()
        for j in range(max(n_chunks - 2, 0), n_chunks):
            for cp in scatters(j):
                cp.wait()

    return scatter_kernel(rows, d0, d1)


def _sc_gather_rows(table, dest0, dest1):
    n = dest0.shape[0]
    w = table.shape[1]
    n_workers = SC_CORES * SC_SUBCORES
    assert n % (n_workers * SC_CHUNK) == 0
    n_chunks = n // (n_workers * SC_CHUNK)
    d0 = dest0.reshape(n // SC_CHUNK, 1, SC_CHUNK)
    d1 = dest1.reshape(n // SC_CHUNK, 1, SC_CHUNK)
    out = jax.ShapeDtypeStruct((n, w), table.dtype)

    @functools.partial(
        pl.kernel, mesh=_sc_mesh(), out_type=(out, out),
        scratch_types=[pltpu.VMEM((n_chunks, 1, SC_CHUNK), jnp.int32), pltpu.VMEM((n_chunks, 1, SC_CHUNK), jnp.int32),
                       pltpu.VMEM((2, SC_CHUNK, w), table.dtype),
                       pltpu.SemaphoreType.DMA((2,)), pltpu.SemaphoreType.DMA((2,))])
    def gather_kernel(table_hbm, d0_hbm, d1_hbm, a_hbm, b_hbm, i0_v, i1_v, rows_v, gather_sem, write_sem):
        wid = lax.axis_index("s") * SC_CORES + lax.axis_index("c")
        first = wid * n_chunks
        pltpu.sync_copy(d0_hbm.at[pl.ds(first, n_chunks)], i0_v)
        pltpu.sync_copy(d1_hbm.at[pl.ds(first, n_chunks)], i1_v)
        n_items = 2 * n_chunks

        def gather(m):
            idx = (i0_v, i1_v)[m % 2]
            return pltpu.make_async_copy(table_hbm.at[idx.at[m // 2].at[0]], rows_v.at[m % 2], gather_sem.at[m % 2])

        def write(m):
            o_hbm = (a_hbm, b_hbm)[m % 2]
            return pltpu.make_async_copy(rows_v.at[m % 2], o_hbm.at[pl.ds((first + m // 2) * SC_CHUNK, SC_CHUNK)],
                                         write_sem.at[m % 2])

        gather(0).start()
        for m in range(n_items):
            gather(m).wait()
            if m + 1 < n_items:
                if m >= 1:
                    write(m - 1).wait()
                gather(m + 1).start()
            write(m).start()
        for m in range(max(n_items - 2, 0), n_items):
            write(m).wait()

    return gather_kernel(table, d0, d1)


def _dispatch_plan(route, counts):
    tm = EXPERT_TILE
    e0, e1, r0, r1 = (route[:, j, :].reshape(-1) for j in range(4))
    experts = jnp.arange(MOE_TOTAL, dtype=jnp.int32)
    tiles_per = (counts + tm - 1) // tm
    tile_end = jnp.cumsum(tiles_per)
    pad_start = ((tile_end - tiles_per) * tm).astype(jnp.int32)

    def lookup(e):
        return jnp.sum(jnp.where(e[None, :] == experts[:, None], pad_start[:, None], 0), axis=0)

    dest0, dest1 = lookup(e0) + r0, lookup(e1) + r1
    n_tiles = (2 * e0.size + MOE_TOTAL * tm) // tm
    tile_expert = jnp.minimum(jnp.sum(tile_end[None, :] <= jnp.arange(n_tiles)[:, None], axis=1), MOE_TOTAL - 1)
    nonempty = counts > 0
    run_of_expert = jnp.cumsum(nonempty.astype(jnp.int32)) - 1
    run_expert = jnp.sum(jnp.where(nonempty[None, :] & (run_of_expert[None, :] == experts[:, None]),
                                   experts[None, :], 0), axis=1).astype(jnp.int32)
    of_tile = tile_expert[:, None] == experts[None, :]
    tile_run = jnp.sum(jnp.where(of_tile, run_of_expert[None, :], 0), axis=1).astype(jnp.int32)
    rows_left = (counts + pad_start)[None, :] - jnp.arange(n_tiles)[:, None] * tm
    tile_valid = jnp.clip(jnp.sum(jnp.where(of_tile, rows_left, 0), axis=1), 0, tm).astype(jnp.int32)
    used = jnp.stack([tile_end[-1], jnp.sum(nonempty)]).astype(jnp.int32)
    return dest0, dest1, tile_run, tile_valid, run_expert, used, n_tiles * tm


def _layer(x, c, rel_bias, w_ada, b_ada, w_in, w_gla_gate, b_gla_gate, gla_norm, w_proj_gla, w_proj_attn, w_out,
           ln1_g, ln1_b, w_rg, b_rg, w_re, b_re, w_eg, w_eu, w_ed, ln2_g, ln2_b):
    bsz, s, d = x.shape
    alpha = (2.0 * DEPTH) ** 0.25
    mods = _ada_mods(c, w_ada, b_ada)
    sh1, sc1, g1, sh2, sc2, g2 = [m.reshape(bsz, 1, d) for m in jnp.split(mods, N_MOD, axis=-1)]

    lr0 = d // 2 * 2 + 2 * d
    z = _in_projection(x, sc1, sh1, _prep_in_weight(w_in, lr0), w_gla_gate, b_gla_gate)

    y_gla = _gla(z["q_in"], z["k_in"], z["q_st"], z["k_st"], z["dec"], z["v_gla"], z["r_gla"], gla_norm)

    o_groups, lse_groups = [], []
    for g, (window, dilation) in enumerate(DIL_PATTERNS):
        l = s // dilation
        qg, kg, vg = (z[f"{n}{g}"].reshape(bsz * dilation, l, DIL_GROUP_WIDTH) for n in ("q_att", "k_att", "v_att"))
        table = rel_bias[:, g * DIL_HEADS_PER_GROUP:(g + 1) * DIL_HEADS_PER_GROUP]
        o, lse = _dilated_group_attention(qg, kg, vg, table, window, dilation)
        o_groups.append(o.reshape(bsz, dilation, l, DIL_GROUP_WIDTH))
        lse_groups.append(lse.reshape(bsz, dilation, l, DIL_GROUP_WIDTH))

    wr = jnp.concatenate([w_rg, w_re, jnp.zeros((d, LANES - MOE_GROUPS - MOE_TOTAL), F32)], axis=1)
    br = jnp.concatenate([b_rg, b_re, jnp.zeros((LANES - MOE_GROUPS - MOE_TOTAL,), F32)]).reshape(1, LANES)
    x1, u2, route, ew, cnt = _merge(alpha, y_gla, o_groups, lse_groups, z["g_gla"], z["g_att"], x, g1, sc2, sh2,
                                    ln1_g, ln1_b, w_proj_gla.astype(BF16), w_proj_attn.astype(BF16),
                                    w_out.astype(BF16), wr, br)

    n = bsz * s
    counts = cnt[0, MOE_GROUPS:MOE_GROUPS + MOE_TOTAL]
    dest0, dest1, tile_run, tile_valid, run_expert, used, n_rows = _dispatch_plan(route, counts)
    xg = _sc_scatter_rows(u2.reshape(n, d // 2), dest0, dest1, n_rows)
    ff = w_eg.shape[-1]
    yo = _expert_ffn(tile_run, tile_valid, run_expert, used, xg, w_eg.reshape(MOE_TOTAL, d, ff),
                     w_eu.reshape(MOE_TOTAL, d, ff), w_ed.reshape(MOE_TOTAL, ff, d))
    ya, yb = (y.reshape(bsz, s, d // 2) for y in _sc_gather_rows(yo, dest0, dest1))
    return _final(alpha, x1, ya, yb, ew, g2, ln2_g, ln2_b)


def kernel(x, c, rel_bias, w_ada, b_ada, w_in, w_gla_gate, b_gla_gate, gla_norm, w_proj_gla, w_proj_attn, w_out,
           ln1_g, ln1_b, w_router_group, b_router_group, w_router_expert, b_router_expert, w_exp_gate, w_exp_up,
           w_exp_down, ln2_g, ln2_b):
    assert w_ada.shape[0] == DEPTH
    return _layer(x, c, rel_bias, w_ada[0], b_ada[0], w_in[0:1], w_gla_gate[0], b_gla_gate[0], gla_norm[0],
                  w_proj_gla[0], w_proj_attn[0], w_out[0], ln1_g[0], ln1_b[0], w_router_group[0],
                  b_router_group[0], w_router_expert[0], b_router_expert[0], w_exp_gate[0], w_exp_up[0],
                  w_exp_down[0], ln2_g[0], ln2_b[0])
```

```python
import functools
import math

import numpy as np
import jax
import jax.numpy as jnp
from jax import lax
from jax.experimental import pallas as pl
from jax.experimental.pallas import tpu as pltpu
from jax.experimental.pallas import tpu_sc as plsc

F32 = jnp.float32
BF16 = jnp.bfloat16

N_MOD = 6
GLA_HEADS = 4
GLA_LOWRANK = 16
GLA_TAU = 16.0
GLA_CHUNK = 64
DIL_PATTERNS = ((128, 1), (512, 4), (2048, 16))
DIL_GROUPS = len(DIL_PATTERNS)
DIL_HEADS_PER_GROUP = 8
DIL_HEAD_DIM = 64
DIL_GROUP_WIDTH = DIL_HEADS_PER_GROUP * DIL_HEAD_DIM
DIL_BLOCK = 128
REL_BUCKETS = 32
REL_MAX_DIST = 2048
MOE_GROUPS = 4
MOE_EXPERTS = 8
MOE_TOTAL = MOE_GROUPS * MOE_EXPERTS
LN_EPS = 1e-5
DEPTH = 1

LANES = 128
VMEM_LIMIT = 56 * 1024 * 1024
LOG2E = 1.4426950408889634
LN2 = 0.6931471805599453
NEG = -1e30
ROW_TILE = 512
EXPERT_TILE = 1024
EXPERT_BLOCK = 256
GLA_STEP_CHUNKS = 8
ATT_STEP_BLOCKS = 4
MERGE_SUB_ROWS = 512
ROUTE_ROWS = 8

HIGHEST = lax.Precision.HIGHEST
NT_DIMS = (((1,), (1,)), ((), ()))
TN_DIMS = (((0,), (0,)), ((), ()))


def _cparams(sem):
    return pltpu.CompilerParams(dimension_semantics=sem, vmem_limit_bytes=VMEM_LIMIT)


def _sigmoid(x):
    return 0.5 * jnp.tanh(0.5 * x) + 0.5


def _silu(x):
    return x * _sigmoid(x)


def _layer_norm(x, g, b):
    mu = jnp.mean(x, axis=-1, keepdims=True)
    xc = x - mu
    var = jnp.mean(xc * xc, axis=-1, keepdims=True)
    return xc * lax.rsqrt(var + LN_EPS) * g + b


def _pack_bf16_pairs(x):
    w = x.shape[1] // 2
    lo = lax.bitcast_convert_type(x[:, :w].astype(BF16).astype(F32), jnp.uint32) >> 16
    hi = lax.bitcast_convert_type(x[:, w:].astype(BF16).astype(F32), jnp.uint32) & jnp.uint32(0xFFFF0000)
    return lax.bitcast_convert_type(lo | hi, jnp.int32)


def _unpack_bf16_pairs(p):
    u = lax.bitcast_convert_type(p, jnp.uint32)
    lo = lax.bitcast_convert_type(u << 16, F32)
    hi = lax.bitcast_convert_type(u & jnp.uint32(0xFFFF0000), F32)
    return jnp.concatenate([lo, hi], axis=1)


def _split3(x):
    hi = x.astype(BF16)
    r1 = x - hi.astype(F32)
    mid = r1.astype(BF16)
    lo = (r1 - mid.astype(F32)).astype(BF16)
    return hi, mid, lo


def _mods_kernel(c_ref, w_ref, b_ref, o_ref):
    a = _silu(c_ref[...])
    o_ref[...] = jnp.dot(a, w_ref[...], precision=HIGHEST, preferred_element_type=F32) + b_ref[...]


def _ada_mods(c, w, b):
    bsz, d = c.shape
    n = w.shape[1]
    tn = 1536
    assert n % tn == 0
    return pl.pallas_call(
        _mods_kernel,
        out_shape=jax.ShapeDtypeStruct((bsz, n), F32),
        grid=(n // tn,),
        in_specs=[pl.BlockSpec((bsz, d), lambda j: (0, 0)),
                  pl.BlockSpec((d, tn), lambda j: (0, j)),
                  pl.BlockSpec((1, tn), lambda j: (0, j))],
        out_specs=pl.BlockSpec((bsz, tn), lambda j: (0, j)),
        compiler_params=_cparams(("arbitrary",)),
        name="ada_mods",
    )(c, w, b.reshape(1, n))


def _proj_pieces(d_model):
    dk = d_model // 2
    pieces = [("q_gla", dk, "scale_q_gla"), ("k_gla", dk, None), ("v_gla", d_model, None), ("r_gla", d_model, "silu")]
    for name, post in (("q_att", "scale_q_att"), ("k_att", None), ("v_att", None)):
        for g, (_, dilation) in enumerate(DIL_PATTERNS):
            pieces.append((f"{name}{g}", DIL_GROUP_WIDTH, (post, dilation)))
    pieces += [("g_gla", d_model, "sigmoid"), ("g_att", d_model, "sigmoid"), ("lr", LANES, "lowrank")]
    return tuple(pieces)


WT_BLOCK = 512


def _wprep_kernel(n_main_blocks, w_ref, o_ref):
    blk = w_ref[0]
    row = lax.broadcasted_iota(jnp.int32, blk.shape, 0)
    keep = (pl.program_id(0) < n_main_blocks) | (row < GLA_LOWRANK)
    o_ref[...] = jnp.where(keep, blk, 0.0).astype(BF16)


def _prep_in_weight(w_in, lr0):
    w_t = jnp.swapaxes(w_in, 1, 2)
    _, n_in, d = w_t.shape
    n_main = n_in - GLA_LOWRANK
    assert lr0 % WT_BLOCK == 0 and n_main % WT_BLOCK == 0
    n_main_blocks = n_main // WT_BLOCK

    def src_row(j):
        start = j * WT_BLOCK
        octet = jnp.where(j < n_main_blocks, (start + jnp.where(start >= lr0, GLA_LOWRANK, 0)) // 8, lr0 // 8)
        return octet * 8

    return pl.pallas_call(
        functools.partial(_wprep_kernel, n_main_blocks),
        out_shape=jax.ShapeDtypeStruct((n_main + WT_BLOCK, d), BF16),
        grid=(n_main_blocks + 1,),
        in_specs=[pl.BlockSpec((pl.Element(1), pl.Element(WT_BLOCK), pl.Element(d)), lambda j: (0, src_row(j), 0))],
        out_specs=pl.BlockSpec((WT_BLOCK, d), lambda j: (j, 0)),
        compiler_params=_cparams(("parallel",)),
        name="prep_in_weight",
    )(w_t)


GLA_HELD = ("lr", "q_gla", "k_gla")


def _gla_operands(hold, wg_ref, bg_ref, qin_ref, kin_ref, qst_ref, kst_ref, dec_ref):
    c = GLA_CHUNK
    tm = hold["q_gla"].shape[0]
    tril = (lax.broadcasted_iota(jnp.int32, (c, c), 0) >= lax.broadcasted_iota(jnp.int32, (c, c), 1)).astype(BF16)
    mid = c // 2 - 1
    lr, wg = hold["lr"][:, 0:GLA_LOWRANK], wg_ref[...]
    lr_hi, wg_hi = lr.astype(BF16), wg.astype(BF16)
    lr_lo, wg_lo = (lr - lr_hi.astype(F32)).astype(BF16), (wg - wg_hi.astype(F32)).astype(BF16)
    gate_in = (jnp.dot(lr_hi, wg_hi, preferred_element_type=F32) + jnp.dot(lr_lo, wg_hi, preferred_element_type=F32)
               + jnp.dot(lr_hi, wg_lo, preferred_element_type=F32)) + bg_ref[...]
    g_all = (jnp.minimum(gate_in, 0.0) - jnp.log(1.0 + jnp.exp(-jnp.abs(gate_in)))) * (1.0 / GLA_TAU)
    g_hi, g_mid, g_lo = _split3(g_all)
    for ci in range(tm // c):
        rows = slice(ci * c, (ci + 1) * c)
        bc = (jnp.dot(tril, g_hi[rows], preferred_element_type=F32)
              + jnp.dot(tril, g_mid[rows], preferred_element_type=F32)
              + jnp.dot(tril, g_lo[rows], preferred_element_type=F32))
        b_mid = bc[mid:mid + 1, :]
        b_last = bc[c - 1:c, :]
        qf = hold["q_gla"][rows, :]
        kf = hold["k_gla"][rows, :]
        qin_ref[0, rows, :] = (qf * jnp.exp(bc - b_mid)).astype(BF16)
        kin_ref[0, rows, :] = (kf * jnp.exp(b_mid - bc)).astype(BF16)
        qst_ref[0, rows, :] = (qf * jnp.exp(bc)).astype(BF16)
        kst_ref[0, rows, :] = (kf * jnp.exp(b_last - bc)).astype(BF16)
        dec_ref[0, ci:ci + 1, :] = jnp.exp(b_last)


def _proj_kernel(pieces, head_k, x_ref, sc_ref, sh_ref, w_ref, wg_ref, bg_ref, *refs):
    n_out = len(pieces) - len(GLA_HELD)
    out_refs = dict(zip([p[0] for p in pieces if p[0] not in GLA_HELD], refs[:n_out]))
    gla_out_refs = refs[n_out:n_out + 5]
    stage_ref = refs[n_out + 5]
    hold = dict(zip(GLA_HELD, refs[n_out + 6:]))
    tm = x_ref.shape[1]
    u = (x_ref[0] * (1.0 + sc_ref[0]) + sh_ref[0]).astype(BF16)
    offsets, off = {}, 0
    for name, width, _ in pieces:
        offsets[name] = off
        off += width
    by_name = {p[0]: p for p in pieces}
    held = [(by_name[n], 0) for n in GLA_HELD]
    rest = [(p, c0) for p in pieces if p[0] not in GLA_HELD for c0 in range(0, p[1], min(p[1], 512))]
    for n, (piece, c0) in enumerate(held + rest):
        if n == len(held):
            _gla_operands(hold, wg_ref, bg_ref, *gla_out_refs)
        name, width, post = piece
        o_ref = hold[name] if name in GLA_HELD else out_refs[name]
        off = offsets[name]
        chunk = min(width, 512)
        if True:
            acc = lax.dot_general(u, w_ref[off + c0:off + c0 + chunk, :], NT_DIMS, preferred_element_type=F32)
            if post == "silu":
                acc = _silu(acc)
            elif post == "sigmoid":
                acc = _sigmoid(acc)
            elif post == "scale_q_gla":
                acc = acc * (head_k ** -0.5)
            if name in GLA_HELD:
                o_ref[...] = acc
            elif isinstance(post, tuple):
                scale, dilation = post
                if scale is not None:
                    acc = acc * (DIL_HEAD_DIM ** -0.5 * LOG2E)
                if dilation == 1:
                    o_ref[0, 0] = acc.astype(o_ref.dtype)
                else:
                    for t in range(width // LANES):
                        stage_ref[t] = acc[:, t * LANES:(t + 1) * LANES]
                    for r in range(dilation):
                        for t in range(width // LANES):
                            o_ref[0, r, :, t * LANES:(t + 1) * LANES] = stage_ref[
                                t, pl.ds(r, tm // dilation, stride=dilation), :].astype(o_ref.dtype)
            else:
                o_ref[0, :, c0:c0 + chunk] = acc.astype(o_ref.dtype)


def _in_projection(x, sc1, sh1, w_perm, w_gate, b_gate):
    bsz, s, d = x.shape
    pieces = _proj_pieces(d)
    assert sum(p[1] for p in pieces) <= w_perm.shape[0]
    tm = min(ROW_TILE, s)
    assert s % tm == 0 and tm % (8 * GLA_CHUNK) == 0
    dk = d // 2
    head_k = dk // GLA_HEADS
    out_shape, out_specs = [], []
    for name, width, post in pieces:
        if name in GLA_HELD:
            continue
        if isinstance(post, tuple):
            dil = post[1]
            assert tm % (dil * 16) == 0
            out_shape.append(jax.ShapeDtypeStruct((bsz, dil, s // dil, width), BF16))
            out_specs.append(pl.BlockSpec((1, dil, tm // dil, width), lambda b, i: (b, 0, i, 0)))
        else:
            out_shape.append(jax.ShapeDtypeStruct((bsz, s, width), BF16))
            out_specs.append(pl.BlockSpec((1, tm, width), lambda b, i: (b, i, 0)))
    row = lambda w: pl.BlockSpec((1, tm, w), lambda b, i: (b, i, 0))
    gla_names = ("q_in", "k_in", "q_st", "k_st", "dec")
    out_shape += [jax.ShapeDtypeStruct((bsz, s, dk), BF16)] * 4 + [jax.ShapeDtypeStruct((bsz, s // GLA_CHUNK, dk), F32)]
    out_specs += [row(dk)] * 4 + [pl.BlockSpec((1, tm // GLA_CHUNK, dk), lambda b, i: (b, i, 0))]
    bg = b_gate.reshape(1, dk)
    full = lambda a: pl.BlockSpec(a.shape, lambda b, i: (0,) * a.ndim)
    outs = pl.pallas_call(
        functools.partial(_proj_kernel, pieces, head_k),
        out_shape=out_shape,
        grid=(bsz, s // tm),
        in_specs=[row(d),
                  pl.BlockSpec((1, 1, d), lambda b, i: (b, 0, 0)),
                  pl.BlockSpec((1, 1, d), lambda b, i: (b, 0, 0)),
                  pl.BlockSpec(w_perm.shape, lambda b, i: (0, 0), pipeline_mode=pl.Buffered(1)),
                  full(w_gate), full(bg)],
        out_specs=out_specs,
        scratch_shapes=[pltpu.VMEM((DIL_GROUP_WIDTH // LANES, tm, LANES), F32),
                        pltpu.VMEM((tm, LANES), F32), pltpu.VMEM((tm, dk), F32), pltpu.VMEM((tm, dk), F32)],
        compiler_params=_cparams(("parallel", "arbitrary")),
        name="in_projection",
    )(x, sc1, sh1, w_perm, w_gate, bg)
    return dict(zip([p[0] for p in pieces if p[0] not in GLA_HELD] + list(gla_names), outs))


def _gla_kernel(n_chunks, head_k, head_v, qin_ref, kin_ref, qst_ref, kst_ref, dec_ref, v_ref, r_ref, ng_ref, o_ref,
                state_ref):
    @pl.when(pl.program_id(1) == 0)
    def _():
        state_ref[...] = jnp.zeros_like(state_ref)

    c = GLA_CHUNK
    causal = lax.broadcasted_iota(jnp.int32, (c, c), 0) >= lax.broadcasted_iota(jnp.int32, (c, c), 1)
    for ci in range(n_chunks):
        rows = slice(ci * c, (ci + 1) * c)
        for h in range(GLA_HEADS):
            ks = slice(h * head_k, (h + 1) * head_k)
            vs = slice(h * head_v, (h + 1) * head_v)
            vh = v_ref[0, rows, vs]
            att = lax.dot_general(qin_ref[0, rows, ks], kin_ref[0, rows, ks], NT_DIMS, preferred_element_type=F32)
            att = jnp.where(causal, att, 0.0).astype(BF16)
            st = state_ref[h]
            o = jnp.dot(att, vh, preferred_element_type=F32)
            o = o + lax.dot_general(qst_ref[0, rows, ks], st.astype(BF16), NT_DIMS, preferred_element_type=F32)
            kv_t = lax.dot_general(vh, kst_ref[0, rows, ks], TN_DIMS, preferred_element_type=F32)
            state_ref[h] = st * dec_ref[0, ci:ci + 1, ks] + kv_t
            ms = jnp.mean(o * o, axis=-1, keepdims=True)
            o = o * lax.rsqrt(ms + LN_EPS) * ng_ref[:, vs] * r_ref[0, rows, vs].astype(F32)
            o_ref[0, rows, vs] = o.astype(o_ref.dtype)


def _gla(q_in, k_in, q_st, k_st, dec, v, r_silu, norm_g):
    bsz, s, dk = q_in.shape
    dv = v.shape[-1]
    head_k, head_v = dk // GLA_HEADS, dv // GLA_HEADS
    n_chunks = min(GLA_STEP_CHUNKS, s // GLA_CHUNK)
    ct = GLA_CHUNK * n_chunks
    assert s % ct == 0
    row_spec = lambda w: pl.BlockSpec((1, ct, w), lambda b, i: (b, i, 0))
    full = lambda a: pl.BlockSpec(a.shape, lambda b, i: (0,) * a.ndim)
    ng = norm_g.reshape(1, dv)
    return pl.pallas_call(
        functools.partial(_gla_kernel, n_chunks, head_k, head_v),
        out_shape=jax.ShapeDtypeStruct((bsz, s, dv), BF16),
        grid=(bsz, s // ct),
        in_specs=[row_spec(dk)] * 4 + [pl.BlockSpec((1, n_chunks, dk), lambda b, i: (b, i, 0)),
                                       row_spec(dv), row_spec(dv), full(ng)],
        out_specs=row_spec(dv),
        scratch_shapes=[pltpu.VMEM((GLA_HEADS, head_v, head_k), F32)],
        compiler_params=_cparams(("parallel", "arbitrary")),
        name="gla",
    )(q_in, k_in, q_st, k_st, dec, v, r_silu, ng)


def _t5_bucket_np(dist):
    exact = REL_BUCKETS // 2
    d = np.maximum(dist, 1).astype(np.float32)
    large = exact + (np.log(d / np.float32(exact)) / np.float32(math.log(REL_MAX_DIST / exact))
                     * np.float32(REL_BUCKETS - exact)).astype(np.int32)
    large = np.minimum(large, REL_BUCKETS - 1)
    return np.where(dist < exact, dist, large).astype(np.int32)


def _band_tables(window, dilation):
    qi = np.arange(DIL_BLOCK)[:, None]
    kj = np.arange(2 * DIL_BLOCK)[None, :]
    m = qi + DIL_BLOCK - kj
    n_steps = window // dilation
    band = (m >= 0) & (m <= n_steps)
    bucket = _t5_bucket_np(np.clip(m, 0, n_steps) * dilation)
    return np.where(band, bucket, -1).astype(np.int32)


def _attn_kernel(nq, table_ref, bucket_ref, q_ref, kp_ref, kc_ref, vp_ref, vc_ref, o_ref, lse_ref,
                 bias_ref, p_ref):
    i = pl.program_id(1)
    blk = DIL_BLOCK
    hpg = DIL_HEADS_PER_GROUP
    n_pairs = hpg // 2

    @pl.when((pl.program_id(0) == 0) & (i == 0))
    def _():
        bucket = bucket_ref[...]
        for h in range(hpg):
            acc = jnp.full(bucket.shape, NEG, F32)
            for bkt in range(REL_BUCKETS):
                acc = jnp.where(bucket == bkt, table_ref[bkt, h] * LOG2E, acc)
            bias_ref[h * blk:(h + 1) * blk, :] = acc

    lane = lax.broadcasted_iota(jnp.int32, (blk, LANES), 1)
    low = lane < DIL_HEAD_DIM
    ones_rhs = jnp.ones((2 * blk, LANES), BF16)

    def windows(ref_p, ref_c, sq, qb, cols):
        if qb == 0:
            return jnp.concatenate([ref_p[sq, :, cols], ref_c[sq, 0:blk, cols]], axis=0)
        return ref_c[sq, (qb - 1) * blk:(qb + 1) * blk, cols]

    key_lane = lax.broadcasted_iota(jnp.int32, (1, 2 * blk), 1)
    no_prev = jnp.where((key_lane < blk) & (i == 0), NEG, 0.0)
    items = [(sq, qb, hp) for sq in range(q_ref.shape[0]) for qb in range(nq) for hp in range(n_pairs)]

    mxs = []
    for n, (sq, qb, hp) in enumerate(items):
        rows = slice(qb * blk, (qb + 1) * blk)
        cols = slice(hp * LANES, (hp + 1) * LANES)
        qp = q_ref[sq, rows, cols]
        zero = jnp.zeros_like(qp)
        qq = jnp.concatenate([jnp.where(low, qp, zero), jnp.where(low, zero, qp)], axis=0)
        keys = windows(kp_ref, kc_ref, sq, qb, cols)
        s = lax.dot_general(qq, keys, NT_DIMS, preferred_element_type=F32) + bias_ref[2 * hp * blk:(2 * hp + 2) * blk, :]
        if qb == 0:
            s = s + no_prev
        mx = jnp.max(s, axis=-1, keepdims=True)
        p_ref[n * 2 * blk:(n + 1) * 2 * blk, :] = jnp.exp2(s - mx).astype(BF16)
        mxs.append(mx)

    for n, (sq, qb, hp) in enumerate(items):
        rows = slice(qb * blk, (qb + 1) * blk)
        cols = slice(hp * LANES, (hp + 1) * LANES)
        vals = windows(vp_ref, vc_ref, sq, qb, cols)
        rhs = jnp.concatenate([vals, ones_rhs], axis=1)
        res = jnp.dot(p_ref[n * 2 * blk:(n + 1) * 2 * blk, :], rhs, preferred_element_type=F32)
        num = jnp.where(low, res[0:blk, 0:LANES], res[blk:2 * blk, 0:LANES])
        den = jnp.where(low, res[0:blk, LANES:], res[blk:2 * blk, LANES:])
        mx = jnp.where(low, mxs[n][0:blk], mxs[n][blk:2 * blk])
        o_ref[sq, rows, cols] = (num / den).astype(o_ref.dtype)
        lse_ref[sq, rows, cols] = (mx + jnp.log2(den)) * LN2


def _dilated_group_attention(q, k, v, table, window, dilation):
    bb, l, w = q.shape
    nq = min(ATT_STEP_BLOCKS, l // DIL_BLOCK)
    nsq = ATT_STEP_BLOCKS // nq
    assert l % (nq * DIL_BLOCK) == 0 and bb % nsq == 0
    steps = l // (nq * DIL_BLOCK)
    bucket = jnp.asarray(_band_tables(window, dilation))
    cur = pl.BlockSpec((nsq, nq * DIL_BLOCK, w), lambda b, i: (b, i, 0))
    prev = pl.BlockSpec((nsq, DIL_BLOCK, w), lambda b, i: (b, jnp.maximum(nq * i - 1, 0), 0))
    rows_all = nsq * nq * DIL_HEADS_PER_GROUP * DIL_BLOCK
    return pl.pallas_call(
        functools.partial(_attn_kernel, nq),
        out_shape=[jax.ShapeDtypeStruct((bb, l, w), BF16), jax.ShapeDtypeStruct((bb, l, w), F32)],
        grid=(bb // nsq, steps),
        in_specs=[pl.BlockSpec(memory_space=pltpu.SMEM),
                  pl.BlockSpec(bucket.shape, lambda b, i: (0, 0)),
                  cur, prev, cur, prev, cur],
        out_specs=[cur, cur],
        scratch_shapes=[pltpu.VMEM((DIL_HEADS_PER_GROUP * DIL_BLOCK, 2 * DIL_BLOCK), F32),
                        pltpu.VMEM((rows_all, 2 * DIL_BLOCK), BF16)],
        compiler_params=_cparams(("arbitrary", "arbitrary")),
        name=f"dilated_attn_d{dilation}",
    )(table, bucket, q, k, k, v, v)


def _merge_kernel(alpha, dilations, ygla_ref, o0_ref, o1_ref, o2_ref, l0_ref, l1_ref, l2_ref, gg_ref, ga_ref, x_ref,
                  g1_ref, sc2_ref, sh2_ref, ln_g_ref, ln_b_ref, wpg_ref, wpa_ref, wout_ref, wr_ref, br_ref, ltri_ref,
                  x1_ref, u2_ref, route_ref, ew_ref, cnt_ref, stage_ref, carry_ref):
    tm = x_ref.shape[1]

    @pl.when((pl.program_id(0) == 0) & (pl.program_id(1) == 0))
    def _():
        carry_ref[...] = jnp.zeros_like(carry_ref)

    n_lt = DIL_GROUP_WIDTH // LANES
    group_refs = tuple(zip((l0_ref, l1_ref, l2_ref), (o0_ref, o1_ref, o2_ref), dilations))
    for gi, (l_ref, o_ref, dil) in enumerate(group_refs):
        if dil > 1:
            for slot, ref in ((2 * gi, l_ref), (2 * gi + 1, o_ref)):
                for r in range(dil):
                    for t in range(n_lt):
                        stage_ref[slot, t, pl.ds(r, tm // dil, stride=dil), :] = ref[
                            0, r, :, t * LANES:(t + 1) * LANES].astype(F32)

    w_hi = wr_ref[...].astype(BF16)
    sub = ltri_ref.shape[0]
    for rows in (slice(r0, r0 + sub) for r0 in range(0, tm, sub)):
        def natural(ref, dil, slot):
            if dil == 1:
                return ref[0, 0, rows, :].astype(F32)
            return jnp.concatenate([stage_ref[slot, t, rows, :] for t in range(n_lt)], axis=1)

        lses = [natural(l_ref, dil, 2 * gi) for gi, (l_ref, _, dil) in enumerate(group_refs)]
        outs = [natural(o_ref, dil, 2 * gi + 1) for gi, (_, o_ref, dil) in enumerate(group_refs)]
        lm = jnp.maximum(jnp.maximum(lses[0], lses[1]), lses[2])
        es = [jnp.exp(l - lm) for l in lses]
        y_att = (es[0] * outs[0] + es[1] * outs[1] + es[2] * outs[2]) / (es[0] + es[1] + es[2])

        p_gla = jnp.dot(ygla_ref[0, rows, :], wpg_ref[...], preferred_element_type=F32)
        p_att = jnp.dot(y_att.astype(BF16), wpa_ref[...], preferred_element_type=F32)
        merged = gg_ref[0, rows, :].astype(F32) * p_gla + ga_ref[0, rows, :].astype(F32) * p_att
        y = jnp.dot(merged.astype(BF16), wout_ref[...], preferred_element_type=F32)
        x1 = _layer_norm(alpha * x_ref[0, rows, :] + g1_ref[0] * y, ln_g_ref[...], ln_b_ref[...])
        x1_ref[0, rows, :] = x1
        u2 = x1 * (1.0 + sc2_ref[0]) + sh2_ref[0]
        u2_ref[0, rows, :] = _pack_bf16_pairs(u2)

        logits = jnp.dot(u2.astype(BF16), w_hi, preferred_element_type=F32) + br_ref[...]
        lane = lax.broadcasted_iota(jnp.int32, logits.shape, 1)
        big = jnp.int32(LANES)
        lg = jnp.where(lane < MOE_GROUPS, logits, NEG)
        gmax = jnp.max(lg, axis=-1, keepdims=True)
        gidx = jnp.min(jnp.where(lg == gmax, lane, big), axis=-1, keepdims=True)
        gval = 1.0 / jnp.sum(jnp.exp(lg - gmax), axis=-1, keepdims=True)
        in_group = (lane >= MOE_GROUPS + gidx * MOE_EXPERTS) & (lane < MOE_GROUPS + (gidx + 1) * MOE_EXPERTS)
        le = jnp.where(in_group, logits, NEG)
        m1 = jnp.max(le, axis=-1, keepdims=True)
        i1 = jnp.min(jnp.where(le == m1, lane, big), axis=-1, keepdims=True)
        le2 = jnp.where(lane == i1, NEG, le)
        m2 = jnp.max(le2, axis=-1, keepdims=True)
        i2 = jnp.min(jnp.where(le2 == m2, lane, big), axis=-1, keepdims=True)
        t = jnp.exp(m2 - m1)
        w1 = 1.0 / (1.0 + t)
        w2 = t * w1

        hit1, hit2 = lane == i1, lane == i2
        onehot = jnp.where(hit1 | hit2, 1.0, 0.0)
        earlier = jnp.dot(ltri_ref[...], onehot.astype(BF16), preferred_element_type=F32) + carry_ref[...]
        rank1 = jnp.sum(jnp.where(hit1, earlier, 0.0), axis=-1, keepdims=True).astype(jnp.int32)
        rank2 = jnp.sum(jnp.where(hit2, earlier, 0.0), axis=-1, keepdims=True).astype(jnp.int32)
        carry_ref[...] = carry_ref[...] + jnp.sum(onehot, axis=0, keepdims=True)
        route = jnp.where(lane == 0, i1 - MOE_GROUPS, jnp.where(lane == 1, i2 - MOE_GROUPS,
                          jnp.where(lane == 2, rank1, jnp.where(lane == 3, rank2, 0))))
        route_ref[0, :, rows] = jnp.transpose(route)[0:ROUTE_ROWS, :]
        ew_ref[0, rows, :] = jnp.where(lane == 0, gval * w1, jnp.where(lane == 1, gval * w2, 0.0))
    cnt_ref[...] = carry_ref[...].astype(jnp.int32)


def _merge(alpha, y_gla, o_groups, lse_groups, g_gla, g_att, x, g1, sc2, sh2, ln_g, ln_b, wpg, wpa, wout, wr, br):
    bsz, s, d = x.shape
    tm = min(ROW_TILE, s)
    assert s % tm == 0
    dilations = tuple(dil for _, dil in DIL_PATTERNS)
    row = lambda w: pl.BlockSpec((1, tm, w), lambda b, i: (b, i, 0))
    sub = lambda dil: pl.BlockSpec((1, dil, tm // dil, DIL_GROUP_WIDTH), lambda b, i: (b, 0, i, 0))
    per_b = pl.BlockSpec((1, 1, d), lambda b, i: (b, 0, 0))
    full = lambda a: pl.BlockSpec(a.shape, lambda b, i: (0,) * a.ndim)
    ln_g2, ln_b2 = ln_g.reshape(1, d), ln_b.reshape(1, d)
    sub_rows = min(MERGE_SUB_ROWS, tm)
    assert tm % sub_rows == 0
    ltri = jnp.asarray(np.tril(np.ones((sub_rows, sub_rows), np.float32), -1), BF16)
    return pl.pallas_call(
        functools.partial(_merge_kernel, alpha, dilations),
        out_shape=[jax.ShapeDtypeStruct((bsz, s, d), F32), jax.ShapeDtypeStruct((bsz, s, d // 2), jnp.int32),
                   jax.ShapeDtypeStruct((bsz, ROUTE_ROWS, s), jnp.int32), jax.ShapeDtypeStruct((bsz, s, LANES), F32),
                   jax.ShapeDtypeStruct((1, LANES), jnp.int32)],
        grid=(bsz, s // tm),
        in_specs=[row(y_gla.shape[-1])] + [sub(dil) for dil in dilations] * 2
                 + [row(d), row(d), row(d), per_b, per_b, per_b, full(ln_g2), full(ln_b2),
                    full(wpg), full(wpa), full(wout), full(wr), full(br), full(ltri)],
        out_specs=[row(d), row(d // 2), pl.BlockSpec((1, ROUTE_ROWS, tm), lambda b, i: (b, 0, i)), row(LANES),
                   pl.BlockSpec((1, LANES), lambda b, i: (0, 0))],
        scratch_shapes=[pltpu.VMEM((2 * DIL_GROUPS, DIL_GROUP_WIDTH // LANES, tm, LANES), F32),
                        pltpu.VMEM((1, LANES), F32)],
        compiler_params=_cparams(("arbitrary", "arbitrary")),
        name="merge_ln1_router",
    )(y_gla, *o_groups, *lse_groups, g_gla, g_att, x, g1, sc2, sh2, ln_g2, ln_b2, wpg, wpa, wout, wr, br, ltri)


def _expert_kernel(run_ref, valid_ref, rexp_ref, used_ref, x_ref, wg_hbm, wu_hbm, wd_hbm, o_ref,
                   wg_f, wu_f, wd_f, wg_s, wu_s, wd_s, sem):
    t = pl.program_id(0)
    n_tiles_used, n_runs = used_ref[0], used_ref[1]
    run = run_ref[t]
    active = t < n_tiles_used
    first_of_run = (t == 0) | (run_ref[jnp.maximum(t - 1, 0)] != run)

    def weight_copies(r):
        e, slot = rexp_ref[r], r % 2
        return [pltpu.make_async_copy(hbm.at[e], buf.at[slot], sem.at[slot, j])
                for j, (hbm, buf) in enumerate(((wg_hbm, wg_f), (wu_hbm, wu_f), (wd_hbm, wd_f)))]

    @pl.when(active & (t == 0))
    def _():
        for cp in weight_copies(0):
            cp.start()

    @pl.when(active & first_of_run)
    def _():
        @pl.when(run + 1 < n_runs)
        def _():
            for cp in weight_copies(run + 1):
                cp.start()

        for cp in weight_copies(run):
            cp.wait()
        slot = run % 2
        wg_s[...] = wg_f[slot].astype(BF16)
        wu_s[...] = wu_f[slot].astype(BF16)
        wd_s[...] = wd_f[slot].astype(BF16)

    n_valid = jnp.where(active, valid_ref[t], 0)
    def ffn(rows):
        xt = _unpack_bf16_pairs(x_ref[rows, :]).astype(BF16)
        hg = jnp.dot(xt, wg_s[...], preferred_element_type=F32)
        hu = jnp.dot(xt, wu_s[...], preferred_element_type=F32)
        h = (_silu(hg) * hu).astype(BF16)
        o_ref[rows, :] = _pack_bf16_pairs(jnp.dot(h, wd_s[...], preferred_element_type=F32))

    def zero(rows):
        o_ref[rows, :] = jnp.zeros((rows.stop - rows.start, o_ref.shape[1]), o_ref.dtype)

    tm = x_ref.shape[0]
    n_blocks = tm // EXPERT_BLOCK
    for k in range(n_blocks + 1):
        lo, hi = (k - 1) * EXPERT_BLOCK, k * EXPERT_BLOCK

        @pl.when((n_valid > lo) & (n_valid <= hi) if 0 < k < n_blocks else (n_valid > lo if k else n_valid <= 0))
        def _():
            if k:
                ffn(slice(0, hi))
            if k < n_blocks:
                zero(slice(hi, tm))


def _expert_ffn(tile_run, tile_valid, run_expert, used, xg, w_gate, w_up, w_down):
    p = xg.shape[0]
    ne, d, ff = w_gate.shape
    tm = EXPERT_TILE
    n_tiles = p // tm
    hbm = pl.BlockSpec(memory_space=pl.ANY)
    grid_spec = pltpu.PrefetchScalarGridSpec(
        num_scalar_prefetch=4,
        grid=(n_tiles,),
        in_specs=[pl.BlockSpec((tm, d // 2), lambda t, *_: (t, 0)), hbm, hbm, hbm],
        out_specs=pl.BlockSpec((tm, d // 2), lambda t, *_: (t, 0)),
        scratch_shapes=[pltpu.VMEM((2, d, ff), F32), pltpu.VMEM((2, d, ff), F32), pltpu.VMEM((2, ff, d), F32),
                        pltpu.VMEM((d, ff), BF16), pltpu.VMEM((d, ff), BF16), pltpu.VMEM((ff, d), BF16),
                        pltpu.SemaphoreType.DMA((2, 3))],
    )
    return pl.pallas_call(
        _expert_kernel,
        out_shape=jax.ShapeDtypeStruct((p, d // 2), jnp.int32),
        grid_spec=grid_spec,
        compiler_params=_cparams(("arbitrary",)),
        name="expert_ffn",
    )(tile_run, tile_valid, run_expert, used, xg, w_gate, w_up, w_down)


def _final_kernel(alpha, x1_ref, ya_ref, yb_ref, ew_ref, g2_ref, ln_g_ref, ln_b_ref, o_ref):
    ew = ew_ref[0]
    y = ew[:, 0:1] * _unpack_bf16_pairs(ya_ref[0]) + ew[:, 1:2] * _unpack_bf16_pairs(yb_ref[0])
    o_ref[0] = _layer_norm(alpha * x1_ref[0] + g2_ref[0] * y, ln_g_ref[...], ln_b_ref[...])


def _final(alpha, x1, ya, yb, ew, g2, ln_g, ln_b):
    bsz, s, d = x1.shape
    tm = min(ROW_TILE, s)
    row = lambda w: pl.BlockSpec((1, tm, w), lambda b, i: (b, i, 0))
    full = lambda a: pl.BlockSpec(a.shape, lambda b, i: (0,) * a.ndim)
    ln_g2, ln_b2 = ln_g.reshape(1, d), ln_b.reshape(1, d)
    return pl.pallas_call(
        functools.partial(_final_kernel, alpha),
        out_shape=jax.ShapeDtypeStruct((bsz, s, d), F32),
        grid=(bsz, s // tm),
        in_specs=[row(d), row(d // 2), row(d // 2), row(LANES), pl.BlockSpec((1, 1, d), lambda b, i: (b, 0, 0)),
                  full(ln_g2), full(ln_b2)],
        out_specs=row(d),
        compiler_params=_cparams(("parallel", "arbitrary")),
        name="combine_ln2",
    )(x1, ya, yb, ew, g2, ln_g2, ln_b2)


SC_CORES = 2
SC_SUBCORES = 16
SC_CHUNK = 64


def _sc_mesh():
    return plsc.VectorSubcoreMesh(core_axis_name="c", subcore_axis_name="s")


def _sc_scatter_rows(rows, dest0, dest1, n_rows):
    n, w = rows.shape
    n_workers = SC_CORES * SC_SUBCORES
    assert n % (n_workers * SC_CHUNK) == 0
    n_chunks = n // (n_workers * SC_CHUNK)
    d0 = dest0.reshape(n // SC_CHUNK, 1, SC_CHUNK)
    d1 = dest1.reshape(n // SC_CHUNK, 1, SC_CHUNK)

    @functools.partial(
        pl.kernel, mesh=_sc_mesh(), out_type=jax.ShapeDtypeStruct((n_rows, w), rows.dtype),
        scratch_types=[pltpu.VMEM((n_chunks, 1, SC_CHUNK), jnp.int32), pltpu.VMEM((n_chunks, 1, SC_CHUNK), jnp.int32),
                       pltpu.VMEM((2, SC_CHUNK, w), rows.dtype),
                       pltpu.SemaphoreType.DMA((2,)), pltpu.SemaphoreType.DMA((2, 2))])
    def scatter_kernel(rows_hbm, d0_hbm, d1_hbm, out_hbm, i0_v, i1_v, rows_v, read_sem, scat_sem):
        wid = lax.axis_index("s") * SC_CORES + lax.axis_index("c")
        first = wid * n_chunks
        pltpu.sync_copy(d0_hbm.at[pl.ds(first, n_chunks)], i0_v)
        pltpu.sync_copy(d1_hbm.at[pl.ds(first, n_chunks)], i1_v)

        def read(j):
            return pltpu.make_async_copy(rows_hbm.at[pl.ds((first + j) * SC_CHUNK, SC_CHUNK)], rows_v.at[j % 2],
                                         read_sem.at[j % 2])

        def scatters(j):
            return [pltpu.make_async_copy(rows_v.at[j % 2], out_hbm.at[idx.at[j].at[0]], scat_sem.at[j % 2, k])
                    for k, idx in enumerate((i0_v, i1_v))]

        read(0).start()
        for j in range(n_chunks):
            read(j).wait()
            if j + 1 < n_chunks:
                if j >= 1:
                    for cp in scatters(j - 1):
                        cp.wait()
                read(j + 1).start()
            for cp in scatters(j):
                cp.start()
        for j in range(max(n_chunks - 2, 0), n_chunks):
            for cp in scatters(j):
                cp.wait()

    return scatter_kernel(rows, d0, d1)


def _sc_gather_rows(table, dest0, dest1):
    n = dest0.shape[0]
    w = table.shape[1]
    n_workers = SC_CORES * SC_SUBCORES
    assert n % (n_workers * SC_CHUNK) == 0
    n_chunks = n // (n_workers * SC_CHUNK)
    d0 = dest0.reshape(n // SC_CHUNK, 1, SC_CHUNK)
    d1 = dest1.reshape(n // SC_CHUNK, 1, SC_CHUNK)
    out = jax.ShapeDtypeStruct((n, w), table.dtype)

    @functools.partial(
        pl.kernel, mesh=_sc_mesh(), out_type=(out, out),
        scratch_types=[pltpu.VMEM((n_chunks, 1, SC_CHUNK), jnp.int32), pltpu.VMEM((n_chunks, 1, SC_CHUNK), jnp.int32),
                       pltpu.VMEM((2, SC_CHUNK, w), table.dtype),
                       pltpu.SemaphoreType.DMA((2,)), pltpu.SemaphoreType.DMA((2,))])
    def gather_kernel(table_hbm, d0_hbm, d1_hbm, a_hbm, b_hbm, i0_v, i1_v, rows_v, gather_sem, write_sem):
        wid = lax.axis_index("s") * SC_CORES + lax.axis_index("c")
        first = wid * n_chunks
        pltpu.sync_copy(d0_hbm.at[pl.ds(first, n_chunks)], i0_v)
        pltpu.sync_copy(d1_hbm.at[pl.ds(first, n_chunks)], i1_v)
        n_items = 2 * n_chunks

        def gather(m):
            idx = (i0_v, i1_v)[m % 2]
            return pltpu.make_async_copy(table_hbm.at[idx.at[m // 2].at[0]], rows_v.at[m % 2], gather_sem.at[m % 2])

        def write(m):
            o_hbm = (a_hbm, b_hbm)[m % 2]
            return pltpu.make_async_copy(rows_v.at[m % 2], o_hbm.at[pl.ds((first + m // 2) * SC_CHUNK, SC_CHUNK)],
                                         write_sem.at[m % 2])

        gather(0).start()
        for m in range(n_items):
            gather(m).wait()
            if m + 1 < n_items:
                if m >= 1:
                    write(m - 1).wait()
                gather(m + 1).start()
            write(m).start()
        for m in range(max(n_items - 2, 0), n_items):
            write(m).wait()

    return gather_kernel(table, d0, d1)


def _dispatch_plan(route, counts):
    tm = EXPERT_TILE
    e0, e1, r0, r1 = (route[:, j, :].reshape(-1) for j in range(4))
    experts = jnp.arange(MOE_TOTAL, dtype=jnp.int32)
    tiles_per = (counts + tm - 1) // tm
    tile_end = jnp.cumsum(tiles_per)
    pad_start = ((tile_end - tiles_per) * tm).astype(jnp.int32)

    def lookup(e):
        return jnp.sum(jnp.where(e[None, :] == experts[:, None], pad_start[:, None], 0), axis=0)

    dest0, dest1 = lookup(e0) + r0, lookup(e1) + r1
    n_tiles = (2 * e0.size + MOE_TOTAL * tm) // tm
    tile_expert = jnp.minimum(jnp.sum(tile_end[None, :] <= jnp.arange(n_tiles)[:, None], axis=1), MOE_TOTAL - 1)
    nonempty = counts > 0
    run_of_expert = jnp.cumsum(nonempty.astype(jnp.int32)) - 1
    run_expert = jnp.sum(jnp.where(nonempty[None, :] & (run_of_expert[None, :] == experts[:, None]),
                                   experts[None, :], 0), axis=1).astype(jnp.int32)
    of_tile = tile_expert[:, None] == experts[None, :]
    tile_run = jnp.sum(jnp.where(of_tile, run_of_expert[None, :], 0), axis=1).astype(jnp.int32)
    rows_left = (counts + pad_start)[None, :] - jnp.arange(n_tiles)[:, None] * tm
    tile_valid = jnp.clip(jnp.sum(jnp.where(of_tile, rows_left, 0), axis=1), 0, tm).astype(jnp.int32)
    used = jnp.stack([tile_end[-1], jnp.sum(nonempty)]).astype(jnp.int32)
    return dest0, dest1, tile_run, tile_valid, run_expert, used, n_tiles * tm


def _layer(x, c, rel_bias, w_ada, b_ada, w_in, w_gla_gate, b_gla_gate, gla_norm, w_proj_gla, w_proj_attn, w_out,
           ln1_g, ln1_b, w_rg, b_rg, w_re, b_re, w_eg, w_eu, w_ed, ln2_g, ln2_b):
    bsz, s, d = x.shape
    alpha = (2.0 * DEPTH) ** 0.25
    mods = _ada_mods(c, w_ada, b_ada)
    sh1, sc1, g1, sh2, sc2, g2 = [m.reshape(bsz, 1, d) for m in jnp.split(mods, N_MOD, axis=-1)]

    lr0 = d // 2 * 2 + 2 * d
    z = _in_projection(x, sc1, sh1, _prep_in_weight(w_in, lr0), w_gla_gate, b_gla_gate)

    y_gla = _gla(z["q_in"], z["k_in"], z["q_st"], z["k_st"], z["dec"], z["v_gla"], z["r_gla"], gla_norm)

    o_groups, lse_groups = [], []
    for g, (window, dilation) in enumerate(DIL_PATTERNS):
        l = s // dilation
        qg, kg, vg = (z[f"{n}{g}"].reshape(bsz * dilation, l, DIL_GROUP_WIDTH) for n in ("q_att", "k_att", "v_att"))
        table = rel_bias[:, g * DIL_HEADS_PER_GROUP:(g + 1) * DIL_HEADS_PER_GROUP]
        o, lse = _dilated_group_attention(qg, kg, vg, table, window, dilation)
        o_groups.append(o.reshape(bsz, dilation, l, DIL_GROUP_WIDTH))
        lse_groups.append(lse.reshape(bsz, dilation, l, DIL_GROUP_WIDTH))

    wr = jnp.concatenate([w_rg, w_re, jnp.zeros((d, LANES - MOE_GROUPS - MOE_TOTAL), F32)], axis=1)
    br = jnp.concatenate([b_rg, b_re, jnp.zeros((LANES - MOE_GROUPS - MOE_TOTAL,), F32)]).reshape(1, LANES)
    x1, u2, route, ew, cnt = _merge(alpha, y_gla, o_groups, lse_groups, z["g_gla"], z["g_att"], x, g1, sc2, sh2,
                                    ln1_g, ln1_b, w_proj_gla.astype(BF16), w_proj_attn.astype(BF16),
                                    w_out.astype(BF16), wr, br)

    n = bsz * s
    counts = cnt[0, MOE_GROUPS:MOE_GROUPS + MOE_TOTAL]
    dest0, dest1, tile_run, tile_valid, run_expert, used, n_rows = _dispatch_plan(route, counts)
    xg = _sc_scatter_rows(u2.reshape(n, d // 2), dest0, dest1, n_rows)
    ff = w_eg.shape[-1]
    yo = _expert_ffn(tile_run, tile_valid, run_expert, used, xg, w_eg.reshape(MOE_TOTAL, d, ff),
                     w_eu.reshape(MOE_TOTAL, d, ff), w_ed.reshape(MOE_TOTAL, ff, d))
    ya, yb = (y.reshape(bsz, s, d // 2) for y in _sc_gather_rows(yo, dest0, dest1))
    return _final(alpha, x1, ya, yb, ew, g2, ln2_g, ln2_b)


def kernel(x, c, rel_bias, w_ada, b_ada, w_in, w_gla_gate, b_gla_gate, gla_norm, w_proj_gla, w_proj_attn, w_out,
           ln1_g, ln1_b, w_router_group, b_router_group, w_router_expert, b_router_expert, w_exp_gate, w_exp_up,
           w_exp_down, ln2_g, ln2_b):
    assert w_ada.shape[0] == DEPTH
    return _layer(x, c, rel_bias, w_ada[0], b_ada[0], w_in[0:1], w_gla_gate[0], b_gla_gate[0], gla_norm[0],
                  w_proj_gla[0], w_proj_attn[0], w_out[0], ln1_g[0], ln1_b[0], w_router_group[0],
                  b_router_group[0], w_router_expert[0], b_router_expert[0], w_exp_gate[0], w_exp_up[0],
                  w_exp_down[0], ln2_g[0], ln2_b[0])
```

```python
import functools
import math

import numpy as np
import jax
import jax.numpy as jnp
from jax import lax
from jax.experimental import pallas as pl
from jax.experimental.pallas import tpu as pltpu
from jax.experimental.pallas import tpu_sc as plsc

F32 = jnp.float32
BF16 = jnp.bfloat16

N_MOD = 6
GLA_HEADS = 4
GLA_LOWRANK = 16
GLA_TAU = 16.0
GLA_CHUNK = 64
DIL_PATTERNS = ((128, 1), (512, 4), (2048, 16))
DIL_GROUPS = len(DIL_PATTERNS)
DIL_HEADS_PER_GROUP = 8
DIL_HEAD_DIM = 64
DIL_GROUP_WIDTH = DIL_HEADS_PER_GROUP * DIL_HEAD_DIM
DIL_BLOCK = 128
REL_BUCKETS = 32
REL_MAX_DIST = 2048
MOE_GROUPS = 4
MOE_EXPERTS = 8
MOE_TOTAL = MOE_GROUPS * MOE_EXPERTS
LN_EPS = 1e-5
DEPTH = 1

LANES = 128
VMEM_LIMIT = 56 * 1024 * 1024
LOG2E = 1.4426950408889634
LN2 = 0.6931471805599453
NEG = -1e30
ROW_TILE = 512
EXPERT_TILE = 512
EXPERT_BLOCK = 256
GLA_STEP_CHUNKS = 8
ATT_STEP_BLOCKS = 4
MERGE_SUB_ROWS = 512
ROUTE_ROWS = 8

HIGHEST = lax.Precision.HIGHEST
NT_DIMS = (((1,), (1,)), ((), ()))
TN_DIMS = (((0,), (0,)), ((), ()))


def _cparams(sem):
    return pltpu.CompilerParams(dimension_semantics=sem, vmem_limit_bytes=VMEM_LIMIT)


def _sigmoid(x):
    return 0.5 * jnp.tanh(0.5 * x) + 0.5


def _silu(x):
    return x * _sigmoid(x)


def _layer_norm(x, g, b):
    mu = jnp.mean(x, axis=-1, keepdims=True)
    xc = x - mu
    var = jnp.mean(xc * xc, axis=-1, keepdims=True)
    return xc * lax.rsqrt(var + LN_EPS) * g + b


def _pack_bf16_pairs(x):
    w = x.shape[1] // 2
    lo = lax.bitcast_convert_type(x[:, :w].astype(BF16).astype(F32), jnp.uint32) >> 16
    hi = lax.bitcast_convert_type(x[:, w:].astype(BF16).astype(F32), jnp.uint32) & jnp.uint32(0xFFFF0000)
    return lax.bitcast_convert_type(lo | hi, jnp.int32)


def _unpack_bf16_pairs(p):
    u = lax.bitcast_convert_type(p, jnp.uint32)
    lo = lax.bitcast_convert_type(u << 16, F32)
    hi = lax.bitcast_convert_type(u & jnp.uint32(0xFFFF0000), F32)
    return jnp.concatenate([lo, hi], axis=1)


def _split3(x):
    hi = x.astype(BF16)
    r1 = x - hi.astype(F32)
    mid = r1.astype(BF16)
    lo = (r1 - mid.astype(F32)).astype(BF16)
    return hi, mid, lo


def _mods_kernel(c_ref, w_ref, b_ref, o_ref):
    a = _silu(c_ref[...])
    o_ref[...] = jnp.dot(a, w_ref[...], precision=HIGHEST, preferred_element_type=F32) + b_ref[...]


def _ada_mods(c, w, b):
    bsz, d = c.shape
    n = w.shape[1]
    tn = 1536
    assert n % tn == 0
    return pl.pallas_call(
        _mods_kernel,
        out_shape=jax.ShapeDtypeStruct((bsz, n), F32),
        grid=(n // tn,),
        in_specs=[pl.BlockSpec((bsz, d), lambda j: (0, 0)),
                  pl.BlockSpec((d, tn), lambda j: (0, j)),
                  pl.BlockSpec((1, tn), lambda j: (0, j))],
        out_specs=pl.BlockSpec((bsz, tn), lambda j: (0, j)),
        compiler_params=_cparams(("arbitrary",)),
        name="ada_mods",
    )(c, w, b.reshape(1, n))


def _proj_pieces(d_model):
    dk = d_model // 2
    pieces = [("q_gla", dk, "scale_q_gla"), ("k_gla", dk, None), ("v_gla", d_model, None), ("r_gla", d_model, "silu")]
    for name, post in (("q_att", "scale_q_att"), ("k_att", None), ("v_att", None)):
        for g, (_, dilation) in enumerate(DIL_PATTERNS):
            pieces.append((f"{name}{g}", DIL_GROUP_WIDTH, (post, dilation)))
    pieces += [("g_gla", d_model, "sigmoid"), ("g_att", d_model, "sigmoid"), ("lr", LANES, "lowrank")]
    return tuple(pieces)


WT_BLOCK = 512


def _wprep_kernel(n_main_blocks, w_ref, o_ref):
    blk = w_ref[0]
    row = lax.broadcasted_iota(jnp.int32, blk.shape, 0)
    keep = (pl.program_id(0) < n_main_blocks) | (row < GLA_LOWRANK)
    o_ref[...] = jnp.where(keep, blk, 0.0).astype(BF16)


def _prep_in_weight(w_in, lr0):
    w_t = jnp.swapaxes(w_in, 1, 2)
    _, n_in, d = w_t.shape
    n_main = n_in - GLA_LOWRANK
    assert lr0 % WT_BLOCK == 0 and n_main % WT_BLOCK == 0
    n_main_blocks = n_main // WT_BLOCK

    def src_row(j):
        start = j * WT_BLOCK
        octet = jnp.where(j < n_main_blocks, (start + jnp.where(start >= lr0, GLA_LOWRANK, 0)) // 8, lr0 // 8)
        return octet * 8

    return pl.pallas_call(
        functools.partial(_wprep_kernel, n_main_blocks),
        out_shape=jax.ShapeDtypeStruct((n_main + WT_BLOCK, d), BF16),
        grid=(n_main_blocks + 1,),
        in_specs=[pl.BlockSpec((pl.Element(1), pl.Element(WT_BLOCK), pl.Element(d)), lambda j: (0, src_row(j), 0))],
        out_specs=pl.BlockSpec((WT_BLOCK, d), lambda j: (j, 0)),
        compiler_params=_cparams(("parallel",)),
        name="prep_in_weight",
    )(w_t)


GLA_HELD = ("lr", "q_gla", "k_gla")


def _gla_operands(hold, wg_ref, bg_ref, qin_ref, kin_ref, qst_ref, kst_ref, dec_ref):
    c = GLA_CHUNK
    tm = hold["q_gla"].shape[0]
    tril = (lax.broadcasted_iota(jnp.int32, (c, c), 0) >= lax.broadcasted_iota(jnp.int32, (c, c), 1)).astype(BF16)
    mid = c // 2 - 1
    lr, wg = hold["lr"][:, 0:GLA_LOWRANK], wg_ref[...]
    lr_hi, wg_hi = lr.astype(BF16), wg.astype(BF16)
    lr_lo, wg_lo = (lr - lr_hi.astype(F32)).astype(BF16), (wg - wg_hi.astype(F32)).astype(BF16)
    gate_in = (jnp.dot(lr_hi, wg_hi, preferred_element_type=F32) + jnp.dot(lr_lo, wg_hi, preferred_element_type=F32)
               + jnp.dot(lr_hi, wg_lo, preferred_element_type=F32)) + bg_ref[...]
    g_all = (jnp.minimum(gate_in, 0.0) - jnp.log(1.0 + jnp.exp(-jnp.abs(gate_in)))) * (1.0 / GLA_TAU)
    g_hi, g_mid, g_lo = _split3(g_all)
    for ci in range(tm // c):
        rows = slice(ci * c, (ci + 1) * c)
        bc = (jnp.dot(tril, g_hi[rows], preferred_element_type=F32)
              + jnp.dot(tril, g_mid[rows], preferred_element_type=F32)
              + jnp.dot(tril, g_lo[rows], preferred_element_type=F32))
        b_mid = bc[mid:mid + 1, :]
        b_last = bc[c - 1:c, :]
        qf = hold["q_gla"][rows, :]
        kf = hold["k_gla"][rows, :]
        q_in = qf * jnp.exp(bc - b_mid)
        k_in = kf * jnp.exp(b_mid - bc)
        qin_ref[0, rows, :] = q_in.astype(BF16)
        kin_ref[0, rows, :] = k_in.astype(BF16)
        qst_ref[0, rows, :] = (q_in * jnp.exp(b_mid)).astype(BF16)
        kst_ref[0, rows, :] = (k_in * jnp.exp(b_last - b_mid)).astype(BF16)
        dec_ref[0, ci:ci + 1, :] = jnp.exp(b_last)


def _proj_kernel(pieces, head_k, x_ref, sc_ref, sh_ref, w_ref, wg_ref, bg_ref, *refs):
    n_out = len(pieces) - len(GLA_HELD)
    out_refs = dict(zip([p[0] for p in pieces if p[0] not in GLA_HELD], refs[:n_out]))
    gla_out_refs = refs[n_out:n_out + 5]
    stage_ref = refs[n_out + 5]
    hold = dict(zip(GLA_HELD, refs[n_out + 6:]))
    tm = x_ref.shape[1]
    u = (x_ref[0] * (1.0 + sc_ref[0]) + sh_ref[0]).astype(BF16)
    offsets, off = {}, 0
    for name, width, _ in pieces:
        offsets[name] = off
        off += width
    by_name = {p[0]: p for p in pieces}
    held = [(by_name[n], 0) for n in GLA_HELD]
    rest = [(p, c0) for p in pieces if p[0] not in GLA_HELD for c0 in range(0, p[1], min(p[1], 512))]
    for n, (piece, c0) in enumerate(held + rest):
        if n == len(held):
            _gla_operands(hold, wg_ref, bg_ref, *gla_out_refs)
        name, width, post = piece
        o_ref = hold[name] if name in GLA_HELD else out_refs[name]
        off = offsets[name]
        chunk = min(width, 512)
        if True:
            acc = lax.dot_general(u, w_ref[off + c0:off + c0 + chunk, :], NT_DIMS, preferred_element_type=F32)
            if post == "silu":
                acc = _silu(acc)
            elif post == "sigmoid":
                acc = _sigmoid(acc)
            elif post == "scale_q_gla":
                acc = acc * (head_k ** -0.5)
            if name in GLA_HELD:
                o_ref[...] = acc
            elif isinstance(post, tuple):
                scale, dilation = post
                if scale is not None:
                    acc = acc * (DIL_HEAD_DIM ** -0.5 * LOG2E)
                if dilation == 1:
                    o_ref[0, 0] = acc.astype(o_ref.dtype)
                else:
                    for t in range(width // LANES):
                        stage_ref[t] = acc[:, t * LANES:(t + 1) * LANES]
                    for r in range(dilation):
                        for t in range(width // LANES):
                            o_ref[0, r, :, t * LANES:(t + 1) * LANES] = stage_ref[
                                t, pl.ds(r, tm // dilation, stride=dilation), :].astype(o_ref.dtype)
            else:
                o_ref[0, :, c0:c0 + chunk] = acc.astype(o_ref.dtype)


def _in_projection(x, sc1, sh1, w_perm, w_gate, b_gate):
    bsz, s, d = x.shape
    pieces = _proj_pieces(d)
    assert sum(p[1] for p in pieces) <= w_perm.shape[0]
    tm = min(ROW_TILE, s)
    assert s % tm == 0 and tm % (8 * GLA_CHUNK) == 0
    dk = d // 2
    head_k = dk // GLA_HEADS
    out_shape, out_specs = [], []
    for name, width, post in pieces:
        if name in GLA_HELD:
            continue
        if isinstance(post, tuple):
            dil = post[1]
            assert tm % (dil * 16) == 0
            out_shape.append(jax.ShapeDtypeStruct((bsz, dil, s // dil, width), BF16))
            out_specs.append(pl.BlockSpec((1, dil, tm // dil, width), lambda b, i: (b, 0, i, 0)))
        else:
            out_shape.append(jax.ShapeDtypeStruct((bsz, s, width), BF16))
            out_specs.append(pl.BlockSpec((1, tm, width), lambda b, i: (b, i, 0)))
    row = lambda w: pl.BlockSpec((1, tm, w), lambda b, i: (b, i, 0))
    gla_names = ("q_in", "k_in", "q_st", "k_st", "dec")
    out_shape += [jax.ShapeDtypeStruct((bsz, s, dk), BF16)] * 4 + [jax.ShapeDtypeStruct((bsz, s // GLA_CHUNK, dk), F32)]
    out_specs += [row(dk)] * 4 + [pl.BlockSpec((1, tm // GLA_CHUNK, dk), lambda b, i: (b, i, 0))]
    bg = b_gate.reshape(1, dk)
    full = lambda a: pl.BlockSpec(a.shape, lambda b, i: (0,) * a.ndim)
    outs = pl.pallas_call(
        functools.partial(_proj_kernel, pieces, head_k),
        out_shape=out_shape,
        grid=(bsz, s // tm),
        in_specs=[row(d),
                  pl.BlockSpec((1, 1, d), lambda b, i: (b, 0, 0)),
                  pl.BlockSpec((1, 1, d), lambda b, i: (b, 0, 0)),
                  pl.BlockSpec(w_perm.shape, lambda b, i: (0, 0), pipeline_mode=pl.Buffered(1)),
                  full(w_gate), full(bg)],
        out_specs=out_specs,
        scratch_shapes=[pltpu.VMEM((DIL_GROUP_WIDTH // LANES, tm, LANES), F32),
                        pltpu.VMEM((tm, LANES), F32), pltpu.VMEM((tm, dk), F32), pltpu.VMEM((tm, dk), F32)],
        compiler_params=_cparams(("parallel", "arbitrary")),
        name="in_projection",
    )(x, sc1, sh1, w_perm, w_gate, bg)
    return dict(zip([p[0] for p in pieces if p[0] not in GLA_HELD] + list(gla_names), outs))


def _gla_kernel(n_chunks, head_k, head_v, qin_ref, kin_ref, qst_ref, kst_ref, dec_ref, v_ref, r_ref, ng_ref, o_ref,
                state_ref):
    @pl.when(pl.program_id(1) == 0)
    def _():
        state_ref[...] = jnp.zeros_like(state_ref)

    c = GLA_CHUNK
    causal = lax.broadcasted_iota(jnp.int32, (c, c), 0) >= lax.broadcasted_iota(jnp.int32, (c, c), 1)
    for ci in range(n_chunks):
        rows = slice(ci * c, (ci + 1) * c)
        for h in range(GLA_HEADS):
            ks = slice(h * head_k, (h + 1) * head_k)
            vs = slice(h * head_v, (h + 1) * head_v)
            vh = v_ref[0, rows, vs]
            att = lax.dot_general(qin_ref[0, rows, ks], kin_ref[0, rows, ks], NT_DIMS, preferred_element_type=F32)
            att = jnp.where(causal, att, 0.0).astype(BF16)
            st = state_ref[h]
            o = jnp.dot(att, vh, preferred_element_type=F32)
            o = o + lax.dot_general(qst_ref[0, rows, ks], st.astype(BF16), NT_DIMS, preferred_element_type=F32)
            kv_t = lax.dot_general(vh, kst_ref[0, rows, ks], TN_DIMS, preferred_element_type=F32)
            state_ref[h] = st * dec_ref[0, ci:ci + 1, ks] + kv_t
            ms = jnp.mean(o * o, axis=-1, keepdims=True)
            o = o * lax.rsqrt(ms + LN_EPS) * ng_ref[:, vs] * r_ref[0, rows, vs].astype(F32)
            o_ref[0, rows, vs] = o.astype(o_ref.dtype)


def _gla(q_in, k_in, q_st, k_st, dec, v, r_silu, norm_g):
    bsz, s, dk = q_in.shape
    dv = v.shape[-1]
    head_k, head_v = dk // GLA_HEADS, dv // GLA_HEADS
    n_chunks = min(GLA_STEP_CHUNKS, s // GLA_CHUNK)
    ct = GLA_CHUNK * n_chunks
    assert s % ct == 0
    row_spec = lambda w: pl.BlockSpec((1, ct, w), lambda b, i: (b, i, 0))
    full = lambda a: pl.BlockSpec(a.shape, lambda b, i: (0,) * a.ndim)
    ng = norm_g.reshape(1, dv)
    return pl.pallas_call(
        functools.partial(_gla_kernel, n_chunks, head_k, head_v),
        out_shape=jax.ShapeDtypeStruct((bsz, s, dv), BF16),
        grid=(bsz, s // ct),
        in_specs=[row_spec(dk)] * 4 + [pl.BlockSpec((1, n_chunks, dk), lambda b, i: (b, i, 0)),
                                       row_spec(dv), row_spec(dv), full(ng)],
        out_specs=row_spec(dv),
        scratch_shapes=[pltpu.VMEM((GLA_HEADS, head_v, head_k), F32)],
        compiler_params=_cparams(("parallel", "arbitrary")),
        name="gla",
    )(q_in, k_in, q_st, k_st, dec, v, r_silu, ng)


def _t5_bucket_np(dist):
    exact = REL_BUCKETS // 2
    d = np.maximum(dist, 1).astype(np.float32)
    large = exact + (np.log(d / np.float32(exact)) / np.float32(math.log(REL_MAX_DIST / exact))
                     * np.float32(REL_BUCKETS - exact)).astype(np.int32)
    large = np.minimum(large, REL_BUCKETS - 1)
    return np.where(dist < exact, dist, large).astype(np.int32)


def _band_tables(window, dilation):
    qi = np.arange(DIL_BLOCK)[:, None]
    kj = np.arange(2 * DIL_BLOCK)[None, :]
    m = qi + DIL_BLOCK - kj
    n_steps = window // dilation
    band = (m >= 0) & (m <= n_steps)
    bucket = _t5_bucket_np(np.clip(m, 0, n_steps) * dilation)
    return np.where(band, bucket, -1).astype(np.int32)


def _attn_kernel(nq, table_ref, bucket_ref, q_ref, kp_ref, kc_ref, vp_ref, vc_ref, o_ref, lse_ref,
                 bias_ref, p_ref):
    i = pl.program_id(1)
    blk = DIL_BLOCK
    hpg = DIL_HEADS_PER_GROUP
    n_pairs = hpg // 2

    @pl.when((pl.program_id(0) == 0) & (i == 0))
    def _():
        bucket = bucket_ref[...]
        for h in range(hpg):
            acc = jnp.full(bucket.shape, NEG, F32)
            for bkt in range(REL_BUCKETS):
                acc = jnp.where(bucket == bkt, table_ref[bkt, h] * LOG2E, acc)
            bias_ref[h * blk:(h + 1) * blk, :] = acc

    lane = lax.broadcasted_iota(jnp.int32, (blk, LANES), 1)
    low = lane < DIL_HEAD_DIM
    ones_rhs = jnp.ones((2 * blk, LANES), BF16)

    def windows(ref_p, ref_c, sq, qb, cols):
        if qb == 0:
            return jnp.concatenate([ref_p[sq, :, cols], ref_c[sq, 0:blk, cols]], axis=0)
        return ref_c[sq, (qb - 1) * blk:(qb + 1) * blk, cols]

    key_lane = lax.broadcasted_iota(jnp.int32, (1, 2 * blk), 1)
    no_prev = jnp.where((key_lane < blk) & (i == 0), NEG, 0.0)
    items = [(sq, qb, hp) for sq in range(q_ref.shape[0]) for qb in range(nq) for hp in range(n_pairs)]

    mxs = []
    for n, (sq, qb, hp) in enumerate(items):
        rows = slice(qb * blk, (qb + 1) * blk)
        cols = slice(hp * LANES, (hp + 1) * LANES)
        qp = q_ref[sq, rows, cols]
        zero = jnp.zeros_like(qp)
        qq = jnp.concatenate([jnp.where(low, qp, zero), jnp.where(low, zero, qp)], axis=0)
        keys = windows(kp_ref, kc_ref, sq, qb, cols)
        s = lax.dot_general(qq, keys, NT_DIMS, preferred_element_type=F32) + bias_ref[2 * hp * blk:(2 * hp + 2) * blk, :]
        if qb == 0:
            s = s + no_prev
        mx = jnp.max(s, axis=-1, keepdims=True)
        p_ref[n * 2 * blk:(n + 1) * 2 * blk, :] = jnp.exp2(s - mx).astype(BF16)
        mxs.append(mx)

    for n, (sq, qb, hp) in enumerate(items):
        rows = slice(qb * blk, (qb + 1) * blk)
        cols = slice(hp * LANES, (hp + 1) * LANES)
        vals = windows(vp_ref, vc_ref, sq, qb, cols)
        rhs = jnp.concatenate([vals, ones_rhs], axis=1)
        res = jnp.dot(p_ref[n * 2 * blk:(n + 1) * 2 * blk, :], rhs, preferred_element_type=F32)
        num = jnp.where(low, res[0:blk, 0:LANES], res[blk:2 * blk, 0:LANES])
        den = jnp.where(low, res[0:blk, LANES:], res[blk:2 * blk, LANES:])
        mx = jnp.where(low, mxs[n][0:blk], mxs[n][blk:2 * blk])
        o_ref[sq, rows, cols] = (num / den).astype(o_ref.dtype)
        lse_ref[sq, rows, cols] = (mx + jnp.log2(den)) * LN2


def _dilated_group_attention(q, k, v, table, window, dilation):
    bb, l, w = q.shape
    nq = min(ATT_STEP_BLOCKS, l // DIL_BLOCK)
    nsq = ATT_STEP_BLOCKS // nq
    assert l % (nq * DIL_BLOCK) == 0 and bb % nsq == 0
    steps = l // (nq * DIL_BLOCK)
    bucket = jnp.asarray(_band_tables(window, dilation))
    cur = pl.BlockSpec((nsq, nq * DIL_BLOCK, w), lambda b, i: (b, i, 0))
    prev = pl.BlockSpec((nsq, DIL_BLOCK, w), lambda b, i: (b, jnp.maximum(nq * i - 1, 0), 0))
    rows_all = nsq * nq * DIL_HEADS_PER_GROUP * DIL_BLOCK
    return pl.pallas_call(
        functools.partial(_attn_kernel, nq),
        out_shape=[jax.ShapeDtypeStruct((bb, l, w), BF16), jax.ShapeDtypeStruct((bb, l, w), F32)],
        grid=(bb // nsq, steps),
        in_specs=[pl.BlockSpec(memory_space=pltpu.SMEM),
                  pl.BlockSpec(bucket.shape, lambda b, i: (0, 0)),
                  cur, prev, cur, prev, cur],
        out_specs=[cur, cur],
        scratch_shapes=[pltpu.VMEM((DIL_HEADS_PER_GROUP * DIL_BLOCK, 2 * DIL_BLOCK), F32),
                        pltpu.VMEM((rows_all, 2 * DIL_BLOCK), BF16)],
        compiler_params=_cparams(("arbitrary", "arbitrary")),
        name=f"dilated_attn_d{dilation}",
    )(table, bucket, q, k, k, v, v)


def _merge_kernel(alpha, dilations, ygla_ref, o0_ref, o1_ref, o2_ref, l0_ref, l1_ref, l2_ref, gg_ref, ga_ref, x_ref,
                  g1_ref, sc2_ref, sh2_ref, ln_g_ref, ln_b_ref, wpg_ref, wpa_ref, wout_ref, wr_ref, br_ref, ltri_ref,
                  x1_ref, u2_ref, route_ref, ew_ref, cnt_ref, stage_ref, carry_ref):
    tm = x_ref.shape[1]

    @pl.when((pl.program_id(0) == 0) & (pl.program_id(1) == 0))
    def _():
        carry_ref[...] = jnp.zeros_like(carry_ref)

    n_lt = DIL_GROUP_WIDTH // LANES
    group_refs = tuple(zip((l0_ref, l1_ref, l2_ref), (o0_ref, o1_ref, o2_ref), dilations))
    for gi, (l_ref, o_ref, dil) in enumerate(group_refs):
        if dil > 1:
            for slot, ref in ((2 * gi, l_ref), (2 * gi + 1, o_ref)):
                for r in range(dil):
                    for t in range(n_lt):
                        stage_ref[slot, t, pl.ds(r, tm // dil, stride=dil), :] = ref[
                            0, r, :, t * LANES:(t + 1) * LANES].astype(F32)

    w_hi = wr_ref[...].astype(BF16)
    sub = ltri_ref.shape[0]
    for rows in (slice(r0, r0 + sub) for r0 in range(0, tm, sub)):
        def natural(ref, dil, slot):
            if dil == 1:
                return ref[0, 0, rows, :].astype(F32)
            return jnp.concatenate([stage_ref[slot, t, rows, :] for t in range(n_lt)], axis=1)

        lses = [natural(l_ref, dil, 2 * gi) for gi, (l_ref, _, dil) in enumerate(group_refs)]
        outs = [natural(o_ref, dil, 2 * gi + 1) for gi, (_, o_ref, dil) in enumerate(group_refs)]
        lm = jnp.maximum(jnp.maximum(lses[0], lses[1]), lses[2])
        es = [jnp.exp(l - lm) for l in lses]
        y_att = (es[0] * outs[0] + es[1] * outs[1] + es[2] * outs[2]) / (es[0] + es[1] + es[2])

        p_gla = jnp.dot(ygla_ref[0, rows, :], wpg_ref[...], preferred_element_type=F32)
        p_att = jnp.dot(y_att.astype(BF16), wpa_ref[...], preferred_element_type=F32)
        merged = gg_ref[0, rows, :].astype(F32) * p_gla + ga_ref[0, rows, :].astype(F32) * p_att
        y = jnp.dot(merged.astype(BF16), wout_ref[...], preferred_element_type=F32)
        x1 = _layer_norm(alpha * x_ref[0, rows, :] + g1_ref[0] * y, ln_g_ref[...], ln_b_ref[...])
        x1_ref[0, rows, :] = x1
        u2 = x1 * (1.0 + sc2_ref[0]) + sh2_ref[0]
        u2_ref[0, rows, :] = _pack_bf16_pairs(u2)

        logits = jnp.dot(u2.astype(BF16), w_hi, preferred_element_type=F32) + br_ref[...]
        lane = lax.broadcasted_iota(jnp.int32, logits.shape, 1)
        big = jnp.int32(LANES)
        lg = jnp.where(lane < MOE_GROUPS, logits, NEG)
        gmax = jnp.max(lg, axis=-1, keepdims=True)
        gidx = jnp.min(jnp.where(lg == gmax, lane, big), axis=-1, keepdims=True)
        gval = 1.0 / jnp.sum(jnp.exp(lg - gmax), axis=-1, keepdims=True)
        in_group = (lane >= MOE_GROUPS + gidx * MOE_EXPERTS) & (lane < MOE_GROUPS + (gidx + 1) * MOE_EXPERTS)
        le = jnp.where(in_group, logits, NEG)
        m1 = jnp.max(le, axis=-1, keepdims=True)
        i1 = jnp.min(jnp.where(le == m1, lane, big), axis=-1, keepdims=True)
        le2 = jnp.where(lane == i1, NEG, le)
        m2 = jnp.max(le2, axis=-1, keepdims=True)
        i2 = jnp.min(jnp.where(le2 == m2, lane, big), axis=-1, keepdims=True)
        t = jnp.exp(m2 - m1)
        w1 = 1.0 / (1.0 + t)
        w2 = t * w1

        hit1, hit2 = lane == i1, lane == i2
        onehot = jnp.where(hit1 | hit2, 1.0, 0.0)
        earlier = jnp.dot(ltri_ref[...], onehot.astype(BF16), preferred_element_type=F32) + carry_ref[...]
        rank1 = jnp.sum(jnp.where(hit1, earlier, 0.0), axis=-1, keepdims=True).astype(jnp.int32)
        rank2 = jnp.sum(jnp.where(hit2, earlier, 0.0), axis=-1, keepdims=True).astype(jnp.int32)
        carry_ref[...] = carry_ref[...] + jnp.sum(onehot, axis=0, keepdims=True)
        route = jnp.where(lane == 0, i1 - MOE_GROUPS, jnp.where(lane == 1, i2 - MOE_GROUPS,
                          jnp.where(lane == 2, rank1, jnp.where(lane == 3, rank2, 0))))
        route_ref[0, :, rows] = jnp.transpose(route)[0:ROUTE_ROWS, :]
        ew_ref[0, rows, :] = jnp.where(lane == 0, gval * w1, jnp.where(lane == 1, gval * w2, 0.0))
    cnt_ref[...] = carry_ref[...].astype(jnp.int32)


def _merge(alpha, y_gla, o_groups, lse_groups, g_gla, g_att, x, g1, sc2, sh2, ln_g, ln_b, wpg, wpa, wout, wr, br):
    bsz, s, d = x.shape
    tm = min(ROW_TILE, s)
    assert s % tm == 0
    dilations = tuple(dil for _, dil in DIL_PATTERNS)
    row = lambda w: pl.BlockSpec((1, tm, w), lambda b, i: (b, i, 0))
    sub = lambda dil: pl.BlockSpec((1, dil, tm // dil, DIL_GROUP_WIDTH), lambda b, i: (b, 0, i, 0))
    per_b = pl.BlockSpec((1, 1, d), lambda b, i: (b, 0, 0))
    full = lambda a: pl.BlockSpec(a.shape, lambda b, i: (0,) * a.ndim)
    ln_g2, ln_b2 = ln_g.reshape(1, d), ln_b.reshape(1, d)
    sub_rows = min(MERGE_SUB_ROWS, tm)
    assert tm % sub_rows == 0
    ltri = jnp.asarray(np.tril(np.ones((sub_rows, sub_rows), np.float32), -1), BF16)
    return pl.pallas_call(
        functools.partial(_merge_kernel, alpha, dilations),
        out_shape=[jax.ShapeDtypeStruct((bsz, s, d), F32), jax.ShapeDtypeStruct((bsz, s, d // 2), jnp.int32),
                   jax.ShapeDtypeStruct((bsz, ROUTE_ROWS, s), jnp.int32), jax.ShapeDtypeStruct((bsz, s, LANES), F32),
                   jax.ShapeDtypeStruct((1, LANES), jnp.int32)],
        grid=(bsz, s // tm),
        in_specs=[row(y_gla.shape[-1])] + [sub(dil) for dil in dilations] * 2
                 + [row(d), row(d), row(d), per_b, per_b, per_b, full(ln_g2), full(ln_b2),
                    full(wpg), full(wpa), full(wout), full(wr), full(br), full(ltri)],
        out_specs=[row(d), row(d // 2), pl.BlockSpec((1, ROUTE_ROWS, tm), lambda b, i: (b, 0, i)), row(LANES),
                   pl.BlockSpec((1, LANES), lambda b, i: (0, 0))],
        scratch_shapes=[pltpu.VMEM((2 * DIL_GROUPS, DIL_GROUP_WIDTH // LANES, tm, LANES), F32),
                        pltpu.VMEM((1, LANES), F32)],
        compiler_params=_cparams(("arbitrary", "arbitrary")),
        name="merge_ln1_router",
    )(y_gla, *o_groups, *lse_groups, g_gla, g_att, x, g1, sc2, sh2, ln_g2, ln_b2, wpg, wpa, wout, wr, br, ltri)


def _expert_kernel(run_ref, valid_ref, rexp_ref, used_ref, x_ref, wg_hbm, wu_hbm, wd_hbm, o_ref,
                   wg_f, wu_f, wd_f, wg_s, wu_s, wd_s, sem):
    t = pl.program_id(0)
    n_tiles_used, n_runs = used_ref[0], used_ref[1]
    run = run_ref[t]
    active = t < n_tiles_used
    first_of_run = (t == 0) | (run_ref[jnp.maximum(t - 1, 0)] != run)

    def weight_copies(r):
        e, slot = rexp_ref[r], r % 2
        return [pltpu.make_async_copy(hbm.at[e], buf.at[slot], sem.at[slot, j])
                for j, (hbm, buf) in enumerate(((wg_hbm, wg_f), (wu_hbm, wu_f), (wd_hbm, wd_f)))]

    @pl.when(active & (t == 0))
    def _():
        for cp in weight_copies(0):
            cp.start()

    @pl.when(active & first_of_run)
    def _():
        @pl.when(run + 1 < n_runs)
        def _():
            for cp in weight_copies(run + 1):
                cp.start()

        for cp in weight_copies(run):
            cp.wait()
        slot = run % 2
        wg_s[...] = wg_f[slot].astype(BF16)
        wu_s[...] = wu_f[slot].astype(BF16)
        wd_s[...] = wd_f[slot].astype(BF16)

    n_valid = jnp.where(active, valid_ref[t], 0)
    def ffn(rows):
        xt = _unpack_bf16_pairs(x_ref[rows, :]).astype(BF16)
        hg = jnp.dot(xt, wg_s[...], preferred_element_type=F32)
        hu = jnp.dot(xt, wu_s[...], preferred_element_type=F32)
        h = (_silu(hg) * hu).astype(BF16)
        o_ref[rows, :] = _pack_bf16_pairs(jnp.dot(h, wd_s[...], preferred_element_type=F32))

    def zero(rows):
        o_ref[rows, :] = jnp.zeros((rows.stop - rows.start, o_ref.shape[1]), o_ref.dtype)

    tm = x_ref.shape[0]
    n_blocks = tm // EXPERT_BLOCK
    for k in range(n_blocks + 1):
        lo, hi = (k - 1) * EXPERT_BLOCK, k * EXPERT_BLOCK

        @pl.when((n_valid > lo) & (n_valid <= hi) if 0 < k < n_blocks else (n_valid > lo if k else n_valid <= 0))
        def _():
            if k:
                ffn(slice(0, hi))
            if k < n_blocks:
                zero(slice(hi, tm))


def _expert_ffn(tile_run, tile_valid, run_expert, used, xg, w_gate, w_up, w_down):
    p = xg.shape[0]
    ne, d, ff = w_gate.shape
    tm = EXPERT_TILE
    n_tiles = p // tm
    hbm = pl.BlockSpec(memory_space=pl.ANY)
    grid_spec = pltpu.PrefetchScalarGridSpec(
        num_scalar_prefetch=4,
        grid=(n_tiles,),
        in_specs=[pl.BlockSpec((tm, d // 2), lambda t, *_: (t, 0)), hbm, hbm, hbm],
        out_specs=pl.BlockSpec((tm, d // 2), lambda t, *_: (t, 0)),
        scratch_shapes=[pltpu.VMEM((2, d, ff), F32), pltpu.VMEM((2, d, ff), F32), pltpu.VMEM((2, ff, d), F32),
                        pltpu.VMEM((d, ff), BF16), pltpu.VMEM((d, ff), BF16), pltpu.VMEM((ff, d), BF16),
                        pltpu.SemaphoreType.DMA((2, 3))],
    )
    return pl.pallas_call(
        _expert_kernel,
        out_shape=jax.ShapeDtypeStruct((p, d // 2), jnp.int32),
        grid_spec=grid_spec,
        compiler_params=_cparams(("arbitrary",)),
        name="expert_ffn",
    )(tile_run, tile_valid, run_expert, used, xg, w_gate, w_up, w_down)


def _final_kernel(alpha, x1_ref, ya_ref, yb_ref, ew_ref, g2_ref, ln_g_ref, ln_b_ref, o_ref):
    ew = ew_ref[0]
    y = ew[:, 0:1] * _unpack_bf16_pairs(ya_ref[0]) + ew[:, 1:2] * _unpack_bf16_pairs(yb_ref[0])
    o_ref[0] = _layer_norm(alpha * x1_ref[0] + g2_ref[0] * y, ln_g_ref[...], ln_b_ref[...])


def _final(alpha, x1, ya, yb, ew, g2, ln_g, ln_b):
    bsz, s, d = x1.shape
    tm = min(ROW_TILE, s)
    row = lambda w: pl.BlockSpec((1, tm, w), lambda b, i: (b, i, 0))
    full = lambda a: pl.BlockSpec(a.shape, lambda b, i: (0,) * a.ndim)
    ln_g2, ln_b2 = ln_g.reshape(1, d), ln_b.reshape(1, d)
    return pl.pallas_call(
        functools.partial(_final_kernel, alpha),
        out_shape=jax.ShapeDtypeStruct((bsz, s, d), F32),
        grid=(bsz, s // tm),
        in_specs=[row(d), row(d // 2), row(d // 2), row(LANES), pl.BlockSpec((1, 1, d), lambda b, i: (b, 0, 0)),
                  full(ln_g2), full(ln_b2)],
        out_specs=row(d),
        compiler_params=_cparams(("parallel", "arbitrary")),
        name="combine_ln2",
    )(x1, ya, yb, ew, g2, ln_g2, ln_b2)


SC_CORES = 2
SC_SUBCORES = 16
SC_CHUNK = 64


def _sc_mesh():
    return plsc.VectorSubcoreMesh(core_axis_name="c", subcore_axis_name="s")


def _sc_scatter_rows(rows, dest0, dest1, n_rows):
    n, w = rows.shape
    n_workers = SC_CORES * SC_SUBCORES
    assert n % (n_workers * SC_CHUNK) == 0
    n_chunks = n // (n_workers * SC_CHUNK)
    d0 = dest0.reshape(n // SC_CHUNK, 1, SC_CHUNK)
    d1 = dest1.reshape(n // SC_CHUNK, 1, SC_CHUNK)

    @functools.partial(
        pl.kernel, mesh=_sc_mesh(), out_type=jax.ShapeDtypeStruct((n_rows, w), rows.dtype),
        scratch_types=[pltpu.VMEM((n_chunks, 1, SC_CHUNK), jnp.int32), pltpu.VMEM((n_chunks, 1, SC_CHUNK), jnp.int32),
                       pltpu.VMEM((2, SC_CHUNK, w), rows.dtype),
                       pltpu.SemaphoreType.DMA((2,)), pltpu.SemaphoreType.DMA((2, 2))])
    def scatter_kernel(rows_hbm, d0_hbm, d1_hbm, out_hbm, i0_v, i1_v, rows_v, read_sem, scat_sem):
        wid = lax.axis_index("s") * SC_CORES + lax.axis_index("c")
        first = wid * n_chunks
        pltpu.sync_copy(d0_hbm.at[pl.ds(first, n_chunks)], i0_v)
        pltpu.sync_copy(d1_hbm.at[pl.ds(first, n_chunks)], i1_v)

        def read(j):
            return pltpu.make_async_copy(rows_hbm.at[pl.ds((first + j) * SC_CHUNK, SC_CHUNK)], rows_v.at[j % 2],
                                         read_sem.at[j % 2])

        def scatters(j):
            return [pltpu.make_async_copy(rows_v.at[j % 2], out_hbm.at[idx.at[j].at[0]], scat_sem.at[j % 2, k])
                    for k, idx in enumerate((i0_v, i1_v))]

        read(0).start()
        for j in range(n_chunks):
            read(j).wait()
            if j + 1 < n_chunks:
                if j >= 1:
                    for cp in scatters(j - 1):
                        cp.wait()
                read(j + 1).start()
            for cp in scatters(j):
                cp.start()
        for j in range(max(n_chunks - 2, 0), n_chunks):
            for cp in scatters(j):
                cp.wait()

    return scatter_kernel(rows, d0, d1)


def _sc_gather_rows(table, dest0, dest1):
    n = dest0.shape[0]
    w = table.shape[1]
    n_workers = SC_CORES * SC_SUBCORES
    assert n % (n_workers * SC_CHUNK) == 0
    n_chunks = n // (n_workers * SC_CHUNK)
    d0 = dest0.reshape(n // SC_CHUNK, 1, SC_CHUNK)
    d1 = dest1.reshape(n // SC_CHUNK, 1, SC_CHUNK)
    out = jax.ShapeDtypeStruct((n, w), table.dtype)

    @functools.partial(
        pl.kernel, mesh=_sc_mesh(), out_type=(out, out),
        scratch_types=[pltpu.VMEM((n_chunks, 1, SC_CHUNK), jnp.int32), pltpu.VMEM((n_chunks, 1, SC_CHUNK), jnp.int32),
                       pltpu.VMEM((2, SC_CHUNK, w), table.dtype),
                       pltpu.SemaphoreType.DMA((2,)), pltpu.SemaphoreType.DMA((2,))])
    def gather_kernel(table_hbm, d0_hbm, d1_hbm, a_hbm, b_hbm, i0_v, i1_v, rows_v, gather_sem, write_sem):
        wid = lax.axis_index("s") * SC_CORES + lax.axis_index("c")
        first = wid * n_chunks
        pltpu.sync_copy(d0_hbm.at[pl.ds(first, n_chunks)], i0_v)
        pltpu.sync_copy(d1_hbm.at[pl.ds(first, n_chunks)], i1_v)
        n_items = 2 * n_chunks

        def gather(m):
            idx = (i0_v, i1_v)[m % 2]
            return pltpu.make_async_copy(table_hbm.at[idx.at[m // 2].at[0]], rows_v.at[m % 2], gather_sem.at[m % 2])

        def write(m):
            o_hbm = (a_hbm, b_hbm)[m % 2]
            return pltpu.make_async_copy(rows_v.at[m % 2], o_hbm.at[pl.ds((first + m // 2) * SC_CHUNK, SC_CHUNK)],
                                         write_sem.at[m % 2])

        gather(0).start()
        for m in range(n_items):
            gather(m).wait()
            if m + 1 < n_items:
                if m >= 1:
                    write(m - 1).wait()
                gather(m + 1).start()
            write(m).start()
        for m in range(max(n_items - 2, 0), n_items):
            write(m).wait()

    return gather_kernel(table, d0, d1)


def _dispatch_plan(route, counts):
    tm = EXPERT_TILE
    e0, e1, r0, r1 = (route[:, j, :].reshape(-1) for j in range(4))
    experts = jnp.arange(MOE_TOTAL, dtype=jnp.int32)
    tiles_per = (counts + tm - 1) // tm
    tile_end = jnp.cumsum(tiles_per)
    pad_start = ((tile_end - tiles_per) * tm).astype(jnp.int32)

    def lookup(e):
        return jnp.sum(jnp.where(e[None, :] == experts[:, None], pad_start[:, None], 0), axis=0)

    dest0, dest1 = lookup(e0) + r0, lookup(e1) + r1
    n_tiles = (2 * e0.size + MOE_TOTAL * tm) // tm
    tile_expert = jnp.minimum(jnp.sum(tile_end[None, :] <= jnp.arange(n_tiles)[:, None], axis=1), MOE_TOTAL - 1)
    nonempty = counts > 0
    run_of_expert = jnp.cumsum(nonempty.astype(jnp.int32)) - 1
    run_expert = jnp.sum(jnp.where(nonempty[None, :] & (run_of_expert[None, :] == experts[:, None]),
                                   experts[None, :], 0), axis=1).astype(jnp.int32)
    of_tile = tile_expert[:, None] == experts[None, :]
    tile_run = jnp.sum(jnp.where(of_tile, run_of_expert[None, :], 0), axis=1).astype(jnp.int32)
    rows_left = (counts + pad_start)[None, :] - jnp.arange(n_tiles)[:, None] * tm
    tile_valid = jnp.clip(jnp.sum(jnp.where(of_tile, rows_left, 0), axis=1), 0, tm).astype(jnp.int32)
    used = jnp.stack([tile_end[-1], jnp.sum(nonempty)]).astype(jnp.int32)
    return dest0, dest1, tile_run, tile_valid, run_expert, used, n_tiles * tm


def _layer(x, c, rel_bias, w_ada, b_ada, w_in, w_gla_gate, b_gla_gate, gla_norm, w_proj_gla, w_proj_attn, w_out,
           ln1_g, ln1_b, w_rg, b_rg, w_re, b_re, w_eg, w_eu, w_ed, ln2_g, ln2_b):
    bsz, s, d = x.shape
    alpha = (2.0 * DEPTH) ** 0.25
    mods = _ada_mods(c, w_ada, b_ada)
    sh1, sc1, g1, sh2, sc2, g2 = [m.reshape(bsz, 1, d) for m in jnp.split(mods, N_MOD, axis=-1)]

    lr0 = d // 2 * 2 + 2 * d
    z = _in_projection(x, sc1, sh1, _prep_in_weight(w_in, lr0), w_gla_gate, b_gla_gate)

    y_gla = _gla(z["q_in"], z["k_in"], z["q_st"], z["k_st"], z["dec"], z["v_gla"], z["r_gla"], gla_norm)

    o_groups, lse_groups = [], []
    for g, (window, dilation) in enumerate(DIL_PATTERNS):
        l = s // dilation
        qg, kg, vg = (z[f"{n}{g}"].reshape(bsz * dilation, l, DIL_GROUP_WIDTH) for n in ("q_att", "k_att", "v_att"))
        table = rel_bias[:, g * DIL_HEADS_PER_GROUP:(g + 1) * DIL_HEADS_PER_GROUP]
        o, lse = _dilated_group_attention(qg, kg, vg, table, window, dilation)
        o_groups.append(o.reshape(bsz, dilation, l, DIL_GROUP_WIDTH))
        lse_groups.append(lse.reshape(bsz, dilation, l, DIL_GROUP_WIDTH))

    wr = jnp.concatenate([w_rg, w_re, jnp.zeros((d, LANES - MOE_GROUPS - MOE_TOTAL), F32)], axis=1)
    br = jnp.concatenate([b_rg, b_re, jnp.zeros((LANES - MOE_GROUPS - MOE_TOTAL,), F32)]).reshape(1, LANES)
    x1, u2, route, ew, cnt = _merge(alpha, y_gla, o_groups, lse_groups, z["g_gla"], z["g_att"], x, g1, sc2, sh2,
                                    ln1_g, ln1_b, w_proj_gla.astype(BF16), w_proj_attn.astype(BF16),
                                    w_out.astype(BF16), wr, br)

    n = bsz * s
    counts = cnt[0, MOE_GROUPS:MOE_GROUPS + MOE_TOTAL]
    dest0, dest1, tile_run, tile_valid, run_expert, used, n_rows = _dispatch_plan(route, counts)
    xg = _sc_scatter_rows(u2.reshape(n, d // 2), dest0, dest1, n_rows)
    ff = w_eg.shape[-1]
    yo = _expert_ffn(tile_run, tile_valid, run_expert, used, xg, w_eg.reshape(MOE_TOTAL, d, ff),
                     w_eu.reshape(MOE_TOTAL, d, ff), w_ed.reshape(MOE_TOTAL, ff, d))
    ya, yb = (y.reshape(bsz, s, d // 2) for y in _sc_gather_rows(yo, dest0, dest1))
    return _final(alpha, x1, ya, yb, ew, g2, ln2_g, ln2_b)


def kernel(x, c, rel_bias, w_ada, b_ada, w_in, w_gla_gate, b_gla_gate, gla_norm, w_proj_gla, w_proj_attn, w_out,
           ln1_g, ln1_b, w_router_group, b_router_group, w_router_expert, b_router_expert, w_exp_gate, w_exp_up,
           w_exp_down, ln2_g, ln2_b):
    assert w_ada.shape[0] == DEPTH
    return _layer(x, c, rel_bias, w_ada[0], b_ada[0], w_in[0:1], w_gla_gate[0], b_gla_gate[0], gla_norm[0],
                  w_proj_gla[0], w_proj_attn[0], w_out[0], ln1_g[0], ln1_b[0], w_router_group[0],
                  b_router_group[0], w_router_expert[0], b_router_expert[0], w_exp_gate[0], w_exp_up[0],
                  w_exp_down[0], ln2_g[0], ln2_b[0])
```

```python
import functools
import math

import numpy as np
import jax
import jax.numpy as jnp
from jax import lax
from jax.experimental import pallas as pl
from jax.experimental.pallas import tpu as pltpu
from jax.experimental.pallas import tpu_sc as plsc

F32 = jnp.float32
BF16 = jnp.bfloat16

N_MOD = 6
GLA_HEADS = 4
GLA_LOWRANK = 16
GLA_TAU = 16.0
GLA_CHUNK = 64
DIL_PATTERNS = ((128, 1), (512, 4), (2048, 16))
DIL_GROUPS = len(DIL_PATTERNS)
DIL_HEADS_PER_GROUP = 8
DIL_HEAD_DIM = 64
DIL_GROUP_WIDTH = DIL_HEADS_PER_GROUP * DIL_HEAD_DIM
DIL_BLOCK = 128
REL_BUCKETS = 32
REL_MAX_DIST = 2048
MOE_GROUPS = 4
MOE_EXPERTS = 8
MOE_TOTAL = MOE_GROUPS * MOE_EXPERTS
LN_EPS = 1e-5
DEPTH = 1

LANES = 128
VMEM_LIMIT = 56 * 1024 * 1024
LOG2E = 1.4426950408889634
LN2 = 0.6931471805599453
NEG = -1e30
ROW_TILE = 512
EXPERT_TILE = 512
EXPERT_BLOCK = 256
GLA_STEP_CHUNKS = 8
ATT_STEP_BLOCKS = 4
MERGE_SUB_ROWS = 512
ROUTER_ROWS = 40
ROUTE_ROWS = 8

HIGHEST = lax.Precision.HIGHEST
NT_DIMS = (((1,), (1,)), ((), ()))
TN_DIMS = (((0,), (0,)), ((), ()))


def _cparams(sem):
    return pltpu.CompilerParams(dimension_semantics=sem, vmem_limit_bytes=VMEM_LIMIT)


def _sigmoid(x):
    return 0.5 * jnp.tanh(0.5 * x) + 0.5


def _silu(x):
    return x * _sigmoid(x)


def _layer_norm(x, g, b):
    mu = jnp.mean(x, axis=-1, keepdims=True)
    xc = x - mu
    var = jnp.mean(xc * xc, axis=-1, keepdims=True)
    return xc * lax.rsqrt(var + LN_EPS) * g + b


def _pack_bf16_pairs(x):
    w = x.shape[1] // 2
    lo = lax.bitcast_convert_type(x[:, :w].astype(BF16).astype(F32), jnp.uint32) >> 16
    hi = lax.bitcast_convert_type(x[:, w:].astype(BF16).astype(F32), jnp.uint32) & jnp.uint32(0xFFFF0000)
    return lax.bitcast_convert_type(lo | hi, jnp.int32)


def _unpack_bf16_pairs(p):
    u = lax.bitcast_convert_type(p, jnp.uint32)
    lo = lax.bitcast_convert_type(u << 16, F32)
    hi = lax.bitcast_convert_type(u & jnp.uint32(0xFFFF0000), F32)
    return jnp.concatenate([lo, hi], axis=1)


def _split3(x):
    hi = x.astype(BF16)
    r1 = x - hi.astype(F32)
    mid = r1.astype(BF16)
    lo = (r1 - mid.astype(F32)).astype(BF16)
    return hi, mid, lo


def _mods_kernel(c_ref, w_ref, b_ref, o_ref):
    a = _silu(c_ref[...])
    o_ref[...] = jnp.dot(a, w_ref[...], precision=HIGHEST, preferred_element_type=F32) + b_ref[...]


def _ada_mods(c, w, b):
    bsz, d = c.shape
    n = w.shape[1]
    tn = 1536
    assert n % tn == 0
    return pl.pallas_call(
        _mods_kernel,
        out_shape=jax.ShapeDtypeStruct((bsz, n), F32),
        grid=(n // tn,),
        in_specs=[pl.BlockSpec((bsz, d), lambda j: (0, 0)),
                  pl.BlockSpec((d, tn), lambda j: (0, j)),
                  pl.BlockSpec((1, tn), lambda j: (0, j))],
        out_specs=pl.BlockSpec((bsz, tn), lambda j: (0, j)),
        compiler_params=_cparams(("arbitrary",)),
        name="ada_mods",
    )(c, w, b.reshape(1, n))


def _proj_pieces(d_model):
    dk = d_model // 2
    pieces = [("q_gla", dk, "scale_q_gla"), ("k_gla", dk, None), ("v_gla", d_model, None), ("r_gla", d_model, "silu")]
    for name, post in (("q_att", "scale_q_att"), ("k_att", None), ("v_att", None)):
        for g, (_, dilation) in enumerate(DIL_PATTERNS):
            pieces.append((f"{name}{g}", DIL_GROUP_WIDTH, (post, dilation)))
    pieces += [("g_gla", d_model, "sigmoid"), ("g_att", d_model, "sigmoid"), ("lr", LANES, "lowrank")]
    return tuple(pieces)


WT_BLOCK = 512


def _wprep_kernel(n_main_blocks, w_ref, o_ref):
    blk = w_ref[0]
    row = lax.broadcasted_iota(jnp.int32, blk.shape, 0)
    keep = (pl.program_id(0) < n_main_blocks) | (row < GLA_LOWRANK)
    o_ref[...] = jnp.where(keep, blk, 0.0).astype(BF16)


def _prep_in_weight(w_in, lr0):
    w_t = jnp.swapaxes(w_in, 1, 2)
    _, n_in, d = w_t.shape
    n_main = n_in - GLA_LOWRANK
    assert lr0 % WT_BLOCK == 0 and n_main % WT_BLOCK == 0
    n_main_blocks = n_main // WT_BLOCK

    def src_row(j):
        start = j * WT_BLOCK
        octet = jnp.where(j < n_main_blocks, (start + jnp.where(start >= lr0, GLA_LOWRANK, 0)) // 8, lr0 // 8)
        return octet * 8

    return pl.pallas_call(
        functools.partial(_wprep_kernel, n_main_blocks),
        out_shape=jax.ShapeDtypeStruct((n_main + WT_BLOCK, d), BF16),
        grid=(n_main_blocks + 1,),
        in_specs=[pl.BlockSpec((pl.Element(1), pl.Element(WT_BLOCK), pl.Element(d)), lambda j: (0, src_row(j), 0))],
        out_specs=pl.BlockSpec((WT_BLOCK, d), lambda j: (j, 0)),
        compiler_params=_cparams(("parallel",)),
        name="prep_in_weight",
    )(w_t)


GLA_HELD = ("lr", "q_gla", "k_gla")


def _gla_operands(hold, wg_ref, bg_ref, qin_ref, kin_ref, qst_ref, kst_ref, dec_ref):
    c = GLA_CHUNK
    tm = hold["q_gla"].shape[0]
    tril = (lax.broadcasted_iota(jnp.int32, (c, c), 0) >= lax.broadcasted_iota(jnp.int32, (c, c), 1)).astype(BF16)
    mid = c // 2 - 1
    lr, wg = hold["lr"][:, 0:GLA_LOWRANK], wg_ref[...]
    lr_hi, wg_hi = lr.astype(BF16), wg.astype(BF16)
    lr_lo, wg_lo = (lr - lr_hi.astype(F32)).astype(BF16), (wg - wg_hi.astype(F32)).astype(BF16)
    gate_in = (jnp.dot(lr_hi, wg_hi, preferred_element_type=F32) + jnp.dot(lr_lo, wg_hi, preferred_element_type=F32)
               + jnp.dot(lr_hi, wg_lo, preferred_element_type=F32)) + bg_ref[...]
    g_all = (jnp.minimum(gate_in, 0.0) - jnp.log(1.0 + jnp.exp(-jnp.abs(gate_in)))) * (1.0 / GLA_TAU)
    g_hi, g_mid, g_lo = _split3(g_all)
    for ci in range(tm // c):
        rows = slice(ci * c, (ci + 1) * c)
        bc = (jnp.dot(tril, g_hi[rows], preferred_element_type=F32)
              + jnp.dot(tril, g_mid[rows], preferred_element_type=F32)
              + jnp.dot(tril, g_lo[rows], preferred_element_type=F32))
        b_mid = bc[mid:mid + 1, :]
        b_last = bc[c - 1:c, :]
        qf = hold["q_gla"][rows, :]
        kf = hold["k_gla"][rows, :]
        q_in = qf * jnp.exp(bc - b_mid)
        k_in = kf * jnp.exp(b_mid - bc)
        qin_ref[0, rows, :] = q_in.astype(BF16)
        kin_ref[0, rows, :] = k_in.astype(BF16)
        qst_ref[0, rows, :] = (q_in * jnp.exp(b_mid)).astype(BF16)
        kst_ref[0, rows, :] = (k_in * jnp.exp(b_last - b_mid)).astype(BF16)
        dec_ref[0, ci:ci + 1, :] = jnp.exp(b_last)


def _proj_kernel(pieces, head_k, x_ref, sc_ref, sh_ref, w_ref, wg_ref, bg_ref, *refs):
    n_out = len(pieces) - len(GLA_HELD)
    out_refs = dict(zip([p[0] for p in pieces if p[0] not in GLA_HELD], refs[:n_out]))
    gla_out_refs = refs[n_out:n_out + 5]
    stage_ref = refs[n_out + 5]
    hold = dict(zip(GLA_HELD, refs[n_out + 6:]))
    tm = x_ref.shape[1]
    u = (x_ref[0] * (1.0 + sc_ref[0]) + sh_ref[0]).astype(BF16)
    offsets, off = {}, 0
    for name, width, _ in pieces:
        offsets[name] = off
        off += width
    by_name = {p[0]: p for p in pieces}
    held = [(by_name[n], 0) for n in GLA_HELD]
    rest = [(p, c0) for p in pieces if p[0] not in GLA_HELD for c0 in range(0, p[1], min(p[1], 512))]
    for n, (piece, c0) in enumerate(held + rest):
        if n == len(held):
            _gla_operands(hold, wg_ref, bg_ref, *gla_out_refs)
        name, width, post = piece
        o_ref = hold[name] if name in GLA_HELD else out_refs[name]
        off = offsets[name]
        chunk = min(width, 512)
        if True:
            acc = lax.dot_general(u, w_ref[off + c0:off + c0 + chunk, :], NT_DIMS, preferred_element_type=F32)
            if post == "silu":
                acc = _silu(acc)
            elif post == "sigmoid":
                acc = _sigmoid(acc)
            elif post == "scale_q_gla":
                acc = acc * (head_k ** -0.5)
            if name in GLA_HELD:
                o_ref[...] = acc
            elif isinstance(post, tuple):
                scale, dilation = post
                if scale is not None:
                    acc = acc * (DIL_HEAD_DIM ** -0.5 * LOG2E)
                if dilation == 1:
                    o_ref[0, 0] = acc.astype(o_ref.dtype)
                else:
                    for t in range(width // LANES):
                        stage_ref[t] = acc[:, t * LANES:(t + 1) * LANES]
                    for r in range(dilation):
                        for t in range(width // LANES):
                            o_ref[0, r, :, t * LANES:(t + 1) * LANES] = stage_ref[
                                t, pl.ds(r, tm // dilation, stride=dilation), :].astype(o_ref.dtype)
            else:
                o_ref[0, :, c0:c0 + chunk] = acc.astype(o_ref.dtype)


def _in_projection(x, sc1, sh1, w_perm, w_gate, b_gate):
    bsz, s, d = x.shape
    pieces = _proj_pieces(d)
    assert sum(p[1] for p in pieces) <= w_perm.shape[0]
    tm = min(ROW_TILE, s)
    assert s % tm == 0 and tm % (8 * GLA_CHUNK) == 0
    dk = d // 2
    head_k = dk // GLA_HEADS
    out_shape, out_specs = [], []
    for name, width, post in pieces:
        if name in GLA_HELD:
            continue
        if isinstance(post, tuple):
            dil = post[1]
            assert tm % (dil * 16) == 0
            out_shape.append(jax.ShapeDtypeStruct((bsz, dil, s // dil, width), BF16))
            out_specs.append(pl.BlockSpec((1, dil, tm // dil, width), lambda b, i: (b, 0, i, 0)))
        else:
            out_shape.append(jax.ShapeDtypeStruct((bsz, s, width), BF16))
            out_specs.append(pl.BlockSpec((1, tm, width), lambda b, i: (b, i, 0)))
    row = lambda w: pl.BlockSpec((1, tm, w), lambda b, i: (b, i, 0))
    gla_names = ("q_in", "k_in", "q_st", "k_st", "dec")
    out_shape += [jax.ShapeDtypeStruct((bsz, s, dk), BF16)] * 4 + [jax.ShapeDtypeStruct((bsz, s // GLA_CHUNK, dk), F32)]
    out_specs += [row(dk)] * 4 + [pl.BlockSpec((1, tm // GLA_CHUNK, dk), lambda b, i: (b, i, 0))]
    bg = b_gate.reshape(1, dk)
    full = lambda a: pl.BlockSpec(a.shape, lambda b, i: (0,) * a.ndim)
    outs = pl.pallas_call(
        functools.partial(_proj_kernel, pieces, head_k),
        out_shape=out_shape,
        grid=(bsz, s // tm),
        in_specs=[row(d),
                  pl.BlockSpec((1, 1, d), lambda b, i: (b, 0, 0)),
                  pl.BlockSpec((1, 1, d), lambda b, i: (b, 0, 0)),
                  pl.BlockSpec(w_perm.shape, lambda b, i: (0, 0), pipeline_mode=pl.Buffered(1)),
                  full(w_gate), full(bg)],
        out_specs=out_specs,
        scratch_shapes=[pltpu.VMEM((DIL_GROUP_WIDTH // LANES, tm, LANES), F32),
                        pltpu.VMEM((tm, LANES), F32), pltpu.VMEM((tm, dk), F32), pltpu.VMEM((tm, dk), F32)],
        compiler_params=_cparams(("parallel", "arbitrary")),
        name="in_projection",
    )(x, sc1, sh1, w_perm, w_gate, bg)
    return dict(zip([p[0] for p in pieces if p[0] not in GLA_HELD] + list(gla_names), outs))


def _gla_kernel(n_chunks, head_k, head_v, qin_ref, kin_ref, qst_ref, kst_ref, dec_ref, v_ref, r_ref, ng_ref, o_ref,
                state_ref):
    @pl.when(pl.program_id(1) == 0)
    def _():
        state_ref[...] = jnp.zeros_like(state_ref)

    c = GLA_CHUNK
    causal = lax.broadcasted_iota(jnp.int32, (c, c), 0) >= lax.broadcasted_iota(jnp.int32, (c, c), 1)
    for ci in range(n_chunks):
        rows = slice(ci * c, (ci + 1) * c)
        for h in range(GLA_HEADS):
            ks = slice(h * head_k, (h + 1) * head_k)
            vs = slice(h * head_v, (h + 1) * head_v)
            vh = v_ref[0, rows, vs]
            att = lax.dot_general(qin_ref[0, rows, ks], kin_ref[0, rows, ks], NT_DIMS, preferred_element_type=F32)
            att = jnp.where(causal, att, 0.0).astype(BF16)
            st = state_ref[h]
            o = jnp.dot(att, vh, preferred_element_type=F32)
            o = o + lax.dot_general(qst_ref[0, rows, ks], st.astype(BF16), NT_DIMS, preferred_element_type=F32)
            kv_t = lax.dot_general(vh, kst_ref[0, rows, ks], TN_DIMS, preferred_element_type=F32)
            state_ref[h] = st * dec_ref[0, ci:ci + 1, ks] + kv_t
            ms = jnp.mean(o * o, axis=-1, keepdims=True)
            o = o * lax.rsqrt(ms + LN_EPS) * ng_ref[:, vs] * r_ref[0, rows, vs].astype(F32)
            o_ref[0, rows, vs] = o.astype(o_ref.dtype)


def _gla(q_in, k_in, q_st, k_st, dec, v, r_silu, norm_g):
    bsz, s, dk = q_in.shape
    dv = v.shape[-1]
    head_k, head_v = dk // GLA_HEADS, dv // GLA_HEADS
    n_chunks = min(GLA_STEP_CHUNKS, s // GLA_CHUNK)
    ct = GLA_CHUNK * n_chunks
    assert s % ct == 0
    row_spec = lambda w: pl.BlockSpec((1, ct, w), lambda b, i: (b, i, 0))
    full = lambda a: pl.BlockSpec(a.shape, lambda b, i: (0,) * a.ndim)
    ng = norm_g.reshape(1, dv)
    return pl.pallas_call(
        functools.partial(_gla_kernel, n_chunks, head_k, head_v),
        out_shape=jax.ShapeDtypeStruct((bsz, s, dv), BF16),
        grid=(bsz, s // ct),
        in_specs=[row_spec(dk)] * 4 + [pl.BlockSpec((1, n_chunks, dk), lambda b, i: (b, i, 0)),
                                       row_spec(dv), row_spec(dv), full(ng)],
        out_specs=row_spec(dv),
        scratch_shapes=[pltpu.VMEM((GLA_HEADS, head_v, head_k), F32)],
        compiler_params=_cparams(("parallel", "arbitrary")),
        name="gla",
    )(q_in, k_in, q_st, k_st, dec, v, r_silu, ng)


def _t5_bucket_np(dist):
    exact = REL_BUCKETS // 2
    d = np.maximum(dist, 1).astype(np.float32)
    large = exact + (np.log(d / np.float32(exact)) / np.float32(math.log(REL_MAX_DIST / exact))
                     * np.float32(REL_BUCKETS - exact)).astype(np.int32)
    large = np.minimum(large, REL_BUCKETS - 1)
    return np.where(dist < exact, dist, large).astype(np.int32)


def _band_tables(window, dilation):
    qi = np.arange(DIL_BLOCK)[:, None]
    kj = np.arange(2 * DIL_BLOCK)[None, :]
    m = qi + DIL_BLOCK - kj
    n_steps = window // dilation
    band = (m >= 0) & (m <= n_steps)
    bucket = _t5_bucket_np(np.clip(m, 0, n_steps) * dilation)
    return np.where(band, bucket, -1).astype(np.int32)


def _attn_kernel(nq, table_ref, bucket_ref, q_ref, kp_ref, kc_ref, vp_ref, vc_ref, o_ref, lse_ref,
                 bias_ref, p_ref):
    i = pl.program_id(1)
    blk = DIL_BLOCK
    hpg = DIL_HEADS_PER_GROUP
    n_pairs = hpg // 2

    @pl.when((pl.program_id(0) == 0) & (i == 0))
    def _():
        bucket = bucket_ref[...]
        for h in range(hpg):
            acc = jnp.full(bucket.shape, NEG, F32)
            for bkt in range(REL_BUCKETS):
                acc = jnp.where(bucket == bkt, table_ref[bkt, h] * LOG2E, acc)
            bias_ref[h * blk:(h + 1) * blk, :] = acc

    lane = lax.broadcasted_iota(jnp.int32, (blk, LANES), 1)
    low = lane < DIL_HEAD_DIM
    ones_rhs = jnp.ones((2 * blk, LANES), BF16)

    def windows(ref_p, ref_c, sq, qb, cols):
        if qb == 0:
            return jnp.concatenate([ref_p[sq, :, cols], ref_c[sq, 0:blk, cols]], axis=0)
        return ref_c[sq, (qb - 1) * blk:(qb + 1) * blk, cols]

    key_lane = lax.broadcasted_iota(jnp.int32, (1, 2 * blk), 1)
    no_prev = jnp.where((key_lane < blk) & (i == 0), NEG, 0.0)
    items = [(sq, qb, hp) for sq in range(q_ref.shape[0]) for qb in range(nq) for hp in range(n_pairs)]

    mxs = []
    for n, (sq, qb, hp) in enumerate(items):
        rows = slice(qb * blk, (qb + 1) * blk)
        cols = slice(hp * LANES, (hp + 1) * LANES)
        qp = q_ref[sq, rows, cols]
        zero = jnp.zeros_like(qp)
        qq = jnp.concatenate([jnp.where(low, qp, zero), jnp.where(low, zero, qp)], axis=0)
        keys = windows(kp_ref, kc_ref, sq, qb, cols)
        s = lax.dot_general(qq, keys, NT_DIMS, preferred_element_type=F32) + bias_ref[2 * hp * blk:(2 * hp + 2) * blk, :]
        if qb == 0:
            s = s + no_prev
        mx = jnp.max(s, axis=-1, keepdims=True)
        p_ref[n * 2 * blk:(n + 1) * 2 * blk, :] = jnp.exp2(s - mx).astype(BF16)
        mxs.append(mx)

    for n, (sq, qb, hp) in enumerate(items):
        rows = slice(qb * blk, (qb + 1) * blk)
        cols = slice(hp * LANES, (hp + 1) * LANES)
        vals = windows(vp_ref, vc_ref, sq, qb, cols)
        rhs = jnp.concatenate([vals, ones_rhs], axis=1)
        res = jnp.dot(p_ref[n * 2 * blk:(n + 1) * 2 * blk, :], rhs, preferred_element_type=F32)
        num = jnp.where(low, res[0:blk, 0:LANES], res[blk:2 * blk, 0:LANES])
        den = jnp.where(low, res[0:blk, LANES:], res[blk:2 * blk, LANES:])
        mx = jnp.where(low, mxs[n][0:blk], mxs[n][blk:2 * blk])
        o_ref[sq, rows, cols] = (num / den).astype(o_ref.dtype)
        lse_ref[sq, rows, cols] = (mx + jnp.log2(den)) * LN2


def _dilated_group_attention(q, k, v, table, window, dilation):
    bb, l, w = q.shape
    nq = min(ATT_STEP_BLOCKS, l // DIL_BLOCK)
    nsq = ATT_STEP_BLOCKS // nq
    assert l % (nq * DIL_BLOCK) == 0 and bb % nsq == 0
    steps = l // (nq * DIL_BLOCK)
    bucket = jnp.asarray(_band_tables(window, dilation))
    cur = pl.BlockSpec((nsq, nq * DIL_BLOCK, w), lambda b, i: (b, i, 0))
    prev = pl.BlockSpec((nsq, DIL_BLOCK, w), lambda b, i: (b, jnp.maximum(nq * i - 1, 0), 0))
    rows_all = nsq * nq * DIL_HEADS_PER_GROUP * DIL_BLOCK
    return pl.pallas_call(
        functools.partial(_attn_kernel, nq),
        out_shape=[jax.ShapeDtypeStruct((bb, l, w), BF16), jax.ShapeDtypeStruct((bb, l, w), F32)],
        grid=(bb // nsq, steps),
        in_specs=[pl.BlockSpec(memory_space=pltpu.SMEM),
                  pl.BlockSpec(bucket.shape, lambda b, i: (0, 0)),
                  cur, prev, cur, prev, cur],
        out_specs=[cur, cur],
        scratch_shapes=[pltpu.VMEM((DIL_HEADS_PER_GROUP * DIL_BLOCK, 2 * DIL_BLOCK), F32),
                        pltpu.VMEM((rows_all, 2 * DIL_BLOCK), BF16)],
        compiler_params=_cparams(("arbitrary", "arbitrary")),
        name=f"dilated_attn_d{dilation}",
    )(table, bucket, q, k, k, v, v)


def _merge_kernel(alpha, dilations, ygla_ref, o0_ref, o1_ref, o2_ref, l0_ref, l1_ref, l2_ref, gg_ref, ga_ref, x_ref,
                  g1_ref, sc2_ref, sh2_ref, ln_g_ref, ln_b_ref, wpg_ref, wpa_ref, wout_ref, wr_ref, br_ref, utri_ref,
                  x1_ref, u2_ref, route_ref, ew_ref, cnt_ref, stage_ref, carry_ref):
    tm = x_ref.shape[1]

    @pl.when((pl.program_id(0) == 0) & (pl.program_id(1) == 0))
    def _():
        carry_ref[...] = jnp.zeros_like(carry_ref)

    n_lt = DIL_GROUP_WIDTH // LANES
    group_refs = tuple(zip((l0_ref, l1_ref, l2_ref), (o0_ref, o1_ref, o2_ref), dilations))
    for gi, (l_ref, o_ref, dil) in enumerate(group_refs):
        if dil > 1:
            for slot, ref in ((2 * gi, l_ref), (2 * gi + 1, o_ref)):
                for r in range(dil):
                    for t in range(n_lt):
                        stage_ref[slot, t, pl.ds(r, tm // dil, stride=dil), :] = ref[
                            0, r, :, t * LANES:(t + 1) * LANES].astype(F32)

    w_hi = wr_ref[...].astype(BF16)
    sub = utri_ref.shape[0]
    for rows in (slice(r0, r0 + sub) for r0 in range(0, tm, sub)):
        def natural(ref, dil, slot):
            if dil == 1:
                return ref[0, 0, rows, :].astype(F32)
            return jnp.concatenate([stage_ref[slot, t, rows, :] for t in range(n_lt)], axis=1)

        lses = [natural(l_ref, dil, 2 * gi) for gi, (l_ref, _, dil) in enumerate(group_refs)]
        outs = [natural(o_ref, dil, 2 * gi + 1) for gi, (_, o_ref, dil) in enumerate(group_refs)]
        lm = jnp.maximum(jnp.maximum(lses[0], lses[1]), lses[2])
        es = [jnp.exp(l - lm) for l in lses]
        y_att = (es[0] * outs[0] + es[1] * outs[1] + es[2] * outs[2]) / (es[0] + es[1] + es[2])

        p_gla = jnp.dot(ygla_ref[0, rows, :], wpg_ref[...], preferred_element_type=F32)
        p_att = jnp.dot(y_att.astype(BF16), wpa_ref[...], preferred_element_type=F32)
        merged = gg_ref[0, rows, :].astype(F32) * p_gla + ga_ref[0, rows, :].astype(F32) * p_att
        y = jnp.dot(merged.astype(BF16), wout_ref[...], preferred_element_type=F32)
        x1 = _layer_norm(alpha * x_ref[0, rows, :] + g1_ref[0] * y, ln_g_ref[...], ln_b_ref[...])
        x1_ref[0, rows, :] = x1
        u2 = x1 * (1.0 + sc2_ref[0]) + sh2_ref[0]
        u2_ref[0, rows, :] = _pack_bf16_pairs(u2)

        logits = jnp.dot(u2.astype(BF16), w_hi, preferred_element_type=F32) + br_ref[...]
        lt = jnp.transpose(logits)[0:ROUTER_ROWS, :]
        rowi = lax.broadcasted_iota(jnp.int32, lt.shape, 0)
        big = jnp.int32(LANES)
        lg = jnp.where(rowi < MOE_GROUPS, lt, NEG)
        gmax = jnp.max(lg, axis=0, keepdims=True)
        gidx = jnp.min(jnp.where(lg == gmax, rowi, big), axis=0, keepdims=True)
        gval = 1.0 / jnp.sum(jnp.exp(lg - gmax), axis=0, keepdims=True)
        first = MOE_GROUPS + gidx * MOE_EXPERTS
        le = jnp.where((rowi >= first) & (rowi < first + MOE_EXPERTS), lt, NEG)
        m1 = jnp.max(le, axis=0, keepdims=True)
        i1 = jnp.min(jnp.where(le == m1, rowi, big), axis=0, keepdims=True)
        le2 = jnp.where(rowi == i1, NEG, le)
        m2 = jnp.max(le2, axis=0, keepdims=True)
        i2 = jnp.min(jnp.where(le2 == m2, rowi, big), axis=0, keepdims=True)
        t = jnp.exp(m2 - m1)
        w1 = 1.0 / (1.0 + t)
        w2 = t * w1

        hit1, hit2 = rowi == i1, rowi == i2
        onehot = jnp.where(hit1 | hit2, 1.0, 0.0)
        earlier = jnp.dot(onehot.astype(BF16), utri_ref[...], preferred_element_type=F32) + carry_ref[...]
        rank1 = jnp.sum(jnp.where(hit1, earlier, 0.0), axis=0, keepdims=True).astype(jnp.int32)
        rank2 = jnp.sum(jnp.where(hit2, earlier, 0.0), axis=0, keepdims=True).astype(jnp.int32)
        carry_ref[...] = carry_ref[...] + jnp.sum(onehot, axis=1, keepdims=True)
        r8 = lax.broadcasted_iota(jnp.int32, (ROUTE_ROWS, lt.shape[1]), 0)
        route_ref[0, :, rows] = jnp.where(r8 == 0, i1 - MOE_GROUPS, jnp.where(r8 == 1, i2 - MOE_GROUPS,
                                          jnp.where(r8 == 2, rank1, jnp.where(r8 == 3, rank2, 0))))
        r128 = lax.broadcasted_iota(jnp.int32, (LANES, lt.shape[1]), 0)
        ew_ref[0, rows, :] = jnp.transpose(jnp.where(r128 == 0, gval * w1, jnp.where(r128 == 1, gval * w2, 0.0)))
    cnt_ref[...] = jnp.broadcast_to(carry_ref[...], cnt_ref.shape).astype(jnp.int32)


def _merge(alpha, y_gla, o_groups, lse_groups, g_gla, g_att, x, g1, sc2, sh2, ln_g, ln_b, wpg, wpa, wout, wr, br):
    bsz, s, d = x.shape
    tm = min(ROW_TILE, s)
    assert s % tm == 0
    dilations = tuple(dil for _, dil in DIL_PATTERNS)
    row = lambda w: pl.BlockSpec((1, tm, w), lambda b, i: (b, i, 0))
    sub = lambda dil: pl.BlockSpec((1, dil, tm // dil, DIL_GROUP_WIDTH), lambda b, i: (b, 0, i, 0))
    per_b = pl.BlockSpec((1, 1, d), lambda b, i: (b, 0, 0))
    full = lambda a: pl.BlockSpec(a.shape, lambda b, i: (0,) * a.ndim)
    ln_g2, ln_b2 = ln_g.reshape(1, d), ln_b.reshape(1, d)
    sub_rows = min(MERGE_SUB_ROWS, tm)
    assert tm % sub_rows == 0
    utri = jnp.asarray(np.triu(np.ones((sub_rows, sub_rows), np.float32), 1), BF16)
    return pl.pallas_call(
        functools.partial(_merge_kernel, alpha, dilations),
        out_shape=[jax.ShapeDtypeStruct((bsz, s, d), F32), jax.ShapeDtypeStruct((bsz, s, d // 2), jnp.int32),
                   jax.ShapeDtypeStruct((bsz, ROUTE_ROWS, s), jnp.int32), jax.ShapeDtypeStruct((bsz, s, LANES), F32),
                   jax.ShapeDtypeStruct((ROUTER_ROWS, LANES), jnp.int32)],
        grid=(bsz, s // tm),
        in_specs=[row(y_gla.shape[-1])] + [sub(dil) for dil in dilations] * 2
                 + [row(d), row(d), row(d), per_b, per_b, per_b, full(ln_g2), full(ln_b2),
                    full(wpg), full(wpa), full(wout), full(wr), full(br), full(utri)],
        out_specs=[row(d), row(d // 2), pl.BlockSpec((1, ROUTE_ROWS, tm), lambda b, i: (b, 0, i)), row(LANES),
                   pl.BlockSpec((ROUTER_ROWS, LANES), lambda b, i: (0, 0))],
        scratch_shapes=[pltpu.VMEM((2 * DIL_GROUPS, DIL_GROUP_WIDTH // LANES, tm, LANES), F32),
                        pltpu.VMEM((ROUTER_ROWS, 1), F32)],
        compiler_params=_cparams(("arbitrary", "arbitrary")),
        name="merge_ln1_router",
    )(y_gla, *o_groups, *lse_groups, g_gla, g_att, x, g1, sc2, sh2, ln_g2, ln_b2, wpg, wpa, wout, wr, br, utri)


def _expert_kernel(run_ref, valid_ref, rexp_ref, used_ref, x_ref, wg_hbm, wu_hbm, wd_hbm, o_ref,
                   wg_f, wu_f, wd_f, wg_s, wu_s, wd_s, sem):
    t = pl.program_id(0)
    n_tiles_used, n_runs = used_ref[0], used_ref[1]
    run = run_ref[t]
    active = t < n_tiles_used
    first_of_run = (t == 0) | (run_ref[jnp.maximum(t - 1, 0)] != run)

    def weight_copies(r):
        e, slot = rexp_ref[r], r % 2
        return [pltpu.make_async_copy(hbm.at[e], buf.at[slot], sem.at[slot, j])
                for j, (hbm, buf) in enumerate(((wg_hbm, wg_f), (wu_hbm, wu_f), (wd_hbm, wd_f)))]

    @pl.when(active & (t == 0))
    def _():
        for cp in weight_copies(0):
            cp.start()

    @pl.when(active & first_of_run)
    def _():
        @pl.when(run + 1 < n_runs)
        def _():
            for cp in weight_copies(run + 1):
                cp.start()

        for cp in weight_copies(run):
            cp.wait()
        slot = run % 2
        wg_s[...] = wg_f[slot].astype(BF16)
        wu_s[...] = wu_f[slot].astype(BF16)
        wd_s[...] = wd_f[slot].astype(BF16)

    n_valid = jnp.where(active, valid_ref[t], 0)
    def ffn(rows):
        xt = _unpack_bf16_pairs(x_ref[rows, :]).astype(BF16)
        hg = jnp.dot(xt, wg_s[...], preferred_element_type=F32)
        hu = jnp.dot(xt, wu_s[...], preferred_element_type=F32)
        h = (_silu(hg) * hu).astype(BF16)
        o_ref[rows, :] = _pack_bf16_pairs(jnp.dot(h, wd_s[...], preferred_element_type=F32))

    def zero(rows):
        o_ref[rows, :] = jnp.zeros((rows.stop - rows.start, o_ref.shape[1]), o_ref.dtype)

    tm = x_ref.shape[0]
    n_blocks = tm // EXPERT_BLOCK
    for k in range(n_blocks + 1):
        lo, hi = (k - 1) * EXPERT_BLOCK, k * EXPERT_BLOCK

        @pl.when((n_valid > lo) & (n_valid <= hi) if 0 < k < n_blocks else (n_valid > lo if k else n_valid <= 0))
        def _():
            if k:
                ffn(slice(0, hi))
            if k < n_blocks:
                zero(slice(hi, tm))


def _expert_ffn(tile_run, tile_valid, run_expert, used, xg, w_gate, w_up, w_down):
    p = xg.shape[0]
    ne, d, ff = w_gate.shape
    tm = EXPERT_TILE
    n_tiles = p // tm
    hbm = pl.BlockSpec(memory_space=pl.ANY)
    grid_spec = pltpu.PrefetchScalarGridSpec(
        num_scalar_prefetch=4,
        grid=(n_tiles,),
        in_specs=[pl.BlockSpec((tm, d // 2), lambda t, *_: (t, 0)), hbm, hbm, hbm],
        out_specs=pl.BlockSpec((tm, d // 2), lambda t, *_: (t, 0)),
        scratch_shapes=[pltpu.VMEM((2, d, ff), F32), pltpu.VMEM((2, d, ff), F32), pltpu.VMEM((2, ff, d), F32),
                        pltpu.VMEM((d, ff), BF16), pltpu.VMEM((d, ff), BF16), pltpu.VMEM((ff, d), BF16),
                        pltpu.SemaphoreType.DMA((2, 3))],
    )
    return pl.pallas_call(
        _expert_kernel,
        out_shape=jax.ShapeDtypeStruct((p, d // 2), jnp.int32),
        grid_spec=grid_spec,
        compiler_params=_cparams(("arbitrary",)),
        name="expert_ffn",
    )(tile_run, tile_valid, run_expert, used, xg, w_gate, w_up, w_down)


def _final_kernel(alpha, x1_ref, ya_ref, yb_ref, ew_ref, g2_ref, ln_g_ref, ln_b_ref, o_ref):
    ew = ew_ref[0]
    y = ew[:, 0:1] * _unpack_bf16_pairs(ya_ref[0]) + ew[:, 1:2] * _unpack_bf16_pairs(yb_ref[0])
    o_ref[0] = _layer_norm(alpha * x1_ref[0] + g2_ref[0] * y, ln_g_ref[...], ln_b_ref[...])


def _final(alpha, x1, ya, yb, ew, g2, ln_g, ln_b):
    bsz, s, d = x1.shape
    tm = min(ROW_TILE, s)
    row = lambda w: pl.BlockSpec((1, tm, w), lambda b, i: (b, i, 0))
    full = lambda a: pl.BlockSpec(a.shape, lambda b, i: (0,) * a.ndim)
    ln_g2, ln_b2 = ln_g.reshape(1, d), ln_b.reshape(1, d)
    return pl.pallas_call(
        functools.partial(_final_kernel, alpha),
        out_shape=jax.ShapeDtypeStruct((bsz, s, d), F32),
        grid=(bsz, s // tm),
        in_specs=[row(d), row(d // 2), row(d // 2), row(LANES), pl.BlockSpec((1, 1, d), lambda b, i: (b, 0, 0)),
                  full(ln_g2), full(ln_b2)],
        out_specs=row(d),
        compiler_params=_cparams(("parallel", "arbitrary")),
        name="combine_ln2",
    )(x1, ya, yb, ew, g2, ln_g2, ln_b2)


SC_CORES = 2
SC_SUBCORES = 16
SC_CHUNK = 64


def _sc_mesh():
    return plsc.VectorSubcoreMesh(core_axis_name="c", subcore_axis_name="s")


def _sc_scatter_rows(rows, dest0, dest1, n_rows):
    n, w = rows.shape
    n_workers = SC_CORES * SC_SUBCORES
    assert n % (n_workers * SC_CHUNK) == 0
    n_chunks = n // (n_workers * SC_CHUNK)
    d0 = dest0.reshape(n // SC_CHUNK, 1, SC_CHUNK)
    d1 = dest1.reshape(n // SC_CHUNK, 1, SC_CHUNK)

    @functools.partial(
        pl.kernel, mesh=_sc_mesh(), out_type=jax.ShapeDtypeStruct((n_rows, w), rows.dtype),
        scratch_types=[pltpu.VMEM((n_chunks, 1, SC_CHUNK), jnp.int32), pltpu.VMEM((n_chunks, 1, SC_CHUNK), jnp.int32),
                       pltpu.VMEM((2, SC_CHUNK, w), rows.dtype),
                       pltpu.SemaphoreType.DMA((2,)), pltpu.SemaphoreType.DMA((2, 2))])
    def scatter_kernel(rows_hbm, d0_hbm, d1_hbm, out_hbm, i0_v, i1_v, rows_v, read_sem, scat_sem):
        wid = lax.axis_index("s") * SC_CORES + lax.axis_index("c")
        first = wid * n_chunks
        pltpu.sync_copy(d0_hbm.at[pl.ds(first, n_chunks)], i0_v)
        pltpu.sync_copy(d1_hbm.at[pl.ds(first, n_chunks)], i1_v)

        def read(j):
            return pltpu.make_async_copy(rows_hbm.at[pl.ds((first + j) * SC_CHUNK, SC_CHUNK)], rows_v.at[j % 2],
                                         read_sem.at[j % 2])

        def scatters(j):
            return [pltpu.make_async_copy(rows_v.at[j % 2], out_hbm.at[idx.at[j].at[0]], scat_sem.at[j % 2, k])
                    for k, idx in enumerate((i0_v, i1_v))]

        read(0).start()
        for j in range(n_chunks):
            read(j).wait()
            if j + 1 < n_chunks:
                if j >= 1:
                    for cp in scatters(j - 1):
                        cp.wait()
                read(j + 1).start()
            for cp in scatters(j):
                cp.start()
        for j in range(max(n_chunks - 2, 0), n_chunks):
            for cp in scatters(j):
                cp.wait()

    return scatter_kernel(rows, d0, d1)


def _sc_gather_rows(table, dest0, dest1):
    n = dest0.shape[0]
    w = table.shape[1]
    n_workers = SC_CORES * SC_SUBCORES
    assert n % (n_workers * SC_CHUNK) == 0
    n_chunks = n // (n_workers * SC_CHUNK)
    d0 = dest0.reshape(n // SC_CHUNK, 1, SC_CHUNK)
    d1 = dest1.reshape(n // SC_CHUNK, 1, SC_CHUNK)
    out = jax.ShapeDtypeStruct((n, w), table.dtype)

    @functools.partial(
        pl.kernel, mesh=_sc_mesh(), out_type=(out, out),
        scratch_types=[pltpu.VMEM((n_chunks, 1, SC_CHUNK), jnp.int32), pltpu.VMEM((n_chunks, 1, SC_CHUNK), jnp.int32),
                       pltpu.VMEM((2, SC_CHUNK, w), table.dtype),
                       pltpu.SemaphoreType.DMA((2,)), pltpu.SemaphoreType.DMA((2,))])
    def gather_kernel(table_hbm, d0_hbm, d1_hbm, a_hbm, b_hbm, i0_v, i1_v, rows_v, gather_sem, write_sem):
        wid = lax.axis_index("s") * SC_CORES + lax.axis_index("c")
        first = wid * n_chunks
        pltpu.sync_copy(d0_hbm.at[pl.ds(first, n_chunks)], i0_v)
        pltpu.sync_copy(d1_hbm.at[pl.ds(first, n_chunks)], i1_v)
        n_items = 2 * n_chunks

        def gather(m):
            idx = (i0_v, i1_v)[m % 2]
            return pltpu.make_async_copy(table_hbm.at[idx.at[m // 2].at[0]], rows_v.at[m % 2], gather_sem.at[m % 2])

        def write(m):
            o_hbm = (a_hbm, b_hbm)[m % 2]
            return pltpu.make_async_copy(rows_v.at[m % 2], o_hbm.at[pl.ds((first + m // 2) * SC_CHUNK, SC_CHUNK)],
                                         write_sem.at[m % 2])

        gather(0).start()
        for m in range(n_items):
            gather(m).wait()
            if m + 1 < n_items:
                if m >= 1:
                    write(m - 1).wait()
                gather(m + 1).start()
            write(m).start()
        for m in range(max(n_items - 2, 0), n_items):
            write(m).wait()

    return gather_kernel(table, d0, d1)


def _dispatch_plan(route, counts):
    tm = EXPERT_TILE
    e0, e1, r0, r1 = (route[:, j, :].reshape(-1) for j in range(4))
    experts = jnp.arange(MOE_TOTAL, dtype=jnp.int32)
    tiles_per = (counts + tm - 1) // tm
    tile_end = jnp.cumsum(tiles_per)
    pad_start = ((tile_end - tiles_per) * tm).astype(jnp.int32)

    def lookup(e):
        return jnp.sum(jnp.where(e[None, :] == experts[:, None], pad_start[:, None], 0), axis=0)

    dest0, dest1 = lookup(e0) + r0, lookup(e1) + r1
    n_tiles = (2 * e0.size + MOE_TOTAL * tm) // tm
    tile_expert = jnp.minimum(jnp.sum(tile_end[None, :] <= jnp.arange(n_tiles)[:, None], axis=1), MOE_TOTAL - 1)
    nonempty = counts > 0
    run_of_expert = jnp.cumsum(nonempty.astype(jnp.int32)) - 1
    run_expert = jnp.sum(jnp.where(nonempty[None, :] & (run_of_expert[None, :] == experts[:, None]),
                                   experts[None, :], 0), axis=1).astype(jnp.int32)
    of_tile = tile_expert[:, None] == experts[None, :]
    tile_run = jnp.sum(jnp.where(of_tile, run_of_expert[None, :], 0), axis=1).astype(jnp.int32)
    rows_left = (counts + pad_start)[None, :] - jnp.arange(n_tiles)[:, None] * tm
    tile_valid = jnp.clip(jnp.sum(jnp.where(of_tile, rows_left, 0), axis=1), 0, tm).astype(jnp.int32)
    used = jnp.stack([tile_end[-1], jnp.sum(nonempty)]).astype(jnp.int32)
    return dest0, dest1, tile_run, tile_valid, run_expert, used, n_tiles * tm


def _layer(x, c, rel_bias, w_ada, b_ada, w_in, w_gla_gate, b_gla_gate, gla_norm, w_proj_gla, w_proj_attn, w_out,
           ln1_g, ln1_b, w_rg, b_rg, w_re, b_re, w_eg, w_eu, w_ed, ln2_g, ln2_b):
    bsz, s, d = x.shape
    alpha = (2.0 * DEPTH) ** 0.25
    mods = _ada_mods(c, w_ada, b_ada)
    sh1, sc1, g1, sh2, sc2, g2 = [m.reshape(bsz, 1, d) for m in jnp.split(mods, N_MOD, axis=-1)]

    lr0 = d // 2 * 2 + 2 * d
    z = _in_projection(x, sc1, sh1, _prep_in_weight(w_in, lr0), w_gla_gate, b_gla_gate)

    y_gla = _gla(z["q_in"], z["k_in"], z["q_st"], z["k_st"], z["dec"], z["v_gla"], z["r_gla"], gla_norm)

    o_groups, lse_groups = [], []
    for g, (window, dilation) in enumerate(DIL_PATTERNS):
        l = s // dilation
        qg, kg, vg = (z[f"{n}{g}"].reshape(bsz * dilation, l, DIL_GROUP_WIDTH) for n in ("q_att", "k_att", "v_att"))
        table = rel_bias[:, g * DIL_HEADS_PER_GROUP:(g + 1) * DIL_HEADS_PER_GROUP]
        o, lse = _dilated_group_attention(qg, kg, vg, table, window, dilation)
        o_groups.append(o.reshape(bsz, dilation, l, DIL_GROUP_WIDTH))
        lse_groups.append(lse.reshape(bsz, dilation, l, DIL_GROUP_WIDTH))

    wr = jnp.concatenate([w_rg, w_re, jnp.zeros((d, LANES - MOE_GROUPS - MOE_TOTAL), F32)], axis=1)
    br = jnp.concatenate([b_rg, b_re, jnp.zeros((LANES - MOE_GROUPS - MOE_TOTAL,), F32)]).reshape(1, LANES)
    x1, u2, route, ew, cnt = _merge(alpha, y_gla, o_groups, lse_groups, z["g_gla"], z["g_att"], x, g1, sc2, sh2,
                                    ln1_g, ln1_b, w_proj_gla.astype(BF16), w_proj_attn.astype(BF16),
                                    w_out.astype(BF16), wr, br)

    n = bsz * s
    counts = cnt[MOE_GROUPS:MOE_GROUPS + MOE_TOTAL, 0]
    dest0, dest1, tile_run, tile_valid, run_expert, used, n_rows = _dispatch_plan(route, counts)
    xg = _sc_scatter_rows(u2.reshape(n, d // 2), dest0, dest1, n_rows)
    ff = w_eg.shape[-1]
    yo = _expert_ffn(tile_run, tile_valid, run_expert, used, xg, w_eg.reshape(MOE_TOTAL, d, ff),
                     w_eu.reshape(MOE_TOTAL, d, ff), w_ed.reshape(MOE_TOTAL, ff, d))
    ya, yb = (y.reshape(bsz, s, d // 2) for y in _sc_gather_rows(yo, dest0, dest1))
    return _final(alpha, x1, ya, yb, ew, g2, ln2_g, ln2_b)


def kernel(x, c, rel_bias, w_ada, b_ada, w_in, w_gla_gate, b_gla_gate, gla_norm, w_proj_gla, w_proj_attn, w_out,
           ln1_g, ln1_b, w_router_group, b_router_group, w_router_expert, b_router_expert, w_exp_gate, w_exp_up,
           w_exp_down, ln2_g, ln2_b):
    assert w_ada.shape[0] == DEPTH
    return _layer(x, c, rel_bias, w_ada[0], b_ada[0], w_in[0:1], w_gla_gate[0], b_gla_gate[0], gla_norm[0],
                  w_proj_gla[0], w_proj_attn[0], w_out[0], ln1_g[0], ln1_b[0], w_router_group[0],
                  b_router_group[0], w_router_expert[0], b_router_expert[0], w_exp_gate[0], w_exp_up[0],
                  w_exp_down[0], ln2_g[0], ln2_b[0])
```

```python
import functools
import math

import numpy as np
import jax
import jax.numpy as jnp
from jax import lax
from jax.experimental import pallas as pl
from jax.experimental.pallas import tpu as pltpu
from jax.experimental.pallas import tpu_sc as plsc

F32 = jnp.float32
BF16 = jnp.bfloat16

N_MOD = 6
GLA_HEADS = 4
GLA_LOWRANK = 16
GLA_TAU = 16.0
GLA_CHUNK = 64
DIL_PATTERNS = ((128, 1), (512, 4), (2048, 16))
DIL_GROUPS = len(DIL_PATTERNS)
DIL_HEADS_PER_GROUP = 8
DIL_HEAD_DIM = 64
DIL_GROUP_WIDTH = DIL_HEADS_PER_GROUP * DIL_HEAD_DIM
DIL_BLOCK = 128
REL_BUCKETS = 32
REL_MAX_DIST = 2048
MOE_GROUPS = 4
MOE_EXPERTS = 8
MOE_TOTAL = MOE_GROUPS * MOE_EXPERTS
LN_EPS = 1e-5
DEPTH = 1

LANES = 128
VMEM_LIMIT = 56 * 1024 * 1024
LOG2E = 1.4426950408889634
LN2 = 0.6931471805599453
NEG = -1e30
ROW_TILE = 512
FINAL_TILE = 1024
EXPERT_TILE = 512
EXPERT_BLOCK = 128
GLA_STEP_CHUNKS = 8
ATT_STEP_BLOCKS = 4
MERGE_SUB_ROWS = 512
ROUTER_ROWS = 40
ROUTE_ROWS = 8

HIGHEST = lax.Precision.HIGHEST
NT_DIMS = (((1,), (1,)), ((), ()))
TN_DIMS = (((0,), (0,)), ((), ()))


def _cparams(sem):
    return pltpu.CompilerParams(dimension_semantics=sem, vmem_limit_bytes=VMEM_LIMIT)


def _sigmoid(x):
    return 0.5 * jnp.tanh(0.5 * x) + 0.5


def _silu(x):
    return x * _sigmoid(x)


def _layer_norm(x, g, b):
    mu = jnp.mean(x, axis=-1, keepdims=True)
    xc = x - mu
    var = jnp.mean(xc * xc, axis=-1, keepdims=True)
    return xc * lax.rsqrt(var + LN_EPS) * g + b


def _pack_bf16_pairs(x):
    w = x.shape[1] // 2
    lo = lax.bitcast_convert_type(x[:, :w].astype(BF16).astype(F32), jnp.uint32) >> 16
    hi = lax.bitcast_convert_type(x[:, w:].astype(BF16).astype(F32), jnp.uint32) & jnp.uint32(0xFFFF0000)
    return lax.bitcast_convert_type(lo | hi, jnp.int32)


def _unpack_bf16_pairs(p):
    u = lax.bitcast_convert_type(p, jnp.uint32)
    lo = lax.bitcast_convert_type(u << 16, F32)
    hi = lax.bitcast_convert_type(u & jnp.uint32(0xFFFF0000), F32)
    return jnp.concatenate([lo, hi], axis=1)


def _mods_kernel(ct_ref, w_ref, b_ref, o_ref):
    a = _silu(ct_ref[...])
    w = w_ref[...]
    for b in range(a.shape[1]):
        o_ref[b:b + 1, :] = jnp.sum(a[:, b:b + 1] * w, axis=0, keepdims=True) + b_ref[...]


def _ada_mods(c, w, b):
    bsz, d = c.shape
    n = w.shape[1]
    tn = 1536
    assert n % tn == 0
    return pl.pallas_call(
        _mods_kernel,
        out_shape=jax.ShapeDtypeStruct((bsz, n), F32),
        grid=(n // tn,),
        in_specs=[pl.BlockSpec((d, bsz), lambda j: (0, 0)),
                  pl.BlockSpec((d, tn), lambda j: (0, j)),
                  pl.BlockSpec((1, tn), lambda j: (0, j))],
        out_specs=pl.BlockSpec((bsz, tn), lambda j: (0, j)),
        compiler_params=_cparams(("arbitrary",)),
        name="ada_mods",
    )(c.T, w, b.reshape(1, n))


def _proj_pieces(d_model):
    dk = d_model // 2
    pieces = [("q_gla", dk, "scale_q_gla"), ("k_gla", dk, None), ("v_gla", d_model, None), ("r_gla", d_model, "silu")]
    for name, post in (("q_att", "scale_q_att"), ("k_att", None), ("v_att", None)):
        for g, (_, dilation) in enumerate(DIL_PATTERNS):
            pieces.append((f"{name}{g}", DIL_GROUP_WIDTH, (post, dilation)))
    pieces += [("g_gla", d_model, "sigmoid"), ("g_att", d_model, "sigmoid"), ("lr", LANES, "lowrank")]
    return tuple(pieces)


WT_BLOCK = 512


def _wprep_kernel(n_main_blocks, w_ref, o_ref):
    blk = w_ref[0]
    row = lax.broadcasted_iota(jnp.int32, blk.shape, 0)
    keep = (pl.program_id(0) < n_main_blocks) | (row < GLA_LOWRANK)
    o_ref[...] = jnp.where(keep, blk, 0.0).astype(BF16)


def _prep_in_weight(w_in, lr0):
    w_t = jnp.swapaxes(w_in, 1, 2)
    _, n_in, d = w_t.shape
    n_main = n_in - GLA_LOWRANK
    assert lr0 % WT_BLOCK == 0 and n_main % WT_BLOCK == 0
    n_main_blocks = n_main // WT_BLOCK

    def src_row(j):
        start = j * WT_BLOCK
        octet = jnp.where(j < n_main_blocks, (start + jnp.where(start >= lr0, GLA_LOWRANK, 0)) // 8, lr0 // 8)
        return octet * 8

    return pl.pallas_call(
        functools.partial(_wprep_kernel, n_main_blocks),
        out_shape=jax.ShapeDtypeStruct((n_main + WT_BLOCK, d), BF16),
        grid=(n_main_blocks + 1,),
        in_specs=[pl.BlockSpec((pl.Element(1), pl.Element(WT_BLOCK), pl.Element(d)), lambda j: (0, src_row(j), 0))],
        out_specs=pl.BlockSpec((WT_BLOCK, d), lambda j: (j, 0)),
        compiler_params=_cparams(("parallel",)),
        name="prep_in_weight",
    )(w_t)


GLA_HELD = ("lr", "q_gla", "k_gla")


def _gla_operands(hold, wg_ref, bg_ref, qin_ref, kin_ref, qst_ref, kst_ref, dec_ref):
    c = GLA_CHUNK
    tm = hold["q_gla"].shape[0]
    tril = (lax.broadcasted_iota(jnp.int32, (c, c), 0) >= lax.broadcasted_iota(jnp.int32, (c, c), 1)).astype(BF16)
    mid = c // 2 - 1
    lr, wg = hold["lr"][:, 0:GLA_LOWRANK], wg_ref[...]
    lr_hi, wg_hi = lr.astype(BF16), wg.astype(BF16)
    lr_lo, wg_lo = (lr - lr_hi.astype(F32)).astype(BF16), (wg - wg_hi.astype(F32)).astype(BF16)
    gate_in = (jnp.dot(lr_hi, wg_hi, preferred_element_type=F32) + jnp.dot(lr_lo, wg_hi, preferred_element_type=F32)
               + jnp.dot(lr_hi, wg_lo, preferred_element_type=F32)) + bg_ref[...]
    g_all = (jnp.minimum(gate_in, 0.0) - jnp.log(1.0 + jnp.exp(-jnp.abs(gate_in)))) * (1.0 / GLA_TAU)
    g_hi = g_all.astype(BF16)
    g_lo = (g_all - g_hi.astype(F32)).astype(BF16)
    for ci in range(tm // c):
        rows = slice(ci * c, (ci + 1) * c)
        bc = jnp.dot(tril, g_hi[rows], preferred_element_type=F32) + jnp.dot(tril, g_lo[rows], preferred_element_type=F32)
        b_mid = bc[mid:mid + 1, :]
        b_last = bc[c - 1:c, :]
        qf = hold["q_gla"][rows, :]
        kf = hold["k_gla"][rows, :]
        q_in = qf * jnp.exp(bc - b_mid)
        k_in = kf * jnp.exp(b_mid - bc)
        qin_ref[0, rows, :] = q_in.astype(BF16)
        kin_ref[0, rows, :] = k_in.astype(BF16)
        qst_ref[0, rows, :] = (q_in * jnp.exp(b_mid)).astype(BF16)
        kst_ref[0, rows, :] = (k_in * jnp.exp(b_last - b_mid)).astype(BF16)
        dec_ref[0, ci:ci + 1, :] = jnp.exp(b_last)


def _proj_kernel(pieces, head_k, x_ref, sc_ref, sh_ref, w_ref, wg_ref, bg_ref, *refs):
    n_out = len(pieces) - len(GLA_HELD)
    out_refs = dict(zip([p[0] for p in pieces if p[0] not in GLA_HELD], refs[:n_out]))
    gla_out_refs = refs[n_out:n_out + 5]
    stage_ref = refs[n_out + 5]
    hold = dict(zip(GLA_HELD, refs[n_out + 6:]))
    tm = x_ref.shape[1]
    u = (x_ref[0] * (1.0 + sc_ref[0]) + sh_ref[0]).astype(BF16)
    offsets, off = {}, 0
    for name, width, _ in pieces:
        offsets[name] = off
        off += width
    by_name = {p[0]: p for p in pieces}
    held = [(by_name[n], 0) for n in GLA_HELD]
    rest = [(p, c0) for p in pieces if p[0] not in GLA_HELD for c0 in range(0, p[1], min(p[1], 512))]
    for n, (piece, c0) in enumerate(held + rest):
        if n == len(held):
            _gla_operands(hold, wg_ref, bg_ref, *gla_out_refs)
        name, width, post = piece
        o_ref = hold[name] if name in GLA_HELD else out_refs[name]
        off = offsets[name]
        chunk = min(width, 512)
        if True:
            acc = lax.dot_general(u, w_ref[off + c0:off + c0 + chunk, :], NT_DIMS, preferred_element_type=F32)
            if post == "silu":
                acc = _silu(acc)
            elif post == "sigmoid":
                acc = _sigmoid(acc)
            elif post == "scale_q_gla":
                acc = acc * (head_k ** -0.5)
            if name in GLA_HELD:
                o_ref[...] = acc
            elif isinstance(post, tuple):
                scale, dilation = post
                if scale is not None:
                    acc = acc * (DIL_HEAD_DIM ** -0.5 * LOG2E)
                if dilation == 1:
                    o_ref[0, 0] = acc.astype(o_ref.dtype)
                else:
                    for t in range(width // LANES):
                        stage_ref[t] = acc[:, t * LANES:(t + 1) * LANES]
                    for r in range(dilation):
                        for t in range(width // LANES):
                            o_ref[0, r, :, t * LANES:(t + 1) * LANES] = stage_ref[
                                t, pl.ds(r, tm // dilation, stride=dilation), :].astype(o_ref.dtype)
            else:
                o_ref[0, :, c0:c0 + chunk] = acc.astype(o_ref.dtype)


def _in_projection(x, sc1, sh1, w_perm, w_gate, b_gate):
    bsz, s, d = x.shape
    pieces = _proj_pieces(d)
    assert sum(p[1] for p in pieces) <= w_perm.shape[0]
    tm = min(ROW_TILE, s)
    assert s % tm == 0 and tm % (8 * GLA_CHUNK) == 0
    dk = d // 2
    head_k = dk // GLA_HEADS
    out_shape, out_specs = [], []
    for name, width, post in pieces:
        if name in GLA_HELD:
            continue
        if isinstance(post, tuple):
            dil = post[1]
            assert tm % (dil * 16) == 0
            out_shape.append(jax.ShapeDtypeStruct((bsz, dil, s // dil, width), BF16))
            out_specs.append(pl.BlockSpec((1, dil, tm // dil, width), lambda b, i: (b, 0, i, 0)))
        else:
            out_shape.append(jax.ShapeDtypeStruct((bsz, s, width), BF16))
            out_specs.append(pl.BlockSpec((1, tm, width), lambda b, i: (b, i, 0)))
    row = lambda w: pl.BlockSpec((1, tm, w), lambda b, i: (b, i, 0))
    gla_names = ("q_in", "k_in", "q_st", "k_st", "dec")
    out_shape += [jax.ShapeDtypeStruct((bsz, s, dk), BF16)] * 4 + [jax.ShapeDtypeStruct((bsz, s // GLA_CHUNK, dk), F32)]
    out_specs += [row(dk)] * 4 + [pl.BlockSpec((1, tm // GLA_CHUNK, dk), lambda b, i: (b, i, 0))]
    bg = b_gate.reshape(1, dk)
    full = lambda a: pl.BlockSpec(a.shape, lambda b, i: (0,) * a.ndim)
    outs = pl.pallas_call(
        functools.partial(_proj_kernel, pieces, head_k),
        out_shape=out_shape,
        grid=(bsz, s // tm),
        in_specs=[row(d),
                  pl.BlockSpec((1, 1, d), lambda b, i: (b, 0, 0)),
                  pl.BlockSpec((1, 1, d), lambda b, i: (b, 0, 0)),
                  pl.BlockSpec(w_perm.shape, lambda b, i: (0, 0), pipeline_mode=pl.Buffered(1)),
                  full(w_gate), full(bg)],
        out_specs=out_specs,
        scratch_shapes=[pltpu.VMEM((DIL_GROUP_WIDTH // LANES, tm, LANES), F32),
                        pltpu.VMEM((tm, LANES), F32), pltpu.VMEM((tm, dk), F32), pltpu.VMEM((tm, dk), F32)],
        compiler_params=_cparams(("parallel", "arbitrary")),
        name="in_projection",
    )(x, sc1, sh1, w_perm, w_gate, bg)
    return dict(zip([p[0] for p in pieces if p[0] not in GLA_HELD] + list(gla_names), outs))


def _gla_kernel(n_chunks, head_k, head_v, qin_ref, kin_ref, qst_ref, kst_ref, dec_ref, v_ref, r_ref, ng_ref, o_ref,
                state_ref):
    @pl.when(pl.program_id(1) == 0)
    def _():
        state_ref[...] = jnp.zeros_like(state_ref)

    c = GLA_CHUNK
    causal = lax.broadcasted_iota(jnp.int32, (c, c), 0) >= lax.broadcasted_iota(jnp.int32, (c, c), 1)
    for ci in range(n_chunks):
        rows = slice(ci * c, (ci + 1) * c)
        for h in range(GLA_HEADS):
            ks = slice(h * head_k, (h + 1) * head_k)
            vs = slice(h * head_v, (h + 1) * head_v)
            vh = v_ref[0, rows, vs]
            att = lax.dot_general(qin_ref[0, rows, ks], kin_ref[0, rows, ks], NT_DIMS, preferred_element_type=F32)
            att = jnp.where(causal, att, 0.0).astype(BF16)
            st = state_ref[h]
            o = jnp.dot(att, vh, preferred_element_type=F32)
            o = o + lax.dot_general(qst_ref[0, rows, ks], st.astype(BF16), NT_DIMS, preferred_element_type=F32)
            kv_t = lax.dot_general(vh, kst_ref[0, rows, ks], TN_DIMS, preferred_element_type=F32)
            state_ref[h] = st * dec_ref[0, ci:ci + 1, ks] + kv_t
            ms = jnp.mean(o * o, axis=-1, keepdims=True)
            o = o * lax.rsqrt(ms + LN_EPS) * ng_ref[:, vs] * r_ref[0, rows, vs].astype(F32)
            o_ref[0, rows, vs] = o.astype(o_ref.dtype)


def _gla(q_in, k_in, q_st, k_st, dec, v, r_silu, norm_g):
    bsz, s, dk = q_in.shape
    dv = v.shape[-1]
    head_k, head_v = dk // GLA_HEADS, dv // GLA_HEADS
    n_chunks = min(GLA_STEP_CHUNKS, s // GLA_CHUNK)
    ct = GLA_CHUNK * n_chunks
    assert s % ct == 0
    row_spec = lambda w: pl.BlockSpec((1, ct, w), lambda b, i: (b, i, 0))
    full = lambda a: pl.BlockSpec(a.shape, lambda b, i: (0,) * a.ndim)
    ng = norm_g.reshape(1, dv)
    return pl.pallas_call(
        functools.partial(_gla_kernel, n_chunks, head_k, head_v),
        out_shape=jax.ShapeDtypeStruct((bsz, s, dv), BF16),
        grid=(bsz, s // ct),
        in_specs=[row_spec(dk)] * 4 + [pl.BlockSpec((1, n_chunks, dk), lambda b, i: (b, i, 0)),
                                       row_spec(dv), row_spec(dv), full(ng)],
        out_specs=row_spec(dv),
        scratch_shapes=[pltpu.VMEM((GLA_HEADS, head_v, head_k), F32)],
        compiler_params=_cparams(("parallel", "arbitrary")),
        name="gla",
    )(q_in, k_in, q_st, k_st, dec, v, r_silu, ng)


def _t5_bucket_np(dist):
    exact = REL_BUCKETS // 2
    d = np.maximum(dist, 1).astype(np.float32)
    large = exact + (np.log(d / np.float32(exact)) / np.float32(math.log(REL_MAX_DIST / exact))
                     * np.float32(REL_BUCKETS - exact)).astype(np.int32)
    large = np.minimum(large, REL_BUCKETS - 1)
    return np.where(dist < exact, dist, large).astype(np.int32)


def _band_tables(window, dilation):
    qi = np.arange(DIL_BLOCK)[:, None]
    kj = np.arange(2 * DIL_BLOCK)[None, :]
    m = qi + DIL_BLOCK - kj
    n_steps = window // dilation
    band = (m >= 0) & (m <= n_steps)
    bucket = _t5_bucket_np(np.clip(m, 0, n_steps) * dilation)
    return np.where(band, bucket, -1).astype(np.int32)


def _attn_kernel(nq, table_ref, bucket_ref, q_ref, kp_ref, kc_ref, vp_ref, vc_ref, o_ref, lse_ref,
                 bias_ref, p_ref):
    i = pl.program_id(1)
    blk = DIL_BLOCK
    hpg = DIL_HEADS_PER_GROUP
    n_pairs = hpg // 2

    @pl.when((pl.program_id(0) == 0) & (i == 0))
    def _():
        bucket = bucket_ref[...]
        for h in range(hpg):
            acc = jnp.full(bucket.shape, NEG, F32)
            for bkt in range(REL_BUCKETS):
                acc = jnp.where(bucket == bkt, table_ref[bkt, h] * LOG2E, acc)
            bias_ref[h * blk:(h + 1) * blk, :] = acc

    lane = lax.broadcasted_iota(jnp.int32, (blk, LANES), 1)
    low = lane < DIL_HEAD_DIM
    ones_rhs = jnp.ones((2 * blk, LANES), BF16)

    def windows(ref_p, ref_c, sq, qb, cols):
        if qb == 0:
            return jnp.concatenate([ref_p[sq, :, cols], ref_c[sq, 0:blk, cols]], axis=0)
        return ref_c[sq, (qb - 1) * blk:(qb + 1) * blk, cols]

    key_lane = lax.broadcasted_iota(jnp.int32, (1, 2 * blk), 1)
    no_prev = jnp.where((key_lane < blk) & (i == 0), NEG, 0.0)
    items = [(sq, qb, hp) for sq in range(q_ref.shape[0]) for qb in range(nq) for hp in range(n_pairs)]

    mxs = []
    for n, (sq, qb, hp) in enumerate(items):
        rows = slice(qb * blk, (qb + 1) * blk)
        cols = slice(hp * LANES, (hp + 1) * LANES)
        qp = q_ref[sq, rows, cols]
        zero = jnp.zeros_like(qp)
        qq = jnp.concatenate([jnp.where(low, qp, zero), jnp.where(low, zero, qp)], axis=0)
        keys = windows(kp_ref, kc_ref, sq, qb, cols)
        s = lax.dot_general(qq, keys, NT_DIMS, preferred_element_type=F32) + bias_ref[2 * hp * blk:(2 * hp + 2) * blk, :]
        if qb == 0:
            s = s + no_prev
        mx = jnp.max(s, axis=-1, keepdims=True)
        p_ref[n * 2 * blk:(n + 1) * 2 * blk, :] = jnp.exp2(s - mx).astype(BF16)
        mxs.append(mx)

    for n, (sq, qb, hp) in enumerate(items):
        rows = slice(qb * blk, (qb + 1) * blk)
        cols = slice(hp * LANES, (hp + 1) * LANES)
        vals = windows(vp_ref, vc_ref, sq, qb, cols)
        rhs = jnp.concatenate([vals, ones_rhs], axis=1)
        res = jnp.dot(p_ref[n * 2 * blk:(n + 1) * 2 * blk, :], rhs, preferred_element_type=F32)
        num = jnp.where(low, res[0:blk, 0:LANES], res[blk:2 * blk, 0:LANES])
        den = jnp.where(low, res[0:blk, LANES:], res[blk:2 * blk, LANES:])
        mx = jnp.where(low, mxs[n][0:blk], mxs[n][blk:2 * blk])
        o_ref[sq, rows, cols] = (num / den).astype(o_ref.dtype)
        lse_ref[sq, rows, cols] = (mx + jnp.log2(den)) * LN2


def _dilated_group_attention(q, k, v, table, window, dilation):
    bb, l, w = q.shape
    nq = min(ATT_STEP_BLOCKS, l // DIL_BLOCK)
    nsq = ATT_STEP_BLOCKS // nq
    assert l % (nq * DIL_BLOCK) == 0 and bb % nsq == 0
    steps = l // (nq * DIL_BLOCK)
    bucket = jnp.asarray(_band_tables(window, dilation))
    cur = pl.BlockSpec((nsq, nq * DIL_BLOCK, w), lambda b, i: (b, i, 0))
    prev = pl.BlockSpec((nsq, DIL_BLOCK, w), lambda b, i: (b, jnp.maximum(nq * i - 1, 0), 0))
    rows_all = nsq * nq * DIL_HEADS_PER_GROUP * DIL_BLOCK
    return pl.pallas_call(
        functools.partial(_attn_kernel, nq),
        out_shape=[jax.ShapeDtypeStruct((bb, l, w), BF16), jax.ShapeDtypeStruct((bb, l, w), F32)],
        grid=(bb // nsq, steps),
        in_specs=[pl.BlockSpec(memory_space=pltpu.SMEM),
                  pl.BlockSpec(bucket.shape, lambda b, i: (0, 0)),
                  cur, prev, cur, prev, cur],
        out_specs=[cur, cur],
        scratch_shapes=[pltpu.VMEM((DIL_HEADS_PER_GROUP * DIL_BLOCK, 2 * DIL_BLOCK), F32),
                        pltpu.VMEM((rows_all, 2 * DIL_BLOCK), BF16)],
        compiler_params=_cparams(("arbitrary", "arbitrary")),
        name=f"dilated_attn_d{dilation}",
    )(table, bucket, q, k, k, v, v)


def _merge_kernel(alpha, dilations, ygla_ref, o0_ref, o1_ref, o2_ref, l0_ref, l1_ref, l2_ref, gg_ref, ga_ref, x_ref,
                  g1_ref, sc2_ref, sh2_ref, ln_g_ref, ln_b_ref, wpg_ref, wpa_ref, wout_ref, wr_ref, br_ref, utri_ref,
                  x1_ref, u2_ref, route_ref, ew_ref, cnt_ref, stage_ref, carry_ref):
    tm = x_ref.shape[1]

    @pl.when((pl.program_id(0) == 0) & (pl.program_id(1) == 0))
    def _():
        carry_ref[...] = jnp.zeros_like(carry_ref)

    n_lt = DIL_GROUP_WIDTH // LANES
    group_refs = tuple(zip((l0_ref, l1_ref, l2_ref), (o0_ref, o1_ref, o2_ref), dilations))
    for gi, (l_ref, o_ref, dil) in enumerate(group_refs):
        if dil > 1:
            for slot, ref in ((2 * gi, l_ref), (2 * gi + 1, o_ref)):
                for r in range(dil):
                    for t in range(n_lt):
                        stage_ref[slot, t, pl.ds(r, tm // dil, stride=dil), :] = ref[
                            0, r, :, t * LANES:(t + 1) * LANES].astype(F32)

    w_hi = wr_ref[...].astype(BF16)
    sub = utri_ref.shape[0]
    for rows in (slice(r0, r0 + sub) for r0 in range(0, tm, sub)):
        def natural(ref, dil, slot):
            if dil == 1:
                return ref[0, 0, rows, :].astype(F32)
            return jnp.concatenate([stage_ref[slot, t, rows, :] for t in range(n_lt)], axis=1)

        lses = [natural(l_ref, dil, 2 * gi) for gi, (l_ref, _, dil) in enumerate(group_refs)]
        outs = [natural(o_ref, dil, 2 * gi + 1) for gi, (_, o_ref, dil) in enumerate(group_refs)]
        lm = jnp.maximum(jnp.maximum(lses[0], lses[1]), lses[2])
        es = [jnp.exp(l - lm) for l in lses]
        y_att = (es[0] * outs[0] + es[1] * outs[1] + es[2] * outs[2]) / (es[0] + es[1] + es[2])

        p_gla = jnp.dot(ygla_ref[0, rows, :], wpg_ref[...], preferred_element_type=F32)
        p_att = jnp.dot(y_att.astype(BF16), wpa_ref[...], preferred_element_type=F32)
        merged = gg_ref[0, rows, :].astype(F32) * p_gla + ga_ref[0, rows, :].astype(F32) * p_att
        y = jnp.dot(merged.astype(BF16), wout_ref[...], preferred_element_type=F32)
        x1 = _layer_norm(alpha * x_ref[0, rows, :] + g1_ref[0] * y, ln_g_ref[...], ln_b_ref[...])
        x1_ref[0, rows, :] = x1
        u2 = x1 * (1.0 + sc2_ref[0]) + sh2_ref[0]
        u2_ref[0, rows, :] = _pack_bf16_pairs(u2)

        logits = jnp.dot(u2.astype(BF16), w_hi, preferred_element_type=F32) + br_ref[...]
        lt = jnp.transpose(logits)[0:ROUTER_ROWS, :]
        rowi = lax.broadcasted_iota(jnp.int32, lt.shape, 0)
        big = jnp.int32(LANES)
        lg = jnp.where(rowi < MOE_GROUPS, lt, NEG)
        gmax = jnp.max(lg, axis=0, keepdims=True)
        gidx = jnp.min(jnp.where(lg == gmax, rowi, big), axis=0, keepdims=True)
        gval = 1.0 / jnp.sum(jnp.exp(lg - gmax), axis=0, keepdims=True)
        first = MOE_GROUPS + gidx * MOE_EXPERTS
        le = jnp.where((rowi >= first) & (rowi < first + MOE_EXPERTS), lt, NEG)
        m1 = jnp.max(le, axis=0, keepdims=True)
        i1 = jnp.min(jnp.where(le == m1, rowi, big), axis=0, keepdims=True)
        le2 = jnp.where(rowi == i1, NEG, le)
        m2 = jnp.max(le2, axis=0, keepdims=True)
        i2 = jnp.min(jnp.where(le2 == m2, rowi, big), axis=0, keepdims=True)
        t = jnp.exp(m2 - m1)
        w1 = 1.0 / (1.0 + t)
        w2 = t * w1

        hit1, hit2 = rowi == i1, rowi == i2
        onehot = jnp.where(hit1 | hit2, 1.0, 0.0)
        earlier = jnp.dot(onehot.astype(BF16), utri_ref[...], preferred_element_type=F32) + carry_ref[...]
        rank1 = jnp.sum(jnp.where(hit1, earlier, 0.0), axis=0, keepdims=True).astype(jnp.int32)
        rank2 = jnp.sum(jnp.where(hit2, earlier, 0.0), axis=0, keepdims=True).astype(jnp.int32)
        carry_ref[...] = carry_ref[...] + jnp.sum(onehot, axis=1, keepdims=True)
        r8 = lax.broadcasted_iota(jnp.int32, (ROUTE_ROWS, lt.shape[1]), 0)
        route_ref[0, :, rows] = jnp.where(r8 == 0, i1 - MOE_GROUPS, jnp.where(r8 == 1, i2 - MOE_GROUPS,
                                          jnp.where(r8 == 2, rank1, jnp.where(r8 == 3, rank2, 0))))
        r128 = lax.broadcasted_iota(jnp.int32, (LANES, lt.shape[1]), 0)
        ew_ref[0, rows, :] = jnp.transpose(jnp.where(r128 == 0, gval * w1, jnp.where(r128 == 1, gval * w2, 0.0)))
    cnt_ref[...] = jnp.broadcast_to(carry_ref[...], cnt_ref.shape).astype(jnp.int32)


def _merge(alpha, y_gla, o_groups, lse_groups, g_gla, g_att, x, g1, sc2, sh2, ln_g, ln_b, wpg, wpa, wout, wr, br):
    bsz, s, d = x.shape
    tm = min(ROW_TILE, s)
    assert s % tm == 0
    dilations = tuple(dil for _, dil in DIL_PATTERNS)
    row = lambda w: pl.BlockSpec((1, tm, w), lambda b, i: (b, i, 0))
    sub = lambda dil: pl.BlockSpec((1, dil, tm // dil, DIL_GROUP_WIDTH), lambda b, i: (b, 0, i, 0))
    per_b = pl.BlockSpec((1, 1, d), lambda b, i: (b, 0, 0))
    full = lambda a: pl.BlockSpec(a.shape, lambda b, i: (0,) * a.ndim)
    ln_g2, ln_b2 = ln_g.reshape(1, d), ln_b.reshape(1, d)
    sub_rows = min(MERGE_SUB_ROWS, tm)
    assert tm % sub_rows == 0
    utri = jnp.asarray(np.triu(np.ones((sub_rows, sub_rows), np.float32), 1), BF16)
    return pl.pallas_call(
        functools.partial(_merge_kernel, alpha, dilations),
        out_shape=[jax.ShapeDtypeStruct((bsz, s, d), F32), jax.ShapeDtypeStruct((bsz, s, d // 2), jnp.int32),
                   jax.ShapeDtypeStruct((bsz, ROUTE_ROWS, s), jnp.int32), jax.ShapeDtypeStruct((bsz, s, LANES), F32),
                   jax.ShapeDtypeStruct((ROUTER_ROWS, LANES), jnp.int32)],
        grid=(bsz, s // tm),
        in_specs=[row(y_gla.shape[-1])] + [sub(dil) for dil in dilations] * 2
                 + [row(d), row(d), row(d), per_b, per_b, per_b, full(ln_g2), full(ln_b2),
                    full(wpg), full(wpa), full(wout), full(wr), full(br), full(utri)],
        out_specs=[row(d), row(d // 2), pl.BlockSpec((1, ROUTE_ROWS, tm), lambda b, i: (b, 0, i)), row(LANES),
                   pl.BlockSpec((ROUTER_ROWS, LANES), lambda b, i: (0, 0))],
        scratch_shapes=[pltpu.VMEM((2 * DIL_GROUPS, DIL_GROUP_WIDTH // LANES, tm, LANES), F32),
                        pltpu.VMEM((ROUTER_ROWS, 1), F32)],
        compiler_params=_cparams(("arbitrary", "arbitrary")),
        name="merge_ln1_router",
    )(y_gla, *o_groups, *lse_groups, g_gla, g_att, x, g1, sc2, sh2, ln_g2, ln_b2, wpg, wpa, wout, wr, br, utri)


def _expert_kernel(run_ref, valid_ref, rexp_ref, used_ref, x_ref, wg_hbm, wu_hbm, wd_hbm, o_ref,
                   wg_f, wu_f, wd_f, wg_s, wu_s, wd_s, sem):
    t = pl.program_id(0)
    n_tiles_used, n_runs = used_ref[0], used_ref[1]
    run = run_ref[t]
    active = t < n_tiles_used
    first_of_run = (t == 0) | (run_ref[jnp.maximum(t - 1, 0)] != run)

    def weight_copies(r):
        e, slot = rexp_ref[r], r % 2
        return [pltpu.make_async_copy(hbm.at[e], buf.at[slot], sem.at[slot, j])
                for j, (hbm, buf) in enumerate(((wg_hbm, wg_f), (wu_hbm, wu_f), (wd_hbm, wd_f)))]

    @pl.when(active & (t == 0))
    def _():
        for cp in weight_copies(0):
            cp.start()

    @pl.when(active & first_of_run)
    def _():
        @pl.when(run + 1 < n_runs)
        def _():
            for cp in weight_copies(run + 1):
                cp.start()

        for cp in weight_copies(run):
            cp.wait()
        slot = run % 2
        wg_s[...] = wg_f[slot].astype(BF16)
        wu_s[...] = wu_f[slot].astype(BF16)
        wd_s[...] = wd_f[slot].astype(BF16)

    n_valid = jnp.where(active, valid_ref[t], 0)
    def ffn(rows):
        xt = _unpack_bf16_pairs(x_ref[rows, :]).astype(BF16)
        hg = jnp.dot(xt, wg_s[...], preferred_element_type=F32)
        hu = jnp.dot(xt, wu_s[...], preferred_element_type=F32)
        h = (_silu(hg) * hu).astype(BF16)
        o_ref[rows, :] = _pack_bf16_pairs(jnp.dot(h, wd_s[...], preferred_element_type=F32))

    def zero(rows):
        o_ref[rows, :] = jnp.zeros((rows.stop - rows.start, o_ref.shape[1]), o_ref.dtype)

    tm = x_ref.shape[0]
    n_blocks = tm // EXPERT_BLOCK
    for k in range(n_blocks + 1):
        lo, hi = (k - 1) * EXPERT_BLOCK, k * EXPERT_BLOCK

        @pl.when((n_valid > lo) & (n_valid <= hi) if 0 < k < n_blocks else (n_valid > lo if k else n_valid <= 0))
        def _():
            if k:
                ffn(slice(0, hi))
            if k < n_blocks:
                zero(slice(hi, tm))


def _expert_ffn(tile_run, tile_valid, run_expert, used, xg, w_gate, w_up, w_down):
    p = xg.shape[0]
    ne, d, ff = w_gate.shape
    tm = EXPERT_TILE
    n_tiles = p // tm
    hbm = pl.BlockSpec(memory_space=pl.ANY)
    grid_spec = pltpu.PrefetchScalarGridSpec(
        num_scalar_prefetch=4,
        grid=(n_tiles,),
        in_specs=[pl.BlockSpec((tm, d // 2), lambda t, *_: (t, 0)), hbm, hbm, hbm],
        out_specs=pl.BlockSpec((tm, d // 2), lambda t, *_: (t, 0)),
        scratch_shapes=[pltpu.VMEM((2, d, ff), F32), pltpu.VMEM((2, d, ff), F32), pltpu.VMEM((2, ff, d), F32),
                        pltpu.VMEM((d, ff), BF16), pltpu.VMEM((d, ff), BF16), pltpu.VMEM((ff, d), BF16),
                        pltpu.SemaphoreType.DMA((2, 3))],
    )
    return pl.pallas_call(
        _expert_kernel,
        out_shape=jax.ShapeDtypeStruct((p, d // 2), jnp.int32),
        grid_spec=grid_spec,
        compiler_params=_cparams(("arbitrary",)),
        name="expert_ffn",
    )(tile_run, tile_valid, run_expert, used, xg, w_gate, w_up, w_down)


def _final_kernel(alpha, x1_ref, ya_ref, yb_ref, ew_ref, g2_ref, ln_g_ref, ln_b_ref, o_ref):
    ew = ew_ref[0]
    y = ew[:, 0:1] * _unpack_bf16_pairs(ya_ref[0]) + ew[:, 1:2] * _unpack_bf16_pairs(yb_ref[0])
    o_ref[0] = _layer_norm(alpha * x1_ref[0] + g2_ref[0] * y, ln_g_ref[...], ln_b_ref[...])


def _final(alpha, x1, ya, yb, ew, g2, ln_g, ln_b):
    bsz, s, d = x1.shape
    tm = min(FINAL_TILE, s)
    row = lambda w: pl.BlockSpec((1, tm, w), lambda b, i: (b, i, 0))
    full = lambda a: pl.BlockSpec(a.shape, lambda b, i: (0,) * a.ndim)
    ln_g2, ln_b2 = ln_g.reshape(1, d), ln_b.reshape(1, d)
    return pl.pallas_call(
        functools.partial(_final_kernel, alpha),
        out_shape=jax.ShapeDtypeStruct((bsz, s, d), F32),
        grid=(bsz, s // tm),
        in_specs=[row(d), row(d // 2), row(d // 2), row(LANES), pl.BlockSpec((1, 1, d), lambda b, i: (b, 0, 0)),
                  full(ln_g2), full(ln_b2)],
        out_specs=row(d),
        compiler_params=_cparams(("parallel", "arbitrary")),
        name="combine_ln2",
    )(x1, ya, yb, ew, g2, ln_g2, ln_b2)


SC_CORES = 2
SC_SUBCORES = 16
SC_CHUNK = 64


def _sc_mesh():
    return plsc.VectorSubcoreMesh(core_axis_name="c", subcore_axis_name="s")


def _sc_scatter_rows(rows, dest0, dest1, n_rows):
    n, w = rows.shape
    n_workers = SC_CORES * SC_SUBCORES
    assert n % (n_workers * SC_CHUNK) == 0
    n_chunks = n // (n_workers * SC_CHUNK)
    d0 = dest0.reshape(n // SC_CHUNK, 1, SC_CHUNK)
    d1 = dest1.reshape(n // SC_CHUNK, 1, SC_CHUNK)

    @functools.partial(
        pl.kernel, mesh=_sc_mesh(), out_type=jax.ShapeDtypeStruct((n_rows, w), rows.dtype),
        scratch_types=[pltpu.VMEM((n_chunks, 1, SC_CHUNK), jnp.int32), pltpu.VMEM((n_chunks, 1, SC_CHUNK), jnp.int32),
                       pltpu.VMEM((2, SC_CHUNK, w), rows.dtype),
                       pltpu.SemaphoreType.DMA((2,)), pltpu.SemaphoreType.DMA((2, 2))])
    def scatter_kernel(rows_hbm, d0_hbm, d1_hbm, out_hbm, i0_v, i1_v, rows_v, read_sem, scat_sem):
        wid = lax.axis_index("s") * SC_CORES + lax.axis_index("c")
        first = wid * n_chunks
        pltpu.sync_copy(d0_hbm.at[pl.ds(first, n_chunks)], i0_v)
        pltpu.sync_copy(d1_hbm.at[pl.ds(first, n_chunks)], i1_v)

        def read(j):
            return pltpu.make_async_copy(rows_hbm.at[pl.ds((first + j) * SC_CHUNK, SC_CHUNK)], rows_v.at[j % 2],
                                         read_sem.at[j % 2])

        def scatters(j):
            return [pltpu.make_async_copy(rows_v.at[j % 2], out_hbm.at[idx.at[j].at[0]], scat_sem.at[j % 2, k])
                    for k, idx in enumerate((i0_v, i1_v))]

        read(0).start()
        for j in range(n_chunks):
            read(j).wait()
            if j + 1 < n_chunks:
                if j >= 1:
                    for cp in scatters(j - 1):
                        cp.wait()
                read(j + 1).start()
            for cp in scatters(j):
                cp.start()
        for j in range(max(n_chunks - 2, 0), n_chunks):
            for cp in scatters(j):
                cp.wait()

    return scatter_kernel(rows, d0, d1)


def _sc_gather_rows(table, dest0, dest1):
    n = dest0.shape[0]
    w = table.shape[1]
    n_workers = SC_CORES * SC_SUBCORES
    assert n % (n_workers * SC_CHUNK) == 0
    n_chunks = n // (n_workers * SC_CHUNK)
    d0 = dest0.reshape(n // SC_CHUNK, 1, SC_CHUNK)
    d1 = dest1.reshape(n // SC_CHUNK, 1, SC_CHUNK)
    out = jax.ShapeDtypeStruct((n, w), table.dtype)

    @functools.partial(
        pl.kernel, mesh=_sc_mesh(), out_type=(out, out),
        scratch_types=[pltpu.VMEM((n_chunks, 1, SC_CHUNK), jnp.int32), pltpu.VMEM((n_chunks, 1, SC_CHUNK), jnp.int32),
                       pltpu.VMEM((2, SC_CHUNK, w), table.dtype),
                       pltpu.SemaphoreType.DMA((2,)), pltpu.SemaphoreType.DMA((2,))])
    def gather_kernel(table_hbm, d0_hbm, d1_hbm, a_hbm, b_hbm, i0_v, i1_v, rows_v, gather_sem, write_sem):
        wid = lax.axis_index("s") * SC_CORES + lax.axis_index("c")
        first = wid * n_chunks
        pltpu.sync_copy(d0_hbm.at[pl.ds(first, n_chunks)], i0_v)
        pltpu.sync_copy(d1_hbm.at[pl.ds(first, n_chunks)], i1_v)
        n_items = 2 * n_chunks

        def gather(m):
            idx = (i0_v, i1_v)[m % 2]
            return pltpu.make_async_copy(table_hbm.at[idx.at[m // 2].at[0]], rows_v.at[m % 2], gather_sem.at[m % 2])

        def write(m):
            o_hbm = (a_hbm, b_hbm)[m % 2]
            return pltpu.make_async_copy(rows_v.at[m % 2], o_hbm.at[pl.ds((first + m // 2) * SC_CHUNK, SC_CHUNK)],
                                         write_sem.at[m % 2])

        gather(0).start()
        for m in range(n_items):
            gather(m).wait()
            if m + 1 < n_items:
                if m >= 1:
                    write(m - 1).wait()
                gather(m + 1).start()
            write(m).start()
        for m in range(max(n_items - 2, 0), n_items):
            write(m).wait()

    return gather_kernel(table, d0, d1)


def _dispatch_plan(route, counts):
    tm = EXPERT_TILE
    e0, e1, r0, r1 = (route[:, j, :].reshape(-1) for j in range(4))
    experts = jnp.arange(MOE_TOTAL, dtype=jnp.int32)
    tiles_per = (counts + tm - 1) // tm
    tile_end = jnp.cumsum(tiles_per)
    pad_start = ((tile_end - tiles_per) * tm).astype(jnp.int32)

    def lookup(e):
        return jnp.sum(jnp.where(e[None, :] == experts[:, None], pad_start[:, None], 0), axis=0)

    dest0, dest1 = lookup(e0) + r0, lookup(e1) + r1
    n_tiles = (2 * e0.size + MOE_TOTAL * tm) // tm
    tile_expert = jnp.minimum(jnp.sum(tile_end[None, :] <= jnp.arange(n_tiles)[:, None], axis=1), MOE_TOTAL - 1)
    nonempty = counts > 0
    run_of_expert = jnp.cumsum(nonempty.astype(jnp.int32)) - 1
    run_expert = jnp.sum(jnp.where(nonempty[None, :] & (run_of_expert[None, :] == experts[:, None]),
                                   experts[None, :], 0), axis=1).astype(jnp.int32)
    of_tile = tile_expert[:, None] == experts[None, :]
    tile_run = jnp.sum(jnp.where(of_tile, run_of_expert[None, :], 0), axis=1).astype(jnp.int32)
    rows_left = (counts + pad_start)[None, :] - jnp.arange(n_tiles)[:, None] * tm
    tile_valid = jnp.clip(jnp.sum(jnp.where(of_tile, rows_left, 0), axis=1), 0, tm).astype(jnp.int32)
    used = jnp.stack([tile_end[-1], jnp.sum(nonempty)]).astype(jnp.int32)
    return dest0, dest1, tile_run, tile_valid, run_expert, used, n_tiles * tm


def _layer(x, c, rel_bias, w_ada, b_ada, w_in, w_gla_gate, b_gla_gate, gla_norm, w_proj_gla, w_proj_attn, w_out,
           ln1_g, ln1_b, w_rg, b_rg, w_re, b_re, w_eg, w_eu, w_ed, ln2_g, ln2_b):
    bsz, s, d = x.shape
    alpha = (2.0 * DEPTH) ** 0.25
    mods = _ada_mods(c, w_ada, b_ada)
    sh1, sc1, g1, sh2, sc2, g2 = [m.reshape(bsz, 1, d) for m in jnp.split(mods, N_MOD, axis=-1)]

    lr0 = d // 2 * 2 + 2 * d
    z = _in_projection(x, sc1, sh1, _prep_in_weight(w_in, lr0), w_gla_gate, b_gla_gate)

    y_gla = _gla(z["q_in"], z["k_in"], z["q_st"], z["k_st"], z["dec"], z["v_gla"], z["r_gla"], gla_norm)

    o_groups, lse_groups = [], []
    for g, (window, dilation) in enumerate(DIL_PATTERNS):
        l = s // dilation
        qg, kg, vg = (z[f"{n}{g}"].reshape(bsz * dilation, l, DIL_GROUP_WIDTH) for n in ("q_att", "k_att", "v_att"))
        table = rel_bias[:, g * DIL_HEADS_PER_GROUP:(g + 1) * DIL_HEADS_PER_GROUP]
        o, lse = _dilated_group_attention(qg, kg, vg, table, window, dilation)
        o_groups.append(o.reshape(bsz, dilation, l, DIL_GROUP_WIDTH))
        lse_groups.append(lse.reshape(bsz, dilation, l, DIL_GROUP_WIDTH))

    wr = jnp.concatenate([w_rg, w_re, jnp.zeros((d, LANES - MOE_GROUPS - MOE_TOTAL), F32)], axis=1)
    br = jnp.concatenate([b_rg, b_re, jnp.zeros((LANES - MOE_GROUPS - MOE_TOTAL,), F32)]).reshape(1, LANES)
    x1, u2, route, ew, cnt = _merge(alpha, y_gla, o_groups, lse_groups, z["g_gla"], z["g_att"], x, g1, sc2, sh2,
                                    ln1_g, ln1_b, w_proj_gla.astype(BF16), w_proj_attn.astype(BF16),
                                    w_out.astype(BF16), wr, br)

    n = bsz * s
    counts = cnt[MOE_GROUPS:MOE_GROUPS + MOE_TOTAL, 0]
    dest0, dest1, tile_run, tile_valid, run_expert, used, n_rows = _dispatch_plan(route, counts)
    xg = _sc_scatter_rows(u2.reshape(n, d // 2), dest0, dest1, n_rows)
    ff = w_eg.shape[-1]
    yo = _expert_ffn(tile_run, tile_valid, run_expert, used, xg, w_eg.reshape(MOE_TOTAL, d, ff),
                     w_eu.reshape(MOE_TOTAL, d, ff), w_ed.reshape(MOE_TOTAL, ff, d))
    ya, yb = (y.reshape(bsz, s, d // 2) for y in _sc_gather_rows(yo, dest0, dest1))
    return _final(alpha, x1, ya, yb, ew, g2, ln2_g, ln2_b)


def kernel(x, c, rel_bias, w_ada, b_ada, w_in, w_gla_gate, b_gla_gate, gla_norm, w_proj_gla, w_proj_attn, w_out,
           ln1_g, ln1_b, w_router_group, b_router_group, w_router_expert, b_router_expert, w_exp_gate, w_exp_up,
           w_exp_down, ln2_g, ln2_b):
    assert w_ada.shape[0] == DEPTH
    return _layer(x, c, rel_bias, w_ada[0], b_ada[0], w_in[0:1], w_gla_gate[0], b_gla_gate[0], gla_norm[0],
                  w_proj_gla[0], w_proj_attn[0], w_out[0], ln1_g[0], ln1_b[0], w_router_group[0],
                  b_router_group[0], w_router_expert[0], b_router_expert[0], w_exp_gate[0], w_exp_up[0],
                  w_exp_down[0], ln2_g[0], ln2_b[0])
```

```python
import functools
import math

import numpy as np
import jax
import jax.numpy as jnp
from jax import lax
from jax.experimental import pallas as pl
from jax.experimental.pallas import tpu as pltpu
from jax.experimental.pallas import tpu_sc as plsc

F32 = jnp.float32
BF16 = jnp.bfloat16

N_MOD = 6
GLA_HEADS = 4
GLA_LOWRANK = 16
GLA_TAU = 16.0
GLA_CHUNK = 64
DIL_PATTERNS = ((128, 1), (512, 4), (2048, 16))
DIL_GROUPS = len(DIL_PATTERNS)
DIL_HEADS_PER_GROUP = 8
DIL_HEAD_DIM = 64
DIL_GROUP_WIDTH = DIL_HEADS_PER_GROUP * DIL_HEAD_DIM
DIL_BLOCK = 128
REL_BUCKETS = 32
REL_MAX_DIST = 2048
MOE_GROUPS = 4
MOE_EXPERTS = 8
MOE_TOTAL = MOE_GROUPS * MOE_EXPERTS
LN_EPS = 1e-5
DEPTH = 1

LANES = 128
VMEM_LIMIT = 56 * 1024 * 1024
LOG2E = 1.4426950408889634
LN2 = 0.6931471805599453
NEG = -1e30
ROW_TILE = 512
FINAL_TILE = 1024
EXPERT_TILE = 512
EXPERT_BLOCK = 128
GLA_STEP_CHUNKS = 8
ATT_STEP_BLOCKS = 4
MERGE_SUB_ROWS = 512
ROUTER_ROWS = 40
ROUTE_ROWS = 8

HIGHEST = lax.Precision.HIGHEST
NT_DIMS = (((1,), (1,)), ((), ()))
TN_DIMS = (((0,), (0,)), ((), ()))


def _cparams(sem):
    return pltpu.CompilerParams(dimension_semantics=sem, vmem_limit_bytes=VMEM_LIMIT)


def _sigmoid(x):
    return 0.5 * jnp.tanh(0.5 * x) + 0.5


def _silu(x):
    return x * _sigmoid(x)


def _layer_norm(x, g, b):
    mu = jnp.mean(x, axis=-1, keepdims=True)
    xc = x - mu
    var = jnp.mean(xc * xc, axis=-1, keepdims=True)
    return xc * lax.rsqrt(var + LN_EPS) * g + b


def _pack_bf16_pairs(x):
    w = x.shape[1] // 2
    lo = lax.bitcast_convert_type(x[:, :w].astype(BF16).astype(F32), jnp.uint32) >> 16
    hi = lax.bitcast_convert_type(x[:, w:].astype(BF16).astype(F32), jnp.uint32) & jnp.uint32(0xFFFF0000)
    return lax.bitcast_convert_type(lo | hi, jnp.int32)


def _unpack_bf16_pairs(p):
    u = lax.bitcast_convert_type(p, jnp.uint32)
    lo = lax.bitcast_convert_type(u << 16, F32)
    hi = lax.bitcast_convert_type(u & jnp.uint32(0xFFFF0000), F32)
    return jnp.concatenate([lo, hi], axis=1)


def _mods_kernel(ct_ref, w_ref, b_ref, o_ref):
    a = _silu(ct_ref[...])
    w = w_ref[...]
    for b in range(a.shape[1]):
        o_ref[b:b + 1, :] = jnp.sum(a[:, b:b + 1] * w, axis=0, keepdims=True) + b_ref[...]


def _ada_mods(c, w, b):
    bsz, d = c.shape
    n = w.shape[1]
    tn = 1536
    assert n % tn == 0
    return pl.pallas_call(
        _mods_kernel,
        out_shape=jax.ShapeDtypeStruct((bsz, n), F32),
        grid=(n // tn,),
        in_specs=[pl.BlockSpec((d, bsz), lambda j: (0, 0)),
                  pl.BlockSpec((d, tn), lambda j: (0, j)),
                  pl.BlockSpec((1, tn), lambda j: (0, j))],
        out_specs=pl.BlockSpec((bsz, tn), lambda j: (0, j)),
        compiler_params=_cparams(("arbitrary",)),
        name="ada_mods",
    )(c.T, w, b.reshape(1, n))


def _proj_pieces(d_model):
    dk = d_model // 2
    pieces = [("q_gla", dk, "scale_q_gla"), ("k_gla", dk, None), ("v_gla", d_model, None), ("r_gla", d_model, "silu")]
    for name, post in (("q_att", "scale_q_att"), ("k_att", None), ("v_att", None)):
        for g, (_, dilation) in enumerate(DIL_PATTERNS):
            pieces.append((f"{name}{g}", DIL_GROUP_WIDTH, (post, dilation)))
    pieces += [("g_gla", d_model, "sigmoid"), ("g_att", d_model, "sigmoid"), ("lr", LANES, "lowrank")]
    return tuple(pieces)


WT_BLOCK = 512


def _wprep_kernel(n_main_blocks, w_ref, o_ref):
    blk = w_ref[0]
    row = lax.broadcasted_iota(jnp.int32, blk.shape, 0)
    keep = (pl.program_id(0) < n_main_blocks) | (row < GLA_LOWRANK)
    o_ref[...] = jnp.where(keep, blk, 0.0).astype(BF16)


def _prep_in_weight(w_in, lr0):
    w_t = jnp.swapaxes(w_in, 1, 2)
    _, n_in, d = w_t.shape
    n_main = n_in - GLA_LOWRANK
    assert lr0 % WT_BLOCK == 0 and n_main % WT_BLOCK == 0
    n_main_blocks = n_main // WT_BLOCK

    def src_row(j):
        start = j * WT_BLOCK
        octet = jnp.where(j < n_main_blocks, (start + jnp.where(start >= lr0, GLA_LOWRANK, 0)) // 8, lr0 // 8)
        return octet * 8

    return pl.pallas_call(
        functools.partial(_wprep_kernel, n_main_blocks),
        out_shape=jax.ShapeDtypeStruct((n_main + WT_BLOCK, d), BF16),
        grid=(n_main_blocks + 1,),
        in_specs=[pl.BlockSpec((pl.Element(1), pl.Element(WT_BLOCK), pl.Element(d)), lambda j: (0, src_row(j), 0))],
        out_specs=pl.BlockSpec((WT_BLOCK, d), lambda j: (j, 0)),
        compiler_params=_cparams(("parallel",)),
        name="prep_in_weight",
    )(w_t)


GLA_HELD = ("lr", "q_gla", "k_gla")


def _gla_operands(hold, wg_ref, bg_ref, qin_ref, kin_ref, qst_ref, kst_ref, dec_ref):
    c = GLA_CHUNK
    tm = hold["q_gla"].shape[0]
    tril = (lax.broadcasted_iota(jnp.int32, (c, c), 0) >= lax.broadcasted_iota(jnp.int32, (c, c), 1)).astype(BF16)
    mid = c // 2 - 1
    lr, wg = hold["lr"][:, 0:GLA_LOWRANK], wg_ref[...]
    lr_hi, wg_hi = lr.astype(BF16), wg.astype(BF16)
    lr_lo, wg_lo = (lr - lr_hi.astype(F32)).astype(BF16), (wg - wg_hi.astype(F32)).astype(BF16)
    gate_in = (jnp.dot(lr_hi, wg_hi, preferred_element_type=F32) + jnp.dot(lr_lo, wg_hi, preferred_element_type=F32)
               + jnp.dot(lr_hi, wg_lo, preferred_element_type=F32)) + bg_ref[...]
    g_all = (jnp.minimum(gate_in, 0.0) - jnp.log(1.0 + jnp.exp(-jnp.abs(gate_in)))) * (1.0 / GLA_TAU)
    g_hi = g_all.astype(BF16)
    g_lo = (g_all - g_hi.astype(F32)).astype(BF16)
    for ci in range(tm // c):
        rows = slice(ci * c, (ci + 1) * c)
        bc = jnp.dot(tril, g_hi[rows], preferred_element_type=F32) + jnp.dot(tril, g_lo[rows], preferred_element_type=F32)
        b_mid = bc[mid:mid + 1, :]
        b_last = bc[c - 1:c, :]
        qf = hold["q_gla"][rows, :]
        kf = hold["k_gla"][rows, :]
        q_in = qf * jnp.exp(bc - b_mid)
        k_in = kf * jnp.exp(b_mid - bc)
        qin_ref[0, rows, :] = q_in.astype(BF16)
        kin_ref[0, rows, :] = k_in.astype(BF16)
        qst_ref[0, rows, :] = (q_in * jnp.exp(b_mid)).astype(BF16)
        kst_ref[0, rows, :] = (k_in * jnp.exp(b_last - b_mid)).astype(BF16)
        dec_ref[0, ci:ci + 1, :] = jnp.exp(b_last)


def _proj_kernel(pieces, head_k, x_ref, sc_ref, sh_ref, w_ref, wg_ref, bg_ref, *refs):
    n_out = len(pieces) - len(GLA_HELD)
    out_refs = dict(zip([p[0] for p in pieces if p[0] not in GLA_HELD], refs[:n_out]))
    gla_out_refs = refs[n_out:n_out + 5]
    stage_ref = refs[n_out + 5]
    hold = dict(zip(GLA_HELD, refs[n_out + 6:]))
    tm = x_ref.shape[1]
    u = (x_ref[0] * (1.0 + sc_ref[0]) + sh_ref[0]).astype(BF16)
    offsets, off = {}, 0
    for name, width, _ in pieces:
        offsets[name] = off
        off += width
    by_name = {p[0]: p for p in pieces}
    held = [(by_name[n], 0) for n in GLA_HELD]
    rest = [(p, c0) for p in pieces if p[0] not in GLA_HELD for c0 in range(0, p[1], min(p[1], 512))]
    for n, (piece, c0) in enumerate(held + rest):
        if n == len(held):
            _gla_operands(hold, wg_ref, bg_ref, *gla_out_refs)
        name, width, post = piece
        o_ref = hold[name] if name in GLA_HELD else out_refs[name]
        off = offsets[name]
        chunk = min(width, 512)
        if True:
            acc = lax.dot_general(u, w_ref[off + c0:off + c0 + chunk, :], NT_DIMS, preferred_element_type=F32)
            if post == "silu":
                acc = _silu(acc)
            elif post == "sigmoid":
                acc = _sigmoid(acc)
            elif post == "scale_q_gla":
                acc = acc * (head_k ** -0.5)
            if name in GLA_HELD:
                o_ref[...] = acc
            elif isinstance(post, tuple):
                scale, dilation = post
                if scale is not None:
                    acc = acc * (DIL_HEAD_DIM ** -0.5 * LOG2E)
                if dilation == 1:
                    o_ref[0, 0] = acc.astype(o_ref.dtype)
                else:
                    for t in range(width // LANES):
                        stage_ref[t] = acc[:, t * LANES:(t + 1) * LANES]
                    for r in range(dilation):
                        for t in range(width // LANES):
                            o_ref[0, r, :, t * LANES:(t + 1) * LANES] = stage_ref[
                                t, pl.ds(r, tm // dilation, stride=dilation), :].astype(o_ref.dtype)
            else:
                o_ref[0, :, c0:c0 + chunk] = acc.astype(o_ref.dtype)


def _in_projection(x, sc1, sh1, w_perm, w_gate, b_gate):
    bsz, s, d = x.shape
    pieces = _proj_pieces(d)
    assert sum(p[1] for p in pieces) <= w_perm.shape[0]
    tm = min(ROW_TILE, s)
    assert s % tm == 0 and tm % (8 * GLA_CHUNK) == 0
    dk = d // 2
    head_k = dk // GLA_HEADS
    out_shape, out_specs = [], []
    for name, width, post in pieces:
        if name in GLA_HELD:
            continue
        if isinstance(post, tuple):
            dil = post[1]
            assert tm % (dil * 16) == 0
            out_shape.append(jax.ShapeDtypeStruct((bsz, dil, s // dil, width), BF16))
            out_specs.append(pl.BlockSpec((1, dil, tm // dil, width), lambda b, i: (b, 0, i, 0)))
        else:
            out_shape.append(jax.ShapeDtypeStruct((bsz, s, width), BF16))
            out_specs.append(pl.BlockSpec((1, tm, width), lambda b, i: (b, i, 0)))
    row = lambda w: pl.BlockSpec((1, tm, w), lambda b, i: (b, i, 0))
    gla_names = ("q_in", "k_in", "q_st", "k_st", "dec")
    out_shape += [jax.ShapeDtypeStruct((bsz, s, dk), BF16)] * 4 + [jax.ShapeDtypeStruct((bsz, s // GLA_CHUNK, dk), F32)]
    out_specs += [row(dk)] * 4 + [pl.BlockSpec((1, tm // GLA_CHUNK, dk), lambda b, i: (b, i, 0))]
    bg = b_gate.reshape(1, dk)
    full = lambda a: pl.BlockSpec(a.shape, lambda b, i: (0,) * a.ndim)
    outs = pl.pallas_call(
        functools.partial(_proj_kernel, pieces, head_k),
        out_shape=out_shape,
        grid=(bsz, s // tm),
        in_specs=[row(d),
                  pl.BlockSpec((1, 1, d), lambda b, i: (b, 0, 0)),
                  pl.BlockSpec((1, 1, d), lambda b, i: (b, 0, 0)),
                  pl.BlockSpec(w_perm.shape, lambda b, i: (0, 0), pipeline_mode=pl.Buffered(1)),
                  full(w_gate), full(bg)],
        out_specs=out_specs,
        scratch_shapes=[pltpu.VMEM((DIL_GROUP_WIDTH // LANES, tm, LANES), F32),
                        pltpu.VMEM((tm, LANES), F32), pltpu.VMEM((tm, dk), F32), pltpu.VMEM((tm, dk), F32)],
        compiler_params=_cparams(("parallel", "arbitrary")),
        name="in_projection",
    )(x, sc1, sh1, w_perm, w_gate, bg)
    return dict(zip([p[0] for p in pieces if p[0] not in GLA_HELD] + list(gla_names), outs))


def _gla_kernel(n_chunks, head_k, head_v, qin_ref, kin_ref, qst_ref, kst_ref, dec_ref, v_ref, r_ref, ng_ref, o_ref,
                state_ref):
    @pl.when(pl.program_id(1) == 0)
    def _():
        state_ref[...] = jnp.zeros_like(state_ref)

    c = GLA_CHUNK
    causal = lax.broadcasted_iota(jnp.int32, (c, c), 0) >= lax.broadcasted_iota(jnp.int32, (c, c), 1)
    for ci in range(n_chunks):
        rows = slice(ci * c, (ci + 1) * c)
        for h in range(GLA_HEADS):
            ks = slice(h * head_k, (h + 1) * head_k)
            vs = slice(h * head_v, (h + 1) * head_v)
            vh = v_ref[0, rows, vs]
            att = lax.dot_general(qin_ref[0, rows, ks], kin_ref[0, rows, ks], NT_DIMS, preferred_element_type=F32)
            att = jnp.where(causal, att, 0.0).astype(BF16)
            st = state_ref[h]
            o = jnp.dot(att, vh, preferred_element_type=F32)
            o = o + lax.dot_general(qst_ref[0, rows, ks], st.astype(BF16), NT_DIMS, preferred_element_type=F32)
            kv_t = lax.dot_general(vh, kst_ref[0, rows, ks], TN_DIMS, preferred_element_type=F32)
            state_ref[h] = st * dec_ref[0, ci:ci + 1, ks] + kv_t
            ms = jnp.mean(o * o, axis=-1, keepdims=True)
            o = o * lax.rsqrt(ms + LN_EPS) * ng_ref[:, vs] * r_ref[0, rows, vs].astype(F32)
            o_ref[0, rows, vs] = o.astype(o_ref.dtype)


def _gla(q_in, k_in, q_st, k_st, dec, v, r_silu, norm_g):
    bsz, s, dk = q_in.shape
    dv = v.shape[-1]
    head_k, head_v = dk // GLA_HEADS, dv // GLA_HEADS
    n_chunks = min(GLA_STEP_CHUNKS, s // GLA_CHUNK)
    ct = GLA_CHUNK * n_chunks
    assert s % ct == 0
    row_spec = lambda w: pl.BlockSpec((1, ct, w), lambda b, i: (b, i, 0))
    full = lambda a: pl.BlockSpec(a.shape, lambda b, i: (0,) * a.ndim)
    ng = norm_g.reshape(1, dv)
    return pl.pallas_call(
        functools.partial(_gla_kernel, n_chunks, head_k, head_v),
        out_shape=jax.ShapeDtypeStruct((bsz, s, dv), BF16),
        grid=(bsz, s // ct),
        in_specs=[row_spec(dk)] * 4 + [pl.BlockSpec((1, n_chunks, dk), lambda b, i: (b, i, 0)),
                                       row_spec(dv), row_spec(dv), full(ng)],
        out_specs=row_spec(dv),
        scratch_shapes=[pltpu.VMEM((GLA_HEADS, head_v, head_k), F32)],
        compiler_params=_cparams(("parallel", "arbitrary")),
        name="gla",
    )(q_in, k_in, q_st, k_st, dec, v, r_silu, ng)


def _t5_bucket_np(dist):
    exact = REL_BUCKETS // 2
    d = np.maximum(dist, 1).astype(np.float32)
    large = exact + (np.log(d / np.float32(exact)) / np.float32(math.log(REL_MAX_DIST / exact))
                     * np.float32(REL_BUCKETS - exact)).astype(np.int32)
    large = np.minimum(large, REL_BUCKETS - 1)
    return np.where(dist < exact, dist, large).astype(np.int32)


def _band_tables(window, dilation):
    qi = np.arange(DIL_BLOCK)[:, None]
    kj = np.arange(2 * DIL_BLOCK)[None, :]
    m = qi + DIL_BLOCK - kj
    n_steps = window // dilation
    band = (m >= 0) & (m <= n_steps)
    bucket = _t5_bucket_np(np.clip(m, 0, n_steps) * dilation)
    return np.where(band, bucket, -1).astype(np.int32)


def _attn_kernel(nq, table_ref, bucket_ref, q_ref, kp_ref, kc_ref, vp_ref, vc_ref, o_ref, lse_ref,
                 bias_ref, p_ref):
    i = pl.program_id(1)
    blk = DIL_BLOCK
    hpg = DIL_HEADS_PER_GROUP
    n_pairs = hpg // 2

    @pl.when((pl.program_id(0) == 0) & (i == 0))
    def _():
        bucket = bucket_ref[...]
        for h in range(hpg):
            acc = jnp.full(bucket.shape, NEG, F32)
            for bkt in range(REL_BUCKETS):
                acc = jnp.where(bucket == bkt, table_ref[bkt, h] * LOG2E, acc)
            bias_ref[h * blk:(h + 1) * blk, :] = acc

    lane = lax.broadcasted_iota(jnp.int32, (blk, LANES), 1)
    low = lane < DIL_HEAD_DIM
    ones_rhs = jnp.ones((2 * blk, LANES), BF16)

    def windows(ref_p, ref_c, sq, qb, cols):
        if qb == 0:
            return jnp.concatenate([ref_p[sq, :, cols], ref_c[sq, 0:blk, cols]], axis=0)
        return ref_c[sq, (qb - 1) * blk:(qb + 1) * blk, cols]

    key_lane = lax.broadcasted_iota(jnp.int32, (1, 2 * blk), 1)
    no_prev = jnp.where((key_lane < blk) & (i == 0), NEG, 0.0)
    items = [(sq, qb, hp) for sq in range(q_ref.shape[0]) for qb in range(nq) for hp in range(n_pairs)]

    mxs = []
    for n, (sq, qb, hp) in enumerate(items):
        rows = slice(qb * blk, (qb + 1) * blk)
        cols = slice(hp * LANES, (hp + 1) * LANES)
        qp = q_ref[sq, rows, cols]
        zero = jnp.zeros_like(qp)
        qq = jnp.concatenate([jnp.where(low, qp, zero), jnp.where(low, zero, qp)], axis=0)
        keys = windows(kp_ref, kc_ref, sq, qb, cols)
        s = lax.dot_general(qq, keys, NT_DIMS, preferred_element_type=F32) + bias_ref[2 * hp * blk:(2 * hp + 2) * blk, :]
        if qb == 0:
            s = s + no_prev
        mx = jnp.max(s, axis=-1, keepdims=True)
        p_ref[n * 2 * blk:(n + 1) * 2 * blk, :] = jnp.exp2(s - mx).astype(BF16)
        mxs.append(mx)

    for n, (sq, qb, hp) in enumerate(items):
        rows = slice(qb * blk, (qb + 1) * blk)
        cols = slice(hp * LANES, (hp + 1) * LANES)
        vals = windows(vp_ref, vc_ref, sq, qb, cols)
        rhs = jnp.concatenate([vals, ones_rhs], axis=1)
        res = jnp.dot(p_ref[n * 2 * blk:(n + 1) * 2 * blk, :], rhs, preferred_element_type=F32)
        num = jnp.where(low, res[0:blk, 0:LANES], res[blk:2 * blk, 0:LANES])
        den = jnp.where(low, res[0:blk, LANES:], res[blk:2 * blk, LANES:])
        mx = jnp.where(low, mxs[n][0:blk], mxs[n][blk:2 * blk])
        o_ref[sq, rows, cols] = (num / den).astype(o_ref.dtype)
        lse_ref[sq, rows, cols] = (mx + jnp.log2(den)) * LN2


def _dilated_group_attention(q, k, v, table, window, dilation):
    bb, l, w = q.shape
    nq = min(ATT_STEP_BLOCKS, l // DIL_BLOCK)
    nsq = ATT_STEP_BLOCKS // nq
    assert l % (nq * DIL_BLOCK) == 0 and bb % nsq == 0
    steps = l // (nq * DIL_BLOCK)
    bucket = jnp.asarray(_band_tables(window, dilation))
    cur = pl.BlockSpec((nsq, nq * DIL_BLOCK, w), lambda b, i: (b, i, 0))
    prev = pl.BlockSpec((nsq, DIL_BLOCK, w), lambda b, i: (b, jnp.maximum(nq * i - 1, 0), 0))
    rows_all = nsq * nq * DIL_HEADS_PER_GROUP * DIL_BLOCK
    return pl.pallas_call(
        functools.partial(_attn_kernel, nq),
        out_shape=[jax.ShapeDtypeStruct((bb, l, w), BF16), jax.ShapeDtypeStruct((bb, l, w), F32)],
        grid=(bb // nsq, steps),
        in_specs=[pl.BlockSpec(memory_space=pltpu.SMEM),
                  pl.BlockSpec(bucket.shape, lambda b, i: (0, 0)),
                  cur, prev, cur, prev, cur],
        out_specs=[cur, cur],
        scratch_shapes=[pltpu.VMEM((DIL_HEADS_PER_GROUP * DIL_BLOCK, 2 * DIL_BLOCK), F32),
                        pltpu.VMEM((rows_all, 2 * DIL_BLOCK), BF16)],
        compiler_params=_cparams(("arbitrary", "arbitrary")),
        name=f"dilated_attn_d{dilation}",
    )(table, bucket, q, k, k, v, v)


def _merge_kernel(alpha, dilations, ygla_ref, o0_ref, o1_ref, o2_ref, l0_ref, l1_ref, l2_ref, gg_ref, ga_ref, x_ref,
                  g1_ref, sc2_ref, sh2_ref, ln_g_ref, ln_b_ref, wpg_ref, wpa_ref, wout_ref, wr_ref, br_ref, utri_ref,
                  x1_ref, u2_ref, route_ref, ew_ref, cnt_ref, stage_ref, carry_ref):
    tm = x_ref.shape[1]

    @pl.when((pl.program_id(0) == 0) & (pl.program_id(1) == 0))
    def _():
        carry_ref[...] = jnp.zeros_like(carry_ref)

    n_lt = DIL_GROUP_WIDTH // LANES
    group_refs = tuple(zip((l0_ref, l1_ref, l2_ref), (o0_ref, o1_ref, o2_ref), dilations))
    for gi, (l_ref, o_ref, dil) in enumerate(group_refs):
        if dil > 1:
            for slot, ref in ((2 * gi, l_ref), (2 * gi + 1, o_ref)):
                for r in range(dil):
                    for t in range(n_lt):
                        stage_ref[slot, t, pl.ds(r, tm // dil, stride=dil), :] = ref[
                            0, r, :, t * LANES:(t + 1) * LANES].astype(F32)

    w_hi = wr_ref[...].astype(BF16)
    sub = utri_ref.shape[0]
    for rows in (slice(r0, r0 + sub) for r0 in range(0, tm, sub)):
        def natural(ref, dil, slot):
            if dil == 1:
                return ref[0, 0, rows, :].astype(F32)
            return jnp.concatenate([stage_ref[slot, t, rows, :] for t in range(n_lt)], axis=1)

        lses = [natural(l_ref, dil, 2 * gi) for gi, (l_ref, _, dil) in enumerate(group_refs)]
        outs = [natural(o_ref, dil, 2 * gi + 1) for gi, (_, o_ref, dil) in enumerate(group_refs)]
        lm = jnp.maximum(jnp.maximum(lses[0], lses[1]), lses[2])
        es = [jnp.exp(l - lm) for l in lses]
        y_att = (es[0] * outs[0] + es[1] * outs[1] + es[2] * outs[2]) / (es[0] + es[1] + es[2])

        p_gla = jnp.dot(ygla_ref[0, rows, :], wpg_ref[...], preferred_element_type=F32)
        p_att = jnp.dot(y_att.astype(BF16), wpa_ref[...], preferred_element_type=F32)
        merged = gg_ref[0, rows, :].astype(F32) * p_gla + ga_ref[0, rows, :].astype(F32) * p_att
        y = jnp.dot(merged.astype(BF16), wout_ref[...], preferred_element_type=F32)
        x1 = _layer_norm(alpha * x_ref[0, rows, :] + g1_ref[0] * y, ln_g_ref[...], ln_b_ref[...])
        x1_ref[0, rows, :] = x1
        u2 = x1 * (1.0 + sc2_ref[0]) + sh2_ref[0]
        u2_ref[0, rows, :] = _pack_bf16_pairs(u2)

        logits = jnp.dot(u2.astype(BF16), w_hi, preferred_element_type=F32) + br_ref[...]
        lt = jnp.transpose(logits)[0:ROUTER_ROWS, :]
        rowi = lax.broadcasted_iota(jnp.int32, lt.shape, 0)
        big = jnp.int32(LANES)
        lg = jnp.where(rowi < MOE_GROUPS, lt, NEG)
        gmax = jnp.max(lg, axis=0, keepdims=True)
        gidx = jnp.min(jnp.where(lg == gmax, rowi, big), axis=0, keepdims=True)
        gval = 1.0 / jnp.sum(jnp.exp(lg - gmax), axis=0, keepdims=True)
        first = MOE_GROUPS + gidx * MOE_EXPERTS
        le = jnp.where((rowi >= first) & (rowi < first + MOE_EXPERTS), lt, NEG)
        m1 = jnp.max(le, axis=0, keepdims=True)
        i1 = jnp.min(jnp.where(le == m1, rowi, big), axis=0, keepdims=True)
        le2 = jnp.where(rowi == i1, NEG, le)
        m2 = jnp.max(le2, axis=0, keepdims=True)
        i2 = jnp.min(jnp.where(le2 == m2, rowi, big), axis=0, keepdims=True)
        t = jnp.exp(m2 - m1)
        w1 = 1.0 / (1.0 + t)
        w2 = t * w1

        hit1, hit2 = rowi == i1, rowi == i2
        onehot = jnp.where(hit1 | hit2, 1.0, 0.0)
        earlier = jnp.dot(onehot.astype(BF16), utri_ref[...], preferred_element_type=F32) + carry_ref[...]
        rank1 = jnp.sum(jnp.where(hit1, earlier, 0.0), axis=0, keepdims=True).astype(jnp.int32)
        rank2 = jnp.sum(jnp.where(hit2, earlier, 0.0), axis=0, keepdims=True).astype(jnp.int32)
        carry_ref[...] = carry_ref[...] + jnp.sum(onehot, axis=1, keepdims=True)
        r8 = lax.broadcasted_iota(jnp.int32, (ROUTE_ROWS, lt.shape[1]), 0)
        route_ref[0, :, rows] = jnp.where(r8 == 0, i1 - MOE_GROUPS, jnp.where(r8 == 1, i2 - MOE_GROUPS,
                                          jnp.where(r8 == 2, rank1, jnp.where(r8 == 3, rank2, 0))))
        r128 = lax.broadcasted_iota(jnp.int32, (LANES, lt.shape[1]), 0)
        ew_ref[0, rows, :] = jnp.transpose(jnp.where(r128 == 0, gval * w1, jnp.where(r128 == 1, gval * w2, 0.0)))
    cnt_ref[...] = jnp.broadcast_to(carry_ref[...], cnt_ref.shape).astype(jnp.int32)


def _merge(alpha, y_gla, o_groups, lse_groups, g_gla, g_att, x, g1, sc2, sh2, ln_g, ln_b, wpg, wpa, wout, wr, br):
    bsz, s, d = x.shape
    tm = min(ROW_TILE, s)
    assert s % tm == 0
    dilations = tuple(dil for _, dil in DIL_PATTERNS)
    row = lambda w: pl.BlockSpec((1, tm, w), lambda b, i: (b, i, 0))
    sub = lambda dil: pl.BlockSpec((1, dil, tm // dil, DIL_GROUP_WIDTH), lambda b, i: (b, 0, i, 0))
    per_b = pl.BlockSpec((1, 1, d), lambda b, i: (b, 0, 0))
    full = lambda a: pl.BlockSpec(a.shape, lambda b, i: (0,) * a.ndim)
    ln_g2, ln_b2 = ln_g.reshape(1, d), ln_b.reshape(1, d)
    sub_rows = min(MERGE_SUB_ROWS, tm)
    assert tm % sub_rows == 0
    utri = jnp.asarray(np.triu(np.ones((sub_rows, sub_rows), np.float32), 1), BF16)
    return pl.pallas_call(
        functools.partial(_merge_kernel, alpha, dilations),
        out_shape=[jax.ShapeDtypeStruct((bsz, s, d), F32), jax.ShapeDtypeStruct((bsz, s, d // 2), jnp.int32),
                   jax.ShapeDtypeStruct((bsz, ROUTE_ROWS, s), jnp.int32), jax.ShapeDtypeStruct((bsz, s, LANES), F32),
                   jax.ShapeDtypeStruct((ROUTER_ROWS, LANES), jnp.int32)],
        grid=(bsz, s // tm),
        in_specs=[row(y_gla.shape[-1])] + [sub(dil) for dil in dilations] * 2
                 + [row(d), row(d), row(d), per_b, per_b, per_b, full(ln_g2), full(ln_b2),
                    full(wpg), full(wpa), full(wout), full(wr), full(br), full(utri)],
        out_specs=[row(d), row(d // 2), pl.BlockSpec((1, ROUTE_ROWS, tm), lambda b, i: (b, 0, i)), row(LANES),
                   pl.BlockSpec((ROUTER_ROWS, LANES), lambda b, i: (0, 0))],
        scratch_shapes=[pltpu.VMEM((2 * DIL_GROUPS, DIL_GROUP_WIDTH // LANES, tm, LANES), F32),
                        pltpu.VMEM((ROUTER_ROWS, 1), F32)],
        compiler_params=_cparams(("arbitrary", "arbitrary")),
        name="merge_ln1_router",
    )(y_gla, *o_groups, *lse_groups, g_gla, g_att, x, g1, sc2, sh2, ln_g2, ln_b2, wpg, wpa, wout, wr, br, utri)


def _expert_kernel(run_ref, valid_ref, rexp_ref, used_ref, x_ref, wg_hbm, wu_hbm, wd_hbm, o_ref,
                   wg_f, wu_f, wd_f, wg_s, wu_s, wd_s, sem):
    t = pl.program_id(0)
    n_tiles_used, n_runs = used_ref[0], used_ref[1]
    run = run_ref[t]
    active = t < n_tiles_used
    first_of_run = (t == 0) | (run_ref[jnp.maximum(t - 1, 0)] != run)

    def weight_copies(r):
        e, slot = rexp_ref[r], r % 2
        return [pltpu.make_async_copy(hbm.at[e], buf.at[slot], sem.at[slot, j])
                for j, (hbm, buf) in enumerate(((wg_hbm, wg_f), (wu_hbm, wu_f), (wd_hbm, wd_f)))]

    @pl.when(active & (t == 0))
    def _():
        for cp in weight_copies(0):
            cp.start()

    @pl.when(active & first_of_run)
    def _():
        @pl.when(run + 1 < n_runs)
        def _():
            for cp in weight_copies(run + 1):
                cp.start()

        for cp in weight_copies(run):
            cp.wait()
        slot = run % 2
        wg_s[...] = wg_f[slot].astype(BF16)
        wu_s[...] = wu_f[slot].astype(BF16)
        wd_s[...] = wd_f[slot].astype(BF16)

    n_valid = jnp.where(active, valid_ref[t], 0)
    def ffn(rows):
        xt = _unpack_bf16_pairs(x_ref[rows, :]).astype(BF16)
        hg = jnp.dot(xt, wg_s[...], preferred_element_type=F32)
        hu = jnp.dot(xt, wu_s[...], preferred_element_type=F32)
        h = (_silu(hg) * hu).astype(BF16)
        o_ref[rows, :] = _pack_bf16_pairs(jnp.dot(h, wd_s[...], preferred_element_type=F32))

    def zero(rows):
        o_ref[rows, :] = jnp.zeros((rows.stop - rows.start, o_ref.shape[1]), o_ref.dtype)

    tm = x_ref.shape[0]
    n_blocks = tm // EXPERT_BLOCK
    for k in range(n_blocks + 1):
        lo, hi = (k - 1) * EXPERT_BLOCK, k * EXPERT_BLOCK

        @pl.when((n_valid > lo) & (n_valid <= hi) if 0 < k < n_blocks else (n_valid > lo if k else n_valid <= 0))
        def _():
            if k:
                ffn(slice(0, hi))
            if k < n_blocks:
                zero(slice(hi, tm))


def _expert_ffn(tile_run, tile_valid, run_expert, used, xg, w_gate, w_up, w_down):
    p = xg.shape[0]
    ne, d, ff = w_gate.shape
    tm = EXPERT_TILE
    n_tiles = p // tm
    hbm = pl.BlockSpec(memory_space=pl.ANY)
    grid_spec = pltpu.PrefetchScalarGridSpec(
        num_scalar_prefetch=4,
        grid=(n_tiles,),
        in_specs=[pl.BlockSpec((tm, d // 2), lambda t, *_: (t, 0)), hbm, hbm, hbm],
        out_specs=pl.BlockSpec((tm, d // 2), lambda t, *_: (t, 0)),
        scratch_shapes=[pltpu.VMEM((2, d, ff), F32), pltpu.VMEM((2, d, ff), F32), pltpu.VMEM((2, ff, d), F32),
                        pltpu.VMEM((d, ff), BF16), pltpu.VMEM((d, ff), BF16), pltpu.VMEM((ff, d), BF16),
                        pltpu.SemaphoreType.DMA((2, 3))],
    )
    return pl.pallas_call(
        _expert_kernel,
        out_shape=jax.ShapeDtypeStruct((p, d // 2), jnp.int32),
        grid_spec=grid_spec,
        compiler_params=_cparams(("arbitrary",)),
        name="expert_ffn",
    )(tile_run, tile_valid, run_expert, used, xg, w_gate, w_up, w_down)


def _final_kernel(alpha, x1_ref, ya_ref, yb_ref, ew_ref, g2_ref, ln_g_ref, ln_b_ref, *refs):
    o_ref = refs[-1]
    ew = ew_ref[0]
    y = ew[:, 0:1] * _unpack_bf16_pairs(ya_ref[0]) + ew[:, 1:2] * _unpack_bf16_pairs(yb_ref[0])
    o_ref[0] = _layer_norm(alpha * x1_ref[0] + g2_ref[0] * y, ln_g_ref[...], ln_b_ref[...])


def _final(alpha, x1, ya, yb, ew, g2, ln_g, ln_b, batch0, carried):
    bsz, s, d = x1.shape
    nb = ya.shape[0]
    tm = min(FINAL_TILE, s)
    whole = lambda w: pl.BlockSpec((1, tm, w), lambda b, i: (b + batch0, i, 0))
    part = lambda w: pl.BlockSpec((1, tm, w), lambda b, i: (b, i, 0))
    full = lambda a: pl.BlockSpec(a.shape, lambda b, i: (0,) * a.ndim)
    ln_g2, ln_b2 = ln_g.reshape(1, d), ln_b.reshape(1, d)
    in_specs = [whole(d), part(d // 2), part(d // 2), whole(LANES),
                pl.BlockSpec((1, 1, d), lambda b, i: (b + batch0, 0, 0)), full(ln_g2), full(ln_b2)]
    args = [x1, ya, yb, ew, g2, ln_g2, ln_b2]
    aliases = {}
    if carried is not None:
        in_specs.append(pl.BlockSpec(memory_space=pl.ANY))
        aliases = {len(args): 0}
        args.append(carried)
    return pl.pallas_call(
        functools.partial(_final_kernel, alpha),
        out_shape=jax.ShapeDtypeStruct((bsz, s, d), F32),
        grid=(nb, s // tm),
        in_specs=in_specs,
        out_specs=whole(d),
        input_output_aliases=aliases,
        compiler_params=_cparams(("parallel", "arbitrary")),
        name="combine_ln2",
    )(*args)


SC_CORES = 2
SC_SUBCORES = 16
SC_CHUNK = 64


def _sc_mesh():
    return plsc.VectorSubcoreMesh(core_axis_name="c", subcore_axis_name="s")


def _sc_scatter_rows(rows, dest0, dest1, n_rows):
    n, w = rows.shape
    n_workers = SC_CORES * SC_SUBCORES
    assert n % (n_workers * SC_CHUNK) == 0
    n_chunks = n // (n_workers * SC_CHUNK)
    d0 = dest0.reshape(n // SC_CHUNK, 1, SC_CHUNK)
    d1 = dest1.reshape(n // SC_CHUNK, 1, SC_CHUNK)

    @functools.partial(
        pl.kernel, mesh=_sc_mesh(), out_type=jax.ShapeDtypeStruct((n_rows, w), rows.dtype),
        scratch_types=[pltpu.VMEM((n_chunks, 1, SC_CHUNK), jnp.int32), pltpu.VMEM((n_chunks, 1, SC_CHUNK), jnp.int32),
                       pltpu.VMEM((2, SC_CHUNK, w), rows.dtype),
                       pltpu.SemaphoreType.DMA((2,)), pltpu.SemaphoreType.DMA((2, 2))])
    def scatter_kernel(rows_hbm, d0_hbm, d1_hbm, out_hbm, i0_v, i1_v, rows_v, read_sem, scat_sem):
        wid = lax.axis_index("s") * SC_CORES + lax.axis_index("c")
        first = wid * n_chunks
        pltpu.sync_copy(d0_hbm.at[pl.ds(first, n_chunks)], i0_v)
        pltpu.sync_copy(d1_hbm.at[pl.ds(first, n_chunks)], i1_v)

        def read(j):
            return pltpu.make_async_copy(rows_hbm.at[pl.ds((first + j) * SC_CHUNK, SC_CHUNK)], rows_v.at[j % 2],
                                         read_sem.at[j % 2])

        def scatters(j):
            return [pltpu.make_async_copy(rows_v.at[j % 2], out_hbm.at[idx.at[j].at[0]], scat_sem.at[j % 2, k])
                    for k, idx in enumerate((i0_v, i1_v))]

        read(0).start()
        for j in range(n_chunks):
            read(j).wait()
            if j + 1 < n_chunks:
                if j >= 1:
                    for cp in scatters(j - 1):
                        cp.wait()
                read(j + 1).start()
            for cp in scatters(j):
                cp.start()
        for j in range(max(n_chunks - 2, 0), n_chunks):
            for cp in scatters(j):
                cp.wait()

    return scatter_kernel(rows, d0, d1)


def _sc_gather_rows(table, dest0, dest1):
    n = dest0.shape[0]
    w = table.shape[1]
    n_workers = SC_CORES * SC_SUBCORES
    assert n % (n_workers * SC_CHUNK) == 0
    n_chunks = n // (n_workers * SC_CHUNK)
    d0 = dest0.reshape(n // SC_CHUNK, 1, SC_CHUNK)
    d1 = dest1.reshape(n // SC_CHUNK, 1, SC_CHUNK)
    out = jax.ShapeDtypeStruct((n, w), table.dtype)

    @functools.partial(
        pl.kernel, mesh=_sc_mesh(), out_type=(out, out),
        scratch_types=[pltpu.VMEM((n_chunks, 1, SC_CHUNK), jnp.int32), pltpu.VMEM((n_chunks, 1, SC_CHUNK), jnp.int32),
                       pltpu.VMEM((2, SC_CHUNK, w), table.dtype),
                       pltpu.SemaphoreType.DMA((2,)), pltpu.SemaphoreType.DMA((2,))])
    def gather_kernel(table_hbm, d0_hbm, d1_hbm, a_hbm, b_hbm, i0_v, i1_v, rows_v, gather_sem, write_sem):
        wid = lax.axis_index("s") * SC_CORES + lax.axis_index("c")
        first = wid * n_chunks
        pltpu.sync_copy(d0_hbm.at[pl.ds(first, n_chunks)], i0_v)
        pltpu.sync_copy(d1_hbm.at[pl.ds(first, n_chunks)], i1_v)
        n_items = 2 * n_chunks

        def gather(m):
            idx = (i0_v, i1_v)[m % 2]
            return pltpu.make_async_copy(table_hbm.at[idx.at[m // 2].at[0]], rows_v.at[m % 2], gather_sem.at[m % 2])

        def write(m):
            o_hbm = (a_hbm, b_hbm)[m % 2]
            return pltpu.make_async_copy(rows_v.at[m % 2], o_hbm.at[pl.ds((first + m // 2) * SC_CHUNK, SC_CHUNK)],
                                         write_sem.at[m % 2])

        gather(0).start()
        for m in range(n_items):
            gather(m).wait()
            if m + 1 < n_items:
                if m >= 1:
                    write(m - 1).wait()
                gather(m + 1).start()
            write(m).start()
        for m in range(max(n_items - 2, 0), n_items):
            write(m).wait()

    return gather_kernel(table, d0, d1)


def _dispatch_plan(route, counts):
    tm = EXPERT_TILE
    e0, e1, r0, r1 = (route[:, j, :].reshape(-1) for j in range(4))
    experts = jnp.arange(MOE_TOTAL, dtype=jnp.int32)
    tiles_per = (counts + tm - 1) // tm
    tile_end = jnp.cumsum(tiles_per)
    pad_start = ((tile_end - tiles_per) * tm).astype(jnp.int32)

    def lookup(e):
        return jnp.sum(jnp.where(e[None, :] == experts[:, None], pad_start[:, None], 0), axis=0)

    dest0, dest1 = lookup(e0) + r0, lookup(e1) + r1
    n_tiles = (2 * e0.size + MOE_TOTAL * tm) // tm
    tile_expert = jnp.minimum(jnp.sum(tile_end[None, :] <= jnp.arange(n_tiles)[:, None], axis=1), MOE_TOTAL - 1)
    nonempty = counts > 0
    run_of_expert = jnp.cumsum(nonempty.astype(jnp.int32)) - 1
    run_expert = jnp.sum(jnp.where(nonempty[None, :] & (run_of_expert[None, :] == experts[:, None]),
                                   experts[None, :], 0), axis=1).astype(jnp.int32)
    of_tile = tile_expert[:, None] == experts[None, :]
    tile_run = jnp.sum(jnp.where(of_tile, run_of_expert[None, :], 0), axis=1).astype(jnp.int32)
    rows_left = (counts + pad_start)[None, :] - jnp.arange(n_tiles)[:, None] * tm
    tile_valid = jnp.clip(jnp.sum(jnp.where(of_tile, rows_left, 0), axis=1), 0, tm).astype(jnp.int32)
    used = jnp.stack([tile_end[-1], jnp.sum(nonempty)]).astype(jnp.int32)
    return dest0, dest1, tile_run, tile_valid, run_expert, used, n_tiles * tm


def _layer(x, c, rel_bias, w_ada, b_ada, w_in, w_gla_gate, b_gla_gate, gla_norm, w_proj_gla, w_proj_attn, w_out,
           ln1_g, ln1_b, w_rg, b_rg, w_re, b_re, w_eg, w_eu, w_ed, ln2_g, ln2_b):
    bsz, s, d = x.shape
    alpha = (2.0 * DEPTH) ** 0.25
    mods = _ada_mods(c, w_ada, b_ada)
    sh1, sc1, g1, sh2, sc2, g2 = [m.reshape(bsz, 1, d) for m in jnp.split(mods, N_MOD, axis=-1)]

    lr0 = d // 2 * 2 + 2 * d
    z = _in_projection(x, sc1, sh1, _prep_in_weight(w_in, lr0), w_gla_gate, b_gla_gate)

    y_gla = _gla(z["q_in"], z["k_in"], z["q_st"], z["k_st"], z["dec"], z["v_gla"], z["r_gla"], gla_norm)

    o_groups, lse_groups = [], []
    for g, (window, dilation) in enumerate(DIL_PATTERNS):
        l = s // dilation
        qg, kg, vg = (z[f"{n}{g}"].reshape(bsz * dilation, l, DIL_GROUP_WIDTH) for n in ("q_att", "k_att", "v_att"))
        table = rel_bias[:, g * DIL_HEADS_PER_GROUP:(g + 1) * DIL_HEADS_PER_GROUP]
        o, lse = _dilated_group_attention(qg, kg, vg, table, window, dilation)
        o_groups.append(o.reshape(bsz, dilation, l, DIL_GROUP_WIDTH))
        lse_groups.append(lse.reshape(bsz, dilation, l, DIL_GROUP_WIDTH))

    wr = jnp.concatenate([w_rg, w_re, jnp.zeros((d, LANES - MOE_GROUPS - MOE_TOTAL), F32)], axis=1)
    br = jnp.concatenate([b_rg, b_re, jnp.zeros((LANES - MOE_GROUPS - MOE_TOTAL,), F32)]).reshape(1, LANES)
    x1, u2, route, ew, cnt = _merge(alpha, y_gla, o_groups, lse_groups, z["g_gla"], z["g_att"], x, g1, sc2, sh2,
                                    ln1_g, ln1_b, w_proj_gla.astype(BF16), w_proj_attn.astype(BF16),
                                    w_out.astype(BF16), wr, br)

    n = bsz * s
    counts = cnt[MOE_GROUPS:MOE_GROUPS + MOE_TOTAL, 0]
    dest0, dest1, tile_run, tile_valid, run_expert, used, n_rows = _dispatch_plan(route, counts)
    xg = _sc_scatter_rows(u2.reshape(n, d // 2), dest0, dest1, n_rows)
    ff = w_eg.shape[-1]
    yo = _expert_ffn(tile_run, tile_valid, run_expert, used, xg, w_eg.reshape(MOE_TOTAL, d, ff),
                     w_eu.reshape(MOE_TOTAL, d, ff), w_ed.reshape(MOE_TOTAL, ff, d))
    halves = 2 if bsz % 2 == 0 and (n // 2) % (SC_CORES * SC_SUBCORES * SC_CHUNK) == 0 else 1
    nb, out = bsz // halves, None
    for hf in range(halves):
        tok = slice(hf * nb * s, (hf + 1) * nb * s)
        ya, yb = (y.reshape(nb, s, d // 2) for y in _sc_gather_rows(yo, dest0[tok], dest1[tok]))
        out = _final(alpha, x1, ya, yb, ew, g2, ln2_g, ln2_b, hf * nb, out)
    return out


def kernel(x, c, rel_bias, w_ada, b_ada, w_in, w_gla_gate, b_gla_gate, gla_norm, w_proj_gla, w_proj_attn, w_out,
           ln1_g, ln1_b, w_router_group, b_router_group, w_router_expert, b_router_expert, w_exp_gate, w_exp_up,
           w_exp_down, ln2_g, ln2_b):
    assert w_ada.shape[0] == DEPTH
    return _layer(x, c, rel_bias, w_ada[0], b_ada[0], w_in[0:1], w_gla_gate[0], b_gla_gate[0], gla_norm[0],
                  w_proj_gla[0], w_proj_attn[0], w_out[0], ln1_g[0], ln1_b[0], w_router_group[0],
                  b_router_group[0], w_router_expert[0], b_router_expert[0], w_exp_gate[0], w_exp_up[0],
                  w_exp_down[0], ln2_g[0], ln2_b[0])
```

```python
import functools
import math

import numpy as np
import jax
import jax.numpy as jnp
from jax import lax
from jax.experimental import pallas as pl
from jax.experimental.pallas import tpu as pltpu
from jax.experimental.pallas import tpu_sc as plsc

F32 = jnp.float32
BF16 = jnp.bfloat16

N_MOD = 6
GLA_HEADS = 4
GLA_LOWRANK = 16
GLA_TAU = 16.0
GLA_CHUNK = 64
DIL_PATTERNS = ((128, 1), (512, 4), (2048, 16))
DIL_GROUPS = len(DIL_PATTERNS)
DIL_HEADS_PER_GROUP = 8
DIL_HEAD_DIM = 64
DIL_GROUP_WIDTH = DIL_HEADS_PER_GROUP * DIL_HEAD_DIM
DIL_BLOCK = 128
REL_BUCKETS = 32
REL_MAX_DIST = 2048
MOE_GROUPS = 4
MOE_EXPERTS = 8
MOE_TOTAL = MOE_GROUPS * MOE_EXPERTS
LN_EPS = 1e-5
DEPTH = 1

LANES = 128
VMEM_LIMIT = 56 * 1024 * 1024
LOG2E = 1.4426950408889634
LN2 = 0.6931471805599453
NEG = -1e30
ROW_TILE = 512
PROJ_CHUNK = 512
FINAL_TILE = 1024
EXPERT_TILE = 512
EXPERT_BLOCK = 128
GLA_STEP_CHUNKS = 16
ATT_STEP_BLOCKS = 8
ROUTER_ROWS = 40
ROUTE_ROWS = 8

NT_DIMS = (((1,), (1,)), ((), ()))
TN_DIMS = (((0,), (0,)), ((), ()))


def _cparams(sem):
    return pltpu.CompilerParams(dimension_semantics=sem, vmem_limit_bytes=VMEM_LIMIT)


def _sigmoid(x):
    return 0.5 * jnp.tanh(0.5 * x) + 0.5


def _silu(x):
    return x * _sigmoid(x)


def _layer_norm(x, g, b):
    mu = jnp.mean(x, axis=-1, keepdims=True)
    xc = x - mu
    var = jnp.mean(xc * xc, axis=-1, keepdims=True)
    return xc * lax.rsqrt(var + LN_EPS) * g + b


def _pack_bf16_pairs(x):
    w = x.shape[1] // 2
    lo = lax.bitcast_convert_type(x[:, :w].astype(BF16).astype(F32), jnp.uint32) >> 16
    hi = lax.bitcast_convert_type(x[:, w:].astype(BF16).astype(F32), jnp.uint32) & jnp.uint32(0xFFFF0000)
    return lax.bitcast_convert_type(lo | hi, jnp.int32)


def _unpack_bf16_pairs(p):
    u = lax.bitcast_convert_type(p, jnp.uint32)
    lo = lax.bitcast_convert_type(u << 16, F32)
    hi = lax.bitcast_convert_type(u & jnp.uint32(0xFFFF0000), F32)
    return jnp.concatenate([lo, hi], axis=1)


def _mods_kernel(ct_ref, w_ref, b_ref, o_ref):
    a = _silu(ct_ref[...])
    w = w_ref[...]
    for b in range(a.shape[1]):
        o_ref[b:b + 1, :] = jnp.sum(a[:, b:b + 1] * w, axis=0, keepdims=True) + b_ref[...]


def _ada_mods(c, w, b):
    bsz, d = c.shape
    n = w.shape[1]
    tn = 1536
    assert n % tn == 0
    return pl.pallas_call(
        _mods_kernel,
        out_shape=jax.ShapeDtypeStruct((bsz, n), F32),
        grid=(n // tn,),
        in_specs=[pl.BlockSpec((d, bsz), lambda j: (0, 0)),
                  pl.BlockSpec((d, tn), lambda j: (0, j)),
                  pl.BlockSpec((1, tn), lambda j: (0, j))],
        out_specs=pl.BlockSpec((bsz, tn), lambda j: (0, j)),
        compiler_params=_cparams(("arbitrary",)),
        name="ada_mods",
    )(c.T, w, b.reshape(1, n))


def _proj_pieces(d_model):
    dk = d_model // 2
    pieces = [("q_gla", dk, "scale_q_gla"), ("k_gla", dk, None), ("v_gla", d_model, None), ("r_gla", d_model, "silu")]
    for name, post in (("q_att", "scale_q_att"), ("k_att", None), ("v_att", None)):
        for g, (_, dilation) in enumerate(DIL_PATTERNS):
            pieces.append((f"{name}{g}", DIL_GROUP_WIDTH, (post, dilation)))
    pieces += [("g_gla", d_model, "sigmoid"), ("g_att", d_model, "sigmoid"), ("lr", LANES, "lowrank")]
    return tuple(pieces)


WT_BLOCK = 512


def _wprep_kernel(n_main_blocks, w_ref, o_ref):
    blk = w_ref[0]
    row = lax.broadcasted_iota(jnp.int32, blk.shape, 0)
    keep = (pl.program_id(0) < n_main_blocks) | (row < GLA_LOWRANK)
    o_ref[...] = jnp.where(keep, blk, 0.0).astype(BF16)


def _prep_in_weight(w_in, lr0):
    w_t = jnp.swapaxes(w_in, 1, 2)
    _, n_in, d = w_t.shape
    n_main = n_in - GLA_LOWRANK
    assert lr0 % WT_BLOCK == 0 and n_main % WT_BLOCK == 0
    n_main_blocks = n_main // WT_BLOCK

    def src_row(j):
        start = j * WT_BLOCK
        octet = jnp.where(j < n_main_blocks, (start + jnp.where(start >= lr0, GLA_LOWRANK, 0)) // 8, lr0 // 8)
        return octet * 8

    return pl.pallas_call(
        functools.partial(_wprep_kernel, n_main_blocks),
        out_shape=jax.ShapeDtypeStruct((n_main + WT_BLOCK, d), BF16),
        grid=(n_main_blocks + 1,),
        in_specs=[pl.BlockSpec((pl.Element(1), pl.Element(WT_BLOCK), pl.Element(d)), lambda j: (0, src_row(j), 0))],
        out_specs=pl.BlockSpec((WT_BLOCK, d), lambda j: (j, 0)),
        compiler_params=_cparams(("parallel",)),
        name="prep_in_weight",
    )(w_t)


GLA_HELD = ("lr", "q_gla", "k_gla")


def _gla_operands(hold, wg_ref, bg_ref, qin_ref, kin_ref, qst_ref, kst_ref, dec_ref):
    c = GLA_CHUNK
    tm = hold["q_gla"].shape[0]
    tril = (lax.broadcasted_iota(jnp.int32, (c, c), 0) >= lax.broadcasted_iota(jnp.int32, (c, c), 1)).astype(BF16)
    mid = c // 2 - 1
    lr, wg = hold["lr"][:, 0:GLA_LOWRANK], wg_ref[...]
    lr_hi, wg_hi = lr.astype(BF16), wg.astype(BF16)
    lr_lo, wg_lo = (lr - lr_hi.astype(F32)).astype(BF16), (wg - wg_hi.astype(F32)).astype(BF16)
    gate_in = (jnp.dot(lr_hi, wg_hi, preferred_element_type=F32) + jnp.dot(lr_lo, wg_hi, preferred_element_type=F32)
               + jnp.dot(lr_hi, wg_lo, preferred_element_type=F32)) + bg_ref[...]
    g_all = (jnp.minimum(gate_in, 0.0) - jnp.log(1.0 + jnp.exp(-jnp.abs(gate_in)))) * (1.0 / GLA_TAU)
    g_hi = g_all.astype(BF16)
    g_lo = (g_all - g_hi.astype(F32)).astype(BF16)
    for ci in range(tm // c):
        rows = slice(ci * c, (ci + 1) * c)
        bc = jnp.dot(tril, g_hi[rows], preferred_element_type=F32) + jnp.dot(tril, g_lo[rows], preferred_element_type=F32)
        b_mid = bc[mid:mid + 1, :]
        b_last = bc[c - 1:c, :]
        qf = hold["q_gla"][rows, :]
        kf = hold["k_gla"][rows, :]
        q_in = qf * jnp.exp(bc - b_mid)
        k_in = kf * jnp.exp(b_mid - bc)
        qin_ref[0, rows, :] = q_in.astype(BF16)
        kin_ref[0, rows, :] = k_in.astype(BF16)
        qst_ref[0, rows, :] = (q_in * jnp.exp(b_mid)).astype(BF16)
        kst_ref[0, rows, :] = (k_in * jnp.exp(b_last - b_mid)).astype(BF16)
        dec_ref[0, ci:ci + 1, :] = jnp.exp(b_last)


def _proj_kernel(pieces, head_k, x_ref, sc_ref, sh_ref, w_ref, wg_ref, bg_ref, *refs):
    n_out = len(pieces) - len(GLA_HELD)
    out_refs = dict(zip([p[0] for p in pieces if p[0] not in GLA_HELD], refs[:n_out]))
    gla_out_refs = refs[n_out:n_out + 5]
    stage_ref = refs[n_out + 5]
    hold = dict(zip(GLA_HELD, refs[n_out + 6:]))
    tm = x_ref.shape[1]
    u = (x_ref[0] * (1.0 + sc_ref[0]) + sh_ref[0]).astype(BF16)
    offsets, off = {}, 0
    for name, width, _ in pieces:
        offsets[name] = off
        off += width
    by_name = {p[0]: p for p in pieces}
    held = [(by_name[n], 0) for n in GLA_HELD]
    rest = [(p, c0) for p in pieces if p[0] not in GLA_HELD for c0 in range(0, p[1], min(p[1], PROJ_CHUNK))]
    for n, (piece, c0) in enumerate(held + rest):
        if n == len(held):
            _gla_operands(hold, wg_ref, bg_ref, *gla_out_refs)
        name, width, post = piece
        o_ref = hold[name] if name in GLA_HELD else out_refs[name]
        off = offsets[name]
        chunk = min(width, PROJ_CHUNK)
        acc = lax.dot_general(u, w_ref[off + c0:off + c0 + chunk, :], NT_DIMS, preferred_element_type=F32)
        if post == "silu":
            acc = _silu(acc)
        elif post == "sigmoid":
            acc = _sigmoid(acc)
        elif post == "scale_q_gla":
            acc = acc * (head_k ** -0.5)
        if name in GLA_HELD:
            o_ref[...] = acc
        elif isinstance(post, tuple):
            scale, dilation = post
            if scale is not None:
                acc = acc * (DIL_HEAD_DIM ** -0.5 * LOG2E)
            if dilation == 1:
                o_ref[0, 0] = acc.astype(o_ref.dtype)
            else:
                for t in range(width // LANES):
                    stage_ref[t] = acc[:, t * LANES:(t + 1) * LANES]
                for r in range(dilation):
                    for t in range(width // LANES):
                        o_ref[0, r, :, t * LANES:(t + 1) * LANES] = stage_ref[
                            t, pl.ds(r, tm // dilation, stride=dilation), :].astype(o_ref.dtype)
        else:
            o_ref[0, :, c0:c0 + chunk] = acc.astype(o_ref.dtype)


def _in_projection(x, sc1, sh1, w_perm, w_gate, b_gate):
    bsz, s, d = x.shape
    pieces = _proj_pieces(d)
    assert sum(p[1] for p in pieces) <= w_perm.shape[0]
    tm = min(ROW_TILE, s)
    assert s % tm == 0 and tm % (8 * GLA_CHUNK) == 0
    dk = d // 2
    head_k = dk // GLA_HEADS
    out_shape, out_specs = [], []
    for name, width, post in pieces:
        if name in GLA_HELD:
            continue
        if isinstance(post, tuple):
            dil = post[1]
            assert tm % (dil * 16) == 0
            out_shape.append(jax.ShapeDtypeStruct((bsz, dil, s // dil, width), BF16))
            out_specs.append(pl.BlockSpec((1, dil, tm // dil, width), lambda b, i: (b, 0, i, 0)))
        else:
            out_shape.append(jax.ShapeDtypeStruct((bsz, s, width), BF16))
            out_specs.append(pl.BlockSpec((1, tm, width), lambda b, i: (b, i, 0)))
    row = lambda w: pl.BlockSpec((1, tm, w), lambda b, i: (b, i, 0))
    gla_names = ("q_in", "k_in", "q_st", "k_st", "dec")
    out_shape += [jax.ShapeDtypeStruct((bsz, s, dk), BF16)] * 4 + [jax.ShapeDtypeStruct((bsz, s // GLA_CHUNK, dk), F32)]
    out_specs += [row(dk)] * 4 + [pl.BlockSpec((1, tm // GLA_CHUNK, dk), lambda b, i: (b, i, 0))]
    bg = b_gate.reshape(1, dk)
    full = lambda a: pl.BlockSpec(a.shape, lambda b, i: (0,) * a.ndim)
    outs = pl.pallas_call(
        functools.partial(_proj_kernel, pieces, head_k),
        out_shape=out_shape,
        grid=(bsz, s // tm),
        in_specs=[row(d),
                  pl.BlockSpec((1, 1, d), lambda b, i: (b, 0, 0)),
                  pl.BlockSpec((1, 1, d), lambda b, i: (b, 0, 0)),
                  pl.BlockSpec(w_perm.shape, lambda b, i: (0, 0), pipeline_mode=pl.Buffered(1)),
                  full(w_gate), full(bg)],
        out_specs=out_specs,
        scratch_shapes=[pltpu.VMEM((DIL_GROUP_WIDTH // LANES, tm, LANES), F32),
                        pltpu.VMEM((tm, LANES), F32), pltpu.VMEM((tm, dk), F32), pltpu.VMEM((tm, dk), F32)],
        compiler_params=_cparams(("parallel", "arbitrary")),
        name="in_projection",
    )(x, sc1, sh1, w_perm, w_gate, bg)
    return dict(zip([p[0] for p in pieces if p[0] not in GLA_HELD] + list(gla_names), outs))


def _gla_kernel(n_chunks, head_k, head_v, qin_ref, kin_ref, qst_ref, kst_ref, dec_ref, v_ref, r_ref, ng_ref, o_ref,
                state_ref):
    @pl.when(pl.program_id(1) == 0)
    def _():
        state_ref[...] = jnp.zeros_like(state_ref)

    c = GLA_CHUNK
    causal = lax.broadcasted_iota(jnp.int32, (c, c), 0) >= lax.broadcasted_iota(jnp.int32, (c, c), 1)
    for ci in range(n_chunks):
        rows = slice(ci * c, (ci + 1) * c)
        for h in range(GLA_HEADS):
            ks = slice(h * head_k, (h + 1) * head_k)
            vs = slice(h * head_v, (h + 1) * head_v)
            vh = v_ref[0, rows, vs]
            att = lax.dot_general(qin_ref[0, rows, ks], kin_ref[0, rows, ks], NT_DIMS, preferred_element_type=F32)
            att = jnp.where(causal, att, 0.0).astype(BF16)
            st = state_ref[h]
            o = jnp.dot(att, vh, preferred_element_type=F32)
            o = o + lax.dot_general(qst_ref[0, rows, ks], st.astype(BF16), NT_DIMS, preferred_element_type=F32)
            kv_t = lax.dot_general(vh, kst_ref[0, rows, ks], TN_DIMS, preferred_element_type=F32)
            state_ref[h] = st * dec_ref[0, ci:ci + 1, ks] + kv_t
            ms = jnp.mean(o * o, axis=-1, keepdims=True)
            o = o * lax.rsqrt(ms + LN_EPS) * ng_ref[:, vs] * r_ref[0, rows, vs].astype(F32)
            o_ref[0, rows, vs] = o.astype(o_ref.dtype)


def _gla(q_in, k_in, q_st, k_st, dec, v, r_silu, norm_g):
    bsz, s, dk = q_in.shape
    dv = v.shape[-1]
    head_k, head_v = dk // GLA_HEADS, dv // GLA_HEADS
    n_chunks = min(GLA_STEP_CHUNKS, s // GLA_CHUNK)
    ct = GLA_CHUNK * n_chunks
    assert s % ct == 0
    row_spec = lambda w: pl.BlockSpec((1, ct, w), lambda b, i: (b, i, 0))
    full = lambda a: pl.BlockSpec(a.shape, lambda b, i: (0,) * a.ndim)
    ng = norm_g.reshape(1, dv)
    return pl.pallas_call(
        functools.partial(_gla_kernel, n_chunks, head_k, head_v),
        out_shape=jax.ShapeDtypeStruct((bsz, s, dv), BF16),
        grid=(bsz, s // ct),
        in_specs=[row_spec(dk)] * 4 + [pl.BlockSpec((1, n_chunks, dk), lambda b, i: (b, i, 0)),
                                       row_spec(dv), row_spec(dv), full(ng)],
        out_specs=row_spec(dv),
        scratch_shapes=[pltpu.VMEM((GLA_HEADS, head_v, head_k), F32)],
        compiler_params=_cparams(("parallel", "arbitrary")),
        name="gla",
    )(q_in, k_in, q_st, k_st, dec, v, r_silu, ng)


def _t5_bucket_np(dist):
    exact = REL_BUCKETS // 2
    d = np.maximum(dist, 1).astype(np.float32)
    large = exact + (np.log(d / np.float32(exact)) / np.float32(math.log(REL_MAX_DIST / exact))
                     * np.float32(REL_BUCKETS - exact)).astype(np.int32)
    large = np.minimum(large, REL_BUCKETS - 1)
    return np.where(dist < exact, dist, large).astype(np.int32)


def _band_tables(window, dilation):
    qi = np.arange(DIL_BLOCK)[:, None]
    kj = np.arange(2 * DIL_BLOCK)[None, :]
    m = qi + DIL_BLOCK - kj
    n_steps = window // dilation
    band = (m >= 0) & (m <= n_steps)
    bucket = _t5_bucket_np(np.clip(m, 0, n_steps) * dilation)
    return np.where(band, bucket, -1).astype(np.int32)


def _attn_kernel(nq, table_ref, bucket_ref, q_ref, kp_ref, kc_ref, vp_ref, vc_ref, o_ref, lse_ref,
                 bias_ref, p_ref):
    i = pl.program_id(1)
    blk = DIL_BLOCK
    hpg = DIL_HEADS_PER_GROUP
    n_pairs = hpg // 2

    @pl.when((pl.program_id(0) == 0) & (i == 0))
    def _():
        bucket = bucket_ref[...]
        for h in range(hpg):
            acc = jnp.full(bucket.shape, NEG, F32)
            for bkt in range(REL_BUCKETS):
                acc = jnp.where(bucket == bkt, table_ref[bkt, h] * LOG2E, acc)
            bias_ref[h * blk:(h + 1) * blk, :] = acc

    lane = lax.broadcasted_iota(jnp.int32, (blk, LANES), 1)
    low = lane < DIL_HEAD_DIM
    ones_rhs = jnp.ones((2 * blk, LANES), BF16)

    def windows(ref_p, ref_c, sq, qb, cols):
        if qb == 0:
            return jnp.concatenate([ref_p[sq, :, cols], ref_c[sq, 0:blk, cols]], axis=0)
        return ref_c[sq, (qb - 1) * blk:(qb + 1) * blk, cols]

    key_lane = lax.broadcasted_iota(jnp.int32, (1, 2 * blk), 1)
    no_prev = jnp.where((key_lane < blk) & (i == 0), NEG, 0.0)
    items = [(sq, qb, hp) for sq in range(q_ref.shape[0]) for qb in range(nq) for hp in range(n_pairs)]

    mxs = []
    for n, (sq, qb, hp) in enumerate(items):
        rows = slice(qb * blk, (qb + 1) * blk)
        cols = slice(hp * LANES, (hp + 1) * LANES)
        qp = q_ref[sq, rows, cols]
        zero = jnp.zeros_like(qp)
        qq = jnp.concatenate([jnp.where(low, qp, zero), jnp.where(low, zero, qp)], axis=0)
        keys = windows(kp_ref, kc_ref, sq, qb, cols)
        s = lax.dot_general(qq, keys, NT_DIMS, preferred_element_type=F32) + bias_ref[2 * hp * blk:(2 * hp + 2) * blk, :]
        if qb == 0:
            s = s + no_prev
        mx = jnp.max(s, axis=-1, keepdims=True)
        p_ref[n * 2 * blk:(n + 1) * 2 * blk, :] = jnp.exp2(s - mx).astype(BF16)
        mxs.append(mx)

    for n, (sq, qb, hp) in enumerate(items):
        rows = slice(qb * blk, (qb + 1) * blk)
        cols = slice(hp * LANES, (hp + 1) * LANES)
        vals = windows(vp_ref, vc_ref, sq, qb, cols)
        rhs = jnp.concatenate([vals, ones_rhs], axis=1)
        res = jnp.dot(p_ref[n * 2 * blk:(n + 1) * 2 * blk, :], rhs, preferred_element_type=F32)
        num = jnp.where(low, res[0:blk, 0:LANES], res[blk:2 * blk, 0:LANES])
        den = jnp.where(low, res[0:blk, LANES:], res[blk:2 * blk, LANES:])
        mx = jnp.where(low, mxs[n][0:blk], mxs[n][blk:2 * blk])
        o_ref[sq, rows, cols] = (num / den).astype(o_ref.dtype)
        lse_ref[sq, rows, cols] = (mx + jnp.log2(den)) * LN2


def _dilated_group_attention(q, k, v, table, window, dilation):
    bb, l, w = q.shape
    nq = min(ATT_STEP_BLOCKS, l // DIL_BLOCK)
    nsq = ATT_STEP_BLOCKS // nq
    assert l % (nq * DIL_BLOCK) == 0 and bb % nsq == 0
    steps = l // (nq * DIL_BLOCK)
    bucket = jnp.asarray(_band_tables(window, dilation))
    cur = pl.BlockSpec((nsq, nq * DIL_BLOCK, w), lambda b, i: (b, i, 0))
    prev = pl.BlockSpec((nsq, DIL_BLOCK, w), lambda b, i: (b, jnp.maximum(nq * i - 1, 0), 0))
    rows_all = nsq * nq * DIL_HEADS_PER_GROUP * DIL_BLOCK
    return pl.pallas_call(
        functools.partial(_attn_kernel, nq),
        out_shape=[jax.ShapeDtypeStruct((bb, l, w), BF16), jax.ShapeDtypeStruct((bb, l, w), F32)],
        grid=(bb // nsq, steps),
        in_specs=[pl.BlockSpec(memory_space=pltpu.SMEM),
                  pl.BlockSpec(bucket.shape, lambda b, i: (0, 0)),
                  cur, prev, cur, prev, cur],
        out_specs=[cur, cur],
        scratch_shapes=[pltpu.VMEM((DIL_HEADS_PER_GROUP * DIL_BLOCK, 2 * DIL_BLOCK), F32),
                        pltpu.VMEM((rows_all, 2 * DIL_BLOCK), BF16)],
        compiler_params=_cparams(("arbitrary", "arbitrary")),
        name=f"dilated_attn_d{dilation}",
    )(table, bucket, q, k, k, v, v)


def _merge_kernel(alpha, dilations, ygla_ref, o0_ref, o1_ref, o2_ref, l0_ref, l1_ref, l2_ref, gg_ref, ga_ref, x_ref,
                  g1_ref, sc2_ref, sh2_ref, ln_g_ref, ln_b_ref, wpg_ref, wpa_ref, wout_ref, wr_ref, br_ref, utri_ref,
                  x1_ref, u2_ref, route_ref, ew_ref, cnt_ref, stage_ref, carry_ref):
    tm = x_ref.shape[1]

    @pl.when((pl.program_id(0) == 0) & (pl.program_id(1) == 0))
    def _():
        carry_ref[...] = jnp.zeros_like(carry_ref)

    n_lt = DIL_GROUP_WIDTH // LANES
    group_refs = tuple(zip((l0_ref, l1_ref, l2_ref), (o0_ref, o1_ref, o2_ref), dilations))
    for gi, (l_ref, o_ref, dil) in enumerate(group_refs):
        if dil > 1:
            for slot, ref in ((2 * gi, l_ref), (2 * gi + 1, o_ref)):
                for r in range(dil):
                    for t in range(n_lt):
                        stage_ref[slot, t, pl.ds(r, tm // dil, stride=dil), :] = ref[
                            0, r, :, t * LANES:(t + 1) * LANES].astype(F32)

    def natural(ref, dil, slot):
        if dil == 1:
            return ref[0, 0].astype(F32)
        return jnp.concatenate([stage_ref[slot, t] for t in range(n_lt)], axis=1)

    lses = [natural(l_ref, dil, 2 * gi) for gi, (l_ref, _, dil) in enumerate(group_refs)]
    outs = [natural(o_ref, dil, 2 * gi + 1) for gi, (_, o_ref, dil) in enumerate(group_refs)]
    lm = jnp.maximum(jnp.maximum(lses[0], lses[1]), lses[2])
    es = [jnp.exp(l - lm) for l in lses]
    y_att = (es[0] * outs[0] + es[1] * outs[1] + es[2] * outs[2]) / (es[0] + es[1] + es[2])

    p_gla = jnp.dot(ygla_ref[0], wpg_ref[...], preferred_element_type=F32)
    p_att = jnp.dot(y_att.astype(BF16), wpa_ref[...], preferred_element_type=F32)
    merged = gg_ref[0].astype(F32) * p_gla + ga_ref[0].astype(F32) * p_att
    y = jnp.dot(merged.astype(BF16), wout_ref[...], preferred_element_type=F32)
    x1 = _layer_norm(alpha * x_ref[0] + g1_ref[0] * y, ln_g_ref[...], ln_b_ref[...])
    x1_ref[0] = x1
    u2 = x1 * (1.0 + sc2_ref[0]) + sh2_ref[0]
    u2_ref[0] = _pack_bf16_pairs(u2)

    logits = jnp.dot(u2.astype(BF16), wr_ref[...].astype(BF16), preferred_element_type=F32) + br_ref[...]
    lt = jnp.transpose(logits)[0:ROUTER_ROWS, :]
    rowi = lax.broadcasted_iota(jnp.int32, lt.shape, 0)
    big = jnp.int32(LANES)
    lg = jnp.where(rowi < MOE_GROUPS, lt, NEG)
    gmax = jnp.max(lg, axis=0, keepdims=True)
    gidx = jnp.min(jnp.where(lg == gmax, rowi, big), axis=0, keepdims=True)
    gval = 1.0 / jnp.sum(jnp.exp(lg - gmax), axis=0, keepdims=True)
    first = MOE_GROUPS + gidx * MOE_EXPERTS
    le = jnp.where((rowi >= first) & (rowi < first + MOE_EXPERTS), lt, NEG)
    m1 = jnp.max(le, axis=0, keepdims=True)
    i1 = jnp.min(jnp.where(le == m1, rowi, big), axis=0, keepdims=True)
    le2 = jnp.where(rowi == i1, NEG, le)
    m2 = jnp.max(le2, axis=0, keepdims=True)
    i2 = jnp.min(jnp.where(le2 == m2, rowi, big), axis=0, keepdims=True)
    t = jnp.exp(m2 - m1)
    w1 = 1.0 / (1.0 + t)
    w2 = t * w1

    hit1, hit2 = rowi == i1, rowi == i2
    onehot = jnp.where(hit1 | hit2, 1.0, 0.0)
    earlier = jnp.dot(onehot.astype(BF16), utri_ref[...], preferred_element_type=F32) + carry_ref[...]
    rank1 = jnp.sum(jnp.where(hit1, earlier, 0.0), axis=0, keepdims=True).astype(jnp.int32)
    rank2 = jnp.sum(jnp.where(hit2, earlier, 0.0), axis=0, keepdims=True).astype(jnp.int32)
    carry = carry_ref[...] + jnp.sum(onehot, axis=1, keepdims=True)
    carry_ref[...] = carry
    cnt_ref[...] = jnp.broadcast_to(carry, cnt_ref.shape).astype(jnp.int32)
    r8 = lax.broadcasted_iota(jnp.int32, (ROUTE_ROWS, tm), 0)
    route_ref[0] = jnp.where(r8 == 0, i1 - MOE_GROUPS, jnp.where(r8 == 1, i2 - MOE_GROUPS,
                             jnp.where(r8 == 2, rank1, jnp.where(r8 == 3, rank2, 0))))
    r128 = lax.broadcasted_iota(jnp.int32, (LANES, tm), 0)
    ew_ref[0] = jnp.transpose(jnp.where(r128 == 0, gval * w1, jnp.where(r128 == 1, gval * w2, 0.0)))


def _merge(alpha, y_gla, o_groups, lse_groups, g_gla, g_att, x, g1, sc2, sh2, ln_g, ln_b, wpg, wpa, wout, wr, br):
    bsz, s, d = x.shape
    tm = min(ROW_TILE, s)
    assert s % tm == 0
    dilations = tuple(dil for _, dil in DIL_PATTERNS)
    row = lambda w: pl.BlockSpec((1, tm, w), lambda b, i: (b, i, 0))
    sub = lambda dil: pl.BlockSpec((1, dil, tm // dil, DIL_GROUP_WIDTH), lambda b, i: (b, 0, i, 0))
    per_b = pl.BlockSpec((1, 1, d), lambda b, i: (b, 0, 0))
    full = lambda a: pl.BlockSpec(a.shape, lambda b, i: (0,) * a.ndim)
    ln_g2, ln_b2 = ln_g.reshape(1, d), ln_b.reshape(1, d)
    utri = jnp.asarray(np.triu(np.ones((tm, tm), np.float32), 1), BF16)
    return pl.pallas_call(
        functools.partial(_merge_kernel, alpha, dilations),
        out_shape=[jax.ShapeDtypeStruct((bsz, s, d), F32), jax.ShapeDtypeStruct((bsz, s, d // 2), jnp.int32),
                   jax.ShapeDtypeStruct((bsz, ROUTE_ROWS, s), jnp.int32), jax.ShapeDtypeStruct((bsz, s, LANES), F32),
                   jax.ShapeDtypeStruct((ROUTER_ROWS, LANES), jnp.int32)],
        grid=(bsz, s // tm),
        in_specs=[row(y_gla.shape[-1])] + [sub(dil) for dil in dilations] * 2
                 + [row(d), row(d), row(d), per_b, per_b, per_b, full(ln_g2), full(ln_b2),
                    full(wpg), full(wpa), full(wout), full(wr), full(br), full(utri)],
        out_specs=[row(d), row(d // 2), pl.BlockSpec((1, ROUTE_ROWS, tm), lambda b, i: (b, 0, i)), row(LANES),
                   pl.BlockSpec((ROUTER_ROWS, LANES), lambda b, i: (0, 0))],
        scratch_shapes=[pltpu.VMEM((2 * DIL_GROUPS, DIL_GROUP_WIDTH // LANES, tm, LANES), F32),
                        pltpu.VMEM((ROUTER_ROWS, 1), F32)],
        compiler_params=_cparams(("arbitrary", "arbitrary")),
        name="merge_ln1_router",
    )(y_gla, *o_groups, *lse_groups, g_gla, g_att, x, g1, sc2, sh2, ln_g2, ln_b2, wpg, wpa, wout, wr, br, utri)


def _expert_kernel(run_ref, valid_ref, rexp_ref, used_ref, x_ref, wg_hbm, wu_hbm, wd_hbm, o_ref,
                   wg_f, wu_f, wd_f, wg_s, wu_s, wd_s, sem):
    t = pl.program_id(0)
    n_tiles_used, n_runs = used_ref[0], used_ref[1]
    run = run_ref[t]
    active = t < n_tiles_used
    first_of_run = (t == 0) | (run_ref[jnp.maximum(t - 1, 0)] != run)

    def weight_copies(r):
        e, slot = rexp_ref[r], r % 2
        return [pltpu.make_async_copy(hbm.at[e], buf.at[slot], sem.at[slot, j])
                for j, (hbm, buf) in enumerate(((wg_hbm, wg_f), (wu_hbm, wu_f), (wd_hbm, wd_f)))]

    @pl.when(active & (t == 0))
    def _():
        for cp in weight_copies(0):
            cp.start()

    @pl.when(active & first_of_run)
    def _():
        @pl.when(run + 1 < n_runs)
        def _():
            for cp in weight_copies(run + 1):
                cp.start()

        for cp in weight_copies(run):
            cp.wait()
        slot = run % 2
        wg_s[...] = wg_f[slot].astype(BF16)
        wu_s[...] = wu_f[slot].astype(BF16)
        wd_s[...] = wd_f[slot].astype(BF16)

    n_valid = jnp.where(active, valid_ref[t], 0)

    def ffn(rows):
        xt = _unpack_bf16_pairs(x_ref[rows, :]).astype(BF16)
        hg = jnp.dot(xt, wg_s[...], preferred_element_type=F32)
        hu = jnp.dot(xt, wu_s[...], preferred_element_type=F32)
        h = (_silu(hg) * hu).astype(BF16)
        o_ref[rows, :] = _pack_bf16_pairs(jnp.dot(h, wd_s[...], preferred_element_type=F32))

    def zero(rows):
        o_ref[rows, :] = jnp.zeros((rows.stop - rows.start, o_ref.shape[1]), o_ref.dtype)

    tm = x_ref.shape[0]
    n_blocks = tm // EXPERT_BLOCK
    for k in range(n_blocks + 1):
        lo, hi = (k - 1) * EXPERT_BLOCK, k * EXPERT_BLOCK

        @pl.when((n_valid > lo) & (n_valid <= hi) if 0 < k < n_blocks else (n_valid > lo if k else n_valid <= 0))
        def _():
            if k:
                ffn(slice(0, hi))
            if k < n_blocks:
                zero(slice(hi, tm))


def _expert_ffn(tile_run, tile_valid, run_expert, used, xg, w_gate, w_up, w_down):
    p = xg.shape[0]
    ne, d, ff = w_gate.shape
    tm = EXPERT_TILE
    n_tiles = p // tm
    hbm = pl.BlockSpec(memory_space=pl.ANY)
    grid_spec = pltpu.PrefetchScalarGridSpec(
        num_scalar_prefetch=4,
        grid=(n_tiles,),
        in_specs=[pl.BlockSpec((tm, d // 2), lambda t, *_: (t, 0)), hbm, hbm, hbm],
        out_specs=pl.BlockSpec((tm, d // 2), lambda t, *_: (t, 0)),
        scratch_shapes=[pltpu.VMEM((2, d, ff), F32), pltpu.VMEM((2, d, ff), F32), pltpu.VMEM((2, ff, d), F32),
                        pltpu.VMEM((d, ff), BF16), pltpu.VMEM((d, ff), BF16), pltpu.VMEM((ff, d), BF16),
                        pltpu.SemaphoreType.DMA((2, 3))],
    )
    return pl.pallas_call(
        _expert_kernel,
        out_shape=jax.ShapeDtypeStruct((p, d // 2), jnp.int32),
        grid_spec=grid_spec,
        compiler_params=_cparams(("arbitrary",)),
        name="expert_ffn",
    )(tile_run, tile_valid, run_expert, used, xg, w_gate, w_up, w_down)


def _final_kernel(alpha, x1_ref, ya_ref, yb_ref, ew_ref, g2_ref, ln_g_ref, ln_b_ref, o_ref):
    ew = ew_ref[0]
    y = ew[:, 0:1] * _unpack_bf16_pairs(ya_ref[0]) + ew[:, 1:2] * _unpack_bf16_pairs(yb_ref[0])
    o_ref[0] = _layer_norm(alpha * x1_ref[0] + g2_ref[0] * y, ln_g_ref[...], ln_b_ref[...])


def _final(alpha, x1, ya, yb, ew, g2, ln_g, ln_b):
    bsz, s, d = x1.shape
    tm = min(FINAL_TILE, s)
    row = lambda w: pl.BlockSpec((1, tm, w), lambda b, i: (b, i, 0))
    full = lambda a: pl.BlockSpec(a.shape, lambda b, i: (0,) * a.ndim)
    ln_g2, ln_b2 = ln_g.reshape(1, d), ln_b.reshape(1, d)
    return pl.pallas_call(
        functools.partial(_final_kernel, alpha),
        out_shape=jax.ShapeDtypeStruct((bsz, s, d), F32),
        grid=(bsz, s // tm),
        in_specs=[row(d), row(d // 2), row(d // 2), row(LANES), pl.BlockSpec((1, 1, d), lambda b, i: (b, 0, 0)),
                  full(ln_g2), full(ln_b2)],
        out_specs=row(d),
        compiler_params=_cparams(("parallel", "arbitrary")),
        name="combine_ln2",
    )(x1, ya, yb, ew, g2, ln_g2, ln_b2)


SC_CORES = 2
SC_SUBCORES = 16
SC_CHUNK = 64


def _sc_mesh():
    return plsc.VectorSubcoreMesh(core_axis_name="c", subcore_axis_name="s")


def _sc_scatter_rows(rows, dest0, dest1, n_rows):
    n, w = rows.shape
    n_workers = SC_CORES * SC_SUBCORES
    assert n % (n_workers * SC_CHUNK) == 0
    n_chunks = n // (n_workers * SC_CHUNK)
    d0 = dest0.reshape(n // SC_CHUNK, 1, SC_CHUNK)
    d1 = dest1.reshape(n // SC_CHUNK, 1, SC_CHUNK)

    @functools.partial(
        pl.kernel, mesh=_sc_mesh(), out_type=jax.ShapeDtypeStruct((n_rows, w), rows.dtype),
        scratch_types=[pltpu.VMEM((n_chunks, 1, SC_CHUNK), jnp.int32), pltpu.VMEM((n_chunks, 1, SC_CHUNK), jnp.int32),
                       pltpu.VMEM((2, SC_CHUNK, w), rows.dtype),
                       pltpu.SemaphoreType.DMA((2,)), pltpu.SemaphoreType.DMA((2, 2))])
    def scatter_kernel(rows_hbm, d0_hbm, d1_hbm, out_hbm, i0_v, i1_v, rows_v, read_sem, scat_sem):
        wid = lax.axis_index("s") * SC_CORES + lax.axis_index("c")
        first = wid * n_chunks
        pltpu.sync_copy(d0_hbm.at[pl.ds(first, n_chunks)], i0_v)
        pltpu.sync_copy(d1_hbm.at[pl.ds(first, n_chunks)], i1_v)

        def read(j):
            return pltpu.make_async_copy(rows_hbm.at[pl.ds((first + j) * SC_CHUNK, SC_CHUNK)], rows_v.at[j % 2],
                                         read_sem.at[j % 2])

        def scatters(j):
            return [pltpu.make_async_copy(rows_v.at[j % 2], out_hbm.at[idx.at[j].at[0]], scat_sem.at[j % 2, k])
                    for k, idx in enumerate((i0_v, i1_v))]

        read(0).start()
        for j in range(n_chunks):
            read(j).wait()
            if j + 1 < n_chunks:
                if j >= 1:
                    for cp in scatters(j - 1):
                        cp.wait()
                read(j + 1).start()
            for cp in scatters(j):
                cp.start()
        for j in range(max(n_chunks - 2, 0), n_chunks):
            for cp in scatters(j):
                cp.wait()

    return scatter_kernel(rows, d0, d1)


def _sc_gather_rows(table, dest0, dest1):
    n = dest0.shape[0]
    w = table.shape[1]
    n_workers = SC_CORES * SC_SUBCORES
    assert n % (n_workers * SC_CHUNK) == 0
    n_chunks = n // (n_workers * SC_CHUNK)
    d0 = dest0.reshape(n // SC_CHUNK, 1, SC_CHUNK)
    d1 = dest1.reshape(n // SC_CHUNK, 1, SC_CHUNK)
    out = jax.ShapeDtypeStruct((n, w), table.dtype)

    @functools.partial(
        pl.kernel, mesh=_sc_mesh(), out_type=(out, out),
        scratch_types=[pltpu.VMEM((n_chunks, 1, SC_CHUNK), jnp.int32), pltpu.VMEM((n_chunks, 1, SC_CHUNK), jnp.int32),
                       pltpu.VMEM((2, SC_CHUNK, w), table.dtype),
                       pltpu.SemaphoreType.DMA((2,)), pltpu.SemaphoreType.DMA((2,))])
    def gather_kernel(table_hbm, d0_hbm, d1_hbm, a_hbm, b_hbm, i0_v, i1_v, rows_v, gather_sem, write_sem):
        wid = lax.axis_index("s") * SC_CORES + lax.axis_index("c")
        first = wid * n_chunks
        pltpu.sync_copy(d0_hbm.at[pl.ds(first, n_chunks)], i0_v)
        pltpu.sync_copy(d1_hbm.at[pl.ds(first, n_chunks)], i1_v)
        n_items = 2 * n_chunks

        def gather(m):
            idx = (i0_v, i1_v)[m % 2]
            return pltpu.make_async_copy(table_hbm.at[idx.at[m // 2].at[0]], rows_v.at[m % 2], gather_sem.at[m % 2])

        def write(m):
            o_hbm = (a_hbm, b_hbm)[m % 2]
            return pltpu.make_async_copy(rows_v.at[m % 2], o_hbm.at[pl.ds((first + m // 2) * SC_CHUNK, SC_CHUNK)],
                                         write_sem.at[m % 2])

        gather(0).start()
        for m in range(n_items):
            gather(m).wait()
            if m + 1 < n_items:
                if m >= 1:
                    write(m - 1).wait()
                gather(m + 1).start()
            write(m).start()
        for m in range(max(n_items - 2, 0), n_items):
            write(m).wait()

    return gather_kernel(table, d0, d1)


def _dispatch_plan(route, counts):
    tm = EXPERT_TILE
    e0, e1, r0, r1 = (route[:, j, :].reshape(-1) for j in range(4))
    experts = jnp.arange(MOE_TOTAL, dtype=jnp.int32)
    tiles_per = (counts + tm - 1) // tm
    tile_end = jnp.cumsum(tiles_per)
    pad_start = ((tile_end - tiles_per) * tm).astype(jnp.int32)

    def lookup(e):
        return jnp.sum(jnp.where(e[None, :] == experts[:, None], pad_start[:, None], 0), axis=0)

    dest0, dest1 = lookup(e0) + r0, lookup(e1) + r1
    n_tiles = (2 * e0.size + MOE_TOTAL * tm) // tm
    tile_expert = jnp.minimum(jnp.sum(tile_end[None, :] <= jnp.arange(n_tiles)[:, None], axis=1), MOE_TOTAL - 1)
    nonempty = counts > 0
    run_of_expert = jnp.cumsum(nonempty.astype(jnp.int32)) - 1
    run_expert = jnp.sum(jnp.where(nonempty[None, :] & (run_of_expert[None, :] == experts[:, None]),
                                   experts[None, :], 0), axis=1).astype(jnp.int32)
    of_tile = tile_expert[:, None] == experts[None, :]
    tile_run = jnp.sum(jnp.where(of_tile, run_of_expert[None, :], 0), axis=1).astype(jnp.int32)
    rows_left = (counts + pad_start)[None, :] - jnp.arange(n_tiles)[:, None] * tm
    tile_valid = jnp.clip(jnp.sum(jnp.where(of_tile, rows_left, 0), axis=1), 0, tm).astype(jnp.int32)
    used = jnp.stack([tile_end[-1], jnp.sum(nonempty)]).astype(jnp.int32)
    return dest0, dest1, tile_run, tile_valid, run_expert, used, n_tiles * tm


def _layer(x, c, rel_bias, w_ada, b_ada, w_in, w_gla_gate, b_gla_gate, gla_norm, w_proj_gla, w_proj_attn, w_out,
           ln1_g, ln1_b, w_rg, b_rg, w_re, b_re, w_eg, w_eu, w_ed, ln2_g, ln2_b):
    bsz, s, d = x.shape
    alpha = (2.0 * DEPTH) ** 0.25
    mods = _ada_mods(c, w_ada, b_ada)
    sh1, sc1, g1, sh2, sc2, g2 = [m.reshape(bsz, 1, d) for m in jnp.split(mods, N_MOD, axis=-1)]

    lr0 = d // 2 * 2 + 2 * d
    z = _in_projection(x, sc1, sh1, _prep_in_weight(w_in, lr0), w_gla_gate, b_gla_gate)

    y_gla = _gla(z["q_in"], z["k_in"], z["q_st"], z["k_st"], z["dec"], z["v_gla"], z["r_gla"], gla_norm)

    o_groups, lse_groups = [], []
    for g, (window, dilation) in enumerate(DIL_PATTERNS):
        l = s // dilation
        qg, kg, vg = (z[f"{n}{g}"].reshape(bsz * dilation, l, DIL_GROUP_WIDTH) for n in ("q_att", "k_att", "v_att"))
        table = rel_bias[:, g * DIL_HEADS_PER_GROUP:(g + 1) * DIL_HEADS_PER_GROUP]
        o, lse = _dilated_group_attention(qg, kg, vg, table, window, dilation)
        o_groups.append(o.reshape(bsz, dilation, l, DIL_GROUP_WIDTH))
        lse_groups.append(lse.reshape(bsz, dilation, l, DIL_GROUP_WIDTH))

    wr = jnp.concatenate([w_rg, w_re, jnp.zeros((d, LANES - MOE_GROUPS - MOE_TOTAL), F32)], axis=1)
    br = jnp.concatenate([b_rg, b_re, jnp.zeros((LANES - MOE_GROUPS - MOE_TOTAL,), F32)]).reshape(1, LANES)
    x1, u2, route, ew, cnt = _merge(alpha, y_gla, o_groups, lse_groups, z["g_gla"], z["g_att"], x, g1, sc2, sh2,
                                    ln1_g, ln1_b, w_proj_gla.astype(BF16), w_proj_attn.astype(BF16),
                                    w_out.astype(BF16), wr, br)

    n = bsz * s
    counts = cnt[MOE_GROUPS:MOE_GROUPS + MOE_TOTAL, 0]
    dest0, dest1, tile_run, tile_valid, run_expert, used, n_rows = _dispatch_plan(route, counts)
    xg = _sc_scatter_rows(u2.reshape(n, d // 2), dest0, dest1, n_rows)
    ff = w_eg.shape[-1]
    yo = _expert_ffn(tile_run, tile_valid, run_expert, used, xg, w_eg.reshape(MOE_TOTAL, d, ff),
                     w_eu.reshape(MOE_TOTAL, d, ff), w_ed.reshape(MOE_TOTAL, ff, d))
    ya, yb = (y.reshape(bsz, s, d // 2) for y in _sc_gather_rows(yo, dest0, dest1))
    return _final(alpha, x1, ya, yb, ew, g2, ln2_g, ln2_b)


def kernel(x, c, rel_bias, w_ada, b_ada, w_in, w_gla_gate, b_gla_gate, gla_norm, w_proj_gla, w_proj_attn, w_out,
           ln1_g, ln1_b, w_router_group, b_router_group, w_router_expert, b_router_expert, w_exp_gate, w_exp_up,
           w_exp_down, ln2_g, ln2_b):
    assert w_ada.shape[0] == DEPTH
    return _layer(x, c, rel_bias, w_ada[0], b_ada[0], w_in[0:1], w_gla_gate[0], b_gla_gate[0], gla_norm[0],
                  w_proj_gla[0], w_proj_attn[0], w_out[0], ln1_g[0], ln1_b[0], w_router_group[0],
                  b_router_group[0], w_router_expert[0], b_router_expert[0], w_exp_gate[0], w_exp_up[0],
                  w_exp_down[0], ln2_g[0], ln2_b[0])
```

```python
import functools
import math

import numpy as np
import jax
import jax.numpy as jnp
from jax import lax
from jax.experimental import pallas as pl
from jax.experimental.pallas import tpu as pltpu
from jax.experimental.pallas import tpu_sc as plsc

F32 = jnp.float32
BF16 = jnp.bfloat16

N_MOD = 6
GLA_HEADS = 4
GLA_LOWRANK = 16
GLA_TAU = 16.0
GLA_CHUNK = 64
DIL_PATTERNS = ((128, 1), (512, 4), (2048, 16))
DIL_GROUPS = len(DIL_PATTERNS)
DIL_HEADS_PER_GROUP = 8
DIL_HEAD_DIM = 64
DIL_GROUP_WIDTH = DIL_HEADS_PER_GROUP * DIL_HEAD_DIM
DIL_BLOCK = 128
REL_BUCKETS = 32
REL_MAX_DIST = 2048
MOE_GROUPS = 4
MOE_EXPERTS = 8
MOE_TOTAL = MOE_GROUPS * MOE_EXPERTS
LN_EPS = 1e-5
DEPTH = 1

LANES = 128
VMEM_LIMIT = 56 * 1024 * 1024
LOG2E = 1.4426950408889634
LN2 = 0.6931471805599453
NEG = -1e30
ROW_TILE = 512
PROJ_CHUNK = 512
FINAL_TILE = 1024
EXPERT_TILE = 512
EXPERT_BLOCK = 128
GLA_STEP_CHUNKS = 32
ATT_STEP_BLOCKS = 16
ROUTER_ROWS = 40
ROUTE_ROWS = 8

NT_DIMS = (((1,), (1,)), ((), ()))
TN_DIMS = (((0,), (0,)), ((), ()))


def _cparams(sem):
    return pltpu.CompilerParams(dimension_semantics=sem, vmem_limit_bytes=VMEM_LIMIT)


def _sigmoid(x):
    return 0.5 * jnp.tanh(0.5 * x) + 0.5


def _silu(x):
    return x * _sigmoid(x)


def _layer_norm(x, g, b):
    mu = jnp.mean(x, axis=-1, keepdims=True)
    xc = x - mu
    var = jnp.mean(xc * xc, axis=-1, keepdims=True)
    return xc * lax.rsqrt(var + LN_EPS) * g + b


def _pack_bf16_pairs(x):
    w = x.shape[1] // 2
    lo = lax.bitcast_convert_type(x[:, :w].astype(BF16).astype(F32), jnp.uint32) >> 16
    hi = lax.bitcast_convert_type(x[:, w:].astype(BF16).astype(F32), jnp.uint32) & jnp.uint32(0xFFFF0000)
    return lax.bitcast_convert_type(lo | hi, jnp.int32)


def _unpack_bf16_pairs(p):
    u = lax.bitcast_convert_type(p, jnp.uint32)
    lo = lax.bitcast_convert_type(u << 16, F32)
    hi = lax.bitcast_convert_type(u & jnp.uint32(0xFFFF0000), F32)
    return jnp.concatenate([lo, hi], axis=1)


def _mods_kernel(ct_ref, w_ref, b_ref, o_ref):
    a = _silu(ct_ref[...])
    w = w_ref[...]
    for b in range(a.shape[1]):
        o_ref[b:b + 1, :] = jnp.sum(a[:, b:b + 1] * w, axis=0, keepdims=True) + b_ref[...]


def _ada_mods(c, w, b):
    bsz, d = c.shape
    n = w.shape[1]
    tn = 1536
    assert n % tn == 0
    return pl.pallas_call(
        _mods_kernel,
        out_shape=jax.ShapeDtypeStruct((bsz, n), F32),
        grid=(n // tn,),
        in_specs=[pl.BlockSpec((d, bsz), lambda j: (0, 0)),
                  pl.BlockSpec((d, tn), lambda j: (0, j)),
                  pl.BlockSpec((1, tn), lambda j: (0, j))],
        out_specs=pl.BlockSpec((bsz, tn), lambda j: (0, j)),
        compiler_params=_cparams(("arbitrary",)),
        name="ada_mods",
    )(c.T, w, b.reshape(1, n))


def _proj_pieces(d_model):
    dk = d_model // 2
    pieces = [("q_gla", dk, "scale_q_gla"), ("k_gla", dk, None), ("v_gla", d_model, None), ("r_gla", d_model, "silu")]
    for name, post in (("q_att", "scale_q_att"), ("k_att", None), ("v_att", None)):
        for g, (_, dilation) in enumerate(DIL_PATTERNS):
            pieces.append((f"{name}{g}", DIL_GROUP_WIDTH, (post, dilation)))
    pieces += [("g_gla", d_model, "sigmoid"), ("g_att", d_model, "sigmoid"), ("lr", LANES, "lowrank")]
    return tuple(pieces)


WT_BLOCK = 512


def _wprep_kernel(n_main_blocks, w_ref, o_ref):
    blk = w_ref[0]
    row = lax.broadcasted_iota(jnp.int32, blk.shape, 0)
    keep = (pl.program_id(0) < n_main_blocks) | (row < GLA_LOWRANK)
    o_ref[...] = jnp.where(keep, blk, 0.0).astype(BF16)


def _prep_in_weight(w_in, lr0):
    w_t = jnp.swapaxes(w_in, 1, 2)
    _, n_in, d = w_t.shape
    n_main = n_in - GLA_LOWRANK
    assert lr0 % WT_BLOCK == 0 and n_main % WT_BLOCK == 0
    n_main_blocks = n_main // WT_BLOCK

    def src_row(j):
        start = j * WT_BLOCK
        octet = jnp.where(j < n_main_blocks, (start + jnp.where(start >= lr0, GLA_LOWRANK, 0)) // 8, lr0 // 8)
        return octet * 8

    return pl.pallas_call(
        functools.partial(_wprep_kernel, n_main_blocks),
        out_shape=jax.ShapeDtypeStruct((n_main + WT_BLOCK, d), BF16),
        grid=(n_main_blocks + 1,),
        in_specs=[pl.BlockSpec((pl.Element(1), pl.Element(WT_BLOCK), pl.Element(d)), lambda j: (0, src_row(j), 0))],
        out_specs=pl.BlockSpec((WT_BLOCK, d), lambda j: (j, 0)),
        compiler_params=_cparams(("parallel",)),
        name="prep_in_weight",
    )(w_t)


GLA_HELD = ("lr", "q_gla", "k_gla")


def _gla_operands(hold, wg_ref, bg_ref, qin_ref, kin_ref, qst_ref, kst_ref, dec_ref):
    c = GLA_CHUNK
    tm = hold["q_gla"].shape[0]
    tril = (lax.broadcasted_iota(jnp.int32, (c, c), 0) >= lax.broadcasted_iota(jnp.int32, (c, c), 1)).astype(BF16)
    mid = c // 2 - 1
    lr, wg = hold["lr"][:, 0:GLA_LOWRANK], wg_ref[...]
    lr_hi, wg_hi = lr.astype(BF16), wg.astype(BF16)
    lr_lo, wg_lo = (lr - lr_hi.astype(F32)).astype(BF16), (wg - wg_hi.astype(F32)).astype(BF16)
    gate_in = (jnp.dot(lr_hi, wg_hi, preferred_element_type=F32) + jnp.dot(lr_lo, wg_hi, preferred_element_type=F32)
               + jnp.dot(lr_hi, wg_lo, preferred_element_type=F32)) + bg_ref[...]
    g_all = (jnp.minimum(gate_in, 0.0) - jnp.log(1.0 + jnp.exp(-jnp.abs(gate_in)))) * (1.0 / GLA_TAU)
    g_hi = g_all.astype(BF16)
    g_lo = (g_all - g_hi.astype(F32)).astype(BF16)
    for ci in range(tm // c):
        rows = slice(ci * c, (ci + 1) * c)
        bc = jnp.dot(tril, g_hi[rows], preferred_element_type=F32) + jnp.dot(tril, g_lo[rows], preferred_element_type=F32)
        b_mid = bc[mid:mid + 1, :]
        b_last = bc[c - 1:c, :]
        qf = hold["q_gla"][rows, :]
        kf = hold["k_gla"][rows, :]
        q_in = qf * jnp.exp(bc - b_mid)
        k_in = kf * jnp.exp(b_mid - bc)
        qin_ref[0, rows, :] = q_in.astype(BF16)
        kin_ref[0, rows, :] = k_in.astype(BF16)
        qst_ref[0, rows, :] = (q_in * jnp.exp(b_mid)).astype(BF16)
        kst_ref[0, rows, :] = (k_in * jnp.exp(b_last - b_mid)).astype(BF16)
        dec_ref[0, ci:ci + 1, :] = jnp.exp(b_last)


def _proj_kernel(pieces, head_k, x_ref, sc_ref, sh_ref, w_ref, wg_ref, bg_ref, *refs):
    n_out = len(pieces) - len(GLA_HELD)
    out_refs = dict(zip([p[0] for p in pieces if p[0] not in GLA_HELD], refs[:n_out]))
    gla_out_refs = refs[n_out:n_out + 5]
    stage_ref = refs[n_out + 5]
    hold = dict(zip(GLA_HELD, refs[n_out + 6:]))
    tm = x_ref.shape[1]
    u = (x_ref[0] * (1.0 + sc_ref[0]) + sh_ref[0]).astype(BF16)
    offsets, off = {}, 0
    for name, width, _ in pieces:
        offsets[name] = off
        off += width
    by_name = {p[0]: p for p in pieces}
    held = [(by_name[n], 0) for n in GLA_HELD]
    rest = [(p, c0) for p in pieces if p[0] not in GLA_HELD for c0 in range(0, p[1], min(p[1], PROJ_CHUNK))]
    for n, (piece, c0) in enumerate(held + rest):
        if n == len(held):
            _gla_operands(hold, wg_ref, bg_ref, *gla_out_refs)
        name, width, post = piece
        o_ref = hold[name] if name in GLA_HELD else out_refs[name]
        off = offsets[name]
        chunk = min(width, PROJ_CHUNK)
        acc = lax.dot_general(u, w_ref[off + c0:off + c0 + chunk, :], NT_DIMS, preferred_element_type=F32)
        if post == "silu":
            acc = _silu(acc)
        elif post == "sigmoid":
            acc = _sigmoid(acc)
        elif post == "scale_q_gla":
            acc = acc * (head_k ** -0.5)
        if name in GLA_HELD:
            o_ref[...] = acc
        elif isinstance(post, tuple):
            scale, dilation = post
            if scale is not None:
                acc = acc * (DIL_HEAD_DIM ** -0.5 * LOG2E)
            if dilation == 1:
                o_ref[0, 0] = acc.astype(o_ref.dtype)
            else:
                for t in range(width // LANES):
                    stage_ref[t] = acc[:, t * LANES:(t + 1) * LANES]
                for r in range(dilation):
                    for t in range(width // LANES):
                        o_ref[0, r, :, t * LANES:(t + 1) * LANES] = stage_ref[
                            t, pl.ds(r, tm // dilation, stride=dilation), :].astype(o_ref.dtype)
        else:
            o_ref[0, :, c0:c0 + chunk] = acc.astype(o_ref.dtype)


def _in_projection(x, sc1, sh1, w_perm, w_gate, b_gate):
    bsz, s, d = x.shape
    pieces = _proj_pieces(d)
    assert sum(p[1] for p in pieces) <= w_perm.shape[0]
    tm = min(ROW_TILE, s)
    assert s % tm == 0 and tm % (8 * GLA_CHUNK) == 0
    dk = d // 2
    head_k = dk // GLA_HEADS
    out_shape, out_specs = [], []
    for name, width, post in pieces:
        if name in GLA_HELD:
            continue
        if isinstance(post, tuple):
            dil = post[1]
            assert tm % (dil * 16) == 0
            out_shape.append(jax.ShapeDtypeStruct((bsz, dil, s // dil, width), BF16))
            out_specs.append(pl.BlockSpec((1, dil, tm // dil, width), lambda b, i: (b, 0, i, 0)))
        else:
            out_shape.append(jax.ShapeDtypeStruct((bsz, s, width), BF16))
            out_specs.append(pl.BlockSpec((1, tm, width), lambda b, i: (b, i, 0)))
    row = lambda w: pl.BlockSpec((1, tm, w), lambda b, i: (b, i, 0))
    gla_names = ("q_in", "k_in", "q_st", "k_st", "dec")
    out_shape += [jax.ShapeDtypeStruct((bsz, s, dk), BF16)] * 4 + [jax.ShapeDtypeStruct((bsz, s // GLA_CHUNK, dk), F32)]
    out_specs += [row(dk)] * 4 + [pl.BlockSpec((1, tm // GLA_CHUNK, dk), lambda b, i: (b, i, 0))]
    bg = b_gate.reshape(1, dk)
    full = lambda a: pl.BlockSpec(a.shape, lambda b, i: (0,) * a.ndim)
    outs = pl.pallas_call(
        functools.partial(_proj_kernel, pieces, head_k),
        out_shape=out_shape,
        grid=(bsz, s // tm),
        in_specs=[row(d),
                  pl.BlockSpec((1, 1, d), lambda b, i: (b, 0, 0)),
                  pl.BlockSpec((1, 1, d), lambda b, i: (b, 0, 0)),
                  pl.BlockSpec(w_perm.shape, lambda b, i: (0, 0), pipeline_mode=pl.Buffered(1)),
                  full(w_gate), full(bg)],
        out_specs=out_specs,
        scratch_shapes=[pltpu.VMEM((DIL_GROUP_WIDTH // LANES, tm, LANES), F32),
                        pltpu.VMEM((tm, LANES), F32), pltpu.VMEM((tm, dk), F32), pltpu.VMEM((tm, dk), F32)],
        compiler_params=_cparams(("parallel", "arbitrary")),
        name="in_projection",
    )(x, sc1, sh1, w_perm, w_gate, bg)
    return dict(zip([p[0] for p in pieces if p[0] not in GLA_HELD] + list(gla_names), outs))


def _gla_kernel(n_chunks, head_k, head_v, qin_ref, kin_ref, qst_ref, kst_ref, dec_ref, v_ref, r_ref, ng_ref, o_ref,
                state_ref):
    @pl.when(pl.program_id(1) == 0)
    def _():
        state_ref[...] = jnp.zeros_like(state_ref)

    c = GLA_CHUNK
    causal = lax.broadcasted_iota(jnp.int32, (c, c), 0) >= lax.broadcasted_iota(jnp.int32, (c, c), 1)
    for ci in range(n_chunks):
        rows = slice(ci * c, (ci + 1) * c)
        for h in range(GLA_HEADS):
            ks = slice(h * head_k, (h + 1) * head_k)
            vs = slice(h * head_v, (h + 1) * head_v)
            vh = v_ref[0, rows, vs]
            att = lax.dot_general(qin_ref[0, rows, ks], kin_ref[0, rows, ks], NT_DIMS, preferred_element_type=F32)
            att = jnp.where(causal, att, 0.0).astype(BF16)
            st = state_ref[h]
            o = jnp.dot(att, vh, preferred_element_type=F32)
            o = o + lax.dot_general(qst_ref[0, rows, ks], st.astype(BF16), NT_DIMS, preferred_element_type=F32)
            kv_t = lax.dot_general(vh, kst_ref[0, rows, ks], TN_DIMS, preferred_element_type=F32)
            state_ref[h] = st * dec_ref[0, ci:ci + 1, ks] + kv_t
            ms = jnp.mean(o * o, axis=-1, keepdims=True)
            o = o * lax.rsqrt(ms + LN_EPS) * ng_ref[:, vs] * r_ref[0, rows, vs].astype(F32)
            o_ref[0, rows, vs] = o.astype(o_ref.dtype)


def _gla(q_in, k_in, q_st, k_st, dec, v, r_silu, norm_g):
    bsz, s, dk = q_in.shape
    dv = v.shape[-1]
    head_k, head_v = dk // GLA_HEADS, dv // GLA_HEADS
    n_chunks = min(GLA_STEP_CHUNKS, s // GLA_CHUNK)
    ct = GLA_CHUNK * n_chunks
    assert s % ct == 0
    row_spec = lambda w: pl.BlockSpec((1, ct, w), lambda b, i: (b, i, 0))
    full = lambda a: pl.BlockSpec(a.shape, lambda b, i: (0,) * a.ndim)
    ng = norm_g.reshape(1, dv)
    return pl.pallas_call(
        functools.partial(_gla_kernel, n_chunks, head_k, head_v),
        out_shape=jax.ShapeDtypeStruct((bsz, s, dv), BF16),
        grid=(bsz, s // ct),
        in_specs=[row_spec(dk)] * 4 + [pl.BlockSpec((1, n_chunks, dk), lambda b, i: (b, i, 0)),
                                       row_spec(dv), row_spec(dv), full(ng)],
        out_specs=row_spec(dv),
        scratch_shapes=[pltpu.VMEM((GLA_HEADS, head_v, head_k), F32)],
        compiler_params=_cparams(("parallel", "arbitrary")),
        name="gla",
    )(q_in, k_in, q_st, k_st, dec, v, r_silu, ng)


def _t5_bucket_np(dist):
    exact = REL_BUCKETS // 2
    d = np.maximum(dist, 1).astype(np.float32)
    large = exact + (np.log(d / np.float32(exact)) / np.float32(math.log(REL_MAX_DIST / exact))
                     * np.float32(REL_BUCKETS - exact)).astype(np.int32)
    large = np.minimum(large, REL_BUCKETS - 1)
    return np.where(dist < exact, dist, large).astype(np.int32)


def _band_tables(window, dilation):
    qi = np.arange(DIL_BLOCK)[:, None]
    kj = np.arange(2 * DIL_BLOCK)[None, :]
    m = qi + DIL_BLOCK - kj
    n_steps = window // dilation
    band = (m >= 0) & (m <= n_steps)
    bucket = _t5_bucket_np(np.clip(m, 0, n_steps) * dilation)
    return np.where(band, bucket, -1).astype(np.int32)


def _attn_kernel(nq, table_ref, bucket_ref, q_ref, kp_ref, kc_ref, vp_ref, vc_ref, o_ref, lse_ref,
                 bias_ref, p_ref):
    i = pl.program_id(1)
    blk = DIL_BLOCK
    hpg = DIL_HEADS_PER_GROUP
    n_pairs = hpg // 2

    @pl.when((pl.program_id(0) == 0) & (i == 0))
    def _():
        bucket = bucket_ref[...]
        for h in range(hpg):
            acc = jnp.full(bucket.shape, NEG, F32)
            for bkt in range(REL_BUCKETS):
                acc = jnp.where(bucket == bkt, table_ref[bkt, h] * LOG2E, acc)
            bias_ref[h * blk:(h + 1) * blk, :] = acc

    lane = lax.broadcasted_iota(jnp.int32, (blk, LANES), 1)
    low = lane < DIL_HEAD_DIM
    ones_rhs = jnp.ones((2 * blk, LANES), BF16)

    def windows(ref_p, ref_c, sq, qb, cols):
        if qb == 0:
            return jnp.concatenate([ref_p[sq, :, cols], ref_c[sq, 0:blk, cols]], axis=0)
        return ref_c[sq, (qb - 1) * blk:(qb + 1) * blk, cols]

    key_lane = lax.broadcasted_iota(jnp.int32, (1, 2 * blk), 1)
    no_prev = jnp.where((key_lane < blk) & (i == 0), NEG, 0.0)
    items = [(sq, qb, hp) for sq in range(q_ref.shape[0]) for qb in range(nq) for hp in range(n_pairs)]

    mxs = []
    for n, (sq, qb, hp) in enumerate(items):
        rows = slice(qb * blk, (qb + 1) * blk)
        cols = slice(hp * LANES, (hp + 1) * LANES)
        qp = q_ref[sq, rows, cols]
        zero = jnp.zeros_like(qp)
        qq = jnp.concatenate([jnp.where(low, qp, zero), jnp.where(low, zero, qp)], axis=0)
        keys = windows(kp_ref, kc_ref, sq, qb, cols)
        s = lax.dot_general(qq, keys, NT_DIMS, preferred_element_type=F32) + bias_ref[2 * hp * blk:(2 * hp + 2) * blk, :]
        if qb == 0:
            s = s + no_prev
        mx = jnp.max(s, axis=-1, keepdims=True)
        p_ref[n * 2 * blk:(n + 1) * 2 * blk, :] = jnp.exp2(s - mx).astype(BF16)
        mxs.append(mx)

    for n, (sq, qb, hp) in enumerate(items):
        rows = slice(qb * blk, (qb + 1) * blk)
        cols = slice(hp * LANES, (hp + 1) * LANES)
        vals = windows(vp_ref, vc_ref, sq, qb, cols)
        rhs = jnp.concatenate([vals, ones_rhs], axis=1)
        res = jnp.dot(p_ref[n * 2 * blk:(n + 1) * 2 * blk, :], rhs, preferred_element_type=F32)
        num = jnp.where(low, res[0:blk, 0:LANES], res[blk:2 * blk, 0:LANES])
        den = jnp.where(low, res[0:blk, LANES:], res[blk:2 * blk, LANES:])
        mx = jnp.where(low, mxs[n][0:blk], mxs[n][blk:2 * blk])
        o_ref[sq, rows, cols] = (num / den).astype(o_ref.dtype)
        lse_ref[sq, rows, cols] = (mx + jnp.log2(den)) * LN2


def _dilated_group_attention(q, k, v, table, window, dilation):
    bb, l, w = q.shape
    nq = min(ATT_STEP_BLOCKS, l // DIL_BLOCK)
    nsq = ATT_STEP_BLOCKS // nq
    assert l % (nq * DIL_BLOCK) == 0 and bb % nsq == 0
    steps = l // (nq * DIL_BLOCK)
    bucket = jnp.asarray(_band_tables(window, dilation))
    cur = pl.BlockSpec((nsq, nq * DIL_BLOCK, w), lambda b, i: (b, i, 0))
    prev = pl.BlockSpec((nsq, DIL_BLOCK, w), lambda b, i: (b, jnp.maximum(nq * i - 1, 0), 0))
    rows_all = nsq * nq * DIL_HEADS_PER_GROUP * DIL_BLOCK
    return pl.pallas_call(
        functools.partial(_attn_kernel, nq),
        out_shape=[jax.ShapeDtypeStruct((bb, l, w), BF16), jax.ShapeDtypeStruct((bb, l, w), F32)],
        grid=(bb // nsq, steps),
        in_specs=[pl.BlockSpec(memory_space=pltpu.SMEM),
                  pl.BlockSpec(bucket.shape, lambda b, i: (0, 0)),
                  cur, prev, cur, prev, cur],
        out_specs=[cur, cur],
        scratch_shapes=[pltpu.VMEM((DIL_HEADS_PER_GROUP * DIL_BLOCK, 2 * DIL_BLOCK), F32),
                        pltpu.VMEM((rows_all, 2 * DIL_BLOCK), BF16)],
        compiler_params=_cparams(("arbitrary", "arbitrary")),
        name=f"dilated_attn_d{dilation}",
    )(table, bucket, q, k, k, v, v)


def _merge_kernel(alpha, dilations, ygla_ref, o0_ref, o1_ref, o2_ref, l0_ref, l1_ref, l2_ref, gg_ref, ga_ref, x_ref,
                  g1_ref, sc2_ref, sh2_ref, ln_g_ref, ln_b_ref, wpg_ref, wpa_ref, wout_ref, wr_ref, br_ref, utri_ref,
                  x1_ref, u2_ref, route_ref, ew_ref, cnt_ref, stage_ref, carry_ref):
    tm = x_ref.shape[1]

    @pl.when((pl.program_id(0) == 0) & (pl.program_id(1) == 0))
    def _():
        carry_ref[...] = jnp.zeros_like(carry_ref)

    n_lt = DIL_GROUP_WIDTH // LANES
    group_refs = tuple(zip((l0_ref, l1_ref, l2_ref), (o0_ref, o1_ref, o2_ref), dilations))
    for gi, (l_ref, o_ref, dil) in enumerate(group_refs):
        if dil > 1:
            for slot, ref in ((2 * gi, l_ref), (2 * gi + 1, o_ref)):
                for r in range(dil):
                    for t in range(n_lt):
                        stage_ref[slot, t, pl.ds(r, tm // dil, stride=dil), :] = ref[
                            0, r, :, t * LANES:(t + 1) * LANES].astype(F32)

    def natural(ref, dil, slot):
        if dil == 1:
            return ref[0, 0].astype(F32)
        return jnp.concatenate([stage_ref[slot, t] for t in range(n_lt)], axis=1)

    lses = [natural(l_ref, dil, 2 * gi) for gi, (l_ref, _, dil) in enumerate(group_refs)]
    outs = [natural(o_ref, dil, 2 * gi + 1) for gi, (_, o_ref, dil) in enumerate(group_refs)]
    lm = jnp.maximum(jnp.maximum(lses[0], lses[1]), lses[2])
    es = [jnp.exp(l - lm) for l in lses]
    y_att = (es[0] * outs[0] + es[1] * outs[1] + es[2] * outs[2]) / (es[0] + es[1] + es[2])

    p_gla = jnp.dot(ygla_ref[0], wpg_ref[...], preferred_element_type=F32)
    p_att = jnp.dot(y_att.astype(BF16), wpa_ref[...], preferred_element_type=F32)
    merged = gg_ref[0].astype(F32) * p_gla + ga_ref[0].astype(F32) * p_att
    y = jnp.dot(merged.astype(BF16), wout_ref[...], preferred_element_type=F32)
    x1 = _layer_norm(alpha * x_ref[0] + g1_ref[0] * y, ln_g_ref[...], ln_b_ref[...])
    x1_ref[0] = x1
    u2 = x1 * (1.0 + sc2_ref[0]) + sh2_ref[0]
    u2_ref[0] = _pack_bf16_pairs(u2)

    logits = jnp.dot(u2.astype(BF16), wr_ref[...].astype(BF16), preferred_element_type=F32) + br_ref[...]
    lt = jnp.transpose(logits)[0:ROUTER_ROWS, :]
    rowi = lax.broadcasted_iota(jnp.int32, lt.shape, 0)
    big = jnp.int32(LANES)
    lg = jnp.where(rowi < MOE_GROUPS, lt, NEG)
    gmax = jnp.max(lg, axis=0, keepdims=True)
    gidx = jnp.min(jnp.where(lg == gmax, rowi, big), axis=0, keepdims=True)
    gval = 1.0 / jnp.sum(jnp.exp(lg - gmax), axis=0, keepdims=True)
    first = MOE_GROUPS + gidx * MOE_EXPERTS
    le = jnp.where((rowi >= first) & (rowi < first + MOE_EXPERTS), lt, NEG)
    m1 = jnp.max(le, axis=0, keepdims=True)
    i1 = jnp.min(jnp.where(le == m1, rowi, big), axis=0, keepdims=True)
    le2 = jnp.where(rowi == i1, NEG, le)
    m2 = jnp.max(le2, axis=0, keepdims=True)
    i2 = jnp.min(jnp.where(le2 == m2, rowi, big), axis=0, keepdims=True)
    t = jnp.exp(m2 - m1)
    w1 = 1.0 / (1.0 + t)
    w2 = t * w1

    hit1, hit2 = rowi == i1, rowi == i2
    onehot = jnp.where(hit1 | hit2, 1.0, 0.0)
    earlier = jnp.dot(onehot.astype(BF16), utri_ref[...], preferred_element_type=F32) + carry_ref[...]
    rank1 = jnp.sum(jnp.where(hit1, earlier, 0.0), axis=0, keepdims=True).astype(jnp.int32)
    rank2 = jnp.sum(jnp.where(hit2, earlier, 0.0), axis=0, keepdims=True).astype(jnp.int32)
    carry = carry_ref[...] + jnp.sum(onehot, axis=1, keepdims=True)
    carry_ref[...] = carry
    cnt_ref[...] = jnp.broadcast_to(carry, cnt_ref.shape).astype(jnp.int32)
    r8 = lax.broadcasted_iota(jnp.int32, (ROUTE_ROWS, tm), 0)
    route_ref[0] = jnp.where(r8 == 0, i1 - MOE_GROUPS, jnp.where(r8 == 1, i2 - MOE_GROUPS,
                             jnp.where(r8 == 2, rank1, jnp.where(r8 == 3, rank2, 0))))
    r128 = lax.broadcasted_iota(jnp.int32, (LANES, tm), 0)
    ew_ref[0] = jnp.transpose(jnp.where(r128 == 0, gval * w1, jnp.where(r128 == 1, gval * w2, 0.0)))


def _merge(alpha, y_gla, o_groups, lse_groups, g_gla, g_att, x, g1, sc2, sh2, ln_g, ln_b, wpg, wpa, wout, wr, br):
    bsz, s, d = x.shape
    tm = min(ROW_TILE, s)
    assert s % tm == 0
    dilations = tuple(dil for _, dil in DIL_PATTERNS)
    row = lambda w: pl.BlockSpec((1, tm, w), lambda b, i: (b, i, 0))
    sub = lambda dil: pl.BlockSpec((1, dil, tm // dil, DIL_GROUP_WIDTH), lambda b, i: (b, 0, i, 0))
    per_b = pl.BlockSpec((1, 1, d), lambda b, i: (b, 0, 0))
    full = lambda a: pl.BlockSpec(a.shape, lambda b, i: (0,) * a.ndim)
    ln_g2, ln_b2 = ln_g.reshape(1, d), ln_b.reshape(1, d)
    utri = jnp.asarray(np.triu(np.ones((tm, tm), np.float32), 1), BF16)
    return pl.pallas_call(
        functools.partial(_merge_kernel, alpha, dilations),
        out_shape=[jax.ShapeDtypeStruct((bsz, s, d), F32), jax.ShapeDtypeStruct((bsz, s, d // 2), jnp.int32),
                   jax.ShapeDtypeStruct((bsz, ROUTE_ROWS, s), jnp.int32), jax.ShapeDtypeStruct((bsz, s, LANES), F32),
                   jax.ShapeDtypeStruct((ROUTER_ROWS, LANES), jnp.int32)],
        grid=(bsz, s // tm),
        in_specs=[row(y_gla.shape[-1])] + [sub(dil) for dil in dilations] * 2
                 + [row(d), row(d), row(d), per_b, per_b, per_b, full(ln_g2), full(ln_b2),
                    full(wpg), full(wpa), full(wout), full(wr), full(br), full(utri)],
        out_specs=[row(d), row(d // 2), pl.BlockSpec((1, ROUTE_ROWS, tm), lambda b, i: (b, 0, i)), row(LANES),
                   pl.BlockSpec((ROUTER_ROWS, LANES), lambda b, i: (0, 0))],
        scratch_shapes=[pltpu.VMEM((2 * DIL_GROUPS, DIL_GROUP_WIDTH // LANES, tm, LANES), F32),
                        pltpu.VMEM((ROUTER_ROWS, 1), F32)],
        compiler_params=_cparams(("arbitrary", "arbitrary")),
        name="merge_ln1_router",
    )(y_gla, *o_groups, *lse_groups, g_gla, g_att, x, g1, sc2, sh2, ln_g2, ln_b2, wpg, wpa, wout, wr, br, utri)


def _expert_kernel(run_ref, valid_ref, rexp_ref, used_ref, x_ref, wg_hbm, wu_hbm, wd_hbm, o_ref,
                   wg_f, wu_f, wd_f, wg_s, wu_s, wd_s, sem):
    t = pl.program_id(0)
    n_tiles_used, n_runs = used_ref[0], used_ref[1]
    run = run_ref[t]
    active = t < n_tiles_used
    first_of_run = (t == 0) | (run_ref[jnp.maximum(t - 1, 0)] != run)

    def weight_copies(r):
        e, slot = rexp_ref[r], r % 2
        return [pltpu.make_async_copy(hbm.at[e], buf.at[slot], sem.at[slot, j])
                for j, (hbm, buf) in enumerate(((wg_hbm, wg_f), (wu_hbm, wu_f), (wd_hbm, wd_f)))]

    @pl.when(active & (t == 0))
    def _():
        for cp in weight_copies(0):
            cp.start()

    @pl.when(active & first_of_run)
    def _():
        @pl.when(run + 1 < n_runs)
        def _():
            for cp in weight_copies(run + 1):
                cp.start()

        for cp in weight_copies(run):
            cp.wait()
        slot = run % 2
        wg_s[...] = wg_f[slot].astype(BF16)
        wu_s[...] = wu_f[slot].astype(BF16)
        wd_s[...] = wd_f[slot].astype(BF16)

    n_valid = jnp.where(active, valid_ref[t], 0)

    def ffn(rows):
        xt = _unpack_bf16_pairs(x_ref[rows, :]).astype(BF16)
        hg = jnp.dot(xt, wg_s[...], preferred_element_type=F32)
        hu = jnp.dot(xt, wu_s[...], preferred_element_type=F32)
        h = (_silu(hg) * hu).astype(BF16)
        o_ref[rows, :] = _pack_bf16_pairs(jnp.dot(h, wd_s[...], preferred_element_type=F32))

    def zero(rows):
        o_ref[rows, :] = jnp.zeros((rows.stop - rows.start, o_ref.shape[1]), o_ref.dtype)

    tm = x_ref.shape[0]
    n_blocks = tm // EXPERT_BLOCK
    for k in range(n_blocks + 1):
        lo, hi = (k - 1) * EXPERT_BLOCK, k * EXPERT_BLOCK

        @pl.when((n_valid > lo) & (n_valid <= hi) if 0 < k < n_blocks else (n_valid > lo if k else n_valid <= 0))
        def _():
            if k:
                ffn(slice(0, hi))
            if k < n_blocks:
                zero(slice(hi, tm))


def _expert_ffn(tile_run, tile_valid, run_expert, used, xg, w_gate, w_up, w_down):
    p = xg.shape[0]
    ne, d, ff = w_gate.shape
    tm = EXPERT_TILE
    n_tiles = p // tm
    hbm = pl.BlockSpec(memory_space=pl.ANY)
    grid_spec = pltpu.PrefetchScalarGridSpec(
        num_scalar_prefetch=4,
        grid=(n_tiles,),
        in_specs=[pl.BlockSpec((tm, d // 2), lambda t, *_: (t, 0)), hbm, hbm, hbm],
        out_specs=pl.BlockSpec((tm, d // 2), lambda t, *_: (t, 0)),
        scratch_shapes=[pltpu.VMEM((2, d, ff), F32), pltpu.VMEM((2, d, ff), F32), pltpu.VMEM((2, ff, d), F32),
                        pltpu.VMEM((d, ff), BF16), pltpu.VMEM((d, ff), BF16), pltpu.VMEM((ff, d), BF16),
                        pltpu.SemaphoreType.DMA((2, 3))],
    )
    return pl.pallas_call(
        _expert_kernel,
        out_shape=jax.ShapeDtypeStruct((p, d // 2), jnp.int32),
        grid_spec=grid_spec,
        compiler_params=_cparams(("arbitrary",)),
        name="expert_ffn",
    )(tile_run, tile_valid, run_expert, used, xg, w_gate, w_up, w_down)


def _final_kernel(alpha, x1_ref, ya_ref, yb_ref, ew_ref, g2_ref, ln_g_ref, ln_b_ref, o_ref):
    ew = ew_ref[0]
    y = ew[:, 0:1] * _unpack_bf16_pairs(ya_ref[0]) + ew[:, 1:2] * _unpack_bf16_pairs(yb_ref[0])
    o_ref[0] = _layer_norm(alpha * x1_ref[0] + g2_ref[0] * y, ln_g_ref[...], ln_b_ref[...])


def _final(alpha, x1, ya, yb, ew, g2, ln_g, ln_b):
    bsz, s, d = x1.shape
    tm = min(FINAL_TILE, s)
    row = lambda w: pl.BlockSpec((1, tm, w), lambda b, i: (b, i, 0))
    full = lambda a: pl.BlockSpec(a.shape, lambda b, i: (0,) * a.ndim)
    ln_g2, ln_b2 = ln_g.reshape(1, d), ln_b.reshape(1, d)
    return pl.pallas_call(
        functools.partial(_final_kernel, alpha),
        out_shape=jax.ShapeDtypeStruct((bsz, s, d), F32),
        grid=(bsz, s // tm),
        in_specs=[row(d), row(d // 2), row(d // 2), row(LANES), pl.BlockSpec((1, 1, d), lambda b, i: (b, 0, 0)),
                  full(ln_g2), full(ln_b2)],
        out_specs=row(d),
        compiler_params=_cparams(("parallel", "arbitrary")),
        name="combine_ln2",
    )(x1, ya, yb, ew, g2, ln_g2, ln_b2)


SC_CORES = 2
SC_SUBCORES = 16
SC_CHUNK = 64


def _sc_mesh():
    return plsc.VectorSubcoreMesh(core_axis_name="c", subcore_axis_name="s")


def _sc_scatter_rows(rows, dest0, dest1, n_rows):
    n, w = rows.shape
    n_workers = SC_CORES * SC_SUBCORES
    assert n % (n_workers * SC_CHUNK) == 0
    n_chunks = n // (n_workers * SC_CHUNK)
    d0 = dest0.reshape(n // SC_CHUNK, 1, SC_CHUNK)
    d1 = dest1.reshape(n // SC_CHUNK, 1, SC_CHUNK)

    @functools.partial(
        pl.kernel, mesh=_sc_mesh(), out_type=jax.ShapeDtypeStruct((n_rows, w), rows.dtype),
        scratch_types=[pltpu.VMEM((n_chunks, 1, SC_CHUNK), jnp.int32), pltpu.VMEM((n_chunks, 1, SC_CHUNK), jnp.int32),
                       pltpu.VMEM((2, SC_CHUNK, w), rows.dtype),
                       pltpu.SemaphoreType.DMA((2,)), pltpu.SemaphoreType.DMA((2, 2))])
    def scatter_kernel(rows_hbm, d0_hbm, d1_hbm, out_hbm, i0_v, i1_v, rows_v, read_sem, scat_sem):
        wid = lax.axis_index("s") * SC_CORES + lax.axis_index("c")
        first = wid * n_chunks
        pltpu.sync_copy(d0_hbm.at[pl.ds(first, n_chunks)], i0_v)
        pltpu.sync_copy(d1_hbm.at[pl.ds(first, n_chunks)], i1_v)

        def read(j):
            return pltpu.make_async_copy(rows_hbm.at[pl.ds((first + j) * SC_CHUNK, SC_CHUNK)], rows_v.at[j % 2],
                                         read_sem.at[j % 2])

        def scatters(j):
            return [pltpu.make_async_copy(rows_v.at[j % 2], out_hbm.at[idx.at[j].at[0]], scat_sem.at[j % 2, k])
                    for k, idx in enumerate((i0_v, i1_v))]

        read(0).start()
        for j in range(n_chunks):
            read(j).wait()
            if j + 1 < n_chunks:
                if j >= 1:
                    for cp in scatters(j - 1):
                        cp.wait()
                read(j + 1).start()
            for cp in scatters(j):
                cp.start()
        for j in range(max(n_chunks - 2, 0), n_chunks):
            for cp in scatters(j):
                cp.wait()

    return scatter_kernel(rows, d0, d1)


def _sc_gather_rows(table, dest0, dest1):
    n = dest0.shape[0]
    w = table.shape[1]
    n_workers = SC_CORES * SC_SUBCORES
    assert n % (n_workers * SC_CHUNK) == 0
    n_chunks = n // (n_workers * SC_CHUNK)
    d0 = dest0.reshape(n // SC_CHUNK, 1, SC_CHUNK)
    d1 = dest1.reshape(n // SC_CHUNK, 1, SC_CHUNK)
    out = jax.ShapeDtypeStruct((n, w), table.dtype)

    @functools.partial(
        pl.kernel, mesh=_sc_mesh(), out_type=(out, out),
        scratch_types=[pltpu.VMEM((n_chunks, 1, SC_CHUNK), jnp.int32), pltpu.VMEM((n_chunks, 1, SC_CHUNK), jnp.int32),
                       pltpu.VMEM((2, SC_CHUNK, w), table.dtype),
                       pltpu.SemaphoreType.DMA((2,)), pltpu.SemaphoreType.DMA((2,))])
    def gather_kernel(table_hbm, d0_hbm, d1_hbm, a_hbm, b_hbm, i0_v, i1_v, rows_v, gather_sem, write_sem):
        wid = lax.axis_index("s") * SC_CORES + lax.axis_index("c")
        first = wid * n_chunks
        pltpu.sync_copy(d0_hbm.at[pl.ds(first, n_chunks)], i0_v)
        pltpu.sync_copy(d1_hbm.at[pl.ds(first, n_chunks)], i1_v)
        n_items = 2 * n_chunks

        def gather(m):
            idx = (i0_v, i1_v)[m % 2]
            return pltpu.make_async_copy(table_hbm.at[idx.at[m // 2].at[0]], rows_v.at[m % 2], gather_sem.at[m % 2])

        def write(m):
            o_hbm = (a_hbm, b_hbm)[m % 2]
            return pltpu.make_async_copy(rows_v.at[m % 2], o_hbm.at[pl.ds((first + m // 2) * SC_CHUNK, SC_CHUNK)],
                                         write_sem.at[m % 2])

        gather(0).start()
        for m in range(n_items):
            gather(m).wait()
            if m + 1 < n_items:
                if m >= 1:
                    write(m - 1).wait()
                gather(m + 1).start()
            write(m).start()
        for m in range(max(n_items - 2, 0), n_items):
            write(m).wait()

    return gather_kernel(table, d0, d1)


def _dispatch_plan(route, counts):
    tm = EXPERT_TILE
    e0, e1, r0, r1 = (route[:, j, :].reshape(-1) for j in range(4))
    experts = jnp.arange(MOE_TOTAL, dtype=jnp.int32)
    tiles_per = (counts + tm - 1) // tm
    tile_end = jnp.cumsum(tiles_per)
    pad_start = ((tile_end - tiles_per) * tm).astype(jnp.int32)

    def lookup(e):
        return jnp.sum(jnp.where(e[None, :] == experts[:, None], pad_start[:, None], 0), axis=0)

    dest0, dest1 = lookup(e0) + r0, lookup(e1) + r1
    n_tiles = (2 * e0.size + MOE_TOTAL * tm) // tm
    tile_expert = jnp.minimum(jnp.sum(tile_end[None, :] <= jnp.arange(n_tiles)[:, None], axis=1), MOE_TOTAL - 1)
    nonempty = counts > 0
    run_of_expert = jnp.cumsum(nonempty.astype(jnp.int32)) - 1
    run_expert = jnp.sum(jnp.where(nonempty[None, :] & (run_of_expert[None, :] == experts[:, None]),
                                   experts[None, :], 0), axis=1).astype(jnp.int32)
    of_tile = tile_expert[:, None] == experts[None, :]
    tile_run = jnp.sum(jnp.where(of_tile, run_of_expert[None, :], 0), axis=1).astype(jnp.int32)
    rows_left = (counts + pad_start)[None, :] - jnp.arange(n_tiles)[:, None] * tm
    tile_valid = jnp.clip(jnp.sum(jnp.where(of_tile, rows_left, 0), axis=1), 0, tm).astype(jnp.int32)
    used = jnp.stack([tile_end[-1], jnp.sum(nonempty)]).astype(jnp.int32)
    return dest0, dest1, tile_run, tile_valid, run_expert, used, n_tiles * tm


def _layer(x, c, rel_bias, w_ada, b_ada, w_in, w_gla_gate, b_gla_gate, gla_norm, w_proj_gla, w_proj_attn, w_out,
           ln1_g, ln1_b, w_rg, b_rg, w_re, b_re, w_eg, w_eu, w_ed, ln2_g, ln2_b):
    bsz, s, d = x.shape
    alpha = (2.0 * DEPTH) ** 0.25
    mods = _ada_mods(c, w_ada, b_ada)
    sh1, sc1, g1, sh2, sc2, g2 = [m.reshape(bsz, 1, d) for m in jnp.split(mods, N_MOD, axis=-1)]

    lr0 = d // 2 * 2 + 2 * d
    z = _in_projection(x, sc1, sh1, _prep_in_weight(w_in, lr0), w_gla_gate, b_gla_gate)

    y_gla = _gla(z["q_in"], z["k_in"], z["q_st"], z["k_st"], z["dec"], z["v_gla"], z["r_gla"], gla_norm)

    o_groups, lse_groups = [], []
    for g, (window, dilation) in enumerate(DIL_PATTERNS):
        l = s // dilation
        qg, kg, vg = (z[f"{n}{g}"].reshape(bsz * dilation, l, DIL_GROUP_WIDTH) for n in ("q_att", "k_att", "v_att"))
        table = rel_bias[:, g * DIL_HEADS_PER_GROUP:(g + 1) * DIL_HEADS_PER_GROUP]
        o, lse = _dilated_group_attention(qg, kg, vg, table, window, dilation)
        o_groups.append(o.reshape(bsz, dilation, l, DIL_GROUP_WIDTH))
        lse_groups.append(lse.reshape(bsz, dilation, l, DIL_GROUP_WIDTH))

    wr = jnp.concatenate([w_rg, w_re, jnp.zeros((d, LANES - MOE_GROUPS - MOE_TOTAL), F32)], axis=1)
    br = jnp.concatenate([b_rg, b_re, jnp.zeros((LANES - MOE_GROUPS - MOE_TOTAL,), F32)]).reshape(1, LANES)
    x1, u2, route, ew, cnt = _merge(alpha, y_gla, o_groups, lse_groups, z["g_gla"], z["g_att"], x, g1, sc2, sh2,
                                    ln1_g, ln1_b, w_proj_gla.astype(BF16), w_proj_attn.astype(BF16),
                                    w_out.astype(BF16), wr, br)

    n = bsz * s
    counts = cnt[MOE_GROUPS:MOE_GROUPS + MOE_TOTAL, 0]
    dest0, dest1, tile_run, tile_valid, run_expert, used, n_rows = _dispatch_plan(route, counts)
    xg = _sc_scatter_rows(u2.reshape(n, d // 2), dest0, dest1, n_rows)
    ff = w_eg.shape[-1]
    yo = _expert_ffn(tile_run, tile_valid, run_expert, used, xg, w_eg.reshape(MOE_TOTAL, d, ff),
                     w_eu.reshape(MOE_TOTAL, d, ff), w_ed.reshape(MOE_TOTAL, ff, d))
    ya, yb = (y.reshape(bsz, s, d // 2) for y in _sc_gather_rows(yo, dest0, dest1))
    return _final(alpha, x1, ya, yb, ew, g2, ln2_g, ln2_b)


def kernel(x, c, rel_bias, w_ada, b_ada, w_in, w_gla_gate, b_gla_gate, gla_norm, w_proj_gla, w_proj_attn, w_out,
           ln1_g, ln1_b, w_router_group, b_router_group, w_router_expert, b_router_expert, w_exp_gate, w_exp_up,
           w_exp_down, ln2_g, ln2_b):
    assert w_ada.shape[0] == DEPTH
    return _layer(x, c, rel_bias, w_ada[0], b_ada[0], w_in[0:1], w_gla_gate[0], b_gla_gate[0], gla_norm[0],
                  w_proj_gla[0], w_proj_attn[0], w_out[0], ln1_g[0], ln1_b[0], w_router_group[0],
                  b_router_group[0], w_router_expert[0], b_router_expert[0], w_exp_gate[0], w_exp_up[0],
                  w_exp_down[0], ln2_g[0], ln2_b[0])
```

```python
import functools
import math

import numpy as np
import jax
import jax.numpy as jnp
from jax import lax
from jax.experimental import pallas as pl
from jax.experimental.pallas import tpu as pltpu
from jax.experimental.pallas import tpu_sc as plsc

F32 = jnp.float32
BF16 = jnp.bfloat16

N_MOD = 6
GLA_HEADS = 4
GLA_LOWRANK = 16
GLA_TAU = 16.0
GLA_CHUNK = 64
DIL_PATTERNS = ((128, 1), (512, 4), (2048, 16))
DIL_GROUPS = len(DIL_PATTERNS)
DIL_HEADS_PER_GROUP = 8
DIL_HEAD_DIM = 64
DIL_GROUP_WIDTH = DIL_HEADS_PER_GROUP * DIL_HEAD_DIM
DIL_BLOCK = 128
REL_BUCKETS = 32
REL_MAX_DIST = 2048
MOE_GROUPS = 4
MOE_EXPERTS = 8
MOE_TOTAL = MOE_GROUPS * MOE_EXPERTS
LN_EPS = 1e-5
DEPTH = 1

LANES = 128
VMEM_LIMIT = 56 * 1024 * 1024
LOG2E = 1.4426950408889634
LN2 = 0.6931471805599453
NEG = -1e30
ROW_TILE = 512
PROJ_CHUNK = 512
FINAL_TILE = 1024
EXPERT_TILE = 512
EXPERT_BLOCK = 128
GLA_STEP_CHUNKS = 16
ATT_STEP_BLOCKS = 8
ROUTER_ROWS = 40
ROUTE_ROWS = 8

NT_DIMS = (((1,), (1,)), ((), ()))
TN_DIMS = (((0,), (0,)), ((), ()))


def _cparams(sem):
    return pltpu.CompilerParams(dimension_semantics=sem, vmem_limit_bytes=VMEM_LIMIT)


def _sigmoid(x):
    return 0.5 * jnp.tanh(0.5 * x) + 0.5


def _silu(x):
    return x * _sigmoid(x)


def _layer_norm(x, g, b):
    mu = jnp.mean(x, axis=-1, keepdims=True)
    xc = x - mu
    var = jnp.mean(xc * xc, axis=-1, keepdims=True)
    return xc * lax.rsqrt(var + LN_EPS) * g + b


def _pack_bf16_pairs(x):
    w = x.shape[1] // 2
    lo = lax.bitcast_convert_type(x[:, :w].astype(BF16).astype(F32), jnp.uint32) >> 16
    hi = lax.bitcast_convert_type(x[:, w:].astype(BF16).astype(F32), jnp.uint32) & jnp.uint32(0xFFFF0000)
    return lax.bitcast_convert_type(lo | hi, jnp.int32)


def _unpack_bf16_pairs(p):
    u = lax.bitcast_convert_type(p, jnp.uint32)
    lo = lax.bitcast_convert_type(u << 16, F32)
    hi = lax.bitcast_convert_type(u & jnp.uint32(0xFFFF0000), F32)
    return jnp.concatenate([lo, hi], axis=1)


def _mods_kernel(ct_ref, w_ref, b_ref, o_ref):
    a = _silu(ct_ref[...])
    w = w_ref[...]
    for b in range(a.shape[1]):
        o_ref[b:b + 1, :] = jnp.sum(a[:, b:b + 1] * w, axis=0, keepdims=True) + b_ref[...]


def _ada_mods(c, w, b):
    bsz, d = c.shape
    n = w.shape[1]
    tn = 1536
    assert n % tn == 0
    return pl.pallas_call(
        _mods_kernel,
        out_shape=jax.ShapeDtypeStruct((bsz, n), F32),
        grid=(n // tn,),
        in_specs=[pl.BlockSpec((d, bsz), lambda j: (0, 0)),
                  pl.BlockSpec((d, tn), lambda j: (0, j)),
                  pl.BlockSpec((1, tn), lambda j: (0, j))],
        out_specs=pl.BlockSpec((bsz, tn), lambda j: (0, j)),
        compiler_params=_cparams(("arbitrary",)),
        name="ada_mods",
    )(c.T, w, b.reshape(1, n))


def _proj_pieces(d_model):
    dk = d_model // 2
    pieces = [("q_gla", dk, "scale_q_gla"), ("k_gla", dk, None), ("v_gla", d_model, None), ("r_gla", d_model, "silu")]
    for name, post in (("q_att", "scale_q_att"), ("k_att", None), ("v_att", None)):
        for g, (_, dilation) in enumerate(DIL_PATTERNS):
            pieces.append((f"{name}{g}", DIL_GROUP_WIDTH, (post, dilation)))
    pieces += [("g_gla", d_model, "sigmoid"), ("g_att", d_model, "sigmoid"), ("lr", LANES, "lowrank")]
    return tuple(pieces)


WT_BLOCK = 512


def _wprep_kernel(n_main_blocks, w_ref, o_ref):
    blk = w_ref[0]
    row = lax.broadcasted_iota(jnp.int32, blk.shape, 0)
    keep = (pl.program_id(0) < n_main_blocks) | (row < GLA_LOWRANK)
    o_ref[...] = jnp.where(keep, blk, 0.0).astype(BF16)


def _prep_in_weight(w_in, lr0):
    w_t = jnp.swapaxes(w_in, 1, 2)
    _, n_in, d = w_t.shape
    n_main = n_in - GLA_LOWRANK
    assert lr0 % WT_BLOCK == 0 and n_main % WT_BLOCK == 0
    n_main_blocks = n_main // WT_BLOCK

    def src_row(j):
        start = j * WT_BLOCK
        octet = jnp.where(j < n_main_blocks, (start + jnp.where(start >= lr0, GLA_LOWRANK, 0)) // 8, lr0 // 8)
        return octet * 8

    return pl.pallas_call(
        functools.partial(_wprep_kernel, n_main_blocks),
        out_shape=jax.ShapeDtypeStruct((n_main + WT_BLOCK, d), BF16),
        grid=(n_main_blocks + 1,),
        in_specs=[pl.BlockSpec((pl.Element(1), pl.Element(WT_BLOCK), pl.Element(d)), lambda j: (0, src_row(j), 0))],
        out_specs=pl.BlockSpec((WT_BLOCK, d), lambda j: (j, 0)),
        compiler_params=_cparams(("parallel",)),
        name="prep_in_weight",
    )(w_t)


GLA_HELD = ("lr", "q_gla", "k_gla")


def _gla_operands(hold, wg_ref, bg_ref, qin_ref, kin_ref, qst_ref, kst_ref, dec_ref):
    c = GLA_CHUNK
    tm = hold["q_gla"].shape[0]
    tril = (lax.broadcasted_iota(jnp.int32, (c, c), 0) >= lax.broadcasted_iota(jnp.int32, (c, c), 1)).astype(BF16)
    mid = c // 2 - 1
    lr, wg = hold["lr"][:, 0:GLA_LOWRANK], wg_ref[...]
    lr_hi, wg_hi = lr.astype(BF16), wg.astype(BF16)
    lr_lo, wg_lo = (lr - lr_hi.astype(F32)).astype(BF16), (wg - wg_hi.astype(F32)).astype(BF16)
    gate_in = (jnp.dot(lr_hi, wg_hi, preferred_element_type=F32) + jnp.dot(lr_lo, wg_hi, preferred_element_type=F32)
               + jnp.dot(lr_hi, wg_lo, preferred_element_type=F32)) + bg_ref[...]
    g_all = (jnp.minimum(gate_in, 0.0) - jnp.log(1.0 + jnp.exp(-jnp.abs(gate_in)))) * (1.0 / GLA_TAU)
    g_hi = g_all.astype(BF16)
    g_lo = (g_all - g_hi.astype(F32)).astype(BF16)
    for ci in range(tm // c):
        rows = slice(ci * c, (ci + 1) * c)
        bc = jnp.dot(tril, g_hi[rows], preferred_element_type=F32) + jnp.dot(tril, g_lo[rows], preferred_element_type=F32)
        b_mid = bc[mid:mid + 1, :]
        b_last = bc[c - 1:c, :]
        qf = hold["q_gla"][rows, :]
        kf = hold["k_gla"][rows, :]
        q_in = qf * jnp.exp(bc - b_mid)
        k_in = kf * jnp.exp(b_mid - bc)
        qin_ref[0, rows, :] = q_in.astype(BF16)
        kin_ref[0, rows, :] = k_in.astype(BF16)
        qst_ref[0, rows, :] = (q_in * jnp.exp(b_mid)).astype(BF16)
        kst_ref[0, rows, :] = (k_in * jnp.exp(b_last - b_mid)).astype(BF16)
        dec_ref[0, ci:ci + 1, :] = jnp.exp(b_last)


def _proj_kernel(pieces, head_k, x_ref, sc_ref, sh_ref, w_ref, wg_ref, bg_ref, *refs):
    n_out = len(pieces) - len(GLA_HELD)
    out_refs = dict(zip([p[0] for p in pieces if p[0] not in GLA_HELD], refs[:n_out]))
    gla_out_refs = refs[n_out:n_out + 5]
    stage_ref = refs[n_out + 5]
    hold = dict(zip(GLA_HELD, refs[n_out + 6:]))
    tm = x_ref.shape[1]
    u = (x_ref[0] * (1.0 + sc_ref[0]) + sh_ref[0]).astype(BF16)
    offsets, off = {}, 0
    for name, width, _ in pieces:
        offsets[name] = off
        off += width
    by_name = {p[0]: p for p in pieces}
    held = [(by_name[n], 0) for n in GLA_HELD]
    rest = [(p, c0) for p in pieces if p[0] not in GLA_HELD for c0 in range(0, p[1], min(p[1], PROJ_CHUNK))]
    for n, (piece, c0) in enumerate(held + rest):
        if n == len(held):
            _gla_operands(hold, wg_ref, bg_ref, *gla_out_refs)
        name, width, post = piece
        o_ref = hold[name] if name in GLA_HELD else out_refs[name]
        off = offsets[name]
        chunk = min(width, PROJ_CHUNK)
        acc = lax.dot_general(u, w_ref[off + c0:off + c0 + chunk, :], NT_DIMS, preferred_element_type=F32)
        if post == "silu":
            acc = _silu(acc)
        elif post == "sigmoid":
            acc = _sigmoid(acc)
        elif post == "scale_q_gla":
            acc = acc * (head_k ** -0.5)
        if name in GLA_HELD:
            o_ref[...] = acc
        elif isinstance(post, tuple):
            scale, dilation = post
            if scale is not None:
                acc = acc * (DIL_HEAD_DIM ** -0.5 * LOG2E)
            if dilation == 1:
                o_ref[0, 0] = acc.astype(o_ref.dtype)
            else:
                for t in range(width // LANES):
                    stage_ref[t] = acc[:, t * LANES:(t + 1) * LANES]
                for r in range(dilation):
                    for t in range(width // LANES):
                        o_ref[0, r, :, t * LANES:(t + 1) * LANES] = stage_ref[
                            t, pl.ds(r, tm // dilation, stride=dilation), :].astype(o_ref.dtype)
        else:
            o_ref[0, :, c0:c0 + chunk] = acc.astype(o_ref.dtype)


def _in_projection(x, sc1, sh1, w_perm, w_gate, b_gate):
    bsz, s, d = x.shape
    pieces = _proj_pieces(d)
    assert sum(p[1] for p in pieces) <= w_perm.shape[0]
    tm = min(ROW_TILE, s)
    assert s % tm == 0 and tm % (8 * GLA_CHUNK) == 0
    dk = d // 2
    head_k = dk // GLA_HEADS
    out_shape, out_specs = [], []
    for name, width, post in pieces:
        if name in GLA_HELD:
            continue
        if isinstance(post, tuple):
            dil = post[1]
            assert tm % (dil * 16) == 0
            out_shape.append(jax.ShapeDtypeStruct((bsz, dil, s // dil, width), BF16))
            out_specs.append(pl.BlockSpec((1, dil, tm // dil, width), lambda b, i: (b, 0, i, 0)))
        else:
            out_shape.append(jax.ShapeDtypeStruct((bsz, s, width), BF16))
            out_specs.append(pl.BlockSpec((1, tm, width), lambda b, i: (b, i, 0)))
    row = lambda w: pl.BlockSpec((1, tm, w), lambda b, i: (b, i, 0))
    gla_names = ("q_in", "k_in", "q_st", "k_st", "dec")
    out_shape += [jax.ShapeDtypeStruct((bsz, s, dk), BF16)] * 4 + [jax.ShapeDtypeStruct((bsz, s // GLA_CHUNK, dk), F32)]
    out_specs += [row(dk)] * 4 + [pl.BlockSpec((1, tm // GLA_CHUNK, dk), lambda b, i: (b, i, 0))]
    bg = b_gate.reshape(1, dk)
    full = lambda a: pl.BlockSpec(a.shape, lambda b, i: (0,) * a.ndim)
    outs = pl.pallas_call(
        functools.partial(_proj_kernel, pieces, head_k),
        out_shape=out_shape,
        grid=(bsz, s // tm),
        in_specs=[row(d),
                  pl.BlockSpec((1, 1, d), lambda b, i: (b, 0, 0)),
                  pl.BlockSpec((1, 1, d), lambda b, i: (b, 0, 0)),
                  pl.BlockSpec(w_perm.shape, lambda b, i: (0, 0), pipeline_mode=pl.Buffered(1)),
                  full(w_gate), full(bg)],
        out_specs=out_specs,
        scratch_shapes=[pltpu.VMEM((DIL_GROUP_WIDTH // LANES, tm, LANES), F32),
                        pltpu.VMEM((tm, LANES), F32), pltpu.VMEM((tm, dk), F32), pltpu.VMEM((tm, dk), F32)],
        compiler_params=_cparams(("parallel", "arbitrary")),
        name="in_projection",
    )(x, sc1, sh1, w_perm, w_gate, bg)
    return dict(zip([p[0] for p in pieces if p[0] not in GLA_HELD] + list(gla_names), outs))


def _gla_kernel(n_chunks, head_k, head_v, qin_ref, kin_ref, qst_ref, kst_ref, dec_ref, v_ref, r_ref, ng_ref, o_ref,
                state_ref):
    @pl.when(pl.program_id(1) == 0)
    def _():
        state_ref[...] = jnp.zeros_like(state_ref)

    c = GLA_CHUNK
    causal = lax.broadcasted_iota(jnp.int32, (c, c), 0) >= lax.broadcasted_iota(jnp.int32, (c, c), 1)
    for ci in range(n_chunks):
        rows = slice(ci * c, (ci + 1) * c)
        for h in range(GLA_HEADS):
            ks = slice(h * head_k, (h + 1) * head_k)
            vs = slice(h * head_v, (h + 1) * head_v)
            vh = v_ref[0, rows, vs]
            att = lax.dot_general(qin_ref[0, rows, ks], kin_ref[0, rows, ks], NT_DIMS, preferred_element_type=F32)
            att = jnp.where(causal, att, 0.0).astype(BF16)
            st = state_ref[h]
            o = jnp.dot(att, vh, preferred_element_type=F32)
            o = o + lax.dot_general(qst_ref[0, rows, ks], st.astype(BF16), NT_DIMS, preferred_element_type=F32)
            kv_t = lax.dot_general(vh, kst_ref[0, rows, ks], TN_DIMS, preferred_element_type=F32)
            state_ref[h] = st * dec_ref[0, ci:ci + 1, ks] + kv_t
            ms = jnp.mean(o * o, axis=-1, keepdims=True)
            o = o * lax.rsqrt(ms + LN_EPS) * ng_ref[:, vs] * r_ref[0, rows, vs].astype(F32)
            o_ref[0, rows, vs] = o.astype(o_ref.dtype)


def _gla(q_in, k_in, q_st, k_st, dec, v, r_silu, norm_g):
    bsz, s, dk = q_in.shape
    dv = v.shape[-1]
    head_k, head_v = dk // GLA_HEADS, dv // GLA_HEADS
    n_chunks = min(GLA_STEP_CHUNKS, s // GLA_CHUNK)
    ct = GLA_CHUNK * n_chunks
    assert s % ct == 0
    row_spec = lambda w: pl.BlockSpec((1, ct, w), lambda b, i: (b, i, 0))
    full = lambda a: pl.BlockSpec(a.shape, lambda b, i: (0,) * a.ndim)
    ng = norm_g.reshape(1, dv)
    return pl.pallas_call(
        functools.partial(_gla_kernel, n_chunks, head_k, head_v),
        out_shape=jax.ShapeDtypeStruct((bsz, s, dv), BF16),
        grid=(bsz, s // ct),
        in_specs=[row_spec(dk)] * 4 + [pl.BlockSpec((1, n_chunks, dk), lambda b, i: (b, i, 0)),
                                       row_spec(dv), row_spec(dv), full(ng)],
        out_specs=row_spec(dv),
        scratch_shapes=[pltpu.VMEM((GLA_HEADS, head_v, head_k), F32)],
        compiler_params=_cparams(("parallel", "arbitrary")),
        name="gla",
    )(q_in, k_in, q_st, k_st, dec, v, r_silu, ng)


def _t5_bucket_np(dist):
    exact = REL_BUCKETS // 2
    d = np.maximum(dist, 1).astype(np.float32)
    large = exact + (np.log(d / np.float32(exact)) / np.float32(math.log(REL_MAX_DIST / exact))
                     * np.float32(REL_BUCKETS - exact)).astype(np.int32)
    large = np.minimum(large, REL_BUCKETS - 1)
    return np.where(dist < exact, dist, large).astype(np.int32)


def _band_tables(window, dilation):
    qi = np.arange(DIL_BLOCK)[:, None]
    kj = np.arange(2 * DIL_BLOCK)[None, :]
    m = qi + DIL_BLOCK - kj
    n_steps = window // dilation
    band = (m >= 0) & (m <= n_steps)
    bucket = _t5_bucket_np(np.clip(m, 0, n_steps) * dilation)
    return np.where(band, bucket, -1).astype(np.int32)


def _attn_kernel(nq, table_ref, bucket_ref, q_ref, kp_ref, kc_ref, vp_ref, vc_ref, o_ref, lse_ref,
                 bias_ref, p_ref):
    i = pl.program_id(1)
    blk = DIL_BLOCK
    hpg = DIL_HEADS_PER_GROUP
    n_pairs = hpg // 2

    @pl.when((pl.program_id(0) == 0) & (i == 0))
    def _():
        bucket = bucket_ref[...]
        for h in range(hpg):
            acc = jnp.full(bucket.shape, NEG, F32)
            for bkt in range(REL_BUCKETS):
                acc = jnp.where(bucket == bkt, table_ref[bkt, h] * LOG2E, acc)
            bias_ref[h * blk:(h + 1) * blk, :] = acc

    lane = lax.broadcasted_iota(jnp.int32, (blk, LANES), 1)
    low = lane < DIL_HEAD_DIM
    ones_rhs = jnp.ones((2 * blk, LANES), BF16)

    def windows(ref_p, ref_c, sq, qb, cols):
        if qb == 0:
            return jnp.concatenate([ref_p[sq, :, cols], ref_c[sq, 0:blk, cols]], axis=0)
        return ref_c[sq, (qb - 1) * blk:(qb + 1) * blk, cols]

    key_lane = lax.broadcasted_iota(jnp.int32, (1, 2 * blk), 1)
    no_prev = jnp.where((key_lane < blk) & (i == 0), NEG, 0.0)
    items = [(sq, qb, hp) for sq in range(q_ref.shape[0]) for qb in range(nq) for hp in range(n_pairs)]

    mxs = []
    for n, (sq, qb, hp) in enumerate(items):
        rows = slice(qb * blk, (qb + 1) * blk)
        cols = slice(hp * LANES, (hp + 1) * LANES)
        qp = q_ref[sq, rows, cols]
        zero = jnp.zeros_like(qp)
        qq = jnp.concatenate([jnp.where(low, qp, zero), jnp.where(low, zero, qp)], axis=0)
        keys = windows(kp_ref, kc_ref, sq, qb, cols)
        s = lax.dot_general(qq, keys, NT_DIMS, preferred_element_type=F32) + bias_ref[2 * hp * blk:(2 * hp + 2) * blk, :]
        if qb == 0:
            s = s + no_prev
        mx = jnp.max(s, axis=-1, keepdims=True)
        p_ref[n * 2 * blk:(n + 1) * 2 * blk, :] = jnp.exp2(s - mx).astype(BF16)
        mxs.append(mx)

    for n, (sq, qb, hp) in enumerate(items):
        rows = slice(qb * blk, (qb + 1) * blk)
        cols = slice(hp * LANES, (hp + 1) * LANES)
        vals = windows(vp_ref, vc_ref, sq, qb, cols)
        rhs = jnp.concatenate([vals, ones_rhs], axis=1)
        res = jnp.dot(p_ref[n * 2 * blk:(n + 1) * 2 * blk, :], rhs, preferred_element_type=F32)
        num = jnp.where(low, res[0:blk, 0:LANES], res[blk:2 * blk, 0:LANES])
        den = jnp.where(low, res[0:blk, LANES:], res[blk:2 * blk, LANES:])
        mx = jnp.where(low, mxs[n][0:blk], mxs[n][blk:2 * blk])
        o_ref[sq, rows, cols] = (num / den).astype(o_ref.dtype)
        lse_ref[sq, rows, cols] = (mx + jnp.log2(den)) * LN2


def _dilated_group_attention(q, k, v, table, window, dilation):
    bb, l, w = q.shape
    nq = min(ATT_STEP_BLOCKS, l // DIL_BLOCK)
    nsq = ATT_STEP_BLOCKS // nq
    assert l % (nq * DIL_BLOCK) == 0 and bb % nsq == 0
    steps = l // (nq * DIL_BLOCK)
    bucket = jnp.asarray(_band_tables(window, dilation))
    cur = pl.BlockSpec((nsq, nq * DIL_BLOCK, w), lambda b, i: (b, i, 0))
    prev = pl.BlockSpec((nsq, DIL_BLOCK, w), lambda b, i: (b, jnp.maximum(nq * i - 1, 0), 0))
    rows_all = nsq * nq * DIL_HEADS_PER_GROUP * DIL_BLOCK
    return pl.pallas_call(
        functools.partial(_attn_kernel, nq),
        out_shape=[jax.ShapeDtypeStruct((bb, l, w), BF16), jax.ShapeDtypeStruct((bb, l, w), F32)],
        grid=(bb // nsq, steps),
        in_specs=[pl.BlockSpec(memory_space=pltpu.SMEM),
                  pl.BlockSpec(bucket.shape, lambda b, i: (0, 0)),
                  cur, prev, cur, prev, cur],
        out_specs=[cur, cur],
        scratch_shapes=[pltpu.VMEM((DIL_HEADS_PER_GROUP * DIL_BLOCK, 2 * DIL_BLOCK), F32),
                        pltpu.VMEM((rows_all, 2 * DIL_BLOCK), BF16)],
        compiler_params=_cparams(("arbitrary", "arbitrary")),
        name=f"dilated_attn_d{dilation}",
    )(table, bucket, q, k, k, v, v)


def _merge_kernel(alpha, dilations, ygla_ref, o0_ref, o1_ref, o2_ref, l0_ref, l1_ref, l2_ref, gg_ref, ga_ref, x_ref,
                  g1_ref, sc2_ref, sh2_ref, ln_g_ref, ln_b_ref, wpg_ref, wpa_ref, wout_ref, wr_ref, br_ref, utri_ref,
                  x1_ref, u2_ref, route_ref, ew_ref, cnt_ref, stage_ref, carry_ref):
    tm = x_ref.shape[1]

    @pl.when((pl.program_id(0) == 0) & (pl.program_id(1) == 0))
    def _():
        carry_ref[...] = jnp.zeros_like(carry_ref)

    n_lt = DIL_GROUP_WIDTH // LANES
    group_refs = tuple(zip((l0_ref, l1_ref, l2_ref), (o0_ref, o1_ref, o2_ref), dilations))
    for gi, (l_ref, o_ref, dil) in enumerate(group_refs):
        if dil > 1:
            for slot, ref in ((2 * gi, l_ref), (2 * gi + 1, o_ref)):
                for r in range(dil):
                    for t in range(n_lt):
                        stage_ref[slot, t, pl.ds(r, tm // dil, stride=dil), :] = ref[
                            0, r, :, t * LANES:(t + 1) * LANES].astype(F32)

    def natural(ref, dil, slot):
        if dil == 1:
            return ref[0, 0].astype(F32)
        return jnp.concatenate([stage_ref[slot, t] for t in range(n_lt)], axis=1)

    lses = [natural(l_ref, dil, 2 * gi) for gi, (l_ref, _, dil) in enumerate(group_refs)]
    outs = [natural(o_ref, dil, 2 * gi + 1) for gi, (_, o_ref, dil) in enumerate(group_refs)]
    lm = jnp.maximum(jnp.maximum(lses[0], lses[1]), lses[2])
    es = [jnp.exp(l - lm) for l in lses]
    y_att = (es[0] * outs[0] + es[1] * outs[1] + es[2] * outs[2]) / (es[0] + es[1] + es[2])

    p_gla = jnp.dot(ygla_ref[0], wpg_ref[...], preferred_element_type=F32)
    p_att = jnp.dot(y_att.astype(BF16), wpa_ref[...], preferred_element_type=F32)
    merged = gg_ref[0].astype(F32) * p_gla + ga_ref[0].astype(F32) * p_att
    y = jnp.dot(merged.astype(BF16), wout_ref[...], preferred_element_type=F32)
    x1 = _layer_norm(alpha * x_ref[0] + g1_ref[0] * y, ln_g_ref[...], ln_b_ref[...])
    x1_ref[0] = x1
    u2 = x1 * (1.0 + sc2_ref[0]) + sh2_ref[0]
    u2_ref[0] = _pack_bf16_pairs(u2)

    logits = jnp.dot(u2.astype(BF16), wr_ref[...].astype(BF16), preferred_element_type=F32) + br_ref[...]
    lt = jnp.transpose(logits)[0:ROUTER_ROWS, :]
    rowi = lax.broadcasted_iota(jnp.int32, lt.shape, 0)
    big = jnp.int32(LANES)
    lg = jnp.where(rowi < MOE_GROUPS, lt, NEG)
    gmax = jnp.max(lg, axis=0, keepdims=True)
    gidx = jnp.min(jnp.where(lg == gmax, rowi, big), axis=0, keepdims=True)
    gval = 1.0 / jnp.sum(jnp.exp(lg - gmax), axis=0, keepdims=True)
    first = MOE_GROUPS + gidx * MOE_EXPERTS
    le = jnp.where((rowi >= first) & (rowi < first + MOE_EXPERTS), lt, NEG)
    m1 = jnp.max(le, axis=0, keepdims=True)
    i1 = jnp.min(jnp.where(le == m1, rowi, big), axis=0, keepdims=True)
    le2 = jnp.where(rowi == i1, NEG, le)
    m2 = jnp.max(le2, axis=0, keepdims=True)
    i2 = jnp.min(jnp.where(le2 == m2, rowi, big), axis=0, keepdims=True)
    t = jnp.exp(m2 - m1)
    w1 = 1.0 / (1.0 + t)
    w2 = t * w1

    hit1, hit2 = rowi == i1, rowi == i2
    onehot = jnp.where(hit1 | hit2, 1.0, 0.0)
    earlier = jnp.dot(onehot.astype(BF16), utri_ref[...], preferred_element_type=F32) + carry_ref[...]
    rank1 = jnp.sum(jnp.where(hit1, earlier, 0.0), axis=0, keepdims=True).astype(jnp.int32)
    rank2 = jnp.sum(jnp.where(hit2, earlier, 0.0), axis=0, keepdims=True).astype(jnp.int32)
    carry = carry_ref[...] + jnp.sum(onehot, axis=1, keepdims=True)
    carry_ref[...] = carry
    cnt_ref[...] = jnp.broadcast_to(carry, cnt_ref.shape).astype(jnp.int32)
    r8 = lax.broadcasted_iota(jnp.int32, (ROUTE_ROWS, tm), 0)
    route_ref[0] = jnp.where(r8 == 0, i1 - MOE_GROUPS, jnp.where(r8 == 1, i2 - MOE_GROUPS,
                             jnp.where(r8 == 2, rank1, jnp.where(r8 == 3, rank2, 0))))
    r128 = lax.broadcasted_iota(jnp.int32, (LANES, tm), 0)
    ew_ref[0] = jnp.transpose(jnp.where(r128 == 0, gval * w1, jnp.where(r128 == 1, gval * w2, 0.0)))


def _merge(alpha, y_gla, o_groups, lse_groups, g_gla, g_att, x, g1, sc2, sh2, ln_g, ln_b, wpg, wpa, wout, wr, br):
    bsz, s, d = x.shape
    tm = min(ROW_TILE, s)
    assert s % tm == 0
    dilations = tuple(dil for _, dil in DIL_PATTERNS)
    row = lambda w: pl.BlockSpec((1, tm, w), lambda b, i: (b, i, 0))
    sub = lambda dil: pl.BlockSpec((1, dil, tm // dil, DIL_GROUP_WIDTH), lambda b, i: (b, 0, i, 0))
    per_b = pl.BlockSpec((1, 1, d), lambda b, i: (b, 0, 0))
    full = lambda a: pl.BlockSpec(a.shape, lambda b, i: (0,) * a.ndim)
    ln_g2, ln_b2 = ln_g.reshape(1, d), ln_b.reshape(1, d)
    utri = jnp.asarray(np.triu(np.ones((tm, tm), np.float32), 1), BF16)
    return pl.pallas_call(
        functools.partial(_merge_kernel, alpha, dilations),
        out_shape=[jax.ShapeDtypeStruct((bsz, s, d), F32), jax.ShapeDtypeStruct((bsz, s, d // 2), jnp.int32),
                   jax.ShapeDtypeStruct((bsz, ROUTE_ROWS, s), jnp.int32), jax.ShapeDtypeStruct((bsz, s, LANES), F32),
                   jax.ShapeDtypeStruct((ROUTER_ROWS, LANES), jnp.int32)],
        grid=(bsz, s // tm),
        in_specs=[row(y_gla.shape[-1])] + [sub(dil) for dil in dilations] * 2
                 + [row(d), row(d), row(d), per_b, per_b, per_b, full(ln_g2), full(ln_b2),
                    full(wpg), full(wpa), full(wout), full(wr), full(br), full(utri)],
        out_specs=[row(d), row(d // 2), pl.BlockSpec((1, ROUTE_ROWS, tm), lambda b, i: (b, 0, i)), row(LANES),
                   pl.BlockSpec((ROUTER_ROWS, LANES), lambda b, i: (0, 0))],
        scratch_shapes=[pltpu.VMEM((2 * DIL_GROUPS, DIL_GROUP_WIDTH // LANES, tm, LANES), F32),
                        pltpu.VMEM((ROUTER_ROWS, 1), F32)],
        compiler_params=_cparams(("arbitrary", "arbitrary")),
        name="merge_ln1_router",
    )(y_gla, *o_groups, *lse_groups, g_gla, g_att, x, g1, sc2, sh2, ln_g2, ln_b2, wpg, wpa, wout, wr, br, utri)


def _expert_kernel(run_ref, valid_ref, rexp_ref, used_ref, x_ref, wg_hbm, wu_hbm, wd_hbm, o_ref,
                   wg_f, wu_f, wd_f, wg_s, wu_s, wd_s, sem):
    t = pl.program_id(0)
    n_tiles_used, n_runs = used_ref[0], used_ref[1]
    run = run_ref[t]
    active = t < n_tiles_used
    first_of_run = (t == 0) | (run_ref[jnp.maximum(t - 1, 0)] != run)

    def weight_copies(r):
        e, slot = rexp_ref[r], r % 2
        return [pltpu.make_async_copy(hbm.at[e], buf.at[slot], sem.at[slot, j])
                for j, (hbm, buf) in enumerate(((wg_hbm, wg_f), (wu_hbm, wu_f), (wd_hbm, wd_f)))]

    @pl.when(active & (t == 0))
    def _():
        for cp in weight_copies(0):
            cp.start()

    @pl.when(active & first_of_run)
    def _():
        @pl.when(run + 1 < n_runs)
        def _():
            for cp in weight_copies(run + 1):
                cp.start()

        for cp in weight_copies(run):
            cp.wait()
        slot = run % 2
        wg_s[...] = wg_f[slot].astype(BF16)
        wu_s[...] = wu_f[slot].astype(BF16)
        wd_s[...] = wd_f[slot].astype(BF16)

    n_valid = jnp.where(active, valid_ref[t], 0)

    def ffn(rows):
        xt = _unpack_bf16_pairs(x_ref[rows, :]).astype(BF16)
        hg = jnp.dot(xt, wg_s[...], preferred_element_type=F32)
        hu = jnp.dot(xt, wu_s[...], preferred_element_type=F32)
        h = (_silu(hg) * hu).astype(BF16)
        o_ref[rows, :] = _pack_bf16_pairs(jnp.dot(h, wd_s[...], preferred_element_type=F32))

    def zero(rows):
        o_ref[rows, :] = jnp.zeros((rows.stop - rows.start, o_ref.shape[1]), o_ref.dtype)

    tm = x_ref.shape[0]
    n_blocks = tm // EXPERT_BLOCK
    for k in range(n_blocks + 1):
        lo, hi = (k - 1) * EXPERT_BLOCK, k * EXPERT_BLOCK

        @pl.when((n_valid > lo) & (n_valid <= hi) if 0 < k < n_blocks else (n_valid > lo if k else n_valid <= 0))
        def _():
            if k:
                ffn(slice(0, hi))
            if k < n_blocks:
                zero(slice(hi, tm))


def _expert_ffn(tile_run, tile_valid, run_expert, used, xg, w_gate, w_up, w_down):
    p = xg.shape[0]
    ne, d, ff = w_gate.shape
    tm = EXPERT_TILE
    n_tiles = p // tm
    hbm = pl.BlockSpec(memory_space=pl.ANY)

    def tile(t, run, valid, rexp, used):
        return jnp.where(t < used[0], t, n_tiles - 1), 0

    grid_spec = pltpu.PrefetchScalarGridSpec(
        num_scalar_prefetch=4,
        grid=(n_tiles,),
        in_specs=[pl.BlockSpec((tm, d // 2), tile), hbm, hbm, hbm],
        out_specs=pl.BlockSpec((tm, d // 2), tile),
        scratch_shapes=[pltpu.VMEM((2, d, ff), F32), pltpu.VMEM((2, d, ff), F32), pltpu.VMEM((2, ff, d), F32),
                        pltpu.VMEM((d, ff), BF16), pltpu.VMEM((d, ff), BF16), pltpu.VMEM((ff, d), BF16),
                        pltpu.SemaphoreType.DMA((2, 3))],
    )
    return pl.pallas_call(
        _expert_kernel,
        out_shape=jax.ShapeDtypeStruct((p, d // 2), jnp.int32),
        grid_spec=grid_spec,
        compiler_params=_cparams(("arbitrary",)),
        name="expert_ffn",
    )(tile_run, tile_valid, run_expert, used, xg, w_gate, w_up, w_down)


def _final_kernel(alpha, x1_ref, ya_ref, yb_ref, ew_ref, g2_ref, ln_g_ref, ln_b_ref, o_ref):
    ew = ew_ref[0]
    y = ew[:, 0:1] * _unpack_bf16_pairs(ya_ref[0]) + ew[:, 1:2] * _unpack_bf16_pairs(yb_ref[0])
    o_ref[0] = _layer_norm(alpha * x1_ref[0] + g2_ref[0] * y, ln_g_ref[...], ln_b_ref[...])


def _final(alpha, x1, ya, yb, ew, g2, ln_g, ln_b):
    bsz, s, d = x1.shape
    tm = min(FINAL_TILE, s)
    row = lambda w: pl.BlockSpec((1, tm, w), lambda b, i: (b, i, 0))
    full = lambda a: pl.BlockSpec(a.shape, lambda b, i: (0,) * a.ndim)
    ln_g2, ln_b2 = ln_g.reshape(1, d), ln_b.reshape(1, d)
    return pl.pallas_call(
        functools.partial(_final_kernel, alpha),
        out_shape=jax.ShapeDtypeStruct((bsz, s, d), F32),
        grid=(bsz, s // tm),
        in_specs=[row(d), row(d // 2), row(d // 2), row(LANES), pl.BlockSpec((1, 1, d), lambda b, i: (b, 0, 0)),
                  full(ln_g2), full(ln_b2)],
        out_specs=row(d),
        compiler_params=_cparams(("parallel", "arbitrary")),
        name="combine_ln2",
    )(x1, ya, yb, ew, g2, ln_g2, ln_b2)


SC_CORES = 2
SC_SUBCORES = 16
SC_CHUNK = 64


def _sc_mesh():
    return plsc.VectorSubcoreMesh(core_axis_name="c", subcore_axis_name="s")


def _sc_scatter_rows(rows, dest0, dest1, n_rows):
    n, w = rows.shape
    n_workers = SC_CORES * SC_SUBCORES
    assert n % (n_workers * SC_CHUNK) == 0
    n_chunks = n // (n_workers * SC_CHUNK)
    d0 = dest0.reshape(n // SC_CHUNK, 1, SC_CHUNK)
    d1 = dest1.reshape(n // SC_CHUNK, 1, SC_CHUNK)

    @functools.partial(
        pl.kernel, mesh=_sc_mesh(), out_type=jax.ShapeDtypeStruct((n_rows, w), rows.dtype),
        scratch_types=[pltpu.VMEM((n_chunks, 1, SC_CHUNK), jnp.int32), pltpu.VMEM((n_chunks, 1, SC_CHUNK), jnp.int32),
                       pltpu.VMEM((2, SC_CHUNK, w), rows.dtype),
                       pltpu.SemaphoreType.DMA((2,)), pltpu.SemaphoreType.DMA((2, 2))])
    def scatter_kernel(rows_hbm, d0_hbm, d1_hbm, out_hbm, i0_v, i1_v, rows_v, read_sem, scat_sem):
        wid = lax.axis_index("s") * SC_CORES + lax.axis_index("c")
        first = wid * n_chunks
        pltpu.sync_copy(d0_hbm.at[pl.ds(first, n_chunks)], i0_v)
        pltpu.sync_copy(d1_hbm.at[pl.ds(first, n_chunks)], i1_v)

        def read(j):
            return pltpu.make_async_copy(rows_hbm.at[pl.ds((first + j) * SC_CHUNK, SC_CHUNK)], rows_v.at[j % 2],
                                         read_sem.at[j % 2])

        def scatters(j):
            return [pltpu.make_async_copy(rows_v.at[j % 2], out_hbm.at[idx.at[j].at[0]], scat_sem.at[j % 2, k])
                    for k, idx in enumerate((i0_v, i1_v))]

        read(0).start()
        for j in range(n_chunks):
            read(j).wait()
            if j + 1 < n_chunks:
                if j >= 1:
                    for cp in scatters(j - 1):
                        cp.wait()
                read(j + 1).start()
            for cp in scatters(j):
                cp.start()
        for j in range(max(n_chunks - 2, 0), n_chunks):
            for cp in scatters(j):
                cp.wait()

    return scatter_kernel(rows, d0, d1)


def _sc_gather_rows(table, dest0, dest1):
    n = dest0.shape[0]
    w = table.shape[1]
    n_workers = SC_CORES * SC_SUBCORES
    assert n % (n_workers * SC_CHUNK) == 0
    n_chunks = n // (n_workers * SC_CHUNK)
    d0 = dest0.reshape(n // SC_CHUNK, 1, SC_CHUNK)
    d1 = dest1.reshape(n // SC_CHUNK, 1, SC_CHUNK)
    out = jax.ShapeDtypeStruct((n, w), table.dtype)

    @functools.partial(
        pl.kernel, mesh=_sc_mesh(), out_type=(out, out),
        scratch_types=[pltpu.VMEM((n_chunks, 1, SC_CHUNK), jnp.int32), pltpu.VMEM((n_chunks, 1, SC_CHUNK), jnp.int32),
                       pltpu.VMEM((2, SC_CHUNK, w), table.dtype),
                       pltpu.SemaphoreType.DMA((2,)), pltpu.SemaphoreType.DMA((2,))])
    def gather_kernel(table_hbm, d0_hbm, d1_hbm, a_hbm, b_hbm, i0_v, i1_v, rows_v, gather_sem, write_sem):
        wid = lax.axis_index("s") * SC_CORES + lax.axis_index("c")
        first = wid * n_chunks
        pltpu.sync_copy(d0_hbm.at[pl.ds(first, n_chunks)], i0_v)
        pltpu.sync_copy(d1_hbm.at[pl.ds(first, n_chunks)], i1_v)
        n_items = 2 * n_chunks

        def gather(m):
            idx = (i0_v, i1_v)[m % 2]
            return pltpu.make_async_copy(table_hbm.at[idx.at[m // 2].at[0]], rows_v.at[m % 2], gather_sem.at[m % 2])

        def write(m):
            o_hbm = (a_hbm, b_hbm)[m % 2]
            return pltpu.make_async_copy(rows_v.at[m % 2], o_hbm.at[pl.ds((first + m // 2) * SC_CHUNK, SC_CHUNK)],
                                         write_sem.at[m % 2])

        gather(0).start()
        for m in range(n_items):
            gather(m).wait()
            if m + 1 < n_items:
                if m >= 1:
                    write(m - 1).wait()
                gather(m + 1).start()
            write(m).start()
        for m in range(max(n_items - 2, 0), n_items):
            write(m).wait()

    return gather_kernel(table, d0, d1)


def _dispatch_plan(route, counts):
    tm = EXPERT_TILE
    e0, e1, r0, r1 = (route[:, j, :].reshape(-1) for j in range(4))
    experts = jnp.arange(MOE_TOTAL, dtype=jnp.int32)
    tiles_per = (counts + tm - 1) // tm
    tile_end = jnp.cumsum(tiles_per)
    pad_start = ((tile_end - tiles_per) * tm).astype(jnp.int32)

    def lookup(e):
        return jnp.sum(jnp.where(e[None, :] == experts[:, None], pad_start[:, None], 0), axis=0)

    dest0, dest1 = lookup(e0) + r0, lookup(e1) + r1
    n_tiles = (2 * e0.size + MOE_TOTAL * tm) // tm
    tile_expert = jnp.minimum(jnp.sum(tile_end[None, :] <= jnp.arange(n_tiles)[:, None], axis=1), MOE_TOTAL - 1)
    nonempty = counts > 0
    run_of_expert = jnp.cumsum(nonempty.astype(jnp.int32)) - 1
    run_expert = jnp.sum(jnp.where(nonempty[None, :] & (run_of_expert[None, :] == experts[:, None]),
                                   experts[None, :], 0), axis=1).astype(jnp.int32)
    of_tile = tile_expert[:, None] == experts[None, :]
    tile_run = jnp.sum(jnp.where(of_tile, run_of_expert[None, :], 0), axis=1).astype(jnp.int32)
    rows_left = (counts + pad_start)[None, :] - jnp.arange(n_tiles)[:, None] * tm
    tile_valid = jnp.clip(jnp.sum(jnp.where(of_tile, rows_left, 0), axis=1), 0, tm).astype(jnp.int32)
    used = jnp.stack([tile_end[-1], jnp.sum(nonempty)]).astype(jnp.int32)
    return dest0, dest1, tile_run, tile_valid, run_expert, used, n_tiles * tm


def _layer(x, c, rel_bias, w_ada, b_ada, w_in, w_gla_gate, b_gla_gate, gla_norm, w_proj_gla, w_proj_attn, w_out,
           ln1_g, ln1_b, w_rg, b_rg, w_re, b_re, w_eg, w_eu, w_ed, ln2_g, ln2_b):
    bsz, s, d = x.shape
    alpha = (2.0 * DEPTH) ** 0.25
    mods = _ada_mods(c, w_ada, b_ada)
    sh1, sc1, g1, sh2, sc2, g2 = [m.reshape(bsz, 1, d) for m in jnp.split(mods, N_MOD, axis=-1)]

    lr0 = d // 2 * 2 + 2 * d
    z = _in_projection(x, sc1, sh1, _prep_in_weight(w_in, lr0), w_gla_gate, b_gla_gate)

    y_gla = _gla(z["q_in"], z["k_in"], z["q_st"], z["k_st"], z["dec"], z["v_gla"], z["r_gla"], gla_norm)

    o_groups, lse_groups = [], []
    for g, (window, dilation) in enumerate(DIL_PATTERNS):
        l = s // dilation
        qg, kg, vg = (z[f"{n}{g}"].reshape(bsz * dilation, l, DIL_GROUP_WIDTH) for n in ("q_att", "k_att", "v_att"))
        table = rel_bias[:, g * DIL_HEADS_PER_GROUP:(g + 1) * DIL_HEADS_PER_GROUP]
        o, lse = _dilated_group_attention(qg, kg, vg, table, window, dilation)
        o_groups.append(o.reshape(bsz, dilation, l, DIL_GROUP_WIDTH))
        lse_groups.append(lse.reshape(bsz, dilation, l, DIL_GROUP_WIDTH))

    wr = jnp.concatenate([w_rg, w_re, jnp.zeros((d, LANES - MOE_GROUPS - MOE_TOTAL), F32)], axis=1)
    br = jnp.concatenate([b_rg, b_re, jnp.zeros((LANES - MOE_GROUPS - MOE_TOTAL,), F32)]).reshape(1, LANES)
    x1, u2, route, ew, cnt = _merge(alpha, y_gla, o_groups, lse_groups, z["g_gla"], z["g_att"], x, g1, sc2, sh2,
                                    ln1_g, ln1_b, w_proj_gla.astype(BF16), w_proj_attn.astype(BF16),
                                    w_out.astype(BF16), wr, br)

    n = bsz * s
    counts = cnt[MOE_GROUPS:MOE_GROUPS + MOE_TOTAL, 0]
    dest0, dest1, tile_run, tile_valid, run_expert, used, n_rows = _dispatch_plan(route, counts)
    xg = _sc_scatter_rows(u2.reshape(n, d // 2), dest0, dest1, n_rows)
    ff = w_eg.shape[-1]
    yo = _expert_ffn(tile_run, tile_valid, run_expert, used, xg, w_eg.reshape(MOE_TOTAL, d, ff),
                     w_eu.reshape(MOE_TOTAL, d, ff), w_ed.reshape(MOE_TOTAL, ff, d))
    ya, yb = (y.reshape(bsz, s, d // 2) for y in _sc_gather_rows(yo, dest0, dest1))
    return _final(alpha, x1, ya, yb, ew, g2, ln2_g, ln2_b)


def kernel(x, c, rel_bias, w_ada, b_ada, w_in, w_gla_gate, b_gla_gate, gla_norm, w_proj_gla, w_proj_attn, w_out,
           ln1_g, ln1_b, w_router_group, b_router_group, w_router_expert, b_router_expert, w_exp_gate, w_exp_up,
           w_exp_down, ln2_g, ln2_b):
    assert w_ada.shape[0] == DEPTH
    return _layer(x, c, rel_bias, w_ada[0], b_ada[0], w_in[0:1], w_gla_gate[0], b_gla_gate[0], gla_norm[0],
                  w_proj_gla[0], w_proj_attn[0], w_out[0], ln1_g[0], ln1_b[0], w_router_group[0],
                  b_router_group[0], w_router_expert[0], b_router_expert[0], w_exp_gate[0], w_exp_up[0],
                  w_exp_down[0], ln2_g[0], ln2_b[0])
```

```python
import functools
import math

import numpy as np
import jax
import jax.numpy as jnp
from jax import lax
from jax.experimental import pallas as pl
from jax.experimental.pallas import tpu as pltpu
from jax.experimental.pallas import tpu_sc as plsc

F32 = jnp.float32
BF16 = jnp.bfloat16

N_MOD = 6
GLA_HEADS = 4
GLA_LOWRANK = 16
GLA_TAU = 16.0
GLA_CHUNK = 64
DIL_PATTERNS = ((128, 1), (512, 4), (2048, 16))
DIL_GROUPS = len(DIL_PATTERNS)
DIL_HEADS_PER_GROUP = 8
DIL_HEAD_DIM = 64
DIL_GROUP_WIDTH = DIL_HEADS_PER_GROUP * DIL_HEAD_DIM
DIL_BLOCK = 128
REL_BUCKETS = 32
REL_MAX_DIST = 2048
MOE_GROUPS = 4
MOE_EXPERTS = 8
MOE_TOTAL = MOE_GROUPS * MOE_EXPERTS
LN_EPS = 1e-5
DEPTH = 1

LANES = 128
VMEM_LIMIT = 56 * 1024 * 1024
LOG2E = 1.4426950408889634
LN2 = 0.6931471805599453
NEG = -1e30
ROW_TILE = 512
PROJ_CHUNK = 512
FINAL_TILE = 1024
EXPERT_TILE = 512
EXPERT_BLOCK = 128
GLA_STEP_CHUNKS = 16
ATT_STEP_BLOCKS = 8
ROUTER_ROWS = 40
ROUTE_ROWS = 8

NT_DIMS = (((1,), (1,)), ((), ()))
TN_DIMS = (((0,), (0,)), ((), ()))


def _cparams(sem):
    return pltpu.CompilerParams(dimension_semantics=sem, vmem_limit_bytes=VMEM_LIMIT)


def _sigmoid(x):
    return 0.5 * jnp.tanh(0.5 * x) + 0.5


def _silu(x):
    return x * _sigmoid(x)


def _layer_norm(x, g, b):
    mu = jnp.mean(x, axis=-1, keepdims=True)
    xc = x - mu
    var = jnp.mean(xc * xc, axis=-1, keepdims=True)
    return xc * lax.rsqrt(var + LN_EPS) * g + b


def _pack_bf16_pairs(x):
    w = x.shape[1] // 2
    lo = lax.bitcast_convert_type(x[:, :w].astype(BF16).astype(F32), jnp.uint32) >> 16
    hi = lax.bitcast_convert_type(x[:, w:].astype(BF16).astype(F32), jnp.uint32) & jnp.uint32(0xFFFF0000)
    return lax.bitcast_convert_type(lo | hi, jnp.int32)


def _unpack_bf16_pairs(p):
    u = lax.bitcast_convert_type(p, jnp.uint32)
    lo = lax.bitcast_convert_type(u << 16, F32)
    hi = lax.bitcast_convert_type(u & jnp.uint32(0xFFFF0000), F32)
    return jnp.concatenate([lo, hi], axis=1)


def _mods_kernel(ct_ref, w_ref, b_ref, o_ref):
    a = _silu(ct_ref[...])
    w = w_ref[...]
    for b in range(a.shape[1]):
        o_ref[b:b + 1, :] = jnp.sum(a[:, b:b + 1] * w, axis=0, keepdims=True) + b_ref[...]


def _ada_mods(c, w, b):
    bsz, d = c.shape
    n = w.shape[1]
    tn = 1536
    assert n % tn == 0
    return pl.pallas_call(
        _mods_kernel,
        out_shape=jax.ShapeDtypeStruct((bsz, n), F32),
        grid=(n // tn,),
        in_specs=[pl.BlockSpec((d, bsz), lambda j: (0, 0)),
                  pl.BlockSpec((d, tn), lambda j: (0, j)),
                  pl.BlockSpec((1, tn), lambda j: (0, j))],
        out_specs=pl.BlockSpec((bsz, tn), lambda j: (0, j)),
        compiler_params=_cparams(("arbitrary",)),
        name="ada_mods",
    )(c.T, w, b.reshape(1, n))


def _proj_pieces(d_model):
    dk = d_model // 2
    pieces = [("q_gla", dk, "scale_q_gla"), ("k_gla", dk, None), ("v_gla", d_model, None), ("r_gla", d_model, "silu")]
    for name, post in (("q_att", "scale_q_att"), ("k_att", None), ("v_att", None)):
        for g, (_, dilation) in enumerate(DIL_PATTERNS):
            pieces.append((f"{name}{g}", DIL_GROUP_WIDTH, (post, dilation)))
    pieces += [("g_gla", d_model, "sigmoid"), ("g_att", d_model, "sigmoid"), ("lr", LANES, "lowrank")]
    return tuple(pieces)


WT_BLOCK = 512


def _load_in_weight(lr0, wt_hbm, w_ref, stage_ref, sem):
    n_main = wt_hbm.shape[1] - GLA_LOWRANK
    n_blocks = n_main // WT_BLOCK + 1
    src_rows = [j * WT_BLOCK + (GLA_LOWRANK if j * WT_BLOCK >= lr0 else 0) for j in range(n_blocks - 1)] + [lr0]

    def copy(j):
        return pltpu.make_async_copy(wt_hbm.at[0, pl.ds(src_rows[j], WT_BLOCK), :], stage_ref.at[j % 2], sem.at[j % 2])

    copy(0).start()
    for j in range(n_blocks):
        if j + 1 < n_blocks:
            copy(j + 1).start()
        copy(j).wait()
        blk = stage_ref[j % 2]
        if j == n_blocks - 1:
            blk = jnp.where(lax.broadcasted_iota(jnp.int32, blk.shape, 0) < GLA_LOWRANK, blk, 0.0)
        w_ref[j * WT_BLOCK:(j + 1) * WT_BLOCK, :] = blk.astype(BF16)


GLA_HELD = ("lr", "q_gla", "k_gla")


def _gla_operands(hold, wg_ref, bg_ref, qin_ref, kin_ref, qst_ref, kst_ref, dec_ref):
    c = GLA_CHUNK
    tm = hold["q_gla"].shape[0]
    tril = (lax.broadcasted_iota(jnp.int32, (c, c), 0) >= lax.broadcasted_iota(jnp.int32, (c, c), 1)).astype(BF16)
    mid = c // 2 - 1
    lr, wg = hold["lr"][:, 0:GLA_LOWRANK], wg_ref[...]
    lr_hi, wg_hi = lr.astype(BF16), wg.astype(BF16)
    lr_lo, wg_lo = (lr - lr_hi.astype(F32)).astype(BF16), (wg - wg_hi.astype(F32)).astype(BF16)
    gate_in = (jnp.dot(lr_hi, wg_hi, preferred_element_type=F32) + jnp.dot(lr_lo, wg_hi, preferred_element_type=F32)
               + jnp.dot(lr_hi, wg_lo, preferred_element_type=F32)) + bg_ref[...]
    g_all = (jnp.minimum(gate_in, 0.0) - jnp.log(1.0 + jnp.exp(-jnp.abs(gate_in)))) * (1.0 / GLA_TAU)
    g_hi = g_all.astype(BF16)
    g_lo = (g_all - g_hi.astype(F32)).astype(BF16)
    for ci in range(tm // c):
        rows = slice(ci * c, (ci + 1) * c)
        bc = jnp.dot(tril, g_hi[rows], preferred_element_type=F32) + jnp.dot(tril, g_lo[rows], preferred_element_type=F32)
        b_mid = bc[mid:mid + 1, :]
        b_last = bc[c - 1:c, :]
        qf = hold["q_gla"][rows, :]
        kf = hold["k_gla"][rows, :]
        q_in = qf * jnp.exp(bc - b_mid)
        k_in = kf * jnp.exp(b_mid - bc)
        qin_ref[0, rows, :] = q_in.astype(BF16)
        kin_ref[0, rows, :] = k_in.astype(BF16)
        qst_ref[0, rows, :] = (q_in * jnp.exp(b_mid)).astype(BF16)
        kst_ref[0, rows, :] = (k_in * jnp.exp(b_last - b_mid)).astype(BF16)
        dec_ref[0, ci:ci + 1, :] = jnp.exp(b_last)


def _proj_kernel(pieces, head_k, lr0, x_ref, sc_ref, sh_ref, wt_hbm, wg_ref, bg_ref, *refs):
    n_out = len(pieces) - len(GLA_HELD)
    out_refs = dict(zip([p[0] for p in pieces if p[0] not in GLA_HELD], refs[:n_out]))
    gla_out_refs = refs[n_out:n_out + 5]
    stage_ref = refs[n_out + 5]
    hold = dict(zip(GLA_HELD, refs[n_out + 6:n_out + 9]))
    w_ref, wstage_ref, wsem = refs[n_out + 9:]
    tm = x_ref.shape[1]

    @pl.when((pl.program_id(0) == 0) & (pl.program_id(1) == 0))
    def _():
        _load_in_weight(lr0, wt_hbm, w_ref, wstage_ref, wsem)

    u = (x_ref[0] * (1.0 + sc_ref[0]) + sh_ref[0]).astype(BF16)
    offsets, off = {}, 0
    for name, width, _ in pieces:
        offsets[name] = off
        off += width
    by_name = {p[0]: p for p in pieces}
    held = [(by_name[n], 0) for n in GLA_HELD]
    rest = [(p, c0) for p in pieces if p[0] not in GLA_HELD for c0 in range(0, p[1], min(p[1], PROJ_CHUNK))]
    for n, (piece, c0) in enumerate(held + rest):
        if n == len(held):
            _gla_operands(hold, wg_ref, bg_ref, *gla_out_refs)
        name, width, post = piece
        o_ref = hold[name] if name in GLA_HELD else out_refs[name]
        off = offsets[name]
        chunk = min(width, PROJ_CHUNK)
        acc = lax.dot_general(u, w_ref[off + c0:off + c0 + chunk, :], NT_DIMS, preferred_element_type=F32)
        if post == "silu":
            acc = _silu(acc)
        elif post == "sigmoid":
            acc = _sigmoid(acc)
        elif post == "scale_q_gla":
            acc = acc * (head_k ** -0.5)
        if name in GLA_HELD:
            o_ref[...] = acc
        elif isinstance(post, tuple):
            scale, dilation = post
            if scale is not None:
                acc = acc * (DIL_HEAD_DIM ** -0.5 * LOG2E)
            if dilation == 1:
                o_ref[0, 0] = acc.astype(o_ref.dtype)
            else:
                for t in range(width // LANES):
                    stage_ref[t] = acc[:, t * LANES:(t + 1) * LANES]
                for r in range(dilation):
                    for t in range(width // LANES):
                        o_ref[0, r, :, t * LANES:(t + 1) * LANES] = stage_ref[
                            t, pl.ds(r, tm // dilation, stride=dilation), :].astype(o_ref.dtype)
        else:
            o_ref[0, :, c0:c0 + chunk] = acc.astype(o_ref.dtype)


def _in_projection(x, sc1, sh1, w_in, lr0, w_gate, b_gate):
    bsz, s, d = x.shape
    pieces = _proj_pieces(d)
    w_t = jnp.swapaxes(w_in, 1, 2)
    n_main = w_t.shape[1] - GLA_LOWRANK
    assert lr0 % WT_BLOCK == 0 and n_main % WT_BLOCK == 0
    w_rows = n_main + WT_BLOCK
    assert sum(p[1] for p in pieces) <= w_rows
    tm = min(ROW_TILE, s)
    assert s % tm == 0 and tm % (8 * GLA_CHUNK) == 0
    dk = d // 2
    head_k = dk // GLA_HEADS
    out_shape, out_specs = [], []
    for name, width, post in pieces:
        if name in GLA_HELD:
            continue
        if isinstance(post, tuple):
            dil = post[1]
            assert tm % (dil * 16) == 0
            out_shape.append(jax.ShapeDtypeStruct((bsz, dil, s // dil, width), BF16))
            out_specs.append(pl.BlockSpec((1, dil, tm // dil, width), lambda b, i: (b, 0, i, 0)))
        else:
            out_shape.append(jax.ShapeDtypeStruct((bsz, s, width), BF16))
            out_specs.append(pl.BlockSpec((1, tm, width), lambda b, i: (b, i, 0)))
    row = lambda w: pl.BlockSpec((1, tm, w), lambda b, i: (b, i, 0))
    gla_names = ("q_in", "k_in", "q_st", "k_st", "dec")
    out_shape += [jax.ShapeDtypeStruct((bsz, s, dk), BF16)] * 4 + [jax.ShapeDtypeStruct((bsz, s // GLA_CHUNK, dk), F32)]
    out_specs += [row(dk)] * 4 + [pl.BlockSpec((1, tm // GLA_CHUNK, dk), lambda b, i: (b, i, 0))]
    bg = b_gate.reshape(1, dk)
    full = lambda a: pl.BlockSpec(a.shape, lambda b, i: (0,) * a.ndim)
    outs = pl.pallas_call(
        functools.partial(_proj_kernel, pieces, head_k, lr0),
        out_shape=out_shape,
        grid=(bsz, s // tm),
        in_specs=[row(d),
                  pl.BlockSpec((1, 1, d), lambda b, i: (b, 0, 0)),
                  pl.BlockSpec((1, 1, d), lambda b, i: (b, 0, 0)),
                  pl.BlockSpec(memory_space=pl.ANY),
                  full(w_gate), full(bg)],
        out_specs=out_specs,
        scratch_shapes=[pltpu.VMEM((DIL_GROUP_WIDTH // LANES, tm, LANES), F32),
                        pltpu.VMEM((tm, LANES), F32), pltpu.VMEM((tm, dk), F32), pltpu.VMEM((tm, dk), F32),
                        pltpu.VMEM((w_rows, d), BF16), pltpu.VMEM((2, WT_BLOCK, d), F32),
                        pltpu.SemaphoreType.DMA((2,))],
        compiler_params=_cparams(("arbitrary", "arbitrary")),
        name="in_projection",
    )(x, sc1, sh1, w_t, w_gate, bg)
    return dict(zip([p[0] for p in pieces if p[0] not in GLA_HELD] + list(gla_names), outs))


def _gla_kernel(n_chunks, head_k, head_v, qin_ref, kin_ref, qst_ref, kst_ref, dec_ref, v_ref, r_ref, ng_ref, o_ref,
                state_ref):
    @pl.when(pl.program_id(1) == 0)
    def _():
        state_ref[...] = jnp.zeros_like(state_ref)

    c = GLA_CHUNK
    causal = lax.broadcasted_iota(jnp.int32, (c, c), 0) >= lax.broadcasted_iota(jnp.int32, (c, c), 1)
    for ci in range(n_chunks):
        rows = slice(ci * c, (ci + 1) * c)
        for h in range(GLA_HEADS):
            ks = slice(h * head_k, (h + 1) * head_k)
            vs = slice(h * head_v, (h + 1) * head_v)
            vh = v_ref[0, rows, vs]
            att = lax.dot_general(qin_ref[0, rows, ks], kin_ref[0, rows, ks], NT_DIMS, preferred_element_type=F32)
            att = jnp.where(causal, att, 0.0).astype(BF16)
            st = state_ref[h]
            o = jnp.dot(att, vh, preferred_element_type=F32)
            o = o + lax.dot_general(qst_ref[0, rows, ks], st.astype(BF16), NT_DIMS, preferred_element_type=F32)
            kv_t = lax.dot_general(vh, kst_ref[0, rows, ks], TN_DIMS, preferred_element_type=F32)
            state_ref[h] = st * dec_ref[0, ci:ci + 1, ks] + kv_t
            ms = jnp.mean(o * o, axis=-1, keepdims=True)
            o = o * lax.rsqrt(ms + LN_EPS) * ng_ref[:, vs] * r_ref[0, rows, vs].astype(F32)
            o_ref[0, rows, vs] = o.astype(o_ref.dtype)


def _gla(q_in, k_in, q_st, k_st, dec, v, r_silu, norm_g):
    bsz, s, dk = q_in.shape
    dv = v.shape[-1]
    head_k, head_v = dk // GLA_HEADS, dv // GLA_HEADS
    n_chunks = min(GLA_STEP_CHUNKS, s // GLA_CHUNK)
    ct = GLA_CHUNK * n_chunks
    assert s % ct == 0
    row_spec = lambda w: pl.BlockSpec((1, ct, w), lambda b, i: (b, i, 0))
    full = lambda a: pl.BlockSpec(a.shape, lambda b, i: (0,) * a.ndim)
    ng = norm_g.reshape(1, dv)
    return pl.pallas_call(
        functools.partial(_gla_kernel, n_chunks, head_k, head_v),
        out_shape=jax.ShapeDtypeStruct((bsz, s, dv), BF16),
        grid=(bsz, s // ct),
        in_specs=[row_spec(dk)] * 4 + [pl.BlockSpec((1, n_chunks, dk), lambda b, i: (b, i, 0)),
                                       row_spec(dv), row_spec(dv), full(ng)],
        out_specs=row_spec(dv),
        scratch_shapes=[pltpu.VMEM((GLA_HEADS, head_v, head_k), F32)],
        compiler_params=_cparams(("parallel", "arbitrary")),
        name="gla",
    )(q_in, k_in, q_st, k_st, dec, v, r_silu, ng)


def _t5_bucket_np(dist):
    exact = REL_BUCKETS // 2
    d = np.maximum(dist, 1).astype(np.float32)
    large = exact + (np.log(d / np.float32(exact)) / np.float32(math.log(REL_MAX_DIST / exact))
                     * np.float32(REL_BUCKETS - exact)).astype(np.int32)
    large = np.minimum(large, REL_BUCKETS - 1)
    return np.where(dist < exact, dist, large).astype(np.int32)


def _band_tables(window, dilation):
    qi = np.arange(DIL_BLOCK)[:, None]
    kj = np.arange(2 * DIL_BLOCK)[None, :]
    m = qi + DIL_BLOCK - kj
    n_steps = window // dilation
    band = (m >= 0) & (m <= n_steps)
    bucket = _t5_bucket_np(np.clip(m, 0, n_steps) * dilation)
    return np.where(band, bucket, -1).astype(np.int32)


def _attn_kernel(nq, table_ref, bucket_ref, q_ref, kp_ref, kc_ref, vp_ref, vc_ref, o_ref, lse_ref,
                 bias_ref, p_ref):
    i = pl.program_id(1)
    blk = DIL_BLOCK
    hpg = DIL_HEADS_PER_GROUP
    n_pairs = hpg // 2

    @pl.when((pl.program_id(0) == 0) & (i == 0))
    def _():
        bucket = bucket_ref[...]
        for h in range(hpg):
            acc = jnp.full(bucket.shape, NEG, F32)
            for bkt in range(REL_BUCKETS):
                acc = jnp.where(bucket == bkt, table_ref[bkt, h] * LOG2E, acc)
            bias_ref[h * blk:(h + 1) * blk, :] = acc

    lane = lax.broadcasted_iota(jnp.int32, (blk, LANES), 1)
    low = lane < DIL_HEAD_DIM
    ones_rhs = jnp.ones((2 * blk, LANES), BF16)

    def windows(ref_p, ref_c, sq, qb, cols):
        if qb == 0:
            return jnp.concatenate([ref_p[sq, :, cols], ref_c[sq, 0:blk, cols]], axis=0)
        return ref_c[sq, (qb - 1) * blk:(qb + 1) * blk, cols]

    key_lane = lax.broadcasted_iota(jnp.int32, (1, 2 * blk), 1)
    no_prev = jnp.where((key_lane < blk) & (i == 0), NEG, 0.0)
    items = [(sq, qb, hp) for sq in range(q_ref.shape[0]) for qb in range(nq) for hp in range(n_pairs)]

    mxs = []
    for n, (sq, qb, hp) in enumerate(items):
        rows = slice(qb * blk, (qb + 1) * blk)
        cols = slice(hp * LANES, (hp + 1) * LANES)
        qp = q_ref[sq, rows, cols]
        zero = jnp.zeros_like(qp)
        qq = jnp.concatenate([jnp.where(low, qp, zero), jnp.where(low, zero, qp)], axis=0)
        keys = windows(kp_ref, kc_ref, sq, qb, cols)
        s = lax.dot_general(qq, keys, NT_DIMS, preferred_element_type=F32) + bias_ref[2 * hp * blk:(2 * hp + 2) * blk, :]
        if qb == 0:
            s = s + no_prev
        mx = jnp.max(s, axis=-1, keepdims=True)
        p_ref[n * 2 * blk:(n + 1) * 2 * blk, :] = jnp.exp2(s - mx).astype(BF16)
        mxs.append(mx)

    for n, (sq, qb, hp) in enumerate(items):
        rows = slice(qb * blk, (qb + 1) * blk)
        cols = slice(hp * LANES, (hp + 1) * LANES)
        vals = windows(vp_ref, vc_ref, sq, qb, cols)
        rhs = jnp.concatenate([vals, ones_rhs], axis=1)
        res = jnp.dot(p_ref[n * 2 * blk:(n + 1) * 2 * blk, :], rhs, preferred_element_type=F32)
        num = jnp.where(low, res[0:blk, 0:LANES], res[blk:2 * blk, 0:LANES])
        den = jnp.where(low, res[0:blk, LANES:], res[blk:2 * blk, LANES:])
        mx = jnp.where(low, mxs[n][0:blk], mxs[n][blk:2 * blk])
        o_ref[sq, rows, cols] = (num / den).astype(o_ref.dtype)
        lse_ref[sq, rows, cols] = (mx + jnp.log2(den)) * LN2


def _dilated_group_attention(q, k, v, table, window, dilation):
    bb, l, w = q.shape
    nq = min(ATT_STEP_BLOCKS, l // DIL_BLOCK)
    nsq = ATT_STEP_BLOCKS // nq
    assert l % (nq * DIL_BLOCK) == 0 and bb % nsq == 0
    steps = l // (nq * DIL_BLOCK)
    bucket = jnp.asarray(_band_tables(window, dilation))
    cur = pl.BlockSpec((nsq, nq * DIL_BLOCK, w), lambda b, i: (b, i, 0))
    prev = pl.BlockSpec((nsq, DIL_BLOCK, w), lambda b, i: (b, jnp.maximum(nq * i - 1, 0), 0))
    rows_all = nsq * nq * DIL_HEADS_PER_GROUP * DIL_BLOCK
    return pl.pallas_call(
        functools.partial(_attn_kernel, nq),
        out_shape=[jax.ShapeDtypeStruct((bb, l, w), BF16), jax.ShapeDtypeStruct((bb, l, w), F32)],
        grid=(bb // nsq, steps),
        in_specs=[pl.BlockSpec(memory_space=pltpu.SMEM),
                  pl.BlockSpec(bucket.shape, lambda b, i: (0, 0)),
                  cur, prev, cur, prev, cur],
        out_specs=[cur, cur],
        scratch_shapes=[pltpu.VMEM((DIL_HEADS_PER_GROUP * DIL_BLOCK, 2 * DIL_BLOCK), F32),
                        pltpu.VMEM((rows_all, 2 * DIL_BLOCK), BF16)],
        compiler_params=_cparams(("arbitrary", "arbitrary")),
        name=f"dilated_attn_d{dilation}",
    )(table, bucket, q, k, k, v, v)


def _merge_kernel(alpha, dilations, ygla_ref, o0_ref, o1_ref, o2_ref, l0_ref, l1_ref, l2_ref, gg_ref, ga_ref, x_ref,
                  g1_ref, sc2_ref, sh2_ref, ln_g_ref, ln_b_ref, wpg_ref, wpa_ref, wout_ref, wr_ref, br_ref, utri_ref,
                  x1_ref, u2_ref, route_ref, ew_ref, cnt_ref, stage_ref, carry_ref):
    tm = x_ref.shape[1]

    @pl.when((pl.program_id(0) == 0) & (pl.program_id(1) == 0))
    def _():
        carry_ref[...] = jnp.zeros_like(carry_ref)

    n_lt = DIL_GROUP_WIDTH // LANES
    group_refs = tuple(zip((l0_ref, l1_ref, l2_ref), (o0_ref, o1_ref, o2_ref), dilations))
    for gi, (l_ref, o_ref, dil) in enumerate(group_refs):
        if dil > 1:
            for slot, ref in ((2 * gi, l_ref), (2 * gi + 1, o_ref)):
                for r in range(dil):
                    for t in range(n_lt):
                        stage_ref[slot, t, pl.ds(r, tm // dil, stride=dil), :] = ref[
                            0, r, :, t * LANES:(t + 1) * LANES].astype(F32)

    def natural(ref, dil, slot):
        if dil == 1:
            return ref[0, 0].astype(F32)
        return jnp.concatenate([stage_ref[slot, t] for t in range(n_lt)], axis=1)

    lses = [natural(l_ref, dil, 2 * gi) for gi, (l_ref, _, dil) in enumerate(group_refs)]
    outs = [natural(o_ref, dil, 2 * gi + 1) for gi, (_, o_ref, dil) in enumerate(group_refs)]
    lm = jnp.maximum(jnp.maximum(lses[0], lses[1]), lses[2])
    es = [jnp.exp(l - lm) for l in lses]
    y_att = (es[0] * outs[0] + es[1] * outs[1] + es[2] * outs[2]) / (es[0] + es[1] + es[2])

    p_gla = jnp.dot(ygla_ref[0], wpg_ref[...], preferred_element_type=F32)
    p_att = jnp.dot(y_att.astype(BF16), wpa_ref[...], preferred_element_type=F32)
    merged = gg_ref[0].astype(F32) * p_gla + ga_ref[0].astype(F32) * p_att
    y = jnp.dot(merged.astype(BF16), wout_ref[...], preferred_element_type=F32)
    x1 = _layer_norm(alpha * x_ref[0] + g1_ref[0] * y, ln_g_ref[...], ln_b_ref[...])
    x1_ref[0] = x1
    u2 = x1 * (1.0 + sc2_ref[0]) + sh2_ref[0]
    u2_ref[0] = _pack_bf16_pairs(u2)

    logits = jnp.dot(u2.astype(BF16), wr_ref[...].astype(BF16), preferred_element_type=F32) + br_ref[...]
    lt = jnp.transpose(logits)[0:ROUTER_ROWS, :]
    rowi = lax.broadcasted_iota(jnp.int32, lt.shape, 0)
    big = jnp.int32(LANES)
    lg = jnp.where(rowi < MOE_GROUPS, lt, NEG)
    gmax = jnp.max(lg, axis=0, keepdims=True)
    gidx = jnp.min(jnp.where(lg == gmax, rowi, big), axis=0, keepdims=True)
    gval = 1.0 / jnp.sum(jnp.exp(lg - gmax), axis=0, keepdims=True)
    first = MOE_GROUPS + gidx * MOE_EXPERTS
    le = jnp.where((rowi >= first) & (rowi < first + MOE_EXPERTS), lt, NEG)
    m1 = jnp.max(le, axis=0, keepdims=True)
    i1 = jnp.min(jnp.where(le == m1, rowi, big), axis=0, keepdims=True)
    le2 = jnp.where(rowi == i1, NEG, le)
    m2 = jnp.max(le2, axis=0, keepdims=True)
    i2 = jnp.min(jnp.where(le2 == m2, rowi, big), axis=0, keepdims=True)
    t = jnp.exp(m2 - m1)
    w1 = 1.0 / (1.0 + t)
    w2 = t * w1

    hit1, hit2 = rowi == i1, rowi == i2
    onehot = jnp.where(hit1 | hit2, 1.0, 0.0)
    earlier = jnp.dot(onehot.astype(BF16), utri_ref[...], preferred_element_type=F32) + carry_ref[...]
    rank1 = jnp.sum(jnp.where(hit1, earlier, 0.0), axis=0, keepdims=True).astype(jnp.int32)
    rank2 = jnp.sum(jnp.where(hit2, earlier, 0.0), axis=0, keepdims=True).astype(jnp.int32)
    carry = carry_ref[...] + jnp.sum(onehot, axis=1, keepdims=True)
    carry_ref[...] = carry
    cnt_ref[...] = jnp.broadcast_to(carry, cnt_ref.shape).astype(jnp.int32)
    r8 = lax.broadcasted_iota(jnp.int32, (ROUTE_ROWS, tm), 0)
    route_ref[0] = jnp.where(r8 == 0, i1 - MOE_GROUPS, jnp.where(r8 == 1, i2 - MOE_GROUPS,
                             jnp.where(r8 == 2, rank1, jnp.where(r8 == 3, rank2, 0))))
    r128 = lax.broadcasted_iota(jnp.int32, (LANES, tm), 0)
    ew_ref[0] = jnp.transpose(jnp.where(r128 == 0, gval * w1, jnp.where(r128 == 1, gval * w2, 0.0)))


def _merge(alpha, y_gla, o_groups, lse_groups, g_gla, g_att, x, g1, sc2, sh2, ln_g, ln_b, wpg, wpa, wout, wr, br):
    bsz, s, d = x.shape
    tm = min(ROW_TILE, s)
    assert s % tm == 0
    dilations = tuple(dil for _, dil in DIL_PATTERNS)
    row = lambda w: pl.BlockSpec((1, tm, w), lambda b, i: (b, i, 0))
    sub = lambda dil: pl.BlockSpec((1, dil, tm // dil, DIL_GROUP_WIDTH), lambda b, i: (b, 0, i, 0))
    per_b = pl.BlockSpec((1, 1, d), lambda b, i: (b, 0, 0))
    full = lambda a: pl.BlockSpec(a.shape, lambda b, i: (0,) * a.ndim)
    ln_g2, ln_b2 = ln_g.reshape(1, d), ln_b.reshape(1, d)
    utri = jnp.asarray(np.triu(np.ones((tm, tm), np.float32), 1), BF16)
    return pl.pallas_call(
        functools.partial(_merge_kernel, alpha, dilations),
        out_shape=[jax.ShapeDtypeStruct((bsz, s, d), F32), jax.ShapeDtypeStruct((bsz, s, d // 2), jnp.int32),
                   jax.ShapeDtypeStruct((bsz, ROUTE_ROWS, s), jnp.int32), jax.ShapeDtypeStruct((bsz, s, LANES), F32),
                   jax.ShapeDtypeStruct((ROUTER_ROWS, LANES), jnp.int32)],
        grid=(bsz, s // tm),
        in_specs=[row(y_gla.shape[-1])] + [sub(dil) for dil in dilations] * 2
                 + [row(d), row(d), row(d), per_b, per_b, per_b, full(ln_g2), full(ln_b2),
                    full(wpg), full(wpa), full(wout), full(wr), full(br), full(utri)],
        out_specs=[row(d), row(d // 2), pl.BlockSpec((1, ROUTE_ROWS, tm), lambda b, i: (b, 0, i)), row(LANES),
                   pl.BlockSpec((ROUTER_ROWS, LANES), lambda b, i: (0, 0))],
        scratch_shapes=[pltpu.VMEM((2 * DIL_GROUPS, DIL_GROUP_WIDTH // LANES, tm, LANES), F32),
                        pltpu.VMEM((ROUTER_ROWS, 1), F32)],
        compiler_params=_cparams(("arbitrary", "arbitrary")),
        name="merge_ln1_router",
    )(y_gla, *o_groups, *lse_groups, g_gla, g_att, x, g1, sc2, sh2, ln_g2, ln_b2, wpg, wpa, wout, wr, br, utri)


def _expert_kernel(run_ref, valid_ref, rexp_ref, used_ref, x_ref, wg_hbm, wu_hbm, wd_hbm, o_ref,
                   wg_f, wu_f, wd_f, wg_s, wu_s, wd_s, sem):
    t = pl.program_id(0)
    n_tiles_used, n_runs = used_ref[0], used_ref[1]
    run = run_ref[t]
    active = t < n_tiles_used
    first_of_run = (t == 0) | (run_ref[jnp.maximum(t - 1, 0)] != run)

    def weight_copies(r):
        e, slot = rexp_ref[r], r % 2
        return [pltpu.make_async_copy(hbm.at[e], buf.at[slot], sem.at[slot, j])
                for j, (hbm, buf) in enumerate(((wg_hbm, wg_f), (wu_hbm, wu_f), (wd_hbm, wd_f)))]

    @pl.when(active & (t == 0))
    def _():
        for cp in weight_copies(0):
            cp.start()

    @pl.when(active & first_of_run)
    def _():
        @pl.when(run + 1 < n_runs)
        def _():
            for cp in weight_copies(run + 1):
                cp.start()

        for cp in weight_copies(run):
            cp.wait()
        slot = run % 2
        wg_s[...] = wg_f[slot].astype(BF16)
        wu_s[...] = wu_f[slot].astype(BF16)
        wd_s[...] = wd_f[slot].astype(BF16)

    n_valid = jnp.where(active, valid_ref[t], 0)

    def ffn(rows):
        xt = _unpack_bf16_pairs(x_ref[rows, :]).astype(BF16)
        hg = jnp.dot(xt, wg_s[...], preferred_element_type=F32)
        hu = jnp.dot(xt, wu_s[...], preferred_element_type=F32)
        h = (_silu(hg) * hu).astype(BF16)
        o_ref[rows, :] = _pack_bf16_pairs(jnp.dot(h, wd_s[...], preferred_element_type=F32))

    def zero(rows):
        o_ref[rows, :] = jnp.zeros((rows.stop - rows.start, o_ref.shape[1]), o_ref.dtype)

    tm = x_ref.shape[0]
    n_blocks = tm // EXPERT_BLOCK
    for k in range(n_blocks + 1):
        lo, hi = (k - 1) * EXPERT_BLOCK, k * EXPERT_BLOCK

        @pl.when((n_valid > lo) & (n_valid <= hi) if 0 < k < n_blocks else (n_valid > lo if k else n_valid <= 0))
        def _():
            if k:
                ffn(slice(0, hi))
            if k < n_blocks:
                zero(slice(hi, tm))


def _expert_ffn(tile_run, tile_valid, run_expert, used, xg, w_gate, w_up, w_down):
    p = xg.shape[0]
    ne, d, ff = w_gate.shape
    tm = EXPERT_TILE
    n_tiles = p // tm
    hbm = pl.BlockSpec(memory_space=pl.ANY)

    def tile(t, run, valid, rexp, used):
        return jnp.where(t < used[0], t, n_tiles - 1), 0

    grid_spec = pltpu.PrefetchScalarGridSpec(
        num_scalar_prefetch=4,
        grid=(n_tiles,),
        in_specs=[pl.BlockSpec((tm, d // 2), tile), hbm, hbm, hbm],
        out_specs=pl.BlockSpec((tm, d // 2), tile),
        scratch_shapes=[pltpu.VMEM((2, d, ff), F32), pltpu.VMEM((2, d, ff), F32), pltpu.VMEM((2, ff, d), F32),
                        pltpu.VMEM((d, ff), BF16), pltpu.VMEM((d, ff), BF16), pltpu.VMEM((ff, d), BF16),
                        pltpu.SemaphoreType.DMA((2, 3))],
    )
    return pl.pallas_call(
        _expert_kernel,
        out_shape=jax.ShapeDtypeStruct((p, d // 2), jnp.int32),
        grid_spec=grid_spec,
        compiler_params=_cparams(("arbitrary",)),
        name="expert_ffn",
    )(tile_run, tile_valid, run_expert, used, xg, w_gate, w_up, w_down)


def _final_kernel(alpha, x1_ref, ya_ref, yb_ref, ew_ref, g2_ref, ln_g_ref, ln_b_ref, o_ref):
    ew = ew_ref[0]
    y = ew[:, 0:1] * _unpack_bf16_pairs(ya_ref[0]) + ew[:, 1:2] * _unpack_bf16_pairs(yb_ref[0])
    o_ref[0] = _layer_norm(alpha * x1_ref[0] + g2_ref[0] * y, ln_g_ref[...], ln_b_ref[...])


def _final(alpha, x1, ya, yb, ew, g2, ln_g, ln_b):
    bsz, s, d = x1.shape
    tm = min(FINAL_TILE, s)
    row = lambda w: pl.BlockSpec((1, tm, w), lambda b, i: (b, i, 0))
    full = lambda a: pl.BlockSpec(a.shape, lambda b, i: (0,) * a.ndim)
    ln_g2, ln_b2 = ln_g.reshape(1, d), ln_b.reshape(1, d)
    return pl.pallas_call(
        functools.partial(_final_kernel, alpha),
        out_shape=jax.ShapeDtypeStruct((bsz, s, d), F32),
        grid=(bsz, s // tm),
        in_specs=[row(d), row(d // 2), row(d // 2), row(LANES), pl.BlockSpec((1, 1, d), lambda b, i: (b, 0, 0)),
                  full(ln_g2), full(ln_b2)],
        out_specs=row(d),
        compiler_params=_cparams(("parallel", "arbitrary")),
        name="combine_ln2",
    )(x1, ya, yb, ew, g2, ln_g2, ln_b2)


SC_CORES = 2
SC_SUBCORES = 16
SC_CHUNK = 64


def _sc_mesh():
    return plsc.VectorSubcoreMesh(core_axis_name="c", subcore_axis_name="s")


def _sc_scatter_rows(rows, dest0, dest1, n_rows):
    n, w = rows.shape
    n_workers = SC_CORES * SC_SUBCORES
    assert n % (n_workers * SC_CHUNK) == 0
    n_chunks = n // (n_workers * SC_CHUNK)
    d0 = dest0.reshape(n // SC_CHUNK, 1, SC_CHUNK)
    d1 = dest1.reshape(n // SC_CHUNK, 1, SC_CHUNK)

    @functools.partial(
        pl.kernel, mesh=_sc_mesh(), out_type=jax.ShapeDtypeStruct((n_rows, w), rows.dtype),
        scratch_types=[pltpu.VMEM((n_chunks, 1, SC_CHUNK), jnp.int32), pltpu.VMEM((n_chunks, 1, SC_CHUNK), jnp.int32),
                       pltpu.VMEM((2, SC_CHUNK, w), rows.dtype),
                       pltpu.SemaphoreType.DMA((2,)), pltpu.SemaphoreType.DMA((2, 2))])
    def scatter_kernel(rows_hbm, d0_hbm, d1_hbm, out_hbm, i0_v, i1_v, rows_v, read_sem, scat_sem):
        wid = lax.axis_index("s") * SC_CORES + lax.axis_index("c")
        first = wid * n_chunks
        pltpu.sync_copy(d0_hbm.at[pl.ds(first, n_chunks)], i0_v)
        pltpu.sync_copy(d1_hbm.at[pl.ds(first, n_chunks)], i1_v)

        def read(j):
            return pltpu.make_async_copy(rows_hbm.at[pl.ds((first + j) * SC_CHUNK, SC_CHUNK)], rows_v.at[j % 2],
                                         read_sem.at[j % 2])

        def scatters(j):
            return [pltpu.make_async_copy(rows_v.at[j % 2], out_hbm.at[idx.at[j].at[0]], scat_sem.at[j % 2, k])
                    for k, idx in enumerate((i0_v, i1_v))]

        read(0).start()
        for j in range(n_chunks):
            read(j).wait()
            if j + 1 < n_chunks:
                if j >= 1:
                    for cp in scatters(j - 1):
                        cp.wait()
                read(j + 1).start()
            for cp in scatters(j):
                cp.start()
        for j in range(max(n_chunks - 2, 0), n_chunks):
            for cp in scatters(j):
                cp.wait()

    return scatter_kernel(rows, d0, d1)


def _sc_gather_rows(table, dest0, dest1):
    n = dest0.shape[0]
    w = table.shape[1]
    n_workers = SC_CORES * SC_SUBCORES
    assert n % (n_workers * SC_CHUNK) == 0
    n_chunks = n // (n_workers * SC_CHUNK)
    d0 = dest0.reshape(n // SC_CHUNK, 1, SC_CHUNK)
    d1 = dest1.reshape(n // SC_CHUNK, 1, SC_CHUNK)
    out = jax.ShapeDtypeStruct((n, w), table.dtype)

    @functools.partial(
        pl.kernel, mesh=_sc_mesh(), out_type=(out, out),
        scratch_types=[pltpu.VMEM((n_chunks, 1, SC_CHUNK), jnp.int32), pltpu.VMEM((n_chunks, 1, SC_CHUNK), jnp.int32),
                       pltpu.VMEM((2, SC_CHUNK, w), table.dtype),
                       pltpu.SemaphoreType.DMA((2,)), pltpu.SemaphoreType.DMA((2,))])
    def gather_kernel(table_hbm, d0_hbm, d1_hbm, a_hbm, b_hbm, i0_v, i1_v, rows_v, gather_sem, write_sem):
        wid = lax.axis_index("s") * SC_CORES + lax.axis_index("c")
        first = wid * n_chunks
        pltpu.sync_copy(d0_hbm.at[pl.ds(first, n_chunks)], i0_v)
        pltpu.sync_copy(d1_hbm.at[pl.ds(first, n_chunks)], i1_v)
        n_items = 2 * n_chunks

        def gather(m):
            idx = (i0_v, i1_v)[m % 2]
            return pltpu.make_async_copy(table_hbm.at[idx.at[m // 2].at[0]], rows_v.at[m % 2], gather_sem.at[m % 2])

        def write(m):
            o_hbm = (a_hbm, b_hbm)[m % 2]
            return pltpu.make_async_copy(rows_v.at[m % 2], o_hbm.at[pl.ds((first + m // 2) * SC_CHUNK, SC_CHUNK)],
                                         write_sem.at[m % 2])

        gather(0).start()
        for m in range(n_items):
            gather(m).wait()
            if m + 1 < n_items:
                if m >= 1:
                    write(m - 1).wait()
                gather(m + 1).start()
            write(m).start()
        for m in range(max(n_items - 2, 0), n_items):
            write(m).wait()

    return gather_kernel(table, d0, d1)


def _dispatch_plan(route, counts):
    tm = EXPERT_TILE
    e0, e1, r0, r1 = (route[:, j, :].reshape(-1) for j in range(4))
    experts = jnp.arange(MOE_TOTAL, dtype=jnp.int32)
    tiles_per = (counts + tm - 1) // tm
    tile_end = jnp.cumsum(tiles_per)
    pad_start = ((tile_end - tiles_per) * tm).astype(jnp.int32)

    def lookup(e):
        return jnp.sum(jnp.where(e[None, :] == experts[:, None], pad_start[:, None], 0), axis=0)

    dest0, dest1 = lookup(e0) + r0, lookup(e1) + r1
    n_tiles = (2 * e0.size + MOE_TOTAL * tm) // tm
    tile_expert = jnp.minimum(jnp.sum(tile_end[None, :] <= jnp.arange(n_tiles)[:, None], axis=1), MOE_TOTAL - 1)
    nonempty = counts > 0
    run_of_expert = jnp.cumsum(nonempty.astype(jnp.int32)) - 1
    run_expert = jnp.sum(jnp.where(nonempty[None, :] & (run_of_expert[None, :] == experts[:, None]),
                                   experts[None, :], 0), axis=1).astype(jnp.int32)
    of_tile = tile_expert[:, None] == experts[None, :]
    tile_run = jnp.sum(jnp.where(of_tile, run_of_expert[None, :], 0), axis=1).astype(jnp.int32)
    rows_left = (counts + pad_start)[None, :] - jnp.arange(n_tiles)[:, None] * tm
    tile_valid = jnp.clip(jnp.sum(jnp.where(of_tile, rows_left, 0), axis=1), 0, tm).astype(jnp.int32)
    used = jnp.stack([tile_end[-1], jnp.sum(nonempty)]).astype(jnp.int32)
    return dest0, dest1, tile_run, tile_valid, run_expert, used, n_tiles * tm


def _layer(x, c, rel_bias, w_ada, b_ada, w_in, w_gla_gate, b_gla_gate, gla_norm, w_proj_gla, w_proj_attn, w_out,
           ln1_g, ln1_b, w_rg, b_rg, w_re, b_re, w_eg, w_eu, w_ed, ln2_g, ln2_b):
    bsz, s, d = x.shape
    alpha = (2.0 * DEPTH) ** 0.25
    mods = _ada_mods(c, w_ada, b_ada)
    sh1, sc1, g1, sh2, sc2, g2 = [m.reshape(bsz, 1, d) for m in jnp.split(mods, N_MOD, axis=-1)]

    lr0 = d // 2 * 2 + 2 * d
    z = _in_projection(x, sc1, sh1, w_in, lr0, w_gla_gate, b_gla_gate)

    y_gla = _gla(z["q_in"], z["k_in"], z["q_st"], z["k_st"], z["dec"], z["v_gla"], z["r_gla"], gla_norm)

    o_groups, lse_groups = [], []
    for g, (window, dilation) in enumerate(DIL_PATTERNS):
        l = s // dilation
        qg, kg, vg = (z[f"{n}{g}"].reshape(bsz * dilation, l, DIL_GROUP_WIDTH) for n in ("q_att", "k_att", "v_att"))
        table = rel_bias[:, g * DIL_HEADS_PER_GROUP:(g + 1) * DIL_HEADS_PER_GROUP]
        o, lse = _dilated_group_attention(qg, kg, vg, table, window, dilation)
        o_groups.append(o.reshape(bsz, dilation, l, DIL_GROUP_WIDTH))
        lse_groups.append(lse.reshape(bsz, dilation, l, DIL_GROUP_WIDTH))

    wr = jnp.concatenate([w_rg, w_re, jnp.zeros((d, LANES - MOE_GROUPS - MOE_TOTAL), F32)], axis=1)
    br = jnp.concatenate([b_rg, b_re, jnp.zeros((LANES - MOE_GROUPS - MOE_TOTAL,), F32)]).reshape(1, LANES)
    x1, u2, route, ew, cnt = _merge(alpha, y_gla, o_groups, lse_groups, z["g_gla"], z["g_att"], x, g1, sc2, sh2,
                                    ln1_g, ln1_b, w_proj_gla.astype(BF16), w_proj_attn.astype(BF16),
                                    w_out.astype(BF16), wr, br)

    n = bsz * s
    counts = cnt[MOE_GROUPS:MOE_GROUPS + MOE_TOTAL, 0]
    dest0, dest1, tile_run, tile_valid, run_expert, used, n_rows = _dispatch_plan(route, counts)
    xg = _sc_scatter_rows(u2.reshape(n, d // 2), dest0, dest1, n_rows)
    ff = w_eg.shape[-1]
    yo = _expert_ffn(tile_run, tile_valid, run_expert, used, xg, w_eg.reshape(MOE_TOTAL, d, ff),
                     w_eu.reshape(MOE_TOTAL, d, ff), w_ed.reshape(MOE_TOTAL, ff, d))
    ya, yb = (y.reshape(bsz, s, d // 2) for y in _sc_gather_rows(yo, dest0, dest1))
    return _final(alpha, x1, ya, yb, ew, g2, ln2_g, ln2_b)


def kernel(x, c, rel_bias, w_ada, b_ada, w_in, w_gla_gate, b_gla_gate, gla_norm, w_proj_gla, w_proj_attn, w_out,
           ln1_g, ln1_b, w_router_group, b_router_group, w_router_expert, b_router_expert, w_exp_gate, w_exp_up,
           w_exp_down, ln2_g, ln2_b):
    assert w_ada.shape[0] == DEPTH
    return _layer(x, c, rel_bias, w_ada[0], b_ada[0], w_in[0:1], w_gla_gate[0], b_gla_gate[0], gla_norm[0],
                  w_proj_gla[0], w_proj_attn[0], w_out[0], ln1_g[0], ln1_b[0], w_router_group[0],
                  b_router_group[0], w_router_expert[0], b_router_expert[0], w_exp_gate[0], w_exp_up[0],
                  w_exp_down[0], ln2_g[0], ln2_b[0])
```

```python
import functools
import math

import numpy as np
import jax
import jax.numpy as jnp
from jax import lax
from jax.experimental import pallas as pl
from jax.experimental.pallas import tpu as pltpu
from jax.experimental.pallas import tpu_sc as plsc

F32 = jnp.float32
BF16 = jnp.bfloat16

N_MOD = 6
GLA_HEADS = 4
GLA_LOWRANK = 16
GLA_TAU = 16.0
GLA_CHUNK = 64
DIL_PATTERNS = ((128, 1), (512, 4), (2048, 16))
DIL_GROUPS = len(DIL_PATTERNS)
DIL_HEADS_PER_GROUP = 8
DIL_HEAD_DIM = 64
DIL_GROUP_WIDTH = DIL_HEADS_PER_GROUP * DIL_HEAD_DIM
DIL_BLOCK = 128
REL_BUCKETS = 32
REL_MAX_DIST = 2048
MOE_GROUPS = 4
MOE_EXPERTS = 8
MOE_TOTAL = MOE_GROUPS * MOE_EXPERTS
LN_EPS = 1e-5
DEPTH = 1

LANES = 128
VMEM_LIMIT = 56 * 1024 * 1024
LOG2E = 1.4426950408889634
LN2 = 0.6931471805599453
NEG = -1e30
ROW_TILE = 512
PROJ_CHUNK = 512
FINAL_TILE = 1024
EXPERT_TILE = 512
EXPERT_BLOCK = 128
GLA_STEP_CHUNKS = 16
ATT_STEP_BLOCKS = 8
ROUTER_ROWS = 40
ROUTE_ROWS = 8

NT_DIMS = (((1,), (1,)), ((), ()))
TN_DIMS = (((0,), (0,)), ((), ()))


def _cparams(sem):
    return pltpu.CompilerParams(dimension_semantics=sem, vmem_limit_bytes=VMEM_LIMIT)


def _sigmoid(x):
    return 0.5 * jnp.tanh(0.5 * x) + 0.5


def _silu(x):
    return x * _sigmoid(x)


def _layer_norm(x, g, b):
    mu = jnp.mean(x, axis=-1, keepdims=True)
    xc = x - mu
    var = jnp.mean(xc * xc, axis=-1, keepdims=True)
    return xc * lax.rsqrt(var + LN_EPS) * g + b


def _pack_bf16_pairs(x):
    w = x.shape[1] // 2
    lo = lax.bitcast_convert_type(x[:, :w].astype(BF16).astype(F32), jnp.uint32) >> 16
    hi = lax.bitcast_convert_type(x[:, w:].astype(BF16).astype(F32), jnp.uint32) & jnp.uint32(0xFFFF0000)
    return lax.bitcast_convert_type(lo | hi, jnp.int32)


def _unpack_bf16_pairs(p):
    u = lax.bitcast_convert_type(p, jnp.uint32)
    lo = lax.bitcast_convert_type(u << 16, F32)
    hi = lax.bitcast_convert_type(u & jnp.uint32(0xFFFF0000), F32)
    return jnp.concatenate([lo, hi], axis=1)


def _mods_kernel(ct_ref, w_ref, b_ref, o_ref):
    a = _silu(ct_ref[...])
    w = w_ref[...]
    for b in range(a.shape[1]):
        o_ref[b:b + 1, :] = jnp.sum(a[:, b:b + 1] * w, axis=0, keepdims=True) + b_ref[...]


def _ada_mods(c, w, b):
    bsz, d = c.shape
    n = w.shape[1]
    tn = 1536
    assert n % tn == 0
    return pl.pallas_call(
        _mods_kernel,
        out_shape=jax.ShapeDtypeStruct((bsz, n), F32),
        grid=(n // tn,),
        in_specs=[pl.BlockSpec((d, bsz), lambda j: (0, 0)),
                  pl.BlockSpec((d, tn), lambda j: (0, j)),
                  pl.BlockSpec((1, tn), lambda j: (0, j))],
        out_specs=pl.BlockSpec((bsz, tn), lambda j: (0, j)),
        compiler_params=_cparams(("arbitrary",)),
        name="ada_mods",
    )(c.T, w, b.reshape(1, n))


def _proj_pieces(d_model):
    dk = d_model // 2
    pieces = [("q_gla", dk, "scale_q_gla"), ("k_gla", dk, None), ("v_gla", d_model, None), ("r_gla", d_model, "silu")]
    for name, post in (("q_att", "scale_q_att"), ("k_att", None), ("v_att", None)):
        for g, (_, dilation) in enumerate(DIL_PATTERNS):
            pieces.append((f"{name}{g}", DIL_GROUP_WIDTH, (post, dilation)))
    pieces += [("g_gla", d_model, "sigmoid"), ("g_att", d_model, "sigmoid"), ("lr", LANES, "lowrank")]
    return tuple(pieces)


WT_BLOCK = 512


def _load_in_weight(lr0, wt_hbm, w_ref, stage_ref, sem):
    n_main = wt_hbm.shape[1] - GLA_LOWRANK
    n_blocks = n_main // WT_BLOCK + 1
    src_rows = [j * WT_BLOCK + (GLA_LOWRANK if j * WT_BLOCK >= lr0 else 0) for j in range(n_blocks - 1)] + [lr0]

    def copy(j):
        return pltpu.make_async_copy(wt_hbm.at[0, pl.ds(src_rows[j], WT_BLOCK), :], stage_ref.at[j % 2], sem.at[j % 2])

    copy(0).start()
    for j in range(n_blocks):
        if j + 1 < n_blocks:
            copy(j + 1).start()
        copy(j).wait()
        blk = stage_ref[j % 2]
        if j == n_blocks - 1:
            blk = jnp.where(lax.broadcasted_iota(jnp.int32, blk.shape, 0) < GLA_LOWRANK, blk, 0.0)
        w_ref[j * WT_BLOCK:(j + 1) * WT_BLOCK, :] = blk.astype(BF16)


GLA_HELD = ("lr", "q_gla", "k_gla")


def _gla_operands(hold, wg_ref, bg_ref, qin_ref, kin_ref, qst_ref, kst_ref, dec_ref):
    c = GLA_CHUNK
    tm = hold["q_gla"].shape[0]
    tril = (lax.broadcasted_iota(jnp.int32, (c, c), 0) >= lax.broadcasted_iota(jnp.int32, (c, c), 1)).astype(BF16)
    mid = c // 2 - 1
    lr, wg = hold["lr"][:, 0:GLA_LOWRANK], wg_ref[...]
    lr_hi, wg_hi = lr.astype(BF16), wg.astype(BF16)
    lr_lo, wg_lo = (lr - lr_hi.astype(F32)).astype(BF16), (wg - wg_hi.astype(F32)).astype(BF16)
    gate_in = (jnp.dot(lr_hi, wg_hi, preferred_element_type=F32) + jnp.dot(lr_lo, wg_hi, preferred_element_type=F32)
               + jnp.dot(lr_hi, wg_lo, preferred_element_type=F32)) + bg_ref[...]
    g_all = (jnp.minimum(gate_in, 0.0) - jnp.log(1.0 + jnp.exp(-jnp.abs(gate_in)))) * (1.0 / GLA_TAU)
    g_hi = g_all.astype(BF16)
    g_lo = (g_all - g_hi.astype(F32)).astype(BF16)
    for ci in range(tm // c):
        rows = slice(ci * c, (ci + 1) * c)
        bc = jnp.dot(tril, g_hi[rows], preferred_element_type=F32) + jnp.dot(tril, g_lo[rows], preferred_element_type=F32)
        b_mid = bc[mid:mid + 1, :]
        b_last = bc[c - 1:c, :]
        qf = hold["q_gla"][rows, :]
        kf = hold["k_gla"][rows, :]
        q_in = qf * jnp.exp(bc - b_mid)
        k_in = kf * jnp.exp(b_mid - bc)
        qin_ref[0, rows, :] = q_in.astype(BF16)
        kin_ref[0, rows, :] = k_in.astype(BF16)
        qst_ref[0, rows, :] = (q_in * jnp.exp(b_mid)).astype(BF16)
        kst_ref[0, rows, :] = (k_in * jnp.exp(b_last - b_mid)).astype(BF16)
        dec_ref[0, ci:ci + 1, :] = jnp.exp(b_last)


def _proj_kernel(pieces, head_k, lr0, x_ref, sc_ref, sh_ref, wt_hbm, wg_ref, bg_ref, *refs):
    n_out = len(pieces) - len(GLA_HELD)
    out_refs = dict(zip([p[0] for p in pieces if p[0] not in GLA_HELD], refs[:n_out]))
    gla_out_refs = refs[n_out:n_out + 5]
    stage_ref = refs[n_out + 5]
    hold = dict(zip(GLA_HELD, refs[n_out + 6:n_out + 9]))
    w_ref, wstage_ref, wsem = refs[n_out + 9:]
    tm = x_ref.shape[1]

    @pl.when((pl.program_id(0) == 0) & (pl.program_id(1) == 0))
    def _():
        _load_in_weight(lr0, wt_hbm, w_ref, wstage_ref, wsem)

    u = (x_ref[0] * (1.0 + sc_ref[0]) + sh_ref[0]).astype(BF16)
    offsets, off = {}, 0
    for name, width, _ in pieces:
        offsets[name] = off
        off += width
    by_name = {p[0]: p for p in pieces}
    held = [(by_name[n], 0) for n in GLA_HELD]
    rest = [(p, c0) for p in pieces if p[0] not in GLA_HELD for c0 in range(0, p[1], min(p[1], PROJ_CHUNK))]
    for n, (piece, c0) in enumerate(held + rest):
        if n == len(held):
            _gla_operands(hold, wg_ref, bg_ref, *gla_out_refs)
        name, width, post = piece
        o_ref = hold[name] if name in GLA_HELD else out_refs[name]
        off = offsets[name]
        chunk = min(width, PROJ_CHUNK)
        acc = lax.dot_general(u, w_ref[off + c0:off + c0 + chunk, :], NT_DIMS, preferred_element_type=F32)
        if post == "silu":
            acc = _silu(acc)
        elif post == "sigmoid":
            acc = _sigmoid(acc)
        elif post == "scale_q_gla":
            acc = acc * (head_k ** -0.5)
        if name in GLA_HELD:
            o_ref[...] = acc
        elif isinstance(post, tuple):
            scale, dilation = post
            if scale is not None:
                acc = acc * (DIL_HEAD_DIM ** -0.5 * LOG2E)
            if dilation == 1:
                o_ref[0, 0] = acc.astype(o_ref.dtype)
            else:
                for t in range(width // LANES):
                    stage_ref[t] = acc[:, t * LANES:(t + 1) * LANES]
                for r in range(dilation):
                    for t in range(width // LANES):
                        o_ref[0, r, :, t * LANES:(t + 1) * LANES] = stage_ref[
                            t, pl.ds(r, tm // dilation, stride=dilation), :].astype(o_ref.dtype)
        else:
            o_ref[0, :, c0:c0 + chunk] = acc.astype(o_ref.dtype)


def _in_projection(x, sc1, sh1, w_in, lr0, w_gate, b_gate):
    bsz, s, d = x.shape
    pieces = _proj_pieces(d)
    w_t = jnp.swapaxes(w_in, 1, 2)
    n_main = w_t.shape[1] - GLA_LOWRANK
    assert lr0 % WT_BLOCK == 0 and n_main % WT_BLOCK == 0
    w_rows = n_main + WT_BLOCK
    assert sum(p[1] for p in pieces) <= w_rows
    tm = min(ROW_TILE, s)
    assert s % tm == 0 and tm % (8 * GLA_CHUNK) == 0
    dk = d // 2
    head_k = dk // GLA_HEADS
    out_shape, out_specs = [], []
    for name, width, post in pieces:
        if name in GLA_HELD:
            continue
        if isinstance(post, tuple):
            dil = post[1]
            assert tm % (dil * 16) == 0
            out_shape.append(jax.ShapeDtypeStruct((bsz, dil, s // dil, width), BF16))
            out_specs.append(pl.BlockSpec((1, dil, tm // dil, width), lambda b, i: (b, 0, i, 0)))
        else:
            out_shape.append(jax.ShapeDtypeStruct((bsz, s, width), BF16))
            out_specs.append(pl.BlockSpec((1, tm, width), lambda b, i: (b, i, 0)))
    row = lambda w: pl.BlockSpec((1, tm, w), lambda b, i: (b, i, 0))
    gla_names = ("q_in", "k_in", "q_st", "k_st", "dec")
    out_shape += [jax.ShapeDtypeStruct((bsz, s, dk), BF16)] * 4 + [jax.ShapeDtypeStruct((bsz, s // GLA_CHUNK, dk), F32)]
    out_specs += [row(dk)] * 4 + [pl.BlockSpec((1, tm // GLA_CHUNK, dk), lambda b, i: (b, i, 0))]
    bg = b_gate.reshape(1, dk)
    full = lambda a: pl.BlockSpec(a.shape, lambda b, i: (0,) * a.ndim)
    outs = pl.pallas_call(
        functools.partial(_proj_kernel, pieces, head_k, lr0),
        out_shape=out_shape,
        grid=(bsz, s // tm),
        in_specs=[row(d),
                  pl.BlockSpec((1, 1, d), lambda b, i: (b, 0, 0)),
                  pl.BlockSpec((1, 1, d), lambda b, i: (b, 0, 0)),
                  pl.BlockSpec(memory_space=pl.ANY),
                  full(w_gate), full(bg)],
        out_specs=out_specs,
        scratch_shapes=[pltpu.VMEM((DIL_GROUP_WIDTH // LANES, tm, LANES), F32),
                        pltpu.VMEM((tm, LANES), F32), pltpu.VMEM((tm, dk), F32), pltpu.VMEM((tm, dk), F32),
                        pltpu.VMEM((w_rows, d), BF16), pltpu.VMEM((2, WT_BLOCK, d), F32),
                        pltpu.SemaphoreType.DMA((2,))],
        compiler_params=_cparams(("arbitrary", "arbitrary")),
        name="in_projection",
    )(x, sc1, sh1, w_t, w_gate, bg)
    return dict(zip([p[0] for p in pieces if p[0] not in GLA_HELD] + list(gla_names), outs))


def _gla_kernel(n_chunks, head_k, head_v, qin_ref, kin_ref, qst_ref, kst_ref, dec_ref, v_ref, r_ref, ng_ref, o_ref,
                state_ref):
    @pl.when(pl.program_id(1) == 0)
    def _():
        state_ref[...] = jnp.zeros_like(state_ref)

    c = GLA_CHUNK
    causal = lax.broadcasted_iota(jnp.int32, (c, c), 0) >= lax.broadcasted_iota(jnp.int32, (c, c), 1)
    for ci in range(n_chunks):
        rows = slice(ci * c, (ci + 1) * c)
        for h in range(GLA_HEADS):
            ks = slice(h * head_k, (h + 1) * head_k)
            vs = slice(h * head_v, (h + 1) * head_v)
            vh = v_ref[0, rows, vs]
            att = lax.dot_general(qin_ref[0, rows, ks], kin_ref[0, rows, ks], NT_DIMS, preferred_element_type=F32)
            att = jnp.where(causal, att, 0.0).astype(BF16)
            st = state_ref[h]
            o = jnp.dot(att, vh, preferred_element_type=F32)
            o = o + lax.dot_general(qst_ref[0, rows, ks], st.astype(BF16), NT_DIMS, preferred_element_type=F32)
            kv_t = lax.dot_general(vh, kst_ref[0, rows, ks], TN_DIMS, preferred_element_type=F32)
            state_ref[h] = st * dec_ref[0, ci:ci + 1, ks] + kv_t
            ms = jnp.mean(o * o, axis=-1, keepdims=True)
            o = o * lax.rsqrt(ms + LN_EPS) * ng_ref[:, vs] * r_ref[0, rows, vs].astype(F32)
            o_ref[0, rows, vs] = o.astype(o_ref.dtype)


def _gla(q_in, k_in, q_st, k_st, dec, v, r_silu, norm_g):
    bsz, s, dk = q_in.shape
    dv = v.shape[-1]
    head_k, head_v = dk // GLA_HEADS, dv // GLA_HEADS
    n_chunks = min(GLA_STEP_CHUNKS, s // GLA_CHUNK)
    ct = GLA_CHUNK * n_chunks
    assert s % ct == 0
    row_spec = lambda w: pl.BlockSpec((1, ct, w), lambda b, i: (b, i, 0))
    full = lambda a: pl.BlockSpec(a.shape, lambda b, i: (0,) * a.ndim)
    ng = norm_g.reshape(1, dv)
    return pl.pallas_call(
        functools.partial(_gla_kernel, n_chunks, head_k, head_v),
        out_shape=jax.ShapeDtypeStruct((bsz, s, dv), BF16),
        grid=(bsz, s // ct),
        in_specs=[row_spec(dk)] * 4 + [pl.BlockSpec((1, n_chunks, dk), lambda b, i: (b, i, 0)),
                                       row_spec(dv), row_spec(dv), full(ng)],
        out_specs=row_spec(dv),
        scratch_shapes=[pltpu.VMEM((GLA_HEADS, head_v, head_k), F32)],
        compiler_params=_cparams(("parallel", "arbitrary")),
        name="gla",
    )(q_in, k_in, q_st, k_st, dec, v, r_silu, ng)


def _t5_bucket_np(dist):
    exact = REL_BUCKETS // 2
    d = np.maximum(dist, 1).astype(np.float32)
    large = exact + (np.log(d / np.float32(exact)) / np.float32(math.log(REL_MAX_DIST / exact))
                     * np.float32(REL_BUCKETS - exact)).astype(np.int32)
    large = np.minimum(large, REL_BUCKETS - 1)
    return np.where(dist < exact, dist, large).astype(np.int32)


def _band_tables(window, dilation):
    qi = np.arange(DIL_BLOCK)[:, None]
    kj = np.arange(2 * DIL_BLOCK)[None, :]
    m = qi + DIL_BLOCK - kj
    n_steps = window // dilation
    band = (m >= 0) & (m <= n_steps)
    bucket = _t5_bucket_np(np.clip(m, 0, n_steps) * dilation)
    return np.where(band, bucket, -1).astype(np.int32)


def _attn_kernel(nq, table_ref, bucket_ref, q_ref, kp_ref, kc_ref, vp_ref, vc_ref, o_ref, lse_ref,
                 bias_ref, p_ref):
    i = pl.program_id(1)
    blk = DIL_BLOCK
    hpg = DIL_HEADS_PER_GROUP
    n_pairs = hpg // 2

    @pl.when((pl.program_id(0) == 0) & (i == 0))
    def _():
        bucket = bucket_ref[...]
        for h in range(hpg):
            acc = jnp.full(bucket.shape, NEG, F32)
            for bkt in range(REL_BUCKETS):
                acc = jnp.where(bucket == bkt, table_ref[bkt, h] * LOG2E, acc)
            bias_ref[h * blk:(h + 1) * blk, :] = acc

    lane = lax.broadcasted_iota(jnp.int32, (blk, LANES), 1)
    low = lane < DIL_HEAD_DIM
    ones_rhs = jnp.ones((2 * blk, LANES), BF16)

    def windows(ref_p, ref_c, sq, qb, cols):
        if qb == 0:
            return jnp.concatenate([ref_p[sq, :, cols], ref_c[sq, 0:blk, cols]], axis=0)
        return ref_c[sq, (qb - 1) * blk:(qb + 1) * blk, cols]

    key_lane = lax.broadcasted_iota(jnp.int32, (1, 2 * blk), 1)
    no_prev = jnp.where((key_lane < blk) & (i == 0), NEG, 0.0)
    items = [(sq, qb, hp) for sq in range(q_ref.shape[0]) for qb in range(nq) for hp in range(n_pairs)]

    mxs = []
    for n, (sq, qb, hp) in enumerate(items):
        rows = slice(qb * blk, (qb + 1) * blk)
        cols = slice(hp * LANES, (hp + 1) * LANES)
        qp = q_ref[sq, rows, cols]
        zero = jnp.zeros_like(qp)
        qq = jnp.concatenate([jnp.where(low, qp, zero), jnp.where(low, zero, qp)], axis=0)
        keys = windows(kp_ref, kc_ref, sq, qb, cols)
        s = lax.dot_general(qq, keys, NT_DIMS, preferred_element_type=F32) + bias_ref[2 * hp * blk:(2 * hp + 2) * blk, :]
        if qb == 0:
            s = s + no_prev
        mx = jnp.max(s, axis=-1, keepdims=True)
        p_ref[n * 2 * blk:(n + 1) * 2 * blk, :] = jnp.exp2(s - mx).astype(BF16)
        mxs.append(mx)

    for n, (sq, qb, hp) in enumerate(items):
        rows = slice(qb * blk, (qb + 1) * blk)
        cols = slice(hp * LANES, (hp + 1) * LANES)
        vals = windows(vp_ref, vc_ref, sq, qb, cols)
        rhs = jnp.concatenate([vals, ones_rhs], axis=1)
        res = jnp.dot(p_ref[n * 2 * blk:(n + 1) * 2 * blk, :], rhs, preferred_element_type=F32)
        num = jnp.where(low, res[0:blk, 0:LANES], res[blk:2 * blk, 0:LANES])
        den = jnp.where(low, res[0:blk, LANES:], res[blk:2 * blk, LANES:])
        mx = jnp.where(low, mxs[n][0:blk], mxs[n][blk:2 * blk])
        o_ref[sq, rows, cols] = (num / den).astype(o_ref.dtype)
        lse_ref[sq, rows, cols] = (mx + jnp.log2(den)) * LN2


def _dilated_group_attention(q, k, v, table, window, dilation):
    bb, l, w = q.shape
    nq = min(ATT_STEP_BLOCKS, l // DIL_BLOCK)
    nsq = ATT_STEP_BLOCKS // nq
    assert l % (nq * DIL_BLOCK) == 0 and bb % nsq == 0
    steps = l // (nq * DIL_BLOCK)
    bucket = jnp.asarray(_band_tables(window, dilation))
    cur = pl.BlockSpec((nsq, nq * DIL_BLOCK, w), lambda b, i: (b, i, 0))
    prev = pl.BlockSpec((nsq, DIL_BLOCK, w), lambda b, i: (b, jnp.maximum(nq * i - 1, 0), 0))
    rows_all = nsq * nq * DIL_HEADS_PER_GROUP * DIL_BLOCK
    return pl.pallas_call(
        functools.partial(_attn_kernel, nq),
        out_shape=[jax.ShapeDtypeStruct((bb, l, w), BF16), jax.ShapeDtypeStruct((bb, l, w), F32)],
        grid=(bb // nsq, steps),
        in_specs=[pl.BlockSpec(memory_space=pltpu.SMEM),
                  pl.BlockSpec(bucket.shape, lambda b, i: (0, 0)),
                  cur, prev, cur, prev, cur],
        out_specs=[cur, cur],
        scratch_shapes=[pltpu.VMEM((DIL_HEADS_PER_GROUP * DIL_BLOCK, 2 * DIL_BLOCK), F32),
                        pltpu.VMEM((rows_all, 2 * DIL_BLOCK), BF16)],
        compiler_params=_cparams(("arbitrary", "arbitrary")),
        name=f"dilated_attn_d{dilation}",
    )(table, bucket, q, k, k, v, v)


def _load_bf16_rows(blocks, stage_ref, sem):
    def copy(j):
        src, r0, _ = blocks[j]
        return pltpu.make_async_copy(src.at[pl.ds(r0, WT_BLOCK), :], stage_ref.at[j % 2], sem.at[j % 2])

    copy(0).start()
    for j, (_, r0, dst) in enumerate(blocks):
        if j + 1 < len(blocks):
            copy(j + 1).start()
        copy(j).wait()
        dst[r0:r0 + WT_BLOCK, :] = stage_ref[j % 2].astype(BF16)


def _merge_kernel(alpha, dilations, ygla_ref, o0_ref, o1_ref, o2_ref, l0_ref, l1_ref, l2_ref, gg_ref, ga_ref, x_ref,
                  g1_ref, sc2_ref, sh2_ref, ln_g_ref, ln_b_ref, wpg_hbm, wpa_hbm, wout_hbm, wr_ref, br_ref, utri_ref,
                  x1_ref, u2_ref, route_ref, ew_ref, cnt_ref, stage_ref, carry_ref,
                  wpg_ref, wpa_ref, wout_ref, wstage_ref, wsem):
    tm = x_ref.shape[1]

    @pl.when((pl.program_id(0) == 0) & (pl.program_id(1) == 0))
    def _():
        carry_ref[...] = jnp.zeros_like(carry_ref)
        _load_bf16_rows([(hbm, r0, dst) for hbm, dst in ((wpg_hbm, wpg_ref), (wpa_hbm, wpa_ref), (wout_hbm, wout_ref))
                         for r0 in range(0, hbm.shape[0], WT_BLOCK)], wstage_ref, wsem)

    n_lt = DIL_GROUP_WIDTH // LANES
    group_refs = tuple(zip((l0_ref, l1_ref, l2_ref), (o0_ref, o1_ref, o2_ref), dilations))
    for gi, (l_ref, o_ref, dil) in enumerate(group_refs):
        if dil > 1:
            for slot, ref in ((2 * gi, l_ref), (2 * gi + 1, o_ref)):
                for r in range(dil):
                    for t in range(n_lt):
                        stage_ref[slot, t, pl.ds(r, tm // dil, stride=dil), :] = ref[
                            0, r, :, t * LANES:(t + 1) * LANES].astype(F32)

    def natural(ref, dil, slot):
        if dil == 1:
            return ref[0, 0].astype(F32)
        return jnp.concatenate([stage_ref[slot, t] for t in range(n_lt)], axis=1)

    lses = [natural(l_ref, dil, 2 * gi) for gi, (l_ref, _, dil) in enumerate(group_refs)]
    outs = [natural(o_ref, dil, 2 * gi + 1) for gi, (_, o_ref, dil) in enumerate(group_refs)]
    lm = jnp.maximum(jnp.maximum(lses[0], lses[1]), lses[2])
    es = [jnp.exp(l - lm) for l in lses]
    y_att = (es[0] * outs[0] + es[1] * outs[1] + es[2] * outs[2]) / (es[0] + es[1] + es[2])

    p_gla = jnp.dot(ygla_ref[0], wpg_ref[...], preferred_element_type=F32)
    p_att = jnp.dot(y_att.astype(BF16), wpa_ref[...], preferred_element_type=F32)
    merged = gg_ref[0].astype(F32) * p_gla + ga_ref[0].astype(F32) * p_att
    y = jnp.dot(merged.astype(BF16), wout_ref[...], preferred_element_type=F32)
    x1 = _layer_norm(alpha * x_ref[0] + g1_ref[0] * y, ln_g_ref[...], ln_b_ref[...])
    x1_ref[0] = x1
    u2 = x1 * (1.0 + sc2_ref[0]) + sh2_ref[0]
    u2_ref[0] = _pack_bf16_pairs(u2)

    logits = jnp.dot(u2.astype(BF16), wr_ref[...].astype(BF16), preferred_element_type=F32) + br_ref[...]
    lt = jnp.transpose(logits)[0:ROUTER_ROWS, :]
    rowi = lax.broadcasted_iota(jnp.int32, lt.shape, 0)
    big = jnp.int32(LANES)
    lg = jnp.where(rowi < MOE_GROUPS, lt, NEG)
    gmax = jnp.max(lg, axis=0, keepdims=True)
    gidx = jnp.min(jnp.where(lg == gmax, rowi, big), axis=0, keepdims=True)
    gval = 1.0 / jnp.sum(jnp.exp(lg - gmax), axis=0, keepdims=True)
    first = MOE_GROUPS + gidx * MOE_EXPERTS
    le = jnp.where((rowi >= first) & (rowi < first + MOE_EXPERTS), lt, NEG)
    m1 = jnp.max(le, axis=0, keepdims=True)
    i1 = jnp.min(jnp.where(le == m1, rowi, big), axis=0, keepdims=True)
    le2 = jnp.where(rowi == i1, NEG, le)
    m2 = jnp.max(le2, axis=0, keepdims=True)
    i2 = jnp.min(jnp.where(le2 == m2, rowi, big), axis=0, keepdims=True)
    t = jnp.exp(m2 - m1)
    w1 = 1.0 / (1.0 + t)
    w2 = t * w1

    hit1, hit2 = rowi == i1, rowi == i2
    onehot = jnp.where(hit1 | hit2, 1.0, 0.0)
    earlier = jnp.dot(onehot.astype(BF16), utri_ref[...], preferred_element_type=F32) + carry_ref[...]
    rank1 = jnp.sum(jnp.where(hit1, earlier, 0.0), axis=0, keepdims=True).astype(jnp.int32)
    rank2 = jnp.sum(jnp.where(hit2, earlier, 0.0), axis=0, keepdims=True).astype(jnp.int32)
    carry = carry_ref[...] + jnp.sum(onehot, axis=1, keepdims=True)
    carry_ref[...] = carry
    cnt_ref[...] = jnp.broadcast_to(carry, cnt_ref.shape).astype(jnp.int32)
    r8 = lax.broadcasted_iota(jnp.int32, (ROUTE_ROWS, tm), 0)
    route_ref[0] = jnp.where(r8 == 0, i1 - MOE_GROUPS, jnp.where(r8 == 1, i2 - MOE_GROUPS,
                             jnp.where(r8 == 2, rank1, jnp.where(r8 == 3, rank2, 0))))
    r128 = lax.broadcasted_iota(jnp.int32, (LANES, tm), 0)
    ew_ref[0] = jnp.transpose(jnp.where(r128 == 0, gval * w1, jnp.where(r128 == 1, gval * w2, 0.0)))


def _merge(alpha, y_gla, o_groups, lse_groups, g_gla, g_att, x, g1, sc2, sh2, ln_g, ln_b, wpg, wpa, wout, wr, br):
    bsz, s, d = x.shape
    tm = min(ROW_TILE, s)
    assert s % tm == 0
    dilations = tuple(dil for _, dil in DIL_PATTERNS)
    row = lambda w: pl.BlockSpec((1, tm, w), lambda b, i: (b, i, 0))
    sub = lambda dil: pl.BlockSpec((1, dil, tm // dil, DIL_GROUP_WIDTH), lambda b, i: (b, 0, i, 0))
    per_b = pl.BlockSpec((1, 1, d), lambda b, i: (b, 0, 0))
    full = lambda a: pl.BlockSpec(a.shape, lambda b, i: (0,) * a.ndim)
    hbm = pl.BlockSpec(memory_space=pl.ANY)
    assert all(w.shape[0] % WT_BLOCK == 0 and w.shape[1] == d for w in (wpg, wpa, wout))
    ln_g2, ln_b2 = ln_g.reshape(1, d), ln_b.reshape(1, d)
    utri = jnp.asarray(np.triu(np.ones((tm, tm), np.float32), 1), BF16)
    return pl.pallas_call(
        functools.partial(_merge_kernel, alpha, dilations),
        out_shape=[jax.ShapeDtypeStruct((bsz, s, d), F32), jax.ShapeDtypeStruct((bsz, s, d // 2), jnp.int32),
                   jax.ShapeDtypeStruct((bsz, ROUTE_ROWS, s), jnp.int32), jax.ShapeDtypeStruct((bsz, s, LANES), F32),
                   jax.ShapeDtypeStruct((ROUTER_ROWS, LANES), jnp.int32)],
        grid=(bsz, s // tm),
        in_specs=[row(y_gla.shape[-1])] + [sub(dil) for dil in dilations] * 2
                 + [row(d), row(d), row(d), per_b, per_b, per_b, full(ln_g2), full(ln_b2),
                    hbm, hbm, hbm, full(wr), full(br), full(utri)],
        out_specs=[row(d), row(d // 2), pl.BlockSpec((1, ROUTE_ROWS, tm), lambda b, i: (b, 0, i)), row(LANES),
                   pl.BlockSpec((ROUTER_ROWS, LANES), lambda b, i: (0, 0))],
        scratch_shapes=[pltpu.VMEM((2 * DIL_GROUPS, DIL_GROUP_WIDTH // LANES, tm, LANES), F32),
                        pltpu.VMEM((ROUTER_ROWS, 1), F32),
                        pltpu.VMEM(wpg.shape, BF16), pltpu.VMEM(wpa.shape, BF16), pltpu.VMEM(wout.shape, BF16),
                        pltpu.VMEM((2, WT_BLOCK, d), F32), pltpu.SemaphoreType.DMA((2,))],
        compiler_params=_cparams(("arbitrary", "arbitrary")),
        name="merge_ln1_router",
    )(y_gla, *o_groups, *lse_groups, g_gla, g_att, x, g1, sc2, sh2, ln_g2, ln_b2, wpg, wpa, wout, wr, br, utri)


def _expert_kernel(run_ref, valid_ref, rexp_ref, used_ref, x_ref, wg_hbm, wu_hbm, wd_hbm, o_ref,
                   wg_f, wu_f, wd_f, wg_s, wu_s, wd_s, sem):
    t = pl.program_id(0)
    n_tiles_used, n_runs = used_ref[0], used_ref[1]
    run = run_ref[t]
    active = t < n_tiles_used
    first_of_run = (t == 0) | (run_ref[jnp.maximum(t - 1, 0)] != run)

    def weight_copies(r):
        e, slot = rexp_ref[r], r % 2
        return [pltpu.make_async_copy(hbm.at[e], buf.at[slot], sem.at[slot, j])
                for j, (hbm, buf) in enumerate(((wg_hbm, wg_f), (wu_hbm, wu_f), (wd_hbm, wd_f)))]

    @pl.when(active & (t == 0))
    def _():
        for cp in weight_copies(0):
            cp.start()

    @pl.when(active & first_of_run)
    def _():
        @pl.when(run + 1 < n_runs)
        def _():
            for cp in weight_copies(run + 1):
                cp.start()

        for cp in weight_copies(run):
            cp.wait()
        slot = run % 2
        wg_s[...] = wg_f[slot].astype(BF16)
        wu_s[...] = wu_f[slot].astype(BF16)
        wd_s[...] = wd_f[slot].astype(BF16)

    n_valid = jnp.where(active, valid_ref[t], 0)

    def ffn(rows):
        xt = _unpack_bf16_pairs(x_ref[rows, :]).astype(BF16)
        hg = jnp.dot(xt, wg_s[...], preferred_element_type=F32)
        hu = jnp.dot(xt, wu_s[...], preferred_element_type=F32)
        h = (_silu(hg) * hu).astype(BF16)
        o_ref[rows, :] = _pack_bf16_pairs(jnp.dot(h, wd_s[...], preferred_element_type=F32))

    def zero(rows):
        o_ref[rows, :] = jnp.zeros((rows.stop - rows.start, o_ref.shape[1]), o_ref.dtype)

    tm = x_ref.shape[0]
    n_blocks = tm // EXPERT_BLOCK
    for k in range(n_blocks + 1):
        lo, hi = (k - 1) * EXPERT_BLOCK, k * EXPERT_BLOCK

        @pl.when((n_valid > lo) & (n_valid <= hi) if 0 < k < n_blocks else (n_valid > lo if k else n_valid <= 0))
        def _():
            if k:
                ffn(slice(0, hi))
            if k < n_blocks:
                zero(slice(hi, tm))


def _expert_ffn(tile_run, tile_valid, run_expert, used, xg, w_gate, w_up, w_down):
    p = xg.shape[0]
    ne, d, ff = w_gate.shape
    tm = EXPERT_TILE
    n_tiles = p // tm
    hbm = pl.BlockSpec(memory_space=pl.ANY)

    def tile(t, run, valid, rexp, used):
        return jnp.where(t < used[0], t, n_tiles - 1), 0

    grid_spec = pltpu.PrefetchScalarGridSpec(
        num_scalar_prefetch=4,
        grid=(n_tiles,),
        in_specs=[pl.BlockSpec((tm, d // 2), tile), hbm, hbm, hbm],
        out_specs=pl.BlockSpec((tm, d // 2), tile),
        scratch_shapes=[pltpu.VMEM((2, d, ff), F32), pltpu.VMEM((2, d, ff), F32), pltpu.VMEM((2, ff, d), F32),
                        pltpu.VMEM((d, ff), BF16), pltpu.VMEM((d, ff), BF16), pltpu.VMEM((ff, d), BF16),
                        pltpu.SemaphoreType.DMA((2, 3))],
    )
    return pl.pallas_call(
        _expert_kernel,
        out_shape=jax.ShapeDtypeStruct((p, d // 2), jnp.int32),
        grid_spec=grid_spec,
        compiler_params=_cparams(("arbitrary",)),
        name="expert_ffn",
    )(tile_run, tile_valid, run_expert, used, xg, w_gate, w_up, w_down)


def _final_kernel(alpha, x1_ref, ya_ref, yb_ref, ew_ref, g2_ref, ln_g_ref, ln_b_ref, o_ref):
    ew = ew_ref[0]
    y = ew[:, 0:1] * _unpack_bf16_pairs(ya_ref[0]) + ew[:, 1:2] * _unpack_bf16_pairs(yb_ref[0])
    o_ref[0] = _layer_norm(alpha * x1_ref[0] + g2_ref[0] * y, ln_g_ref[...], ln_b_ref[...])


def _final(alpha, x1, ya, yb, ew, g2, ln_g, ln_b):
    bsz, s, d = x1.shape
    tm = min(FINAL_TILE, s)
    row = lambda w: pl.BlockSpec((1, tm, w), lambda b, i: (b, i, 0))
    full = lambda a: pl.BlockSpec(a.shape, lambda b, i: (0,) * a.ndim)
    ln_g2, ln_b2 = ln_g.reshape(1, d), ln_b.reshape(1, d)
    return pl.pallas_call(
        functools.partial(_final_kernel, alpha),
        out_shape=jax.ShapeDtypeStruct((bsz, s, d), F32),
        grid=(bsz, s // tm),
        in_specs=[row(d), row(d // 2), row(d // 2), row(LANES), pl.BlockSpec((1, 1, d), lambda b, i: (b, 0, 0)),
                  full(ln_g2), full(ln_b2)],
        out_specs=row(d),
        compiler_params=_cparams(("parallel", "arbitrary")),
        name="combine_ln2",
    )(x1, ya, yb, ew, g2, ln_g2, ln_b2)


SC_CORES = 2
SC_SUBCORES = 16
SC_CHUNK = 64


def _sc_mesh():
    return plsc.VectorSubcoreMesh(core_axis_name="c", subcore_axis_name="s")


def _sc_scatter_rows(rows, dest0, dest1, n_rows):
    n, w = rows.shape
    n_workers = SC_CORES * SC_SUBCORES
    assert n % (n_workers * SC_CHUNK) == 0
    n_chunks = n // (n_workers * SC_CHUNK)
    d0 = dest0.reshape(n // SC_CHUNK, 1, SC_CHUNK)
    d1 = dest1.reshape(n // SC_CHUNK, 1, SC_CHUNK)

    @functools.partial(
        pl.kernel, mesh=_sc_mesh(), out_type=jax.ShapeDtypeStruct((n_rows, w), rows.dtype),
        scratch_types=[pltpu.VMEM((n_chunks, 1, SC_CHUNK), jnp.int32), pltpu.VMEM((n_chunks, 1, SC_CHUNK), jnp.int32),
                       pltpu.VMEM((2, SC_CHUNK, w), rows.dtype),
                       pltpu.SemaphoreType.DMA((2,)), pltpu.SemaphoreType.DMA((2, 2))])
    def scatter_kernel(rows_hbm, d0_hbm, d1_hbm, out_hbm, i0_v, i1_v, rows_v, read_sem, scat_sem):
        wid = lax.axis_index("s") * SC_CORES + lax.axis_index("c")
        first = wid * n_chunks
        pltpu.sync_copy(d0_hbm.at[pl.ds(first, n_chunks)], i0_v)
        pltpu.sync_copy(d1_hbm.at[pl.ds(first, n_chunks)], i1_v)

        def read(j):
            return pltpu.make_async_copy(rows_hbm.at[pl.ds((first + j) * SC_CHUNK, SC_CHUNK)], rows_v.at[j % 2],
                                         read_sem.at[j % 2])

        def scatters(j):
            return [pltpu.make_async_copy(rows_v.at[j % 2], out_hbm.at[idx.at[j].at[0]], scat_sem.at[j % 2, k])
                    for k, idx in enumerate((i0_v, i1_v))]

        read(0).start()
        for j in range(n_chunks):
            read(j).wait()
            if j + 1 < n_chunks:
                if j >= 1:
                    for cp in scatters(j - 1):
                        cp.wait()
                read(j + 1).start()
            for cp in scatters(j):
                cp.start()
        for j in range(max(n_chunks - 2, 0), n_chunks):
            for cp in scatters(j):
                cp.wait()

    return scatter_kernel(rows, d0, d1)


def _sc_gather_rows(table, dest0, dest1):
    n = dest0.shape[0]
    w = table.shape[1]
    n_workers = SC_CORES * SC_SUBCORES
    assert n % (n_workers * SC_CHUNK) == 0
    n_chunks = n // (n_workers * SC_CHUNK)
    d0 = dest0.reshape(n // SC_CHUNK, 1, SC_CHUNK)
    d1 = dest1.reshape(n // SC_CHUNK, 1, SC_CHUNK)
    out = jax.ShapeDtypeStruct((n, w), table.dtype)

    @functools.partial(
        pl.kernel, mesh=_sc_mesh(), out_type=(out, out),
        scratch_types=[pltpu.VMEM((n_chunks, 1, SC_CHUNK), jnp.int32), pltpu.VMEM((n_chunks, 1, SC_CHUNK), jnp.int32),
                       pltpu.VMEM((2, SC_CHUNK, w), table.dtype),
                       pltpu.SemaphoreType.DMA((2,)), pltpu.SemaphoreType.DMA((2,))])
    def gather_kernel(table_hbm, d0_hbm, d1_hbm, a_hbm, b_hbm, i0_v, i1_v, rows_v, gather_sem, write_sem):
        wid = lax.axis_index("s") * SC_CORES + lax.axis_index("c")
        first = wid * n_chunks
        pltpu.sync_copy(d0_hbm.at[pl.ds(first, n_chunks)], i0_v)
        pltpu.sync_copy(d1_hbm.at[pl.ds(first, n_chunks)], i1_v)
        n_items = 2 * n_chunks

        def gather(m):
            idx = (i0_v, i1_v)[m % 2]
            return pltpu.make_async_copy(table_hbm.at[idx.at[m // 2].at[0]], rows_v.at[m % 2], gather_sem.at[m % 2])

        def write(m):
            o_hbm = (a_hbm, b_hbm)[m % 2]
            return pltpu.make_async_copy(rows_v.at[m % 2], o_hbm.at[pl.ds((first + m // 2) * SC_CHUNK, SC_CHUNK)],
                                         write_sem.at[m % 2])

        gather(0).start()
        for m in range(n_items):
            gather(m).wait()
            if m + 1 < n_items:
                if m >= 1:
                    write(m - 1).wait()
                gather(m + 1).start()
            write(m).start()
        for m in range(max(n_items - 2, 0), n_items):
            write(m).wait()

    return gather_kernel(table, d0, d1)


def _dispatch_plan(route, counts):
    tm = EXPERT_TILE
    e0, e1, r0, r1 = (route[:, j, :].reshape(-1) for j in range(4))
    experts = jnp.arange(MOE_TOTAL, dtype=jnp.int32)
    tiles_per = (counts + tm - 1) // tm
    tile_end = jnp.cumsum(tiles_per)
    pad_start = ((tile_end - tiles_per) * tm).astype(jnp.int32)

    def lookup(e):
        return jnp.sum(jnp.where(e[None, :] == experts[:, None], pad_start[:, None], 0), axis=0)

    dest0, dest1 = lookup(e0) + r0, lookup(e1) + r1
    n_tiles = (2 * e0.size + MOE_TOTAL * tm) // tm
    tile_expert = jnp.minimum(jnp.sum(tile_end[None, :] <= jnp.arange(n_tiles)[:, None], axis=1), MOE_TOTAL - 1)
    nonempty = counts > 0
    run_of_expert = jnp.cumsum(nonempty.astype(jnp.int32)) - 1
    run_expert = jnp.sum(jnp.where(nonempty[None, :] & (run_of_expert[None, :] == experts[:, None]),
                                   experts[None, :], 0), axis=1).astype(jnp.int32)
    of_tile = tile_expert[:, None] == experts[None, :]
    tile_run = jnp.sum(jnp.where(of_tile, run_of_expert[None, :], 0), axis=1).astype(jnp.int32)
    rows_left = (counts + pad_start)[None, :] - jnp.arange(n_tiles)[:, None] * tm
    tile_valid = jnp.clip(jnp.sum(jnp.where(of_tile, rows_left, 0), axis=1), 0, tm).astype(jnp.int32)
    used = jnp.stack([tile_end[-1], jnp.sum(nonempty)]).astype(jnp.int32)
    return dest0, dest1, tile_run, tile_valid, run_expert, used, n_tiles * tm


def _layer(x, c, rel_bias, w_ada, b_ada, w_in, w_gla_gate, b_gla_gate, gla_norm, w_proj_gla, w_proj_attn, w_out,
           ln1_g, ln1_b, w_rg, b_rg, w_re, b_re, w_eg, w_eu, w_ed, ln2_g, ln2_b):
    bsz, s, d = x.shape
    alpha = (2.0 * DEPTH) ** 0.25
    mods = _ada_mods(c, w_ada, b_ada)
    sh1, sc1, g1, sh2, sc2, g2 = [m.reshape(bsz, 1, d) for m in jnp.split(mods, N_MOD, axis=-1)]

    lr0 = d // 2 * 2 + 2 * d
    z = _in_projection(x, sc1, sh1, w_in, lr0, w_gla_gate, b_gla_gate)

    y_gla = _gla(z["q_in"], z["k_in"], z["q_st"], z["k_st"], z["dec"], z["v_gla"], z["r_gla"], gla_norm)

    o_groups, lse_groups = [], []
    for g, (window, dilation) in enumerate(DIL_PATTERNS):
        l = s // dilation
        qg, kg, vg = (z[f"{n}{g}"].reshape(bsz * dilation, l, DIL_GROUP_WIDTH) for n in ("q_att", "k_att", "v_att"))
        table = rel_bias[:, g * DIL_HEADS_PER_GROUP:(g + 1) * DIL_HEADS_PER_GROUP]
        o, lse = _dilated_group_attention(qg, kg, vg, table, window, dilation)
        o_groups.append(o.reshape(bsz, dilation, l, DIL_GROUP_WIDTH))
        lse_groups.append(lse.reshape(bsz, dilation, l, DIL_GROUP_WIDTH))

    wr = jnp.concatenate([w_rg, w_re, jnp.zeros((d, LANES - MOE_GROUPS - MOE_TOTAL), F32)], axis=1)
    br = jnp.concatenate([b_rg, b_re, jnp.zeros((LANES - MOE_GROUPS - MOE_TOTAL,), F32)]).reshape(1, LANES)
    x1, u2, route, ew, cnt = _merge(alpha, y_gla, o_groups, lse_groups, z["g_gla"], z["g_att"], x, g1, sc2, sh2,
                                    ln1_g, ln1_b, w_proj_gla, w_proj_attn, w_out, wr, br)

    n = bsz * s
    counts = cnt[MOE_GROUPS:MOE_GROUPS + MOE_TOTAL, 0]
    dest0, dest1, tile_run, tile_valid, run_expert, used, n_rows = _dispatch_plan(route, counts)
    xg = _sc_scatter_rows(u2.reshape(n, d // 2), dest0, dest1, n_rows)
    ff = w_eg.shape[-1]
    yo = _expert_ffn(tile_run, tile_valid, run_expert, used, xg, w_eg.reshape(MOE_TOTAL, d, ff),
                     w_eu.reshape(MOE_TOTAL, d, ff), w_ed.reshape(MOE_TOTAL, ff, d))
    ya, yb = (y.reshape(bsz, s, d // 2) for y in _sc_gather_rows(yo, dest0, dest1))
    return _final(alpha, x1, ya, yb, ew, g2, ln2_g, ln2_b)


def kernel(x, c, rel_bias, w_ada, b_ada, w_in, w_gla_gate, b_gla_gate, gla_norm, w_proj_gla, w_proj_attn, w_out,
           ln1_g, ln1_b, w_router_group, b_router_group, w_router_expert, b_router_expert, w_exp_gate, w_exp_up,
           w_exp_down, ln2_g, ln2_b):
    assert w_ada.shape[0] == DEPTH
    return _layer(x, c, rel_bias, w_ada[0], b_ada[0], w_in[0:1], w_gla_gate[0], b_gla_gate[0], gla_norm[0],
                  w_proj_gla[0], w_proj_attn[0], w_out[0], ln1_g[0], ln1_b[0], w_router_group[0],
                  b_router_group[0], w_router_expert[0], b_router_expert[0], w_exp_gate[0], w_exp_up[0],
                  w_exp_down[0], ln2_g[0], ln2_b[0])
```

```python
import functools
import math

import numpy as np
import jax
import jax.numpy as jnp
from jax import lax
from jax.experimental import pallas as pl
from jax.experimental.pallas import tpu as pltpu
from jax.experimental.pallas import tpu_sc as plsc

F32 = jnp.float32
BF16 = jnp.bfloat16

N_MOD = 6
GLA_HEADS = 4
GLA_LOWRANK = 16
GLA_TAU = 16.0
GLA_CHUNK = 64
DIL_PATTERNS = ((128, 1), (512, 4), (2048, 16))
DIL_GROUPS = len(DIL_PATTERNS)
DIL_HEADS_PER_GROUP = 8
DIL_HEAD_DIM = 64
DIL_GROUP_WIDTH = DIL_HEADS_PER_GROUP * DIL_HEAD_DIM
DIL_BLOCK = 128
REL_BUCKETS = 32
REL_MAX_DIST = 2048
MOE_GROUPS = 4
MOE_EXPERTS = 8
MOE_TOTAL = MOE_GROUPS * MOE_EXPERTS
LN_EPS = 1e-5
DEPTH = 1

LANES = 128
VMEM_LIMIT = 56 * 1024 * 1024
LOG2E = 1.4426950408889634
LN2 = 0.6931471805599453
NEG = -1e30
ROW_TILE = 512
PROJ_CHUNK = 512
FINAL_TILE = 1024
EXPERT_TILE = 512
EXPERT_BLOCK = 128
GLA_STEP_CHUNKS = 16
ATT_STEP_BLOCKS = 8
ROUTER_ROWS = 40
ROUTE_ROWS = 8

NT_DIMS = (((1,), (1,)), ((), ()))
TN_DIMS = (((0,), (0,)), ((), ()))


def _cparams(sem):
    return pltpu.CompilerParams(dimension_semantics=sem, vmem_limit_bytes=VMEM_LIMIT)


def _sigmoid(x):
    return 0.5 * jnp.tanh(0.5 * x) + 0.5


def _silu(x):
    return x * _sigmoid(x)


def _layer_norm(x, g, b):
    mu = jnp.mean(x, axis=-1, keepdims=True)
    xc = x - mu
    var = jnp.mean(xc * xc, axis=-1, keepdims=True)
    return xc * lax.rsqrt(var + LN_EPS) * g + b


def _pack_bf16_pairs(x):
    w = x.shape[1] // 2
    lo = lax.bitcast_convert_type(x[:, :w].astype(BF16).astype(F32), jnp.uint32) >> 16
    hi = lax.bitcast_convert_type(x[:, w:].astype(BF16).astype(F32), jnp.uint32) & jnp.uint32(0xFFFF0000)
    return lax.bitcast_convert_type(lo | hi, jnp.int32)


def _unpack_bf16_pairs(p):
    u = lax.bitcast_convert_type(p, jnp.uint32)
    lo = lax.bitcast_convert_type(u << 16, F32)
    hi = lax.bitcast_convert_type(u & jnp.uint32(0xFFFF0000), F32)
    return jnp.concatenate([lo, hi], axis=1)


def _mods_kernel(ct_ref, w_ref, b_ref, o_ref):
    a = _silu(ct_ref[...])
    w = w_ref[...]
    for b in range(a.shape[1]):
        o_ref[b:b + 1, :] = jnp.sum(a[:, b:b + 1] * w, axis=0, keepdims=True) + b_ref[...]


def _ada_mods(c, w, b):
    bsz, d = c.shape
    n = w.shape[1]
    tn = 1536
    assert n % tn == 0
    return pl.pallas_call(
        _mods_kernel,
        out_shape=jax.ShapeDtypeStruct((bsz, n), F32),
        grid=(n // tn,),
        in_specs=[pl.BlockSpec((d, bsz), lambda j: (0, 0)),
                  pl.BlockSpec((d, tn), lambda j: (0, j)),
                  pl.BlockSpec((1, tn), lambda j: (0, j))],
        out_specs=pl.BlockSpec((bsz, tn), lambda j: (0, j)),
        compiler_params=_cparams(("arbitrary",)),
        name="ada_mods",
    )(c.T, w, b.reshape(1, n))


def _proj_pieces(d_model):
    dk = d_model // 2
    pieces = [("q_gla", dk, "scale_q_gla"), ("k_gla", dk, None), ("v_gla", d_model, None), ("r_gla", d_model, "silu")]
    for name, post in (("q_att", "scale_q_att"), ("k_att", None), ("v_att", None)):
        for g, (_, dilation) in enumerate(DIL_PATTERNS):
            pieces.append((f"{name}{g}", DIL_GROUP_WIDTH, (post, dilation)))
    pieces += [("g_gla", d_model, "sigmoid"), ("g_att", d_model, "sigmoid"), ("lr", LANES, "lowrank")]
    return tuple(pieces)


WT_BLOCK = 512


def _stage_pitch(dilation):
    return dilation + 8 if dilation % 16 == 0 else dilation


def _stage_rows(tm, dilations):
    return max(tm // d * _stage_pitch(d) for d in dilations)


def _load_in_weight(lr0, wt_hbm, w_ref, stage_ref, sem):
    n_main = wt_hbm.shape[1] - GLA_LOWRANK
    n_blocks = n_main // WT_BLOCK + 1
    src_rows = [j * WT_BLOCK + (GLA_LOWRANK if j * WT_BLOCK >= lr0 else 0) for j in range(n_blocks - 1)] + [lr0]

    def copy(j):
        return pltpu.make_async_copy(wt_hbm.at[0, pl.ds(src_rows[j], WT_BLOCK), :], stage_ref.at[j % 2], sem.at[j % 2])

    copy(0).start()
    for j in range(n_blocks):
        if j + 1 < n_blocks:
            copy(j + 1).start()
        copy(j).wait()
        blk = stage_ref[j % 2]
        if j == n_blocks - 1:
            blk = jnp.where(lax.broadcasted_iota(jnp.int32, blk.shape, 0) < GLA_LOWRANK, blk, 0.0)
        w_ref[j * WT_BLOCK:(j + 1) * WT_BLOCK, :] = blk.astype(BF16)


GLA_HELD = ("lr", "q_gla", "k_gla")


def _gla_operands(hold, wg_ref, bg_ref, qin_ref, kin_ref, qst_ref, kst_ref, dec_ref):
    c = GLA_CHUNK
    tm = hold["q_gla"].shape[0]
    tril = (lax.broadcasted_iota(jnp.int32, (c, c), 0) >= lax.broadcasted_iota(jnp.int32, (c, c), 1)).astype(BF16)
    mid = c // 2 - 1
    lr, wg = hold["lr"][:, 0:GLA_LOWRANK], wg_ref[...]
    lr_hi, wg_hi = lr.astype(BF16), wg.astype(BF16)
    lr_lo, wg_lo = (lr - lr_hi.astype(F32)).astype(BF16), (wg - wg_hi.astype(F32)).astype(BF16)
    gate_in = (jnp.dot(lr_hi, wg_hi, preferred_element_type=F32) + jnp.dot(lr_lo, wg_hi, preferred_element_type=F32)
               + jnp.dot(lr_hi, wg_lo, preferred_element_type=F32)) + bg_ref[...]
    g_all = (jnp.minimum(gate_in, 0.0) - jnp.log(1.0 + jnp.exp(-jnp.abs(gate_in)))) * (1.0 / GLA_TAU)
    g_hi = g_all.astype(BF16)
    g_lo = (g_all - g_hi.astype(F32)).astype(BF16)
    for ci in range(tm // c):
        rows = slice(ci * c, (ci + 1) * c)
        bc = jnp.dot(tril, g_hi[rows], preferred_element_type=F32) + jnp.dot(tril, g_lo[rows], preferred_element_type=F32)
        b_mid = bc[mid:mid + 1, :]
        b_last = bc[c - 1:c, :]
        qf = hold["q_gla"][rows, :]
        kf = hold["k_gla"][rows, :]
        q_in = qf * jnp.exp(bc - b_mid)
        k_in = kf * jnp.exp(b_mid - bc)
        qin_ref[0, rows, :] = q_in.astype(BF16)
        kin_ref[0, rows, :] = k_in.astype(BF16)
        qst_ref[0, rows, :] = (q_in * jnp.exp(b_mid)).astype(BF16)
        kst_ref[0, rows, :] = (k_in * jnp.exp(b_last - b_mid)).astype(BF16)
        dec_ref[0, ci:ci + 1, :] = jnp.exp(b_last)


def _proj_kernel(pieces, head_k, lr0, x_ref, sc_ref, sh_ref, wt_hbm, wg_ref, bg_ref, *refs):
    n_out = len(pieces) - len(GLA_HELD)
    out_refs = dict(zip([p[0] for p in pieces if p[0] not in GLA_HELD], refs[:n_out]))
    gla_out_refs = refs[n_out:n_out + 5]
    stage_ref = refs[n_out + 5]
    hold = dict(zip(GLA_HELD, refs[n_out + 6:n_out + 9]))
    w_ref, wstage_ref, wsem = refs[n_out + 9:]
    tm = x_ref.shape[1]

    @pl.when((pl.program_id(0) == 0) & (pl.program_id(1) == 0))
    def _():
        _load_in_weight(lr0, wt_hbm, w_ref, wstage_ref, wsem)

    u = (x_ref[0] * (1.0 + sc_ref[0]) + sh_ref[0]).astype(BF16)
    offsets, off = {}, 0
    for name, width, _ in pieces:
        offsets[name] = off
        off += width
    by_name = {p[0]: p for p in pieces}
    held = [(by_name[n], 0) for n in GLA_HELD]
    rest = [(p, c0) for p in pieces if p[0] not in GLA_HELD for c0 in range(0, p[1], min(p[1], PROJ_CHUNK))]
    for n, (piece, c0) in enumerate(held + rest):
        if n == len(held):
            _gla_operands(hold, wg_ref, bg_ref, *gla_out_refs)
        name, width, post = piece
        o_ref = hold[name] if name in GLA_HELD else out_refs[name]
        off = offsets[name]
        chunk = min(width, PROJ_CHUNK)
        acc = lax.dot_general(u, w_ref[off + c0:off + c0 + chunk, :], NT_DIMS, preferred_element_type=F32)
        if post == "silu":
            acc = _silu(acc)
        elif post == "sigmoid":
            acc = _sigmoid(acc)
        elif post == "scale_q_gla":
            acc = acc * (head_k ** -0.5)
        if name in GLA_HELD:
            o_ref[...] = acc
        elif isinstance(post, tuple):
            scale, dilation = post
            if scale is not None:
                acc = acc * (DIL_HEAD_DIM ** -0.5 * LOG2E)
            if dilation == 1:
                o_ref[0, 0] = acc.astype(o_ref.dtype)
            else:
                pitch = _stage_pitch(dilation)
                for t in range(width // LANES):
                    if pitch == dilation:
                        stage_ref[t, 0:tm, :] = acc[:, t * LANES:(t + 1) * LANES]
                    else:
                        for j in range(tm // dilation):
                            stage_ref[t, j * pitch:j * pitch + dilation, :] = acc[
                                j * dilation:(j + 1) * dilation, t * LANES:(t + 1) * LANES]
                for r in range(dilation):
                    for t in range(width // LANES):
                        o_ref[0, r, :, t * LANES:(t + 1) * LANES] = stage_ref[
                            t, pl.ds(r, tm // dilation, stride=pitch), :].astype(o_ref.dtype)
        else:
            o_ref[0, :, c0:c0 + chunk] = acc.astype(o_ref.dtype)


def _in_projection(x, sc1, sh1, w_in, lr0, w_gate, b_gate):
    bsz, s, d = x.shape
    pieces = _proj_pieces(d)
    w_t = jnp.swapaxes(w_in, 1, 2)
    n_main = w_t.shape[1] - GLA_LOWRANK
    assert lr0 % WT_BLOCK == 0 and n_main % WT_BLOCK == 0
    w_rows = n_main + WT_BLOCK
    assert sum(p[1] for p in pieces) <= w_rows
    tm = min(ROW_TILE, s)
    assert s % tm == 0 and tm % (8 * GLA_CHUNK) == 0
    dk = d // 2
    head_k = dk // GLA_HEADS
    out_shape, out_specs = [], []
    for name, width, post in pieces:
        if name in GLA_HELD:
            continue
        if isinstance(post, tuple):
            dil = post[1]
            assert tm % (dil * 16) == 0
            out_shape.append(jax.ShapeDtypeStruct((bsz, dil, s // dil, width), BF16))
            out_specs.append(pl.BlockSpec((1, dil, tm // dil, width), lambda b, i: (b, 0, i, 0)))
        else:
            out_shape.append(jax.ShapeDtypeStruct((bsz, s, width), BF16))
            out_specs.append(pl.BlockSpec((1, tm, width), lambda b, i: (b, i, 0)))
    row = lambda w: pl.BlockSpec((1, tm, w), lambda b, i: (b, i, 0))
    gla_names = ("q_in", "k_in", "q_st", "k_st", "dec")
    out_shape += [jax.ShapeDtypeStruct((bsz, s, dk), BF16)] * 4 + [jax.ShapeDtypeStruct((bsz, s // GLA_CHUNK, dk), F32)]
    out_specs += [row(dk)] * 4 + [pl.BlockSpec((1, tm // GLA_CHUNK, dk), lambda b, i: (b, i, 0))]
    bg = b_gate.reshape(1, dk)
    full = lambda a: pl.BlockSpec(a.shape, lambda b, i: (0,) * a.ndim)
    outs = pl.pallas_call(
        functools.partial(_proj_kernel, pieces, head_k, lr0),
        out_shape=out_shape,
        grid=(bsz, s // tm),
        in_specs=[row(d),
                  pl.BlockSpec((1, 1, d), lambda b, i: (b, 0, 0)),
                  pl.BlockSpec((1, 1, d), lambda b, i: (b, 0, 0)),
                  pl.BlockSpec(memory_space=pl.ANY),
                  full(w_gate), full(bg)],
        out_specs=out_specs,
        scratch_shapes=[pltpu.VMEM((DIL_GROUP_WIDTH // LANES, _stage_rows(tm, [dl for _, dl in DIL_PATTERNS]), LANES),
                                   F32),
                        pltpu.VMEM((tm, LANES), F32), pltpu.VMEM((tm, dk), F32), pltpu.VMEM((tm, dk), F32),
                        pltpu.VMEM((w_rows, d), BF16), pltpu.VMEM((2, WT_BLOCK, d), F32),
                        pltpu.SemaphoreType.DMA((2,))],
        compiler_params=_cparams(("arbitrary", "arbitrary")),
        name="in_projection",
    )(x, sc1, sh1, w_t, w_gate, bg)
    return dict(zip([p[0] for p in pieces if p[0] not in GLA_HELD] + list(gla_names), outs))


def _gla_kernel(n_chunks, head_k, head_v, qin_ref, kin_ref, qst_ref, kst_ref, dec_ref, v_ref, r_ref, ng_ref, o_ref,
                state_ref):
    @pl.when(pl.program_id(1) == 0)
    def _():
        state_ref[...] = jnp.zeros_like(state_ref)

    c = GLA_CHUNK
    causal = lax.broadcasted_iota(jnp.int32, (c, c), 0) >= lax.broadcasted_iota(jnp.int32, (c, c), 1)
    for ci in range(n_chunks):
        rows = slice(ci * c, (ci + 1) * c)
        for h in range(GLA_HEADS):
            ks = slice(h * head_k, (h + 1) * head_k)
            vs = slice(h * head_v, (h + 1) * head_v)
            vh = v_ref[0, rows, vs]
            att = lax.dot_general(qin_ref[0, rows, ks], kin_ref[0, rows, ks], NT_DIMS, preferred_element_type=F32)
            att = jnp.where(causal, att, 0.0).astype(BF16)
            st = state_ref[h]
            o = jnp.dot(att, vh, preferred_element_type=F32)
            o = o + lax.dot_general(qst_ref[0, rows, ks], st.astype(BF16), NT_DIMS, preferred_element_type=F32)
            kv_t = lax.dot_general(vh, kst_ref[0, rows, ks], TN_DIMS, preferred_element_type=F32)
            state_ref[h] = st * dec_ref[0, ci:ci + 1, ks] + kv_t
            ms = jnp.mean(o * o, axis=-1, keepdims=True)
            o = o * lax.rsqrt(ms + LN_EPS) * ng_ref[:, vs] * r_ref[0, rows, vs].astype(F32)
            o_ref[0, rows, vs] = o.astype(o_ref.dtype)


def _gla(q_in, k_in, q_st, k_st, dec, v, r_silu, norm_g):
    bsz, s, dk = q_in.shape
    dv = v.shape[-1]
    head_k, head_v = dk // GLA_HEADS, dv // GLA_HEADS
    n_chunks = min(GLA_STEP_CHUNKS, s // GLA_CHUNK)
    ct = GLA_CHUNK * n_chunks
    assert s % ct == 0
    row_spec = lambda w: pl.BlockSpec((1, ct, w), lambda b, i: (b, i, 0))
    full = lambda a: pl.BlockSpec(a.shape, lambda b, i: (0,) * a.ndim)
    ng = norm_g.reshape(1, dv)
    return pl.pallas_call(
        functools.partial(_gla_kernel, n_chunks, head_k, head_v),
        out_shape=jax.ShapeDtypeStruct((bsz, s, dv), BF16),
        grid=(bsz, s // ct),
        in_specs=[row_spec(dk)] * 4 + [pl.BlockSpec((1, n_chunks, dk), lambda b, i: (b, i, 0)),
                                       row_spec(dv), row_spec(dv), full(ng)],
        out_specs=row_spec(dv),
        scratch_shapes=[pltpu.VMEM((GLA_HEADS, head_v, head_k), F32)],
        compiler_params=_cparams(("parallel", "arbitrary")),
        name="gla",
    )(q_in, k_in, q_st, k_st, dec, v, r_silu, ng)


def _t5_bucket_np(dist):
    exact = REL_BUCKETS // 2
    d = np.maximum(dist, 1).astype(np.float32)
    large = exact + (np.log(d / np.float32(exact)) / np.float32(math.log(REL_MAX_DIST / exact))
                     * np.float32(REL_BUCKETS - exact)).astype(np.int32)
    large = np.minimum(large, REL_BUCKETS - 1)
    return np.where(dist < exact, dist, large).astype(np.int32)


def _band_tables(window, dilation):
    qi = np.arange(DIL_BLOCK)[:, None]
    kj = np.arange(2 * DIL_BLOCK)[None, :]
    m = qi + DIL_BLOCK - kj
    n_steps = window // dilation
    band = (m >= 0) & (m <= n_steps)
    bucket = _t5_bucket_np(np.clip(m, 0, n_steps) * dilation)
    return np.where(band, bucket, -1).astype(np.int32)


def _attn_kernel(nq, table_ref, bucket_ref, q_ref, kp_ref, kc_ref, vp_ref, vc_ref, o_ref, lse_ref,
                 bias_ref, p_ref):
    i = pl.program_id(1)
    blk = DIL_BLOCK
    hpg = DIL_HEADS_PER_GROUP
    n_pairs = hpg // 2

    @pl.when((pl.program_id(0) == 0) & (i == 0))
    def _():
        bucket = bucket_ref[...]
        for h in range(hpg):
            acc = jnp.full(bucket.shape, NEG, F32)
            for bkt in range(REL_BUCKETS):
                acc = jnp.where(bucket == bkt, table_ref[bkt, h] * LOG2E, acc)
            bias_ref[h * blk:(h + 1) * blk, :] = acc

    lane = lax.broadcasted_iota(jnp.int32, (blk, LANES), 1)
    low = lane < DIL_HEAD_DIM
    ones_rhs = jnp.ones((2 * blk, LANES), BF16)

    def windows(ref_p, ref_c, sq, qb, cols):
        if qb == 0:
            return jnp.concatenate([ref_p[sq, :, cols], ref_c[sq, 0:blk, cols]], axis=0)
        return ref_c[sq, (qb - 1) * blk:(qb + 1) * blk, cols]

    key_lane = lax.broadcasted_iota(jnp.int32, (1, 2 * blk), 1)
    no_prev = jnp.where((key_lane < blk) & (i == 0), NEG, 0.0)
    items = [(sq, qb, hp) for sq in range(q_ref.shape[0]) for qb in range(nq) for hp in range(n_pairs)]

    mxs = []
    for n, (sq, qb, hp) in enumerate(items):
        rows = slice(qb * blk, (qb + 1) * blk)
        cols = slice(hp * LANES, (hp + 1) * LANES)
        qp = q_ref[sq, rows, cols]
        zero = jnp.zeros_like(qp)
        qq = jnp.concatenate([jnp.where(low, qp, zero), jnp.where(low, zero, qp)], axis=0)
        keys = windows(kp_ref, kc_ref, sq, qb, cols)
        s = lax.dot_general(qq, keys, NT_DIMS, preferred_element_type=F32) + bias_ref[2 * hp * blk:(2 * hp + 2) * blk, :]
        if qb == 0:
            s = s + no_prev
        mx = jnp.max(s, axis=-1, keepdims=True)
        p_ref[n * 2 * blk:(n + 1) * 2 * blk, :] = jnp.exp2(s - mx).astype(BF16)
        mxs.append(mx)

    for n, (sq, qb, hp) in enumerate(items):
        rows = slice(qb * blk, (qb + 1) * blk)
        cols = slice(hp * LANES, (hp + 1) * LANES)
        vals = windows(vp_ref, vc_ref, sq, qb, cols)
        rhs = jnp.concatenate([vals, ones_rhs], axis=1)
        res = jnp.dot(p_ref[n * 2 * blk:(n + 1) * 2 * blk, :], rhs, preferred_element_type=F32)
        num = jnp.where(low, res[0:blk, 0:LANES], res[blk:2 * blk, 0:LANES])
        den = jnp.where(low, res[0:blk, LANES:], res[blk:2 * blk, LANES:])
        mx = jnp.where(low, mxs[n][0:blk], mxs[n][blk:2 * blk])
        o_ref[sq, rows, cols] = (num / den).astype(o_ref.dtype)
        lse_ref[sq, rows, cols] = (mx + jnp.log2(den)) * LN2


def _dilated_group_attention(q, k, v, table, window, dilation):
    bb, l, w = q.shape
    nq = min(ATT_STEP_BLOCKS, l // DIL_BLOCK)
    nsq = ATT_STEP_BLOCKS // nq
    assert l % (nq * DIL_BLOCK) == 0 and bb % nsq == 0
    steps = l // (nq * DIL_BLOCK)
    bucket = jnp.asarray(_band_tables(window, dilation))
    cur = pl.BlockSpec((nsq, nq * DIL_BLOCK, w), lambda b, i: (b, i, 0))
    prev = pl.BlockSpec((nsq, DIL_BLOCK, w), lambda b, i: (b, jnp.maximum(nq * i - 1, 0), 0))
    rows_all = nsq * nq * DIL_HEADS_PER_GROUP * DIL_BLOCK
    return pl.pallas_call(
        functools.partial(_attn_kernel, nq),
        out_shape=[jax.ShapeDtypeStruct((bb, l, w), BF16), jax.ShapeDtypeStruct((bb, l, w), F32)],
        grid=(bb // nsq, steps),
        in_specs=[pl.BlockSpec(memory_space=pltpu.SMEM),
                  pl.BlockSpec(bucket.shape, lambda b, i: (0, 0)),
                  cur, prev, cur, prev, cur],
        out_specs=[cur, cur],
        scratch_shapes=[pltpu.VMEM((DIL_HEADS_PER_GROUP * DIL_BLOCK, 2 * DIL_BLOCK), F32),
                        pltpu.VMEM((rows_all, 2 * DIL_BLOCK), BF16)],
        compiler_params=_cparams(("arbitrary", "arbitrary")),
        name=f"dilated_attn_d{dilation}",
    )(table, bucket, q, k, k, v, v)


def _merge_kernel(alpha, dilations, ygla_ref, o0_ref, o1_ref, o2_ref, l0_ref, l1_ref, l2_ref, gg_ref, ga_ref, x_ref,
                  g1_ref, sc2_ref, sh2_ref, ln_g_ref, ln_b_ref, wpg_ref, wpa_ref, wout_ref, wr_ref, br_ref, utri_ref,
                  x1_ref, u2_ref, route_ref, ew_ref, cnt_ref, stage_ref, carry_ref):
    tm = x_ref.shape[1]

    @pl.when((pl.program_id(0) == 0) & (pl.program_id(1) == 0))
    def _():
        carry_ref[...] = jnp.zeros_like(carry_ref)

    n_lt = DIL_GROUP_WIDTH // LANES
    group_refs = tuple(zip((l0_ref, l1_ref, l2_ref), (o0_ref, o1_ref, o2_ref), dilations))
    for gi, (l_ref, o_ref, dil) in enumerate(group_refs):
        if dil > 1:
            for slot, ref in ((2 * gi, l_ref), (2 * gi + 1, o_ref)):
                for r in range(dil):
                    for t in range(n_lt):
                        stage_ref[slot, t, pl.ds(r, tm // dil, stride=_stage_pitch(dil)), :] = ref[
                            0, r, :, t * LANES:(t + 1) * LANES].astype(F32)

    def natural(ref, dil, slot):
        if dil == 1:
            return ref[0, 0].astype(F32)
        pitch = _stage_pitch(dil)
        if pitch == dil:
            return jnp.concatenate([stage_ref[slot, t, 0:tm, :] for t in range(n_lt)], axis=1)
        return jnp.concatenate(
            [jnp.concatenate([stage_ref[slot, t, j * pitch:j * pitch + dil, :] for j in range(tm // dil)], axis=0)
             for t in range(n_lt)], axis=1)

    lses = [natural(l_ref, dil, 2 * gi) for gi, (l_ref, _, dil) in enumerate(group_refs)]
    outs = [natural(o_ref, dil, 2 * gi + 1) for gi, (_, o_ref, dil) in enumerate(group_refs)]
    lm = jnp.maximum(jnp.maximum(lses[0], lses[1]), lses[2])
    es = [jnp.exp(l - lm) for l in lses]
    y_att = (es[0] * outs[0] + es[1] * outs[1] + es[2] * outs[2]) / (es[0] + es[1] + es[2])

    p_gla = jnp.dot(ygla_ref[0], wpg_ref[...], preferred_element_type=F32)
    p_att = jnp.dot(y_att.astype(BF16), wpa_ref[...], preferred_element_type=F32)
    merged = gg_ref[0].astype(F32) * p_gla + ga_ref[0].astype(F32) * p_att
    y = jnp.dot(merged.astype(BF16), wout_ref[...], preferred_element_type=F32)
    x1 = _layer_norm(alpha * x_ref[0] + g1_ref[0] * y, ln_g_ref[...], ln_b_ref[...])
    x1_ref[0] = x1
    u2 = x1 * (1.0 + sc2_ref[0]) + sh2_ref[0]
    u2_ref[0] = _pack_bf16_pairs(u2)

    logits = jnp.dot(u2.astype(BF16), wr_ref[...].astype(BF16), preferred_element_type=F32) + br_ref[...]
    lt = jnp.transpose(logits)[0:ROUTER_ROWS, :]
    rowi = lax.broadcasted_iota(jnp.int32, lt.shape, 0)
    big = jnp.int32(LANES)
    lg = jnp.where(rowi < MOE_GROUPS, lt, NEG)
    gmax = jnp.max(lg, axis=0, keepdims=True)
    gidx = jnp.min(jnp.where(lg == gmax, rowi, big), axis=0, keepdims=True)
    gval = 1.0 / jnp.sum(jnp.exp(lg - gmax), axis=0, keepdims=True)
    first = MOE_GROUPS + gidx * MOE_EXPERTS
    le = jnp.where((rowi >= first) & (rowi < first + MOE_EXPERTS), lt, NEG)
    m1 = jnp.max(le, axis=0, keepdims=True)
    i1 = jnp.min(jnp.where(le == m1, rowi, big), axis=0, keepdims=True)
    le2 = jnp.where(rowi == i1, NEG, le)
    m2 = jnp.max(le2, axis=0, keepdims=True)
    i2 = jnp.min(jnp.where(le2 == m2, rowi, big), axis=0, keepdims=True)
    t = jnp.exp(m2 - m1)
    w1 = 1.0 / (1.0 + t)
    w2 = t * w1

    hit1, hit2 = rowi == i1, rowi == i2
    onehot = jnp.where(hit1 | hit2, 1.0, 0.0)
    earlier = jnp.dot(onehot.astype(BF16), utri_ref[...], preferred_element_type=F32) + carry_ref[...]
    rank1 = jnp.sum(jnp.where(hit1, earlier, 0.0), axis=0, keepdims=True).astype(jnp.int32)
    rank2 = jnp.sum(jnp.where(hit2, earlier, 0.0), axis=0, keepdims=True).astype(jnp.int32)
    carry = carry_ref[...] + jnp.sum(onehot, axis=1, keepdims=True)
    carry_ref[...] = carry
    cnt_ref[...] = jnp.broadcast_to(carry, cnt_ref.shape).astype(jnp.int32)
    r8 = lax.broadcasted_iota(jnp.int32, (ROUTE_ROWS, tm), 0)
    route_ref[0] = jnp.where(r8 == 0, i1 - MOE_GROUPS, jnp.where(r8 == 1, i2 - MOE_GROUPS,
                             jnp.where(r8 == 2, rank1, jnp.where(r8 == 3, rank2, 0))))
    r128 = lax.broadcasted_iota(jnp.int32, (LANES, tm), 0)
    ew_ref[0] = jnp.transpose(jnp.where(r128 == 0, gval * w1, jnp.where(r128 == 1, gval * w2, 0.0)))


def _merge(alpha, y_gla, o_groups, lse_groups, g_gla, g_att, x, g1, sc2, sh2, ln_g, ln_b, wpg, wpa, wout, wr, br):
    bsz, s, d = x.shape
    tm = min(ROW_TILE, s)
    assert s % tm == 0
    dilations = tuple(dil for _, dil in DIL_PATTERNS)
    row = lambda w: pl.BlockSpec((1, tm, w), lambda b, i: (b, i, 0))
    sub = lambda dil: pl.BlockSpec((1, dil, tm // dil, DIL_GROUP_WIDTH), lambda b, i: (b, 0, i, 0))
    per_b = pl.BlockSpec((1, 1, d), lambda b, i: (b, 0, 0))
    full = lambda a: pl.BlockSpec(a.shape, lambda b, i: (0,) * a.ndim)
    ln_g2, ln_b2 = ln_g.reshape(1, d), ln_b.reshape(1, d)
    utri = jnp.asarray(np.triu(np.ones((tm, tm), np.float32), 1), BF16)
    return pl.pallas_call(
        functools.partial(_merge_kernel, alpha, dilations),
        out_shape=[jax.ShapeDtypeStruct((bsz, s, d), F32), jax.ShapeDtypeStruct((bsz, s, d // 2), jnp.int32),
                   jax.ShapeDtypeStruct((bsz, ROUTE_ROWS, s), jnp.int32), jax.ShapeDtypeStruct((bsz, s, LANES), F32),
                   jax.ShapeDtypeStruct((ROUTER_ROWS, LANES), jnp.int32)],
        grid=(bsz, s // tm),
        in_specs=[row(y_gla.shape[-1])] + [sub(dil) for dil in dilations] * 2
                 + [row(d), row(d), row(d), per_b, per_b, per_b, full(ln_g2), full(ln_b2),
                    full(wpg), full(wpa), full(wout), full(wr), full(br), full(utri)],
        out_specs=[row(d), row(d // 2), pl.BlockSpec((1, ROUTE_ROWS, tm), lambda b, i: (b, 0, i)), row(LANES),
                   pl.BlockSpec((ROUTER_ROWS, LANES), lambda b, i: (0, 0))],
        scratch_shapes=[pltpu.VMEM((2 * DIL_GROUPS, DIL_GROUP_WIDTH // LANES, _stage_rows(tm, dilations), LANES), F32),
                        pltpu.VMEM((ROUTER_ROWS, 1), F32)],
        compiler_params=_cparams(("arbitrary", "arbitrary")),
        name="merge_ln1_router",
    )(y_gla, *o_groups, *lse_groups, g_gla, g_att, x, g1, sc2, sh2, ln_g2, ln_b2, wpg, wpa, wout, wr, br, utri)


def _expert_kernel(run_ref, valid_ref, rexp_ref, used_ref, x_ref, wg_hbm, wu_hbm, wd_hbm, o_ref,
                   wg_f, wu_f, wd_f, wg_s, wu_s, wd_s, sem):
    t = pl.program_id(0)
    n_tiles_used, n_runs = used_ref[0], used_ref[1]
    run = run_ref[t]
    active = t < n_tiles_used
    first_of_run = (t == 0) | (run_ref[jnp.maximum(t - 1, 0)] != run)

    def weight_copies(r):
        e, slot = rexp_ref[r], r % 2
        return [pltpu.make_async_copy(hbm.at[e], buf.at[slot], sem.at[slot, j])
                for j, (hbm, buf) in enumerate(((wg_hbm, wg_f), (wu_hbm, wu_f), (wd_hbm, wd_f)))]

    @pl.when(active & (t == 0))
    def _():
        for cp in weight_copies(0):
            cp.start()

    @pl.when(active & first_of_run)
    def _():
        @pl.when(run + 1 < n_runs)
        def _():
            for cp in weight_copies(run + 1):
                cp.start()

        for cp in weight_copies(run):
            cp.wait()
        slot = run % 2
        wg_s[...] = wg_f[slot].astype(BF16)
        wu_s[...] = wu_f[slot].astype(BF16)
        wd_s[...] = wd_f[slot].astype(BF16)

    n_valid = jnp.where(active, valid_ref[t], 0)

    def ffn(rows):
        xt = _unpack_bf16_pairs(x_ref[rows, :]).astype(BF16)
        hg = jnp.dot(xt, wg_s[...], preferred_element_type=F32)
        hu = jnp.dot(xt, wu_s[...], preferred_element_type=F32)
        h = (_silu(hg) * hu).astype(BF16)
        o_ref[rows, :] = _pack_bf16_pairs(jnp.dot(h, wd_s[...], preferred_element_type=F32))

    def zero(rows):
        o_ref[rows, :] = jnp.zeros((rows.stop - rows.start, o_ref.shape[1]), o_ref.dtype)

    tm = x_ref.shape[0]
    n_blocks = tm // EXPERT_BLOCK
    for k in range(n_blocks + 1):
        lo, hi = (k - 1) * EXPERT_BLOCK, k * EXPERT_BLOCK

        @pl.when((n_valid > lo) & (n_valid <= hi) if 0 < k < n_blocks else (n_valid > lo if k else n_valid <= 0))
        def _():
            if k:
                ffn(slice(0, hi))
            if k < n_blocks:
                zero(slice(hi, tm))


def _expert_ffn(tile_run, tile_valid, run_expert, used, xg, w_gate, w_up, w_down):
    p = xg.shape[0]
    ne, d, ff = w_gate.shape
    tm = EXPERT_TILE
    n_tiles = p // tm
    hbm = pl.BlockSpec(memory_space=pl.ANY)

    def tile(t, run, valid, rexp, used):
        return jnp.where(t < used[0], t, n_tiles - 1), 0

    grid_spec = pltpu.PrefetchScalarGridSpec(
        num_scalar_prefetch=4,
        grid=(n_tiles,),
        in_specs=[pl.BlockSpec((tm, d // 2), tile), hbm, hbm, hbm],
        out_specs=pl.BlockSpec((tm, d // 2), tile),
        scratch_shapes=[pltpu.VMEM((2, d, ff), F32), pltpu.VMEM((2, d, ff), F32), pltpu.VMEM((2, ff, d), F32),
                        pltpu.VMEM((d, ff), BF16), pltpu.VMEM((d, ff), BF16), pltpu.VMEM((ff, d), BF16),
                        pltpu.SemaphoreType.DMA((2, 3))],
    )
    return pl.pallas_call(
        _expert_kernel,
        out_shape=jax.ShapeDtypeStruct((p, d // 2), jnp.int32),
        grid_spec=grid_spec,
        compiler_params=_cparams(("arbitrary",)),
        name="expert_ffn",
    )(tile_run, tile_valid, run_expert, used, xg, w_gate, w_up, w_down)


def _final_kernel(alpha, x1_ref, ya_ref, yb_ref, ew_ref, g2_ref, ln_g_ref, ln_b_ref, o_ref):
    ew = ew_ref[0]
    y = ew[:, 0:1] * _unpack_bf16_pairs(ya_ref[0]) + ew[:, 1:2] * _unpack_bf16_pairs(yb_ref[0])
    o_ref[0] = _layer_norm(alpha * x1_ref[0] + g2_ref[0] * y, ln_g_ref[...], ln_b_ref[...])


def _final(alpha, x1, ya, yb, ew, g2, ln_g, ln_b):
    bsz, s, d = x1.shape
    tm = min(FINAL_TILE, s)
    row = lambda w: pl.BlockSpec((1, tm, w), lambda b, i: (b, i, 0))
    full = lambda a: pl.BlockSpec(a.shape, lambda b, i: (0,) * a.ndim)
    ln_g2, ln_b2 = ln_g.reshape(1, d), ln_b.reshape(1, d)
    return pl.pallas_call(
        functools.partial(_final_kernel, alpha),
        out_shape=jax.ShapeDtypeStruct((bsz, s, d), F32),
        grid=(bsz, s // tm),
        in_specs=[row(d), row(d // 2), row(d // 2), row(LANES), pl.BlockSpec((1, 1, d), lambda b, i: (b, 0, 0)),
                  full(ln_g2), full(ln_b2)],
        out_specs=row(d),
        compiler_params=_cparams(("parallel", "arbitrary")),
        name="combine_ln2",
    )(x1, ya, yb, ew, g2, ln_g2, ln_b2)


SC_CORES = 2
SC_SUBCORES = 16
SC_CHUNK = 64


def _sc_mesh():
    return plsc.VectorSubcoreMesh(core_axis_name="c", subcore_axis_name="s")


def _sc_scatter_rows(rows, dest0, dest1, n_rows):
    n, w = rows.shape
    n_workers = SC_CORES * SC_SUBCORES
    assert n % (n_workers * SC_CHUNK) == 0
    n_chunks = n // (n_workers * SC_CHUNK)
    d0 = dest0.reshape(n // SC_CHUNK, 1, SC_CHUNK)
    d1 = dest1.reshape(n // SC_CHUNK, 1, SC_CHUNK)

    @functools.partial(
        pl.kernel, mesh=_sc_mesh(), out_type=jax.ShapeDtypeStruct((n_rows, w), rows.dtype),
        scratch_types=[pltpu.VMEM((n_chunks, 1, SC_CHUNK), jnp.int32), pltpu.VMEM((n_chunks, 1, SC_CHUNK), jnp.int32),
                       pltpu.VMEM((2, SC_CHUNK, w), rows.dtype),
                       pltpu.SemaphoreType.DMA((2,)), pltpu.SemaphoreType.DMA((2, 2))])
    def scatter_kernel(rows_hbm, d0_hbm, d1_hbm, out_hbm, i0_v, i1_v, rows_v, read_sem, scat_sem):
        wid = lax.axis_index("s") * SC_CORES + lax.axis_index("c")
        first = wid * n_chunks
        pltpu.sync_copy(d0_hbm.at[pl.ds(first, n_chunks)], i0_v)
        pltpu.sync_copy(d1_hbm.at[pl.ds(first, n_chunks)], i1_v)

        def read(j):
            return pltpu.make_async_copy(rows_hbm.at[pl.ds((first + j) * SC_CHUNK, SC_CHUNK)], rows_v.at[j % 2],
                                         read_sem.at[j % 2])

        def scatters(j):
            return [pltpu.make_async_copy(rows_v.at[j % 2], out_hbm.at[idx.at[j].at[0]], scat_sem.at[j % 2, k])
                    for k, idx in enumerate((i0_v, i1_v))]

        read(0).start()
        for j in range(n_chunks):
            read(j).wait()
            if j + 1 < n_chunks:
                if j >= 1:
                    for cp in scatters(j - 1):
                        cp.wait()
                read(j + 1).start()
            for cp in scatters(j):
                cp.start()
        for j in range(max(n_chunks - 2, 0), n_chunks):
            for cp in scatters(j):
                cp.wait()

    return scatter_kernel(rows, d0, d1)


def _sc_gather_rows(table, dest0, dest1):
    n = dest0.shape[0]
    w = table.shape[1]
    n_workers = SC_CORES * SC_SUBCORES
    assert n % (n_workers * SC_CHUNK) == 0
    n_chunks = n // (n_workers * SC_CHUNK)
    d0 = dest0.reshape(n // SC_CHUNK, 1, SC_CHUNK)
    d1 = dest1.reshape(n // SC_CHUNK, 1, SC_CHUNK)
    out = jax.ShapeDtypeStruct((n, w), table.dtype)

    @functools.partial(
        pl.kernel, mesh=_sc_mesh(), out_type=(out, out),
        scratch_types=[pltpu.VMEM((n_chunks, 1, SC_CHUNK), jnp.int32), pltpu.VMEM((n_chunks, 1, SC_CHUNK), jnp.int32),
                       pltpu.VMEM((2, SC_CHUNK, w), table.dtype),
                       pltpu.SemaphoreType.DMA((2,)), pltpu.SemaphoreType.DMA((2,))])
    def gather_kernel(table_hbm, d0_hbm, d1_hbm, a_hbm, b_hbm, i0_v, i1_v, rows_v, gather_sem, write_sem):
        wid = lax.axis_index("s") * SC_CORES + lax.axis_index("c")
        first = wid * n_chunks
        pltpu.sync_copy(d0_hbm.at[pl.ds(first, n_chunks)], i0_v)
        pltpu.sync_copy(d1_hbm.at[pl.ds(first, n_chunks)], i1_v)
        n_items = 2 * n_chunks

        def gather(m):
            idx = (i0_v, i1_v)[m % 2]
            return pltpu.make_async_copy(table_hbm.at[idx.at[m // 2].at[0]], rows_v.at[m % 2], gather_sem.at[m % 2])

        def write(m):
            o_hbm = (a_hbm, b_hbm)[m % 2]
            return pltpu.make_async_copy(rows_v.at[m % 2], o_hbm.at[pl.ds((first + m // 2) * SC_CHUNK, SC_CHUNK)],
                                         write_sem.at[m % 2])

        gather(0).start()
        for m in range(n_items):
            gather(m).wait()
            if m + 1 < n_items:
                if m >= 1:
                    write(m - 1).wait()
                gather(m + 1).start()
            write(m).start()
        for m in range(max(n_items - 2, 0), n_items):
            write(m).wait()

    return gather_kernel(table, d0, d1)


def _dispatch_plan(route, counts):
    tm = EXPERT_TILE
    e0, e1, r0, r1 = (route[:, j, :].reshape(-1) for j in range(4))
    experts = jnp.arange(MOE_TOTAL, dtype=jnp.int32)
    tiles_per = (counts + tm - 1) // tm
    tile_end = jnp.cumsum(tiles_per)
    pad_start = ((tile_end - tiles_per) * tm).astype(jnp.int32)

    def lookup(e):
        return jnp.sum(jnp.where(e[None, :] == experts[:, None], pad_start[:, None], 0), axis=0)

    dest0, dest1 = lookup(e0) + r0, lookup(e1) + r1
    n_tiles = (2 * e0.size + MOE_TOTAL * tm) // tm
    tile_expert = jnp.minimum(jnp.sum(tile_end[None, :] <= jnp.arange(n_tiles)[:, None], axis=1), MOE_TOTAL - 1)
    nonempty = counts > 0
    run_of_expert = jnp.cumsum(nonempty.astype(jnp.int32)) - 1
    run_expert = jnp.sum(jnp.where(nonempty[None, :] & (run_of_expert[None, :] == experts[:, None]),
                                   experts[None, :], 0), axis=1).astype(jnp.int32)
    of_tile = tile_expert[:, None] == experts[None, :]
    tile_run = jnp.sum(jnp.where(of_tile, run_of_expert[None, :], 0), axis=1).astype(jnp.int32)
    rows_left = (counts + pad_start)[None, :] - jnp.arange(n_tiles)[:, None] * tm
    tile_valid = jnp.clip(jnp.sum(jnp.where(of_tile, rows_left, 0), axis=1), 0, tm).astype(jnp.int32)
    used = jnp.stack([tile_end[-1], jnp.sum(nonempty)]).astype(jnp.int32)
    return dest0, dest1, tile_run, tile_valid, run_expert, used, n_tiles * tm


def _layer(x, c, rel_bias, w_ada, b_ada, w_in, w_gla_gate, b_gla_gate, gla_norm, w_proj_gla, w_proj_attn, w_out,
           ln1_g, ln1_b, w_rg, b_rg, w_re, b_re, w_eg, w_eu, w_ed, ln2_g, ln2_b):
    bsz, s, d = x.shape
    alpha = (2.0 * DEPTH) ** 0.25
    mods = _ada_mods(c, w_ada, b_ada)
    sh1, sc1, g1, sh2, sc2, g2 = [m.reshape(bsz, 1, d) for m in jnp.split(mods, N_MOD, axis=-1)]

    lr0 = d // 2 * 2 + 2 * d
    z = _in_projection(x, sc1, sh1, w_in, lr0, w_gla_gate, b_gla_gate)

    y_gla = _gla(z["q_in"], z["k_in"], z["q_st"], z["k_st"], z["dec"], z["v_gla"], z["r_gla"], gla_norm)

    o_groups, lse_groups = [], []
    for g, (window, dilation) in enumerate(DIL_PATTERNS):
        l = s // dilation
        qg, kg, vg = (z[f"{n}{g}"].reshape(bsz * dilation, l, DIL_GROUP_WIDTH) for n in ("q_att", "k_att", "v_att"))
        table = rel_bias[:, g * DIL_HEADS_PER_GROUP:(g + 1) * DIL_HEADS_PER_GROUP]
        o, lse = _dilated_group_attention(qg, kg, vg, table, window, dilation)
        o_groups.append(o.reshape(bsz, dilation, l, DIL_GROUP_WIDTH))
        lse_groups.append(lse.reshape(bsz, dilation, l, DIL_GROUP_WIDTH))

    wr = jnp.concatenate([w_rg, w_re, jnp.zeros((d, LANES - MOE_GROUPS - MOE_TOTAL), F32)], axis=1)
    br = jnp.concatenate([b_rg, b_re, jnp.zeros((LANES - MOE_GROUPS - MOE_TOTAL,), F32)]).reshape(1, LANES)
    x1, u2, route, ew, cnt = _merge(alpha, y_gla, o_groups, lse_groups, z["g_gla"], z["g_att"], x, g1, sc2, sh2,
                                    ln1_g, ln1_b, w_proj_gla.astype(BF16), w_proj_attn.astype(BF16),
                                    w_out.astype(BF16), wr, br)

    n = bsz * s
    counts = cnt[MOE_GROUPS:MOE_GROUPS + MOE_TOTAL, 0]
    dest0, dest1, tile_run, tile_valid, run_expert, used, n_rows = _dispatch_plan(route, counts)
    xg = _sc_scatter_rows(u2.reshape(n, d // 2), dest0, dest1, n_rows)
    ff = w_eg.shape[-1]
    yo = _expert_ffn(tile_run, tile_valid, run_expert, used, xg, w_eg.reshape(MOE_TOTAL, d, ff),
                     w_eu.reshape(MOE_TOTAL, d, ff), w_ed.reshape(MOE_TOTAL, ff, d))
    ya, yb = (y.reshape(bsz, s, d // 2) for y in _sc_gather_rows(yo, dest0, dest1))
    return _final(alpha, x1, ya, yb, ew, g2, ln2_g, ln2_b)


def kernel(x, c, rel_bias, w_ada, b_ada, w_in, w_gla_gate, b_gla_gate, gla_norm, w_proj_gla, w_proj_attn, w_out,
           ln1_g, ln1_b, w_router_group, b_router_group, w_router_expert, b_router_expert, w_exp_gate, w_exp_up,
           w_exp_down, ln2_g, ln2_b):
    assert w_ada.shape[0] == DEPTH
    return _layer(x, c, rel_bias, w_ada[0], b_ada[0], w_in[0:1], w_gla_gate[0], b_gla_gate[0], gla_norm[0],
                  w_proj_gla[0], w_proj_attn[0], w_out[0], ln1_g[0], ln1_b[0], w_router_group[0],
                  b_router_group[0], w_router_expert[0], b_router_expert[0], w_exp_gate[0], w_exp_up[0],
                  w_exp_down[0], ln2_g[0], ln2_b[0])
```

```python
import functools
import math

import numpy as np
import jax
import jax.numpy as jnp
from jax import lax
from jax.experimental import pallas as pl
from jax.experimental.pallas import tpu as pltpu
from jax.experimental.pallas import tpu_sc as plsc

F32 = jnp.float32
BF16 = jnp.bfloat16

N_MOD = 6
GLA_HEADS = 4
GLA_LOWRANK = 16
GLA_TAU = 16.0
GLA_CHUNK = 64
DIL_PATTERNS = ((128, 1), (512, 4), (2048, 16))
DIL_GROUPS = len(DIL_PATTERNS)
DIL_HEADS_PER_GROUP = 8
DIL_HEAD_DIM = 64
DIL_GROUP_WIDTH = DIL_HEADS_PER_GROUP * DIL_HEAD_DIM
DIL_BLOCK = 128
REL_BUCKETS = 32
REL_MAX_DIST = 2048
MOE_GROUPS = 4
MOE_EXPERTS = 8
MOE_TOTAL = MOE_GROUPS * MOE_EXPERTS
LN_EPS = 1e-5
DEPTH = 1

LANES = 128
VMEM_LIMIT = 56 * 1024 * 1024
LOG2E = 1.4426950408889634
LN2 = 0.6931471805599453
NEG = -1e30
ROW_TILE = 512
PROJ_CHUNK = 512
FINAL_TILE = 1024
EXPERT_TILE = 512
EXPERT_BLOCK = 128
GLA_STEP_CHUNKS = 16
ATT_STEP_BLOCKS = 8
ROUTER_ROWS = 40
ROUTE_ROWS = 8

NT_DIMS = (((1,), (1,)), ((), ()))
TN_DIMS = (((0,), (0,)), ((), ()))


def _cparams(sem):
    return pltpu.CompilerParams(dimension_semantics=sem, vmem_limit_bytes=VMEM_LIMIT)


def _sigmoid(x):
    return 0.5 * jnp.tanh(0.5 * x) + 0.5


def _silu(x):
    return x * _sigmoid(x)


def _layer_norm(x, g, b):
    mu = jnp.mean(x, axis=-1, keepdims=True)
    xc = x - mu
    var = jnp.mean(xc * xc, axis=-1, keepdims=True)
    return xc * lax.rsqrt(var + LN_EPS) * g + b


def _pack_bf16_pairs(x):
    w = x.shape[1] // 2
    lo = lax.bitcast_convert_type(x[:, :w].astype(BF16).astype(F32), jnp.uint32) >> 16
    hi = lax.bitcast_convert_type(x[:, w:].astype(BF16).astype(F32), jnp.uint32) & jnp.uint32(0xFFFF0000)
    return lax.bitcast_convert_type(lo | hi, jnp.int32)


def _unpack_bf16_pairs(p):
    u = lax.bitcast_convert_type(p, jnp.uint32)
    lo = lax.bitcast_convert_type(u << 16, F32)
    hi = lax.bitcast_convert_type(u & jnp.uint32(0xFFFF0000), F32)
    return jnp.concatenate([lo, hi], axis=1)


def _mods_kernel(ct_ref, w_ref, b_ref, o_ref):
    a = _silu(ct_ref[...])
    w = w_ref[...]
    for b in range(a.shape[1]):
        o_ref[b:b + 1, :] = jnp.sum(a[:, b:b + 1] * w, axis=0, keepdims=True) + b_ref[...]


def _ada_mods(c, w, b):
    bsz, d = c.shape
    n = w.shape[1]
    tn = 1536
    assert n % tn == 0
    return pl.pallas_call(
        _mods_kernel,
        out_shape=jax.ShapeDtypeStruct((bsz, n), F32),
        grid=(n // tn,),
        in_specs=[pl.BlockSpec((d, bsz), lambda j: (0, 0)),
                  pl.BlockSpec((d, tn), lambda j: (0, j)),
                  pl.BlockSpec((1, tn), lambda j: (0, j))],
        out_specs=pl.BlockSpec((bsz, tn), lambda j: (0, j)),
        compiler_params=_cparams(("arbitrary",)),
        name="ada_mods",
    )(c.T, w, b.reshape(1, n))


def _proj_pieces(d_model):
    dk = d_model // 2
    pieces = [("q_gla", dk, "scale_q_gla"), ("k_gla", dk, None), ("v_gla", d_model, None), ("r_gla", d_model, "silu")]
    for name, post in (("q_att", "scale_q_att"), ("k_att", None), ("v_att", None)):
        for g, (_, dilation) in enumerate(DIL_PATTERNS):
            pieces.append((f"{name}{g}", DIL_GROUP_WIDTH, (post, dilation)))
    pieces += [("g_gla", d_model, "sigmoid"), ("g_att", d_model, "sigmoid"), ("lr", LANES, "lowrank")]
    return tuple(pieces)


WT_BLOCK = 512


def _stage_pitch(dilation):
    return dilation + 8 if dilation % 16 == 0 else dilation


def _merge_pitch(dilation):
    return dilation + 4 if dilation % 16 == 0 else dilation


def _stage_rows(tm, dilations):
    return max(tm // d * _stage_pitch(d) for d in dilations)


def _load_in_weight(lr0, wt_hbm, w_ref, stage_ref, sem):
    n_main = wt_hbm.shape[1] - GLA_LOWRANK
    n_blocks = n_main // WT_BLOCK + 1
    src_rows = [j * WT_BLOCK + (GLA_LOWRANK if j * WT_BLOCK >= lr0 else 0) for j in range(n_blocks - 1)] + [lr0]

    def copy(j):
        return pltpu.make_async_copy(wt_hbm.at[0, pl.ds(src_rows[j], WT_BLOCK), :], stage_ref.at[j % 2], sem.at[j % 2])

    copy(0).start()
    for j in range(n_blocks):
        if j + 1 < n_blocks:
            copy(j + 1).start()
        copy(j).wait()
        blk = stage_ref[j % 2]
        if j == n_blocks - 1:
            blk = jnp.where(lax.broadcasted_iota(jnp.int32, blk.shape, 0) < GLA_LOWRANK, blk, 0.0)
        w_ref[j * WT_BLOCK:(j + 1) * WT_BLOCK, :] = blk.astype(BF16)


GLA_HELD = ("lr", "q_gla", "k_gla")


def _gla_operands(hold, wg_ref, bg_ref, qin_ref, kin_ref, qst_ref, kst_ref, dec_ref):
    c = GLA_CHUNK
    tm = hold["q_gla"].shape[0]
    tril = (lax.broadcasted_iota(jnp.int32, (c, c), 0) >= lax.broadcasted_iota(jnp.int32, (c, c), 1)).astype(BF16)
    mid = c // 2 - 1
    lr, wg = hold["lr"][:, 0:GLA_LOWRANK], wg_ref[...]
    lr_hi, wg_hi = lr.astype(BF16), wg.astype(BF16)
    lr_lo, wg_lo = (lr - lr_hi.astype(F32)).astype(BF16), (wg - wg_hi.astype(F32)).astype(BF16)
    gate_in = (jnp.dot(lr_hi, wg_hi, preferred_element_type=F32) + jnp.dot(lr_lo, wg_hi, preferred_element_type=F32)
               + jnp.dot(lr_hi, wg_lo, preferred_element_type=F32)) + bg_ref[...]
    g_all = (jnp.minimum(gate_in, 0.0) - jnp.log(1.0 + jnp.exp(-jnp.abs(gate_in)))) * (1.0 / GLA_TAU)
    g_hi = g_all.astype(BF16)
    g_lo = (g_all - g_hi.astype(F32)).astype(BF16)
    for ci in range(tm // c):
        rows = slice(ci * c, (ci + 1) * c)
        bc = jnp.dot(tril, g_hi[rows], preferred_element_type=F32) + jnp.dot(tril, g_lo[rows], preferred_element_type=F32)
        b_mid = bc[mid:mid + 1, :]
        b_last = bc[c - 1:c, :]
        qf = hold["q_gla"][rows, :]
        kf = hold["k_gla"][rows, :]
        q_in = qf * jnp.exp(bc - b_mid)
        k_in = kf * jnp.exp(b_mid - bc)
        qin_ref[0, rows, :] = q_in.astype(BF16)
        kin_ref[0, rows, :] = k_in.astype(BF16)
        qst_ref[0, rows, :] = (q_in * jnp.exp(b_mid)).astype(BF16)
        kst_ref[0, rows, :] = (k_in * jnp.exp(b_last - b_mid)).astype(BF16)
        dec_ref[0, ci:ci + 1, :] = jnp.exp(b_last)


def _proj_kernel(pieces, head_k, lr0, x_ref, sc_ref, sh_ref, wt_hbm, wg_ref, bg_ref, *refs):
    n_out = len(pieces) - len(GLA_HELD)
    out_refs = dict(zip([p[0] for p in pieces if p[0] not in GLA_HELD], refs[:n_out]))
    gla_out_refs = refs[n_out:n_out + 5]
    stage_ref = refs[n_out + 5]
    hold = dict(zip(GLA_HELD, refs[n_out + 6:n_out + 9]))
    w_ref, wstage_ref, wsem = refs[n_out + 9:]
    tm = x_ref.shape[1]

    @pl.when((pl.program_id(0) == 0) & (pl.program_id(1) == 0))
    def _():
        _load_in_weight(lr0, wt_hbm, w_ref, wstage_ref, wsem)

    u = (x_ref[0] * (1.0 + sc_ref[0]) + sh_ref[0]).astype(BF16)
    offsets, off = {}, 0
    for name, width, _ in pieces:
        offsets[name] = off
        off += width
    by_name = {p[0]: p for p in pieces}
    held = [(by_name[n], 0) for n in GLA_HELD]
    rest = [(p, c0) for p in pieces if p[0] not in GLA_HELD for c0 in range(0, p[1], min(p[1], PROJ_CHUNK))]
    for n, (piece, c0) in enumerate(held + rest):
        if n == len(held):
            _gla_operands(hold, wg_ref, bg_ref, *gla_out_refs)
        name, width, post = piece
        o_ref = hold[name] if name in GLA_HELD else out_refs[name]
        off = offsets[name]
        chunk = min(width, PROJ_CHUNK)
        acc = lax.dot_general(u, w_ref[off + c0:off + c0 + chunk, :], NT_DIMS, preferred_element_type=F32)
        if post == "silu":
            acc = _silu(acc)
        elif post == "sigmoid":
            acc = _sigmoid(acc)
        elif post == "scale_q_gla":
            acc = acc * (head_k ** -0.5)
        if name in GLA_HELD:
            o_ref[...] = acc
        elif isinstance(post, tuple):
            scale, dilation = post
            if scale is not None:
                acc = acc * (DIL_HEAD_DIM ** -0.5 * LOG2E)
            if dilation == 1:
                o_ref[0, 0] = acc.astype(o_ref.dtype)
            else:
                pitch = _merge_pitch(dilation)
                for t in range(width // LANES):
                    if pitch == dilation:
                        stage_ref[t, 0:tm, :] = acc[:, t * LANES:(t + 1) * LANES]
                    else:
                        for j in range(tm // dilation):
                            stage_ref[t, j * pitch:j * pitch + dilation, :] = acc[
                                j * dilation:(j + 1) * dilation, t * LANES:(t + 1) * LANES]
                for r in range(dilation):
                    for t in range(width // LANES):
                        o_ref[0, r, :, t * LANES:(t + 1) * LANES] = stage_ref[
                            t, pl.ds(r, tm // dilation, stride=pitch), :].astype(o_ref.dtype)
        else:
            o_ref[0, :, c0:c0 + chunk] = acc.astype(o_ref.dtype)


def _in_projection(x, sc1, sh1, w_in, lr0, w_gate, b_gate):
    bsz, s, d = x.shape
    pieces = _proj_pieces(d)
    w_t = jnp.swapaxes(w_in, 1, 2)
    n_main = w_t.shape[1] - GLA_LOWRANK
    assert lr0 % WT_BLOCK == 0 and n_main % WT_BLOCK == 0
    w_rows = n_main + WT_BLOCK
    assert sum(p[1] for p in pieces) <= w_rows
    tm = min(ROW_TILE, s)
    assert s % tm == 0 and tm % (8 * GLA_CHUNK) == 0
    dk = d // 2
    head_k = dk // GLA_HEADS
    out_shape, out_specs = [], []
    for name, width, post in pieces:
        if name in GLA_HELD:
            continue
        if isinstance(post, tuple):
            dil = post[1]
            assert tm % (dil * 16) == 0
            out_shape.append(jax.ShapeDtypeStruct((bsz, dil, s // dil, width), BF16))
            out_specs.append(pl.BlockSpec((1, dil, tm // dil, width), lambda b, i: (b, 0, i, 0)))
        else:
            out_shape.append(jax.ShapeDtypeStruct((bsz, s, width), BF16))
            out_specs.append(pl.BlockSpec((1, tm, width), lambda b, i: (b, i, 0)))
    row = lambda w: pl.BlockSpec((1, tm, w), lambda b, i: (b, i, 0))
    gla_names = ("q_in", "k_in", "q_st", "k_st", "dec")
    out_shape += [jax.ShapeDtypeStruct((bsz, s, dk), BF16)] * 4 + [jax.ShapeDtypeStruct((bsz, s // GLA_CHUNK, dk), F32)]
    out_specs += [row(dk)] * 4 + [pl.BlockSpec((1, tm // GLA_CHUNK, dk), lambda b, i: (b, i, 0))]
    bg = b_gate.reshape(1, dk)
    full = lambda a: pl.BlockSpec(a.shape, lambda b, i: (0,) * a.ndim)
    outs = pl.pallas_call(
        functools.partial(_proj_kernel, pieces, head_k, lr0),
        out_shape=out_shape,
        grid=(bsz, s // tm),
        in_specs=[row(d),
                  pl.BlockSpec((1, 1, d), lambda b, i: (b, 0, 0)),
                  pl.BlockSpec((1, 1, d), lambda b, i: (b, 0, 0)),
                  pl.BlockSpec(memory_space=pl.ANY),
                  full(w_gate), full(bg)],
        out_specs=out_specs,
        scratch_shapes=[pltpu.VMEM((DIL_GROUP_WIDTH // LANES, _stage_rows(tm, [dl for _, dl in DIL_PATTERNS]), LANES),
                                   F32),
                        pltpu.VMEM((tm, LANES), F32), pltpu.VMEM((tm, dk), F32), pltpu.VMEM((tm, dk), F32),
                        pltpu.VMEM((w_rows, d), BF16), pltpu.VMEM((2, WT_BLOCK, d), F32),
                        pltpu.SemaphoreType.DMA((2,))],
        compiler_params=_cparams(("arbitrary", "arbitrary")),
        name="in_projection",
    )(x, sc1, sh1, w_t, w_gate, bg)
    return dict(zip([p[0] for p in pieces if p[0] not in GLA_HELD] + list(gla_names), outs))


def _gla_kernel(n_chunks, head_k, head_v, qin_ref, kin_ref, qst_ref, kst_ref, dec_ref, v_ref, r_ref, ng_ref, o_ref,
                state_ref):
    @pl.when(pl.program_id(1) == 0)
    def _():
        state_ref[...] = jnp.zeros_like(state_ref)

    c = GLA_CHUNK
    causal = lax.broadcasted_iota(jnp.int32, (c, c), 0) >= lax.broadcasted_iota(jnp.int32, (c, c), 1)
    for ci in range(n_chunks):
        rows = slice(ci * c, (ci + 1) * c)
        for h in range(GLA_HEADS):
            ks = slice(h * head_k, (h + 1) * head_k)
            vs = slice(h * head_v, (h + 1) * head_v)
            vh = v_ref[0, rows, vs]
            att = lax.dot_general(qin_ref[0, rows, ks], kin_ref[0, rows, ks], NT_DIMS, preferred_element_type=F32)
            att = jnp.where(causal, att, 0.0).astype(BF16)
            st = state_ref[h]
            o = jnp.dot(att, vh, preferred_element_type=F32)
            o = o + lax.dot_general(qst_ref[0, rows, ks], st.astype(BF16), NT_DIMS, preferred_element_type=F32)
            kv_t = lax.dot_general(vh, kst_ref[0, rows, ks], TN_DIMS, preferred_element_type=F32)
            state_ref[h] = st * dec_ref[0, ci:ci + 1, ks] + kv_t
            ms = jnp.mean(o * o, axis=-1, keepdims=True)
            o = o * lax.rsqrt(ms + LN_EPS) * ng_ref[:, vs] * r_ref[0, rows, vs].astype(F32)
            o_ref[0, rows, vs] = o.astype(o_ref.dtype)


def _gla(q_in, k_in, q_st, k_st, dec, v, r_silu, norm_g):
    bsz, s, dk = q_in.shape
    dv = v.shape[-1]
    head_k, head_v = dk // GLA_HEADS, dv // GLA_HEADS
    n_chunks = min(GLA_STEP_CHUNKS, s // GLA_CHUNK)
    ct = GLA_CHUNK * n_chunks
    assert s % ct == 0
    row_spec = lambda w: pl.BlockSpec((1, ct, w), lambda b, i: (b, i, 0))
    full = lambda a: pl.BlockSpec(a.shape, lambda b, i: (0,) * a.ndim)
    ng = norm_g.reshape(1, dv)
    return pl.pallas_call(
        functools.partial(_gla_kernel, n_chunks, head_k, head_v),
        out_shape=jax.ShapeDtypeStruct((bsz, s, dv), BF16),
        grid=(bsz, s // ct),
        in_specs=[row_spec(dk)] * 4 + [pl.BlockSpec((1, n_chunks, dk), lambda b, i: (b, i, 0)),
                                       row_spec(dv), row_spec(dv), full(ng)],
        out_specs=row_spec(dv),
        scratch_shapes=[pltpu.VMEM((GLA_HEADS, head_v, head_k), F32)],
        compiler_params=_cparams(("parallel", "arbitrary")),
        name="gla",
    )(q_in, k_in, q_st, k_st, dec, v, r_silu, ng)


def _t5_bucket_np(dist):
    exact = REL_BUCKETS // 2
    d = np.maximum(dist, 1).astype(np.float32)
    large = exact + (np.log(d / np.float32(exact)) / np.float32(math.log(REL_MAX_DIST / exact))
                     * np.float32(REL_BUCKETS - exact)).astype(np.int32)
    large = np.minimum(large, REL_BUCKETS - 1)
    return np.where(dist < exact, dist, large).astype(np.int32)


def _band_tables(window, dilation):
    qi = np.arange(DIL_BLOCK)[:, None]
    kj = np.arange(2 * DIL_BLOCK)[None, :]
    m = qi + DIL_BLOCK - kj
    n_steps = window // dilation
    band = (m >= 0) & (m <= n_steps)
    bucket = _t5_bucket_np(np.clip(m, 0, n_steps) * dilation)
    return np.where(band, bucket, -1).astype(np.int32)


def _attn_kernel(nq, table_ref, bucket_ref, q_ref, kp_ref, kc_ref, vp_ref, vc_ref, o_ref, lse_ref,
                 bias_ref, p_ref):
    i = pl.program_id(1)
    blk = DIL_BLOCK
    hpg = DIL_HEADS_PER_GROUP
    n_pairs = hpg // 2

    @pl.when((pl.program_id(0) == 0) & (i == 0))
    def _():
        bucket = bucket_ref[...]
        for h in range(hpg):
            acc = jnp.full(bucket.shape, NEG, F32)
            for bkt in range(REL_BUCKETS):
                acc = jnp.where(bucket == bkt, table_ref[bkt, h] * LOG2E, acc)
            bias_ref[h * blk:(h + 1) * blk, :] = acc

    lane = lax.broadcasted_iota(jnp.int32, (blk, LANES), 1)
    low = lane < DIL_HEAD_DIM
    ones_rhs = jnp.ones((2 * blk, LANES), BF16)

    def windows(ref_p, ref_c, sq, qb, cols):
        if qb == 0:
            return jnp.concatenate([ref_p[sq, :, cols], ref_c[sq, 0:blk, cols]], axis=0)
        return ref_c[sq, (qb - 1) * blk:(qb + 1) * blk, cols]

    key_lane = lax.broadcasted_iota(jnp.int32, (1, 2 * blk), 1)
    no_prev = jnp.where((key_lane < blk) & (i == 0), NEG, 0.0)
    items = [(sq, qb, hp) for sq in range(q_ref.shape[0]) for qb in range(nq) for hp in range(n_pairs)]

    mxs = []
    for n, (sq, qb, hp) in enumerate(items):
        rows = slice(qb * blk, (qb + 1) * blk)
        cols = slice(hp * LANES, (hp + 1) * LANES)
        qp = q_ref[sq, rows, cols]
        zero = jnp.zeros_like(qp)
        qq = jnp.concatenate([jnp.where(low, qp, zero), jnp.where(low, zero, qp)], axis=0)
        keys = windows(kp_ref, kc_ref, sq, qb, cols)
        s = lax.dot_general(qq, keys, NT_DIMS, preferred_element_type=F32) + bias_ref[2 * hp * blk:(2 * hp + 2) * blk, :]
        if qb == 0:
            s = s + no_prev
        mx = jnp.max(s, axis=-1, keepdims=True)
        p_ref[n * 2 * blk:(n + 1) * 2 * blk, :] = jnp.exp2(s - mx).astype(BF16)
        mxs.append(mx)

    for n, (sq, qb, hp) in enumerate(items):
        rows = slice(qb * blk, (qb + 1) * blk)
        cols = slice(hp * LANES, (hp + 1) * LANES)
        vals = windows(vp_ref, vc_ref, sq, qb, cols)
        rhs = jnp.concatenate([vals, ones_rhs], axis=1)
        res = jnp.dot(p_ref[n * 2 * blk:(n + 1) * 2 * blk, :], rhs, preferred_element_type=F32)
        num = jnp.where(low, res[0:blk, 0:LANES], res[blk:2 * blk, 0:LANES])
        den = jnp.where(low, res[0:blk, LANES:], res[blk:2 * blk, LANES:])
        mx = jnp.where(low, mxs[n][0:blk], mxs[n][blk:2 * blk])
        o_ref[sq, rows, cols] = (num / den).astype(o_ref.dtype)
        lse_ref[sq, rows, cols] = (mx + jnp.log2(den)) * LN2


def _dilated_group_attention(q, k, v, table, window, dilation):
    bb, l, w = q.shape
    nq = min(ATT_STEP_BLOCKS, l // DIL_BLOCK)
    nsq = ATT_STEP_BLOCKS // nq
    assert l % (nq * DIL_BLOCK) == 0 and bb % nsq == 0
    steps = l // (nq * DIL_BLOCK)
    bucket = jnp.asarray(_band_tables(window, dilation))
    cur = pl.BlockSpec((nsq, nq * DIL_BLOCK, w), lambda b, i: (b, i, 0))
    prev = pl.BlockSpec((nsq, DIL_BLOCK, w), lambda b, i: (b, jnp.maximum(nq * i - 1, 0), 0))
    rows_all = nsq * nq * DIL_HEADS_PER_GROUP * DIL_BLOCK
    return pl.pallas_call(
        functools.partial(_attn_kernel, nq),
        out_shape=[jax.ShapeDtypeStruct((bb, l, w), BF16), jax.ShapeDtypeStruct((bb, l, w), F32)],
        grid=(bb // nsq, steps),
        in_specs=[pl.BlockSpec(memory_space=pltpu.SMEM),
                  pl.BlockSpec(bucket.shape, lambda b, i: (0, 0)),
                  cur, prev, cur, prev, cur],
        out_specs=[cur, cur],
        scratch_shapes=[pltpu.VMEM((DIL_HEADS_PER_GROUP * DIL_BLOCK, 2 * DIL_BLOCK), F32),
                        pltpu.VMEM((rows_all, 2 * DIL_BLOCK), BF16)],
        compiler_params=_cparams(("arbitrary", "arbitrary")),
        name=f"dilated_attn_d{dilation}",
    )(table, bucket, q, k, k, v, v)


def _merge_kernel(alpha, dilations, ygla_ref, o0_ref, o1_ref, o2_ref, l0_ref, l1_ref, l2_ref, gg_ref, ga_ref, x_ref,
                  g1_ref, sc2_ref, sh2_ref, ln_g_ref, ln_b_ref, wpg_ref, wpa_ref, wout_ref, wr_ref, br_ref, utri_ref,
                  x1_ref, u2_ref, route_ref, ew_ref, cnt_ref, stage_ref, carry_ref):
    tm = x_ref.shape[1]

    @pl.when((pl.program_id(0) == 0) & (pl.program_id(1) == 0))
    def _():
        carry_ref[...] = jnp.zeros_like(carry_ref)

    n_lt = DIL_GROUP_WIDTH // LANES
    group_refs = tuple(zip((l0_ref, l1_ref, l2_ref), (o0_ref, o1_ref, o2_ref), dilations))
    for gi, (l_ref, o_ref, dil) in enumerate(group_refs):
        if dil > 1:
            for slot, ref in ((2 * gi, l_ref), (2 * gi + 1, o_ref)):
                for r in range(dil):
                    for t in range(n_lt):
                        stage_ref[slot, t, pl.ds(r, tm // dil, stride=_merge_pitch(dil)), :] = ref[
                            0, r, :, t * LANES:(t + 1) * LANES].astype(F32)

    def natural(ref, dil, slot):
        if dil == 1:
            return ref[0, 0].astype(F32)
        pitch = _merge_pitch(dil)
        if pitch == dil:
            return jnp.concatenate([stage_ref[slot, t, 0:tm, :] for t in range(n_lt)], axis=1)
        return jnp.concatenate(
            [jnp.concatenate([stage_ref[slot, t, j * pitch:j * pitch + dil, :] for j in range(tm // dil)], axis=0)
             for t in range(n_lt)], axis=1)

    lses = [natural(l_ref, dil, 2 * gi) for gi, (l_ref, _, dil) in enumerate(group_refs)]
    outs = [natural(o_ref, dil, 2 * gi + 1) for gi, (_, o_ref, dil) in enumerate(group_refs)]
    lm = jnp.maximum(jnp.maximum(lses[0], lses[1]), lses[2])
    es = [jnp.exp(l - lm) for l in lses]
    y_att = (es[0] * outs[0] + es[1] * outs[1] + es[2] * outs[2]) / (es[0] + es[1] + es[2])

    p_gla = jnp.dot(ygla_ref[0], wpg_ref[...], preferred_element_type=F32)
    p_att = jnp.dot(y_att.astype(BF16), wpa_ref[...], preferred_element_type=F32)
    merged = gg_ref[0].astype(F32) * p_gla + ga_ref[0].astype(F32) * p_att
    y = jnp.dot(merged.astype(BF16), wout_ref[...], preferred_element_type=F32)
    x1 = _layer_norm(alpha * x_ref[0] + g1_ref[0] * y, ln_g_ref[...], ln_b_ref[...])
    x1_ref[0] = x1
    u2 = x1 * (1.0 + sc2_ref[0]) + sh2_ref[0]
    u2_ref[0] = _pack_bf16_pairs(u2)

    logits = jnp.dot(u2.astype(BF16), wr_ref[...].astype(BF16), preferred_element_type=F32) + br_ref[...]
    lt = jnp.transpose(logits)[0:ROUTER_ROWS, :]
    rowi = lax.broadcasted_iota(jnp.int32, lt.shape, 0)
    big = jnp.int32(LANES)
    lg = jnp.where(rowi < MOE_GROUPS, lt, NEG)
    gmax = jnp.max(lg, axis=0, keepdims=True)
    gidx = jnp.min(jnp.where(lg == gmax, rowi, big), axis=0, keepdims=True)
    gval = 1.0 / jnp.sum(jnp.exp(lg - gmax), axis=0, keepdims=True)
    first = MOE_GROUPS + gidx * MOE_EXPERTS
    le = jnp.where((rowi >= first) & (rowi < first + MOE_EXPERTS), lt, NEG)
    m1 = jnp.max(le, axis=0, keepdims=True)
    i1 = jnp.min(jnp.where(le == m1, rowi, big), axis=0, keepdims=True)
    le2 = jnp.where(rowi == i1, NEG, le)
    m2 = jnp.max(le2, axis=0, keepdims=True)
    i2 = jnp.min(jnp.where(le2 == m2, rowi, big), axis=0, keepdims=True)
    t = jnp.exp(m2 - m1)
    w1 = 1.0 / (1.0 + t)
    w2 = t * w1

    hit1, hit2 = rowi == i1, rowi == i2
    onehot = jnp.where(hit1 | hit2, 1.0, 0.0)
    earlier = jnp.dot(onehot.astype(BF16), utri_ref[...], preferred_element_type=F32) + carry_ref[...]
    rank1 = jnp.sum(jnp.where(hit1, earlier, 0.0), axis=0, keepdims=True).astype(jnp.int32)
    rank2 = jnp.sum(jnp.where(hit2, earlier, 0.0), axis=0, keepdims=True).astype(jnp.int32)
    carry = carry_ref[...] + jnp.sum(onehot, axis=1, keepdims=True)
    carry_ref[...] = carry
    cnt_ref[...] = jnp.broadcast_to(carry, cnt_ref.shape).astype(jnp.int32)
    r8 = lax.broadcasted_iota(jnp.int32, (ROUTE_ROWS, tm), 0)
    route_ref[0] = jnp.where(r8 == 0, i1 - MOE_GROUPS, jnp.where(r8 == 1, i2 - MOE_GROUPS,
                             jnp.where(r8 == 2, rank1, jnp.where(r8 == 3, rank2, 0))))
    r128 = lax.broadcasted_iota(jnp.int32, (LANES, tm), 0)
    ew_ref[0] = jnp.transpose(jnp.where(r128 == 0, gval * w1, jnp.where(r128 == 1, gval * w2, 0.0)))


def _merge(alpha, y_gla, o_groups, lse_groups, g_gla, g_att, x, g1, sc2, sh2, ln_g, ln_b, wpg, wpa, wout, wr, br):
    bsz, s, d = x.shape
    tm = min(ROW_TILE, s)
    assert s % tm == 0
    dilations = tuple(dil for _, dil in DIL_PATTERNS)
    row = lambda w: pl.BlockSpec((1, tm, w), lambda b, i: (b, i, 0))
    sub = lambda dil: pl.BlockSpec((1, dil, tm // dil, DIL_GROUP_WIDTH), lambda b, i: (b, 0, i, 0))
    per_b = pl.BlockSpec((1, 1, d), lambda b, i: (b, 0, 0))
    full = lambda a: pl.BlockSpec(a.shape, lambda b, i: (0,) * a.ndim)
    ln_g2, ln_b2 = ln_g.reshape(1, d), ln_b.reshape(1, d)
    utri = jnp.asarray(np.triu(np.ones((tm, tm), np.float32), 1), BF16)
    return pl.pallas_call(
        functools.partial(_merge_kernel, alpha, dilations),
        out_shape=[jax.ShapeDtypeStruct((bsz, s, d), F32), jax.ShapeDtypeStruct((bsz, s, d // 2), jnp.int32),
                   jax.ShapeDtypeStruct((bsz, ROUTE_ROWS, s), jnp.int32), jax.ShapeDtypeStruct((bsz, s, LANES), F32),
                   jax.ShapeDtypeStruct((ROUTER_ROWS, LANES), jnp.int32)],
        grid=(bsz, s // tm),
        in_specs=[row(y_gla.shape[-1])] + [sub(dil) for dil in dilations] * 2
                 + [row(d), row(d), row(d), per_b, per_b, per_b, full(ln_g2), full(ln_b2),
                    full(wpg), full(wpa), full(wout), full(wr), full(br), full(utri)],
        out_specs=[row(d), row(d // 2), pl.BlockSpec((1, ROUTE_ROWS, tm), lambda b, i: (b, 0, i)), row(LANES),
                   pl.BlockSpec((ROUTER_ROWS, LANES), lambda b, i: (0, 0))],
        scratch_shapes=[pltpu.VMEM((2 * DIL_GROUPS, DIL_GROUP_WIDTH // LANES, _stage_rows(tm, dilations), LANES), F32),
                        pltpu.VMEM((ROUTER_ROWS, 1), F32)],
        compiler_params=_cparams(("arbitrary", "arbitrary")),
        name="merge_ln1_router",
    )(y_gla, *o_groups, *lse_groups, g_gla, g_att, x, g1, sc2, sh2, ln_g2, ln_b2, wpg, wpa, wout, wr, br, utri)


def _expert_kernel(run_ref, valid_ref, rexp_ref, used_ref, x_ref, wg_hbm, wu_hbm, wd_hbm, o_ref,
                   wg_f, wu_f, wd_f, wg_s, wu_s, wd_s, sem):
    t = pl.program_id(0)
    n_tiles_used, n_runs = used_ref[0], used_ref[1]
    run = run_ref[t]
    active = t < n_tiles_used
    first_of_run = (t == 0) | (run_ref[jnp.maximum(t - 1, 0)] != run)

    def weight_copies(r):
        e, slot = rexp_ref[r], r % 2
        return [pltpu.make_async_copy(hbm.at[e], buf.at[slot], sem.at[slot, j])
                for j, (hbm, buf) in enumerate(((wg_hbm, wg_f), (wu_hbm, wu_f), (wd_hbm, wd_f)))]

    @pl.when(active & (t == 0))
    def _():
        for cp in weight_copies(0):
            cp.start()

    @pl.when(active & first_of_run)
    def _():
        @pl.when(run + 1 < n_runs)
        def _():
            for cp in weight_copies(run + 1):
                cp.start()

        for cp in weight_copies(run):
            cp.wait()
        slot = run % 2
        wg_s[...] = wg_f[slot].astype(BF16)
        wu_s[...] = wu_f[slot].astype(BF16)
        wd_s[...] = wd_f[slot].astype(BF16)

    n_valid = jnp.where(active, valid_ref[t], 0)

    def ffn(rows):
        xt = _unpack_bf16_pairs(x_ref[rows, :]).astype(BF16)
        hg = jnp.dot(xt, wg_s[...], preferred_element_type=F32)
        hu = jnp.dot(xt, wu_s[...], preferred_element_type=F32)
        h = (_silu(hg) * hu).astype(BF16)
        o_ref[rows, :] = _pack_bf16_pairs(jnp.dot(h, wd_s[...], preferred_element_type=F32))

    def zero(rows):
        o_ref[rows, :] = jnp.zeros((rows.stop - rows.start, o_ref.shape[1]), o_ref.dtype)

    tm = x_ref.shape[0]
    n_blocks = tm // EXPERT_BLOCK
    for k in range(n_blocks + 1):
        lo, hi = (k - 1) * EXPERT_BLOCK, k * EXPERT_BLOCK

        @pl.when((n_valid > lo) & (n_valid <= hi) if 0 < k < n_blocks else (n_valid > lo if k else n_valid <= 0))
        def _():
            if k:
                ffn(slice(0, hi))
            if k < n_blocks:
                zero(slice(hi, tm))


def _expert_ffn(tile_run, tile_valid, run_expert, used, xg, w_gate, w_up, w_down):
    p = xg.shape[0]
    ne, d, ff = w_gate.shape
    tm = EXPERT_TILE
    n_tiles = p // tm
    hbm = pl.BlockSpec(memory_space=pl.ANY)

    def tile(t, run, valid, rexp, used):
        return jnp.where(t < used[0], t, n_tiles - 1), 0

    grid_spec = pltpu.PrefetchScalarGridSpec(
        num_scalar_prefetch=4,
        grid=(n_tiles,),
        in_specs=[pl.BlockSpec((tm, d // 2), tile), hbm, hbm, hbm],
        out_specs=pl.BlockSpec((tm, d // 2), tile),
        scratch_shapes=[pltpu.VMEM((2, d, ff), F32), pltpu.VMEM((2, d, ff), F32), pltpu.VMEM((2, ff, d), F32),
                        pltpu.VMEM((d, ff), BF16), pltpu.VMEM((d, ff), BF16), pltpu.VMEM((ff, d), BF16),
                        pltpu.SemaphoreType.DMA((2, 3))],
    )
    return pl.pallas_call(
        _expert_kernel,
        out_shape=jax.ShapeDtypeStruct((p, d // 2), jnp.int32),
        grid_spec=grid_spec,
        compiler_params=_cparams(("arbitrary",)),
        name="expert_ffn",
    )(tile_run, tile_valid, run_expert, used, xg, w_gate, w_up, w_down)


def _final_kernel(alpha, x1_ref, ya_ref, yb_ref, ew_ref, g2_ref, ln_g_ref, ln_b_ref, o_ref):
    ew = ew_ref[0]
    y = ew[:, 0:1] * _unpack_bf16_pairs(ya_ref[0]) + ew[:, 1:2] * _unpack_bf16_pairs(yb_ref[0])
    o_ref[0] = _layer_norm(alpha * x1_ref[0] + g2_ref[0] * y, ln_g_ref[...], ln_b_ref[...])


def _final(alpha, x1, ya, yb, ew, g2, ln_g, ln_b):
    bsz, s, d = x1.shape
    tm = min(FINAL_TILE, s)
    row = lambda w: pl.BlockSpec((1, tm, w), lambda b, i: (b, i, 0))
    full = lambda a: pl.BlockSpec(a.shape, lambda b, i: (0,) * a.ndim)
    ln_g2, ln_b2 = ln_g.reshape(1, d), ln_b.reshape(1, d)
    return pl.pallas_call(
        functools.partial(_final_kernel, alpha),
        out_shape=jax.ShapeDtypeStruct((bsz, s, d), F32),
        grid=(bsz, s // tm),
        in_specs=[row(d), row(d // 2), row(d // 2), row(LANES), pl.BlockSpec((1, 1, d), lambda b, i: (b, 0, 0)),
                  full(ln_g2), full(ln_b2)],
        out_specs=row(d),
        compiler_params=_cparams(("parallel", "arbitrary")),
        name="combine_ln2",
    )(x1, ya, yb, ew, g2, ln_g2, ln_b2)


SC_CORES = 2
SC_SUBCORES = 16
SC_CHUNK = 64


def _sc_mesh():
    return plsc.VectorSubcoreMesh(core_axis_name="c", subcore_axis_name="s")


def _sc_scatter_rows(rows, dest0, dest1, n_rows):
    n, w = rows.shape
    n_workers = SC_CORES * SC_SUBCORES
    assert n % (n_workers * SC_CHUNK) == 0
    n_chunks = n // (n_workers * SC_CHUNK)
    d0 = dest0.reshape(n // SC_CHUNK, 1, SC_CHUNK)
    d1 = dest1.reshape(n // SC_CHUNK, 1, SC_CHUNK)

    @functools.partial(
        pl.kernel, mesh=_sc_mesh(), out_type=jax.ShapeDtypeStruct((n_rows, w), rows.dtype),
        scratch_types=[pltpu.VMEM((n_chunks, 1, SC_CHUNK), jnp.int32), pltpu.VMEM((n_chunks, 1, SC_CHUNK), jnp.int32),
                       pltpu.VMEM((2, SC_CHUNK, w), rows.dtype),
                       pltpu.SemaphoreType.DMA((2,)), pltpu.SemaphoreType.DMA((2, 2))])
    def scatter_kernel(rows_hbm, d0_hbm, d1_hbm, out_hbm, i0_v, i1_v, rows_v, read_sem, scat_sem):
        wid = lax.axis_index("s") * SC_CORES + lax.axis_index("c")
        first = wid * n_chunks
        pltpu.sync_copy(d0_hbm.at[pl.ds(first, n_chunks)], i0_v)
        pltpu.sync_copy(d1_hbm.at[pl.ds(first, n_chunks)], i1_v)

        def read(j):
            return pltpu.make_async_copy(rows_hbm.at[pl.ds((first + j) * SC_CHUNK, SC_CHUNK)], rows_v.at[j % 2],
                                         read_sem.at[j % 2])

        def scatters(j):
            return [pltpu.make_async_copy(rows_v.at[j % 2], out_hbm.at[idx.at[j].at[0]], scat_sem.at[j % 2, k])
                    for k, idx in enumerate((i0_v, i1_v))]

        read(0).start()
        for j in range(n_chunks):
            read(j).wait()
            if j + 1 < n_chunks:
                if j >= 1:
                    for cp in scatters(j - 1):
                        cp.wait()
                read(j + 1).start()
            for cp in scatters(j):
                cp.start()
        for j in range(max(n_chunks - 2, 0), n_chunks):
            for cp in scatters(j):
                cp.wait()

    return scatter_kernel(rows, d0, d1)


def _sc_gather_rows(table, dest0, dest1):
    n = dest0.shape[0]
    w = table.shape[1]
    n_workers = SC_CORES * SC_SUBCORES
    assert n % (n_workers * SC_CHUNK) == 0
    n_chunks = n // (n_workers * SC_CHUNK)
    d0 = dest0.reshape(n // SC_CHUNK, 1, SC_CHUNK)
    d1 = dest1.reshape(n // SC_CHUNK, 1, SC_CHUNK)
    out = jax.ShapeDtypeStruct((n, w), table.dtype)

    @functools.partial(
        pl.kernel, mesh=_sc_mesh(), out_type=(out, out),
        scratch_types=[pltpu.VMEM((n_chunks, 1, SC_CHUNK), jnp.int32), pltpu.VMEM((n_chunks, 1, SC_CHUNK), jnp.int32),
                       pltpu.VMEM((2, SC_CHUNK, w), table.dtype),
                       pltpu.SemaphoreType.DMA((2,)), pltpu.SemaphoreType.DMA((2,))])
    def gather_kernel(table_hbm, d0_hbm, d1_hbm, a_hbm, b_hbm, i0_v, i1_v, rows_v, gather_sem, write_sem):
        wid = lax.axis_index("s") * SC_CORES + lax.axis_index("c")
        first = wid * n_chunks
        pltpu.sync_copy(d0_hbm.at[pl.ds(first, n_chunks)], i0_v)
        pltpu.sync_copy(d1_hbm.at[pl.ds(first, n_chunks)], i1_v)
        n_items = 2 * n_chunks

        def gather(m):
            idx = (i0_v, i1_v)[m % 2]
            return pltpu.make_async_copy(table_hbm.at[idx.at[m // 2].at[0]], rows_v.at[m % 2], gather_sem.at[m % 2])

        def write(m):
            o_hbm = (a_hbm, b_hbm)[m % 2]
            return pltpu.make_async_copy(rows_v.at[m % 2], o_hbm.at[pl.ds((first + m // 2) * SC_CHUNK, SC_CHUNK)],
                                         write_sem.at[m % 2])

        gather(0).start()
        for m in range(n_items):
            gather(m).wait()
            if m + 1 < n_items:
                if m >= 1:
                    write(m - 1).wait()
                gather(m + 1).start()
            write(m).start()
        for m in range(max(n_items - 2, 0), n_items):
            write(m).wait()

    return gather_kernel(table, d0, d1)


def _dispatch_plan(route, counts):
    tm = EXPERT_TILE
    e0, e1, r0, r1 = (route[:, j, :].reshape(-1) for j in range(4))
    experts = jnp.arange(MOE_TOTAL, dtype=jnp.int32)
    tiles_per = (counts + tm - 1) // tm
    tile_end = jnp.cumsum(tiles_per)
    pad_start = ((tile_end - tiles_per) * tm).astype(jnp.int32)

    def lookup(e):
        return jnp.sum(jnp.where(e[None, :] == experts[:, None], pad_start[:, None], 0), axis=0)

    dest0, dest1 = lookup(e0) + r0, lookup(e1) + r1
    n_tiles = (2 * e0.size + MOE_TOTAL * tm) // tm
    tile_expert = jnp.minimum(jnp.sum(tile_end[None, :] <= jnp.arange(n_tiles)[:, None], axis=1), MOE_TOTAL - 1)
    nonempty = counts > 0
    run_of_expert = jnp.cumsum(nonempty.astype(jnp.int32)) - 1
    run_expert = jnp.sum(jnp.where(nonempty[None, :] & (run_of_expert[None, :] == experts[:, None]),
                                   experts[None, :], 0), axis=1).astype(jnp.int32)
    of_tile = tile_expert[:, None] == experts[None, :]
    tile_run = jnp.sum(jnp.where(of_tile, run_of_expert[None, :], 0), axis=1).astype(jnp.int32)
    rows_left = (counts + pad_start)[None, :] - jnp.arange(n_tiles)[:, None] * tm
    tile_valid = jnp.clip(jnp.sum(jnp.where(of_tile, rows_left, 0), axis=1), 0, tm).astype(jnp.int32)
    used = jnp.stack([tile_end[-1], jnp.sum(nonempty)]).astype(jnp.int32)
    return dest0, dest1, tile_run, tile_valid, run_expert, used, n_tiles * tm


def _layer(x, c, rel_bias, w_ada, b_ada, w_in, w_gla_gate, b_gla_gate, gla_norm, w_proj_gla, w_proj_attn, w_out,
           ln1_g, ln1_b, w_rg, b_rg, w_re, b_re, w_eg, w_eu, w_ed, ln2_g, ln2_b):
    bsz, s, d = x.shape
    alpha = (2.0 * DEPTH) ** 0.25
    mods = _ada_mods(c, w_ada, b_ada)
    sh1, sc1, g1, sh2, sc2, g2 = [m.reshape(bsz, 1, d) for m in jnp.split(mods, N_MOD, axis=-1)]

    lr0 = d // 2 * 2 + 2 * d
    z = _in_projection(x, sc1, sh1, w_in, lr0, w_gla_gate, b_gla_gate)

    y_gla = _gla(z["q_in"], z["k_in"], z["q_st"], z["k_st"], z["dec"], z["v_gla"], z["r_gla"], gla_norm)

    o_groups, lse_groups = [], []
    for g, (window, dilation) in enumerate(DIL_PATTERNS):
        l = s // dilation
        qg, kg, vg = (z[f"{n}{g}"].reshape(bsz * dilation, l, DIL_GROUP_WIDTH) for n in ("q_att", "k_att", "v_att"))
        table = rel_bias[:, g * DIL_HEADS_PER_GROUP:(g + 1) * DIL_HEADS_PER_GROUP]
        o, lse = _dilated_group_attention(qg, kg, vg, table, window, dilation)
        o_groups.append(o.reshape(bsz, dilation, l, DIL_GROUP_WIDTH))
        lse_groups.append(lse.reshape(bsz, dilation, l, DIL_GROUP_WIDTH))

    wr = jnp.concatenate([w_rg, w_re, jnp.zeros((d, LANES - MOE_GROUPS - MOE_TOTAL), F32)], axis=1)
    br = jnp.concatenate([b_rg, b_re, jnp.zeros((LANES - MOE_GROUPS - MOE_TOTAL,), F32)]).reshape(1, LANES)
    x1, u2, route, ew, cnt = _merge(alpha, y_gla, o_groups, lse_groups, z["g_gla"], z["g_att"], x, g1, sc2, sh2,
                                    ln1_g, ln1_b, w_proj_gla.astype(BF16), w_proj_attn.astype(BF16),
                                    w_out.astype(BF16), wr, br)

    n = bsz * s
    counts = cnt[MOE_GROUPS:MOE_GROUPS + MOE_TOTAL, 0]
    dest0, dest1, tile_run, tile_valid, run_expert, used, n_rows = _dispatch_plan(route, counts)
    xg = _sc_scatter_rows(u2.reshape(n, d // 2), dest0, dest1, n_rows)
    ff = w_eg.shape[-1]
    yo = _expert_ffn(tile_run, tile_valid, run_expert, used, xg, w_eg.reshape(MOE_TOTAL, d, ff),
                     w_eu.reshape(MOE_TOTAL, d, ff), w_ed.reshape(MOE_TOTAL, ff, d))
    ya, yb = (y.reshape(bsz, s, d // 2) for y in _sc_gather_rows(yo, dest0, dest1))
    return _final(alpha, x1, ya, yb, ew, g2, ln2_g, ln2_b)


def kernel(x, c, rel_bias, w_ada, b_ada, w_in, w_gla_gate, b_gla_gate, gla_norm, w_proj_gla, w_proj_attn, w_out,
           ln1_g, ln1_b, w_router_group, b_router_group, w_router_expert, b_router_expert, w_exp_gate, w_exp_up,
           w_exp_down, ln2_g, ln2_b):
    assert w_ada.shape[0] == DEPTH
    return _layer(x, c, rel_bias, w_ada[0], b_ada[0], w_in[0:1], w_gla_gate[0], b_gla_gate[0], gla_norm[0],
                  w_proj_gla[0], w_proj_attn[0], w_out[0], ln1_g[0], ln1_b[0], w_router_group[0],
                  b_router_group[0], w_router_expert[0], b_router_expert[0], w_exp_gate[0], w_exp_up[0],
                  w_exp_down[0], ln2_g[0], ln2_b[0])
```
